```python
import jax, jax.numpy as jnp
from jax import lax
import numpy as np

D_MODEL = 1024
BATCH = 1
SEQ = 16384
DEPTH = 2
DEC_BATCH = 32
DEC_SEQ = 32
PAST_LEN = 2048

CHUNK = 64
N_A = DEPTH // 2
N_B = DEPTH - N_A
HEAD_DIM = 64
GM_CHUNK = 128
GM_GROUPS = 4
GM_W = 3 * D_MODEL // 4
GM_GW = GM_W // GM_GROUPS
MEM_LEN = 256
MEM_HEADS = 4
MEM_W = MEM_HEADS * HEAD_DIM
B_HEADS = GM_W // HEAD_DIM
B_W = B_HEADS * HEAD_DIM
LOOKBACK_CHUNKS = 8
BAND_PAST = LOOKBACK_CHUNKS * CHUNK
BAND = BAND_PAST + CHUNK
REL_CLIP = 128
D_FF = 4 * D_MODEL
EPS = 1e-6

kernel_name = "yoco_gmlp_chunkband_stream_step"


def _rms_norm(x, g):
    xf = x.astype(jnp.float32)
    y = xf * lax.rsqrt(jnp.mean(xf * xf, axis=-1, keepdims=True) + EPS)
    return (y * g.astype(jnp.float32)).astype(x.dtype)


def _group_layer_norm(x, g, b):
    shp = x.shape
    xf = x.astype(jnp.float32).reshape(shp[:-1] + (GM_GROUPS, GM_GW))
    mu = jnp.mean(xf, axis=-1, keepdims=True)
    var = jnp.mean(jnp.square(xf - mu), axis=-1, keepdims=True)
    y = ((xf - mu) * lax.rsqrt(var + EPS)).reshape(shp)
    return (y * g.astype(jnp.float32) + b.astype(jnp.float32)).astype(x.dtype)


def _ffn(x, g_pre, g_post, w1, w2):
    h = jnp.square(jax.nn.relu(_rms_norm(x, g_pre) @ w1)) @ w2
    return x + _rms_norm(h, g_post)


def _mem_kv(mem, g, w):
    b, m, _ = mem.shape
    h = _rms_norm(mem, g) @ w
    k = h[..., :MEM_W].reshape(b, m, MEM_HEADS, HEAD_DIM)
    v = h[..., MEM_W:].reshape(b, m, MEM_HEADS, HEAD_DIM)
    return k, v


def _shared_kv(x, g, w):
    b, s, _ = x.shape
    h = _rms_norm(x, g) @ w
    k = h[..., :B_W].reshape(b, s, B_HEADS, HEAD_DIM)
    v = h[..., B_W:].reshape(b, s, B_HEADS, HEAD_DIM)
    return k, v


def _softmax_attend(q, k, v, bias=None, valid=None):
    s = jnp.einsum('bqhd,bkhd->bhqk', q, k).astype(jnp.float32) * (HEAD_DIM ** -0.5)
    if bias is not None:
        s = s + bias.astype(jnp.float32)
    if valid is not None:
        s = jnp.where(valid, s, -jnp.inf)
    p = jax.nn.softmax(s, axis=-1).astype(v.dtype)
    return jnp.einsum('bhqk,bkhd->bqhd', p, v)


def _rel_bias(rel_bias, n_q, n_k, n_past):
    rel = n_past + jnp.arange(n_q)[:, None] - jnp.arange(n_k)[None, :]
    idx = jnp.clip(rel, -REL_CLIP, REL_CLIP) + REL_CLIP
    return rel_bias[:, idx]


def _band_prompt(q, k, v, rel_bias):
    b, s, h, d = q.shape
    n = s // CHUNK
    pad = ((0, 0), (BAND_PAST, 0), (0, 0), (0, 0))
    kp, vp = jnp.pad(k, pad), jnp.pad(v, pad)
    bias = _rel_bias(rel_bias, CHUNK, BAND, BAND_PAST)
    j = jnp.arange(BAND)

    def one_chunk(c):
        start = c * CHUNK
        qc = lax.dynamic_slice_in_dim(q, start, CHUNK, axis=1)
        kc = lax.dynamic_slice_in_dim(kp, start, BAND, axis=1)
        vc = lax.dynamic_slice_in_dim(vp, start, BAND, axis=1)
        valid = (start - BAND_PAST + j >= 0)[None, :]
        return _softmax_attend(qc, kc, vc, bias, valid)

    out = lax.map(one_chunk, jnp.arange(n))
    return jnp.moveaxis(out, 0, 1).reshape(b, s, h * d)


def _band_sample(q, k_cache, v_cache, k_new, v_new, rel_bias):
    b, t, h, d = q.shape
    p = k_cache.shape[1]
    k = jnp.concatenate([k_cache, k_new], axis=1)
    v = jnp.concatenate([v_cache, v_new], axis=1)
    bias = _rel_bias(rel_bias, t, p + t, p)
    return _softmax_attend(q, k, v, bias).reshape(b, t, h * d)


def _causal_spatial(w_s):
    mask = jnp.tril(jnp.ones((GM_CHUNK, GM_CHUNK), dtype=bool))
    return jnp.where(mask, w_s, jnp.zeros((), w_s.dtype))


def _gmlp_prompt(u, v, w_s, b_s):
    b, s, _ = v.shape
    n = s // GM_CHUNK
    vg = v.reshape(b, n, GM_CHUNK, GM_GROUPS, GM_GW)
    mixed = jnp.einsum('gts,bnsgc->bntgc', _causal_spatial(w_s), vg) + b_s.T[:, :, None]
    return u * mixed.reshape(b, s, GM_W)


def _gmlp_sample(u, v, w_s, b_s):
    b, t, _ = v.shape
    vg = v.reshape(b, t, GM_GROUPS, GM_GW)
    mixed = jnp.einsum('gts,bsgc->btgc', _causal_spatial(w_s)[:, :t, :t], vg) + b_s[:, :t].T[:, :, None]
    return u * mixed.reshape(b, t, GM_W)


def _mixer_a(x, mem_k, mem_v, spatial, g_pre, g_post, w_in, g_ln, b_ln, w_s, b_s, w_out):
    b, s, _ = x.shape
    z = _rms_norm(x, g_pre) @ w_in
    zg = jax.nn.gelu(z[..., :2 * GM_W])
    u = zg[..., :GM_W]
    v = _group_layer_norm(zg[..., GM_W:], g_ln, b_ln)
    gm = spatial(u, v, w_s, b_s)
    qm = z[..., 2 * GM_W:].reshape(b, s, MEM_HEADS, HEAD_DIM)
    mo = _softmax_attend(qm, mem_k, mem_v).reshape(b, s, MEM_W)
    y = jnp.concatenate([gm, mo], axis=-1) @ w_out
    return x + _rms_norm(y, g_post), v


def _mixer_b(x, mem_k, mem_v, band, g_pre, g_post, w_in, rel_bias, w_out):
    b, s, _ = x.shape
    z = _rms_norm(x, g_pre) @ w_in
    q = z[..., :B_W].reshape(b, s, B_HEADS, HEAD_DIM)
    qm = z[..., B_W:].reshape(b, s, MEM_HEADS, HEAD_DIM)
    mo = _softmax_attend(qm, mem_k, mem_v).reshape(b, s, MEM_W)
    y = jnp.concatenate([band(q, rel_bias), mo], axis=-1) @ w_out
    return x + _rms_norm(y, g_post)


def setup_inputs(seed: int = 0) -> dict:
    key = jax.random.key(seed)
    ks = jax.random.split(key, 26)
    f32 = jnp.float32

    def nrm(k, shape, scale=1.0):
        return jax.random.normal(k, shape, f32) * scale

    def gain(k, shape):
        return 1.0 + 0.05 * jax.random.normal(k, shape, f32)

    n_keep = min(BAND_PAST, PAST_LEN)
    return {
        "x_prompt": nrm(ks[0], (BATCH, SEQ, D_MODEL)),
        "x_sample": nrm(ks[1], (DEC_BATCH, DEC_SEQ, D_MODEL)),
        "cache_mem_k": nrm(ks[2], (DEPTH, DEC_BATCH, MEM_LEN, MEM_HEADS, HEAD_DIM)),
        "cache_mem_v": nrm(ks[3], (DEPTH, DEC_BATCH, MEM_LEN, MEM_HEADS, HEAD_DIM)),
        "cache_band_k": nrm(ks[4], (DEC_BATCH, n_keep, B_HEADS, HEAD_DIM)),
        "cache_band_v": nrm(ks[5], (DEC_BATCH, n_keep, B_HEADS, HEAD_DIM)),
        "mem_prompt": nrm(ks[6], (BATCH, MEM_LEN, D_MODEL)),
        "g_mix_pre": gain(ks[7], (DEPTH, D_MODEL)),
        "g_mix_post": gain(ks[8], (DEPTH, D_MODEL)),
        "g_ffn_pre": gain(ks[9], (DEPTH, D_MODEL)),
        "g_ffn_post": gain(ks[10], (DEPTH, D_MODEL)),
        "g_mem": gain(ks[11], (DEPTH, D_MODEL)),
        "w_mem_kv": nrm(ks[12], (DEPTH, D_MODEL, 2 * MEM_W), D_MODEL ** -0.5),
        "w_in_a": nrm(ks[13], (N_A, D_MODEL, 2 * GM_W + MEM_W), D_MODEL ** -0.5),
        "g_gm_ln": gain(ks[14], (N_A, GM_W)),
        "b_gm_ln": nrm(ks[15], (N_A, GM_W), 0.02),
        "w_spatial": nrm(ks[16], (N_A, GM_GROUPS, GM_CHUNK, GM_CHUNK), GM_CHUNK ** -0.5),
        "b_spatial": gain(ks[17], (N_A, GM_GROUPS, GM_CHUNK)),
        "w_out_a": nrm(ks[18], (N_A, GM_W + MEM_W, D_MODEL), (GM_W + MEM_W) ** -0.5),
        "g_kv": gain(ks[19], (D_MODEL,)),
        "w_kv": nrm(ks[20], (D_MODEL, 2 * B_W), D_MODEL ** -0.5),
        "w_in_b": nrm(ks[21], (N_B, D_MODEL, B_W + MEM_W), D_MODEL ** -0.5),
        "rel_bias": nrm(ks[22], (N_B, B_HEADS, 2 * REL_CLIP + 1), 0.5),
        "w_out_b": nrm(ks[23], (N_B, B_W + MEM_W, D_MODEL), (B_W + MEM_W) ** -0.5),
        "w_ff1": nrm(ks[24], (DEPTH, D_MODEL, D_FF), D_MODEL ** -0.5),
        "w_ff2": nrm(ks[25], (DEPTH, D_FF, D_MODEL), D_FF ** -0.5),
    }


def reference(x_prompt, x_sample, cache_mem_k, cache_mem_v, cache_band_k, cache_band_v, mem_prompt,
              g_mix_pre, g_mix_post, g_ffn_pre, g_ffn_post, g_mem, w_mem_kv,
              w_in_a, g_gm_ln, b_gm_ln, w_spatial, b_spatial, w_out_a,
              g_kv, w_kv, w_in_b, rel_bias, w_out_b, w_ff1, w_ff2):
    x = x_prompt
    mem_k_p, mem_v_p = [], []
    for l in range(DEPTH):
        mk, mv = _mem_kv(mem_prompt, g_mem[l], w_mem_kv[l])
        mem_k_p.append(mk)
        mem_v_p.append(mv)
        if l < N_A:
            x, _ = _mixer_a(x, mk, mv, _gmlp_prompt, g_mix_pre[l], g_mix_post[l], w_in_a[l],
                            g_gm_ln[l], b_gm_ln[l], w_spatial[l], b_spatial[l], w_out_a[l])
        else:
            if l == N_A:
                k_sh, v_sh = _shared_kv(x, g_kv, w_kv)
                band_p = lambda q, rb: _band_prompt(q, k_sh, v_sh, rb)
            i = l - N_A
            x = _mixer_b(x, mk, mv, band_p, g_mix_pre[l], g_mix_post[l], w_in_b[i], rel_bias[i], w_out_b[i])
        x = _ffn(x, g_ffn_pre[l], g_ffn_post[l], w_ff1[l], w_ff2[l])
    y_prompt = x
    n_keep = min(BAND_PAST, x_prompt.shape[1])
    band_k_prompt = k_sh[:, -n_keep:]
    band_v_prompt = v_sh[:, -n_keep:]

    x = x_sample
    gm_v = []
    for l in range(DEPTH):
        mk, mv = cache_mem_k[l], cache_mem_v[l]
        if l < N_A:
            x, v_rows = _mixer_a(x, mk, mv, _gmlp_sample, g_mix_pre[l], g_mix_post[l], w_in_a[l],
                                 g_gm_ln[l], b_gm_ln[l], w_spatial[l], b_spatial[l], w_out_a[l])
            gm_v.append(v_rows)
        else:
            if l == N_A:
                k_new, v_new = _shared_kv(x, g_kv, w_kv)
                band_s = lambda q, rb: _band_sample(q, cache_band_k, cache_band_v, k_new, v_new, rb)
            i = l - N_A
            x = _mixer_b(x, mk, mv, band_s, g_mix_pre[l], g_mix_post[l], w_in_b[i], rel_bias[i], w_out_b[i])
        x = _ffn(x, g_ffn_pre[l], g_ffn_post[l], w_ff1[l], w_ff2[l])
    y_sample = x

    return (y_prompt, y_sample, jnp.stack(mem_k_p), jnp.stack(mem_v_p), band_k_prompt, band_v_prompt,
            jnp.stack(gm_v), k_new, v_new)
```

```python
import functools

import jax
import jax.numpy as jnp
from jax import lax
from jax.experimental import pallas as pl
from jax.experimental.pallas import tpu as pltpu

D_MODEL = 1024
DEPTH = 2
CHUNK = 64
HEAD_DIM = 64
GM_CHUNK = 128
GM_GROUPS = 4
GM_W = 768
GM_GW = GM_W // GM_GROUPS
MEM_LEN = 256
MEM_HEADS = 4
MEM_W = MEM_HEADS * HEAD_DIM
B_HEADS = 12
B_W = B_HEADS * HEAD_DIM
BAND_PAST = 512
REL_CLIP = 128
D_FF = 4 * D_MODEL
EPS = 1e-6

LANES = 128
HEAD_PAIRS = B_W // LANES
Q_SCALE = HEAD_DIM ** -0.5
NEG = -1e30
TILE = 256
VMEM_LIMIT = 56 * 1024 * 1024

BF16 = jnp.bfloat16
F32 = jnp.float32


def _dot(a, b):
    return jnp.dot(a, b, preferred_element_type=F32)


def _dot_nt(a, b):
    return lax.dot_general(a, b, (((1,), (1,)), ((), ())), preferred_element_type=F32)


def _rms(x, g):
    ms = jnp.mean(x * x, axis=-1, keepdims=True)
    return x * lax.rsqrt(ms + EPS) * g


def _lsum(a):
    return jnp.sum(a, axis=-1, keepdims=True)


def _lmax(a):
    return jnp.max(a, axis=-1, keepdims=True)


def _const_spec(shape):
    nd = len(shape)
    return pl.BlockSpec(shape, lambda *_: (0,) * nd, pipeline_mode=pl.Buffered(1))


def _call(body, name, **kw):
    return pl.pallas_call(
        body, name=name,
        compiler_params=pltpu.CompilerParams(dimension_semantics=("parallel",),
                                             vmem_limit_bytes=VMEM_LIMIT), **kw)


def _memkv_kernel(mem_ref, g_ref, w_ref, k_ref, v_ref):
    h = _dot(_rms(mem_ref[...], g_ref[...]).astype(BF16), w_ref[...])
    k_ref[...] = h[:, :MEM_W]
    v_ref[...] = h[:, MEM_W:]


def _memkv(mem, g_mem, w_mem_kv):
    out = jax.ShapeDtypeStruct((DEPTH, MEM_LEN, MEM_W), F32)
    return _call(
        _memkv_kernel, "mem_kv",
        grid=(DEPTH,),
        in_specs=[
            pl.BlockSpec((MEM_LEN, D_MODEL), lambda l: (0, 0)),
            pl.BlockSpec((None, 1, D_MODEL), lambda l: (l, 0, 0)),
            pl.BlockSpec((None, D_MODEL, 2 * MEM_W), lambda l: (l, 0, 0)),
        ],
        out_specs=[pl.BlockSpec((None, MEM_LEN, MEM_W), lambda l: (l, 0, 0))] * 2,
        out_shape=[out, out],
    )(mem, g_mem.reshape(DEPTH, 1, D_MODEL), w_mem_kv.astype(BF16))


def _mem_attend(qb, k, v):
    r = qb.shape[0]
    lane = lax.broadcasted_iota(jnp.int32, (1, MEM_W), 1)
    masks = [(lane >= h * HEAD_DIM) & (lane < (h + 1) * HEAD_DIM) for h in range(MEM_HEADS)]
    qs = jnp.concatenate([jnp.where(m, qb, jnp.zeros_like(qb)) for m in masks], axis=0)
    s = _dot_nt(qs, k)
    e = jnp.exp(s - _lmax(s))
    pv = _dot(e.astype(BF16), v) * (1.0 / _lsum(e))
    out = jnp.where(masks[0], pv[:r], 0.0)
    for h in range(1, MEM_HEADS):
        out = out + jnp.where(masks[h], pv[h * r:(h + 1) * r], 0.0)
    return out


def _gelu(x):
    return jax.nn.gelu(x, approximate=True)


def _mixer_a_kernel(x_ref, gpre_ref, gpost_ref, win_ref, gln_ref, bln_ref, s_ref, bs_ref,
                    mk_ref, mv_ref, wout_ref, o_ref, *v_out, rows_per_mem):
    x = x_ref[...]
    z = _dot(_rms(x, gpre_ref[...]).astype(BF16), win_ref[...])
    nt = GM_W // LANES
    u = [_gelu(z[:, j * LANES:(j + 1) * LANES]) for j in range(nt)]
    t = [_gelu(z[:, GM_W + j * LANES:GM_W + (j + 1) * LANES]) for j in range(nt)]

    lane = lax.broadcasted_iota(jnp.int32, (1, LANES), 1)
    lo = lane < (GM_GW - LANES)
    inv = 1.0 / GM_GW

    def group_stat(a):
        s0 = _lsum(a[0] + jnp.where(lo, a[1], 0.0)) * inv
        s1 = _lsum(jnp.where(lo, 0.0, a[1]) + a[2]) * inv
        s2 = _lsum(a[3] + jnp.where(lo, a[4], 0.0)) * inv
        s3 = _lsum(jnp.where(lo, 0.0, a[4]) + a[5]) * inv
        return [s0, jnp.where(lo, s0, s1), s1, s2, jnp.where(lo, s2, s3), s3]

    mu = group_stat(t)
    c = [t[j] - mu[j] for j in range(nt)]
    var = group_stat([cj * cj for cj in c])
    gln = gln_ref[...]
    bln = bln_ref[...]
    vn = [c[j] * lax.rsqrt(var[j] + EPS) * gln[:, j * LANES:(j + 1) * LANES]
          + bln[:, j * LANES:(j + 1) * LANES] for j in range(nt)]
    if v_out:
        v_out[0][...] = jnp.concatenate(vn, axis=-1)

    vb = [a.astype(BF16) for a in vn]
    win = [(0, 1), (1, 2), (3, 4), (4, 5)]
    m = [_dot(s_ref[g], jnp.concatenate([vb[a], vb[b]], axis=-1)) for g, (a, b) in enumerate(win)]
    mixed = [m[0][:, :LANES], jnp.where(lo, m[0][:, LANES:], m[1][:, :LANES]), m[1][:, LANES:],
             m[2][:, :LANES], jnp.where(lo, m[2][:, LANES:], m[3][:, :LANES]), m[3][:, LANES:]]
    bs = bs_ref[...]
    gm = [u[j] * (mixed[j] + bs[:, j * LANES:(j + 1) * LANES]) for j in range(nt)]

    qm = (z[:, 2 * GM_W:] * Q_SCALE).astype(BF16)
    mo = []
    for b in range(TILE // rows_per_mem):
        rows = slice(b * rows_per_mem, (b + 1) * rows_per_mem)
        mo.append(_mem_attend(qm[rows], mk_ref[b].astype(BF16), mv_ref[b].astype(BF16)))
    mo = mo[0] if len(mo) == 1 else jnp.concatenate(mo, axis=0)

    cat = jnp.concatenate([a.astype(BF16) for a in gm] + [mo.astype(BF16)], axis=-1)
    o_ref[...] = x + _rms(_dot(cat, wout_ref[...]), gpost_ref[...])


def _mixer_a(x, gpre, gpost, win, gln, bln, s_mat, bs, mk, mv, wout, *, rows_per_mem, emit_v):
    n = x.shape[0]
    nm = TILE // rows_per_mem
    shared_mem = mk.shape[0] == 1
    mem_spec = (_const_spec((1, MEM_LEN, MEM_W)) if shared_mem
                else pl.BlockSpec((nm, MEM_LEN, MEM_W), lambda i: (i, 0, 0)))
    row = lambda w: pl.BlockSpec((TILE, w), lambda i: (i, 0))
    out_shape = [jax.ShapeDtypeStruct((n, D_MODEL), F32)]
    out_specs = [row(D_MODEL)]
    if emit_v:
        out_shape.append(jax.ShapeDtypeStruct((n, GM_W), F32))
        out_specs.append(row(GM_W))
    return _call(
        functools.partial(_mixer_a_kernel, rows_per_mem=rows_per_mem), "mixer_a",
        grid=(n // TILE,),
        in_specs=[
            row(D_MODEL),
            _const_spec((1, D_MODEL)), _const_spec((1, D_MODEL)),
            _const_spec((D_MODEL, 2 * GM_W + MEM_W)),
            _const_spec((1, GM_W)), _const_spec((1, GM_W)),
            _const_spec((GM_GROUPS, TILE, TILE)), _const_spec((TILE, GM_W)),
            mem_spec, mem_spec,
            _const_spec((GM_W + MEM_W, D_MODEL)),
        ],
        out_specs=out_specs,
        out_shape=out_shape,
    )(x, gpre, gpost, win, gln, bln, s_mat, bs, mk, mv, wout)


def _ffn_kernel(x_ref, gpre_ref, gpost_ref, w1_ref, w2_ref, o_ref, *, ff_chunk):
    x = x_ref[...]
    xn = _rms(x, gpre_ref[...]).astype(BF16)
    acc = jnp.zeros(x.shape, F32)
    for c in range(D_FF // ff_chunk):
        h = _dot(xn, w1_ref[:, c * ff_chunk:(c + 1) * ff_chunk])
        h = jnp.square(jnp.maximum(h, 0.0)).astype(BF16)
        acc = acc + _dot(h, w2_ref[c * ff_chunk:(c + 1) * ff_chunk, :])
    o_ref[...] = x + _rms(acc, gpost_ref[...])


def _ffn(x, gpre, gpost, w1, w2, *, rows=512, ff_chunk=1024):
    n = x.shape[0]
    row = pl.BlockSpec((rows, D_MODEL), lambda i: (i, 0))
    return _call(
        functools.partial(_ffn_kernel, ff_chunk=ff_chunk), "ffn",
        grid=(n // rows,),
        in_specs=[row, _const_spec((1, D_MODEL)), _const_spec((1, D_MODEL)),
                  _const_spec((D_MODEL, D_FF)), _const_spec((D_FF, D_MODEL))],
        out_specs=row,
        out_shape=jax.ShapeDtypeStruct((n, D_MODEL), F32),
    )(x, gpre, gpost, w1, w2)


def _proj_b_kernel(x_ref, gkv_ref, gpre_ref, wkt_ref, wk_ref, wv_ref, win_ref, *outs,
                   want_q, want_kv_bf16, want_kv_f32):
    outs = list(outs)
    x = x_ref[...]
    xh = x * lax.rsqrt(jnp.mean(x * x, axis=-1, keepdims=True) + EPS)
    xkv = (xh * gkv_ref[...]).astype(BF16)
    if want_q:
        q_ref, qm_ref = outs.pop(0), outs.pop(0)
        z = _dot((xh * gpre_ref[...]).astype(BF16), win_ref[...]) * Q_SCALE
        for p in range(HEAD_PAIRS):
            q_ref[p] = z[:, p * LANES:(p + 1) * LANES].astype(BF16)
        qm_ref[...] = z[:, B_W:].astype(BF16)
    if want_kv_bf16:
        kt_ref, v_ref = outs.pop(0), outs.pop(0)
        kt = _dot_nt(wkt_ref[...], xkv)
        v = _dot(xkv, wv_ref[...])
        for p in range(HEAD_PAIRS):
            kt_ref[p] = kt[p * LANES:(p + 1) * LANES, :].astype(BF16)
            v_ref[p] = v[:, p * LANES:(p + 1) * LANES].astype(BF16)
    if want_kv_f32:
        k32_ref, v32_ref = outs.pop(0), outs.pop(0)
        k32_ref[...] = _dot(xkv, wk_ref[...])
        v32_ref[...] = _dot(xkv, wv_ref[...])


def _proj_b(x, gkv, gpre, wkt, wk, wv, win, *, rows, row_offset=0, n_rows=None,
            want_q, want_kv_bf16, want_kv_f32):
    n = x.shape[0] if n_rows is None else n_rows
    off = row_offset // rows
    out_shape, out_specs = [], []
    pair_rows = jax.ShapeDtypeStruct((HEAD_PAIRS, n, LANES), BF16)
    pair_rows_spec = pl.BlockSpec((HEAD_PAIRS, rows, LANES), lambda i: (0, i, 0))
    if want_q:
        out_shape += [pair_rows, jax.ShapeDtypeStruct((n, MEM_W), BF16)]
        out_specs += [pair_rows_spec, pl.BlockSpec((rows, MEM_W), lambda i: (i, 0))]
    if want_kv_bf16:
        out_shape += [jax.ShapeDtypeStruct((HEAD_PAIRS, LANES, n), BF16), pair_rows]
        out_specs += [pl.BlockSpec((HEAD_PAIRS, LANES, rows), lambda i: (0, 0, i)), pair_rows_spec]
    if want_kv_f32:
        out_shape += [jax.ShapeDtypeStruct((n, B_W), F32)] * 2
        out_specs += [pl.BlockSpec((rows, B_W), lambda i: (i, 0))] * 2
    return _call(
        functools.partial(_proj_b_kernel, want_q=want_q, want_kv_bf16=want_kv_bf16,
                          want_kv_f32=want_kv_f32), "proj_b",
        grid=(n // rows,),
        in_specs=[pl.BlockSpec((rows, D_MODEL), lambda i: (i + off, 0)),
                  _const_spec((1, D_MODEL)), _const_spec((1, D_MODEL)),
                  _const_spec((B_W, D_MODEL)), _const_spec((D_MODEL, B_W)),
                  _const_spec((D_MODEL, B_W)), _const_spec((D_MODEL, B_W + MEM_W))],
        out_specs=out_specs,
        out_shape=out_shape,
    )(x, gkv, gpre, wkt, wk, wv, win)


def _band_prompt_kernel(x_ref, q_ref, qm_ref, kt0_ref, kt1_ref, kt2_ref, v0_ref, v1_ref, v2_ref,
                        bias_ref, mk_ref, mv_ref, wout_ref, gpost_ref, o_ref):
    i = pl.program_id(0)
    kts = (kt0_ref, kt1_ref, kt2_ref)
    vs = (v0_ref, v1_ref, v2_ref)
    nblk = len(kts)
    lane = lax.broadcasted_iota(jnp.int32, (1, LANES), 1)
    lo = lane < HEAD_DIM
    band = []
    for p in range(HEAD_PAIRS):
        qp = q_ref[p]
        zero = jnp.zeros_like(qp)
        heads = []
        for hh in range(2):
            h = 2 * p + hh
            qh = jnp.where(lo, qp, zero) if hh == 0 else jnp.where(lo, zero, qp)
            s = [_dot(qh, kts[j][p]) + bias_ref[h, :, j * TILE:(j + 1) * TILE] for j in range(nblk)]
            s = [jnp.where(i + j >= nblk - 1, s[j], NEG) if j < nblk - 1 else s[j] for j in range(nblk)]
            m = functools.reduce(jnp.maximum, [_lmax(a) for a in s])
            e = [jnp.exp(a - m) for a in s]
            l = functools.reduce(lambda a, b: a + b, [_lsum(a) for a in e])
            o = functools.reduce(lambda a, b: a + b,
                                 [_dot(e[j].astype(BF16), vs[j][p]) for j in range(nblk)])
            heads.append(o * (1.0 / l))
        band.append(jnp.where(lo, heads[0], heads[1]).astype(BF16))
    mo = _mem_attend(qm_ref[...], mk_ref[0].astype(BF16), mv_ref[0].astype(BF16))
    cat = jnp.concatenate(band + [mo.astype(BF16)], axis=-1)
    o_ref[...] = x_ref[...] + _rms(_dot(cat, wout_ref[...]), gpost_ref[...])


def _band_prompt(x, q3, qm, kt3, v3, bias, mk, mv, wout, gpost):
    n = x.shape[0]
    nblk = BAND_PAST // TILE + 1
    kt_specs = [pl.BlockSpec((HEAD_PAIRS, LANES, TILE),
                             functools.partial(lambda i, j: (0, 0, jnp.maximum(i - (nblk - 1) + j, 0)), j=j))
                for j in range(nblk)]
    v_specs = [pl.BlockSpec((HEAD_PAIRS, TILE, LANES),
                            functools.partial(lambda i, j: (0, jnp.maximum(i - (nblk - 1) + j, 0), 0), j=j))
               for j in range(nblk)]
    return _call(
        _band_prompt_kernel, "band_prompt",
        grid=(n // TILE,),
        in_specs=[pl.BlockSpec((TILE, D_MODEL), lambda i: (i, 0)),
                  pl.BlockSpec((HEAD_PAIRS, TILE, LANES), lambda i: (0, i, 0)),
                  pl.BlockSpec((TILE, MEM_W), lambda i: (i, 0)),
                  *kt_specs, *v_specs,
                  _const_spec((B_HEADS, TILE, nblk * TILE)),
                  _const_spec((1, MEM_LEN, MEM_W)), _const_spec((1, MEM_LEN, MEM_W)),
                  _const_spec((B_W + MEM_W, D_MODEL)), _const_spec((1, D_MODEL))],
        out_specs=pl.BlockSpec((TILE, D_MODEL), lambda i: (i, 0)),
        out_shape=jax.ShapeDtypeStruct((n, D_MODEL), F32),
    )(x, q3, qm, kt3, kt3, kt3, v3, v3, v3, bias, mk, mv, wout, gpost)


def _band_sample_kernel(x_ref, q_ref, qm_ref, kn_ref, vn_ref, ck_ref, cv_ref, bc_ref, bn_ref,
                        mk_ref, mv_ref, wout_ref, gpost_ref, o_ref, *, seqs, seq_len):
    lane = lax.broadcasted_iota(jnp.int32, (1, LANES), 1)
    lo = lane < HEAD_DIM
    rows_out = []
    for b in range(seqs):
        rows = slice(b * seq_len, (b + 1) * seq_len)
        band = []
        for p in range(HEAD_PAIRS):
            cols = slice(p * LANES, (p + 1) * LANES)
            qp = q_ref[p, rows, :]
            zero = jnp.zeros_like(qp)
            qs = jnp.concatenate([jnp.where(lo, qp, zero), jnp.where(lo, zero, qp)], axis=0)
            sc = _dot_nt(qs, ck_ref[b, :, cols].astype(BF16)) + bc_ref[p]
            sn = _dot_nt(qs, kn_ref[rows, cols].astype(BF16)) + bn_ref[p]
            m = jnp.maximum(_lmax(sc), _lmax(sn))
            ec = jnp.exp(sc - m)
            en = jnp.exp(sn - m)
            o = (_dot(ec.astype(BF16), cv_ref[b, :, cols].astype(BF16))
                 + _dot(en.astype(BF16), vn_ref[rows, cols].astype(BF16)))
            o = o * (1.0 / (_lsum(ec) + _lsum(en)))
            band.append(jnp.where(lo, o[:seq_len], o[seq_len:]).astype(BF16))
        mo = _mem_attend(qm_ref[rows, :], mk_ref[b].astype(BF16), mv_ref[b].astype(BF16))
        rows_out.append(jnp.concatenate(band + [mo.astype(BF16)], axis=-1))
    cat = jnp.concatenate(rows_out, axis=0)
    o_ref[...] = x_ref[...] + _rms(_dot(cat, wout_ref[...]), gpost_ref[...])


def _band_sample(x, q3, qm, kn, vn, ck, cv, bias_c, bias_n, mk, mv, wout, gpost, *, seqs=4):
    n = x.shape[0]
    n_seq, past = ck.shape[0], ck.shape[1]
    seq_len = n // n_seq
    rows = seqs * seq_len
    return _call(
        functools.partial(_band_sample_kernel, seqs=seqs, seq_len=seq_len), "band_sample",
        grid=(n_seq // seqs,),
        in_specs=[pl.BlockSpec((rows, D_MODEL), lambda i: (i, 0)),
                  pl.BlockSpec((HEAD_PAIRS, rows, LANES), lambda i: (0, i, 0)),
                  pl.BlockSpec((rows, MEM_W), lambda i: (i, 0)),
                  pl.BlockSpec((rows, B_W), lambda i: (i, 0)),
                  pl.BlockSpec((rows, B_W), lambda i: (i, 0)),
                  pl.BlockSpec((seqs, past, B_W), lambda i: (i, 0, 0)),
                  pl.BlockSpec((seqs, past, B_W), lambda i: (i, 0, 0)),
                  _const_spec((HEAD_PAIRS, 2 * seq_len, past)),
                  _const_spec((HEAD_PAIRS, 2 * seq_len, seq_len)),
                  pl.BlockSpec((seqs, MEM_LEN, MEM_W), lambda i: (i, 0, 0)),
                  pl.BlockSpec((seqs, MEM_LEN, MEM_W), lambda i: (i, 0, 0)),
                  _const_spec((B_W + MEM_W, D_MODEL)), _const_spec((1, D_MODEL))],
        out_specs=pl.BlockSpec((rows, D_MODEL), lambda i: (i, 0)),
        out_shape=jax.ShapeDtypeStruct((n, D_MODEL), F32),
    )(x, q3, qm, kn, vn, ck, cv, bias_c, bias_n, mk, mv, wout, gpost)


def _spatial_tile(w_s, b_s, period):
    tril = jnp.tril(jnp.ones((GM_CHUNK, GM_CHUNK), dtype=bool))
    w = jnp.where(tril, w_s, jnp.zeros((), w_s.dtype))[:, :period, :period]
    eye = jnp.eye(TILE // period, dtype=w.dtype)
    s_mat = jnp.einsum("ab,gts->gatbs", eye, w).reshape(GM_GROUPS, TILE, TILE)
    rows = jnp.tile(b_s[:, :period], (1, TILE // period))
    bs = jnp.repeat(rows.T, GM_GW, axis=1)
    return s_mat.astype(BF16), bs


def _rel_bias_table(rel_bias, n_q, n_k, n_past):
    rel = n_past + jnp.arange(n_q)[:, None] - jnp.arange(n_k)[None, :]
    return rel_bias[:, jnp.clip(rel, -REL_CLIP, REL_CLIP) + REL_CLIP]


def _prompt_bias(rel_bias):
    n_k = BAND_PAST + TILE
    r = jnp.arange(TILE)[:, None]
    j = jnp.arange(n_k)[None, :] - CHUNK * (r // CHUNK)
    ok = (j >= 0) & (j < BAND_PAST + CHUNK)
    return jnp.where(ok[None], _rel_bias_table(rel_bias, TILE, n_k, BAND_PAST), NEG)


def _pair_rows(b):
    h, t, k = b.shape
    return b.reshape(h // 2, 2 * t, k)


def kernel(x_prompt, x_sample, cache_mem_k, cache_mem_v, cache_band_k, cache_band_v, mem_prompt,
           g_mix_pre, g_mix_post, g_ffn_pre, g_ffn_post, g_mem, w_mem_kv,
           w_in_a, g_gm_ln, b_gm_ln, w_spatial, b_spatial, w_out_a,
           g_kv, w_kv, w_in_b, rel_bias, w_out_b, w_ff1, w_ff2):
    seq = x_prompt.shape[1]
    n_seq, seq_len = x_sample.shape[0], x_sample.shape[1]
    past = cache_band_k.shape[1]
    vec = lambda a: a.reshape(1, -1)

    win_a = w_in_a[0].astype(BF16)
    wout_a = w_out_a[0].astype(BF16)
    wk = w_kv[:, :B_W].astype(BF16)
    wv = w_kv[:, B_W:].astype(BF16)
    wkt = wk.T
    win_b = w_in_b[0].astype(BF16)
    wout_b = w_out_b[0].astype(BF16)
    w1 = w_ff1.astype(BF16)
    w2 = w_ff2.astype(BF16)
    ln_g, ln_b = vec(g_gm_ln[0]), vec(b_gm_ln[0])
    proj_w = (vec(g_kv), vec(g_mix_pre[1]), wkt, wk, wv, win_b)

    def ffn(x, l):
        return _ffn(x, vec(g_ffn_pre[l]), vec(g_ffn_post[l]), w1[l], w2[l])

    mem_k, mem_v = _memkv(mem_prompt[0], g_mem, w_mem_kv)
    s_p, bs_p = _spatial_tile(w_spatial[0], b_spatial[0], GM_CHUNK)
    x = x_prompt[0]
    x, = _mixer_a(x, vec(g_mix_pre[0]), vec(g_mix_post[0]), win_a, ln_g, ln_b, s_p, bs_p,
                  mem_k[0:1], mem_v[0:1], wout_a, rows_per_mem=TILE, emit_v=False)
    x = ffn(x, 0)
    q3, qm, kt3, v3 = _proj_b(x, *proj_w, rows=512, want_q=True, want_kv_bf16=True, want_kv_f32=False)
    n_keep = min(BAND_PAST, seq)
    k_tail, v_tail = _proj_b(x, *proj_w, rows=TILE, row_offset=seq - n_keep, n_rows=n_keep,
                             want_q=False, want_kv_bf16=False, want_kv_f32=True)
    x = _band_prompt(x, q3, qm, kt3, v3, _prompt_bias(rel_bias[0]), mem_k[1:2], mem_v[1:2],
                     wout_b, vec(g_mix_post[1]))
    y_prompt = ffn(x, 1)[None]

    s_s, bs_s = _spatial_tile(w_spatial[0], b_spatial[0], seq_len)
    xs = x_sample.reshape(n_seq * seq_len, D_MODEL)
    cmk = cache_mem_k.reshape(DEPTH, n_seq, MEM_LEN, MEM_W)
    cmv = cache_mem_v.reshape(DEPTH, n_seq, MEM_LEN, MEM_W)
    xs, v_rows = _mixer_a(xs, vec(g_mix_pre[0]), vec(g_mix_post[0]), win_a, ln_g, ln_b, s_s, bs_s,
                          cmk[0], cmv[0], wout_a, rows_per_mem=seq_len, emit_v=True)
    xs = ffn(xs, 0)
    q3s, qms, k_new, v_new = _proj_b(xs, *proj_w, rows=512, want_q=True, want_kv_bf16=False,
                                     want_kv_f32=True)
    bias_s = _rel_bias_table(rel_bias[0], seq_len, past + seq_len, past)
    xs = _band_sample(xs, q3s, qms, k_new, v_new,
                      cache_band_k.reshape(n_seq, past, B_W), cache_band_v.reshape(n_seq, past, B_W),
                      _pair_rows(bias_s[:, :, :past]), _pair_rows(bias_s[:, :, past:]),
                      cmk[1], cmv[1], wout_b, vec(g_mix_post[1]))
    y_sample = ffn(xs, 1).reshape(n_seq, seq_len, D_MODEL)

    heads = lambda a, h: a.reshape(a.shape[:-1] + (h, HEAD_DIM))
    return (y_prompt, y_sample,
            heads(mem_k, MEM_HEADS)[:, None], heads(mem_v, MEM_HEADS)[:, None],
            heads(k_tail, B_HEADS)[None], heads(v_tail, B_HEADS)[None],
            v_rows.reshape(1, n_seq, seq_len, GM_W),
            heads(k_new, B_HEADS).reshape(n_seq, seq_len, B_HEADS, HEAD_DIM),
            heads(v_new, B_HEADS).reshape(n_seq, seq_len, B_HEADS, HEAD_DIM))
```

```python
import functools

import jax
import jax.numpy as jnp
from jax import lax
from jax.experimental import pallas as pl
from jax.experimental.pallas import tpu as pltpu

D_MODEL = 1024
DEPTH = 2
CHUNK = 64
HEAD_DIM = 64
GM_CHUNK = 128
GM_GROUPS = 4
GM_W = 768
GM_GW = GM_W // GM_GROUPS
MEM_LEN = 256
MEM_HEADS = 4
MEM_W = MEM_HEADS * HEAD_DIM
B_HEADS = 12
B_W = B_HEADS * HEAD_DIM
BAND_PAST = 512
REL_CLIP = 128
D_FF = 4 * D_MODEL
EPS = 1e-6

LANES = 128
HEAD_PAIRS = B_W // LANES
Q_SCALE = HEAD_DIM ** -0.5
NEG = -1e30
TILE = 256
BIAS_PERIOD = 1024
VMEM_LIMIT = 56 * 1024 * 1024

BF16 = jnp.bfloat16
F32 = jnp.float32


def _dot(a, b):
    return jnp.dot(a, b, preferred_element_type=F32)


def _dot_nt(a, b):
    return lax.dot_general(a, b, (((1,), (1,)), ((), ())), preferred_element_type=F32)


def _rms(x, g):
    ms = jnp.mean(x * x, axis=-1, keepdims=True)
    return x * lax.rsqrt(ms + EPS) * g


def _lsum(a):
    return jnp.sum(a, axis=-1, keepdims=True)


def _lmax(a):
    return jnp.max(a, axis=-1, keepdims=True)


def _const_spec(shape):
    nd = len(shape)
    return pl.BlockSpec(shape, lambda *_: (0,) * nd, pipeline_mode=pl.Buffered(1))


def _layer_spec(shape, layer):
    nd = len(shape)
    return pl.BlockSpec((None,) + shape, lambda *_: (layer,) + (0,) * nd, pipeline_mode=pl.Buffered(1))


def _call(body, name, **kw):
    return pl.pallas_call(
        body, name=name,
        compiler_params=pltpu.CompilerParams(dimension_semantics=("parallel",),
                                             vmem_limit_bytes=VMEM_LIMIT), **kw)


def _memkv_kernel(mem_ref, g_ref, wt_ref, kt_ref, vt_ref):
    ht = _dot_nt(wt_ref[...], _rms(mem_ref[...], g_ref[...]).astype(BF16))
    kt_ref[0] = ht[:MEM_W]
    vt_ref[0] = ht[MEM_W:]


def _memkv(mem, g_mem, w_mem_kv_t):
    out = jax.ShapeDtypeStruct((DEPTH, 1, MEM_W, MEM_LEN), F32)
    return _call(
        _memkv_kernel, "mem_kv",
        grid=(DEPTH,),
        in_specs=[
            pl.BlockSpec((MEM_LEN, D_MODEL), lambda l: (0, 0)),
            pl.BlockSpec((None, 1, D_MODEL), lambda l: (l, 0, 0)),
            pl.BlockSpec((None, 2 * MEM_W, D_MODEL), lambda l: (l, 0, 0)),
        ],
        out_specs=[pl.BlockSpec((None, 1, MEM_W, MEM_LEN), lambda l: (l, 0, 0, 0))] * 2,
        out_shape=[out, out],
    )(mem, g_mem.reshape(DEPTH, 1, D_MODEL), w_mem_kv_t)


def _mem_attend(qb, kt, vt):
    r = qb.shape[0]
    lane = lax.broadcasted_iota(jnp.int32, (1, MEM_W), 1)
    masks = [(lane >= h * HEAD_DIM) & (lane < (h + 1) * HEAD_DIM) for h in range(MEM_HEADS)]
    qs = jnp.concatenate([jnp.where(m, qb, jnp.zeros_like(qb)) for m in masks], axis=0)
    s = _dot(qs, kt)
    e = jnp.exp(s - _lmax(s))
    pv = _dot_nt(e.astype(BF16), vt) * (1.0 / _lsum(e))
    out = jnp.where(masks[0], pv[:r], 0.0)
    for h in range(1, MEM_HEADS):
        out = out + jnp.where(masks[h], pv[h * r:(h + 1) * r], 0.0)
    return out


def _gelu(x):
    return jax.nn.gelu(x, approximate=True)


def _mixer_a_kernel(x_ref, gpre_ref, gpost_ref, win_ref, gln_ref, bln_ref, s_ref, bs_ref,
                    mk_ref, mv_ref, wout_ref, o_ref, *v_out, rows_per_mem):
    x = x_ref[...]
    z = _dot(_rms(x, gpre_ref[...]).astype(BF16), win_ref[...])
    nt = GM_W // LANES
    u = [_gelu(z[:, j * LANES:(j + 1) * LANES]) for j in range(nt)]
    t = [_gelu(z[:, GM_W + j * LANES:GM_W + (j + 1) * LANES]) for j in range(nt)]

    lane = lax.broadcasted_iota(jnp.int32, (1, LANES), 1)
    lo = lane < (GM_GW - LANES)
    inv = 1.0 / GM_GW

    def group_stat(a):
        s0 = _lsum(a[0] + jnp.where(lo, a[1], 0.0)) * inv
        s1 = _lsum(jnp.where(lo, 0.0, a[1]) + a[2]) * inv
        s2 = _lsum(a[3] + jnp.where(lo, a[4], 0.0)) * inv
        s3 = _lsum(jnp.where(lo, 0.0, a[4]) + a[5]) * inv
        return [s0, jnp.where(lo, s0, s1), s1, s2, jnp.where(lo, s2, s3), s3]

    mu = group_stat(t)
    c = [t[j] - mu[j] for j in range(nt)]
    var = group_stat([cj * cj for cj in c])
    gln = gln_ref[...]
    bln = bln_ref[...]
    vn = [c[j] * lax.rsqrt(var[j] + EPS) * gln[:, j * LANES:(j + 1) * LANES]
          + bln[:, j * LANES:(j + 1) * LANES] for j in range(nt)]
    if v_out:
        v_out[0][...] = jnp.concatenate(vn, axis=-1)

    vb = [a.astype(BF16) for a in vn]
    win = [(0, 1), (1, 2), (3, 4), (4, 5)]
    m = [_dot(s_ref[g], jnp.concatenate([vb[a], vb[b]], axis=-1)) for g, (a, b) in enumerate(win)]
    mixed = [m[0][:, :LANES], jnp.where(lo, m[0][:, LANES:], m[1][:, :LANES]), m[1][:, LANES:],
             m[2][:, :LANES], jnp.where(lo, m[2][:, LANES:], m[3][:, :LANES]), m[3][:, LANES:]]
    bs = bs_ref[...]
    gm = [u[j] * (mixed[j] + bs[:, j * LANES:(j + 1) * LANES]) for j in range(nt)]

    qm = (z[:, 2 * GM_W:] * Q_SCALE).astype(BF16)
    mo = []
    for b in range(TILE // rows_per_mem):
        rows = slice(b * rows_per_mem, (b + 1) * rows_per_mem)
        mo.append(_mem_attend(qm[rows], mk_ref[b].astype(BF16), mv_ref[b].astype(BF16)))
    mo = mo[0] if len(mo) == 1 else jnp.concatenate(mo, axis=0)

    cat = jnp.concatenate([a.astype(BF16) for a in gm] + [mo.astype(BF16)], axis=-1)
    o_ref[...] = x + _rms(_dot(cat, wout_ref[...]), gpost_ref[...])


def _mem_spec(mem, seqs, layer):
    if mem.shape[1] == 1:
        return pl.BlockSpec((None, 1, MEM_W, MEM_LEN), lambda i: (layer, 0, 0, 0))
    return pl.BlockSpec((None, seqs, MEM_W, MEM_LEN), lambda i: (layer, i, 0, 0))


def _mixer_a(x, gpre, gpost, win, gln, bln, s_mat, bs, mkt, mvt, wout, *, rows_per_mem, emit_v):
    n = x.shape[0]
    seqs = TILE // rows_per_mem
    row = lambda w: pl.BlockSpec((TILE, w), lambda i: (i, 0))
    out_shape = [jax.ShapeDtypeStruct((n, D_MODEL), F32)]
    out_specs = [row(D_MODEL)]
    if emit_v:
        out_shape.append(jax.ShapeDtypeStruct((n, GM_W), F32))
        out_specs.append(row(GM_W))
    return _call(
        functools.partial(_mixer_a_kernel, rows_per_mem=rows_per_mem), "mixer_a",
        grid=(n // TILE,),
        in_specs=[
            row(D_MODEL),
            _const_spec((1, D_MODEL)), _const_spec((1, D_MODEL)),
            _const_spec((D_MODEL, 2 * GM_W + MEM_W)),
            _const_spec((1, GM_W)), _const_spec((1, GM_W)),
            _const_spec((GM_GROUPS, TILE, TILE)), _const_spec((TILE, GM_W)),
            _mem_spec(mkt, seqs, 0), _mem_spec(mvt, seqs, 0),
            _const_spec((GM_W + MEM_W, D_MODEL)),
        ],
        out_specs=out_specs,
        out_shape=out_shape,
    )(x, gpre, gpost, win, gln, bln, s_mat, bs, mkt, mvt, wout)


def _ffn_kernel(x_ref, gpre_ref, gpost_ref, w1_ref, w2_ref, o_ref, *, ff_chunk):
    x = x_ref[...]
    xn = _rms(x, gpre_ref[...]).astype(BF16)
    acc = jnp.zeros(x.shape, F32)
    for c in range(D_FF // ff_chunk):
        h = _dot(xn, w1_ref[:, c * ff_chunk:(c + 1) * ff_chunk])
        h = jnp.square(jnp.maximum(h, 0.0)).astype(BF16)
        acc = acc + _dot(h, w2_ref[c * ff_chunk:(c + 1) * ff_chunk, :])
    o_ref[...] = x + _rms(acc, gpost_ref[...])


def _ffn(x, gpre, gpost, w1, w2, layer, *, rows=512, ff_chunk=1024):
    n = x.shape[0]
    row = pl.BlockSpec((rows, D_MODEL), lambda i: (i, 0))
    return _call(
        functools.partial(_ffn_kernel, ff_chunk=ff_chunk), "ffn",
        grid=(n // rows,),
        in_specs=[row, _layer_spec((1, D_MODEL), layer), _layer_spec((1, D_MODEL), layer),
                  _layer_spec((D_MODEL, D_FF), layer), _layer_spec((D_FF, D_MODEL), layer)],
        out_specs=row,
        out_shape=jax.ShapeDtypeStruct((n, D_MODEL), F32),
    )(x, gpre, gpost, w1, w2)


def _proj_b_kernel(x_ref, gkv_ref, gpre_ref, *refs, mode, seq_len):
    refs = list(refs)
    x = x_ref[...]
    rows = x.shape[0]
    xh = x * lax.rsqrt(jnp.mean(x * x, axis=-1, keepdims=True) + EPS)
    xkv = (xh * gkv_ref[...]).astype(BF16)
    if mode == "tail":
        wkt_ref, wvt_ref, kt_ref, vt_ref = refs
        kt_ref[...] = _dot_nt(wkt_ref[...], xkv)
        vt_ref[...] = _dot_nt(wvt_ref[...], xkv)
        return
    win_ref, wk_ref, wv_ref, q_ref, qm_ref, k_ref, v_ref = refs[:7]
    z = _dot((xh * gpre_ref[...]).astype(BF16), win_ref[...]) * Q_SCALE
    for p in range(HEAD_PAIRS):
        q_ref[p] = z[:, p * LANES:(p + 1) * LANES].astype(BF16)
    qm_ref[...] = z[:, B_W:].astype(BF16)
    v = _dot(xkv, wv_ref[...])
    for p in range(HEAD_PAIRS):
        v_ref[p] = v[:, p * LANES:(p + 1) * LANES].astype(BF16)
    if mode == "prompt":
        kt = _dot_nt(wk_ref[...], xkv)
        for p in range(HEAD_PAIRS):
            k_ref[p] = kt[p * LANES:(p + 1) * LANES, :].astype(BF16)
    else:
        k4_ref, v4_ref = refs[7:]
        k = _dot(xkv, wk_ref[...])
        for p in range(HEAD_PAIRS):
            k_ref[p] = k[:, p * LANES:(p + 1) * LANES].astype(BF16)
        for b in range(rows // seq_len):
            for h in range(B_HEADS):
                r, c = slice(b * seq_len, (b + 1) * seq_len), slice(h * HEAD_DIM, (h + 1) * HEAD_DIM)
                k4_ref[b, h] = k[r, c]
                v4_ref[b, h] = v[r, c]


def _proj_b(x, gkv, gpre, weights, *, mode, rows, row_offset=0, n_rows=None, seq_len=None):
    n = x.shape[0] if n_rows is None else n_rows
    off = row_offset // rows
    pair_rows = jax.ShapeDtypeStruct((HEAD_PAIRS, n, LANES), BF16)
    pair_rows_spec = pl.BlockSpec((HEAD_PAIRS, rows, LANES), lambda i: (0, i, 0))
    w_spec = lambda w: _const_spec(w.shape)
    if mode == "tail":
        out_shape = [jax.ShapeDtypeStruct((B_W, n), F32)] * 2
        out_specs = [pl.BlockSpec((B_W, rows), lambda i: (0, i))] * 2
    else:
        out_shape = [pair_rows, jax.ShapeDtypeStruct((n, MEM_W), BF16)]
        out_specs = [pair_rows_spec, pl.BlockSpec((rows, MEM_W), lambda i: (i, 0))]
        if mode == "prompt":
            out_shape += [jax.ShapeDtypeStruct((HEAD_PAIRS, LANES, n), BF16), pair_rows]
            out_specs += [pl.BlockSpec((HEAD_PAIRS, LANES, rows), lambda i: (0, 0, i)), pair_rows_spec]
        else:
            seqs = rows // seq_len
            per_head = jax.ShapeDtypeStruct((n // seq_len, B_HEADS, seq_len, HEAD_DIM), F32)
            per_head_spec = pl.BlockSpec((seqs, B_HEADS, seq_len, HEAD_DIM), lambda i: (i, 0, 0, 0))
            out_shape += [pair_rows, pair_rows, per_head, per_head]
            out_specs += [pair_rows_spec, pair_rows_spec, per_head_spec, per_head_spec]
    return _call(
        functools.partial(_proj_b_kernel, mode=mode, seq_len=seq_len), "proj_b_" + mode,
        grid=(n // rows,),
        in_specs=[pl.BlockSpec((rows, D_MODEL), lambda i: (i + off, 0)),
                  _const_spec((1, D_MODEL)), _const_spec((1, D_MODEL))] + [w_spec(w) for w in weights],
        out_specs=out_specs,
        out_shape=out_shape,
    )(x, gkv, gpre, *weights)


def _bias_kernel(g_ref, *o_refs, n_q, splits, band):
    n_k = splits[-1][1]
    for hh in range(2):
        x = jnp.broadcast_to(g_ref[hh:hh + 1, :], (n_q, BIAS_PERIOD))
        t = pltpu.roll(x, BIAS_PERIOD - (n_q - 1), 1, stride=1, stride_axis=0)
        if band:
            r = lax.broadcasted_iota(jnp.int32, (n_q, BIAS_PERIOD), 0)
            w = lax.broadcasted_iota(jnp.int32, (n_q, BIAS_PERIOD), 1)
            j = w - (r - (r & (CHUNK - 1)))
            t = jnp.where((j >= 0) & (j < BAND_PAST + CHUNK), t, NEG)
        for o_ref, (a, b) in zip(o_refs, splits):
            o_ref[hh * n_q:(hh + 1) * n_q, :] = t[:, a:b]


def _rel_bias_tables(rel_bias, n_q, n_past, splits, band):
    c0 = n_past + n_q - 1
    far = jnp.broadcast_to(rel_bias[:, -1:], (B_HEADS, c0 - REL_CLIP))
    near = jnp.broadcast_to(rel_bias[:, :1], (B_HEADS, BIAS_PERIOD - (c0 - REL_CLIP) - (2 * REL_CLIP + 1)))
    gen = jnp.concatenate([far, rel_bias[:, ::-1], near], axis=1).reshape(HEAD_PAIRS, 2, BIAS_PERIOD)
    return _call(
        functools.partial(_bias_kernel, n_q=n_q, splits=splits, band=band), "rel_bias",
        grid=(HEAD_PAIRS,),
        in_specs=[pl.BlockSpec((None, 2, BIAS_PERIOD), lambda p: (p, 0, 0))],
        out_specs=[pl.BlockSpec((None, 2 * n_q, b - a), lambda p: (p, 0, 0)) for a, b in splits],
        out_shape=[jax.ShapeDtypeStruct((HEAD_PAIRS, 2 * n_q, b - a), F32) for a, b in splits],
    )(gen)


def _band_prompt_kernel(x_ref, q_ref, qm_ref, kt0_ref, kt1_ref, kt2_ref, v0_ref, v1_ref, v2_ref,
                        bias_ref, mk_ref, mv_ref, wout_ref, gpost_ref, o_ref):
    i = pl.program_id(0)
    kts = (kt0_ref, kt1_ref, kt2_ref)
    vs = (v0_ref, v1_ref, v2_ref)
    nblk = len(kts)
    lane = lax.broadcasted_iota(jnp.int32, (1, LANES), 1)
    lo = lane < HEAD_DIM
    band = []
    for p in range(HEAD_PAIRS):
        qp = q_ref[p]
        zero = jnp.zeros_like(qp)
        heads = []
        for hh in range(2):
            qh = jnp.where(lo, qp, zero) if hh == 0 else jnp.where(lo, zero, qp)
            s = [_dot(qh, kts[j][p]) + bias_ref[p, hh * TILE:(hh + 1) * TILE, j * TILE:(j + 1) * TILE]
                 for j in range(nblk)]
            s = [jnp.where(i + j >= nblk - 1, s[j], NEG) if j < nblk - 1 else s[j] for j in range(nblk)]
            m = functools.reduce(jnp.maximum, [_lmax(a) for a in s])
            e = [jnp.exp(a - m) for a in s]
            l = functools.reduce(lambda a, b: a + b, [_lsum(a) for a in e])
            o = functools.reduce(lambda a, b: a + b,
                                 [_dot(e[j].astype(BF16), vs[j][p]) for j in range(nblk)])
            heads.append(o * (1.0 / l))
        band.append(jnp.where(lo, heads[0], heads[1]).astype(BF16))
    mo = _mem_attend(qm_ref[...], mk_ref[0].astype(BF16), mv_ref[0].astype(BF16))
    cat = jnp.concatenate(band + [mo.astype(BF16)], axis=-1)
    o_ref[...] = x_ref[...] + _rms(_dot(cat, wout_ref[...]), gpost_ref[...])


def _band_prompt(x, q3, qm, kt3, v3, bias, mkt, mvt, wout, gpost):
    n = x.shape[0]
    nblk = BAND_PAST // TILE + 1
    kt_specs = [pl.BlockSpec((HEAD_PAIRS, LANES, TILE),
                             functools.partial(lambda i, j: (0, 0, jnp.maximum(i - (nblk - 1) + j, 0)), j=j))
                for j in range(nblk)]
    v_specs = [pl.BlockSpec((HEAD_PAIRS, TILE, LANES),
                            functools.partial(lambda i, j: (0, jnp.maximum(i - (nblk - 1) + j, 0), 0), j=j))
               for j in range(nblk)]
    return _call(
        _band_prompt_kernel, "band_prompt",
        grid=(n // TILE,),
        in_specs=[pl.BlockSpec((TILE, D_MODEL), lambda i: (i, 0)),
                  pl.BlockSpec((HEAD_PAIRS, TILE, LANES), lambda i: (0, i, 0)),
                  pl.BlockSpec((TILE, MEM_W), lambda i: (i, 0)),
                  *kt_specs, *v_specs,
                  _const_spec((HEAD_PAIRS, 2 * TILE, nblk * TILE)),
                  _mem_spec(mkt, 1, 1), _mem_spec(mvt, 1, 1),
                  _const_spec((B_W + MEM_W, D_MODEL)), _const_spec((1, D_MODEL))],
        out_specs=pl.BlockSpec((TILE, D_MODEL), lambda i: (i, 0)),
        out_shape=jax.ShapeDtypeStruct((n, D_MODEL), F32),
    )(x, q3, qm, kt3, kt3, kt3, v3, v3, v3, bias, mkt, mvt, wout, gpost)


def _band_sample_kernel(x_ref, q_ref, qm_ref, kn_ref, vn_ref, ck_ref, cv_ref, bc_ref, bn_ref,
                        mk_ref, mv_ref, wout_ref, gpost_ref, o_ref, *, seqs, seq_len):
    lane = lax.broadcasted_iota(jnp.int32, (1, LANES), 1)
    lo = lane < HEAD_DIM
    rows_out = []
    for b in range(seqs):
        rows = slice(b * seq_len, (b + 1) * seq_len)
        band = []
        for p in range(HEAD_PAIRS):
            hd = slice(p * LANES, (p + 1) * LANES)
            qp = q_ref[p, rows, :]
            zero = jnp.zeros_like(qp)
            qs = jnp.concatenate([jnp.where(lo, qp, zero), jnp.where(lo, zero, qp)], axis=0)
            sc = _dot(qs, ck_ref[b, hd, :].astype(BF16)) + bc_ref[p]
            sn = _dot_nt(qs, kn_ref[p, rows, :]) + bn_ref[p]
            m = jnp.maximum(_lmax(sc), _lmax(sn))
            ec = jnp.exp(sc - m)
            en = jnp.exp(sn - m)
            o = (_dot_nt(ec.astype(BF16), cv_ref[b, hd, :].astype(BF16))
                 + _dot(en.astype(BF16), vn_ref[p, rows, :]))
            o = o * (1.0 / (_lsum(ec) + _lsum(en)))
            band.append(jnp.where(lo, o[:seq_len], o[seq_len:]).astype(BF16))
        mo = _mem_attend(qm_ref[rows, :], mk_ref[b].astype(BF16), mv_ref[b].astype(BF16))
        rows_out.append(jnp.concatenate(band + [mo.astype(BF16)], axis=-1))
    cat = jnp.concatenate(rows_out, axis=0)
    o_ref[...] = x_ref[...] + _rms(_dot(cat, wout_ref[...]), gpost_ref[...])


def _band_sample(x, q3, qm, kn3, vn3, ckt, cvt, bias_c, bias_n, mkt, mvt, wout, gpost, *, seqs=4):
    n = x.shape[0]
    n_seq, past = ckt.shape[0], ckt.shape[2]
    seq_len = n // n_seq
    rows = seqs * seq_len
    pair_rows_spec = pl.BlockSpec((HEAD_PAIRS, rows, LANES), lambda i: (0, i, 0))
    cache_spec = pl.BlockSpec((seqs, B_W, past), lambda i: (i, 0, 0))
    return _call(
        functools.partial(_band_sample_kernel, seqs=seqs, seq_len=seq_len), "band_sample",
        grid=(n_seq // seqs,),
        in_specs=[pl.BlockSpec((rows, D_MODEL), lambda i: (i, 0)),
                  pair_rows_spec,
                  pl.BlockSpec((rows, MEM_W), lambda i: (i, 0)),
                  pair_rows_spec, pair_rows_spec,
                  cache_spec, cache_spec,
                  _const_spec((HEAD_PAIRS, 2 * seq_len, past)),
                  _const_spec((HEAD_PAIRS, 2 * seq_len, seq_len)),
                  _mem_spec(mkt, seqs, 1), _mem_spec(mvt, seqs, 1),
                  _const_spec((B_W + MEM_W, D_MODEL)), _const_spec((1, D_MODEL))],
        out_specs=pl.BlockSpec((rows, D_MODEL), lambda i: (i, 0)),
        out_shape=jax.ShapeDtypeStruct((n, D_MODEL), F32),
    )(x, q3, qm, kn3, vn3, ckt, cvt, bias_c, bias_n, mkt, mvt, wout, gpost)


def _spatial_tile(w_s, b_s, period):
    tril = jnp.tril(jnp.ones((GM_CHUNK, GM_CHUNK), dtype=bool))
    w = jnp.where(tril, w_s, jnp.zeros((), w_s.dtype))[:, :period, :period]
    eye = jnp.eye(TILE // period, dtype=w.dtype)
    s_mat = jnp.einsum("ab,gts->gatbs", eye, w).reshape(GM_GROUPS, TILE, TILE)
    rows = jnp.tile(b_s[:, :period], (1, TILE // period))
    bs = jnp.repeat(rows.T, GM_GW, axis=1)
    return s_mat.astype(BF16), bs


def _heads_last(t, lead):
    pos = t.shape[-1]
    t = t.reshape(lead + (-1, HEAD_DIM, pos))
    nd = len(lead)
    return jnp.transpose(t, tuple(range(nd)) + (nd + 2, nd, nd + 1))


def _positions_last(c):
    nd = c.ndim
    t = jnp.transpose(c, tuple(range(nd - 3)) + (nd - 2, nd - 1, nd - 3))
    return t.reshape(c.shape[:-3] + (c.shape[-2] * c.shape[-1], c.shape[-3]))


def kernel(x_prompt, x_sample, cache_mem_k, cache_mem_v, cache_band_k, cache_band_v, mem_prompt,
           g_mix_pre, g_mix_post, g_ffn_pre, g_ffn_post, g_mem, w_mem_kv,
           w_in_a, g_gm_ln, b_gm_ln, w_spatial, b_spatial, w_out_a,
           g_kv, w_kv, w_in_b, rel_bias, w_out_b, w_ff1, w_ff2):
    seq = x_prompt.shape[1]
    n_seq, seq_len = x_sample.shape[0], x_sample.shape[1]
    past = cache_band_k.shape[1]
    vec = lambda a: a.reshape(1, -1)
    stack = lambda a: a.reshape(DEPTH, 1, -1)

    win_a = w_in_a[0].astype(BF16)
    wout_a = w_out_a[0].astype(BF16)
    wk = w_kv[:, :B_W].astype(BF16)
    wv = w_kv[:, B_W:].astype(BF16)
    wkt, wvt = wk.T, wv.T
    win_b = w_in_b[0].astype(BF16)
    wout_b = w_out_b[0].astype(BF16)
    w1 = w_ff1.astype(BF16)
    w2 = w_ff2.astype(BF16)
    ln_g, ln_b = vec(g_gm_ln[0]), vec(b_gm_ln[0])
    gkv, gpre_b = vec(g_kv), vec(g_mix_pre[1])
    gf_pre, gf_post = stack(g_ffn_pre), stack(g_ffn_post)

    def ffn(x, l):
        return _ffn(x, gf_pre, gf_post, w1, w2, l)

    mem_kt, mem_vt = _memkv(mem_prompt[0], g_mem, jnp.swapaxes(w_mem_kv, 1, 2).astype(BF16))
    s_p, bs_p = _spatial_tile(w_spatial[0], b_spatial[0], GM_CHUNK)
    x = x_prompt[0]
    x, = _mixer_a(x, vec(g_mix_pre[0]), vec(g_mix_post[0]), win_a, ln_g, ln_b, s_p, bs_p,
                  mem_kt, mem_vt, wout_a, rows_per_mem=TILE, emit_v=False)
    x = ffn(x, 0)
    q3, qm, kt3, v3 = _proj_b(x, gkv, gpre_b, (win_b, wkt, wv), mode="prompt", rows=512)
    n_keep = min(BAND_PAST, seq)
    kt_tail, vt_tail = _proj_b(x, gkv, gpre_b, (wkt, wvt), mode="tail", rows=TILE,
                               row_offset=seq - n_keep, n_rows=n_keep)
    bias_p, = _rel_bias_tables(rel_bias[0], TILE, BAND_PAST, ((0, BAND_PAST + TILE),), band=True)
    x = _band_prompt(x, q3, qm, kt3, v3, bias_p, mem_kt, mem_vt, wout_b, vec(g_mix_post[1]))
    y_prompt = ffn(x, 1)[None]

    s_s, bs_s = _spatial_tile(w_spatial[0], b_spatial[0], seq_len)
    xs = x_sample.reshape(n_seq * seq_len, D_MODEL)
    cmkt, cmvt = _positions_last(cache_mem_k), _positions_last(cache_mem_v)
    xs, v_rows = _mixer_a(xs, vec(g_mix_pre[0]), vec(g_mix_post[0]), win_a, ln_g, ln_b, s_s, bs_s,
                          cmkt, cmvt, wout_a, rows_per_mem=seq_len, emit_v=True)
    xs = ffn(xs, 0)
    q3s, qms, kn3, vn3, k_new, v_new = _proj_b(xs, gkv, gpre_b, (win_b, wk, wv), mode="sample",
                                               rows=512, seq_len=seq_len)
    bias_c, bias_n = _rel_bias_tables(rel_bias[0], seq_len, past, ((0, past), (past, past + seq_len)),
                                      band=False)
    xs = _band_sample(xs, q3s, qms, kn3, vn3, _positions_last(cache_band_k), _positions_last(cache_band_v),
                      bias_c, bias_n, cmkt, cmvt, wout_b, vec(g_mix_post[1]))
    y_sample = ffn(xs, 1).reshape(n_seq, seq_len, D_MODEL)

    return (y_prompt, y_sample,
            _heads_last(mem_kt, (DEPTH, 1)), _heads_last(mem_vt, (DEPTH, 1)),
            _heads_last(kt_tail, (1,)), _heads_last(vt_tail, (1,)),
            v_rows.reshape(1, n_seq, seq_len, GM_W),
            jnp.swapaxes(k_new, 1, 2), jnp.swapaxes(v_new, 1, 2))
```

```python
import functools

import jax
import jax.numpy as jnp
from jax import lax
from jax.experimental import pallas as pl
from jax.experimental.pallas import tpu as pltpu

D_MODEL = 1024
DEPTH = 2
CHUNK = 64
HEAD_DIM = 64
GM_CHUNK = 128
GM_GROUPS = 4
GM_W = 768
GM_GW = GM_W // GM_GROUPS
MEM_LEN = 256
MEM_HEADS = 4
MEM_W = MEM_HEADS * HEAD_DIM
B_HEADS = 12
B_W = B_HEADS * HEAD_DIM
BAND_PAST = 512
REL_CLIP = 128
D_FF = 4 * D_MODEL
EPS = 1e-6

LANES = 128
HEAD_PAIRS = B_W // LANES
Q_SCALE = HEAD_DIM ** -0.5
NEG = -1e30
TILE = 256
KEY_BLOCKS = BAND_PAST // TILE + 1
KEY_STEP = 128
BIAS_PERIOD = 1024
VMEM_LIMIT = 56 * 1024 * 1024

BF16 = jnp.bfloat16
F32 = jnp.float32


def _dot(a, b):
    return jnp.dot(a, b, preferred_element_type=F32)


def _dot_nt(a, b):
    return lax.dot_general(a, b, (((1,), (1,)), ((), ())), preferred_element_type=F32)


def _rms(x, g):
    ms = jnp.mean(x * x, axis=-1, keepdims=True)
    return x * lax.rsqrt(ms + EPS) * g


def _lsum(a):
    return jnp.sum(a, axis=-1, keepdims=True)


def _lmax(a):
    return jnp.max(a, axis=-1, keepdims=True)


def _const_spec(shape):
    nd = len(shape)
    return pl.BlockSpec(shape, lambda *_: (0,) * nd, pipeline_mode=pl.Buffered(1))


def _layer_spec(shape, layer):
    nd = len(shape)
    return pl.BlockSpec((None,) + shape, lambda *_: (layer,) + (0,) * nd, pipeline_mode=pl.Buffered(1))


def _call(body, name, **kw):
    return pl.pallas_call(
        body, name=name,
        compiler_params=pltpu.CompilerParams(dimension_semantics=("parallel",),
                                             vmem_limit_bytes=VMEM_LIMIT), **kw)


def _memkv_kernel(mem_ref, g_ref, wt_ref, kt_ref, vt_ref):
    ht = _dot_nt(wt_ref[...], _rms(mem_ref[...], g_ref[...]).astype(BF16))
    kt_ref[0] = ht[:MEM_W]
    vt_ref[0] = ht[MEM_W:]


def _memkv(mem, g_mem, w_mem_kv_t):
    out = jax.ShapeDtypeStruct((DEPTH, 1, MEM_W, MEM_LEN), F32)
    return _call(
        _memkv_kernel, "mem_kv",
        grid=(DEPTH,),
        in_specs=[
            pl.BlockSpec((MEM_LEN, D_MODEL), lambda l: (0, 0)),
            pl.BlockSpec((None, 1, D_MODEL), lambda l: (l, 0, 0)),
            pl.BlockSpec((None, 2 * MEM_W, D_MODEL), lambda l: (l, 0, 0)),
        ],
        out_specs=[pl.BlockSpec((None, 1, MEM_W, MEM_LEN), lambda l: (l, 0, 0, 0))] * 2,
        out_shape=[out, out],
    )(mem, g_mem.reshape(DEPTH, 1, D_MODEL), w_mem_kv_t)


def _mem_attend(qb, kt, vt):
    r = qb.shape[0]
    lane = lax.broadcasted_iota(jnp.int32, (1, MEM_W), 1)
    masks = [(lane >= h * HEAD_DIM) & (lane < (h + 1) * HEAD_DIM) for h in range(MEM_HEADS)]
    qs = jnp.concatenate([jnp.where(m, qb, jnp.zeros_like(qb)) for m in masks], axis=0)
    s = _dot(qs, kt)
    e = jnp.exp(s - _lmax(s))
    pv = _dot_nt(e.astype(BF16), vt) * (1.0 / _lsum(e))
    out = jnp.where(masks[0], pv[:r], 0.0)
    for h in range(1, MEM_HEADS):
        out = out + jnp.where(masks[h], pv[h * r:(h + 1) * r], 0.0)
    return out


def _gelu(x):
    return jax.nn.gelu(x, approximate=True)


def _mixer_a_kernel(x_ref, gpre_ref, gpost_ref, win_ref, gln_ref, bln_ref, s_ref, bs_ref,
                    mk_ref, mv_ref, wout_ref, o_ref, *v_out, rows_per_mem):
    x = x_ref[...]
    z = _dot(_rms(x, gpre_ref[...]).astype(BF16), win_ref[...])
    nt = GM_W // LANES
    u = [_gelu(z[:, j * LANES:(j + 1) * LANES]) for j in range(nt)]
    t = [_gelu(z[:, GM_W + j * LANES:GM_W + (j + 1) * LANES]) for j in range(nt)]

    lane = lax.broadcasted_iota(jnp.int32, (1, LANES), 1)
    lo = lane < (GM_GW - LANES)
    inv = 1.0 / GM_GW

    def group_stat(a):
        s0 = _lsum(a[0] + jnp.where(lo, a[1], 0.0)) * inv
        s1 = _lsum(jnp.where(lo, 0.0, a[1]) + a[2]) * inv
        s2 = _lsum(a[3] + jnp.where(lo, a[4], 0.0)) * inv
        s3 = _lsum(jnp.where(lo, 0.0, a[4]) + a[5]) * inv
        return [s0, jnp.where(lo, s0, s1), s1, s2, jnp.where(lo, s2, s3), s3]

    mu = group_stat(t)
    c = [t[j] - mu[j] for j in range(nt)]
    var = group_stat([cj * cj for cj in c])
    gln = gln_ref[...]
    bln = bln_ref[...]
    vn = [c[j] * lax.rsqrt(var[j] + EPS) * gln[:, j * LANES:(j + 1) * LANES]
          + bln[:, j * LANES:(j + 1) * LANES] for j in range(nt)]
    if v_out:
        v_out[0][...] = jnp.concatenate(vn, axis=-1)

    vb = [a.astype(BF16) for a in vn]
    win = [(0, 1), (1, 2), (3, 4), (4, 5)]
    m = [_dot(s_ref[g], jnp.concatenate([vb[a], vb[b]], axis=-1)) for g, (a, b) in enumerate(win)]
    mixed = [m[0][:, :LANES], jnp.where(lo, m[0][:, LANES:], m[1][:, :LANES]), m[1][:, LANES:],
             m[2][:, :LANES], jnp.where(lo, m[2][:, LANES:], m[3][:, :LANES]), m[3][:, LANES:]]
    bs = bs_ref[...]
    gm = [u[j] * (mixed[j] + bs[:, j * LANES:(j + 1) * LANES]) for j in range(nt)]

    qm = (z[:, 2 * GM_W:] * Q_SCALE).astype(BF16)
    mo = []
    for b in range(TILE // rows_per_mem):
        rows = slice(b * rows_per_mem, (b + 1) * rows_per_mem)
        mo.append(_mem_attend(qm[rows], mk_ref[b].astype(BF16), mv_ref[b].astype(BF16)))
    mo = mo[0] if len(mo) == 1 else jnp.concatenate(mo, axis=0)

    cat = jnp.concatenate([a.astype(BF16) for a in gm] + [mo.astype(BF16)], axis=-1)
    o_ref[...] = x + _rms(_dot(cat, wout_ref[...]), gpost_ref[...])


def _mem_spec(mem, seqs, layer):
    if mem.shape[1] == 1:
        return pl.BlockSpec((None, 1, MEM_W, MEM_LEN), lambda i: (layer, 0, 0, 0))
    return pl.BlockSpec((None, seqs, MEM_W, MEM_LEN), lambda i: (layer, i, 0, 0))


def _mixer_a(x, gpre, gpost, win, gln, bln, s_mat, bs, mkt, mvt, wout, *, rows_per_mem, emit_v):
    n = x.shape[0]
    seqs = TILE // rows_per_mem
    row = lambda w: pl.BlockSpec((TILE, w), lambda i: (i, 0))
    out_shape = [jax.ShapeDtypeStruct((n, D_MODEL), F32)]
    out_specs = [row(D_MODEL)]
    if emit_v:
        out_shape.append(jax.ShapeDtypeStruct((n, GM_W), F32))
        out_specs.append(row(GM_W))
    return _call(
        functools.partial(_mixer_a_kernel, rows_per_mem=rows_per_mem), "mixer_a",
        grid=(n // TILE,),
        in_specs=[
            row(D_MODEL),
            _const_spec((1, D_MODEL)), _const_spec((1, D_MODEL)),
            _const_spec((D_MODEL, 2 * GM_W + MEM_W)),
            _const_spec((1, GM_W)), _const_spec((1, GM_W)),
            _const_spec((GM_GROUPS, TILE, TILE)), _const_spec((TILE, GM_W)),
            _mem_spec(mkt, seqs, 0), _mem_spec(mvt, seqs, 0),
            _const_spec((GM_W + MEM_W, D_MODEL)),
        ],
        out_specs=out_specs,
        out_shape=out_shape,
    )(x, gpre, gpost, win, gln, bln, s_mat, bs, mkt, mvt, wout)


def _ffn_kernel(x_ref, gpre_ref, gpost_ref, w1_ref, w2_ref, o_ref, *, ff_chunk):
    x = x_ref[...]
    xn = _rms(x, gpre_ref[...]).astype(BF16)
    acc = jnp.zeros(x.shape, F32)
    for c in range(D_FF // ff_chunk):
        h = _dot(xn, w1_ref[:, c * ff_chunk:(c + 1) * ff_chunk])
        h = jnp.square(jnp.maximum(h, 0.0)).astype(BF16)
        acc = acc + _dot(h, w2_ref[c * ff_chunk:(c + 1) * ff_chunk, :])
    o_ref[...] = x + _rms(acc, gpost_ref[...])


def _ffn(x, gpre, gpost, w1, w2, layer, *, rows=512, ff_chunk=1024):
    n = x.shape[0]
    row = pl.BlockSpec((rows, D_MODEL), lambda i: (i, 0))
    return _call(
        functools.partial(_ffn_kernel, ff_chunk=ff_chunk), "ffn",
        grid=(n // rows,),
        in_specs=[row, _layer_spec((1, D_MODEL), layer), _layer_spec((1, D_MODEL), layer),
                  _layer_spec((D_MODEL, D_FF), layer), _layer_spec((D_FF, D_MODEL), layer)],
        out_specs=row,
        out_shape=jax.ShapeDtypeStruct((n, D_MODEL), F32),
    )(x, gpre, gpost, w1, w2)


def _proj_b_kernel(x_ref, gkv_ref, gpre_ref, *refs, mode, seq_len):
    x = x_ref[...]
    rows = x.shape[0]
    xh = x * lax.rsqrt(jnp.mean(x * x, axis=-1, keepdims=True) + EPS)
    xkv = (xh * gkv_ref[...]).astype(BF16)
    if mode == "tail":
        wkt_ref, wvt_ref, kt_ref, vt_ref = refs
        kt_ref[...] = _dot_nt(wkt_ref[...], xkv)
        vt_ref[...] = _dot_nt(wvt_ref[...], xkv)
        return
    xq = (xh * gpre_ref[...]).astype(BF16)
    if mode == "prompt":
        wqt_ref, wqm_ref, wk_ref, wvt_ref, qt_ref, qm_ref, k_ref, vt_ref = refs
        qt = _dot_nt(wqt_ref[...], xq) * Q_SCALE
        vt = _dot_nt(wvt_ref[...], xkv)
        k = _dot(xkv, wk_ref[...])
        qm_ref[...] = (_dot(xq, wqm_ref[...]) * Q_SCALE).astype(BF16)
        for p in range(HEAD_PAIRS):
            qt_ref[p] = qt[p * LANES:(p + 1) * LANES, :].astype(BF16)
            vt_ref[p] = vt[p * LANES:(p + 1) * LANES, :].astype(BF16)
            k_ref[p] = k[:, p * LANES:(p + 1) * LANES].astype(BF16)
        return
    win_ref, wk_ref, wv_ref, q_ref, qm_ref, k_ref, v_ref, k4_ref, v4_ref = refs
    z = _dot(xq, win_ref[...]) * Q_SCALE
    qm_ref[...] = z[:, B_W:].astype(BF16)
    k = _dot(xkv, wk_ref[...])
    v = _dot(xkv, wv_ref[...])
    for p in range(HEAD_PAIRS):
        cols = slice(p * LANES, (p + 1) * LANES)
        q_ref[p] = z[:, cols].astype(BF16)
        k_ref[p] = k[:, cols].astype(BF16)
        v_ref[p] = v[:, cols].astype(BF16)
    for b in range(rows // seq_len):
        for h in range(B_HEADS):
            r, c = slice(b * seq_len, (b + 1) * seq_len), slice(h * HEAD_DIM, (h + 1) * HEAD_DIM)
            k4_ref[b, h] = k[r, c]
            v4_ref[b, h] = v[r, c]


def _proj_b(x, gkv, gpre, weights, *, mode, rows, row_offset=0, n_rows=None, seq_len=None):
    n = x.shape[0] if n_rows is None else n_rows
    off = row_offset // rows
    pair_rows = jax.ShapeDtypeStruct((HEAD_PAIRS, n, LANES), BF16)
    pair_rows_spec = pl.BlockSpec((HEAD_PAIRS, rows, LANES), lambda i: (0, i, 0))
    pair_cols = jax.ShapeDtypeStruct((HEAD_PAIRS, LANES, n), BF16)
    pair_cols_spec = pl.BlockSpec((HEAD_PAIRS, LANES, rows), lambda i: (0, 0, i))
    qm = jax.ShapeDtypeStruct((n, MEM_W), BF16)
    qm_spec = pl.BlockSpec((rows, MEM_W), lambda i: (i, 0))
    if mode == "tail":
        out_shape = [jax.ShapeDtypeStruct((B_W, n), F32)] * 2
        out_specs = [pl.BlockSpec((B_W, rows), lambda i: (0, i))] * 2
    elif mode == "prompt":
        out_shape = [pair_cols, qm, pair_rows, pair_cols]
        out_specs = [pair_cols_spec, qm_spec, pair_rows_spec, pair_cols_spec]
    else:
        seqs = rows // seq_len
        per_head = jax.ShapeDtypeStruct((n // seq_len, B_HEADS, seq_len, HEAD_DIM), F32)
        per_head_spec = pl.BlockSpec((seqs, B_HEADS, seq_len, HEAD_DIM), lambda i: (i, 0, 0, 0))
        out_shape = [pair_rows, qm, pair_rows, pair_rows, per_head, per_head]
        out_specs = [pair_rows_spec, qm_spec, pair_rows_spec, pair_rows_spec, per_head_spec, per_head_spec]
    return _call(
        functools.partial(_proj_b_kernel, mode=mode, seq_len=seq_len), "proj_b_" + mode,
        grid=(n // rows,),
        in_specs=[pl.BlockSpec((rows, D_MODEL), lambda i: (i + off, 0)),
                  _const_spec((1, D_MODEL)), _const_spec((1, D_MODEL))]
                 + [_const_spec(w.shape) for w in weights],
        out_specs=out_specs,
        out_shape=out_shape,
    )(x, gkv, gpre, *weights)


def _bias_kernel(g_ref, *o_refs, n_q, n_k, splits, band, keys_on_rows):
    n_rows, shift = (n_k, n_q) if keys_on_rows else (n_q, BIAS_PERIOD - (n_q - 1))
    for hh in range(2):
        x = jnp.broadcast_to(g_ref[hh:hh + 1, :], (n_rows, BIAS_PERIOD))
        t = pltpu.roll(x, shift, 1, stride=1, stride_axis=0)
        if band:
            a = lax.broadcasted_iota(jnp.int32, (n_rows, BIAS_PERIOD), 0)
            b = lax.broadcasted_iota(jnp.int32, (n_rows, BIAS_PERIOD), 1)
            r, w = (b, a) if keys_on_rows else (a, b)
            j = w - (r - (r & (CHUNK - 1)))
            t = jnp.where((j >= 0) & (j < BAND_PAST + CHUNK), t, NEG)
        for o_ref, (lo, hi) in zip(o_refs, splits):
            o_ref[hh * n_rows:(hh + 1) * n_rows, :] = t[:, lo:hi]


def _rel_bias_tables(rel_bias, n_q, n_k, splits, *, band, keys_on_rows):
    c0 = n_k - 1
    far = jnp.broadcast_to(rel_bias[:, -1:], (B_HEADS, c0 - REL_CLIP))
    near = jnp.broadcast_to(rel_bias[:, :1], (B_HEADS, BIAS_PERIOD - (c0 - REL_CLIP) - (2 * REL_CLIP + 1)))
    gen = [near, rel_bias, far] if keys_on_rows else [far, rel_bias[:, ::-1], near]
    gen = jnp.concatenate(gen, axis=1).reshape(HEAD_PAIRS, 2, BIAS_PERIOD)
    n_rows = n_k if keys_on_rows else n_q
    return _call(
        functools.partial(_bias_kernel, n_q=n_q, n_k=n_k, splits=splits, band=band,
                          keys_on_rows=keys_on_rows), "rel_bias",
        grid=(HEAD_PAIRS,),
        in_specs=[pl.BlockSpec((None, 2, BIAS_PERIOD), lambda p: (p, 0, 0))],
        out_specs=[pl.BlockSpec((None, 2 * n_rows, hi - lo), lambda p: (p, 0, 0)) for lo, hi in splits],
        out_shape=[jax.ShapeDtypeStruct((HEAD_PAIRS, 2 * n_rows, hi - lo), F32) for lo, hi in splits],
    )(gen)


def _band_prompt_kernel(x_ref, qt_ref, qm_ref, k0_ref, k1_ref, k2_ref, vt0_ref, vt1_ref, vt2_ref,
                        bias_ref, mk_ref, mv_ref, wout_ref, gpost_ref, o_ref):
    i = pl.program_id(0)
    n_k = KEY_BLOCKS * TILE
    w = lax.broadcasted_iota(jnp.int32, (n_k, LANES), 0)
    c = lax.broadcasted_iota(jnp.int32, (n_k, LANES), 1)
    pen = jnp.where((w < (KEY_BLOCKS - 1 - i) * TILE) & (c == 0), NEG, 0.0).astype(BF16)
    row = lax.broadcasted_iota(jnp.int32, (LANES, TILE), 0)
    one_hot_row = jnp.where(row == 0, 1.0, 0.0).astype(BF16)
    lo = row < HEAD_DIM
    outs = []
    for p in range(HEAD_PAIRS):
        k_ext = jnp.concatenate(
            [jnp.concatenate([k0_ref[p], k1_ref[p], k2_ref[p]], axis=0), pen], axis=1)
        vt = jnp.concatenate([vt0_ref[p], vt1_ref[p], vt2_ref[p]], axis=1)
        qt = qt_ref[p]
        zero = jnp.zeros_like(qt)
        for hh in range(2):
            qh = jnp.where(lo, qt, zero) if hh == 0 else jnp.where(lo, zero, qt)
            s = _dot(k_ext, jnp.concatenate([qh, one_hot_row], axis=0))
            vth = vt[hh * HEAD_DIM:(hh + 1) * HEAD_DIM, :]
            m = l = o = None
            for j in range(n_k // KEY_STEP):
                rows = slice(j * KEY_STEP, (j + 1) * KEY_STEP)
                sj = s[rows] + bias_ref[p, hh * n_k + j * KEY_STEP:hh * n_k + (j + 1) * KEY_STEP, :]
                mj = jnp.max(sj, axis=0, keepdims=True)
                if j == 0:
                    m = mj
                    e = jnp.exp(sj - m)
                    l = jnp.sum(e, axis=0, keepdims=True)
                    o = _dot(vth[:, rows], e.astype(BF16))
                else:
                    m_new = jnp.maximum(m, mj)
                    alpha = jnp.exp(m - m_new)
                    e = jnp.exp(sj - m_new)
                    l = l * alpha + jnp.sum(e, axis=0, keepdims=True)
                    o = o * alpha + _dot(vth[:, rows], e.astype(BF16))
                    m = m_new
            outs.append(o * (1.0 / l))
    band = jnp.concatenate(outs, axis=0).T.astype(BF16)
    mo = _mem_attend(qm_ref[...], mk_ref[0].astype(BF16), mv_ref[0].astype(BF16))
    cat = jnp.concatenate([band, mo.astype(BF16)], axis=-1)
    o_ref[...] = x_ref[...] + _rms(_dot(cat, wout_ref[...]), gpost_ref[...])


def _band_prompt(x, qt3, qm, k3, vt3, bias, mkt, mvt, wout, gpost):
    n = x.shape[0]
    blk = lambda i, j: jnp.maximum(i - (KEY_BLOCKS - 1) + j, 0)
    k_specs = [pl.BlockSpec((HEAD_PAIRS, TILE, LANES), functools.partial(lambda i, j: (0, blk(i, j), 0), j=j))
               for j in range(KEY_BLOCKS)]
    vt_specs = [pl.BlockSpec((HEAD_PAIRS, LANES, TILE), functools.partial(lambda i, j: (0, 0, blk(i, j)), j=j))
                for j in range(KEY_BLOCKS)]
    return _call(
        _band_prompt_kernel, "band_prompt",
        grid=(n // TILE,),
        in_specs=[pl.BlockSpec((TILE, D_MODEL), lambda i: (i, 0)),
                  pl.BlockSpec((HEAD_PAIRS, LANES, TILE), lambda i: (0, 0, i)),
                  pl.BlockSpec((TILE, MEM_W), lambda i: (i, 0)),
                  *k_specs, *vt_specs,
                  _const_spec((HEAD_PAIRS, 2 * KEY_BLOCKS * TILE, TILE)),
                  _mem_spec(mkt, 1, 1), _mem_spec(mvt, 1, 1),
                  _const_spec((B_W + MEM_W, D_MODEL)), _const_spec((1, D_MODEL))],
        out_specs=pl.BlockSpec((TILE, D_MODEL), lambda i: (i, 0)),
        out_shape=jax.ShapeDtypeStruct((n, D_MODEL), F32),
    )(x, qt3, qm, *[k3] * KEY_BLOCKS, *[vt3] * KEY_BLOCKS, bias, mkt, mvt, wout, gpost)


def _band_sample_kernel(x_ref, q_ref, qm_ref, kn_ref, vn_ref, ck_ref, cv_ref, bc_ref, bn_ref,
                        mk_ref, mv_ref, wout_ref, gpost_ref, o_ref, *, seqs, seq_len):
    lane = lax.broadcasted_iota(jnp.int32, (1, LANES), 1)
    lo = lane < HEAD_DIM
    rows_out = []
    for b in range(seqs):
        rows = slice(b * seq_len, (b + 1) * seq_len)
        band = []
        for p in range(HEAD_PAIRS):
            hd = slice(p * LANES, (p + 1) * LANES)
            qp = q_ref[p, rows, :]
            zero = jnp.zeros_like(qp)
            qs = jnp.concatenate([jnp.where(lo, qp, zero), jnp.where(lo, zero, qp)], axis=0)
            sc = _dot(qs, ck_ref[b, hd, :].astype(BF16)) + bc_ref[p]
            sn = _dot_nt(qs, kn_ref[p, rows, :]) + bn_ref[p]
            m = jnp.maximum(_lmax(sc), _lmax(sn))
            ec = jnp.exp(sc - m)
            en = jnp.exp(sn - m)
            o = (_dot_nt(ec.astype(BF16), cv_ref[b, hd, :].astype(BF16))
                 + _dot(en.astype(BF16), vn_ref[p, rows, :]))
            o = o * (1.0 / (_lsum(ec) + _lsum(en)))
            band.append(jnp.where(lo, o[:seq_len], o[seq_len:]).astype(BF16))
        mo = _mem_attend(qm_ref[rows, :], mk_ref[b].astype(BF16), mv_ref[b].astype(BF16))
        rows_out.append(jnp.concatenate(band + [mo.astype(BF16)], axis=-1))
    cat = jnp.concatenate(rows_out, axis=0)
    o_ref[...] = x_ref[...] + _rms(_dot(cat, wout_ref[...]), gpost_ref[...])


def _band_sample(x, q3, qm, kn3, vn3, ckt, cvt, bias_c, bias_n, mkt, mvt, wout, gpost, *, seqs=4):
    n = x.shape[0]
    n_seq, past = ckt.shape[0], ckt.shape[2]
    seq_len = n // n_seq
    rows = seqs * seq_len
    pair_rows_spec = pl.BlockSpec((HEAD_PAIRS, rows, LANES), lambda i: (0, i, 0))
    cache_spec = pl.BlockSpec((seqs, B_W, past), lambda i: (i, 0, 0))
    return _call(
        functools.partial(_band_sample_kernel, seqs=seqs, seq_len=seq_len), "band_sample",
        grid=(n_seq // seqs,),
        in_specs=[pl.BlockSpec((rows, D_MODEL), lambda i: (i, 0)),
                  pair_rows_spec,
                  pl.BlockSpec((rows, MEM_W), lambda i: (i, 0)),
                  pair_rows_spec, pair_rows_spec,
                  cache_spec, cache_spec,
                  _const_spec((HEAD_PAIRS, 2 * seq_len, past)),
                  _const_spec((HEAD_PAIRS, 2 * seq_len, seq_len)),
                  _mem_spec(mkt, seqs, 1), _mem_spec(mvt, seqs, 1),
                  _const_spec((B_W + MEM_W, D_MODEL)), _const_spec((1, D_MODEL))],
        out_specs=pl.BlockSpec((rows, D_MODEL), lambda i: (i, 0)),
        out_shape=jax.ShapeDtypeStruct((n, D_MODEL), F32),
    )(x, q3, qm, kn3, vn3, ckt, cvt, bias_c, bias_n, mkt, mvt, wout, gpost)


def _spatial_tile(w_s, b_s, period):
    tril = jnp.tril(jnp.ones((GM_CHUNK, GM_CHUNK), dtype=bool))
    w = jnp.where(tril, w_s, jnp.zeros((), w_s.dtype))[:, :period, :period]
    eye = jnp.eye(TILE // period, dtype=w.dtype)
    s_mat = jnp.einsum("ab,gts->gatbs", eye, w).reshape(GM_GROUPS, TILE, TILE)
    rows = jnp.tile(b_s[:, :period], (1, TILE // period))
    bs = jnp.repeat(rows.T, GM_GW, axis=1)
    return s_mat.astype(BF16), bs


def _heads_last(t, lead):
    pos = t.shape[-1]
    t = t.reshape(lead + (-1, HEAD_DIM, pos))
    nd = len(lead)
    return jnp.transpose(t, tuple(range(nd)) + (nd + 2, nd, nd + 1))


def _positions_last(c):
    nd = c.ndim
    t = jnp.transpose(c, tuple(range(nd - 3)) + (nd - 2, nd - 1, nd - 3))
    return t.reshape(c.shape[:-3] + (c.shape[-2] * c.shape[-1], c.shape[-3]))


def kernel(x_prompt, x_sample, cache_mem_k, cache_mem_v, cache_band_k, cache_band_v, mem_prompt,
           g_mix_pre, g_mix_post, g_ffn_pre, g_ffn_post, g_mem, w_mem_kv,
           w_in_a, g_gm_ln, b_gm_ln, w_spatial, b_spatial, w_out_a,
           g_kv, w_kv, w_in_b, rel_bias, w_out_b, w_ff1, w_ff2):
    seq = x_prompt.shape[1]
    n_seq, seq_len = x_sample.shape[0], x_sample.shape[1]
    past = cache_band_k.shape[1]
    vec = lambda a: a.reshape(1, -1)
    stack = lambda a: a.reshape(DEPTH, 1, -1)

    win_a = w_in_a[0].astype(BF16)
    wout_a = w_out_a[0].astype(BF16)
    wk = w_kv[:, :B_W].astype(BF16)
    wv = w_kv[:, B_W:].astype(BF16)
    wkt, wvt = wk.T, wv.T
    win_b = w_in_b[0].astype(BF16)
    wqt, wqm = win_b[:, :B_W].T, win_b[:, B_W:]
    wout_b = w_out_b[0].astype(BF16)
    w1 = w_ff1.astype(BF16)
    w2 = w_ff2.astype(BF16)
    ln_g, ln_b = vec(g_gm_ln[0]), vec(b_gm_ln[0])
    gkv, gpre_b = vec(g_kv), vec(g_mix_pre[1])
    gf_pre, gf_post = stack(g_ffn_pre), stack(g_ffn_post)

    def ffn(x, l):
        return _ffn(x, gf_pre, gf_post, w1, w2, l)

    mem_kt, mem_vt = _memkv(mem_prompt[0], g_mem, jnp.swapaxes(w_mem_kv, 1, 2).astype(BF16))
    s_p, bs_p = _spatial_tile(w_spatial[0], b_spatial[0], GM_CHUNK)
    x = x_prompt[0]
    x, = _mixer_a(x, vec(g_mix_pre[0]), vec(g_mix_post[0]), win_a, ln_g, ln_b, s_p, bs_p,
                  mem_kt, mem_vt, wout_a, rows_per_mem=TILE, emit_v=False)
    x = ffn(x, 0)
    qt3, qm, k3, vt3 = _proj_b(x, gkv, gpre_b, (wqt, wqm, wk, wvt), mode="prompt", rows=512)
    n_keep = min(BAND_PAST, seq)
    kt_tail, vt_tail = _proj_b(x, gkv, gpre_b, (wkt, wvt), mode="tail", rows=TILE,
                               row_offset=seq - n_keep, n_rows=n_keep)
    n_k = BAND_PAST + TILE
    bias_p, = _rel_bias_tables(rel_bias[0], TILE, n_k, ((0, TILE),), band=True, keys_on_rows=True)
    x = _band_prompt(x, qt3, qm, k3, vt3, bias_p, mem_kt, mem_vt, wout_b, vec(g_mix_post[1]))
    y_prompt = ffn(x, 1)[None]

    s_s, bs_s = _spatial_tile(w_spatial[0], b_spatial[0], seq_len)
    xs = x_sample.reshape(n_seq * seq_len, D_MODEL)
    cmkt, cmvt = _positions_last(cache_mem_k), _positions_last(cache_mem_v)
    xs, v_rows = _mixer_a(xs, vec(g_mix_pre[0]), vec(g_mix_post[0]), win_a, ln_g, ln_b, s_s, bs_s,
                          cmkt, cmvt, wout_a, rows_per_mem=seq_len, emit_v=True)
    xs = ffn(xs, 0)
    q3s, qms, kn3, vn3, k_new, v_new = _proj_b(xs, gkv, gpre_b, (win_b, wk, wv), mode="sample",
                                               rows=512, seq_len=seq_len)
    bias_c, bias_n = _rel_bias_tables(rel_bias[0], seq_len, past + seq_len,
                                      ((0, past), (past, past + seq_len)), band=False, keys_on_rows=False)
    xs = _band_sample(xs, q3s, qms, kn3, vn3, _positions_last(cache_band_k), _positions_last(cache_band_v),
                      bias_c, bias_n, cmkt, cmvt, wout_b, vec(g_mix_post[1]))
    y_sample = ffn(xs, 1).reshape(n_seq, seq_len, D_MODEL)

    return (y_prompt, y_sample,
            _heads_last(mem_kt, (DEPTH, 1)), _heads_last(mem_vt, (DEPTH, 1)),
            _heads_last(kt_tail, (1,)), _heads_last(vt_tail, (1,)),
            v_rows.reshape(1, n_seq, seq_len, GM_W),
            jnp.swapaxes(k_new, 1, 2), jnp.swapaxes(v_new, 1, 2))
```

```python
import functools

import jax
import jax.numpy as jnp
import numpy as np
from jax import lax
from jax.experimental import pallas as pl
from jax.experimental.pallas import tpu as pltpu

D_MODEL = 1024
DEPTH = 2
CHUNK = 64
HEAD_DIM = 64
GM_CHUNK = 128
GM_GROUPS = 4
GM_W = 768
GM_GW = GM_W // GM_GROUPS
MEM_LEN = 256
MEM_HEADS = 4
MEM_W = MEM_HEADS * HEAD_DIM
B_HEADS = 12
B_W = B_HEADS * HEAD_DIM
BAND_PAST = 512
REL_CLIP = 128
D_FF = 4 * D_MODEL
EPS = 1e-6

LANES = 128
HEAD_PAIRS = B_W // LANES
Q_SCALE = HEAD_DIM ** -0.5
NEG = -1e30
TILE = 256
KEY_BLOCKS = BAND_PAST // TILE + 1
KEY_STEP = 128
GROUP_PAIRS = 2
BIAS_PERIOD = 1024
VMEM_LIMIT = 56 * 1024 * 1024

BF16 = jnp.bfloat16
F32 = jnp.float32


def _dot(a, b):
    return jnp.dot(a, b, preferred_element_type=F32)


def _dot_nt(a, b):
    return lax.dot_general(a, b, (((1,), (1,)), ((), ())), preferred_element_type=F32)


def _rms(x, g):
    ms = jnp.mean(x * x, axis=-1, keepdims=True)
    return x * lax.rsqrt(ms + EPS) * g


def _lsum(a):
    return jnp.sum(a, axis=-1, keepdims=True)


def _lmax(a):
    return jnp.max(a, axis=-1, keepdims=True)


def _const_spec(shape):
    nd = len(shape)
    return pl.BlockSpec(shape, lambda *_: (0,) * nd, pipeline_mode=pl.Buffered(1))


def _layer_spec(shape, layer):
    nd = len(shape)
    return pl.BlockSpec((None,) + shape, lambda *_: (layer,) + (0,) * nd, pipeline_mode=pl.Buffered(1))


def _call(body, name, flags=None, **kw):
    return pl.pallas_call(
        body, name=name,
        compiler_params=pltpu.CompilerParams(dimension_semantics=("parallel",),
                                             vmem_limit_bytes=VMEM_LIMIT, flags=flags), **kw)


def _memkv_kernel(mem_ref, g_ref, wt_ref, kt_ref, vt_ref):
    ht = _dot_nt(wt_ref[...], _rms(mem_ref[...], g_ref[...]).astype(BF16))
    kt_ref[0] = ht[:MEM_W]
    vt_ref[0] = ht[MEM_W:]


def _memkv(mem, g_mem, w_mem_kv_t):
    out = jax.ShapeDtypeStruct((DEPTH, 1, MEM_W, MEM_LEN), F32)
    return _call(
        _memkv_kernel, "mem_kv",
        grid=(DEPTH,),
        in_specs=[
            pl.BlockSpec((MEM_LEN, D_MODEL), lambda l: (0, 0)),
            pl.BlockSpec((None, 1, D_MODEL), lambda l: (l, 0, 0)),
            pl.BlockSpec((None, 2 * MEM_W, D_MODEL), lambda l: (l, 0, 0)),
        ],
        out_specs=[pl.BlockSpec((None, 1, MEM_W, MEM_LEN), lambda l: (l, 0, 0, 0))] * 2,
        out_shape=[out, out],
    )(mem, g_mem.reshape(DEPTH, 1, D_MODEL), w_mem_kv_t)


def _mem_attend(qb, kt, vt):
    r = qb.shape[0]
    lane = lax.broadcasted_iota(jnp.int32, (1, MEM_W), 1)
    masks = [(lane >= h * HEAD_DIM) & (lane < (h + 1) * HEAD_DIM) for h in range(MEM_HEADS)]
    qs = jnp.concatenate([jnp.where(m, qb, jnp.zeros_like(qb)) for m in masks], axis=0)
    s = _dot(qs, kt)
    e = jnp.exp(s - _lmax(s))
    pv = _dot_nt(e.astype(BF16), vt) * (1.0 / _lsum(e))
    out = jnp.where(masks[0], pv[:r], 0.0)
    for h in range(1, MEM_HEADS):
        out = out + jnp.where(masks[h], pv[h * r:(h + 1) * r], 0.0)
    return out


def _gelu(x):
    return jax.nn.gelu(x, approximate=True)


def _mixer_a_kernel(x_ref, gpre_ref, gpost_ref, win_ref, gln_ref, bln_ref, s_ref, bs_ref,
                    mk_ref, mv_ref, wout_ref, o_ref, *v_out, rows_per_mem):
    x = x_ref[...]
    z = _dot(_rms(x, gpre_ref[...]).astype(BF16), win_ref[...])
    nt = GM_W // LANES
    u = [_gelu(z[:, j * LANES:(j + 1) * LANES]) for j in range(nt)]
    t = [_gelu(z[:, GM_W + j * LANES:GM_W + (j + 1) * LANES]) for j in range(nt)]

    lane = lax.broadcasted_iota(jnp.int32, (1, LANES), 1)
    lo = lane < (GM_GW - LANES)
    inv = 1.0 / GM_GW

    def group_stat(a):
        s0 = _lsum(a[0] + jnp.where(lo, a[1], 0.0)) * inv
        s1 = _lsum(jnp.where(lo, 0.0, a[1]) + a[2]) * inv
        s2 = _lsum(a[3] + jnp.where(lo, a[4], 0.0)) * inv
        s3 = _lsum(jnp.where(lo, 0.0, a[4]) + a[5]) * inv
        return [s0, jnp.where(lo, s0, s1), s1, s2, jnp.where(lo, s2, s3), s3]

    mu = group_stat(t)
    c = [t[j] - mu[j] for j in range(nt)]
    var = group_stat([cj * cj for cj in c])
    gln = gln_ref[...]
    bln = bln_ref[...]
    vn = [c[j] * lax.rsqrt(var[j] + EPS) * gln[:, j * LANES:(j + 1) * LANES]
          + bln[:, j * LANES:(j + 1) * LANES] for j in range(nt)]
    if v_out:
        v_out[0][...] = jnp.concatenate(vn, axis=-1)

    vb = [a.astype(BF16) for a in vn]
    win = [(0, 1), (1, 2), (3, 4), (4, 5)]
    m = [_dot(s_ref[g], jnp.concatenate([vb[a], vb[b]], axis=-1)) for g, (a, b) in enumerate(win)]
    mixed = [m[0][:, :LANES], jnp.where(lo, m[0][:, LANES:], m[1][:, :LANES]), m[1][:, LANES:],
             m[2][:, :LANES], jnp.where(lo, m[2][:, LANES:], m[3][:, :LANES]), m[3][:, LANES:]]
    bs = bs_ref[...]
    gm = [u[j] * (mixed[j] + bs[:, j * LANES:(j + 1) * LANES]) for j in range(nt)]

    qm = (z[:, 2 * GM_W:] * Q_SCALE).astype(BF16)
    mo = []
    for b in range(TILE // rows_per_mem):
        rows = slice(b * rows_per_mem, (b + 1) * rows_per_mem)
        mo.append(_mem_attend(qm[rows], mk_ref[b].astype(BF16), mv_ref[b].astype(BF16)))
    mo = mo[0] if len(mo) == 1 else jnp.concatenate(mo, axis=0)

    cat = jnp.concatenate([a.astype(BF16) for a in gm] + [mo.astype(BF16)], axis=-1)
    o_ref[...] = x + _rms(_dot(cat, wout_ref[...]), gpost_ref[...])


def _mem_spec(mem, seqs, layer):
    if mem.shape[1] == 1:
        return pl.BlockSpec((None, 1, MEM_W, MEM_LEN), lambda i: (layer, 0, 0, 0))
    return pl.BlockSpec((None, seqs, MEM_W, MEM_LEN), lambda i: (layer, i, 0, 0))


def _mixer_a(x, gpre, gpost, win, gln, bln, s_mat, bs, mkt, mvt, wout, *, rows_per_mem, emit_v):
    n = x.shape[0]
    seqs = TILE // rows_per_mem
    row = lambda w: pl.BlockSpec((TILE, w), lambda i: (i, 0))
    out_shape = [jax.ShapeDtypeStruct((n, D_MODEL), F32)]
    out_specs = [row(D_MODEL)]
    if emit_v:
        out_shape.append(jax.ShapeDtypeStruct((n, GM_W), F32))
        out_specs.append(row(GM_W))
    return _call(
        functools.partial(_mixer_a_kernel, rows_per_mem=rows_per_mem), "mixer_a",
        grid=(n // TILE,),
        in_specs=[
            row(D_MODEL),
            _const_spec((1, D_MODEL)), _const_spec((1, D_MODEL)),
            _const_spec((D_MODEL, 2 * GM_W + MEM_W)),
            _const_spec((1, GM_W)), _const_spec((1, GM_W)),
            _const_spec((GM_GROUPS, TILE, TILE)), _const_spec((TILE, GM_W)),
            _mem_spec(mkt, seqs, 0), _mem_spec(mvt, seqs, 0),
            _const_spec((GM_W + MEM_W, D_MODEL)),
        ],
        out_specs=out_specs,
        out_shape=out_shape,
    )(x, gpre, gpost, win, gln, bln, s_mat, bs, mkt, mvt, wout)


def _ffn_kernel(x_ref, gpre_ref, gpost_ref, w1_ref, w2_ref, o_ref, *, ff_chunk):
    x = x_ref[...]
    xn = _rms(x, gpre_ref[...]).astype(BF16)
    acc = jnp.zeros(x.shape, F32)
    for c in range(D_FF // ff_chunk):
        h = _dot(xn, w1_ref[:, c * ff_chunk:(c + 1) * ff_chunk])
        h = jnp.square(jnp.maximum(h, 0.0)).astype(BF16)
        acc = acc + _dot(h, w2_ref[c * ff_chunk:(c + 1) * ff_chunk, :])
    o_ref[...] = x + _rms(acc, gpost_ref[...])


def _ffn(x, gpre, gpost, w1, w2, layer, *, rows=512, ff_chunk=1024):
    n = x.shape[0]
    row = pl.BlockSpec((rows, D_MODEL), lambda i: (i, 0))
    return _call(
        functools.partial(_ffn_kernel, ff_chunk=ff_chunk), "ffn",
        grid=(n // rows,),
        in_specs=[row, _layer_spec((1, D_MODEL), layer), _layer_spec((1, D_MODEL), layer),
                  _layer_spec((D_MODEL, D_FF), layer), _layer_spec((D_FF, D_MODEL), layer)],
        out_specs=row,
        out_shape=jax.ShapeDtypeStruct((n, D_MODEL), F32),
    )(x, gpre, gpost, w1, w2)


def _proj_b_kernel(x_ref, gkv_ref, gpre_ref, *refs, mode, seq_len):
    x = x_ref[...]
    rows = x.shape[0]
    xh = x * lax.rsqrt(jnp.mean(x * x, axis=-1, keepdims=True) + EPS)
    xkv = (xh * gkv_ref[...]).astype(BF16)
    if mode == "tail":
        wkt_ref, wvt_ref, kt_ref, vt_ref = refs
        kt_ref[...] = _dot_nt(wkt_ref[...], xkv)
        vt_ref[...] = _dot_nt(wvt_ref[...], xkv)
        return
    xq = (xh * gpre_ref[...]).astype(BF16)
    if mode == "prompt":
        wqt_ref, wqm_ref, wk_ref, wvt_ref, qt_ref, qm_ref, k_ref, vt_ref = refs
        qt = _dot_nt(wqt_ref[...], xq) * Q_SCALE
        vt = _dot_nt(wvt_ref[...], xkv)
        k = _dot(xkv, wk_ref[...])
        qm_ref[...] = (_dot(xq, wqm_ref[...]) * Q_SCALE).astype(BF16)
        for p in range(HEAD_PAIRS):
            qt_ref[p] = qt[p * LANES:(p + 1) * LANES, :].astype(BF16)
            vt_ref[p] = vt[p * LANES:(p + 1) * LANES, :].astype(BF16)
            k_ref[p] = k[:, p * LANES:(p + 1) * LANES].astype(BF16)
        return
    win_ref, wk_ref, wv_ref, q_ref, qm_ref, k_ref, v_ref, k4_ref, v4_ref = refs
    z = _dot(xq, win_ref[...]) * Q_SCALE
    qm_ref[...] = z[:, B_W:].astype(BF16)
    k = _dot(xkv, wk_ref[...])
    v = _dot(xkv, wv_ref[...])
    for p in range(HEAD_PAIRS):
        cols = slice(p * LANES, (p + 1) * LANES)
        q_ref[p] = z[:, cols].astype(BF16)
        k_ref[p] = k[:, cols].astype(BF16)
        v_ref[p] = v[:, cols].astype(BF16)
    for b in range(rows // seq_len):
        for h in range(B_HEADS):
            r, c = slice(b * seq_len, (b + 1) * seq_len), slice(h * HEAD_DIM, (h + 1) * HEAD_DIM)
            k4_ref[b, h] = k[r, c]
            v4_ref[b, h] = v[r, c]


def _proj_b(x, gkv, gpre, weights, *, mode, rows, row_offset=0, n_rows=None, seq_len=None):
    n = x.shape[0] if n_rows is None else n_rows
    off = row_offset // rows
    pair_rows = jax.ShapeDtypeStruct((HEAD_PAIRS, n, LANES), BF16)
    pair_rows_spec = pl.BlockSpec((HEAD_PAIRS, rows, LANES), lambda i: (0, i, 0))
    pair_cols = jax.ShapeDtypeStruct((HEAD_PAIRS, LANES, n), BF16)
    pair_cols_spec = pl.BlockSpec((HEAD_PAIRS, LANES, rows), lambda i: (0, 0, i))
    qm = jax.ShapeDtypeStruct((n, MEM_W), BF16)
    qm_spec = pl.BlockSpec((rows, MEM_W), lambda i: (i, 0))
    if mode == "tail":
        out_shape = [jax.ShapeDtypeStruct((B_W, n), F32)] * 2
        out_specs = [pl.BlockSpec((B_W, rows), lambda i: (0, i))] * 2
    elif mode == "prompt":
        out_shape = [pair_cols, qm, pair_rows, pair_cols]
        out_specs = [pair_cols_spec, qm_spec, pair_rows_spec, pair_cols_spec]
    else:
        seqs = rows // seq_len
        per_head = jax.ShapeDtypeStruct((n // seq_len, B_HEADS, seq_len, HEAD_DIM), F32)
        per_head_spec = pl.BlockSpec((seqs, B_HEADS, seq_len, HEAD_DIM), lambda i: (i, 0, 0, 0))
        out_shape = [pair_rows, qm, pair_rows, pair_rows, per_head, per_head]
        out_specs = [pair_rows_spec, qm_spec, pair_rows_spec, pair_rows_spec, per_head_spec, per_head_spec]
    return _call(
        functools.partial(_proj_b_kernel, mode=mode, seq_len=seq_len), "proj_b_" + mode,
        grid=(n // rows,),
        in_specs=[pl.BlockSpec((rows, D_MODEL), lambda i: (i + off, 0)),
                  _const_spec((1, D_MODEL)), _const_spec((1, D_MODEL))]
                 + [_const_spec(w.shape) for w in weights],
        out_specs=out_specs,
        out_shape=out_shape,
    )(x, gkv, gpre, *weights)


def _bias_kernel(g_ref, *o_refs, n_q, n_k, splits, band, keys_on_rows):
    n_rows, shift = (n_k, n_q) if keys_on_rows else (n_q, BIAS_PERIOD - (n_q - 1))
    for hh in range(2):
        x = jnp.broadcast_to(g_ref[hh:hh + 1, :], (n_rows, BIAS_PERIOD))
        t = pltpu.roll(x, shift, 1, stride=1, stride_axis=0)
        if band:
            a = lax.broadcasted_iota(jnp.int32, (n_rows, BIAS_PERIOD), 0)
            b = lax.broadcasted_iota(jnp.int32, (n_rows, BIAS_PERIOD), 1)
            r, w = (b, a) if keys_on_rows else (a, b)
            j = w - (r - (r & (CHUNK - 1)))
            t = jnp.where((j >= 0) & (j < BAND_PAST + CHUNK), t, NEG)
        for o_ref, (lo, hi) in zip(o_refs, splits):
            o_ref[hh * n_rows:(hh + 1) * n_rows, :] = t[:, lo:hi]


def _rel_bias_tables(rel_bias, n_q, n_k, splits, *, band, keys_on_rows):
    rel_bias = rel_bias - rel_bias[:, -1:]
    c0 = n_k - 1
    far = jnp.broadcast_to(rel_bias[:, -1:], (B_HEADS, c0 - REL_CLIP))
    near = jnp.broadcast_to(rel_bias[:, :1], (B_HEADS, BIAS_PERIOD - (c0 - REL_CLIP) - (2 * REL_CLIP + 1)))
    gen = [near, rel_bias, far] if keys_on_rows else [far, rel_bias[:, ::-1], near]
    gen = jnp.concatenate(gen, axis=1).reshape(HEAD_PAIRS, 2, BIAS_PERIOD)
    n_rows = n_k if keys_on_rows else n_q
    return _call(
        functools.partial(_bias_kernel, n_q=n_q, n_k=n_k, splits=splits, band=band,
                          keys_on_rows=keys_on_rows), "rel_bias",
        grid=(HEAD_PAIRS,),
        in_specs=[pl.BlockSpec((None, 2, BIAS_PERIOD), lambda p: (p, 0, 0))],
        out_specs=[pl.BlockSpec((None, 2 * n_rows, hi - lo), lambda p: (p, 0, 0)) for lo, hi in splits],
        out_shape=[jax.ShapeDtypeStruct((HEAD_PAIRS, 2 * n_rows, hi - lo), F32) for lo, hi in splits],
    )(gen)


def _band_block_kinds():
    n_k = KEY_BLOCKS * TILE
    r = np.arange(TILE)[None, :]
    w = np.arange(n_k)[:, None]
    j = w - CHUNK * (r // CHUNK)
    ok = (j >= 0) & (j < BAND_PAST + CHUNK)
    plain = ok & (BAND_PAST + r - w >= REL_CLIP)
    kinds = []
    for a in range(n_k // KEY_STEP):
        rows = slice(a * KEY_STEP, (a + 1) * KEY_STEP)
        blocks = [(rows, slice(b * LANES, (b + 1) * LANES)) for b in range(TILE // LANES)]
        kinds.append(["skip" if not ok[blk].any() else "plain" if plain[blk].all() else "biased"
                      for blk in blocks])
    return kinds


def _band_prompt_kernel(x_ref, qt_ref, qm_ref, k0_ref, k1_ref, k2_ref, vt0_ref, vt1_ref, vt2_ref,
                        bias_ref, mk_ref, mv_ref, wout_ref, gpost_ref, o_ref):
    i = pl.program_id(0)
    n_k = KEY_BLOCKS * TILE
    kinds = _band_block_kinds()
    w = lax.broadcasted_iota(jnp.int32, (n_k, LANES), 0)
    c = lax.broadcasted_iota(jnp.int32, (n_k, LANES), 1)
    pen = jnp.where((w < (KEY_BLOCKS - 1 - i) * TILE) & (c == 0), NEG, 0.0).astype(BF16)
    row = lax.broadcasted_iota(jnp.int32, (LANES, TILE), 0)
    one_hot_row = jnp.where(row == 0, 1.0, 0.0).astype(BF16)
    lo = row < HEAD_DIM

    def key_step(a, s, vth, bias_rows, state):
        rows = slice(a * KEY_STEP, (a + 1) * KEY_STEP)
        es, alphas = [], []
        for b, kind in enumerate(kinds[a]):
            cols = slice(b * LANES, (b + 1) * LANES)
            if kind == "skip":
                es.append(jnp.zeros((KEY_STEP, LANES), F32))
                alphas.append(None)
                continue
            sj = s[rows, cols]
            if kind == "biased":
                sj = sj + bias_rows(rows, cols)
            mj = jnp.max(sj, axis=0, keepdims=True)
            if state[b] is None:
                e = jnp.exp(sj - mj)
                state[b] = [mj, jnp.sum(e, axis=0, keepdims=True), None]
                alphas.append(None)
            else:
                m_old, l_old, _ = state[b]
                m_new = jnp.maximum(m_old, mj)
                alpha = jnp.exp(m_old - m_new)
                e = jnp.exp(sj - m_new)
                state[b][:2] = [m_new, l_old * alpha + jnp.sum(e, axis=0, keepdims=True)]
                alphas.append(alpha)
            es.append(e)
        pv = _dot(vth[:, rows], jnp.concatenate(es, axis=1).astype(BF16))
        for b, kind in enumerate(kinds[a]):
            if kind != "skip":
                pv_b = pv[:, b * LANES:(b + 1) * LANES]
                state[b][2] = pv_b if alphas[b] is None else state[b][2] * alphas[b] + pv_b

    outs = []
    for p0 in range(0, HEAD_PAIRS, GROUP_PAIRS):
        heads = []
        for p in range(p0, p0 + GROUP_PAIRS):
            k_ext = jnp.concatenate(
                [jnp.concatenate([k0_ref[p], k1_ref[p], k2_ref[p]], axis=0), pen], axis=1)
            vt = jnp.concatenate([vt0_ref[p], vt1_ref[p], vt2_ref[p]], axis=1)
            qt = qt_ref[p]
            zero = jnp.zeros_like(qt)
            for hh in range(2):
                qh = jnp.where(lo, qt, zero) if hh == 0 else jnp.where(lo, zero, qt)
                s = _dot(k_ext, jnp.concatenate([qh, one_hot_row], axis=0))
                vth = vt[hh * HEAD_DIM:(hh + 1) * HEAD_DIM, :]
                bias_rows = functools.partial(
                    lambda rows, cols, p, base: bias_ref[p, base + rows.start:base + rows.stop, cols],
                    p=p, base=hh * n_k)
                heads.append((s, vth, bias_rows, [None] * len(kinds[0])))
        for a in range(len(kinds)):
            for head in heads:
                key_step(a, *head)
        outs += [jnp.concatenate([o * (1.0 / l) for _, l, o in state], axis=1) for *_, state in heads]
    band = jnp.concatenate(outs, axis=0).T.astype(BF16)
    mo = _mem_attend(qm_ref[...], mk_ref[0].astype(BF16), mv_ref[0].astype(BF16))
    cat = jnp.concatenate([band, mo.astype(BF16)], axis=-1)
    o_ref[...] = x_ref[...] + _rms(_dot(cat, wout_ref[...]), gpost_ref[...])


def _band_prompt(x, qt3, qm, k3, vt3, bias, mkt, mvt, wout, gpost):
    n = x.shape[0]
    blk = lambda i, j: jnp.maximum(i - (KEY_BLOCKS - 1) + j, 0)
    k_specs = [pl.BlockSpec((HEAD_PAIRS, TILE, LANES), functools.partial(lambda i, j: (0, blk(i, j), 0), j=j))
               for j in range(KEY_BLOCKS)]
    vt_specs = [pl.BlockSpec((HEAD_PAIRS, LANES, TILE), functools.partial(lambda i, j: (0, 0, blk(i, j)), j=j))
                for j in range(KEY_BLOCKS)]
    return _call(
        _band_prompt_kernel, "band_prompt",
        grid=(n // TILE,),
        in_specs=[pl.BlockSpec((TILE, D_MODEL), lambda i: (i, 0)),
                  pl.BlockSpec((HEAD_PAIRS, LANES, TILE), lambda i: (0, 0, i)),
                  pl.BlockSpec((TILE, MEM_W), lambda i: (i, 0)),
                  *k_specs, *vt_specs,
                  _const_spec((HEAD_PAIRS, 2 * KEY_BLOCKS * TILE, TILE)),
                  _mem_spec(mkt, 1, 1), _mem_spec(mvt, 1, 1),
                  _const_spec((B_W + MEM_W, D_MODEL)), _const_spec((1, D_MODEL))],
        out_specs=pl.BlockSpec((TILE, D_MODEL), lambda i: (i, 0)),
        out_shape=jax.ShapeDtypeStruct((n, D_MODEL), F32),
    )(x, qt3, qm, *[k3] * KEY_BLOCKS, *[vt3] * KEY_BLOCKS, bias, mkt, mvt, wout, gpost)


def _band_sample_kernel(x_ref, q_ref, qm_ref, kn_ref, vn_ref, ck_ref, cv_ref, bc_ref, bn_ref,
                        mk_ref, mv_ref, wout_ref, gpost_ref, o_ref, *, seqs, seq_len):
    lane = lax.broadcasted_iota(jnp.int32, (1, LANES), 1)
    lo = lane < HEAD_DIM
    rows_out = []
    for b in range(seqs):
        rows = slice(b * seq_len, (b + 1) * seq_len)
        band = []
        for p in range(HEAD_PAIRS):
            hd = slice(p * LANES, (p + 1) * LANES)
            qp = q_ref[p, rows, :]
            zero = jnp.zeros_like(qp)
            qs = jnp.concatenate([jnp.where(lo, qp, zero), jnp.where(lo, zero, qp)], axis=0)
            sc = _dot(qs, ck_ref[b, hd, :].astype(BF16)) + bc_ref[p]
            sn = _dot_nt(qs, kn_ref[p, rows, :]) + bn_ref[p]
            m = jnp.maximum(_lmax(sc), _lmax(sn))
            ec = jnp.exp(sc - m)
            en = jnp.exp(sn - m)
            o = (_dot_nt(ec.astype(BF16), cv_ref[b, hd, :].astype(BF16))
                 + _dot(en.astype(BF16), vn_ref[p, rows, :]))
            o = o * (1.0 / (_lsum(ec) + _lsum(en)))
            band.append(jnp.where(lo, o[:seq_len], o[seq_len:]).astype(BF16))
        mo = _mem_attend(qm_ref[rows, :], mk_ref[b].astype(BF16), mv_ref[b].astype(BF16))
        rows_out.append(jnp.concatenate(band + [mo.astype(BF16)], axis=-1))
    cat = jnp.concatenate(rows_out, axis=0)
    o_ref[...] = x_ref[...] + _rms(_dot(cat, wout_ref[...]), gpost_ref[...])


def _band_sample(x, q3, qm, kn3, vn3, ckt, cvt, bias_c, bias_n, mkt, mvt, wout, gpost, *, seqs=4):
    n = x.shape[0]
    n_seq, past = ckt.shape[0], ckt.shape[2]
    seq_len = n // n_seq
    rows = seqs * seq_len
    pair_rows_spec = pl.BlockSpec((HEAD_PAIRS, rows, LANES), lambda i: (0, i, 0))
    cache_spec = pl.BlockSpec((seqs, B_W, past), lambda i: (i, 0, 0))
    return _call(
        functools.partial(_band_sample_kernel, seqs=seqs, seq_len=seq_len), "band_sample",
        grid=(n_seq // seqs,),
        in_specs=[pl.BlockSpec((rows, D_MODEL), lambda i: (i, 0)),
                  pair_rows_spec,
                  pl.BlockSpec((rows, MEM_W), lambda i: (i, 0)),
                  pair_rows_spec, pair_rows_spec,
                  cache_spec, cache_spec,
                  _const_spec((HEAD_PAIRS, 2 * seq_len, past)),
                  _const_spec((HEAD_PAIRS, 2 * seq_len, seq_len)),
                  _mem_spec(mkt, seqs, 1), _mem_spec(mvt, seqs, 1),
                  _const_spec((B_W + MEM_W, D_MODEL)), _const_spec((1, D_MODEL))],
        out_specs=pl.BlockSpec((rows, D_MODEL), lambda i: (i, 0)),
        out_shape=jax.ShapeDtypeStruct((n, D_MODEL), F32),
    )(x, q3, qm, kn3, vn3, ckt, cvt, bias_c, bias_n, mkt, mvt, wout, gpost)


def _spatial_tile(w_s, b_s, period):
    tril = jnp.tril(jnp.ones((GM_CHUNK, GM_CHUNK), dtype=bool))
    w = jnp.where(tril, w_s, jnp.zeros((), w_s.dtype))[:, :period, :period]
    eye = jnp.eye(TILE // period, dtype=w.dtype)
    s_mat = jnp.einsum("ab,gts->gatbs", eye, w).reshape(GM_GROUPS, TILE, TILE)
    rows = jnp.tile(b_s[:, :period], (1, TILE // period))
    bs = jnp.repeat(rows.T, GM_GW, axis=1)
    return s_mat.astype(BF16), bs


def _heads_last(t, lead):
    pos = t.shape[-1]
    t = t.reshape(lead + (-1, HEAD_DIM, pos))
    nd = len(lead)
    return jnp.transpose(t, tuple(range(nd)) + (nd + 2, nd, nd + 1))


def _positions_last(c):
    nd = c.ndim
    t = jnp.transpose(c, tuple(range(nd - 3)) + (nd - 2, nd - 1, nd - 3))
    return t.reshape(c.shape[:-3] + (c.shape[-2] * c.shape[-1], c.shape[-3]))


def kernel(x_prompt, x_sample, cache_mem_k, cache_mem_v, cache_band_k, cache_band_v, mem_prompt,
           g_mix_pre, g_mix_post, g_ffn_pre, g_ffn_post, g_mem, w_mem_kv,
           w_in_a, g_gm_ln, b_gm_ln, w_spatial, b_spatial, w_out_a,
           g_kv, w_kv, w_in_b, rel_bias, w_out_b, w_ff1, w_ff2):
    seq = x_prompt.shape[1]
    n_seq, seq_len = x_sample.shape[0], x_sample.shape[1]
    past = cache_band_k.shape[1]
    vec = lambda a: a.reshape(1, -1)
    stack = lambda a: a.reshape(DEPTH, 1, -1)

    win_a = w_in_a[0].astype(BF16)
    wout_a = w_out_a[0].astype(BF16)
    wk = w_kv[:, :B_W].astype(BF16)
    wv = w_kv[:, B_W:].astype(BF16)
    wkt, wvt = wk.T, wv.T
    win_b = w_in_b[0].astype(BF16)
    wqt, wqm = win_b[:, :B_W].T, win_b[:, B_W:]
    wout_b = w_out_b[0].astype(BF16)
    w1 = w_ff1.astype(BF16)
    w2 = w_ff2.astype(BF16)
    ln_g, ln_b = vec(g_gm_ln[0]), vec(b_gm_ln[0])
    gkv, gpre_b = vec(g_kv), vec(g_mix_pre[1])
    gf_pre, gf_post = stack(g_ffn_pre), stack(g_ffn_post)

    def ffn(x, l):
        return _ffn(x, gf_pre, gf_post, w1, w2, l)

    mem_kt, mem_vt = _memkv(mem_prompt[0], g_mem, jnp.swapaxes(w_mem_kv, 1, 2).astype(BF16))
    s_p, bs_p = _spatial_tile(w_spatial[0], b_spatial[0], GM_CHUNK)
    x = x_prompt[0]
    x, = _mixer_a(x, vec(g_mix_pre[0]), vec(g_mix_post[0]), win_a, ln_g, ln_b, s_p, bs_p,
                  mem_kt, mem_vt, wout_a, rows_per_mem=TILE, emit_v=False)
    x = ffn(x, 0)
    qt3, qm, k3, vt3 = _proj_b(x, gkv, gpre_b, (wqt, wqm, wk, wvt), mode="prompt", rows=512)
    n_keep = min(BAND_PAST, seq)
    kt_tail, vt_tail = _proj_b(x, gkv, gpre_b, (wkt, wvt), mode="tail", rows=TILE,
                               row_offset=seq - n_keep, n_rows=n_keep)
    n_k = BAND_PAST + TILE
    bias_p, = _rel_bias_tables(rel_bias[0], TILE, n_k, ((0, TILE),), band=True, keys_on_rows=True)
    x = _band_prompt(x, qt3, qm, k3, vt3, bias_p, mem_kt, mem_vt, wout_b, vec(g_mix_post[1]))
    y_prompt = ffn(x, 1)[None]

    s_s, bs_s = _spatial_tile(w_spatial[0], b_spatial[0], seq_len)
    xs = x_sample.reshape(n_seq * seq_len, D_MODEL)
    cmkt, cmvt = _positions_last(cache_mem_k), _positions_last(cache_mem_v)
    xs, v_rows = _mixer_a(xs, vec(g_mix_pre[0]), vec(g_mix_post[0]), win_a, ln_g, ln_b, s_s, bs_s,
                          cmkt, cmvt, wout_a, rows_per_mem=seq_len, emit_v=True)
    xs = ffn(xs, 0)
    q3s, qms, kn3, vn3, k_new, v_new = _proj_b(xs, gkv, gpre_b, (win_b, wk, wv), mode="sample",
                                               rows=512, seq_len=seq_len)
    bias_c, bias_n = _rel_bias_tables(rel_bias[0], seq_len, past + seq_len,
                                      ((0, past), (past, past + seq_len)), band=False, keys_on_rows=False)
    xs = _band_sample(xs, q3s, qms, kn3, vn3, _positions_last(cache_band_k), _positions_last(cache_band_v),
                      bias_c, bias_n, cmkt, cmvt, wout_b, vec(g_mix_post[1]))
    y_sample = ffn(xs, 1).reshape(n_seq, seq_len, D_MODEL)

    return (y_prompt, y_sample,
            _heads_last(mem_kt, (DEPTH, 1)), _heads_last(mem_vt, (DEPTH, 1)),
            _heads_last(kt_tail, (1,)), _heads_last(vt_tail, (1,)),
            v_rows.reshape(1, n_seq, seq_len, GM_W),
            jnp.swapaxes(k_new, 1, 2), jnp.swapaxes(v_new, 1, 2))
```

```python
import functools

import jax
import jax.numpy as jnp
import numpy as np
from jax import lax
from jax.experimental import pallas as pl
from jax.experimental.pallas import tpu as pltpu

D_MODEL = 1024
DEPTH = 2
CHUNK = 64
HEAD_DIM = 64
GM_CHUNK = 128
GM_GROUPS = 4
GM_W = 768
GM_GW = GM_W // GM_GROUPS
MEM_LEN = 256
MEM_HEADS = 4
MEM_W = MEM_HEADS * HEAD_DIM
B_HEADS = 12
B_W = B_HEADS * HEAD_DIM
BAND_PAST = 512
REL_CLIP = 128
D_FF = 4 * D_MODEL
EPS = 1e-6

LANES = 128
HEAD_PAIRS = B_W // LANES
Q_SCALE = HEAD_DIM ** -0.5
NEG = -1e30
TILE = 256
KEY_BLOCKS = BAND_PAST // TILE + 1
KEY_STEP = 128
PROJ_ROWS = 512
GROUP_PAIRS = 2
BIAS_PERIOD = 1024
VMEM_LIMIT = 56 * 1024 * 1024

BF16 = jnp.bfloat16
F32 = jnp.float32


def _dot(a, b):
    return jnp.dot(a, b, preferred_element_type=F32)


def _dot_nt(a, b):
    return lax.dot_general(a, b, (((1,), (1,)), ((), ())), preferred_element_type=F32)


def _rms(x, g):
    ms = jnp.mean(x * x, axis=-1, keepdims=True)
    return x * lax.rsqrt(ms + EPS) * g


def _lsum(a):
    return jnp.sum(a, axis=-1, keepdims=True)


def _lmax(a):
    return jnp.max(a, axis=-1, keepdims=True)


def _const_spec(shape):
    nd = len(shape)
    return pl.BlockSpec(shape, lambda *_: (0,) * nd, pipeline_mode=pl.Buffered(1))


def _layer_spec(shape, layer):
    nd = len(shape)
    return pl.BlockSpec((None,) + shape, lambda *_: (layer,) + (0,) * nd, pipeline_mode=pl.Buffered(1))


def _call(body, name, flags=None, **kw):
    return pl.pallas_call(
        body, name=name,
        compiler_params=pltpu.CompilerParams(dimension_semantics=("parallel",),
                                             vmem_limit_bytes=VMEM_LIMIT, flags=flags), **kw)


def _memkv_kernel(mem_ref, g_ref, wt_ref, kt_ref, vt_ref):
    ht = _dot_nt(wt_ref[...], _rms(mem_ref[...], g_ref[...]).astype(BF16))
    kt_ref[0] = ht[:MEM_W]
    vt_ref[0] = ht[MEM_W:]


def _memkv(mem, g_mem, w_mem_kv_t):
    out = jax.ShapeDtypeStruct((DEPTH, 1, MEM_W, MEM_LEN), F32)
    return _call(
        _memkv_kernel, "mem_kv",
        grid=(DEPTH,),
        in_specs=[
            pl.BlockSpec((MEM_LEN, D_MODEL), lambda l: (0, 0)),
            pl.BlockSpec((None, 1, D_MODEL), lambda l: (l, 0, 0)),
            pl.BlockSpec((None, 2 * MEM_W, D_MODEL), lambda l: (l, 0, 0)),
        ],
        out_specs=[pl.BlockSpec((None, 1, MEM_W, MEM_LEN), lambda l: (l, 0, 0, 0))] * 2,
        out_shape=[out, out],
    )(mem, g_mem.reshape(DEPTH, 1, D_MODEL), w_mem_kv_t)


def _mem_attend(qb, kt, vt):
    r = qb.shape[0]
    lane = lax.broadcasted_iota(jnp.int32, (1, MEM_W), 1)
    masks = [(lane >= h * HEAD_DIM) & (lane < (h + 1) * HEAD_DIM) for h in range(MEM_HEADS)]
    qs = jnp.concatenate([jnp.where(m, qb, jnp.zeros_like(qb)) for m in masks], axis=0)
    s = _dot(qs, kt)
    e = jnp.exp(s - _lmax(s))
    pv = _dot_nt(e.astype(BF16), vt) * (1.0 / _lsum(e))
    out = jnp.where(masks[0], pv[:r], 0.0)
    for h in range(1, MEM_HEADS):
        out = out + jnp.where(masks[h], pv[h * r:(h + 1) * r], 0.0)
    return out


def _gelu(x):
    return jax.nn.gelu(x, approximate=True)


def _interleave(stage_lists):
    live = list(stage_lists)
    while live:
        for g in list(live):
            if next(g, StopIteration) is StopIteration:
                live.remove(g)


def _mixer_a_kernel(x_ref, gpre_ref, gpost_ref, win_ref, gln_ref, bln_ref, s_ref, bs_ref,
                    mk_ref, mv_ref, wout_ref, o_ref, *v_out, rows_per_mem, tiles, shared_mem):
    nt = GM_W // LANES
    seqs = TILE // rows_per_mem
    lane = lax.broadcasted_iota(jnp.int32, (1, LANES), 1)
    lo = lane < (GM_GW - LANES)
    inv = 1.0 / GM_GW

    def group_stat(a):
        s0 = _lsum(a[0] + jnp.where(lo, a[1], 0.0)) * inv
        s1 = _lsum(jnp.where(lo, 0.0, a[1]) + a[2]) * inv
        s2 = _lsum(a[3] + jnp.where(lo, a[4], 0.0)) * inv
        s3 = _lsum(jnp.where(lo, 0.0, a[4]) + a[5]) * inv
        return [s0, jnp.where(lo, s0, s1), s1, s2, jnp.where(lo, s2, s3), s3]

    def tile(t):
        rows = slice(t * TILE, (t + 1) * TILE)
        x = x_ref[rows, :]
        z = _dot(_rms(x, gpre_ref[...]).astype(BF16), win_ref[...])
        yield
        u = [_gelu(z[:, j * LANES:(j + 1) * LANES]) for j in range(nt)]
        g = [_gelu(z[:, GM_W + j * LANES:GM_W + (j + 1) * LANES]) for j in range(nt)]
        mu = group_stat(g)
        c = [g[j] - mu[j] for j in range(nt)]
        var = group_stat([cj * cj for cj in c])
        gln = gln_ref[...]
        bln = bln_ref[...]
        vn = [c[j] * lax.rsqrt(var[j] + EPS) * gln[:, j * LANES:(j + 1) * LANES]
              + bln[:, j * LANES:(j + 1) * LANES] for j in range(nt)]
        if v_out:
            v_out[0][rows, :] = jnp.concatenate(vn, axis=-1)
        yield
        vb = [a.astype(BF16) for a in vn]
        win = [(0, 1), (1, 2), (3, 4), (4, 5)]
        m = [_dot(s_ref[k], jnp.concatenate([vb[a], vb[b]], axis=-1)) for k, (a, b) in enumerate(win)]
        mixed = [m[0][:, :LANES], jnp.where(lo, m[0][:, LANES:], m[1][:, :LANES]), m[1][:, LANES:],
                 m[2][:, :LANES], jnp.where(lo, m[2][:, LANES:], m[3][:, :LANES]), m[3][:, LANES:]]
        bs = bs_ref[...]
        gm = [u[j] * (mixed[j] + bs[:, j * LANES:(j + 1) * LANES]) for j in range(nt)]
        yield
        qm = (z[:, 2 * GM_W:] * Q_SCALE).astype(BF16)
        mo = []
        for b in range(seqs):
            r = slice(b * rows_per_mem, (b + 1) * rows_per_mem)
            mi = 0 if shared_mem else t * seqs + b
            mo.append(_mem_attend(qm[r], mk_ref[mi].astype(BF16), mv_ref[mi].astype(BF16)))
        mo = mo[0] if len(mo) == 1 else jnp.concatenate(mo, axis=0)
        yield
        cat = jnp.concatenate([a.astype(BF16) for a in gm] + [mo.astype(BF16)], axis=-1)
        o_ref[rows, :] = x + _rms(_dot(cat, wout_ref[...]), gpost_ref[...])

    _interleave([tile(t) for t in range(tiles)])


def _mem_spec(mem, seqs, layer):
    if mem.shape[1] == 1:
        return pl.BlockSpec((None, 1, MEM_W, MEM_LEN), lambda i: (layer, 0, 0, 0))
    return pl.BlockSpec((None, seqs, MEM_W, MEM_LEN), lambda i: (layer, i, 0, 0))


def _mixer_a(x, gpre, gpost, win, gln, bln, s_mat, bs, mkt, mvt, wout, *, rows_per_mem, emit_v, tiles=2):
    n = x.shape[0]
    seqs = tiles * TILE // rows_per_mem
    row = lambda w: pl.BlockSpec((tiles * TILE, w), lambda i: (i, 0))
    out_shape = [jax.ShapeDtypeStruct((n, D_MODEL), F32)]
    out_specs = [row(D_MODEL)]
    if emit_v:
        out_shape.append(jax.ShapeDtypeStruct((n, GM_W), F32))
        out_specs.append(row(GM_W))
    return _call(
        functools.partial(_mixer_a_kernel, rows_per_mem=rows_per_mem, tiles=tiles,
                          shared_mem=mkt.shape[1] == 1), "mixer_a",
        grid=(n // (tiles * TILE),),
        in_specs=[
            row(D_MODEL),
            _const_spec((1, D_MODEL)), _const_spec((1, D_MODEL)),
            _const_spec((D_MODEL, 2 * GM_W + MEM_W)),
            _const_spec((1, GM_W)), _const_spec((1, GM_W)),
            _const_spec((GM_GROUPS, TILE, TILE)), _const_spec((TILE, GM_W)),
            _mem_spec(mkt, seqs, 0), _mem_spec(mvt, seqs, 0),
            _const_spec((GM_W + MEM_W, D_MODEL)),
        ],
        out_specs=out_specs,
        out_shape=out_shape,
    )(x, gpre, gpost, win, gln, bln, s_mat, bs, mkt, mvt, wout)


def _ffn_kernel(x_ref, gpre_ref, gpost_ref, w1_ref, w2_ref, o_ref, *, rows, tiles, ff_chunk):
    def tile(t):
        r = slice(t * rows, (t + 1) * rows)
        x = x_ref[r, :]
        xn = _rms(x, gpre_ref[...]).astype(BF16)
        acc = jnp.zeros(x.shape, F32)
        yield
        for c in range(D_FF // ff_chunk):
            h = _dot(xn, w1_ref[:, c * ff_chunk:(c + 1) * ff_chunk])
            h = jnp.square(jnp.maximum(h, 0.0)).astype(BF16)
            acc = acc + _dot(h, w2_ref[c * ff_chunk:(c + 1) * ff_chunk, :])
            yield
        o_ref[r, :] = x + _rms(acc, gpost_ref[...])

    _interleave([tile(t) for t in range(tiles)])


def _ffn(x, gpre, gpost, w1, w2, layer, *, rows=512, tiles=2, ff_chunk=1024):
    n = x.shape[0]
    row = pl.BlockSpec((tiles * rows, D_MODEL), lambda i: (i, 0))
    return _call(
        functools.partial(_ffn_kernel, rows=rows, tiles=tiles, ff_chunk=ff_chunk), "ffn",
        grid=(n // (tiles * rows),),
        in_specs=[row, _layer_spec((1, D_MODEL), layer), _layer_spec((1, D_MODEL), layer),
                  _layer_spec((D_MODEL, D_FF), layer), _layer_spec((D_FF, D_MODEL), layer)],
        out_specs=row,
        out_shape=jax.ShapeDtypeStruct((n, D_MODEL), F32),
    )(x, gpre, gpost, w1, w2)


def _proj_b_kernel(x_ref, gkv_ref, gpre_ref, *refs, mode, seq_len):
    rows = x_ref.shape[0]

    def normed(r):
        x = x_ref[r, :]
        xh = x * lax.rsqrt(jnp.mean(x * x, axis=-1, keepdims=True) + EPS)
        return (xh * gkv_ref[...]).astype(BF16), (xh * gpre_ref[...]).astype(BF16)

    if mode == "tail":
        wkt_ref, wvt_ref, kt_ref, vt_ref = refs
        xkv, _ = normed(slice(None))
        kt_ref[...] = _dot_nt(wkt_ref[...], xkv)
        vt_ref[...] = _dot_nt(wvt_ref[...], xkv)
        return
    if mode == "prompt":
        wqt_ref, wqm_ref, wk_ref, wvt_ref, qt_ref, qm_ref, k_ref, vt_ref = refs

        def tile(t):
            r = slice(t * PROJ_ROWS, (t + 1) * PROJ_ROWS)
            xkv, xq = normed(r)
            yield
            qt = _dot_nt(wqt_ref[...], xq) * Q_SCALE
            for p in range(HEAD_PAIRS):
                qt_ref[p, :, r] = qt[p * LANES:(p + 1) * LANES, :].astype(BF16)
            yield
            vt = _dot_nt(wvt_ref[...], xkv)
            for p in range(HEAD_PAIRS):
                vt_ref[p, :, r] = vt[p * LANES:(p + 1) * LANES, :].astype(BF16)
            yield
            k = _dot(xkv, wk_ref[...])
            for p in range(HEAD_PAIRS):
                k_ref[p, r, :] = k[:, p * LANES:(p + 1) * LANES].astype(BF16)
            yield
            qm_ref[r, :] = (_dot(xq, wqm_ref[...]) * Q_SCALE).astype(BF16)

        _interleave([tile(t) for t in range(rows // PROJ_ROWS)])
        return
    win_ref, wk_ref, wv_ref, q_ref, qm_ref, k_ref, v_ref, k4_ref, v4_ref = refs
    xkv, xq = normed(slice(None))
    z = _dot(xq, win_ref[...]) * Q_SCALE
    qm_ref[...] = z[:, B_W:].astype(BF16)
    k = _dot(xkv, wk_ref[...])
    v = _dot(xkv, wv_ref[...])
    for p in range(HEAD_PAIRS):
        cols = slice(p * LANES, (p + 1) * LANES)
        q_ref[p] = z[:, cols].astype(BF16)
        k_ref[p] = k[:, cols].astype(BF16)
        v_ref[p] = v[:, cols].astype(BF16)
    for b in range(rows // seq_len):
        for h in range(B_HEADS):
            r, c = slice(b * seq_len, (b + 1) * seq_len), slice(h * HEAD_DIM, (h + 1) * HEAD_DIM)
            k4_ref[b, h] = k[r, c]
            v4_ref[b, h] = v[r, c]


def _proj_b(x, gkv, gpre, weights, *, mode, rows, row_offset=0, n_rows=None, seq_len=None):
    n = x.shape[0] if n_rows is None else n_rows
    off = row_offset // rows
    pair_rows = jax.ShapeDtypeStruct((HEAD_PAIRS, n, LANES), BF16)
    pair_rows_spec = pl.BlockSpec((HEAD_PAIRS, rows, LANES), lambda i: (0, i, 0))
    pair_cols = jax.ShapeDtypeStruct((HEAD_PAIRS, LANES, n), BF16)
    pair_cols_spec = pl.BlockSpec((HEAD_PAIRS, LANES, rows), lambda i: (0, 0, i))
    qm = jax.ShapeDtypeStruct((n, MEM_W), BF16)
    qm_spec = pl.BlockSpec((rows, MEM_W), lambda i: (i, 0))
    if mode == "tail":
        out_shape = [jax.ShapeDtypeStruct((B_W, n), F32)] * 2
        out_specs = [pl.BlockSpec((B_W, rows), lambda i: (0, i))] * 2
    elif mode == "prompt":
        out_shape = [pair_cols, qm, pair_rows, pair_cols]
        out_specs = [pair_cols_spec, qm_spec, pair_rows_spec, pair_cols_spec]
    else:
        seqs = rows // seq_len
        per_head = jax.ShapeDtypeStruct((n // seq_len, B_HEADS, seq_len, HEAD_DIM), F32)
        per_head_spec = pl.BlockSpec((seqs, B_HEADS, seq_len, HEAD_DIM), lambda i: (i, 0, 0, 0))
        out_shape = [pair_rows, qm, pair_rows, pair_rows, per_head, per_head]
        out_specs = [pair_rows_spec, qm_spec, pair_rows_spec, pair_rows_spec, per_head_spec, per_head_spec]
    return _call(
        functools.partial(_proj_b_kernel, mode=mode, seq_len=seq_len), "proj_b_" + mode,
        grid=(n // rows,),
        in_specs=[pl.BlockSpec((rows, D_MODEL), lambda i: (i + off, 0)),
                  _const_spec((1, D_MODEL)), _const_spec((1, D_MODEL))]
                 + [_const_spec(w.shape) for w in weights],
        out_specs=out_specs,
        out_shape=out_shape,
    )(x, gkv, gpre, *weights)


def _bias_kernel(g_ref, *o_refs, n_q, n_k, splits, band, keys_on_rows):
    n_rows, shift = (n_k, n_q) if keys_on_rows else (n_q, BIAS_PERIOD - (n_q - 1))
    for hh in range(2):
        x = jnp.broadcast_to(g_ref[hh:hh + 1, :], (n_rows, BIAS_PERIOD))
        t = pltpu.roll(x, shift, 1, stride=1, stride_axis=0)
        if band:
            a = lax.broadcasted_iota(jnp.int32, (n_rows, BIAS_PERIOD), 0)
            b = lax.broadcasted_iota(jnp.int32, (n_rows, BIAS_PERIOD), 1)
            r, w = (b, a) if keys_on_rows else (a, b)
            j = w - (r - (r & (CHUNK - 1)))
            t = jnp.where((j >= 0) & (j < BAND_PAST + CHUNK), t, NEG)
        for o_ref, (lo, hi) in zip(o_refs, splits):
            o_ref[hh * n_rows:(hh + 1) * n_rows, :] = t[:, lo:hi]


def _rel_bias_tables(rel_bias, n_q, n_k, splits, *, band, keys_on_rows):
    rel_bias = rel_bias - rel_bias[:, -1:]
    c0 = n_k - 1
    far = jnp.broadcast_to(rel_bias[:, -1:], (B_HEADS, c0 - REL_CLIP))
    near = jnp.broadcast_to(rel_bias[:, :1], (B_HEADS, BIAS_PERIOD - (c0 - REL_CLIP) - (2 * REL_CLIP + 1)))
    gen = [near, rel_bias, far] if keys_on_rows else [far, rel_bias[:, ::-1], near]
    gen = jnp.concatenate(gen, axis=1).reshape(HEAD_PAIRS, 2, BIAS_PERIOD)
    n_rows = n_k if keys_on_rows else n_q
    return _call(
        functools.partial(_bias_kernel, n_q=n_q, n_k=n_k, splits=splits, band=band,
                          keys_on_rows=keys_on_rows), "rel_bias",
        grid=(HEAD_PAIRS,),
        in_specs=[pl.BlockSpec((None, 2, BIAS_PERIOD), lambda p: (p, 0, 0))],
        out_specs=[pl.BlockSpec((None, 2 * n_rows, hi - lo), lambda p: (p, 0, 0)) for lo, hi in splits],
        out_shape=[jax.ShapeDtypeStruct((HEAD_PAIRS, 2 * n_rows, hi - lo), F32) for lo, hi in splits],
    )(gen)


def _band_block_kinds():
    n_k = KEY_BLOCKS * TILE
    r = np.arange(TILE)[None, :]
    w = np.arange(n_k)[:, None]
    j = w - CHUNK * (r // CHUNK)
    ok = (j >= 0) & (j < BAND_PAST + CHUNK)
    plain = ok & (BAND_PAST + r - w >= REL_CLIP)
    kinds = []
    for a in range(n_k // KEY_STEP):
        rows = slice(a * KEY_STEP, (a + 1) * KEY_STEP)
        blocks = [(rows, slice(b * LANES, (b + 1) * LANES)) for b in range(TILE // LANES)]
        kinds.append(["skip" if not ok[blk].any() else "plain" if plain[blk].all() else "biased"
                      for blk in blocks])
    return kinds


def _band_prompt_kernel(x_ref, qt_ref, qm_ref, k0_ref, k1_ref, k2_ref, vt0_ref, vt1_ref, vt2_ref,
                        bias_ref, mk_ref, mv_ref, wout_ref, gpost_ref, o_ref):
    i = pl.program_id(0)
    n_k = KEY_BLOCKS * TILE
    kinds = _band_block_kinds()
    w = lax.broadcasted_iota(jnp.int32, (n_k, LANES), 0)
    c = lax.broadcasted_iota(jnp.int32, (n_k, LANES), 1)
    pen = jnp.where((w < (KEY_BLOCKS - 1 - i) * TILE) & (c == 0), NEG, 0.0).astype(BF16)
    row = lax.broadcasted_iota(jnp.int32, (LANES, TILE), 0)
    one_hot_row = jnp.where(row == 0, 1.0, 0.0).astype(BF16)
    lo = row < HEAD_DIM

    def key_step(a, s, vth, bias_rows, state):
        rows = slice(a * KEY_STEP, (a + 1) * KEY_STEP)
        es, alphas = [], []
        for b, kind in enumerate(kinds[a]):
            cols = slice(b * LANES, (b + 1) * LANES)
            if kind == "skip":
                es.append(jnp.zeros((KEY_STEP, LANES), F32))
                alphas.append(None)
                continue
            sj = s[rows, cols]
            if kind == "biased":
                sj = sj + bias_rows(rows, cols)
            mj = jnp.max(sj, axis=0, keepdims=True)
            if state[b] is None:
                e = jnp.exp(sj - mj)
                state[b] = [mj, jnp.sum(e, axis=0, keepdims=True), None]
                alphas.append(None)
            else:
                m_old, l_old, _ = state[b]
                m_new = jnp.maximum(m_old, mj)
                alpha = jnp.exp(m_old - m_new)
                e = jnp.exp(sj - m_new)
                state[b][:2] = [m_new, l_old * alpha + jnp.sum(e, axis=0, keepdims=True)]
                alphas.append(alpha)
            es.append(e)
        pv = _dot(vth[:, rows], jnp.concatenate(es, axis=1).astype(BF16))
        for b, kind in enumerate(kinds[a]):
            if kind != "skip":
                pv_b = pv[:, b * LANES:(b + 1) * LANES]
                state[b][2] = pv_b if alphas[b] is None else state[b][2] * alphas[b] + pv_b

    outs = []
    for p0 in range(0, HEAD_PAIRS, GROUP_PAIRS):
        heads = []
        for p in range(p0, p0 + GROUP_PAIRS):
            k_ext = jnp.concatenate(
                [jnp.concatenate([k0_ref[p], k1_ref[p], k2_ref[p]], axis=0), pen], axis=1)
            vt = jnp.concatenate([vt0_ref[p], vt1_ref[p], vt2_ref[p]], axis=1)
            qt = qt_ref[p]
            zero = jnp.zeros_like(qt)
            for hh in range(2):
                qh = jnp.where(lo, qt, zero) if hh == 0 else jnp.where(lo, zero, qt)
                s = _dot(k_ext, jnp.concatenate([qh, one_hot_row], axis=0))
                vth = vt[hh * HEAD_DIM:(hh + 1) * HEAD_DIM, :]
                bias_rows = functools.partial(
                    lambda rows, cols, p, base: bias_ref[p, base + rows.start:base + rows.stop, cols],
                    p=p, base=hh * n_k)
                heads.append((s, vth, bias_rows, [None] * len(kinds[0])))
        for a in range(len(kinds)):
            for head in heads:
                key_step(a, *head)
        outs += [jnp.concatenate([o * (1.0 / l) for _, l, o in state], axis=1) for *_, state in heads]
    band = jnp.concatenate(outs, axis=0).T.astype(BF16)
    mo = _mem_attend(qm_ref[...], mk_ref[0].astype(BF16), mv_ref[0].astype(BF16))
    cat = jnp.concatenate([band, mo.astype(BF16)], axis=-1)
    o_ref[...] = x_ref[...] + _rms(_dot(cat, wout_ref[...]), gpost_ref[...])


def _band_prompt(x, qt3, qm, k3, vt3, bias, mkt, mvt, wout, gpost):
    n = x.shape[0]
    blk = lambda i, j: jnp.maximum(i - (KEY_BLOCKS - 1) + j, 0)
    k_specs = [pl.BlockSpec((HEAD_PAIRS, TILE, LANES), functools.partial(lambda i, j: (0, blk(i, j), 0), j=j))
               for j in range(KEY_BLOCKS)]
    vt_specs = [pl.BlockSpec((HEAD_PAIRS, LANES, TILE), functools.partial(lambda i, j: (0, 0, blk(i, j)), j=j))
                for j in range(KEY_BLOCKS)]
    return _call(
        _band_prompt_kernel, "band_prompt",
        grid=(n // TILE,),
        in_specs=[pl.BlockSpec((TILE, D_MODEL), lambda i: (i, 0)),
                  pl.BlockSpec((HEAD_PAIRS, LANES, TILE), lambda i: (0, 0, i)),
                  pl.BlockSpec((TILE, MEM_W), lambda i: (i, 0)),
                  *k_specs, *vt_specs,
                  _const_spec((HEAD_PAIRS, 2 * KEY_BLOCKS * TILE, TILE)),
                  _mem_spec(mkt, 1, 1), _mem_spec(mvt, 1, 1),
                  _const_spec((B_W + MEM_W, D_MODEL)), _const_spec((1, D_MODEL))],
        out_specs=pl.BlockSpec((TILE, D_MODEL), lambda i: (i, 0)),
        out_shape=jax.ShapeDtypeStruct((n, D_MODEL), F32),
    )(x, qt3, qm, *[k3] * KEY_BLOCKS, *[vt3] * KEY_BLOCKS, bias, mkt, mvt, wout, gpost)


def _band_sample_kernel(x_ref, q_ref, qm_ref, kn_ref, vn_ref, ck_ref, cv_ref, bc_ref, bn_ref,
                        mk_ref, mv_ref, wout_ref, gpost_ref, o_ref, *, seqs, seq_len):
    lane = lax.broadcasted_iota(jnp.int32, (1, LANES), 1)
    lo = lane < HEAD_DIM

    def pair_attend(b, p, out):
        rows = slice(b * seq_len, (b + 1) * seq_len)
        hd = slice(p * LANES, (p + 1) * LANES)
        qp = q_ref[p, rows, :]
        zero = jnp.zeros_like(qp)
        qs = jnp.concatenate([jnp.where(lo, qp, zero), jnp.where(lo, zero, qp)], axis=0)
        sc = _dot(qs, ck_ref[b, hd, :].astype(BF16)) + bc_ref[p]
        sn = _dot_nt(qs, kn_ref[p, rows, :]) + bn_ref[p]
        yield
        m = jnp.maximum(_lmax(sc), _lmax(sn))
        ec = jnp.exp(sc - m)
        en = jnp.exp(sn - m)
        l = _lsum(ec) + _lsum(en)
        yield
        o = (_dot_nt(ec.astype(BF16), cv_ref[b, hd, :].astype(BF16))
             + _dot(en.astype(BF16), vn_ref[p, rows, :]))
        yield
        o = o * (1.0 / l)
        out[p] = jnp.where(lo, o[:seq_len], o[seq_len:]).astype(BF16)

    rows_out = []
    for b in range(seqs):
        band = [None] * HEAD_PAIRS
        _interleave([pair_attend(b, p, band) for p in range(HEAD_PAIRS)])
        rows = slice(b * seq_len, (b + 1) * seq_len)
        mo = _mem_attend(qm_ref[rows, :], mk_ref[b].astype(BF16), mv_ref[b].astype(BF16))
        rows_out.append(jnp.concatenate(band + [mo.astype(BF16)], axis=-1))
    cat = jnp.concatenate(rows_out, axis=0)
    o_ref[...] = x_ref[...] + _rms(_dot(cat, wout_ref[...]), gpost_ref[...])


def _band_sample(x, q3, qm, kn3, vn3, ckt, cvt, bias_c, bias_n, mkt, mvt, wout, gpost, *, seqs=4):
    n = x.shape[0]
    n_seq, past = ckt.shape[0], ckt.shape[2]
    seq_len = n // n_seq
    rows = seqs * seq_len
    pair_rows_spec = pl.BlockSpec((HEAD_PAIRS, rows, LANES), lambda i: (0, i, 0))
    cache_spec = pl.BlockSpec((seqs, B_W, past), lambda i: (i, 0, 0))
    return _call(
        functools.partial(_band_sample_kernel, seqs=seqs, seq_len=seq_len), "band_sample",
        grid=(n_seq // seqs,),
        in_specs=[pl.BlockSpec((rows, D_MODEL), lambda i: (i, 0)),
                  pair_rows_spec,
                  pl.BlockSpec((rows, MEM_W), lambda i: (i, 0)),
                  pair_rows_spec, pair_rows_spec,
                  cache_spec, cache_spec,
                  _const_spec((HEAD_PAIRS, 2 * seq_len, past)),
                  _const_spec((HEAD_PAIRS, 2 * seq_len, seq_len)),
                  _mem_spec(mkt, seqs, 1), _mem_spec(mvt, seqs, 1),
                  _const_spec((B_W + MEM_W, D_MODEL)), _const_spec((1, D_MODEL))],
        out_specs=pl.BlockSpec((rows, D_MODEL), lambda i: (i, 0)),
        out_shape=jax.ShapeDtypeStruct((n, D_MODEL), F32),
    )(x, q3, qm, kn3, vn3, ckt, cvt, bias_c, bias_n, mkt, mvt, wout, gpost)


def _spatial_tile(w_s, b_s, period):
    tril = jnp.tril(jnp.ones((GM_CHUNK, GM_CHUNK), dtype=bool))
    w = jnp.where(tril, w_s, jnp.zeros((), w_s.dtype))[:, :period, :period]
    eye = jnp.eye(TILE // period, dtype=w.dtype)
    s_mat = jnp.einsum("ab,gts->gatbs", eye, w).reshape(GM_GROUPS, TILE, TILE)
    rows = jnp.tile(b_s[:, :period], (1, TILE // period))
    bs = jnp.repeat(rows.T, GM_GW, axis=1)
    return s_mat.astype(BF16), bs


def _heads_last(t, lead):
    pos = t.shape[-1]
    t = t.reshape(lead + (-1, HEAD_DIM, pos))
    nd = len(lead)
    return jnp.transpose(t, tuple(range(nd)) + (nd + 2, nd, nd + 1))


def _positions_last(c):
    nd = c.ndim
    t = jnp.transpose(c, tuple(range(nd - 3)) + (nd - 2, nd - 1, nd - 3))
    return t.reshape(c.shape[:-3] + (c.shape[-2] * c.shape[-1], c.shape[-3]))


def kernel(x_prompt, x_sample, cache_mem_k, cache_mem_v, cache_band_k, cache_band_v, mem_prompt,
           g_mix_pre, g_mix_post, g_ffn_pre, g_ffn_post, g_mem, w_mem_kv,
           w_in_a, g_gm_ln, b_gm_ln, w_spatial, b_spatial, w_out_a,
           g_kv, w_kv, w_in_b, rel_bias, w_out_b, w_ff1, w_ff2):
    seq = x_prompt.shape[1]
    n_seq, seq_len = x_sample.shape[0], x_sample.shape[1]
    past = cache_band_k.shape[1]
    vec = lambda a: a.reshape(1, -1)
    stack = lambda a: a.reshape(DEPTH, 1, -1)

    win_a = w_in_a[0].astype(BF16)
    wout_a = w_out_a[0].astype(BF16)
    wk = w_kv[:, :B_W].astype(BF16)
    wv = w_kv[:, B_W:].astype(BF16)
    wkt, wvt = wk.T, wv.T
    win_b = w_in_b[0].astype(BF16)
    wqt, wqm = win_b[:, :B_W].T, win_b[:, B_W:]
    wout_b = w_out_b[0].astype(BF16)
    w1 = w_ff1.astype(BF16)
    w2 = w_ff2.astype(BF16)
    ln_g, ln_b = vec(g_gm_ln[0]), vec(b_gm_ln[0])
    gkv, gpre_b = vec(g_kv), vec(g_mix_pre[1])
    gf_pre, gf_post = stack(g_ffn_pre), stack(g_ffn_post)

    def ffn(x, l):
        return _ffn(x, gf_pre, gf_post, w1, w2, l)

    mem_kt, mem_vt = _memkv(mem_prompt[0], g_mem, jnp.swapaxes(w_mem_kv, 1, 2).astype(BF16))
    s_p, bs_p = _spatial_tile(w_spatial[0], b_spatial[0], GM_CHUNK)
    x = x_prompt[0]
    x, = _mixer_a(x, vec(g_mix_pre[0]), vec(g_mix_post[0]), win_a, ln_g, ln_b, s_p, bs_p,
                  mem_kt, mem_vt, wout_a, rows_per_mem=TILE, emit_v=False, tiles=4)
    x = ffn(x, 0)
    qt3, qm, k3, vt3 = _proj_b(x, gkv, gpre_b, (wqt, wqm, wk, wvt), mode="prompt", rows=2 * PROJ_ROWS)
    n_keep = min(BAND_PAST, seq)
    kt_tail, vt_tail = _proj_b(x, gkv, gpre_b, (wkt, wvt), mode="tail", rows=TILE,
                               row_offset=seq - n_keep, n_rows=n_keep)
    n_k = BAND_PAST + TILE
    bias_p, = _rel_bias_tables(rel_bias[0], TILE, n_k, ((0, TILE),), band=True, keys_on_rows=True)
    x = _band_prompt(x, qt3, qm, k3, vt3, bias_p, mem_kt, mem_vt, wout_b, vec(g_mix_post[1]))
    y_prompt = ffn(x, 1)[None]

    s_s, bs_s = _spatial_tile(w_spatial[0], b_spatial[0], seq_len)
    xs = x_sample.reshape(n_seq * seq_len, D_MODEL)
    cmkt, cmvt = _positions_last(cache_mem_k), _positions_last(cache_mem_v)
    xs, v_rows = _mixer_a(xs, vec(g_mix_pre[0]), vec(g_mix_post[0]), win_a, ln_g, ln_b, s_s, bs_s,
                          cmkt, cmvt, wout_a, rows_per_mem=seq_len, emit_v=True)
    xs = ffn(xs, 0)
    q3s, qms, kn3, vn3, k_new, v_new = _proj_b(xs, gkv, gpre_b, (win_b, wk, wv), mode="sample",
                                               rows=512, seq_len=seq_len)
    bias_c, bias_n = _rel_bias_tables(rel_bias[0], seq_len, past + seq_len,
                                      ((0, past), (past, past + seq_len)), band=False, keys_on_rows=False)
    xs = _band_sample(xs, q3s, qms, kn3, vn3, _positions_last(cache_band_k), _positions_last(cache_band_v),
                      bias_c, bias_n, cmkt, cmvt, wout_b, vec(g_mix_post[1]))
    y_sample = ffn(xs, 1).reshape(n_seq, seq_len, D_MODEL)

    return (y_prompt, y_sample,
            _heads_last(mem_kt, (DEPTH, 1)), _heads_last(mem_vt, (DEPTH, 1)),
            _heads_last(kt_tail, (1,)), _heads_last(vt_tail, (1,)),
            v_rows.reshape(1, n_seq, seq_len, GM_W),
            jnp.swapaxes(k_new, 1, 2), jnp.swapaxes(v_new, 1, 2))
```

```python
import functools

import jax
import jax.numpy as jnp
import numpy as np
from jax import lax
from jax.experimental import pallas as pl
from jax.experimental.pallas import tpu as pltpu

D_MODEL = 1024
DEPTH = 2
CHUNK = 64
HEAD_DIM = 64
GM_CHUNK = 128
GM_GROUPS = 4
GM_W = 768
GM_GW = GM_W // GM_GROUPS
MEM_LEN = 256
MEM_HEADS = 4
MEM_W = MEM_HEADS * HEAD_DIM
B_HEADS = 12
B_W = B_HEADS * HEAD_DIM
BAND_PAST = 512
REL_CLIP = 128
D_FF = 4 * D_MODEL
EPS = 1e-6

LANES = 128
HEAD_PAIRS = B_W // LANES
Q_SCALE = HEAD_DIM ** -0.5
NEG = -1e30
TILE = 256
KEY_BLOCKS = BAND_PAST // TILE + 1
KEY_STEP = 128
PROJ_ROWS = 512
BAND_TILES = BAND_PAST // TILE
GROUP_PAIRS = 2
BIAS_PERIOD = 1024
VMEM_LIMIT = 56 * 1024 * 1024

BF16 = jnp.bfloat16
F32 = jnp.float32


def _dot(a, b):
    return jnp.dot(a, b, preferred_element_type=F32)


def _dot_nt(a, b):
    return lax.dot_general(a, b, (((1,), (1,)), ((), ())), preferred_element_type=F32)


def _rms(x, g):
    ms = jnp.mean(x * x, axis=-1, keepdims=True)
    return x * lax.rsqrt(ms + EPS) * g


def _lsum(a):
    return jnp.sum(a, axis=-1, keepdims=True)


def _lmax(a):
    return jnp.max(a, axis=-1, keepdims=True)


def _const_spec(shape):
    nd = len(shape)
    return pl.BlockSpec(shape, lambda *_: (0,) * nd, pipeline_mode=pl.Buffered(1))


def _layer_spec(shape, layer):
    nd = len(shape)
    return pl.BlockSpec((None,) + shape, lambda *_: (layer,) + (0,) * nd, pipeline_mode=pl.Buffered(1))


def _call(body, name, flags=None, **kw):
    return pl.pallas_call(
        body, name=name,
        compiler_params=pltpu.CompilerParams(dimension_semantics=("parallel",),
                                             vmem_limit_bytes=VMEM_LIMIT, flags=flags), **kw)


def _memkv_kernel(mem_ref, g_ref, wt_ref, kt_ref, vt_ref):
    ht = _dot_nt(wt_ref[...], _rms(mem_ref[...], g_ref[...]).astype(BF16))
    kt_ref[0] = ht[:MEM_W]
    vt_ref[0] = ht[MEM_W:]


def _memkv(mem, g_mem, w_mem_kv_t):
    out = jax.ShapeDtypeStruct((DEPTH, 1, MEM_W, MEM_LEN), F32)
    return _call(
        _memkv_kernel, "mem_kv",
        grid=(DEPTH,),
        in_specs=[
            pl.BlockSpec((MEM_LEN, D_MODEL), lambda l: (0, 0)),
            pl.BlockSpec((None, 1, D_MODEL), lambda l: (l, 0, 0)),
            pl.BlockSpec((None, 2 * MEM_W, D_MODEL), lambda l: (l, 0, 0)),
        ],
        out_specs=[pl.BlockSpec((None, 1, MEM_W, MEM_LEN), lambda l: (l, 0, 0, 0))] * 2,
        out_shape=[out, out],
    )(mem, g_mem.reshape(DEPTH, 1, D_MODEL), w_mem_kv_t)


def _mem_attend(qb, kt, vt):
    r = qb.shape[0]
    lane = lax.broadcasted_iota(jnp.int32, (1, MEM_W), 1)
    masks = [(lane >= h * HEAD_DIM) & (lane < (h + 1) * HEAD_DIM) for h in range(MEM_HEADS)]
    qs = jnp.concatenate([jnp.where(m, qb, jnp.zeros_like(qb)) for m in masks], axis=0)
    s = _dot(qs, kt)
    e = jnp.exp(s - _lmax(s))
    pv = _dot_nt(e.astype(BF16), vt) * (1.0 / _lsum(e))
    out = jnp.where(masks[0], pv[:r], 0.0)
    for h in range(1, MEM_HEADS):
        out = out + jnp.where(masks[h], pv[h * r:(h + 1) * r], 0.0)
    return out


def _gelu(x):
    return jax.nn.gelu(x, approximate=True)


def _interleave(stage_lists):
    live = list(stage_lists)
    while live:
        for g in list(live):
            if next(g, StopIteration) is StopIteration:
                live.remove(g)


def _mixer_a_kernel(x_ref, gpre_ref, gpost_ref, win_ref, gln_ref, bln_ref, s_ref, bs_ref,
                    mk_ref, mv_ref, wout_ref, o_ref, *v_out, rows_per_mem, tiles, shared_mem):
    nt = GM_W // LANES
    seqs = TILE // rows_per_mem
    lane = lax.broadcasted_iota(jnp.int32, (1, LANES), 1)
    lo = lane < (GM_GW - LANES)
    inv = 1.0 / GM_GW

    def group_stat(a):
        s0 = _lsum(a[0] + jnp.where(lo, a[1], 0.0)) * inv
        s1 = _lsum(jnp.where(lo, 0.0, a[1]) + a[2]) * inv
        s2 = _lsum(a[3] + jnp.where(lo, a[4], 0.0)) * inv
        s3 = _lsum(jnp.where(lo, 0.0, a[4]) + a[5]) * inv
        return [s0, jnp.where(lo, s0, s1), s1, s2, jnp.where(lo, s2, s3), s3]

    def tile(t):
        rows = slice(t * TILE, (t + 1) * TILE)
        x = x_ref[rows, :]
        z = _dot(_rms(x, gpre_ref[...]).astype(BF16), win_ref[...])
        yield
        u = [_gelu(z[:, j * LANES:(j + 1) * LANES]) for j in range(nt)]
        g = [_gelu(z[:, GM_W + j * LANES:GM_W + (j + 1) * LANES]) for j in range(nt)]
        mu = group_stat(g)
        c = [g[j] - mu[j] for j in range(nt)]
        var = group_stat([cj * cj for cj in c])
        gln = gln_ref[...]
        bln = bln_ref[...]
        vn = [c[j] * lax.rsqrt(var[j] + EPS) * gln[:, j * LANES:(j + 1) * LANES]
              + bln[:, j * LANES:(j + 1) * LANES] for j in range(nt)]
        if v_out:
            v_out[0][rows, :] = jnp.concatenate(vn, axis=-1)
        yield
        vb = [a.astype(BF16) for a in vn]
        win = [(0, 1), (1, 2), (3, 4), (4, 5)]
        m = [_dot(s_ref[k], jnp.concatenate([vb[a], vb[b]], axis=-1)) for k, (a, b) in enumerate(win)]
        mixed = [m[0][:, :LANES], jnp.where(lo, m[0][:, LANES:], m[1][:, :LANES]), m[1][:, LANES:],
                 m[2][:, :LANES], jnp.where(lo, m[2][:, LANES:], m[3][:, :LANES]), m[3][:, LANES:]]
        bs = bs_ref[...]
        gm = [u[j] * (mixed[j] + bs[:, j * LANES:(j + 1) * LANES]) for j in range(nt)]
        yield
        qm = (z[:, 2 * GM_W:] * Q_SCALE).astype(BF16)
        mo = []
        for b in range(seqs):
            r = slice(b * rows_per_mem, (b + 1) * rows_per_mem)
            mi = 0 if shared_mem else t * seqs + b
            mo.append(_mem_attend(qm[r], mk_ref[mi].astype(BF16), mv_ref[mi].astype(BF16)))
        mo = mo[0] if len(mo) == 1 else jnp.concatenate(mo, axis=0)
        yield
        cat = jnp.concatenate([a.astype(BF16) for a in gm] + [mo.astype(BF16)], axis=-1)
        o_ref[rows, :] = x + _rms(_dot(cat, wout_ref[...]), gpost_ref[...])

    _interleave([tile(t) for t in range(tiles)])


def _mem_spec(mem, seqs, layer):
    if mem.shape[1] == 1:
        return pl.BlockSpec((None, 1, MEM_W, MEM_LEN), lambda i: (layer, 0, 0, 0))
    return pl.BlockSpec((None, seqs, MEM_W, MEM_LEN), lambda i: (layer, i, 0, 0))


def _mixer_a(x, gpre, gpost, win, gln, bln, s_mat, bs, mkt, mvt, wout, *, rows_per_mem, emit_v, tiles=2):
    n = x.shape[0]
    seqs = tiles * TILE // rows_per_mem
    row = lambda w: pl.BlockSpec((tiles * TILE, w), lambda i: (i, 0))
    out_shape = [jax.ShapeDtypeStruct((n, D_MODEL), F32)]
    out_specs = [row(D_MODEL)]
    if emit_v:
        out_shape.append(jax.ShapeDtypeStruct((n, GM_W), F32))
        out_specs.append(row(GM_W))
    return _call(
        functools.partial(_mixer_a_kernel, rows_per_mem=rows_per_mem, tiles=tiles,
                          shared_mem=mkt.shape[1] == 1), "mixer_a",
        grid=(n // (tiles * TILE),),
        in_specs=[
            row(D_MODEL),
            _const_spec((1, D_MODEL)), _const_spec((1, D_MODEL)),
            _const_spec((D_MODEL, 2 * GM_W + MEM_W)),
            _const_spec((1, GM_W)), _const_spec((1, GM_W)),
            _const_spec((GM_GROUPS, TILE, TILE)), _const_spec((TILE, GM_W)),
            _mem_spec(mkt, seqs, 0), _mem_spec(mvt, seqs, 0),
            _const_spec((GM_W + MEM_W, D_MODEL)),
        ],
        out_specs=out_specs,
        out_shape=out_shape,
    )(x, gpre, gpost, win, gln, bln, s_mat, bs, mkt, mvt, wout)


def _ffn_kernel(x_ref, gpre_ref, gpost_ref, w1_ref, w2_ref, o_ref, *, rows, tiles, ff_chunk):
    def tile(t):
        r = slice(t * rows, (t + 1) * rows)
        x = x_ref[r, :]
        xn = _rms(x, gpre_ref[...]).astype(BF16)
        acc = jnp.zeros(x.shape, F32)
        yield
        for c in range(D_FF // ff_chunk):
            h = _dot(xn, w1_ref[:, c * ff_chunk:(c + 1) * ff_chunk])
            h = jnp.square(jnp.maximum(h, 0.0)).astype(BF16)
            acc = acc + _dot(h, w2_ref[c * ff_chunk:(c + 1) * ff_chunk, :])
            yield
        o_ref[r, :] = x + _rms(acc, gpost_ref[...])

    _interleave([tile(t) for t in range(tiles)])


def _ffn(x, gpre, gpost, w1, w2, layer, *, rows=512, tiles=2, ff_chunk=1024):
    n = x.shape[0]
    row = pl.BlockSpec((tiles * rows, D_MODEL), lambda i: (i, 0))
    return _call(
        functools.partial(_ffn_kernel, rows=rows, tiles=tiles, ff_chunk=ff_chunk), "ffn",
        grid=(n // (tiles * rows),),
        in_specs=[row, _layer_spec((1, D_MODEL), layer), _layer_spec((1, D_MODEL), layer),
                  _layer_spec((D_MODEL, D_FF), layer), _layer_spec((D_FF, D_MODEL), layer)],
        out_specs=row,
        out_shape=jax.ShapeDtypeStruct((n, D_MODEL), F32),
    )(x, gpre, gpost, w1, w2)


def _proj_b_kernel(x_ref, gkv_ref, gpre_ref, *refs, mode, seq_len):
    rows = x_ref.shape[0]

    def normed(r):
        x = x_ref[r, :]
        xh = x * lax.rsqrt(jnp.mean(x * x, axis=-1, keepdims=True) + EPS)
        return (xh * gkv_ref[...]).astype(BF16), (xh * gpre_ref[...]).astype(BF16)

    if mode == "tail":
        wkt_ref, wvt_ref, kt_ref, vt_ref = refs
        xkv, _ = normed(slice(None))
        kt_ref[...] = _dot_nt(wkt_ref[...], xkv)
        vt_ref[...] = _dot_nt(wvt_ref[...], xkv)
        return
    if mode == "prompt":
        wqt_ref, wqm_ref, wk_ref, wvt_ref, qt_ref, qm_ref, k_ref, vt_ref = refs

        def tile(t):
            r = slice(t * PROJ_ROWS, (t + 1) * PROJ_ROWS)
            xkv, xq = normed(r)
            yield
            qt = _dot_nt(wqt_ref[...], xq) * Q_SCALE
            for p in range(HEAD_PAIRS):
                qt_ref[p, :, r] = qt[p * LANES:(p + 1) * LANES, :].astype(BF16)
            yield
            vt = _dot_nt(wvt_ref[...], xkv)
            for p in range(HEAD_PAIRS):
                vt_ref[p, :, r] = vt[p * LANES:(p + 1) * LANES, :].astype(BF16)
            yield
            k = _dot(xkv, wk_ref[...])
            for p in range(HEAD_PAIRS):
                k_ref[p, r, :] = k[:, p * LANES:(p + 1) * LANES].astype(BF16)
            yield
            qm_ref[r, :] = (_dot(xq, wqm_ref[...]) * Q_SCALE).astype(BF16)

        _interleave([tile(t) for t in range(rows // PROJ_ROWS)])
        return
    win_ref, wk_ref, wv_ref, q_ref, qm_ref, k_ref, v_ref, k4_ref, v4_ref = refs
    xkv, xq = normed(slice(None))
    z = _dot(xq, win_ref[...]) * Q_SCALE
    qm_ref[...] = z[:, B_W:].astype(BF16)
    k = _dot(xkv, wk_ref[...])
    v = _dot(xkv, wv_ref[...])
    for p in range(HEAD_PAIRS):
        cols = slice(p * LANES, (p + 1) * LANES)
        q_ref[p] = z[:, cols].astype(BF16)
        k_ref[p] = k[:, cols].astype(BF16)
        v_ref[p] = v[:, cols].astype(BF16)
    for b in range(rows // seq_len):
        for h in range(B_HEADS):
            r, c = slice(b * seq_len, (b + 1) * seq_len), slice(h * HEAD_DIM, (h + 1) * HEAD_DIM)
            k4_ref[b, h] = k[r, c]
            v4_ref[b, h] = v[r, c]


def _proj_b(x, gkv, gpre, weights, *, mode, rows, row_offset=0, n_rows=None, seq_len=None):
    n = x.shape[0] if n_rows is None else n_rows
    off = row_offset // rows
    pair_rows = jax.ShapeDtypeStruct((HEAD_PAIRS, n, LANES), BF16)
    pair_rows_spec = pl.BlockSpec((HEAD_PAIRS, rows, LANES), lambda i: (0, i, 0))
    pair_cols = jax.ShapeDtypeStruct((HEAD_PAIRS, LANES, n), BF16)
    pair_cols_spec = pl.BlockSpec((HEAD_PAIRS, LANES, rows), lambda i: (0, 0, i))
    qm = jax.ShapeDtypeStruct((n, MEM_W), BF16)
    qm_spec = pl.BlockSpec((rows, MEM_W), lambda i: (i, 0))
    if mode == "tail":
        out_shape = [jax.ShapeDtypeStruct((B_W, n), F32)] * 2
        out_specs = [pl.BlockSpec((B_W, rows), lambda i: (0, i))] * 2
    elif mode == "prompt":
        out_shape = [pair_cols, qm, pair_rows, pair_cols]
        out_specs = [pair_cols_spec, qm_spec, pair_rows_spec, pair_cols_spec]
    else:
        seqs = rows // seq_len
        per_head = jax.ShapeDtypeStruct((n // seq_len, B_HEADS, seq_len, HEAD_DIM), F32)
        per_head_spec = pl.BlockSpec((seqs, B_HEADS, seq_len, HEAD_DIM), lambda i: (i, 0, 0, 0))
        out_shape = [pair_rows, qm, pair_rows, pair_rows, per_head, per_head]
        out_specs = [pair_rows_spec, qm_spec, pair_rows_spec, pair_rows_spec, per_head_spec, per_head_spec]
    return _call(
        functools.partial(_proj_b_kernel, mode=mode, seq_len=seq_len), "proj_b_" + mode,
        grid=(n // rows,),
        in_specs=[pl.BlockSpec((rows, D_MODEL), lambda i: (i + off, 0)),
                  _const_spec((1, D_MODEL)), _const_spec((1, D_MODEL))]
                 + [_const_spec(w.shape) for w in weights],
        out_specs=out_specs,
        out_shape=out_shape,
    )(x, gkv, gpre, *weights)


def _bias_kernel(g_ref, *o_refs, n_q, n_k, splits, band, keys_on_rows):
    n_rows, shift = (n_k, n_q) if keys_on_rows else (n_q, BIAS_PERIOD - (n_q - 1))
    for hh in range(2):
        x = jnp.broadcast_to(g_ref[hh:hh + 1, :], (n_rows, BIAS_PERIOD))
        t = pltpu.roll(x, shift, 1, stride=1, stride_axis=0)
        if band:
            a = lax.broadcasted_iota(jnp.int32, (n_rows, BIAS_PERIOD), 0)
            b = lax.broadcasted_iota(jnp.int32, (n_rows, BIAS_PERIOD), 1)
            r, w = (b, a) if keys_on_rows else (a, b)
            j = w - (r - (r & (CHUNK - 1)))
            t = jnp.where((j >= 0) & (j < BAND_PAST + CHUNK), t, NEG)
        for o_ref, (lo, hi) in zip(o_refs, splits):
            o_ref[hh * n_rows:(hh + 1) * n_rows, :] = t[:, lo:hi]


def _rel_bias_tables(rel_bias, n_q, n_k, splits, *, band, keys_on_rows):
    rel_bias = rel_bias - rel_bias[:, -1:]
    c0 = n_k - 1
    far = jnp.broadcast_to(rel_bias[:, -1:], (B_HEADS, c0 - REL_CLIP))
    near = jnp.broadcast_to(rel_bias[:, :1], (B_HEADS, BIAS_PERIOD - (c0 - REL_CLIP) - (2 * REL_CLIP + 1)))
    gen = [near, rel_bias, far] if keys_on_rows else [far, rel_bias[:, ::-1], near]
    gen = jnp.concatenate(gen, axis=1).reshape(HEAD_PAIRS, 2, BIAS_PERIOD)
    n_rows = n_k if keys_on_rows else n_q
    return _call(
        functools.partial(_bias_kernel, n_q=n_q, n_k=n_k, splits=splits, band=band,
                          keys_on_rows=keys_on_rows), "rel_bias",
        grid=(HEAD_PAIRS,),
        in_specs=[pl.BlockSpec((None, 2, BIAS_PERIOD), lambda p: (p, 0, 0))],
        out_specs=[pl.BlockSpec((None, 2 * n_rows, hi - lo), lambda p: (p, 0, 0)) for lo, hi in splits],
        out_shape=[jax.ShapeDtypeStruct((HEAD_PAIRS, 2 * n_rows, hi - lo), F32) for lo, hi in splits],
    )(gen)


def _band_block_kinds():
    n_k = KEY_BLOCKS * TILE
    r = np.arange(TILE)[None, :]
    w = np.arange(n_k)[:, None]
    j = w - CHUNK * (r // CHUNK)
    ok = (j >= 0) & (j < BAND_PAST + CHUNK)
    plain = ok & (BAND_PAST + r - w >= REL_CLIP)
    kinds = []
    for a in range(n_k // KEY_STEP):
        rows = slice(a * KEY_STEP, (a + 1) * KEY_STEP)
        blocks = [(rows, slice(b * LANES, (b + 1) * LANES)) for b in range(TILE // LANES)]
        kinds.append(["skip" if not ok[blk].any() else "plain" if plain[blk].all() else "biased"
                      for blk in blocks])
    return kinds


def _band_prompt_kernel(x_ref, qt_ref, qm_ref, kp_ref, kc_ref, vtp_ref, vtc_ref,
                        bias_ref, mk_ref, mv_ref, wout_ref, gpost_ref, o_ref):
    i = pl.program_id(0)
    n_k = KEY_BLOCKS * TILE
    kinds = _band_block_kinds()
    w = lax.broadcasted_iota(jnp.int32, (n_k, LANES), 0)
    c = lax.broadcasted_iota(jnp.int32, (n_k, LANES), 1)
    row = lax.broadcasted_iota(jnp.int32, (LANES, TILE), 0)
    one_hot_row = jnp.where(row == 0, 1.0, 0.0).astype(BF16)
    lo = row < HEAD_DIM

    def key_step(a, s, vth, bias_rows, state):
        rows = slice(a * KEY_STEP, (a + 1) * KEY_STEP)
        es, alphas = [], []
        for b, kind in enumerate(kinds[a]):
            cols = slice(b * LANES, (b + 1) * LANES)
            if kind == "skip":
                es.append(jnp.zeros((KEY_STEP, LANES), F32))
                alphas.append(None)
                continue
            sj = s[rows, cols]
            if kind == "biased":
                sj = sj + bias_rows(rows, cols)
            mj = jnp.max(sj, axis=0, keepdims=True)
            if state[b] is None:
                e = jnp.exp(sj - mj)
                state[b] = [mj, jnp.sum(e, axis=0, keepdims=True), None]
                alphas.append(None)
            else:
                m_old, l_old, _ = state[b]
                m_new = jnp.maximum(m_old, mj)
                alpha = jnp.exp(m_old - m_new)
                e = jnp.exp(sj - m_new)
                state[b][:2] = [m_new, l_old * alpha + jnp.sum(e, axis=0, keepdims=True)]
                alphas.append(alpha)
            es.append(e)
        pv = _dot(vth[:, rows], jnp.concatenate(es, axis=1).astype(BF16))
        for b, kind in enumerate(kinds[a]):
            if kind != "skip":
                pv_b = pv[:, b * LANES:(b + 1) * LANES]
                state[b][2] = pv_b if alphas[b] is None else state[b][2] * alphas[b] + pv_b

    def tile(t):
        q_rows = slice(t * TILE, (t + 1) * TILE)
        k_rows = slice(t * TILE, t * TILE + n_k)
        first_key_tile = i * BAND_TILES + t - (KEY_BLOCKS - 1)
        pen = jnp.where((w < -first_key_tile * TILE) & (c == 0), NEG, 0.0).astype(BF16)
        outs = []
        for p0 in range(0, HEAD_PAIRS, GROUP_PAIRS):
            heads = []
            for p in range(p0, p0 + GROUP_PAIRS):
                k_win = jnp.concatenate([kp_ref[p], kc_ref[p]], axis=0)[k_rows]
                k_ext = jnp.concatenate([k_win, pen], axis=1)
                vt = jnp.concatenate([vtp_ref[p], vtc_ref[p]], axis=1)[:, k_rows]
                qt = qt_ref[p, :, q_rows]
                zero = jnp.zeros_like(qt)
                for hh in range(2):
                    qh = jnp.where(lo, qt, zero) if hh == 0 else jnp.where(lo, zero, qt)
                    s = _dot(k_ext, jnp.concatenate([qh, one_hot_row], axis=0))
                    vth = vt[hh * HEAD_DIM:(hh + 1) * HEAD_DIM, :]
                    bias_rows = functools.partial(
                        lambda rows, cols, p, base: bias_ref[p, base + rows.start:base + rows.stop, cols],
                        p=p, base=hh * n_k)
                    heads.append((s, vth, bias_rows, [None] * len(kinds[0])))
            for a in range(len(kinds)):
                for head in heads:
                    key_step(a, *head)
            outs += [jnp.concatenate([o * (1.0 / l) for _, l, o in state], axis=1) for *_, state in heads]
            yield
        band = jnp.concatenate(outs, axis=0).T.astype(BF16)
        mo = _mem_attend(qm_ref[q_rows, :], mk_ref[0].astype(BF16), mv_ref[0].astype(BF16))
        cat = jnp.concatenate([band, mo.astype(BF16)], axis=-1)
        o_ref[q_rows, :] = x_ref[q_rows, :] + _rms(_dot(cat, wout_ref[...]), gpost_ref[...])

    _interleave([tile(t) for t in range(BAND_TILES)])


def _band_prompt(x, qt3, qm, k3, vt3, bias, mkt, mvt, wout, gpost):
    n = x.shape[0]
    rows = BAND_TILES * TILE
    assert rows == BAND_PAST and n % rows == 0
    past = lambda i: jnp.maximum(i - 1, 0)
    return _call(
        _band_prompt_kernel, "band_prompt",
        grid=(n // rows,),
        in_specs=[pl.BlockSpec((rows, D_MODEL), lambda i: (i, 0)),
                  pl.BlockSpec((HEAD_PAIRS, LANES, rows), lambda i: (0, 0, i)),
                  pl.BlockSpec((rows, MEM_W), lambda i: (i, 0)),
                  pl.BlockSpec((HEAD_PAIRS, rows, LANES), lambda i: (0, past(i), 0)),
                  pl.BlockSpec((HEAD_PAIRS, rows, LANES), lambda i: (0, i, 0)),
                  pl.BlockSpec((HEAD_PAIRS, LANES, rows), lambda i: (0, 0, past(i))),
                  pl.BlockSpec((HEAD_PAIRS, LANES, rows), lambda i: (0, 0, i)),
                  _const_spec((HEAD_PAIRS, 2 * KEY_BLOCKS * TILE, TILE)),
                  _mem_spec(mkt, 1, 1), _mem_spec(mvt, 1, 1),
                  _const_spec((B_W + MEM_W, D_MODEL)), _const_spec((1, D_MODEL))],
        out_specs=pl.BlockSpec((rows, D_MODEL), lambda i: (i, 0)),
        out_shape=jax.ShapeDtypeStruct((n, D_MODEL), F32),
    )(x, qt3, qm, k3, k3, vt3, vt3, bias, mkt, mvt, wout, gpost)


def _band_sample_kernel(x_ref, q_ref, qm_ref, kn_ref, vn_ref, ck_ref, cv_ref, bc_ref, bn_ref,
                        mk_ref, mv_ref, wout_ref, gpost_ref, o_ref, *, seqs, seq_len):
    lane = lax.broadcasted_iota(jnp.int32, (1, LANES), 1)
    lo = lane < HEAD_DIM

    def pair_attend(b, p, out):
        rows = slice(b * seq_len, (b + 1) * seq_len)
        hd = slice(p * LANES, (p + 1) * LANES)
        qp = q_ref[p, rows, :]
        zero = jnp.zeros_like(qp)
        qs = jnp.concatenate([jnp.where(lo, qp, zero), jnp.where(lo, zero, qp)], axis=0)
        sc = _dot(qs, ck_ref[b, hd, :].astype(BF16)) + bc_ref[p]
        sn = _dot_nt(qs, kn_ref[p, rows, :]) + bn_ref[p]
        yield
        m = jnp.maximum(_lmax(sc), _lmax(sn))
        ec = jnp.exp(sc - m)
        en = jnp.exp(sn - m)
        l = _lsum(ec) + _lsum(en)
        yield
        o = (_dot_nt(ec.astype(BF16), cv_ref[b, hd, :].astype(BF16))
             + _dot(en.astype(BF16), vn_ref[p, rows, :]))
        yield
        o = o * (1.0 / l)
        out[p] = jnp.where(lo, o[:seq_len], o[seq_len:]).astype(BF16)

    rows_out = []
    for b in range(seqs):
        band = [None] * HEAD_PAIRS
        _interleave([pair_attend(b, p, band) for p in range(HEAD_PAIRS)])
        rows = slice(b * seq_len, (b + 1) * seq_len)
        mo = _mem_attend(qm_ref[rows, :], mk_ref[b].astype(BF16), mv_ref[b].astype(BF16))
        rows_out.append(jnp.concatenate(band + [mo.astype(BF16)], axis=-1))
    cat = jnp.concatenate(rows_out, axis=0)
    o_ref[...] = x_ref[...] + _rms(_dot(cat, wout_ref[...]), gpost_ref[...])


def _band_sample(x, q3, qm, kn3, vn3, ckt, cvt, bias_c, bias_n, mkt, mvt, wout, gpost, *, seqs=4):
    n = x.shape[0]
    n_seq, past = ckt.shape[0], ckt.shape[2]
    seq_len = n // n_seq
    rows = seqs * seq_len
    pair_rows_spec = pl.BlockSpec((HEAD_PAIRS, rows, LANES), lambda i: (0, i, 0))
    cache_spec = pl.BlockSpec((seqs, B_W, past), lambda i: (i, 0, 0))
    return _call(
        functools.partial(_band_sample_kernel, seqs=seqs, seq_len=seq_len), "band_sample",
        grid=(n_seq // seqs,),
        in_specs=[pl.BlockSpec((rows, D_MODEL), lambda i: (i, 0)),
                  pair_rows_spec,
                  pl.BlockSpec((rows, MEM_W), lambda i: (i, 0)),
                  pair_rows_spec, pair_rows_spec,
                  cache_spec, cache_spec,
                  _const_spec((HEAD_PAIRS, 2 * seq_len, past)),
                  _const_spec((HEAD_PAIRS, 2 * seq_len, seq_len)),
                  _mem_spec(mkt, seqs, 1), _mem_spec(mvt, seqs, 1),
                  _const_spec((B_W + MEM_W, D_MODEL)), _const_spec((1, D_MODEL))],
        out_specs=pl.BlockSpec((rows, D_MODEL), lambda i: (i, 0)),
        out_shape=jax.ShapeDtypeStruct((n, D_MODEL), F32),
    )(x, q3, qm, kn3, vn3, ckt, cvt, bias_c, bias_n, mkt, mvt, wout, gpost)


def _spatial_tile(w_s, b_s, period):
    tril = jnp.tril(jnp.ones((GM_CHUNK, GM_CHUNK), dtype=bool))
    w = jnp.where(tril, w_s, jnp.zeros((), w_s.dtype))[:, :period, :period]
    eye = jnp.eye(TILE // period, dtype=w.dtype)
    s_mat = jnp.einsum("ab,gts->gatbs", eye, w).reshape(GM_GROUPS, TILE, TILE)
    rows = jnp.tile(b_s[:, :period], (1, TILE // period))
    bs = jnp.repeat(rows.T, GM_GW, axis=1)
    return s_mat.astype(BF16), bs


def _heads_last(t, lead):
    pos = t.shape[-1]
    t = t.reshape(lead + (-1, HEAD_DIM, pos))
    nd = len(lead)
    return jnp.transpose(t, tuple(range(nd)) + (nd + 2, nd, nd + 1))


def _positions_last(c):
    nd = c.ndim
    t = jnp.transpose(c, tuple(range(nd - 3)) + (nd - 2, nd - 1, nd - 3))
    return t.reshape(c.shape[:-3] + (c.shape[-2] * c.shape[-1], c.shape[-3]))


def kernel(x_prompt, x_sample, cache_mem_k, cache_mem_v, cache_band_k, cache_band_v, mem_prompt,
           g_mix_pre, g_mix_post, g_ffn_pre, g_ffn_post, g_mem, w_mem_kv,
           w_in_a, g_gm_ln, b_gm_ln, w_spatial, b_spatial, w_out_a,
           g_kv, w_kv, w_in_b, rel_bias, w_out_b, w_ff1, w_ff2):
    seq = x_prompt.shape[1]
    n_seq, seq_len = x_sample.shape[0], x_sample.shape[1]
    past = cache_band_k.shape[1]
    vec = lambda a: a.reshape(1, -1)
    stack = lambda a: a.reshape(DEPTH, 1, -1)

    win_a = w_in_a[0].astype(BF16)
    wout_a = w_out_a[0].astype(BF16)
    wk = w_kv[:, :B_W].astype(BF16)
    wv = w_kv[:, B_W:].astype(BF16)
    wkt, wvt = wk.T, wv.T
    win_b = w_in_b[0].astype(BF16)
    wqt, wqm = win_b[:, :B_W].T, win_b[:, B_W:]
    wout_b = w_out_b[0].astype(BF16)
    w1 = w_ff1.astype(BF16)
    w2 = w_ff2.astype(BF16)
    ln_g, ln_b = vec(g_gm_ln[0]), vec(b_gm_ln[0])
    gkv, gpre_b = vec(g_kv), vec(g_mix_pre[1])
    gf_pre, gf_post = stack(g_ffn_pre), stack(g_ffn_post)

    def ffn(x, l):
        return _ffn(x, gf_pre, gf_post, w1, w2, l)

    mem_kt, mem_vt = _memkv(mem_prompt[0], g_mem, jnp.swapaxes(w_mem_kv, 1, 2).astype(BF16))
    s_p, bs_p = _spatial_tile(w_spatial[0], b_spatial[0], GM_CHUNK)
    x = x_prompt[0]
    x, = _mixer_a(x, vec(g_mix_pre[0]), vec(g_mix_post[0]), win_a, ln_g, ln_b, s_p, bs_p,
                  mem_kt, mem_vt, wout_a, rows_per_mem=TILE, emit_v=False, tiles=4)
    x = ffn(x, 0)
    qt3, qm, k3, vt3 = _proj_b(x, gkv, gpre_b, (wqt, wqm, wk, wvt), mode="prompt", rows=2 * PROJ_ROWS)
    n_keep = min(BAND_PAST, seq)
    kt_tail, vt_tail = _proj_b(x, gkv, gpre_b, (wkt, wvt), mode="tail", rows=TILE,
                               row_offset=seq - n_keep, n_rows=n_keep)
    n_k = BAND_PAST + TILE
    bias_p, = _rel_bias_tables(rel_bias[0], TILE, n_k, ((0, TILE),), band=True, keys_on_rows=True)
    x = _band_prompt(x, qt3, qm, k3, vt3, bias_p, mem_kt, mem_vt, wout_b, vec(g_mix_post[1]))
    y_prompt = ffn(x, 1)[None]

    s_s, bs_s = _spatial_tile(w_spatial[0], b_spatial[0], seq_len)
    xs = x_sample.reshape(n_seq * seq_len, D_MODEL)
    cmkt, cmvt = _positions_last(cache_mem_k), _positions_last(cache_mem_v)
    xs, v_rows = _mixer_a(xs, vec(g_mix_pre[0]), vec(g_mix_post[0]), win_a, ln_g, ln_b, s_s, bs_s,
                          cmkt, cmvt, wout_a, rows_per_mem=seq_len, emit_v=True)
    xs = ffn(xs, 0)
    q3s, qms, kn3, vn3, k_new, v_new = _proj_b(xs, gkv, gpre_b, (win_b, wk, wv), mode="sample",
                                               rows=512, seq_len=seq_len)
    bias_c, bias_n = _rel_bias_tables(rel_bias[0], seq_len, past + seq_len,
                                      ((0, past), (past, past + seq_len)), band=False, keys_on_rows=False)
    xs = _band_sample(xs, q3s, qms, kn3, vn3, _positions_last(cache_band_k), _positions_last(cache_band_v),
                      bias_c, bias_n, cmkt, cmvt, wout_b, vec(g_mix_post[1]))
    y_sample = ffn(xs, 1).reshape(n_seq, seq_len, D_MODEL)

    return (y_prompt, y_sample,
            _heads_last(mem_kt, (DEPTH, 1)), _heads_last(mem_vt, (DEPTH, 1)),
            _heads_last(kt_tail, (1,)), _heads_last(vt_tail, (1,)),
            v_rows.reshape(1, n_seq, seq_len, GM_W),
            jnp.swapaxes(k_new, 1, 2), jnp.swapaxes(v_new, 1, 2))
```

```python
import functools

import jax
import jax.numpy as jnp
import numpy as np
from jax import lax
from jax.experimental import pallas as pl
from jax.experimental.pallas import tpu as pltpu

D_MODEL = 1024
DEPTH = 2
CHUNK = 64
HEAD_DIM = 64
GM_CHUNK = 128
GM_GROUPS = 4
GM_W = 768
GM_GW = GM_W // GM_GROUPS
MEM_LEN = 256
MEM_HEADS = 4
MEM_W = MEM_HEADS * HEAD_DIM
B_HEADS = 12
B_W = B_HEADS * HEAD_DIM
BAND_PAST = 512
REL_CLIP = 128
D_FF = 4 * D_MODEL
EPS = 1e-6

LANES = 128
HEAD_PAIRS = B_W // LANES
Q_SCALE = HEAD_DIM ** -0.5
NEG = -1e30
TILE = 256
KEY_BLOCKS = BAND_PAST // TILE + 1
KEY_STEP = 128
KEY_BLOCK = 128
PROJ_ROWS = 512
BF16_ROWS = 16
OUT_ROWS = HEAD_DIM + 8
BAND_TILES = BAND_PAST // TILE
GROUP_PAIRS = 2
BIAS_PERIOD = 1024
VMEM_LIMIT = 56 * 1024 * 1024

BF16 = jnp.bfloat16
F32 = jnp.float32


def _dot(a, b):
    return jnp.dot(a, b, preferred_element_type=F32)


def _dot_nt(a, b):
    return lax.dot_general(a, b, (((1,), (1,)), ((), ())), preferred_element_type=F32)


def _rms(x, g):
    ms = jnp.mean(x * x, axis=-1, keepdims=True)
    return x * lax.rsqrt(ms + EPS) * g


def _lsum(a):
    return jnp.sum(a, axis=-1, keepdims=True)


def _lmax(a):
    return jnp.max(a, axis=-1, keepdims=True)


def _const_spec(shape):
    nd = len(shape)
    return pl.BlockSpec(shape, lambda *_: (0,) * nd, pipeline_mode=pl.Buffered(1))


def _layer_spec(shape, layer):
    nd = len(shape)
    return pl.BlockSpec((None,) + shape, lambda *_: (layer,) + (0,) * nd, pipeline_mode=pl.Buffered(1))


def _call(body, name, flags=None, **kw):
    return pl.pallas_call(
        body, name=name,
        compiler_params=pltpu.CompilerParams(dimension_semantics=("parallel",),
                                             vmem_limit_bytes=VMEM_LIMIT, flags=flags), **kw)


def _memkv_kernel(mem_ref, g_ref, wt_ref, kt_ref, vt_ref):
    ht = _dot_nt(wt_ref[...], _rms(mem_ref[...], g_ref[...]).astype(BF16))
    kt_ref[0] = ht[:MEM_W]
    vt_ref[0] = ht[MEM_W:]


def _memkv(mem, g_mem, w_mem_kv_t):
    out = jax.ShapeDtypeStruct((DEPTH, 1, MEM_W, MEM_LEN), F32)
    return _call(
        _memkv_kernel, "mem_kv",
        grid=(DEPTH,),
        in_specs=[
            pl.BlockSpec((MEM_LEN, D_MODEL), lambda l: (0, 0)),
            pl.BlockSpec((None, 1, D_MODEL), lambda l: (l, 0, 0)),
            pl.BlockSpec((None, 2 * MEM_W, D_MODEL), lambda l: (l, 0, 0)),
        ],
        out_specs=[pl.BlockSpec((None, 1, MEM_W, MEM_LEN), lambda l: (l, 0, 0, 0))] * 2,
        out_shape=[out, out],
    )(mem, g_mem.reshape(DEPTH, 1, D_MODEL), w_mem_kv_t)


def _mem_attend(qb, kt, vt):
    r = qb.shape[0]
    lane = lax.broadcasted_iota(jnp.int32, (1, MEM_W), 1)
    masks = [(lane >= h * HEAD_DIM) & (lane < (h + 1) * HEAD_DIM) for h in range(MEM_HEADS)]
    qs = jnp.concatenate([jnp.where(m, qb, jnp.zeros_like(qb)) for m in masks], axis=0)
    s = _dot(qs, kt)
    e = jnp.exp(s - _lmax(s))
    pv = _dot_nt(e.astype(BF16), vt) * (1.0 / _lsum(e))
    out = jnp.where(masks[0], pv[:r], 0.0)
    for h in range(1, MEM_HEADS):
        out = out + jnp.where(masks[h], pv[h * r:(h + 1) * r], 0.0)
    return out


def _gelu(x):
    return jax.nn.gelu(x, approximate=True)


def _interleave(stage_lists):
    live = list(stage_lists)
    while live:
        for g in list(live):
            if next(g, StopIteration) is StopIteration:
                live.remove(g)


def _mixer_a_kernel(x_ref, gpre_ref, gpost_ref, win_ref, gln_ref, bln_ref, s_ref, bs_ref,
                    mk_ref, mv_ref, wout_ref, o_ref, *v_out, rows_per_mem, tiles, shared_mem):
    nt = GM_W // LANES
    seqs = TILE // rows_per_mem
    lane = lax.broadcasted_iota(jnp.int32, (1, LANES), 1)
    lo = lane < (GM_GW - LANES)
    inv = 1.0 / GM_GW

    def group_stat(a):
        s0 = _lsum(a[0] + jnp.where(lo, a[1], 0.0)) * inv
        s1 = _lsum(jnp.where(lo, 0.0, a[1]) + a[2]) * inv
        s2 = _lsum(a[3] + jnp.where(lo, a[4], 0.0)) * inv
        s3 = _lsum(jnp.where(lo, 0.0, a[4]) + a[5]) * inv
        return [s0, jnp.where(lo, s0, s1), s1, s2, jnp.where(lo, s2, s3), s3]

    def tile(t):
        rows = slice(t * TILE, (t + 1) * TILE)
        x = x_ref[rows, :]
        z = _dot(_rms(x, gpre_ref[...]).astype(BF16), win_ref[...])
        yield
        u = [_gelu(z[:, j * LANES:(j + 1) * LANES]) for j in range(nt)]
        g = [_gelu(z[:, GM_W + j * LANES:GM_W + (j + 1) * LANES]) for j in range(nt)]
        mu = group_stat(g)
        c = [g[j] - mu[j] for j in range(nt)]
        var = group_stat([cj * cj for cj in c])
        gln = gln_ref[...]
        bln = bln_ref[...]
        vn = [c[j] * lax.rsqrt(var[j] + EPS) * gln[:, j * LANES:(j + 1) * LANES]
              + bln[:, j * LANES:(j + 1) * LANES] for j in range(nt)]
        if v_out:
            v_out[0][rows, :] = jnp.concatenate(vn, axis=-1)
        yield
        vb = [a.astype(BF16) for a in vn]
        win = [(0, 1), (1, 2), (3, 4), (4, 5)]
        m = [_dot(s_ref[k], jnp.concatenate([vb[a], vb[b]], axis=-1)) for k, (a, b) in enumerate(win)]
        mixed = [m[0][:, :LANES], jnp.where(lo, m[0][:, LANES:], m[1][:, :LANES]), m[1][:, LANES:],
                 m[2][:, :LANES], jnp.where(lo, m[2][:, LANES:], m[3][:, :LANES]), m[3][:, LANES:]]
        bs = bs_ref[...]
        gm = [u[j] * (mixed[j] + bs[:, j * LANES:(j + 1) * LANES]) for j in range(nt)]
        yield
        qm = (z[:, 2 * GM_W:] * Q_SCALE).astype(BF16)
        mo = []
        for b in range(seqs):
            r = slice(b * rows_per_mem, (b + 1) * rows_per_mem)
            mi = 0 if shared_mem else t * seqs + b
            mo.append(_mem_attend(qm[r], mk_ref[mi].astype(BF16), mv_ref[mi].astype(BF16)))
        mo = mo[0] if len(mo) == 1 else jnp.concatenate(mo, axis=0)
        yield
        cat = jnp.concatenate([a.astype(BF16) for a in gm] + [mo.astype(BF16)], axis=-1)
        o_ref[rows, :] = x + _rms(_dot(cat, wout_ref[...]), gpost_ref[...])

    _interleave([tile(t) for t in range(tiles)])


def _mem_spec(mem, seqs, layer):
    if mem.shape[1] == 1:
        return pl.BlockSpec((None, 1, MEM_W, MEM_LEN), lambda i: (layer, 0, 0, 0))
    return pl.BlockSpec((None, seqs, MEM_W, MEM_LEN), lambda i: (layer, i, 0, 0))


def _mixer_a(x, gpre, gpost, win, gln, bln, s_mat, bs, mkt, mvt, wout, *, rows_per_mem, emit_v, tiles=2):
    n = x.shape[0]
    seqs = tiles * TILE // rows_per_mem
    row = lambda w: pl.BlockSpec((tiles * TILE, w), lambda i: (i, 0))
    out_shape = [jax.ShapeDtypeStruct((n, D_MODEL), F32)]
    out_specs = [row(D_MODEL)]
    if emit_v:
        out_shape.append(jax.ShapeDtypeStruct((n, GM_W), F32))
        out_specs.append(row(GM_W))
    return _call(
        functools.partial(_mixer_a_kernel, rows_per_mem=rows_per_mem, tiles=tiles,
                          shared_mem=mkt.shape[1] == 1), "mixer_a",
        grid=(n // (tiles * TILE),),
        in_specs=[
            row(D_MODEL),
            _const_spec((1, D_MODEL)), _const_spec((1, D_MODEL)),
            _const_spec((D_MODEL, 2 * GM_W + MEM_W)),
            _const_spec((1, GM_W)), _const_spec((1, GM_W)),
            _const_spec((GM_GROUPS, TILE, TILE)), _const_spec((TILE, GM_W)),
            _mem_spec(mkt, seqs, 0), _mem_spec(mvt, seqs, 0),
            _const_spec((GM_W + MEM_W, D_MODEL)),
        ],
        out_specs=out_specs,
        out_shape=out_shape,
    )(x, gpre, gpost, win, gln, bln, s_mat, bs, mkt, mvt, wout)


def _ffn_kernel(x_ref, gpre_ref, gpost_ref, w1_ref, w2_ref, o_ref, *, rows, tiles, ff_chunk):
    def tile(t):
        r = slice(t * rows, (t + 1) * rows)
        x = x_ref[r, :]
        xn = _rms(x, gpre_ref[...]).astype(BF16)
        acc = jnp.zeros(x.shape, F32)
        yield
        for c in range(D_FF // ff_chunk):
            h = _dot(xn, w1_ref[:, c * ff_chunk:(c + 1) * ff_chunk])
            h = jnp.square(jnp.maximum(h, 0.0)).astype(BF16)
            acc = acc + _dot(h, w2_ref[c * ff_chunk:(c + 1) * ff_chunk, :])
            yield
        o_ref[r, :] = x + _rms(acc, gpost_ref[...])

    _interleave([tile(t) for t in range(tiles)])


def _ffn(x, gpre, gpost, w1, w2, layer, *, rows=512, tiles=2, ff_chunk=1024):
    n = x.shape[0]
    row = pl.BlockSpec((tiles * rows, D_MODEL), lambda i: (i, 0))
    return _call(
        functools.partial(_ffn_kernel, rows=rows, tiles=tiles, ff_chunk=ff_chunk), "ffn",
        grid=(n // (tiles * rows),),
        in_specs=[row, _layer_spec((1, D_MODEL), layer), _layer_spec((1, D_MODEL), layer),
                  _layer_spec((D_MODEL, D_FF), layer), _layer_spec((D_FF, D_MODEL), layer)],
        out_specs=row,
        out_shape=jax.ShapeDtypeStruct((n, D_MODEL), F32),
    )(x, gpre, gpost, w1, w2)


def _proj_b_kernel(x_ref, gkv_ref, gpre_ref, *refs, mode, seq_len):
    rows = x_ref.shape[0]

    def normed(r):
        x = x_ref[r, :]
        xh = x * lax.rsqrt(jnp.mean(x * x, axis=-1, keepdims=True) + EPS)
        return (xh * gkv_ref[...]).astype(BF16), (xh * gpre_ref[...]).astype(BF16)

    if mode == "tail":
        wkt_ref, wvt_ref, kt_ref, vt_ref = refs
        xkv, _ = normed(slice(None))
        kt_ref[...] = _dot_nt(wkt_ref[...], xkv)
        vt_ref[...] = _dot_nt(wvt_ref[...], xkv)
        return
    if mode == "prompt":
        wqt_ref, wqm_ref, wk_ref, wvt_ref, qt_ref, qm_ref, k_ref, vt_ref = refs

        def tile(t):
            r = slice(t * PROJ_ROWS, (t + 1) * PROJ_ROWS)
            xkv, xq = normed(r)
            yield
            qt = _dot_nt(wqt_ref[...], xq) * Q_SCALE
            for p in range(HEAD_PAIRS):
                qt_ref[p, :, r] = qt[p * LANES:(p + 1) * LANES, :].astype(BF16)
            yield
            vt = _dot_nt(wvt_ref[...], xkv)
            for p in range(HEAD_PAIRS):
                vt_ref[p, :, r] = vt[p * LANES:(p + 1) * LANES, :].astype(BF16)
            yield
            k = _dot(xkv, wk_ref[...])
            for p in range(HEAD_PAIRS):
                k_ref[p, r, :] = k[:, p * LANES:(p + 1) * LANES].astype(BF16)
            yield
            qm_ref[r, :] = (_dot(xq, wqm_ref[...]) * Q_SCALE).astype(BF16)

        _interleave([tile(t) for t in range(rows // PROJ_ROWS)])
        return
    win_ref, wk_ref, wv_ref, q_ref, qm_ref, k_ref, v_ref, k4_ref, v4_ref = refs
    xkv, xq = normed(slice(None))
    z = _dot(xq, win_ref[...]) * Q_SCALE
    qm_ref[...] = z[:, B_W:].astype(BF16)
    k = _dot(xkv, wk_ref[...])
    v = _dot(xkv, wv_ref[...])
    for p in range(HEAD_PAIRS):
        cols = slice(p * LANES, (p + 1) * LANES)
        q_ref[p] = z[:, cols].astype(BF16)
        k_ref[p] = k[:, cols].astype(BF16)
        v_ref[p] = v[:, cols].astype(BF16)
    for b in range(rows // seq_len):
        for h in range(B_HEADS):
            r, c = slice(b * seq_len, (b + 1) * seq_len), slice(h * HEAD_DIM, (h + 1) * HEAD_DIM)
            k4_ref[b, h] = k[r, c]
            v4_ref[b, h] = v[r, c]


def _proj_b(x, gkv, gpre, weights, *, mode, rows, row_offset=0, n_rows=None, seq_len=None):
    n = x.shape[0] if n_rows is None else n_rows
    off = row_offset // rows
    pair_rows = jax.ShapeDtypeStruct((HEAD_PAIRS, n, LANES), BF16)
    pair_rows_spec = pl.BlockSpec((HEAD_PAIRS, rows, LANES), lambda i: (0, i, 0))
    pair_cols = jax.ShapeDtypeStruct((HEAD_PAIRS, LANES, n), BF16)
    pair_cols_spec = pl.BlockSpec((HEAD_PAIRS, LANES, rows), lambda i: (0, 0, i))
    qm = jax.ShapeDtypeStruct((n, MEM_W), BF16)
    qm_spec = pl.BlockSpec((rows, MEM_W), lambda i: (i, 0))
    if mode == "tail":
        out_shape = [jax.ShapeDtypeStruct((B_W, n), F32)] * 2
        out_specs = [pl.BlockSpec((B_W, rows), lambda i: (0, i))] * 2
    elif mode == "prompt":
        out_shape = [pair_cols, qm, pair_rows, pair_cols]
        out_specs = [pair_cols_spec, qm_spec, pair_rows_spec, pair_cols_spec]
    else:
        seqs = rows // seq_len
        per_head = jax.ShapeDtypeStruct((n // seq_len, B_HEADS, seq_len, HEAD_DIM), F32)
        per_head_spec = pl.BlockSpec((seqs, B_HEADS, seq_len, HEAD_DIM), lambda i: (i, 0, 0, 0))
        out_shape = [pair_rows, qm, pair_rows, pair_rows, per_head, per_head]
        out_specs = [pair_rows_spec, qm_spec, pair_rows_spec, pair_rows_spec, per_head_spec, per_head_spec]
    return _call(
        functools.partial(_proj_b_kernel, mode=mode, seq_len=seq_len), "proj_b_" + mode,
        grid=(n // rows,),
        in_specs=[pl.BlockSpec((rows, D_MODEL), lambda i: (i + off, 0)),
                  _const_spec((1, D_MODEL)), _const_spec((1, D_MODEL))]
                 + [_const_spec(w.shape) for w in weights],
        out_specs=out_specs,
        out_shape=out_shape,
    )(x, gkv, gpre, *weights)


def _bias_kernel(g_ref, *o_refs, n_q, n_k, splits, band, keys_on_rows):
    n_rows, shift = (n_k, n_q) if keys_on_rows else (n_q, BIAS_PERIOD - (n_q - 1))
    for hh in range(2):
        x = jnp.broadcast_to(g_ref[hh:hh + 1, :], (n_rows, BIAS_PERIOD))
        t = pltpu.roll(x, shift, 1, stride=1, stride_axis=0)
        if band:
            a = lax.broadcasted_iota(jnp.int32, (n_rows, BIAS_PERIOD), 0)
            b = lax.broadcasted_iota(jnp.int32, (n_rows, BIAS_PERIOD), 1)
            r, w = (b, a) if keys_on_rows else (a, b)
            j = w - (r - (r & (CHUNK - 1)))
            t = jnp.where((j >= 0) & (j < BAND_PAST + CHUNK), t, NEG)
        for o_ref, (lo, hi) in zip(o_refs, splits):
            o_ref[hh * n_rows:(hh + 1) * n_rows, :] = t[:, lo:hi]


def _rel_bias_tables(rel_bias, n_q, n_k, splits, *, band, keys_on_rows):
    rel_bias = rel_bias - rel_bias[:, -1:]
    c0 = n_k - 1
    far = jnp.broadcast_to(rel_bias[:, -1:], (B_HEADS, c0 - REL_CLIP))
    near = jnp.broadcast_to(rel_bias[:, :1], (B_HEADS, BIAS_PERIOD - (c0 - REL_CLIP) - (2 * REL_CLIP + 1)))
    gen = [near, rel_bias, far] if keys_on_rows else [far, rel_bias[:, ::-1], near]
    gen = jnp.concatenate(gen, axis=1).reshape(HEAD_PAIRS, 2, BIAS_PERIOD)
    n_rows = n_k if keys_on_rows else n_q
    return _call(
        functools.partial(_bias_kernel, n_q=n_q, n_k=n_k, splits=splits, band=band,
                          keys_on_rows=keys_on_rows), "rel_bias",
        grid=(HEAD_PAIRS,),
        in_specs=[pl.BlockSpec((None, 2, BIAS_PERIOD), lambda p: (p, 0, 0))],
        out_specs=[pl.BlockSpec((None, 2 * n_rows, hi - lo), lambda p: (p, 0, 0)) for lo, hi in splits],
        out_shape=[jax.ShapeDtypeStruct((HEAD_PAIRS, 2 * n_rows, hi - lo), F32) for lo, hi in splits],
    )(gen)


def _band_block_kinds():
    n_k = KEY_BLOCKS * TILE
    r = np.arange(TILE)[None, :]
    w = np.arange(n_k)[:, None]
    j = w - CHUNK * (r // CHUNK)
    ok = (j >= 0) & (j < BAND_PAST + CHUNK)
    plain = ok & (BAND_PAST + r - w >= REL_CLIP)
    kinds = []
    for a in range(n_k // KEY_BLOCK):
        rows = slice(a * KEY_BLOCK, (a + 1) * KEY_BLOCK)
        blocks = [(rows, slice(b * LANES, (b + 1) * LANES)) for b in range(TILE // LANES)]
        kinds.append(["skip" if not ok[blk].any() else "plain" if plain[blk].all() else "biased"
                      for blk in blocks])
    return kinds


def _band_prompt_kernel(x_ref, qt_ref, qm_ref, kp_ref, kc_ref, vtp_ref, vtc_ref,
                        bias_ref, mk_ref, mv_ref, wout_ref, gpost_ref, o_ref):
    i = pl.program_id(0)
    n_k = KEY_BLOCKS * TILE
    kinds = _band_block_kinds()
    w = lax.broadcasted_iota(jnp.int32, (n_k, LANES), 0)
    c = lax.broadcasted_iota(jnp.int32, (n_k, LANES), 1)
    ones_rows = jnp.where(lax.broadcasted_iota(jnp.int32, (BF16_ROWS, n_k), 0) == 0, 1.0, 0.0).astype(BF16)
    row = lax.broadcasted_iota(jnp.int32, (LANES, TILE), 0)
    one_hot_row = jnp.where(row == 0, 1.0, 0.0).astype(BF16)
    lo = row < HEAD_DIM

    def key_step(a, s, vth, bias_rows, state):
        blocks = range(a * KEY_STEP // KEY_BLOCK, (a + 1) * KEY_STEP // KEY_BLOCK)
        es, alphas, active = [], [], []
        for b in range(TILE // LANES):
            cols = slice(b * LANES, (b + 1) * LANES)
            sjs = {}
            for blk in blocks:
                if kinds[blk][b] != "skip":
                    rows = slice(blk * KEY_BLOCK, (blk + 1) * KEY_BLOCK)
                    sjs[blk] = s[rows, cols] + bias_rows(rows, cols) if kinds[blk][b] == "biased" else s[rows, cols]
            active.append(bool(sjs))
            alphas.append(None)
            if not sjs:
                es.append(jnp.zeros((KEY_STEP, LANES), F32))
                continue
            mj = jnp.max(functools.reduce(jnp.maximum, sjs.values()), axis=0, keepdims=True)
            if state[b] is None:
                state[b] = [mj, None]
            else:
                m_new = jnp.maximum(state[b][0], mj)
                alphas[b] = jnp.exp(state[b][0] - m_new)
                state[b][0] = m_new
            es.append(jnp.concatenate(
                [jnp.exp(sjs[blk] - state[b][0]) if blk in sjs else jnp.zeros((KEY_BLOCK, LANES), F32)
                 for blk in blocks], axis=0))
        rows = slice(a * KEY_STEP, (a + 1) * KEY_STEP)
        pv = _dot(vth[:, rows], jnp.concatenate(es, axis=1).astype(BF16))[:OUT_ROWS]
        for b in range(TILE // LANES):
            if active[b]:
                pv_b = pv[:, b * LANES:(b + 1) * LANES]
                state[b][1] = pv_b if alphas[b] is None else state[b][1] * alphas[b] + pv_b

    def tile(t):
        q_rows = slice(t * TILE, (t + 1) * TILE)
        k_rows = slice(t * TILE, t * TILE + n_k)
        first_key_tile = i * BAND_TILES + t - (KEY_BLOCKS - 1)
        pen = jnp.where((w < -first_key_tile * TILE) & (c == 0), NEG, 0.0).astype(BF16)
        outs = []
        for p0 in range(0, HEAD_PAIRS, GROUP_PAIRS):
            heads = []
            for p in range(p0, p0 + GROUP_PAIRS):
                k_win = jnp.concatenate([kp_ref[p], kc_ref[p]], axis=0)[k_rows]
                k_ext = jnp.concatenate([k_win, pen], axis=1)
                vt = jnp.concatenate([vtp_ref[p], vtc_ref[p]], axis=1)[:, k_rows]
                qt = qt_ref[p, :, q_rows]
                zero = jnp.zeros_like(qt)
                for hh in range(2):
                    qh = jnp.where(lo, qt, zero) if hh == 0 else jnp.where(lo, zero, qt)
                    s = _dot(k_ext, jnp.concatenate([qh, one_hot_row], axis=0))
                    vth = jnp.concatenate([vt[hh * HEAD_DIM:(hh + 1) * HEAD_DIM, :], ones_rows], axis=0)
                    bias_rows = functools.partial(
                        lambda rows, cols, p, base: bias_ref[p, base + rows.start:base + rows.stop, cols],
                        p=p, base=hh * n_k)
                    heads.append((s, vth, bias_rows, [None] * (TILE // LANES)))
            for a in range(n_k // KEY_STEP):
                for head in heads:
                    key_step(a, *head)
            outs += [jnp.concatenate([o[:HEAD_DIM] * (1.0 / o[HEAD_DIM:HEAD_DIM + 1]) for _, o in state], axis=1)
                     for *_, state in heads]
            yield
        band = jnp.concatenate(outs, axis=0).T.astype(BF16)
        mo = _mem_attend(qm_ref[q_rows, :], mk_ref[0].astype(BF16), mv_ref[0].astype(BF16))
        cat = jnp.concatenate([band, mo.astype(BF16)], axis=-1)
        o_ref[q_rows, :] = x_ref[q_rows, :] + _rms(_dot(cat, wout_ref[...]), gpost_ref[...])

    _interleave([tile(t) for t in range(BAND_TILES)])


def _band_prompt(x, qt3, qm, k3, vt3, bias, mkt, mvt, wout, gpost):
    n = x.shape[0]
    rows = BAND_TILES * TILE
    assert rows == BAND_PAST and n % rows == 0
    past = lambda i: jnp.maximum(i - 1, 0)
    return _call(
        _band_prompt_kernel, "band_prompt",
        grid=(n // rows,),
        in_specs=[pl.BlockSpec((rows, D_MODEL), lambda i: (i, 0)),
                  pl.BlockSpec((HEAD_PAIRS, LANES, rows), lambda i: (0, 0, i)),
                  pl.BlockSpec((rows, MEM_W), lambda i: (i, 0)),
                  pl.BlockSpec((HEAD_PAIRS, rows, LANES), lambda i: (0, past(i), 0)),
                  pl.BlockSpec((HEAD_PAIRS, rows, LANES), lambda i: (0, i, 0)),
                  pl.BlockSpec((HEAD_PAIRS, LANES, rows), lambda i: (0, 0, past(i))),
                  pl.BlockSpec((HEAD_PAIRS, LANES, rows), lambda i: (0, 0, i)),
                  _const_spec((HEAD_PAIRS, 2 * KEY_BLOCKS * TILE, TILE)),
                  _mem_spec(mkt, 1, 1), _mem_spec(mvt, 1, 1),
                  _const_spec((B_W + MEM_W, D_MODEL)), _const_spec((1, D_MODEL))],
        out_specs=pl.BlockSpec((rows, D_MODEL), lambda i: (i, 0)),
        out_shape=jax.ShapeDtypeStruct((n, D_MODEL), F32),
    )(x, qt3, qm, k3, k3, vt3, vt3, bias, mkt, mvt, wout, gpost)


def _band_sample_kernel(x_ref, q_ref, qm_ref, kn_ref, vn_ref, ck_ref, cv_ref, bc_ref, bn_ref,
                        mk_ref, mv_ref, wout_ref, gpost_ref, o_ref, *, seqs, seq_len):
    lane = lax.broadcasted_iota(jnp.int32, (1, LANES), 1)
    lo = lane < HEAD_DIM

    def pair_attend(b, p, out):
        rows = slice(b * seq_len, (b + 1) * seq_len)
        hd = slice(p * LANES, (p + 1) * LANES)
        qp = q_ref[p, rows, :]
        zero = jnp.zeros_like(qp)
        qs = jnp.concatenate([jnp.where(lo, qp, zero), jnp.where(lo, zero, qp)], axis=0)
        sc = _dot(qs, ck_ref[b, hd, :].astype(BF16)) + bc_ref[p]
        sn = _dot_nt(qs, kn_ref[p, rows, :]) + bn_ref[p]
        yield
        m = jnp.maximum(_lmax(sc), _lmax(sn))
        ec = jnp.exp(sc - m)
        en = jnp.exp(sn - m)
        l = _lsum(ec) + _lsum(en)
        yield
        o = (_dot_nt(ec.astype(BF16), cv_ref[b, hd, :].astype(BF16))
             + _dot(en.astype(BF16), vn_ref[p, rows, :]))
        yield
        o = o * (1.0 / l)
        out[p] = jnp.where(lo, o[:seq_len], o[seq_len:]).astype(BF16)

    rows_out = []
    for b in range(seqs):
        band = [None] * HEAD_PAIRS
        _interleave([pair_attend(b, p, band) for p in range(HEAD_PAIRS)])
        rows = slice(b * seq_len, (b + 1) * seq_len)
        mo = _mem_attend(qm_ref[rows, :], mk_ref[b].astype(BF16), mv_ref[b].astype(BF16))
        rows_out.append(jnp.concatenate(band + [mo.astype(BF16)], axis=-1))
    cat = jnp.concatenate(rows_out, axis=0)
    o_ref[...] = x_ref[...] + _rms(_dot(cat, wout_ref[...]), gpost_ref[...])


def _band_sample(x, q3, qm, kn3, vn3, ckt, cvt, bias_c, bias_n, mkt, mvt, wout, gpost, *, seqs=4):
    n = x.shape[0]
    n_seq, past = ckt.shape[0], ckt.shape[2]
    seq_len = n // n_seq
    rows = seqs * seq_len
    pair_rows_spec = pl.BlockSpec((HEAD_PAIRS, rows, LANES), lambda i: (0, i, 0))
    cache_spec = pl.BlockSpec((seqs, B_W, past), lambda i: (i, 0, 0))
    return _call(
        functools.partial(_band_sample_kernel, seqs=seqs, seq_len=seq_len), "band_sample",
        grid=(n_seq // seqs,),
        in_specs=[pl.BlockSpec((rows, D_MODEL), lambda i: (i, 0)),
                  pair_rows_spec,
                  pl.BlockSpec((rows, MEM_W), lambda i: (i, 0)),
                  pair_rows_spec, pair_rows_spec,
                  cache_spec, cache_spec,
                  _const_spec((HEAD_PAIRS, 2 * seq_len, past)),
                  _const_spec((HEAD_PAIRS, 2 * seq_len, seq_len)),
                  _mem_spec(mkt, seqs, 1), _mem_spec(mvt, seqs, 1),
                  _const_spec((B_W + MEM_W, D_MODEL)), _const_spec((1, D_MODEL))],
        out_specs=pl.BlockSpec((rows, D_MODEL), lambda i: (i, 0)),
        out_shape=jax.ShapeDtypeStruct((n, D_MODEL), F32),
    )(x, q3, qm, kn3, vn3, ckt, cvt, bias_c, bias_n, mkt, mvt, wout, gpost)


def _spatial_tile(w_s, b_s, period):
    tril = jnp.tril(jnp.ones((GM_CHUNK, GM_CHUNK), dtype=bool))
    w = jnp.where(tril, w_s, jnp.zeros((), w_s.dtype))[:, :period, :period]
    eye = jnp.eye(TILE // period, dtype=w.dtype)
    s_mat = jnp.einsum("ab,gts->gatbs", eye, w).reshape(GM_GROUPS, TILE, TILE)
    rows = jnp.tile(b_s[:, :period], (1, TILE // period))
    bs = jnp.repeat(rows.T, GM_GW, axis=1)
    return s_mat.astype(BF16), bs


def _heads_last(t, lead):
    pos = t.shape[-1]
    t = t.reshape(lead + (-1, HEAD_DIM, pos))
    nd = len(lead)
    return jnp.transpose(t, tuple(range(nd)) + (nd + 2, nd, nd + 1))


def _positions_last(c):
    nd = c.ndim
    t = jnp.transpose(c, tuple(range(nd - 3)) + (nd - 2, nd - 1, nd - 3))
    return t.reshape(c.shape[:-3] + (c.shape[-2] * c.shape[-1], c.shape[-3]))


def kernel(x_prompt, x_sample, cache_mem_k, cache_mem_v, cache_band_k, cache_band_v, mem_prompt,
           g_mix_pre, g_mix_post, g_ffn_pre, g_ffn_post, g_mem, w_mem_kv,
           w_in_a, g_gm_ln, b_gm_ln, w_spatial, b_spatial, w_out_a,
           g_kv, w_kv, w_in_b, rel_bias, w_out_b, w_ff1, w_ff2):
    seq = x_prompt.shape[1]
    n_seq, seq_len = x_sample.shape[0], x_sample.shape[1]
    past = cache_band_k.shape[1]
    vec = lambda a: a.reshape(1, -1)
    stack = lambda a: a.reshape(DEPTH, 1, -1)

    win_a = w_in_a[0].astype(BF16)
    wout_a = w_out_a[0].astype(BF16)
    wk = w_kv[:, :B_W].astype(BF16)
    wv = w_kv[:, B_W:].astype(BF16)
    wkt, wvt = wk.T, wv.T
    win_b = w_in_b[0].astype(BF16)
    wqt, wqm = win_b[:, :B_W].T, win_b[:, B_W:]
    wout_b = w_out_b[0].astype(BF16)
    w1 = w_ff1.astype(BF16)
    w2 = w_ff2.astype(BF16)
    ln_g, ln_b = vec(g_gm_ln[0]), vec(b_gm_ln[0])
    gkv, gpre_b = vec(g_kv), vec(g_mix_pre[1])
    gf_pre, gf_post = stack(g_ffn_pre), stack(g_ffn_post)

    def ffn(x, l):
        return _ffn(x, gf_pre, gf_post, w1, w2, l)

    mem_kt, mem_vt = _memkv(mem_prompt[0], g_mem, jnp.swapaxes(w_mem_kv, 1, 2).astype(BF16))
    s_p, bs_p = _spatial_tile(w_spatial[0], b_spatial[0], GM_CHUNK)
    x = x_prompt[0]
    x, = _mixer_a(x, vec(g_mix_pre[0]), vec(g_mix_post[0]), win_a, ln_g, ln_b, s_p, bs_p,
                  mem_kt, mem_vt, wout_a, rows_per_mem=TILE, emit_v=False, tiles=4)
    x = ffn(x, 0)
    qt3, qm, k3, vt3 = _proj_b(x, gkv, gpre_b, (wqt, wqm, wk, wvt), mode="prompt", rows=2 * PROJ_ROWS)
    n_keep = min(BAND_PAST, seq)
    kt_tail, vt_tail = _proj_b(x, gkv, gpre_b, (wkt, wvt), mode="tail", rows=TILE,
                               row_offset=seq - n_keep, n_rows=n_keep)
    n_k = BAND_PAST + TILE
    bias_p, = _rel_bias_tables(rel_bias[0], TILE, n_k, ((0, TILE),), band=True, keys_on_rows=True)
    x = _band_prompt(x, qt3, qm, k3, vt3, bias_p, mem_kt, mem_vt, wout_b, vec(g_mix_post[1]))
    y_prompt = ffn(x, 1)[None]

    s_s, bs_s = _spatial_tile(w_spatial[0], b_spatial[0], seq_len)
    xs = x_sample.reshape(n_seq * seq_len, D_MODEL)
    cmkt, cmvt = _positions_last(cache_mem_k), _positions_last(cache_mem_v)
    xs, v_rows = _mixer_a(xs, vec(g_mix_pre[0]), vec(g_mix_post[0]), win_a, ln_g, ln_b, s_s, bs_s,
                          cmkt, cmvt, wout_a, rows_per_mem=seq_len, emit_v=True)
    xs = ffn(xs, 0)
    q3s, qms, kn3, vn3, k_new, v_new = _proj_b(xs, gkv, gpre_b, (win_b, wk, wv), mode="sample",
                                               rows=512, seq_len=seq_len)
    bias_c, bias_n = _rel_bias_tables(rel_bias[0], seq_len, past + seq_len,
                                      ((0, past), (past, past + seq_len)), band=False, keys_on_rows=False)
    xs = _band_sample(xs, q3s, qms, kn3, vn3, _positions_last(cache_band_k), _positions_last(cache_band_v),
                      bias_c, bias_n, cmkt, cmvt, wout_b, vec(g_mix_post[1]))
    y_sample = ffn(xs, 1).reshape(n_seq, seq_len, D_MODEL)

    return (y_prompt, y_sample,
            _heads_last(mem_kt, (DEPTH, 1)), _heads_last(mem_vt, (DEPTH, 1)),
            _heads_last(kt_tail, (1,)), _heads_last(vt_tail, (1,)),
            v_rows.reshape(1, n_seq, seq_len, GM_W),
            jnp.swapaxes(k_new, 1, 2), jnp.swapaxes(v_new, 1, 2))
```

```python
import functools

import jax
import jax.numpy as jnp
import numpy as np
from jax import lax
from jax.experimental import pallas as pl
from jax.experimental.pallas import tpu as pltpu

D_MODEL = 1024
DEPTH = 2
CHUNK = 64
HEAD_DIM = 64
GM_CHUNK = 128
GM_GROUPS = 4
GM_W = 768
GM_GW = GM_W // GM_GROUPS
MEM_LEN = 256
MEM_HEADS = 4
MEM_W = MEM_HEADS * HEAD_DIM
B_HEADS = 12
B_W = B_HEADS * HEAD_DIM
BAND_PAST = 512
REL_CLIP = 128
D_FF = 4 * D_MODEL
EPS = 1e-6

LANES = 128
HEAD_PAIRS = B_W // LANES
Q_SCALE = HEAD_DIM ** -0.5
NEG = -1e30
TILE = 256
KEY_BLOCKS = BAND_PAST // TILE + 1
KEY_STEP = 128
KEY_BLOCK = 128
PROJ_ROWS = 512
BF16_ROWS = 16
OUT_ROWS = HEAD_DIM + 8
BAND_TILES = BAND_PAST // TILE
GROUP_PAIRS = 2
BIAS_PERIOD = 1024
VMEM_LIMIT = 56 * 1024 * 1024

BF16 = jnp.bfloat16
F32 = jnp.float32


def _dot(a, b):
    return jnp.dot(a, b, preferred_element_type=F32)


def _dot_nt(a, b):
    return lax.dot_general(a, b, (((1,), (1,)), ((), ())), preferred_element_type=F32)


def _rms(x, g):
    ms = jnp.mean(x * x, axis=-1, keepdims=True)
    return x * lax.rsqrt(ms + EPS) * g


def _lsum(a):
    return jnp.sum(a, axis=-1, keepdims=True)


def _lmax(a):
    return jnp.max(a, axis=-1, keepdims=True)


def _const_spec(shape):
    nd = len(shape)
    return pl.BlockSpec(shape, lambda *_: (0,) * nd, pipeline_mode=pl.Buffered(1))


def _layer_spec(shape, layer):
    nd = len(shape)
    return pl.BlockSpec((None,) + shape, lambda *_: (layer,) + (0,) * nd, pipeline_mode=pl.Buffered(1))


def _without_ref(body, k):
    def wrapped(*refs):
        return body(*refs[:k], *refs[k + 1:])
    return wrapped


def _call(body, name, *, in_specs, into=None, semantics="parallel", **kw):
    params = pltpu.CompilerParams(dimension_semantics=(semantics,), vmem_limit_bytes=VMEM_LIMIT)
    if into is None:
        return pl.pallas_call(body, name=name, in_specs=in_specs, compiler_params=params, **kw)
    n_in = len(in_specs)
    call = pl.pallas_call(_without_ref(body, n_in), name=name,
                          in_specs=[*in_specs, pl.BlockSpec(memory_space=pl.ANY)],
                          input_output_aliases={n_in: 0}, compiler_params=params, **kw)
    return lambda *args: call(*args, into)


def _memkv_kernel(mem_ref, g_ref, wt_ref, kt_ref, vt_ref):
    ht = _dot_nt(wt_ref[...], _rms(mem_ref[...], g_ref[...]).astype(BF16))
    kt_ref[0] = ht[:MEM_W]
    vt_ref[0] = ht[MEM_W:]


def _memkv(mem, g_mem, w_mem_kv_t):
    out = jax.ShapeDtypeStruct((DEPTH, 1, MEM_W, MEM_LEN), F32)
    return _call(
        _memkv_kernel, "mem_kv",
        grid=(DEPTH,),
        in_specs=[
            pl.BlockSpec((MEM_LEN, D_MODEL), lambda l: (0, 0)),
            pl.BlockSpec((None, 1, D_MODEL), lambda l: (l, 0, 0)),
            pl.BlockSpec((None, 2 * MEM_W, D_MODEL), lambda l: (l, 0, 0)),
        ],
        out_specs=[pl.BlockSpec((None, 1, MEM_W, MEM_LEN), lambda l: (l, 0, 0, 0))] * 2,
        out_shape=[out, out],
    )(mem, g_mem.reshape(DEPTH, 1, D_MODEL), w_mem_kv_t)


def _mem_attend(qb, kt, vt):
    r = qb.shape[0]
    lane = lax.broadcasted_iota(jnp.int32, (1, MEM_W), 1)
    masks = [(lane >= h * HEAD_DIM) & (lane < (h + 1) * HEAD_DIM) for h in range(MEM_HEADS)]
    qs = jnp.concatenate([jnp.where(m, qb, jnp.zeros_like(qb)) for m in masks], axis=0)
    s = _dot(qs, kt)
    e = jnp.exp(s - _lmax(s))
    pv = _dot_nt(e.astype(BF16), vt) * (1.0 / _lsum(e))
    out = jnp.where(masks[0], pv[:r], 0.0)
    for h in range(1, MEM_HEADS):
        out = out + jnp.where(masks[h], pv[h * r:(h + 1) * r], 0.0)
    return out


def _gelu(x):
    c1 = float(np.sqrt(2.0 / np.pi))
    c2 = c1 * 0.044715
    half = 0.5 * x
    return half + half * jnp.tanh(x * (c1 + c2 * (x * x)))


def _interleave(stage_lists):
    live = list(stage_lists)
    while live:
        for g in list(live):
            if next(g, StopIteration) is StopIteration:
                live.remove(g)


def _mixer_a_kernel(x_ref, gpre_ref, gpost_ref, win_ref, gln_ref, bln_ref, s_ref, bs_ref,
                    mk_ref, mv_ref, wout_ref, o_ref, *v_out, rows_per_mem, tiles, shared_mem):
    nt = GM_W // LANES
    seqs = TILE // rows_per_mem
    lane = lax.broadcasted_iota(jnp.int32, (1, LANES), 1)
    lo = lane < (GM_GW - LANES)
    inv = 1.0 / GM_GW

    def group_stat(a):
        s0 = _lsum(a[0] + jnp.where(lo, a[1], 0.0)) * inv
        s1 = _lsum(jnp.where(lo, 0.0, a[1]) + a[2]) * inv
        s2 = _lsum(a[3] + jnp.where(lo, a[4], 0.0)) * inv
        s3 = _lsum(jnp.where(lo, 0.0, a[4]) + a[5]) * inv
        return [s0, jnp.where(lo, s0, s1), s1, s2, jnp.where(lo, s2, s3), s3]

    def tile(t):
        rows = slice(t * TILE, (t + 1) * TILE)
        z = _dot(_rms(x_ref[rows, :], gpre_ref[...]).astype(BF16), win_ref[...])
        yield
        u = [_gelu(z[:, j * LANES:(j + 1) * LANES]) for j in range(nt)]
        g = [_gelu(z[:, GM_W + j * LANES:GM_W + (j + 1) * LANES]) for j in range(nt)]
        mu = group_stat(g)
        c = [g[j] - mu[j] for j in range(nt)]
        var = group_stat([cj * cj for cj in c])
        gln = gln_ref[...]
        bln = bln_ref[...]
        vn = [c[j] * lax.rsqrt(var[j] + EPS) * gln[:, j * LANES:(j + 1) * LANES]
              + bln[:, j * LANES:(j + 1) * LANES] for j in range(nt)]
        if v_out:
            v_out[0][rows, :] = jnp.concatenate(vn, axis=-1)
        yield
        vb = [a.astype(BF16) for a in vn]
        win = [(0, 1), (1, 2), (3, 4), (4, 5)]
        m = [_dot(s_ref[k], jnp.concatenate([vb[a], vb[b]], axis=-1)) for k, (a, b) in enumerate(win)]
        mixed = [m[0][:, :LANES], jnp.where(lo, m[0][:, LANES:], m[1][:, :LANES]), m[1][:, LANES:],
                 m[2][:, :LANES], jnp.where(lo, m[2][:, LANES:], m[3][:, :LANES]), m[3][:, LANES:]]
        bs = bs_ref[...]
        gm = [u[j] * (mixed[j] + bs[:, j * LANES:(j + 1) * LANES]) for j in range(nt)]
        yield
        qm = (z[:, 2 * GM_W:] * Q_SCALE).astype(BF16)
        mo = []
        for b in range(seqs):
            r = slice(b * rows_per_mem, (b + 1) * rows_per_mem)
            mi = 0 if shared_mem else t * seqs + b
            mo.append(_mem_attend(qm[r], mk_ref[mi].astype(BF16), mv_ref[mi].astype(BF16)))
        mo = mo[0] if len(mo) == 1 else jnp.concatenate(mo, axis=0)
        yield
        cat = jnp.concatenate([a.astype(BF16) for a in gm] + [mo.astype(BF16)], axis=-1)
        o_ref[rows, :] = x_ref[rows, :] + _rms(_dot(cat, wout_ref[...]), gpost_ref[...])

    _interleave([tile(t) for t in range(tiles)])


def _mem_spec(mem, seqs, layer):
    if mem.shape[1] == 1:
        return pl.BlockSpec((None, 1, MEM_W, MEM_LEN), lambda i: (layer, 0, 0, 0))
    return pl.BlockSpec((None, seqs, MEM_W, MEM_LEN), lambda i: (layer, i, 0, 0))


def _mixer_a(x, gpre, gpost, win, gln, bln, s_mat, bs, mkt, mvt, wout, *, rows_per_mem, emit_v, tiles=2,
             slab_rows, slab_offset=0, into=None):
    n = x.shape[0]
    seqs = tiles * TILE // rows_per_mem
    row = lambda w: pl.BlockSpec((tiles * TILE, w), lambda i: (i, 0))
    off = slab_offset // (tiles * TILE)
    out_shape = [jax.ShapeDtypeStruct((slab_rows, D_MODEL), F32)]
    out_specs = [pl.BlockSpec((tiles * TILE, D_MODEL), lambda i: (i + off, 0))]
    if emit_v:
        out_shape.append(jax.ShapeDtypeStruct((n, GM_W), F32))
        out_specs.append(row(GM_W))
    return _call(
        functools.partial(_mixer_a_kernel, rows_per_mem=rows_per_mem, tiles=tiles,
                          shared_mem=mkt.shape[1] == 1), "mixer_a", into=into,
        grid=(n // (tiles * TILE),),
        in_specs=[
            row(D_MODEL),
            _const_spec((1, D_MODEL)), _const_spec((1, D_MODEL)),
            _const_spec((D_MODEL, 2 * GM_W + MEM_W)),
            _const_spec((1, GM_W)), _const_spec((1, GM_W)),
            _const_spec((GM_GROUPS, TILE, TILE)), _const_spec((TILE, GM_W)),
            _mem_spec(mkt, seqs, 0), _mem_spec(mvt, seqs, 0),
            _const_spec((GM_W + MEM_W, D_MODEL)),
        ],
        out_specs=out_specs,
        out_shape=out_shape,
    )(x, gpre, gpost, win, gln, bln, s_mat, bs, mkt, mvt, wout)


def _ffn_kernel(x_ref, gpre_ref, gpost_ref, w1_ref, w2_ref, o_ref, *o_tail, rows, tiles, ff_chunk, head_steps):
    def tile(t):
        r = slice(t * rows, (t + 1) * rows)
        x = x_ref[r, :]
        xn = _rms(x, gpre_ref[...]).astype(BF16)
        acc = jnp.zeros(x.shape, F32)
        yield
        for c in range(D_FF // ff_chunk):
            h = _dot(xn, w1_ref[:, c * ff_chunk:(c + 1) * ff_chunk])
            h = jnp.square(jnp.maximum(h, 0.0)).astype(BF16)
            acc = acc + _dot(h, w2_ref[c * ff_chunk:(c + 1) * ff_chunk, :])
            yield
        y = x + _rms(acc, gpost_ref[...])
        if not o_tail:
            o_ref[r, :] = y
        else:
            @pl.when(pl.program_id(0) < head_steps)
            def _():
                o_ref[r, :] = y

            @pl.when(pl.program_id(0) >= head_steps)
            def _():
                o_tail[0][r, :] = y

    _interleave([tile(t) for t in range(tiles)])


def _ffn(x, gpre, gpost, w1, w2, layer, *, rows=512, tiles=2, ff_chunk=1024, split=None):
    n = x.shape[0]
    blk = tiles * rows
    row = pl.BlockSpec((blk, D_MODEL), lambda i: (i, 0))
    if split is None:
        head_steps, out_specs, out_shape = None, row, jax.ShapeDtypeStruct((n, D_MODEL), F32)
    else:
        head_steps = split // blk
        out_specs = [pl.BlockSpec((blk, D_MODEL), lambda i: (jnp.minimum(i, head_steps - 1), 0)),
                     pl.BlockSpec((blk, D_MODEL), lambda i: (jnp.maximum(i - head_steps, 0), 0))]
        out_shape = [jax.ShapeDtypeStruct((split, D_MODEL), F32), jax.ShapeDtypeStruct((n - split, D_MODEL), F32)]
    return _call(
        functools.partial(_ffn_kernel, rows=rows, tiles=tiles, ff_chunk=ff_chunk, head_steps=head_steps),
        "ffn", semantics="arbitrary",
        grid=(n // blk,),
        in_specs=[row, _layer_spec((1, D_MODEL), layer), _layer_spec((1, D_MODEL), layer),
                  _layer_spec((D_MODEL, D_FF), layer), _layer_spec((D_FF, D_MODEL), layer)],
        out_specs=out_specs,
        out_shape=out_shape,
    )(x, gpre, gpost, w1, w2)


def _proj_b_kernel(x_ref, gkv_ref, gpre_ref, *refs, mode, seq_len):
    rows = x_ref.shape[0]

    def normed(r):
        x = x_ref[r, :]
        xh = x * lax.rsqrt(jnp.mean(x * x, axis=-1, keepdims=True) + EPS)
        return (xh * gkv_ref[...]).astype(BF16), (xh * gpre_ref[...]).astype(BF16)

    if mode == "tail":
        wkt_ref, wvt_ref, kt_ref, vt_ref = refs
        xkv, _ = normed(slice(None))
        kt_ref[...] = _dot_nt(wkt_ref[...], xkv)
        vt_ref[...] = _dot_nt(wvt_ref[...], xkv)
        return
    if mode == "prompt":
        wqt_ref, wqm_ref, wk_ref, wvt_ref, qt_ref, qm_ref, k_ref, vt_ref = refs

        def tile(t):
            r = slice(t * PROJ_ROWS, (t + 1) * PROJ_ROWS)
            xkv, xq = normed(r)
            yield
            qt = _dot_nt(wqt_ref[...], xq) * Q_SCALE
            for p in range(HEAD_PAIRS):
                qt_ref[p, :, r] = qt[p * LANES:(p + 1) * LANES, :].astype(BF16)
            yield
            vt = _dot_nt(wvt_ref[...], xkv)
            for p in range(HEAD_PAIRS):
                vt_ref[p, :, r] = vt[p * LANES:(p + 1) * LANES, :].astype(BF16)
            yield
            k = _dot(xkv, wk_ref[...])
            for p in range(HEAD_PAIRS):
                k_ref[p, r, :] = k[:, p * LANES:(p + 1) * LANES].astype(BF16)
            yield
            qm_ref[r, :] = (_dot(xq, wqm_ref[...]) * Q_SCALE).astype(BF16)

        _interleave([tile(t) for t in range(rows // PROJ_ROWS)])
        return
    win_ref, wk_ref, wv_ref, q_ref, qm_ref, k_ref, v_ref, k4_ref, v4_ref = refs
    xkv, xq = normed(slice(None))
    z = _dot(xq, win_ref[...]) * Q_SCALE
    qm_ref[...] = z[:, B_W:].astype(BF16)
    k = _dot(xkv, wk_ref[...])
    v = _dot(xkv, wv_ref[...])
    for p in range(HEAD_PAIRS):
        cols = slice(p * LANES, (p + 1) * LANES)
        q_ref[p] = z[:, cols].astype(BF16)
        k_ref[p] = k[:, cols].astype(BF16)
        v_ref[p] = v[:, cols].astype(BF16)
    for b in range(rows // seq_len):
        for h in range(B_HEADS):
            r, c = slice(b * seq_len, (b + 1) * seq_len), slice(h * HEAD_DIM, (h + 1) * HEAD_DIM)
            k4_ref[b, h] = k[r, c]
            v4_ref[b, h] = v[r, c]


def _proj_b(x, gkv, gpre, weights, *, mode, rows, row_offset=0, n_rows=None, seq_len=None):
    n = x.shape[0] if n_rows is None else n_rows
    off = row_offset // rows
    pair_rows = jax.ShapeDtypeStruct((HEAD_PAIRS, n, LANES), BF16)
    pair_rows_spec = pl.BlockSpec((HEAD_PAIRS, rows, LANES), lambda i: (0, i, 0))
    pair_cols = jax.ShapeDtypeStruct((HEAD_PAIRS, LANES, n), BF16)
    pair_cols_spec = pl.BlockSpec((HEAD_PAIRS, LANES, rows), lambda i: (0, 0, i))
    qm = jax.ShapeDtypeStruct((n, MEM_W), BF16)
    qm_spec = pl.BlockSpec((rows, MEM_W), lambda i: (i, 0))
    if mode == "tail":
        out_shape = [jax.ShapeDtypeStruct((B_W, n), F32)] * 2
        out_specs = [pl.BlockSpec((B_W, rows), lambda i: (0, i))] * 2
    elif mode == "prompt":
        out_shape = [pair_cols, qm, pair_rows, pair_cols]
        out_specs = [pair_cols_spec, qm_spec, pair_rows_spec, pair_cols_spec]
    else:
        seqs = rows // seq_len
        per_head = jax.ShapeDtypeStruct((n // seq_len, B_HEADS, seq_len, HEAD_DIM), F32)
        per_head_spec = pl.BlockSpec((seqs, B_HEADS, seq_len, HEAD_DIM), lambda i: (i, 0, 0, 0))
        out_shape = [pair_rows, qm, pair_rows, pair_rows, per_head, per_head]
        out_specs = [pair_rows_spec, qm_spec, pair_rows_spec, pair_rows_spec, per_head_spec, per_head_spec]
    return _call(
        functools.partial(_proj_b_kernel, mode=mode, seq_len=seq_len), "proj_b_" + mode,
        grid=(n // rows,),
        in_specs=[pl.BlockSpec((rows, D_MODEL), lambda i: (i + off, 0)),
                  _const_spec((1, D_MODEL)), _const_spec((1, D_MODEL))]
                 + [_const_spec(w.shape) for w in weights],
        out_specs=out_specs,
        out_shape=out_shape,
    )(x, gkv, gpre, *weights)


def _bias_kernel(g_ref, *o_refs, n_q, n_k, splits, band, keys_on_rows):
    n_rows, shift = (n_k, n_q) if keys_on_rows else (n_q, BIAS_PERIOD - (n_q - 1))
    for hh in range(2):
        x = jnp.broadcast_to(g_ref[hh:hh + 1, :], (n_rows, BIAS_PERIOD))
        t = pltpu.roll(x, shift, 1, stride=1, stride_axis=0)
        if band:
            a = lax.broadcasted_iota(jnp.int32, (n_rows, BIAS_PERIOD), 0)
            b = lax.broadcasted_iota(jnp.int32, (n_rows, BIAS_PERIOD), 1)
            r, w = (b, a) if keys_on_rows else (a, b)
            j = w - (r - (r & (CHUNK - 1)))
            t = jnp.where((j >= 0) & (j < BAND_PAST + CHUNK), t, NEG)
        for o_ref, (lo, hi) in zip(o_refs, splits):
            o_ref[hh * n_rows:(hh + 1) * n_rows, :] = t[:, lo:hi]


def _rel_bias_tables(rel_bias, n_q, n_k, splits, *, band, keys_on_rows):
    rel_bias = rel_bias - rel_bias[:, -1:]
    c0 = n_k - 1
    far = jnp.broadcast_to(rel_bias[:, -1:], (B_HEADS, c0 - REL_CLIP))
    near = jnp.broadcast_to(rel_bias[:, :1], (B_HEADS, BIAS_PERIOD - (c0 - REL_CLIP) - (2 * REL_CLIP + 1)))
    gen = [near, rel_bias, far] if keys_on_rows else [far, rel_bias[:, ::-1], near]
    gen = jnp.concatenate(gen, axis=1).reshape(HEAD_PAIRS, 2, BIAS_PERIOD)
    n_rows = n_k if keys_on_rows else n_q
    return _call(
        functools.partial(_bias_kernel, n_q=n_q, n_k=n_k, splits=splits, band=band,
                          keys_on_rows=keys_on_rows), "rel_bias",
        grid=(HEAD_PAIRS,),
        in_specs=[pl.BlockSpec((None, 2, BIAS_PERIOD), lambda p: (p, 0, 0))],
        out_specs=[pl.BlockSpec((None, 2 * n_rows, hi - lo), lambda p: (p, 0, 0)) for lo, hi in splits],
        out_shape=[jax.ShapeDtypeStruct((HEAD_PAIRS, 2 * n_rows, hi - lo), F32) for lo, hi in splits],
    )(gen)


def _band_block_kinds():
    n_k = KEY_BLOCKS * TILE
    r = np.arange(TILE)[None, :]
    w = np.arange(n_k)[:, None]
    j = w - CHUNK * (r // CHUNK)
    ok = (j >= 0) & (j < BAND_PAST + CHUNK)
    plain = ok & (BAND_PAST + r - w >= REL_CLIP)
    kinds = []
    for a in range(n_k // KEY_BLOCK):
        rows = slice(a * KEY_BLOCK, (a + 1) * KEY_BLOCK)
        blocks = [(rows, slice(b * LANES, (b + 1) * LANES)) for b in range(TILE // LANES)]
        kinds.append(["skip" if not ok[blk].any() else "plain" if plain[blk].all() else "biased"
                      for blk in blocks])
    return kinds


def _band_prompt_kernel(x_ref, qt_ref, qm_ref, kp_ref, kc_ref, vtp_ref, vtc_ref,
                        bias_ref, mk_ref, mv_ref, wout_ref, gpost_ref, o_ref):
    i = pl.program_id(0)
    n_k = KEY_BLOCKS * TILE
    kinds = _band_block_kinds()
    w = lax.broadcasted_iota(jnp.int32, (n_k, LANES), 0)
    c = lax.broadcasted_iota(jnp.int32, (n_k, LANES), 1)
    ones_rows = jnp.where(lax.broadcasted_iota(jnp.int32, (BF16_ROWS, n_k), 0) == 0, 1.0, 0.0).astype(BF16)
    row = lax.broadcasted_iota(jnp.int32, (LANES, TILE), 0)
    one_hot_row = jnp.where(row == 0, 1.0, 0.0).astype(BF16)
    lo = row < HEAD_DIM

    def key_step(a, s, vth, bias_rows, state):
        blocks = range(a * KEY_STEP // KEY_BLOCK, (a + 1) * KEY_STEP // KEY_BLOCK)
        es, alphas, active = [], [], []
        for b in range(TILE // LANES):
            cols = slice(b * LANES, (b + 1) * LANES)
            sjs = {}
            for blk in blocks:
                if kinds[blk][b] != "skip":
                    rows = slice(blk * KEY_BLOCK, (blk + 1) * KEY_BLOCK)
                    sjs[blk] = s[rows, cols] + bias_rows(rows, cols) if kinds[blk][b] == "biased" else s[rows, cols]
            active.append(bool(sjs))
            alphas.append(None)
            if not sjs:
                es.append(jnp.zeros((KEY_STEP, LANES), F32))
                continue
            mj = jnp.max(functools.reduce(jnp.maximum, sjs.values()), axis=0, keepdims=True)
            if state[b] is None:
                state[b] = [mj, None]
            else:
                m_new = jnp.maximum(state[b][0], mj)
                alphas[b] = jnp.exp(state[b][0] - m_new)
                state[b][0] = m_new
            es.append(jnp.concatenate(
                [jnp.exp(sjs[blk] - state[b][0]) if blk in sjs else jnp.zeros((KEY_BLOCK, LANES), F32)
                 for blk in blocks], axis=0))
        rows = slice(a * KEY_STEP, (a + 1) * KEY_STEP)
        pv = _dot(vth[:, rows], jnp.concatenate(es, axis=1).astype(BF16))[:OUT_ROWS]
        for b in range(TILE // LANES):
            if active[b]:
                pv_b = pv[:, b * LANES:(b + 1) * LANES]
                state[b][1] = pv_b if alphas[b] is None else state[b][1] * alphas[b] + pv_b

    def tile(t):
        q_rows = slice(t * TILE, (t + 1) * TILE)
        k_rows = slice(t * TILE, t * TILE + n_k)
        first_key_tile = i * BAND_TILES + t - (KEY_BLOCKS - 1)
        pen = jnp.where((w < -first_key_tile * TILE) & (c == 0), NEG, 0.0).astype(BF16)
        outs = []
        for p0 in range(0, HEAD_PAIRS, GROUP_PAIRS):
            heads = []
            for p in range(p0, p0 + GROUP_PAIRS):
                k_win = jnp.concatenate([kp_ref[p], kc_ref[p]], axis=0)[k_rows]
                k_ext = jnp.concatenate([k_win, pen], axis=1)
                vt = jnp.concatenate([vtp_ref[p], vtc_ref[p]], axis=1)[:, k_rows]
                qt = qt_ref[p, :, q_rows]
                zero = jnp.zeros_like(qt)
                for hh in range(2):
                    qh = jnp.where(lo, qt, zero) if hh == 0 else jnp.where(lo, zero, qt)
                    s = _dot(k_ext, jnp.concatenate([qh, one_hot_row], axis=0))
                    vth = jnp.concatenate([vt[hh * HEAD_DIM:(hh + 1) * HEAD_DIM, :], ones_rows], axis=0)
                    bias_rows = functools.partial(
                        lambda rows, cols, p, base: bias_ref[p, base + rows.start:base + rows.stop, cols],
                        p=p, base=hh * n_k)
                    heads.append((s, vth, bias_rows, [None] * (TILE // LANES)))
            for a in range(n_k // KEY_STEP):
                for head in heads:
                    key_step(a, *head)
            outs += [jnp.concatenate([o[:HEAD_DIM] * (1.0 / o[HEAD_DIM:HEAD_DIM + 1]) for _, o in state], axis=1)
                     for *_, state in heads]
            yield
        band = jnp.concatenate(outs, axis=0).T.astype(BF16)
        mo = _mem_attend(qm_ref[q_rows, :], mk_ref[0].astype(BF16), mv_ref[0].astype(BF16))
        cat = jnp.concatenate([band, mo.astype(BF16)], axis=-1)
        o_ref[q_rows, :] = x_ref[q_rows, :] + _rms(_dot(cat, wout_ref[...]), gpost_ref[...])

    _interleave([tile(t) for t in range(BAND_TILES)])


def _band_prompt(x, qt3, qm, k3, vt3, bias, mkt, mvt, wout, gpost):
    n = qm.shape[0]
    rows = BAND_TILES * TILE
    assert rows == BAND_PAST and n % rows == 0
    past = lambda i: jnp.maximum(i - 1, 0)
    return _call(
        _band_prompt_kernel, "band_prompt",
        grid=(n // rows,),
        in_specs=[pl.BlockSpec((rows, D_MODEL), lambda i: (i, 0)),
                  pl.BlockSpec((HEAD_PAIRS, LANES, rows), lambda i: (0, 0, i)),
                  pl.BlockSpec((rows, MEM_W), lambda i: (i, 0)),
                  pl.BlockSpec((HEAD_PAIRS, rows, LANES), lambda i: (0, past(i), 0)),
                  pl.BlockSpec((HEAD_PAIRS, rows, LANES), lambda i: (0, i, 0)),
                  pl.BlockSpec((HEAD_PAIRS, LANES, rows), lambda i: (0, 0, past(i))),
                  pl.BlockSpec((HEAD_PAIRS, LANES, rows), lambda i: (0, 0, i)),
                  _const_spec((HEAD_PAIRS, 2 * KEY_BLOCKS * TILE, TILE)),
                  _mem_spec(mkt, 1, 1), _mem_spec(mvt, 1, 1),
                  _const_spec((B_W + MEM_W, D_MODEL)), _const_spec((1, D_MODEL))],
        out_specs=pl.BlockSpec((rows, D_MODEL), lambda i: (i, 0)),
        out_shape=jax.ShapeDtypeStruct(x.shape, F32),
    )(x, qt3, qm, k3, k3, vt3, vt3, bias, mkt, mvt, wout, gpost)


def _band_sample_kernel(x_ref, q_ref, qm_ref, kn_ref, vn_ref, ck_ref, cv_ref, bc_ref, bn_ref,
                        mk_ref, mv_ref, wout_ref, gpost_ref, o_ref, *, seqs, seq_len):
    lane = lax.broadcasted_iota(jnp.int32, (1, LANES), 1)
    lo = lane < HEAD_DIM

    def pair_attend(b, p, out):
        rows = slice(b * seq_len, (b + 1) * seq_len)
        hd = slice(p * LANES, (p + 1) * LANES)
        qp = q_ref[p, rows, :]
        zero = jnp.zeros_like(qp)
        qs = jnp.concatenate([jnp.where(lo, qp, zero), jnp.where(lo, zero, qp)], axis=0)
        sc = _dot(qs, ck_ref[b, hd, :].astype(BF16)) + bc_ref[p]
        sn = _dot_nt(qs, kn_ref[p, rows, :]) + bn_ref[p]
        yield
        m = jnp.maximum(_lmax(sc), _lmax(sn))
        ec = jnp.exp(sc - m)
        en = jnp.exp(sn - m)
        l = _lsum(ec) + _lsum(en)
        yield
        o = (_dot_nt(ec.astype(BF16), cv_ref[b, hd, :].astype(BF16))
             + _dot(en.astype(BF16), vn_ref[p, rows, :]))
        yield
        o = o * (1.0 / l)
        out[p] = jnp.where(lo, o[:seq_len], o[seq_len:]).astype(BF16)

    rows_out = []
    for b in range(seqs):
        band = [None] * HEAD_PAIRS
        _interleave([pair_attend(b, p, band) for p in range(HEAD_PAIRS)])
        rows = slice(b * seq_len, (b + 1) * seq_len)
        mo = _mem_attend(qm_ref[rows, :], mk_ref[b].astype(BF16), mv_ref[b].astype(BF16))
        rows_out.append(jnp.concatenate(band + [mo.astype(BF16)], axis=-1))
    cat = jnp.concatenate(rows_out, axis=0)
    o_ref[...] = x_ref[...] + _rms(_dot(cat, wout_ref[...]), gpost_ref[...])


def _band_sample(x, q3, qm, kn3, vn3, ckt, cvt, bias_c, bias_n, mkt, mvt, wout, gpost, *, into, seqs=4):
    n = qm.shape[0]
    n_seq, past = ckt.shape[0], ckt.shape[2]
    seq_len = n // n_seq
    rows = seqs * seq_len
    off = (x.shape[0] - n) // rows
    slab_spec = pl.BlockSpec((rows, D_MODEL), lambda i: (i + off, 0))
    pair_rows_spec = pl.BlockSpec((HEAD_PAIRS, rows, LANES), lambda i: (0, i, 0))
    cache_spec = pl.BlockSpec((seqs, B_W, past), lambda i: (i, 0, 0))
    return _call(
        functools.partial(_band_sample_kernel, seqs=seqs, seq_len=seq_len), "band_sample", into=into,
        grid=(n_seq // seqs,),
        in_specs=[slab_spec,
                  pair_rows_spec,
                  pl.BlockSpec((rows, MEM_W), lambda i: (i, 0)),
                  pair_rows_spec, pair_rows_spec,
                  cache_spec, cache_spec,
                  _const_spec((HEAD_PAIRS, 2 * seq_len, past)),
                  _const_spec((HEAD_PAIRS, 2 * seq_len, seq_len)),
                  _mem_spec(mkt, seqs, 1), _mem_spec(mvt, seqs, 1),
                  _const_spec((B_W + MEM_W, D_MODEL)), _const_spec((1, D_MODEL))],
        out_specs=slab_spec,
        out_shape=jax.ShapeDtypeStruct(x.shape, F32),
    )(x, q3, qm, kn3, vn3, ckt, cvt, bias_c, bias_n, mkt, mvt, wout, gpost)


def _spatial_tile(w_s, b_s, period):
    tril = jnp.tril(jnp.ones((GM_CHUNK, GM_CHUNK), dtype=bool))
    w = jnp.where(tril, w_s, jnp.zeros((), w_s.dtype))[:, :period, :period]
    eye = jnp.eye(TILE // period, dtype=w.dtype)
    s_mat = jnp.einsum("ab,gts->gatbs", eye, w).reshape(GM_GROUPS, TILE, TILE)
    rows = jnp.tile(b_s[:, :period], (1, TILE // period))
    bs = jnp.repeat(rows.T, GM_GW, axis=1)
    return s_mat.astype(BF16), bs


def _heads_last(t, lead):
    pos = t.shape[-1]
    t = t.reshape(lead + (-1, HEAD_DIM, pos))
    nd = len(lead)
    return jnp.transpose(t, tuple(range(nd)) + (nd + 2, nd, nd + 1))


def _positions_last(c):
    nd = c.ndim
    t = jnp.transpose(c, tuple(range(nd - 3)) + (nd - 2, nd - 1, nd - 3))
    return t.reshape(c.shape[:-3] + (c.shape[-2] * c.shape[-1], c.shape[-3]))


def kernel(x_prompt, x_sample, cache_mem_k, cache_mem_v, cache_band_k, cache_band_v, mem_prompt,
           g_mix_pre, g_mix_post, g_ffn_pre, g_ffn_post, g_mem, w_mem_kv,
           w_in_a, g_gm_ln, b_gm_ln, w_spatial, b_spatial, w_out_a,
           g_kv, w_kv, w_in_b, rel_bias, w_out_b, w_ff1, w_ff2):
    seq = x_prompt.shape[1]
    n_seq, seq_len = x_sample.shape[0], x_sample.shape[1]
    past = cache_band_k.shape[1]
    vec = lambda a: a.reshape(1, -1)
    stack = lambda a: a.reshape(DEPTH, 1, -1)

    win_a = w_in_a[0].astype(BF16)
    wout_a = w_out_a[0].astype(BF16)
    wk = w_kv[:, :B_W].astype(BF16)
    wv = w_kv[:, B_W:].astype(BF16)
    wkt, wvt = wk.T, wv.T
    win_b = w_in_b[0].astype(BF16)
    wqt, wqm = win_b[:, :B_W].T, win_b[:, B_W:]
    wout_b = w_out_b[0].astype(BF16)
    w1 = w_ff1.astype(BF16)
    w2 = w_ff2.astype(BF16)
    ln_g, ln_b = vec(g_gm_ln[0]), vec(b_gm_ln[0])
    gkv, gpre_b = vec(g_kv), vec(g_mix_pre[1])
    gf_pre, gf_post = stack(g_ffn_pre), stack(g_ffn_post)

    n_sample = n_seq * seq_len
    slab_rows = seq + n_sample
    pre_a, post_a = vec(g_mix_pre[0]), vec(g_mix_post[0])

    mem_kt, mem_vt = _memkv(mem_prompt[0], g_mem, jnp.swapaxes(w_mem_kv, 1, 2).astype(BF16))
    s_p, bs_p = _spatial_tile(w_spatial[0], b_spatial[0], GM_CHUNK)
    s_s, bs_s = _spatial_tile(w_spatial[0], b_spatial[0], seq_len)
    cmkt, cmvt = _positions_last(cache_mem_k), _positions_last(cache_mem_v)
    x, = _mixer_a(x_prompt[0], pre_a, post_a, win_a, ln_g, ln_b, s_p, bs_p, mem_kt, mem_vt, wout_a,
                  rows_per_mem=TILE, emit_v=False, tiles=4, slab_rows=slab_rows)
    x, v_rows = _mixer_a(x_sample.reshape(n_sample, D_MODEL), pre_a, post_a, win_a, ln_g, ln_b, s_s, bs_s,
                         cmkt, cmvt, wout_a, rows_per_mem=seq_len, emit_v=True,
                         slab_rows=slab_rows, slab_offset=seq, into=x)
    x = _ffn(x, gf_pre, gf_post, w1, w2, 0)

    qt3, qm, k3, vt3 = _proj_b(x, gkv, gpre_b, (wqt, wqm, wk, wvt), mode="prompt", rows=2 * PROJ_ROWS,
                               n_rows=seq)
    n_keep = min(BAND_PAST, seq)
    kt_tail, vt_tail = _proj_b(x, gkv, gpre_b, (wkt, wvt), mode="tail", rows=TILE,
                               row_offset=seq - n_keep, n_rows=n_keep)
    q3s, qms, kn3, vn3, k_new, v_new = _proj_b(x, gkv, gpre_b, (win_b, wk, wv), mode="sample", rows=512,
                                               row_offset=seq, n_rows=n_sample, seq_len=seq_len)
    n_k = BAND_PAST + TILE
    bias_p, = _rel_bias_tables(rel_bias[0], TILE, n_k, ((0, TILE),), band=True, keys_on_rows=True)
    bias_c, bias_n = _rel_bias_tables(rel_bias[0], seq_len, past + seq_len,
                                      ((0, past), (past, past + seq_len)), band=False, keys_on_rows=False)
    post_b = vec(g_mix_post[1])
    y = _band_prompt(x, qt3, qm, k3, vt3, bias_p, mem_kt, mem_vt, wout_b, post_b)
    y = _band_sample(x, q3s, qms, kn3, vn3, _positions_last(cache_band_k), _positions_last(cache_band_v),
                     bias_c, bias_n, cmkt, cmvt, wout_b, post_b, into=y)
    y_prompt, y_sample = _ffn(y, gf_pre, gf_post, w1, w2, 1, split=seq)
    y_prompt = y_prompt[None]
    y_sample = y_sample.reshape(n_seq, seq_len, D_MODEL)

    return (y_prompt, y_sample,
            _heads_last(mem_kt, (DEPTH, 1)), _heads_last(mem_vt, (DEPTH, 1)),
            _heads_last(kt_tail, (1,)), _heads_last(vt_tail, (1,)),
            v_rows.reshape(1, n_seq, seq_len, GM_W),
            jnp.swapaxes(k_new, 1, 2), jnp.swapaxes(v_new, 1, 2))
```

```python
import functools

import jax
import jax.numpy as jnp
import numpy as np
from jax import lax
from jax.experimental import pallas as pl
from jax.experimental.pallas import tpu as pltpu

D_MODEL = 1024
DEPTH = 2
CHUNK = 64
HEAD_DIM = 64
GM_CHUNK = 128
GM_GROUPS = 4
GM_W = 768
GM_GW = GM_W // GM_GROUPS
MEM_LEN = 256
MEM_HEADS = 4
MEM_W = MEM_HEADS * HEAD_DIM
B_HEADS = 12
B_W = B_HEADS * HEAD_DIM
BAND_PAST = 512
REL_CLIP = 128
D_FF = 4 * D_MODEL
EPS = 1e-6

LANES = 128
HEAD_PAIRS = B_W // LANES
Q_SCALE = HEAD_DIM ** -0.5
NEG = -1e30
TILE = 256
KEY_BLOCKS = BAND_PAST // TILE + 1
KEY_STEP = 128
KEY_BLOCK = 128
PROJ_ROWS = 512
BF16_ROWS = 16
OUT_ROWS = HEAD_DIM + 8
BAND_TILES = BAND_PAST // TILE
GROUP_PAIRS = 2
BIAS_PERIOD = 1024
VMEM_LIMIT = 56 * 1024 * 1024

BF16 = jnp.bfloat16
F32 = jnp.float32


def _dot(a, b):
    return jnp.dot(a, b, preferred_element_type=F32)


def _dot_nt(a, b):
    return lax.dot_general(a, b, (((1,), (1,)), ((), ())), preferred_element_type=F32)


def _rms(x, g):
    ms = jnp.mean(x * x, axis=-1, keepdims=True)
    return x * lax.rsqrt(ms + EPS) * g


def _lsum(a):
    return jnp.sum(a, axis=-1, keepdims=True)


def _lmax(a):
    return jnp.max(a, axis=-1, keepdims=True)


def _const_spec(shape):
    nd = len(shape)
    return pl.BlockSpec(shape, lambda *_: (0,) * nd, pipeline_mode=pl.Buffered(1))


def _layer_spec(shape, layer):
    nd = len(shape)
    return pl.BlockSpec((None,) + shape, lambda *_: (layer,) + (0,) * nd, pipeline_mode=pl.Buffered(1))


def _without_ref(body, k):
    def wrapped(*refs):
        return body(*refs[:k], *refs[k + 1:])
    return wrapped


def _call(body, name, *, in_specs, into=None, semantics="parallel", **kw):
    params = pltpu.CompilerParams(dimension_semantics=(semantics,), vmem_limit_bytes=VMEM_LIMIT)
    if into is None:
        return pl.pallas_call(body, name=name, in_specs=in_specs, compiler_params=params, **kw)
    n_in = len(in_specs)
    call = pl.pallas_call(_without_ref(body, n_in), name=name,
                          in_specs=[*in_specs, pl.BlockSpec(memory_space=pl.ANY)],
                          input_output_aliases={n_in: 0}, compiler_params=params, **kw)
    return lambda *args: call(*args, into)


def _memkv_kernel(mem_ref, g_ref, wt_ref, kt_ref, vt_ref):
    ht = _dot_nt(wt_ref[...], _rms(mem_ref[...], g_ref[...]).astype(BF16))
    kt_ref[0] = ht[:MEM_W]
    vt_ref[0] = ht[MEM_W:]


def _memkv(mem, g_mem, w_mem_kv_t):
    out = jax.ShapeDtypeStruct((DEPTH, 1, MEM_W, MEM_LEN), F32)
    return _call(
        _memkv_kernel, "mem_kv",
        grid=(DEPTH,),
        in_specs=[
            pl.BlockSpec((MEM_LEN, D_MODEL), lambda l: (0, 0)),
            pl.BlockSpec((None, 1, D_MODEL), lambda l: (l, 0, 0)),
            pl.BlockSpec((None, 2 * MEM_W, D_MODEL), lambda l: (l, 0, 0)),
        ],
        out_specs=[pl.BlockSpec((None, 1, MEM_W, MEM_LEN), lambda l: (l, 0, 0, 0))] * 2,
        out_shape=[out, out],
    )(mem, g_mem.reshape(DEPTH, 1, D_MODEL), w_mem_kv_t)


def _mem_attend(qb, kt, vt):
    r = qb.shape[0]
    lane = lax.broadcasted_iota(jnp.int32, (1, MEM_W), 1)
    masks = [(lane >= h * HEAD_DIM) & (lane < (h + 1) * HEAD_DIM) for h in range(MEM_HEADS)]
    qs = jnp.concatenate([jnp.where(m, qb, jnp.zeros_like(qb)) for m in masks], axis=0)
    s = _dot(qs, kt)
    e = jnp.exp(s - _lmax(s))
    pv = _dot_nt(e.astype(BF16), vt) * (1.0 / _lsum(e))
    out = jnp.where(masks[0], pv[:r], 0.0)
    for h in range(1, MEM_HEADS):
        out = out + jnp.where(masks[h], pv[h * r:(h + 1) * r], 0.0)
    return out


def _gelu(x):
    c1 = float(np.sqrt(2.0 / np.pi))
    c2 = c1 * 0.044715
    half = 0.5 * x
    return half + half * jnp.tanh(x * (c1 + c2 * (x * x)))


def _interleave(stage_lists):
    live = list(stage_lists)
    while live:
        for g in list(live):
            if next(g, StopIteration) is StopIteration:
                live.remove(g)


def _mixer_a_kernel(x_ref, gpre_ref, gpost_ref, win_ref, gln_ref, bln_ref, s_ref, bs_ref,
                    mk_ref, mv_ref, wout_ref, o_ref, *v_out, rows_per_mem, tiles, shared_mem):
    nt = GM_W // LANES
    seqs = TILE // rows_per_mem
    lane = lax.broadcasted_iota(jnp.int32, (1, LANES), 1)
    lo = lane < (GM_GW - LANES)
    inv = 1.0 / GM_GW

    def group_stat(a):
        s0 = _lsum(a[0] + jnp.where(lo, a[1], 0.0)) * inv
        s1 = _lsum(jnp.where(lo, 0.0, a[1]) + a[2]) * inv
        s2 = _lsum(a[3] + jnp.where(lo, a[4], 0.0)) * inv
        s3 = _lsum(jnp.where(lo, 0.0, a[4]) + a[5]) * inv
        return [s0, jnp.where(lo, s0, s1), s1, s2, jnp.where(lo, s2, s3), s3]

    def tile(t):
        rows = slice(t * TILE, (t + 1) * TILE)
        z = _dot(_rms(x_ref[rows, :], gpre_ref[...]).astype(BF16), win_ref[...])
        yield
        u = [_gelu(z[:, j * LANES:(j + 1) * LANES]) for j in range(nt)]
        g = [_gelu(z[:, GM_W + j * LANES:GM_W + (j + 1) * LANES]) for j in range(nt)]
        mu = group_stat(g)
        c = [g[j] - mu[j] for j in range(nt)]
        var = group_stat([cj * cj for cj in c])
        gln = gln_ref[...]
        bln = bln_ref[...]
        vn = [c[j] * lax.rsqrt(var[j] + EPS) * gln[:, j * LANES:(j + 1) * LANES]
              + bln[:, j * LANES:(j + 1) * LANES] for j in range(nt)]
        if v_out:
            v_out[0][rows, :] = jnp.concatenate(vn, axis=-1)
        yield
        vb = [a.astype(BF16) for a in vn]
        win = [(0, 1), (1, 2), (3, 4), (4, 5)]
        m = [_dot(s_ref[k], jnp.concatenate([vb[a], vb[b]], axis=-1)) for k, (a, b) in enumerate(win)]
        mixed = [m[0][:, :LANES], jnp.where(lo, m[0][:, LANES:], m[1][:, :LANES]), m[1][:, LANES:],
                 m[2][:, :LANES], jnp.where(lo, m[2][:, LANES:], m[3][:, :LANES]), m[3][:, LANES:]]
        bs = bs_ref[...]
        gm = [u[j] * (mixed[j] + bs[:, j * LANES:(j + 1) * LANES]) for j in range(nt)]
        yield
        qm = (z[:, 2 * GM_W:] * Q_SCALE).astype(BF16)
        mo = []
        for b in range(seqs):
            r = slice(b * rows_per_mem, (b + 1) * rows_per_mem)
            mi = 0 if shared_mem else t * seqs + b
            mo.append(_mem_attend(qm[r], mk_ref[mi].astype(BF16), mv_ref[mi].astype(BF16)))
        mo = mo[0] if len(mo) == 1 else jnp.concatenate(mo, axis=0)
        yield
        cat = jnp.concatenate([a.astype(BF16) for a in gm] + [mo.astype(BF16)], axis=-1)
        o_ref[rows, :] = x_ref[rows, :] + _rms(_dot(cat, wout_ref[...]), gpost_ref[...])

    _interleave([tile(t) for t in range(tiles)])


def _mem_spec(mem, seqs, layer):
    if mem.shape[1] == 1:
        return pl.BlockSpec((None, 1, MEM_W, MEM_LEN), lambda i: (layer, 0, 0, 0))
    return pl.BlockSpec((None, seqs, MEM_W, MEM_LEN), lambda i: (layer, i, 0, 0))


def _mixer_a(x, gpre, gpost, win, gln, bln, s_mat, bs, mkt, mvt, wout, *, rows_per_mem, emit_v, tiles=2,
             slab_rows, slab_offset=0, into=None):
    n = x.shape[0]
    seqs = tiles * TILE // rows_per_mem
    row = lambda w: pl.BlockSpec((tiles * TILE, w), lambda i: (i, 0))
    off = slab_offset // (tiles * TILE)
    out_shape = [jax.ShapeDtypeStruct((slab_rows, D_MODEL), F32)]
    out_specs = [pl.BlockSpec((tiles * TILE, D_MODEL), lambda i: (i + off, 0))]
    if emit_v:
        out_shape.append(jax.ShapeDtypeStruct((n, GM_W), F32))
        out_specs.append(row(GM_W))
    return _call(
        functools.partial(_mixer_a_kernel, rows_per_mem=rows_per_mem, tiles=tiles,
                          shared_mem=mkt.shape[1] == 1), "mixer_a", into=into,
        grid=(n // (tiles * TILE),),
        in_specs=[
            row(D_MODEL),
            _const_spec((1, D_MODEL)), _const_spec((1, D_MODEL)),
            _const_spec((D_MODEL, 2 * GM_W + MEM_W)),
            _const_spec((1, GM_W)), _const_spec((1, GM_W)),
            _const_spec((GM_GROUPS, TILE, TILE)), _const_spec((TILE, GM_W)),
            _mem_spec(mkt, seqs, 0), _mem_spec(mvt, seqs, 0),
            _const_spec((GM_W + MEM_W, D_MODEL)),
        ],
        out_specs=out_specs,
        out_shape=out_shape,
    )(x, gpre, gpost, win, gln, bln, s_mat, bs, mkt, mvt, wout)


def _ffn_kernel(x_ref, gpre_ref, gpost_ref, w1_ref, w2_ref, o_ref, *o_tail, rows, tiles, ff_chunk, head_steps):
    def tile(t):
        r = slice(t * rows, (t + 1) * rows)
        x = x_ref[r, :]
        xn = _rms(x, gpre_ref[...]).astype(BF16)
        acc = jnp.zeros(x.shape, F32)
        yield
        for c in range(D_FF // ff_chunk):
            h = _dot(xn, w1_ref[:, c * ff_chunk:(c + 1) * ff_chunk])
            h = jnp.square(jnp.maximum(h, 0.0)).astype(BF16)
            acc = acc + _dot(h, w2_ref[c * ff_chunk:(c + 1) * ff_chunk, :])
            yield
        y = x + _rms(acc, gpost_ref[...])
        if not o_tail:
            o_ref[r, :] = y
        else:
            is_head = pl.program_id(0) < head_steps
            o_ref[r, :] = jnp.where(is_head, y, o_ref[r, :])
            o_tail[0][r, :] = y

    if o_tail:
        @pl.when(pl.program_id(0) < head_steps)
        def _():
            o_ref[...] = jnp.zeros(o_ref.shape, F32)

    _interleave([tile(t) for t in range(tiles)])


def _ffn(x, gpre, gpost, w1, w2, layer, *, rows=512, tiles=2, ff_chunk=1024, split=None):
    n = x.shape[0]
    blk = tiles * rows
    row = pl.BlockSpec((blk, D_MODEL), lambda i: (i, 0))
    if split is None:
        head_steps, out_specs, out_shape = None, row, jax.ShapeDtypeStruct((n, D_MODEL), F32)
    else:
        head_steps = split // blk
        out_specs = [pl.BlockSpec((blk, D_MODEL), lambda i: (jnp.minimum(i, head_steps - 1), 0)),
                     pl.BlockSpec((blk, D_MODEL), lambda i: (jnp.maximum(i - head_steps, 0), 0))]
        out_shape = [jax.ShapeDtypeStruct((split, D_MODEL), F32), jax.ShapeDtypeStruct((n - split, D_MODEL), F32)]
    return _call(
        functools.partial(_ffn_kernel, rows=rows, tiles=tiles, ff_chunk=ff_chunk, head_steps=head_steps),
        "ffn", semantics="arbitrary",
        grid=(n // blk,),
        in_specs=[row, _layer_spec((1, D_MODEL), layer), _layer_spec((1, D_MODEL), layer),
                  _layer_spec((D_MODEL, D_FF), layer), _layer_spec((D_FF, D_MODEL), layer)],
        out_specs=out_specs,
        out_shape=out_shape,
    )(x, gpre, gpost, w1, w2)


def _proj_b_kernel(x_ref, gkv_ref, gpre_ref, *refs, mode, seq_len):
    rows = x_ref.shape[0]

    def normed(r):
        x = x_ref[r, :]
        xh = x * lax.rsqrt(jnp.mean(x * x, axis=-1, keepdims=True) + EPS)
        return (xh * gkv_ref[...]).astype(BF16), (xh * gpre_ref[...]).astype(BF16)

    if mode == "tail":
        wkt_ref, wvt_ref, kt_ref, vt_ref = refs
        xkv, _ = normed(slice(None))
        kt_ref[...] = _dot_nt(wkt_ref[...], xkv)
        vt_ref[...] = _dot_nt(wvt_ref[...], xkv)
        return
    if mode == "prompt":
        wqt_ref, wqm_ref, wk_ref, wvt_ref, qt_ref, qm_ref, k_ref, vt_ref = refs

        def tile(t):
            r = slice(t * PROJ_ROWS, (t + 1) * PROJ_ROWS)
            xkv, xq = normed(r)
            yield
            qt = _dot_nt(wqt_ref[...], xq) * Q_SCALE
            for p in range(HEAD_PAIRS):
                qt_ref[p, :, r] = qt[p * LANES:(p + 1) * LANES, :].astype(BF16)
            yield
            vt = _dot_nt(wvt_ref[...], xkv)
            for p in range(HEAD_PAIRS):
                vt_ref[p, :, r] = vt[p * LANES:(p + 1) * LANES, :].astype(BF16)
            yield
            k = _dot(xkv, wk_ref[...])
            for p in range(HEAD_PAIRS):
                k_ref[p, r, :] = k[:, p * LANES:(p + 1) * LANES].astype(BF16)
            yield
            qm_ref[r, :] = (_dot(xq, wqm_ref[...]) * Q_SCALE).astype(BF16)

        _interleave([tile(t) for t in range(rows // PROJ_ROWS)])
        return
    win_ref, wk_ref, wv_ref, q_ref, qm_ref, k_ref, v_ref, k4_ref, v4_ref = refs
    xkv, xq = normed(slice(None))
    z = _dot(xq, win_ref[...]) * Q_SCALE
    qm_ref[...] = z[:, B_W:].astype(BF16)
    k = _dot(xkv, wk_ref[...])
    v = _dot(xkv, wv_ref[...])
    for p in range(HEAD_PAIRS):
        cols = slice(p * LANES, (p + 1) * LANES)
        q_ref[p] = z[:, cols].astype(BF16)
        k_ref[p] = k[:, cols].astype(BF16)
        v_ref[p] = v[:, cols].astype(BF16)
    for b in range(rows // seq_len):
        for h in range(B_HEADS):
            r, c = slice(b * seq_len, (b + 1) * seq_len), slice(h * HEAD_DIM, (h + 1) * HEAD_DIM)
            k4_ref[b, h] = k[r, c]
            v4_ref[b, h] = v[r, c]


def _proj_b(x, gkv, gpre, weights, *, mode, rows, row_offset=0, n_rows=None, seq_len=None):
    n = x.shape[0] if n_rows is None else n_rows
    off = row_offset // rows
    pair_rows = jax.ShapeDtypeStruct((HEAD_PAIRS, n, LANES), BF16)
    pair_rows_spec = pl.BlockSpec((HEAD_PAIRS, rows, LANES), lambda i: (0, i, 0))
    pair_cols = jax.ShapeDtypeStruct((HEAD_PAIRS, LANES, n), BF16)
    pair_cols_spec = pl.BlockSpec((HEAD_PAIRS, LANES, rows), lambda i: (0, 0, i))
    qm = jax.ShapeDtypeStruct((n, MEM_W), BF16)
    qm_spec = pl.BlockSpec((rows, MEM_W), lambda i: (i, 0))
    if mode == "tail":
        out_shape = [jax.ShapeDtypeStruct((B_W, n), F32)] * 2
        out_specs = [pl.BlockSpec((B_W, rows), lambda i: (0, i))] * 2
    elif mode == "prompt":
        out_shape = [pair_cols, qm, pair_rows, pair_cols]
        out_specs = [pair_cols_spec, qm_spec, pair_rows_spec, pair_cols_spec]
    else:
        seqs = rows // seq_len
        per_head = jax.ShapeDtypeStruct((n // seq_len, B_HEADS, seq_len, HEAD_DIM), F32)
        per_head_spec = pl.BlockSpec((seqs, B_HEADS, seq_len, HEAD_DIM), lambda i: (i, 0, 0, 0))
        out_shape = [pair_rows, qm, pair_rows, pair_rows, per_head, per_head]
        out_specs = [pair_rows_spec, qm_spec, pair_rows_spec, pair_rows_spec, per_head_spec, per_head_spec]
    return _call(
        functools.partial(_proj_b_kernel, mode=mode, seq_len=seq_len), "proj_b_" + mode,
        grid=(n // rows,),
        in_specs=[pl.BlockSpec((rows, D_MODEL), lambda i: (i + off, 0)),
                  _const_spec((1, D_MODEL)), _const_spec((1, D_MODEL))]
                 + [_const_spec(w.shape) for w in weights],
        out_specs=out_specs,
        out_shape=out_shape,
    )(x, gkv, gpre, *weights)


def _bias_kernel(g_ref, *o_refs, n_q, n_k, splits, band, keys_on_rows):
    n_rows, shift = (n_k, n_q) if keys_on_rows else (n_q, BIAS_PERIOD - (n_q - 1))
    for hh in range(2):
        x = jnp.broadcast_to(g_ref[hh:hh + 1, :], (n_rows, BIAS_PERIOD))
        t = pltpu.roll(x, shift, 1, stride=1, stride_axis=0)
        if band:
            a = lax.broadcasted_iota(jnp.int32, (n_rows, BIAS_PERIOD), 0)
            b = lax.broadcasted_iota(jnp.int32, (n_rows, BIAS_PERIOD), 1)
            r, w = (b, a) if keys_on_rows else (a, b)
            j = w - (r - (r & (CHUNK - 1)))
            t = jnp.where((j >= 0) & (j < BAND_PAST + CHUNK), t, NEG)
        for o_ref, (lo, hi) in zip(o_refs, splits):
            o_ref[hh * n_rows:(hh + 1) * n_rows, :] = t[:, lo:hi]


def _rel_bias_tables(rel_bias, n_q, n_k, splits, *, band, keys_on_rows):
    rel_bias = rel_bias - rel_bias[:, -1:]
    c0 = n_k - 1
    far = jnp.broadcast_to(rel_bias[:, -1:], (B_HEADS, c0 - REL_CLIP))
    near = jnp.broadcast_to(rel_bias[:, :1], (B_HEADS, BIAS_PERIOD - (c0 - REL_CLIP) - (2 * REL_CLIP + 1)))
    gen = [near, rel_bias, far] if keys_on_rows else [far, rel_bias[:, ::-1], near]
    gen = jnp.concatenate(gen, axis=1).reshape(HEAD_PAIRS, 2, BIAS_PERIOD)
    n_rows = n_k if keys_on_rows else n_q
    return _call(
        functools.partial(_bias_kernel, n_q=n_q, n_k=n_k, splits=splits, band=band,
                          keys_on_rows=keys_on_rows), "rel_bias",
        grid=(HEAD_PAIRS,),
        in_specs=[pl.BlockSpec((None, 2, BIAS_PERIOD), lambda p: (p, 0, 0))],
        out_specs=[pl.BlockSpec((None, 2 * n_rows, hi - lo), lambda p: (p, 0, 0)) for lo, hi in splits],
        out_shape=[jax.ShapeDtypeStruct((HEAD_PAIRS, 2 * n_rows, hi - lo), F32) for lo, hi in splits],
    )(gen)


def _band_block_kinds():
    n_k = KEY_BLOCKS * TILE
    r = np.arange(TILE)[None, :]
    w = np.arange(n_k)[:, None]
    j = w - CHUNK * (r // CHUNK)
    ok = (j >= 0) & (j < BAND_PAST + CHUNK)
    plain = ok & (BAND_PAST + r - w >= REL_CLIP)
    kinds = []
    for a in range(n_k // KEY_BLOCK):
        rows = slice(a * KEY_BLOCK, (a + 1) * KEY_BLOCK)
        blocks = [(rows, slice(b * LANES, (b + 1) * LANES)) for b in range(TILE // LANES)]
        kinds.append(["skip" if not ok[blk].any() else "plain" if plain[blk].all() else "biased"
                      for blk in blocks])
    return kinds


def _band_prompt_kernel(x_ref, qt_ref, qm_ref, kp_ref, kc_ref, vtp_ref, vtc_ref,
                        bias_ref, mk_ref, mv_ref, wout_ref, gpost_ref, o_ref):
    i = pl.program_id(0)
    n_k = KEY_BLOCKS * TILE
    kinds = _band_block_kinds()
    w = lax.broadcasted_iota(jnp.int32, (n_k, LANES), 0)
    c = lax.broadcasted_iota(jnp.int32, (n_k, LANES), 1)
    ones_rows = jnp.where(lax.broadcasted_iota(jnp.int32, (BF16_ROWS, n_k), 0) == 0, 1.0, 0.0).astype(BF16)
    row = lax.broadcasted_iota(jnp.int32, (LANES, TILE), 0)
    one_hot_row = jnp.where(row == 0, 1.0, 0.0).astype(BF16)
    lo = row < HEAD_DIM

    def key_step(a, s, vth, bias_rows, state):
        blocks = range(a * KEY_STEP // KEY_BLOCK, (a + 1) * KEY_STEP // KEY_BLOCK)
        es, alphas, active = [], [], []
        for b in range(TILE // LANES):
            cols = slice(b * LANES, (b + 1) * LANES)
            sjs = {}
            for blk in blocks:
                if kinds[blk][b] != "skip":
                    rows = slice(blk * KEY_BLOCK, (blk + 1) * KEY_BLOCK)
                    sjs[blk] = s[rows, cols] + bias_rows(rows, cols) if kinds[blk][b] == "biased" else s[rows, cols]
            active.append(bool(sjs))
            alphas.append(None)
            if not sjs:
                es.append(jnp.zeros((KEY_STEP, LANES), F32))
                continue
            mj = jnp.max(functools.reduce(jnp.maximum, sjs.values()), axis=0, keepdims=True)
            if state[b] is None:
                state[b] = [mj, None]
            else:
                m_new = jnp.maximum(state[b][0], mj)
                alphas[b] = jnp.exp(state[b][0] - m_new)
                state[b][0] = m_new
            es.append(jnp.concatenate(
                [jnp.exp(sjs[blk] - state[b][0]) if blk in sjs else jnp.zeros((KEY_BLOCK, LANES), F32)
                 for blk in blocks], axis=0))
        rows = slice(a * KEY_STEP, (a + 1) * KEY_STEP)
        pv = _dot(vth[:, rows], jnp.concatenate(es, axis=1).astype(BF16))[:OUT_ROWS]
        for b in range(TILE // LANES):
            if active[b]:
                pv_b = pv[:, b * LANES:(b + 1) * LANES]
                state[b][1] = pv_b if alphas[b] is None else state[b][1] * alphas[b] + pv_b

    def tile(t):
        q_rows = slice(t * TILE, (t + 1) * TILE)
        k_rows = slice(t * TILE, t * TILE + n_k)
        first_key_tile = i * BAND_TILES + t - (KEY_BLOCKS - 1)
        pen = jnp.where((w < -first_key_tile * TILE) & (c == 0), NEG, 0.0).astype(BF16)
        outs = []
        for p0 in range(0, HEAD_PAIRS, GROUP_PAIRS):
            heads = []
            for p in range(p0, p0 + GROUP_PAIRS):
                k_win = jnp.concatenate([kp_ref[p], kc_ref[p]], axis=0)[k_rows]
                k_ext = jnp.concatenate([k_win, pen], axis=1)
                vt = jnp.concatenate([vtp_ref[p], vtc_ref[p]], axis=1)[:, k_rows]
                qt = qt_ref[p, :, q_rows]
                zero = jnp.zeros_like(qt)
                for hh in range(2):
                    qh = jnp.where(lo, qt, zero) if hh == 0 else jnp.where(lo, zero, qt)
                    s = _dot(k_ext, jnp.concatenate([qh, one_hot_row], axis=0))
                    vth = jnp.concatenate([vt[hh * HEAD_DIM:(hh + 1) * HEAD_DIM, :], ones_rows], axis=0)
                    bias_rows = functools.partial(
                        lambda rows, cols, p, base: bias_ref[p, base + rows.start:base + rows.stop, cols],
                        p=p, base=hh * n_k)
                    heads.append((s, vth, bias_rows, [None] * (TILE // LANES)))
            for a in range(n_k // KEY_STEP):
                for head in heads:
                    key_step(a, *head)
            outs += [jnp.concatenate([o[:HEAD_DIM] * (1.0 / o[HEAD_DIM:HEAD_DIM + 1]) for _, o in state], axis=1)
                     for *_, state in heads]
            yield
        band = jnp.concatenate(outs, axis=0).T.astype(BF16)
        mo = _mem_attend(qm_ref[q_rows, :], mk_ref[0].astype(BF16), mv_ref[0].astype(BF16))
        cat = jnp.concatenate([band, mo.astype(BF16)], axis=-1)
        o_ref[q_rows, :] = x_ref[q_rows, :] + _rms(_dot(cat, wout_ref[...]), gpost_ref[...])

    _interleave([tile(t) for t in range(BAND_TILES)])


def _band_prompt(x, qt3, qm, k3, vt3, bias, mkt, mvt, wout, gpost):
    n = qm.shape[0]
    rows = BAND_TILES * TILE
    assert rows == BAND_PAST and n % rows == 0 and (x.shape[0] - n) % rows == 0
    off = (x.shape[0] - n) // rows
    slab_spec = pl.BlockSpec((rows, D_MODEL), lambda i: (i + off, 0))
    past = lambda i: jnp.maximum(i - 1, 0)
    return _call(
        _band_prompt_kernel, "band_prompt",
        grid=(n // rows,),
        in_specs=[slab_spec,
                  pl.BlockSpec((HEAD_PAIRS, LANES, rows), lambda i: (0, 0, i)),
                  pl.BlockSpec((rows, MEM_W), lambda i: (i, 0)),
                  pl.BlockSpec((HEAD_PAIRS, rows, LANES), lambda i: (0, past(i), 0)),
                  pl.BlockSpec((HEAD_PAIRS, rows, LANES), lambda i: (0, i, 0)),
                  pl.BlockSpec((HEAD_PAIRS, LANES, rows), lambda i: (0, 0, past(i))),
                  pl.BlockSpec((HEAD_PAIRS, LANES, rows), lambda i: (0, 0, i)),
                  _const_spec((HEAD_PAIRS, 2 * KEY_BLOCKS * TILE, TILE)),
                  _mem_spec(mkt, 1, 1), _mem_spec(mvt, 1, 1),
                  _const_spec((B_W + MEM_W, D_MODEL)), _const_spec((1, D_MODEL))],
        out_specs=slab_spec,
        out_shape=jax.ShapeDtypeStruct(x.shape, F32),
    )(x, qt3, qm, k3, k3, vt3, vt3, bias, mkt, mvt, wout, gpost)


def _band_sample_kernel(x_ref, q_ref, qm_ref, kn_ref, vn_ref, ck_ref, cv_ref, bc_ref, bn_ref,
                        mk_ref, mv_ref, wout_ref, gpost_ref, o_ref, *, seqs, seq_len):
    lane = lax.broadcasted_iota(jnp.int32, (1, LANES), 1)
    lo = lane < HEAD_DIM

    def pair_attend(b, p, out):
        rows = slice(b * seq_len, (b + 1) * seq_len)
        hd = slice(p * LANES, (p + 1) * LANES)
        qp = q_ref[p, rows, :]
        zero = jnp.zeros_like(qp)
        qs = jnp.concatenate([jnp.where(lo, qp, zero), jnp.where(lo, zero, qp)], axis=0)
        sc = _dot(qs, ck_ref[b, hd, :].astype(BF16)) + bc_ref[p]
        sn = _dot_nt(qs, kn_ref[p, rows, :]) + bn_ref[p]
        yield
        m = jnp.maximum(_lmax(sc), _lmax(sn))
        ec = jnp.exp(sc - m)
        en = jnp.exp(sn - m)
        l = _lsum(ec) + _lsum(en)
        yield
        o = (_dot_nt(ec.astype(BF16), cv_ref[b, hd, :].astype(BF16))
             + _dot(en.astype(BF16), vn_ref[p, rows, :]))
        yield
        o = o * (1.0 / l)
        out[p] = jnp.where(lo, o[:seq_len], o[seq_len:]).astype(BF16)

    rows_out = []
    for b in range(seqs):
        band = [None] * HEAD_PAIRS
        _interleave([pair_attend(b, p, band) for p in range(HEAD_PAIRS)])
        rows = slice(b * seq_len, (b + 1) * seq_len)
        mo = _mem_attend(qm_ref[rows, :], mk_ref[b].astype(BF16), mv_ref[b].astype(BF16))
        rows_out.append(jnp.concatenate(band + [mo.astype(BF16)], axis=-1))
    cat = jnp.concatenate(rows_out, axis=0)
    o_ref[...] = x_ref[...] + _rms(_dot(cat, wout_ref[...]), gpost_ref[...])


def _band_sample(x, q3, qm, kn3, vn3, ckt, cvt, bias_c, bias_n, mkt, mvt, wout, gpost, *, into, seqs=4):
    n = qm.shape[0]
    n_seq, past = ckt.shape[0], ckt.shape[2]
    seq_len = n // n_seq
    rows = seqs * seq_len
    slab_spec = pl.BlockSpec((rows, D_MODEL), lambda i: (i, 0))
    pair_rows_spec = pl.BlockSpec((HEAD_PAIRS, rows, LANES), lambda i: (0, i, 0))
    cache_spec = pl.BlockSpec((seqs, B_W, past), lambda i: (i, 0, 0))
    return _call(
        functools.partial(_band_sample_kernel, seqs=seqs, seq_len=seq_len), "band_sample", into=into,
        grid=(n_seq // seqs,),
        in_specs=[slab_spec,
                  pair_rows_spec,
                  pl.BlockSpec((rows, MEM_W), lambda i: (i, 0)),
                  pair_rows_spec, pair_rows_spec,
                  cache_spec, cache_spec,
                  _const_spec((HEAD_PAIRS, 2 * seq_len, past)),
                  _const_spec((HEAD_PAIRS, 2 * seq_len, seq_len)),
                  _mem_spec(mkt, seqs, 1), _mem_spec(mvt, seqs, 1),
                  _const_spec((B_W + MEM_W, D_MODEL)), _const_spec((1, D_MODEL))],
        out_specs=slab_spec,
        out_shape=jax.ShapeDtypeStruct(x.shape, F32),
    )(x, q3, qm, kn3, vn3, ckt, cvt, bias_c, bias_n, mkt, mvt, wout, gpost)


def _spatial_tile(w_s, b_s, period):
    tril = jnp.tril(jnp.ones((GM_CHUNK, GM_CHUNK), dtype=bool))
    w = jnp.where(tril, w_s, jnp.zeros((), w_s.dtype))[:, :period, :period]
    eye = jnp.eye(TILE // period, dtype=w.dtype)
    s_mat = jnp.einsum("ab,gts->gatbs", eye, w).reshape(GM_GROUPS, TILE, TILE)
    rows = jnp.tile(b_s[:, :period], (1, TILE // period))
    bs = jnp.repeat(rows.T, GM_GW, axis=1)
    return s_mat.astype(BF16), bs


def _heads_last(t, lead):
    pos = t.shape[-1]
    t = t.reshape(lead + (-1, HEAD_DIM, pos))
    nd = len(lead)
    return jnp.transpose(t, tuple(range(nd)) + (nd + 2, nd, nd + 1))


def _positions_last(c):
    nd = c.ndim
    t = jnp.transpose(c, tuple(range(nd - 3)) + (nd - 2, nd - 1, nd - 3))
    return t.reshape(c.shape[:-3] + (c.shape[-2] * c.shape[-1], c.shape[-3]))


def kernel(x_prompt, x_sample, cache_mem_k, cache_mem_v, cache_band_k, cache_band_v, mem_prompt,
           g_mix_pre, g_mix_post, g_ffn_pre, g_ffn_post, g_mem, w_mem_kv,
           w_in_a, g_gm_ln, b_gm_ln, w_spatial, b_spatial, w_out_a,
           g_kv, w_kv, w_in_b, rel_bias, w_out_b, w_ff1, w_ff2):
    seq = x_prompt.shape[1]
    n_seq, seq_len = x_sample.shape[0], x_sample.shape[1]
    past = cache_band_k.shape[1]
    vec = lambda a: a.reshape(1, -1)
    stack = lambda a: a.reshape(DEPTH, 1, -1)

    win_a = w_in_a[0].astype(BF16)
    wout_a = w_out_a[0].astype(BF16)
    wk = w_kv[:, :B_W].astype(BF16)
    wv = w_kv[:, B_W:].astype(BF16)
    wkt, wvt = wk.T, wv.T
    win_b = w_in_b[0].astype(BF16)
    wqt, wqm = win_b[:, :B_W].T, win_b[:, B_W:]
    wout_b = w_out_b[0].astype(BF16)
    w1 = w_ff1.astype(BF16)
    w2 = w_ff2.astype(BF16)
    ln_g, ln_b = vec(g_gm_ln[0]), vec(b_gm_ln[0])
    gkv, gpre_b = vec(g_kv), vec(g_mix_pre[1])
    gf_pre, gf_post = stack(g_ffn_pre), stack(g_ffn_post)

    n_sample = n_seq * seq_len
    slab_rows = seq + n_sample
    pre_a, post_a = vec(g_mix_pre[0]), vec(g_mix_post[0])

    mem_kt, mem_vt = _memkv(mem_prompt[0], g_mem, jnp.swapaxes(w_mem_kv, 1, 2).astype(BF16))
    s_p, bs_p = _spatial_tile(w_spatial[0], b_spatial[0], GM_CHUNK)
    s_s, bs_s = _spatial_tile(w_spatial[0], b_spatial[0], seq_len)
    cmkt, cmvt = _positions_last(cache_mem_k), _positions_last(cache_mem_v)
    x, = _mixer_a(x_prompt[0], pre_a, post_a, win_a, ln_g, ln_b, s_p, bs_p, mem_kt, mem_vt, wout_a,
                  rows_per_mem=TILE, emit_v=False, tiles=4, slab_rows=slab_rows, slab_offset=n_sample)
    x, v_rows = _mixer_a(x_sample.reshape(n_sample, D_MODEL), pre_a, post_a, win_a, ln_g, ln_b, s_s, bs_s,
                         cmkt, cmvt, wout_a, rows_per_mem=seq_len, emit_v=True,
                         slab_rows=slab_rows, into=x)
    x = _ffn(x, gf_pre, gf_post, w1, w2, 0)

    qt3, qm, k3, vt3 = _proj_b(x, gkv, gpre_b, (wqt, wqm, wk, wvt), mode="prompt", rows=2 * PROJ_ROWS,
                               row_offset=n_sample, n_rows=seq)
    n_keep = min(BAND_PAST, seq)
    kt_tail, vt_tail = _proj_b(x, gkv, gpre_b, (wkt, wvt), mode="tail", rows=TILE,
                               row_offset=n_sample + seq - n_keep, n_rows=n_keep)
    q3s, qms, kn3, vn3, k_new, v_new = _proj_b(x, gkv, gpre_b, (win_b, wk, wv), mode="sample", rows=512,
                                               n_rows=n_sample, seq_len=seq_len)
    n_k = BAND_PAST + TILE
    bias_p, = _rel_bias_tables(rel_bias[0], TILE, n_k, ((0, TILE),), band=True, keys_on_rows=True)
    bias_c, bias_n = _rel_bias_tables(rel_bias[0], seq_len, past + seq_len,
                                      ((0, past), (past, past + seq_len)), band=False, keys_on_rows=False)
    post_b = vec(g_mix_post[1])
    y = _band_prompt(x, qt3, qm, k3, vt3, bias_p, mem_kt, mem_vt, wout_b, post_b)
    y = _band_sample(x, q3s, qms, kn3, vn3, _positions_last(cache_band_k), _positions_last(cache_band_v),
                     bias_c, bias_n, cmkt, cmvt, wout_b, post_b, into=y)
    y_sample, y_prompt = _ffn(y, gf_pre, gf_post, w1, w2, 1, split=n_sample)
    y_prompt = y_prompt[None]
    y_sample = y_sample.reshape(n_seq, seq_len, D_MODEL)

    return (y_prompt, y_sample,
            _heads_last(mem_kt, (DEPTH, 1)), _heads_last(mem_vt, (DEPTH, 1)),
            _heads_last(kt_tail, (1,)), _heads_last(vt_tail, (1,)),
            v_rows.reshape(1, n_seq, seq_len, GM_W),
            jnp.swapaxes(k_new, 1, 2), jnp.swapaxes(v_new, 1, 2))
```

```python
import functools

import jax
import jax.numpy as jnp
import numpy as np
from jax import lax
from jax.experimental import pallas as pl
from jax.experimental.pallas import tpu as pltpu

D_MODEL = 1024
DEPTH = 2
CHUNK = 64
HEAD_DIM = 64
GM_CHUNK = 128
GM_GROUPS = 4
GM_W = 768
GM_GW = GM_W // GM_GROUPS
MEM_LEN = 256
MEM_HEADS = 4
MEM_W = MEM_HEADS * HEAD_DIM
B_HEADS = 12
B_W = B_HEADS * HEAD_DIM
BAND_PAST = 512
REL_CLIP = 128
D_FF = 4 * D_MODEL
EPS = 1e-6

LANES = 128
HEAD_PAIRS = B_W // LANES
Q_SCALE = HEAD_DIM ** -0.5
NEG = -1e30
TILE = 256
KEY_BLOCKS = BAND_PAST // TILE + 1
KEY_STEP = 128
KEY_BLOCK = 128
PROJ_ROWS = 512
BF16_ROWS = 16
OUT_ROWS = HEAD_DIM + 8
BAND_TILES = BAND_PAST // TILE
GROUP_PAIRS = 2
BIAS_PERIOD = 1024
VMEM_LIMIT = 56 * 1024 * 1024

BF16 = jnp.bfloat16
F32 = jnp.float32


def _dot(a, b):
    return jnp.dot(a, b, preferred_element_type=F32)


def _dot_nt(a, b):
    return lax.dot_general(a, b, (((1,), (1,)), ((), ())), preferred_element_type=F32)


def _rms(x, g):
    ms = jnp.mean(x * x, axis=-1, keepdims=True)
    return x * lax.rsqrt(ms + EPS) * g


def _lsum(a):
    return jnp.sum(a, axis=-1, keepdims=True)


def _lmax(a):
    return jnp.max(a, axis=-1, keepdims=True)


def _const_spec(shape):
    nd = len(shape)
    return pl.BlockSpec(shape, lambda *_: (0,) * nd, pipeline_mode=pl.Buffered(1))


def _layer_spec(shape, layer):
    nd = len(shape)
    return pl.BlockSpec((None,) + shape, lambda *_: (layer,) + (0,) * nd, pipeline_mode=pl.Buffered(1))


def _without_ref(body, k):
    def wrapped(*refs):
        return body(*refs[:k], *refs[k + 1:])
    return wrapped


def _call(body, name, *, in_specs, into=None, semantics="parallel", **kw):
    params = pltpu.CompilerParams(dimension_semantics=(semantics,), vmem_limit_bytes=VMEM_LIMIT)
    if into is None:
        return pl.pallas_call(body, name=name, in_specs=in_specs, compiler_params=params, **kw)
    n_in = len(in_specs)
    call = pl.pallas_call(_without_ref(body, n_in), name=name,
                          in_specs=[*in_specs, pl.BlockSpec(memory_space=pl.ANY)],
                          input_output_aliases={n_in: 0}, compiler_params=params, **kw)
    return lambda *args: call(*args, into)


def _memkv_kernel(mem_ref, g_ref, wt_ref, kt_ref, vt_ref):
    ht = _dot_nt(wt_ref[...], _rms(mem_ref[...], g_ref[...]).astype(BF16))
    kt_ref[0] = ht[:MEM_W]
    vt_ref[0] = ht[MEM_W:]


def _memkv(mem, g_mem, w_mem_kv_t):
    out = jax.ShapeDtypeStruct((DEPTH, 1, MEM_W, MEM_LEN), F32)
    return _call(
        _memkv_kernel, "mem_kv",
        grid=(DEPTH,),
        in_specs=[
            pl.BlockSpec((MEM_LEN, D_MODEL), lambda l: (0, 0)),
            pl.BlockSpec((None, 1, D_MODEL), lambda l: (l, 0, 0)),
            pl.BlockSpec((None, 2 * MEM_W, D_MODEL), lambda l: (l, 0, 0)),
        ],
        out_specs=[pl.BlockSpec((None, 1, MEM_W, MEM_LEN), lambda l: (l, 0, 0, 0))] * 2,
        out_shape=[out, out],
    )(mem, g_mem.reshape(DEPTH, 1, D_MODEL), w_mem_kv_t)


def _mem_attend(qb, kt, vt):
    r = qb.shape[0]
    lane = lax.broadcasted_iota(jnp.int32, (1, MEM_W), 1)
    masks = [(lane >= h * HEAD_DIM) & (lane < (h + 1) * HEAD_DIM) for h in range(MEM_HEADS)]
    qs = jnp.concatenate([jnp.where(m, qb, jnp.zeros_like(qb)) for m in masks], axis=0)
    s = _dot(qs, kt)
    e = jnp.exp(s - _lmax(s))
    pv = _dot_nt(e.astype(BF16), vt) * (1.0 / _lsum(e))
    out = jnp.where(masks[0], pv[:r], 0.0)
    for h in range(1, MEM_HEADS):
        out = out + jnp.where(masks[h], pv[h * r:(h + 1) * r], 0.0)
    return out


def _gelu(x):
    c1 = float(np.sqrt(2.0 / np.pi))
    c2 = c1 * 0.044715
    half = 0.5 * x
    return half + half * jnp.tanh(x * (c1 + c2 * (x * x)))


def _after_fill(o_ref, fill_steps, body):
    if not fill_steps:
        return body()

    @pl.when(pl.program_id(0) < fill_steps)
    def _():
        o_ref[...] = jnp.zeros(o_ref.shape, o_ref.dtype)

    @pl.when(pl.program_id(0) >= fill_steps)
    def _():
        body()


def _interleave(stage_lists):
    live = list(stage_lists)
    while live:
        for g in list(live):
            if next(g, StopIteration) is StopIteration:
                live.remove(g)


def _mixer_a_kernel(x_ref, gpre_ref, gpost_ref, win_ref, gln_ref, bln_ref, s_ref, bs_ref,
                    mk_ref, mv_ref, wout_ref, o_ref, *v_out, rows_per_mem, tiles, shared_mem, fill_steps):
    nt = GM_W // LANES
    seqs = TILE // rows_per_mem
    lane = lax.broadcasted_iota(jnp.int32, (1, LANES), 1)
    lo = lane < (GM_GW - LANES)
    inv = 1.0 / GM_GW

    def group_stat(a):
        s0 = _lsum(a[0] + jnp.where(lo, a[1], 0.0)) * inv
        s1 = _lsum(jnp.where(lo, 0.0, a[1]) + a[2]) * inv
        s2 = _lsum(a[3] + jnp.where(lo, a[4], 0.0)) * inv
        s3 = _lsum(jnp.where(lo, 0.0, a[4]) + a[5]) * inv
        return [s0, jnp.where(lo, s0, s1), s1, s2, jnp.where(lo, s2, s3), s3]

    def tile(t):
        rows = slice(t * TILE, (t + 1) * TILE)
        z = _dot(_rms(x_ref[rows, :], gpre_ref[...]).astype(BF16), win_ref[...])
        yield
        u = [_gelu(z[:, j * LANES:(j + 1) * LANES]) for j in range(nt)]
        g = [_gelu(z[:, GM_W + j * LANES:GM_W + (j + 1) * LANES]) for j in range(nt)]
        mu = group_stat(g)
        c = [g[j] - mu[j] for j in range(nt)]
        var = group_stat([cj * cj for cj in c])
        gln = gln_ref[...]
        bln = bln_ref[...]
        vn = [c[j] * lax.rsqrt(var[j] + EPS) * gln[:, j * LANES:(j + 1) * LANES]
              + bln[:, j * LANES:(j + 1) * LANES] for j in range(nt)]
        if v_out:
            v_out[0][rows, :] = jnp.concatenate(vn, axis=-1)
        yield
        vb = [a.astype(BF16) for a in vn]
        win = [(0, 1), (1, 2), (3, 4), (4, 5)]
        m = [_dot(s_ref[k], jnp.concatenate([vb[a], vb[b]], axis=-1)) for k, (a, b) in enumerate(win)]
        mixed = [m[0][:, :LANES], jnp.where(lo, m[0][:, LANES:], m[1][:, :LANES]), m[1][:, LANES:],
                 m[2][:, :LANES], jnp.where(lo, m[2][:, LANES:], m[3][:, :LANES]), m[3][:, LANES:]]
        bs = bs_ref[...]
        gm = [u[j] * (mixed[j] + bs[:, j * LANES:(j + 1) * LANES]) for j in range(nt)]
        yield
        qm = (z[:, 2 * GM_W:] * Q_SCALE).astype(BF16)
        mo = []
        for b in range(seqs):
            r = slice(b * rows_per_mem, (b + 1) * rows_per_mem)
            mi = 0 if shared_mem else t * seqs + b
            mo.append(_mem_attend(qm[r], mk_ref[mi].astype(BF16), mv_ref[mi].astype(BF16)))
        mo = mo[0] if len(mo) == 1 else jnp.concatenate(mo, axis=0)
        yield
        cat = jnp.concatenate([a.astype(BF16) for a in gm] + [mo.astype(BF16)], axis=-1)
        o_ref[rows, :] = x_ref[rows, :] + _rms(_dot(cat, wout_ref[...]), gpost_ref[...])

    _after_fill(o_ref, fill_steps, lambda: _interleave([tile(t) for t in range(tiles)]))


def _mem_spec(mem, seqs, layer):
    if mem.shape[1] == 1:
        return pl.BlockSpec((None, 1, MEM_W, MEM_LEN), lambda i: (layer, 0, 0, 0))
    return pl.BlockSpec((None, seqs, MEM_W, MEM_LEN), lambda i: (layer, i, 0, 0))


def _mixer_a(x, gpre, gpost, win, gln, bln, s_mat, bs, mkt, mvt, wout, *, rows_per_mem, emit_v, tiles=2,
             slab_rows, slab_offset=0, into=None):
    n = x.shape[0]
    blk = tiles * TILE
    seqs = blk // rows_per_mem
    off = slab_offset // blk
    fill, shift = (0, off) if into is not None else (off, 0)
    row = lambda w: pl.BlockSpec((blk, w), lambda i: (jnp.maximum(i - fill, 0), 0))
    out_shape = [jax.ShapeDtypeStruct((slab_rows, D_MODEL), F32)]
    out_specs = [pl.BlockSpec((blk, D_MODEL), lambda i: (i + shift, 0))]
    if emit_v:
        out_shape.append(jax.ShapeDtypeStruct((n, GM_W), F32))
        out_specs.append(row(GM_W))
    return _call(
        functools.partial(_mixer_a_kernel, rows_per_mem=rows_per_mem, tiles=tiles,
                          shared_mem=mkt.shape[1] == 1, fill_steps=fill), "mixer_a", into=into,
        grid=(n // blk + fill,),
        in_specs=[
            row(D_MODEL),
            _const_spec((1, D_MODEL)), _const_spec((1, D_MODEL)),
            _const_spec((D_MODEL, 2 * GM_W + MEM_W)),
            _const_spec((1, GM_W)), _const_spec((1, GM_W)),
            _const_spec((GM_GROUPS, TILE, TILE)), _const_spec((TILE, GM_W)),
            _mem_spec(mkt, seqs, 0), _mem_spec(mvt, seqs, 0),
            _const_spec((GM_W + MEM_W, D_MODEL)),
        ],
        out_specs=out_specs,
        out_shape=out_shape,
    )(x, gpre, gpost, win, gln, bln, s_mat, bs, mkt, mvt, wout)


def _ffn_kernel(x_ref, gpre_ref, gpost_ref, w1_ref, w2_ref, o_ref, *o_tail, rows, tiles, ff_chunk, head_steps):
    def tile(t):
        r = slice(t * rows, (t + 1) * rows)
        x = x_ref[r, :]
        xn = _rms(x, gpre_ref[...]).astype(BF16)
        acc = jnp.zeros(x.shape, F32)
        yield
        for c in range(D_FF // ff_chunk):
            h = _dot(xn, w1_ref[:, c * ff_chunk:(c + 1) * ff_chunk])
            h = jnp.square(jnp.maximum(h, 0.0)).astype(BF16)
            acc = acc + _dot(h, w2_ref[c * ff_chunk:(c + 1) * ff_chunk, :])
            yield
        y = x + _rms(acc, gpost_ref[...])
        if not o_tail:
            o_ref[r, :] = y
        else:
            is_head = pl.program_id(0) < head_steps
            o_ref[r, :] = jnp.where(is_head, y, o_ref[r, :])
            o_tail[0][r, :] = y

    if o_tail:
        @pl.when(pl.program_id(0) < head_steps)
        def _():
            o_ref[...] = jnp.zeros(o_ref.shape, F32)

    _interleave([tile(t) for t in range(tiles)])


def _ffn(x, gpre, gpost, w1, w2, layer, *, rows=512, tiles=2, ff_chunk=1024, split=None):
    n = x.shape[0]
    blk = tiles * rows
    row = pl.BlockSpec((blk, D_MODEL), lambda i: (i, 0))
    if split is None:
        head_steps, out_specs, out_shape = None, row, jax.ShapeDtypeStruct((n, D_MODEL), F32)
    else:
        head_steps = split // blk
        out_specs = [pl.BlockSpec((blk, D_MODEL), lambda i: (jnp.minimum(i, head_steps - 1), 0)),
                     pl.BlockSpec((blk, D_MODEL), lambda i: (jnp.maximum(i - head_steps, 0), 0))]
        out_shape = [jax.ShapeDtypeStruct((split, D_MODEL), F32), jax.ShapeDtypeStruct((n - split, D_MODEL), F32)]
    return _call(
        functools.partial(_ffn_kernel, rows=rows, tiles=tiles, ff_chunk=ff_chunk, head_steps=head_steps),
        "ffn", semantics="arbitrary",
        grid=(n // blk,),
        in_specs=[row, _layer_spec((1, D_MODEL), layer), _layer_spec((1, D_MODEL), layer),
                  _layer_spec((D_MODEL, D_FF), layer), _layer_spec((D_FF, D_MODEL), layer)],
        out_specs=out_specs,
        out_shape=out_shape,
    )(x, gpre, gpost, w1, w2)


def _proj_b_kernel(x_ref, gkv_ref, gpre_ref, *refs, mode, seq_len):
    rows = x_ref.shape[0]

    def normed(r):
        x = x_ref[r, :]
        xh = x * lax.rsqrt(jnp.mean(x * x, axis=-1, keepdims=True) + EPS)
        return (xh * gkv_ref[...]).astype(BF16), (xh * gpre_ref[...]).astype(BF16)

    if mode == "tail":
        wkt_ref, wvt_ref, kt_ref, vt_ref = refs
        xkv, _ = normed(slice(None))
        kt_ref[...] = _dot_nt(wkt_ref[...], xkv)
        vt_ref[...] = _dot_nt(wvt_ref[...], xkv)
        return
    if mode == "prompt":
        wqt_ref, wqm_ref, wk_ref, wvt_ref, qt_ref, qm_ref, k_ref, vt_ref = refs

        def tile(t):
            r = slice(t * PROJ_ROWS, (t + 1) * PROJ_ROWS)
            xkv, xq = normed(r)
            yield
            qt = _dot_nt(wqt_ref[...], xq) * Q_SCALE
            for p in range(HEAD_PAIRS):
                qt_ref[p, :, r] = qt[p * LANES:(p + 1) * LANES, :].astype(BF16)
            yield
            vt = _dot_nt(wvt_ref[...], xkv)
            for p in range(HEAD_PAIRS):
                vt_ref[p, :, r] = vt[p * LANES:(p + 1) * LANES, :].astype(BF16)
            yield
            k = _dot(xkv, wk_ref[...])
            for p in range(HEAD_PAIRS):
                k_ref[p, r, :] = k[:, p * LANES:(p + 1) * LANES].astype(BF16)
            yield
            qm_ref[r, :] = (_dot(xq, wqm_ref[...]) * Q_SCALE).astype(BF16)

        _interleave([tile(t) for t in range(rows // PROJ_ROWS)])
        return
    win_ref, wk_ref, wv_ref, q_ref, qm_ref, k_ref, v_ref, k4_ref, v4_ref = refs
    xkv, xq = normed(slice(None))
    z = _dot(xq, win_ref[...]) * Q_SCALE
    qm_ref[...] = z[:, B_W:].astype(BF16)
    k = _dot(xkv, wk_ref[...])
    v = _dot(xkv, wv_ref[...])
    for p in range(HEAD_PAIRS):
        cols = slice(p * LANES, (p + 1) * LANES)
        q_ref[p] = z[:, cols].astype(BF16)
        k_ref[p] = k[:, cols].astype(BF16)
        v_ref[p] = v[:, cols].astype(BF16)
    for b in range(rows // seq_len):
        for h in range(B_HEADS):
            r, c = slice(b * seq_len, (b + 1) * seq_len), slice(h * HEAD_DIM, (h + 1) * HEAD_DIM)
            k4_ref[b, h] = k[r, c]
            v4_ref[b, h] = v[r, c]


def _proj_b(x, gkv, gpre, weights, *, mode, rows, row_offset=0, n_rows=None, seq_len=None):
    n = x.shape[0] if n_rows is None else n_rows
    off = row_offset // rows
    pair_rows = jax.ShapeDtypeStruct((HEAD_PAIRS, n, LANES), BF16)
    pair_rows_spec = pl.BlockSpec((HEAD_PAIRS, rows, LANES), lambda i: (0, i, 0))
    pair_cols = jax.ShapeDtypeStruct((HEAD_PAIRS, LANES, n), BF16)
    pair_cols_spec = pl.BlockSpec((HEAD_PAIRS, LANES, rows), lambda i: (0, 0, i))
    qm = jax.ShapeDtypeStruct((n, MEM_W), BF16)
    qm_spec = pl.BlockSpec((rows, MEM_W), lambda i: (i, 0))
    if mode == "tail":
        out_shape = [jax.ShapeDtypeStruct((B_W, n), F32)] * 2
        out_specs = [pl.BlockSpec((B_W, rows), lambda i: (0, i))] * 2
    elif mode == "prompt":
        out_shape = [pair_cols, qm, pair_rows, pair_cols]
        out_specs = [pair_cols_spec, qm_spec, pair_rows_spec, pair_cols_spec]
    else:
        seqs = rows // seq_len
        per_head = jax.ShapeDtypeStruct((n // seq_len, B_HEADS, seq_len, HEAD_DIM), F32)
        per_head_spec = pl.BlockSpec((seqs, B_HEADS, seq_len, HEAD_DIM), lambda i: (i, 0, 0, 0))
        out_shape = [pair_rows, qm, pair_rows, pair_rows, per_head, per_head]
        out_specs = [pair_rows_spec, qm_spec, pair_rows_spec, pair_rows_spec, per_head_spec, per_head_spec]
    return _call(
        functools.partial(_proj_b_kernel, mode=mode, seq_len=seq_len), "proj_b_" + mode,
        grid=(n // rows,),
        in_specs=[pl.BlockSpec((rows, D_MODEL), lambda i: (i + off, 0)),
                  _const_spec((1, D_MODEL)), _const_spec((1, D_MODEL))]
                 + [_const_spec(w.shape) for w in weights],
        out_specs=out_specs,
        out_shape=out_shape,
    )(x, gkv, gpre, *weights)


def _bias_kernel(g_ref, *o_refs, n_q, n_k, splits, band, keys_on_rows):
    n_rows, shift = (n_k, n_q) if keys_on_rows else (n_q, BIAS_PERIOD - (n_q - 1))
    for hh in range(2):
        x = jnp.broadcast_to(g_ref[hh:hh + 1, :], (n_rows, BIAS_PERIOD))
        t = pltpu.roll(x, shift, 1, stride=1, stride_axis=0)
        if band:
            a = lax.broadcasted_iota(jnp.int32, (n_rows, BIAS_PERIOD), 0)
            b = lax.broadcasted_iota(jnp.int32, (n_rows, BIAS_PERIOD), 1)
            r, w = (b, a) if keys_on_rows else (a, b)
            j = w - (r - (r & (CHUNK - 1)))
            t = jnp.where((j >= 0) & (j < BAND_PAST + CHUNK), t, NEG)
        for o_ref, (lo, hi) in zip(o_refs, splits):
            o_ref[hh * n_rows:(hh + 1) * n_rows, :] = t[:, lo:hi]


def _rel_bias_tables(rel_bias, n_q, n_k, splits, *, band, keys_on_rows):
    rel_bias = rel_bias - rel_bias[:, -1:]
    c0 = n_k - 1
    far = jnp.broadcast_to(rel_bias[:, -1:], (B_HEADS, c0 - REL_CLIP))
    near = jnp.broadcast_to(rel_bias[:, :1], (B_HEADS, BIAS_PERIOD - (c0 - REL_CLIP) - (2 * REL_CLIP + 1)))
    gen = [near, rel_bias, far] if keys_on_rows else [far, rel_bias[:, ::-1], near]
    gen = jnp.concatenate(gen, axis=1).reshape(HEAD_PAIRS, 2, BIAS_PERIOD)
    n_rows = n_k if keys_on_rows else n_q
    return _call(
        functools.partial(_bias_kernel, n_q=n_q, n_k=n_k, splits=splits, band=band,
                          keys_on_rows=keys_on_rows), "rel_bias",
        grid=(HEAD_PAIRS,),
        in_specs=[pl.BlockSpec((None, 2, BIAS_PERIOD), lambda p: (p, 0, 0))],
        out_specs=[pl.BlockSpec((None, 2 * n_rows, hi - lo), lambda p: (p, 0, 0)) for lo, hi in splits],
        out_shape=[jax.ShapeDtypeStruct((HEAD_PAIRS, 2 * n_rows, hi - lo), F32) for lo, hi in splits],
    )(gen)


def _band_block_kinds():
    n_k = KEY_BLOCKS * TILE
    r = np.arange(TILE)[None, :]
    w = np.arange(n_k)[:, None]
    j = w - CHUNK * (r // CHUNK)
    ok = (j >= 0) & (j < BAND_PAST + CHUNK)
    plain = ok & (BAND_PAST + r - w >= REL_CLIP)
    kinds = []
    for a in range(n_k // KEY_BLOCK):
        rows = slice(a * KEY_BLOCK, (a + 1) * KEY_BLOCK)
        blocks = [(rows, slice(b * LANES, (b + 1) * LANES)) for b in range(TILE // LANES)]
        kinds.append(["skip" if not ok[blk].any() else "plain" if plain[blk].all() else "biased"
                      for blk in blocks])
    return kinds


def _band_prompt_kernel(x_ref, qt_ref, qm_ref, kp_ref, kc_ref, vtp_ref, vtc_ref,
                        bias_ref, mk_ref, mv_ref, wout_ref, gpost_ref, o_ref, *, fill_steps):
    i = pl.program_id(0) - fill_steps
    n_k = KEY_BLOCKS * TILE
    kinds = _band_block_kinds()
    w = lax.broadcasted_iota(jnp.int32, (n_k, LANES), 0)
    c = lax.broadcasted_iota(jnp.int32, (n_k, LANES), 1)
    ones_rows = jnp.where(lax.broadcasted_iota(jnp.int32, (BF16_ROWS, n_k), 0) == 0, 1.0, 0.0).astype(BF16)
    row = lax.broadcasted_iota(jnp.int32, (LANES, TILE), 0)
    one_hot_row = jnp.where(row == 0, 1.0, 0.0).astype(BF16)
    lo = row < HEAD_DIM

    def key_step(a, s, vth, bias_rows, state):
        blocks = range(a * KEY_STEP // KEY_BLOCK, (a + 1) * KEY_STEP // KEY_BLOCK)
        es, alphas, active = [], [], []
        for b in range(TILE // LANES):
            cols = slice(b * LANES, (b + 1) * LANES)
            sjs = {}
            for blk in blocks:
                if kinds[blk][b] != "skip":
                    rows = slice(blk * KEY_BLOCK, (blk + 1) * KEY_BLOCK)
                    sjs[blk] = s[rows, cols] + bias_rows(rows, cols) if kinds[blk][b] == "biased" else s[rows, cols]
            active.append(bool(sjs))
            alphas.append(None)
            if not sjs:
                es.append(jnp.zeros((KEY_STEP, LANES), F32))
                continue
            mj = jnp.max(functools.reduce(jnp.maximum, sjs.values()), axis=0, keepdims=True)
            if state[b] is None:
                state[b] = [mj, None]
            else:
                m_new = jnp.maximum(state[b][0], mj)
                alphas[b] = jnp.exp(state[b][0] - m_new)
                state[b][0] = m_new
            es.append(jnp.concatenate(
                [jnp.exp(sjs[blk] - state[b][0]) if blk in sjs else jnp.zeros((KEY_BLOCK, LANES), F32)
                 for blk in blocks], axis=0))
        rows = slice(a * KEY_STEP, (a + 1) * KEY_STEP)
        pv = _dot(vth[:, rows], jnp.concatenate(es, axis=1).astype(BF16))[:OUT_ROWS]
        for b in range(TILE // LANES):
            if active[b]:
                pv_b = pv[:, b * LANES:(b + 1) * LANES]
                state[b][1] = pv_b if alphas[b] is None else state[b][1] * alphas[b] + pv_b

    def tile(t):
        q_rows = slice(t * TILE, (t + 1) * TILE)
        k_rows = slice(t * TILE, t * TILE + n_k)
        first_key_tile = i * BAND_TILES + t - (KEY_BLOCKS - 1)
        pen = jnp.where((w < -first_key_tile * TILE) & (c == 0), NEG, 0.0).astype(BF16)
        outs = []
        for p0 in range(0, HEAD_PAIRS, GROUP_PAIRS):
            heads = []
            for p in range(p0, p0 + GROUP_PAIRS):
                k_win = jnp.concatenate([kp_ref[p], kc_ref[p]], axis=0)[k_rows]
                k_ext = jnp.concatenate([k_win, pen], axis=1)
                vt = jnp.concatenate([vtp_ref[p], vtc_ref[p]], axis=1)[:, k_rows]
                qt = qt_ref[p, :, q_rows]
                zero = jnp.zeros_like(qt)
                for hh in range(2):
                    qh = jnp.where(lo, qt, zero) if hh == 0 else jnp.where(lo, zero, qt)
                    s = _dot(k_ext, jnp.concatenate([qh, one_hot_row], axis=0))
                    vth = jnp.concatenate([vt[hh * HEAD_DIM:(hh + 1) * HEAD_DIM, :], ones_rows], axis=0)
                    bias_rows = functools.partial(
                        lambda rows, cols, p, base: bias_ref[p, base + rows.start:base + rows.stop, cols],
                        p=p, base=hh * n_k)
                    heads.append((s, vth, bias_rows, [None] * (TILE // LANES)))
            for a in range(n_k // KEY_STEP):
                for head in heads:
                    key_step(a, *head)
            outs += [jnp.concatenate([o[:HEAD_DIM] * (1.0 / o[HEAD_DIM:HEAD_DIM + 1]) for _, o in state], axis=1)
                     for *_, state in heads]
            yield
        band = jnp.concatenate(outs, axis=0).T.astype(BF16)
        mo = _mem_attend(qm_ref[q_rows, :], mk_ref[0].astype(BF16), mv_ref[0].astype(BF16))
        cat = jnp.concatenate([band, mo.astype(BF16)], axis=-1)
        o_ref[q_rows, :] = x_ref[q_rows, :] + _rms(_dot(cat, wout_ref[...]), gpost_ref[...])

    _after_fill(o_ref, fill_steps, lambda: _interleave([tile(t) for t in range(BAND_TILES)]))


def _band_prompt(x, qt3, qm, k3, vt3, bias, mkt, mvt, wout, gpost):
    n = qm.shape[0]
    rows = BAND_TILES * TILE
    assert rows == BAND_PAST and n % rows == 0 and (x.shape[0] - n) % rows == 0
    fill = (x.shape[0] - n) // rows
    slab_spec = pl.BlockSpec((rows, D_MODEL), lambda i: (i, 0))
    own = lambda i: jnp.maximum(i - fill, 0)
    past = lambda i: jnp.maximum(i - fill - 1, 0)
    return _call(
        functools.partial(_band_prompt_kernel, fill_steps=fill), "band_prompt",
        grid=(n // rows + fill,),
        in_specs=[slab_spec,
                  pl.BlockSpec((HEAD_PAIRS, LANES, rows), lambda i: (0, 0, own(i))),
                  pl.BlockSpec((rows, MEM_W), lambda i: (own(i), 0)),
                  pl.BlockSpec((HEAD_PAIRS, rows, LANES), lambda i: (0, past(i), 0)),
                  pl.BlockSpec((HEAD_PAIRS, rows, LANES), lambda i: (0, own(i), 0)),
                  pl.BlockSpec((HEAD_PAIRS, LANES, rows), lambda i: (0, 0, past(i))),
                  pl.BlockSpec((HEAD_PAIRS, LANES, rows), lambda i: (0, 0, own(i))),
                  _const_spec((HEAD_PAIRS, 2 * KEY_BLOCKS * TILE, TILE)),
                  _mem_spec(mkt, 1, 1), _mem_spec(mvt, 1, 1),
                  _const_spec((B_W + MEM_W, D_MODEL)), _const_spec((1, D_MODEL))],
        out_specs=slab_spec,
        out_shape=jax.ShapeDtypeStruct(x.shape, F32),
    )(x, qt3, qm, k3, k3, vt3, vt3, bias, mkt, mvt, wout, gpost)


def _band_sample_kernel(x_ref, q_ref, qm_ref, kn_ref, vn_ref, ck_ref, cv_ref, bc_ref, bn_ref,
                        mk_ref, mv_ref, wout_ref, gpost_ref, o_ref, *, seqs, seq_len):
    lane = lax.broadcasted_iota(jnp.int32, (1, LANES), 1)
    lo = lane < HEAD_DIM

    def pair_attend(b, p, out):
        rows = slice(b * seq_len, (b + 1) * seq_len)
        hd = slice(p * LANES, (p + 1) * LANES)
        qp = q_ref[p, rows, :]
        zero = jnp.zeros_like(qp)
        qs = jnp.concatenate([jnp.where(lo, qp, zero), jnp.where(lo, zero, qp)], axis=0)
        sc = _dot(qs, ck_ref[b, hd, :].astype(BF16)) + bc_ref[p]
        sn = _dot_nt(qs, kn_ref[p, rows, :]) + bn_ref[p]
        yield
        m = jnp.maximum(_lmax(sc), _lmax(sn))
        ec = jnp.exp(sc - m)
        en = jnp.exp(sn - m)
        l = _lsum(ec) + _lsum(en)
        yield
        o = (_dot_nt(ec.astype(BF16), cv_ref[b, hd, :].astype(BF16))
             + _dot(en.astype(BF16), vn_ref[p, rows, :]))
        yield
        o = o * (1.0 / l)
        out[p] = jnp.where(lo, o[:seq_len], o[seq_len:]).astype(BF16)

    rows_out = []
    for b in range(seqs):
        band = [None] * HEAD_PAIRS
        _interleave([pair_attend(b, p, band) for p in range(HEAD_PAIRS)])
        rows = slice(b * seq_len, (b + 1) * seq_len)
        mo = _mem_attend(qm_ref[rows, :], mk_ref[b].astype(BF16), mv_ref[b].astype(BF16))
        rows_out.append(jnp.concatenate(band + [mo.astype(BF16)], axis=-1))
    cat = jnp.concatenate(rows_out, axis=0)
    o_ref[...] = x_ref[...] + _rms(_dot(cat, wout_ref[...]), gpost_ref[...])


def _band_sample(x, q3, qm, kn3, vn3, ckt, cvt, bias_c, bias_n, mkt, mvt, wout, gpost, *, into, seqs=4):
    n = qm.shape[0]
    n_seq, past = ckt.shape[0], ckt.shape[2]
    seq_len = n // n_seq
    rows = seqs * seq_len
    slab_spec = pl.BlockSpec((rows, D_MODEL), lambda i: (i, 0))
    pair_rows_spec = pl.BlockSpec((HEAD_PAIRS, rows, LANES), lambda i: (0, i, 0))
    cache_spec = pl.BlockSpec((seqs, B_W, past), lambda i: (i, 0, 0))
    return _call(
        functools.partial(_band_sample_kernel, seqs=seqs, seq_len=seq_len), "band_sample", into=into,
        grid=(n_seq // seqs,),
        in_specs=[slab_spec,
                  pair_rows_spec,
                  pl.BlockSpec((rows, MEM_W), lambda i: (i, 0)),
                  pair_rows_spec, pair_rows_spec,
                  cache_spec, cache_spec,
                  _const_spec((HEAD_PAIRS, 2 * seq_len, past)),
                  _const_spec((HEAD_PAIRS, 2 * seq_len, seq_len)),
                  _mem_spec(mkt, seqs, 1), _mem_spec(mvt, seqs, 1),
                  _const_spec((B_W + MEM_W, D_MODEL)), _const_spec((1, D_MODEL))],
        out_specs=slab_spec,
        out_shape=jax.ShapeDtypeStruct(x.shape, F32),
    )(x, q3, qm, kn3, vn3, ckt, cvt, bias_c, bias_n, mkt, mvt, wout, gpost)


def _spatial_tile(w_s, b_s, period):
    tril = jnp.tril(jnp.ones((GM_CHUNK, GM_CHUNK), dtype=bool))
    w = jnp.where(tril, w_s, jnp.zeros((), w_s.dtype))[:, :period, :period]
    eye = jnp.eye(TILE // period, dtype=w.dtype)
    s_mat = jnp.einsum("ab,gts->gatbs", eye, w).reshape(GM_GROUPS, TILE, TILE)
    rows = jnp.tile(b_s[:, :period], (1, TILE // period))
    bs = jnp.repeat(rows.T, GM_GW, axis=1)
    return s_mat.astype(BF16), bs


def _heads_last(t, lead):
    pos = t.shape[-1]
    t = t.reshape(lead + (-1, HEAD_DIM, pos))
    nd = len(lead)
    return jnp.transpose(t, tuple(range(nd)) + (nd + 2, nd, nd + 1))


def _positions_last(c):
    nd = c.ndim
    t = jnp.transpose(c, tuple(range(nd - 3)) + (nd - 2, nd - 1, nd - 3))
    return t.reshape(c.shape[:-3] + (c.shape[-2] * c.shape[-1], c.shape[-3]))


def kernel(x_prompt, x_sample, cache_mem_k, cache_mem_v, cache_band_k, cache_band_v, mem_prompt,
           g_mix_pre, g_mix_post, g_ffn_pre, g_ffn_post, g_mem, w_mem_kv,
           w_in_a, g_gm_ln, b_gm_ln, w_spatial, b_spatial, w_out_a,
           g_kv, w_kv, w_in_b, rel_bias, w_out_b, w_ff1, w_ff2):
    seq = x_prompt.shape[1]
    n_seq, seq_len = x_sample.shape[0], x_sample.shape[1]
    past = cache_band_k.shape[1]
    vec = lambda a: a.reshape(1, -1)
    stack = lambda a: a.reshape(DEPTH, 1, -1)

    win_a = w_in_a[0].astype(BF16)
    wout_a = w_out_a[0].astype(BF16)
    wk = w_kv[:, :B_W].astype(BF16)
    wv = w_kv[:, B_W:].astype(BF16)
    wkt, wvt = wk.T, wv.T
    win_b = w_in_b[0].astype(BF16)
    wqt, wqm = win_b[:, :B_W].T, win_b[:, B_W:]
    wout_b = w_out_b[0].astype(BF16)
    w1 = w_ff1.astype(BF16)
    w2 = w_ff2.astype(BF16)
    ln_g, ln_b = vec(g_gm_ln[0]), vec(b_gm_ln[0])
    gkv, gpre_b = vec(g_kv), vec(g_mix_pre[1])
    gf_pre, gf_post = stack(g_ffn_pre), stack(g_ffn_post)

    n_sample = n_seq * seq_len
    slab_rows = seq + n_sample
    pre_a, post_a = vec(g_mix_pre[0]), vec(g_mix_post[0])

    mem_kt, mem_vt = _memkv(mem_prompt[0], g_mem, jnp.swapaxes(w_mem_kv, 1, 2).astype(BF16))
    s_p, bs_p = _spatial_tile(w_spatial[0], b_spatial[0], GM_CHUNK)
    s_s, bs_s = _spatial_tile(w_spatial[0], b_spatial[0], seq_len)
    cmkt, cmvt = _positions_last(cache_mem_k), _positions_last(cache_mem_v)
    x, = _mixer_a(x_prompt[0], pre_a, post_a, win_a, ln_g, ln_b, s_p, bs_p, mem_kt, mem_vt, wout_a,
                  rows_per_mem=TILE, emit_v=False, tiles=4, slab_rows=slab_rows, slab_offset=n_sample)
    x, v_rows = _mixer_a(x_sample.reshape(n_sample, D_MODEL), pre_a, post_a, win_a, ln_g, ln_b, s_s, bs_s,
                         cmkt, cmvt, wout_a, rows_per_mem=seq_len, emit_v=True,
                         slab_rows=slab_rows, into=x)
    x = _ffn(x, gf_pre, gf_post, w1, w2, 0)

    qt3, qm, k3, vt3 = _proj_b(x, gkv, gpre_b, (wqt, wqm, wk, wvt), mode="prompt", rows=2 * PROJ_ROWS,
                               row_offset=n_sample, n_rows=seq)
    n_keep = min(BAND_PAST, seq)
    kt_tail, vt_tail = _proj_b(x, gkv, gpre_b, (wkt, wvt), mode="tail", rows=TILE,
                               row_offset=n_sample + seq - n_keep, n_rows=n_keep)
    q3s, qms, kn3, vn3, k_new, v_new = _proj_b(x, gkv, gpre_b, (win_b, wk, wv), mode="sample", rows=512,
                                               n_rows=n_sample, seq_len=seq_len)
    n_k = BAND_PAST + TILE
    bias_p, = _rel_bias_tables(rel_bias[0], TILE, n_k, ((0, TILE),), band=True, keys_on_rows=True)
    bias_c, bias_n = _rel_bias_tables(rel_bias[0], seq_len, past + seq_len,
                                      ((0, past), (past, past + seq_len)), band=False, keys_on_rows=False)
    post_b = vec(g_mix_post[1])
    y = _band_prompt(x, qt3, qm, k3, vt3, bias_p, mem_kt, mem_vt, wout_b, post_b)
    y = _band_sample(x, q3s, qms, kn3, vn3, _positions_last(cache_band_k), _positions_last(cache_band_v),
                     bias_c, bias_n, cmkt, cmvt, wout_b, post_b, into=y)
    y_sample, y_prompt = _ffn(y, gf_pre, gf_post, w1, w2, 1, split=n_sample)
    y_prompt = y_prompt[None]
    y_sample = y_sample.reshape(n_seq, seq_len, D_MODEL)

    return (y_prompt, y_sample,
            _heads_last(mem_kt, (DEPTH, 1)), _heads_last(mem_vt, (DEPTH, 1)),
            _heads_last(kt_tail, (1,)), _heads_last(vt_tail, (1,)),
            v_rows.reshape(1, n_seq, seq_len, GM_W),
            jnp.swapaxes(k_new, 1, 2), jnp.swapaxes(v_new, 1, 2))
```

```python
import functools

import jax
import jax.numpy as jnp
import numpy as np
from jax import lax
from jax.experimental import pallas as pl
from jax.experimental.pallas import tpu as pltpu

D_MODEL = 1024
DEPTH = 2
CHUNK = 64
HEAD_DIM = 64
GM_CHUNK = 128
GM_GROUPS = 4
GM_W = 768
GM_GW = GM_W // GM_GROUPS
MEM_LEN = 256
MEM_HEADS = 4
MEM_W = MEM_HEADS * HEAD_DIM
B_HEADS = 12
B_W = B_HEADS * HEAD_DIM
BAND_PAST = 512
REL_CLIP = 128
D_FF = 4 * D_MODEL
EPS = 1e-6

LANES = 128
HEAD_PAIRS = B_W // LANES
Q_SCALE = HEAD_DIM ** -0.5
NEG = -1e30
TILE = 256
KEY_BLOCKS = BAND_PAST // TILE + 1
KEY_STEP = 128
KEY_BLOCK = 128
PROJ_ROWS = 512
BF16_ROWS = 16
OUT_ROWS = HEAD_DIM + 8
BAND_TILES = BAND_PAST // TILE
GROUP_PAIRS = 2
BIAS_PERIOD = 1024
VMEM_LIMIT = 56 * 1024 * 1024

BF16 = jnp.bfloat16
F32 = jnp.float32


def _dot(a, b):
    return jnp.dot(a, b, preferred_element_type=F32)


def _dot_nt(a, b):
    return lax.dot_general(a, b, (((1,), (1,)), ((), ())), preferred_element_type=F32)


def _rms(x, g):
    ms = jnp.mean(x * x, axis=-1, keepdims=True)
    return x * lax.rsqrt(ms + EPS) * g


def _lsum(a):
    return jnp.sum(a, axis=-1, keepdims=True)


def _lmax(a):
    return jnp.max(a, axis=-1, keepdims=True)


def _const_spec(shape):
    nd = len(shape)
    return pl.BlockSpec(shape, lambda *_: (0,) * nd, pipeline_mode=pl.Buffered(1))


def _layer_spec(shape, layer):
    nd = len(shape)
    return pl.BlockSpec((None,) + shape, lambda *_: (layer,) + (0,) * nd, pipeline_mode=pl.Buffered(1))


def _without_ref(body, k):
    def wrapped(*refs):
        return body(*refs[:k], *refs[k + 1:])
    return wrapped


def _call(body, name, *, in_specs, into=None, semantics="parallel", **kw):
    params = pltpu.CompilerParams(dimension_semantics=(semantics,), vmem_limit_bytes=VMEM_LIMIT)
    if into is None:
        return pl.pallas_call(body, name=name, in_specs=in_specs, compiler_params=params, **kw)
    n_in = len(in_specs)
    call = pl.pallas_call(_without_ref(body, n_in), name=name,
                          in_specs=[*in_specs, pl.BlockSpec(memory_space=pl.ANY)],
                          input_output_aliases={n_in: 0}, compiler_params=params, **kw)
    return lambda *args: call(*args, into)


def _memkv_kernel(mem_ref, g_ref, wt_ref, kt_ref, vt_ref):
    ht = _dot_nt(wt_ref[...], _rms(mem_ref[...], g_ref[...]).astype(BF16))
    kt_ref[0] = ht[:MEM_W]
    vt_ref[0] = ht[MEM_W:]


def _memkv(mem, g_mem, w_mem_kv_t):
    out = jax.ShapeDtypeStruct((DEPTH, 1, MEM_W, MEM_LEN), F32)
    return _call(
        _memkv_kernel, "mem_kv",
        grid=(DEPTH,),
        in_specs=[
            pl.BlockSpec((MEM_LEN, D_MODEL), lambda l: (0, 0)),
            pl.BlockSpec((None, 1, D_MODEL), lambda l: (l, 0, 0)),
            pl.BlockSpec((None, 2 * MEM_W, D_MODEL), lambda l: (l, 0, 0)),
        ],
        out_specs=[pl.BlockSpec((None, 1, MEM_W, MEM_LEN), lambda l: (l, 0, 0, 0))] * 2,
        out_shape=[out, out],
    )(mem, g_mem.reshape(DEPTH, 1, D_MODEL), w_mem_kv_t)


def _mem_attend(qb, kt, vt):
    r = qb.shape[0]
    lane = lax.broadcasted_iota(jnp.int32, (1, MEM_W), 1)
    masks = [(lane >= h * HEAD_DIM) & (lane < (h + 1) * HEAD_DIM) for h in range(MEM_HEADS)]
    qs = jnp.concatenate([jnp.where(m, qb, jnp.zeros_like(qb)) for m in masks], axis=0)
    s = _dot(qs, kt)
    e = jnp.exp(s - _lmax(s))
    pv = _dot_nt(e.astype(BF16), vt) * (1.0 / _lsum(e))
    out = jnp.where(masks[0], pv[:r], 0.0)
    for h in range(1, MEM_HEADS):
        out = out + jnp.where(masks[h], pv[h * r:(h + 1) * r], 0.0)
    return out


def _gelu(x):
    c1 = float(np.sqrt(2.0 / np.pi))
    c2 = c1 * 0.044715
    half = 0.5 * x
    return half + half * jnp.tanh(x * (c1 + c2 * (x * x)))


def _after_fill(o_ref, fill_steps, body):
    if not fill_steps:
        return body()

    @pl.when(pl.program_id(0) < fill_steps)
    def _():
        o_ref[...] = jnp.zeros(o_ref.shape, o_ref.dtype)

    @pl.when(pl.program_id(0) >= fill_steps)
    def _():
        body()


def _round_robin(stage_lists):
    live = list(stage_lists)
    while live:
        for g in list(live):
            if next(g, StopIteration) is StopIteration:
                live.remove(g)
        yield


def _interleave(stage_lists):
    for _ in _round_robin(stage_lists):
        pass


def _mixer_a_kernel(x_ref, gpre_ref, gpost_ref, win_ref, gln_ref, bln_ref, s_ref, bs_ref,
                    mk_ref, mv_ref, wout_ref, o_ref, *v_out, rows_per_mem, tiles, shared_mem, fill_steps):
    nt = GM_W // LANES
    seqs = TILE // rows_per_mem
    lane = lax.broadcasted_iota(jnp.int32, (1, LANES), 1)
    lo = lane < (GM_GW - LANES)
    inv = 1.0 / GM_GW

    def group_stat(a):
        s0 = _lsum(a[0] + jnp.where(lo, a[1], 0.0)) * inv
        s1 = _lsum(jnp.where(lo, 0.0, a[1]) + a[2]) * inv
        s2 = _lsum(a[3] + jnp.where(lo, a[4], 0.0)) * inv
        s3 = _lsum(jnp.where(lo, 0.0, a[4]) + a[5]) * inv
        return [s0, jnp.where(lo, s0, s1), s1, s2, jnp.where(lo, s2, s3), s3]

    def tile(t):
        rows = slice(t * TILE, (t + 1) * TILE)
        z = _dot(_rms(x_ref[rows, :], gpre_ref[...]).astype(BF16), win_ref[...])
        yield
        u = [_gelu(z[:, j * LANES:(j + 1) * LANES]) for j in range(nt)]
        g = [_gelu(z[:, GM_W + j * LANES:GM_W + (j + 1) * LANES]) for j in range(nt)]
        mu = group_stat(g)
        c = [g[j] - mu[j] for j in range(nt)]
        var = group_stat([cj * cj for cj in c])
        gln = gln_ref[...]
        bln = bln_ref[...]
        vn = [c[j] * lax.rsqrt(var[j] + EPS) * gln[:, j * LANES:(j + 1) * LANES]
              + bln[:, j * LANES:(j + 1) * LANES] for j in range(nt)]
        if v_out:
            v_out[0][rows, :] = jnp.concatenate(vn, axis=-1)
        yield
        vb = [a.astype(BF16) for a in vn]
        win = [(0, 1), (1, 2), (3, 4), (4, 5)]
        m = [_dot(s_ref[k], jnp.concatenate([vb[a], vb[b]], axis=-1)) for k, (a, b) in enumerate(win)]
        mixed = [m[0][:, :LANES], jnp.where(lo, m[0][:, LANES:], m[1][:, :LANES]), m[1][:, LANES:],
                 m[2][:, :LANES], jnp.where(lo, m[2][:, LANES:], m[3][:, :LANES]), m[3][:, LANES:]]
        bs = bs_ref[...]
        gm = [u[j] * (mixed[j] + bs[:, j * LANES:(j + 1) * LANES]) for j in range(nt)]
        yield
        qm = (z[:, 2 * GM_W:] * Q_SCALE).astype(BF16)
        mo = []
        for b in range(seqs):
            r = slice(b * rows_per_mem, (b + 1) * rows_per_mem)
            mi = 0 if shared_mem else t * seqs + b
            mo.append(_mem_attend(qm[r], mk_ref[mi].astype(BF16), mv_ref[mi].astype(BF16)))
        mo = mo[0] if len(mo) == 1 else jnp.concatenate(mo, axis=0)
        yield
        cat = jnp.concatenate([a.astype(BF16) for a in gm] + [mo.astype(BF16)], axis=-1)
        o_ref[rows, :] = x_ref[rows, :] + _rms(_dot(cat, wout_ref[...]), gpost_ref[...])

    _after_fill(o_ref, fill_steps, lambda: _interleave([tile(t) for t in range(tiles)]))


def _mem_spec(mem, seqs, layer):
    if mem.shape[1] == 1:
        return pl.BlockSpec((None, 1, MEM_W, MEM_LEN), lambda i: (layer, 0, 0, 0))
    return pl.BlockSpec((None, seqs, MEM_W, MEM_LEN), lambda i: (layer, i, 0, 0))


def _mixer_a(x, gpre, gpost, win, gln, bln, s_mat, bs, mkt, mvt, wout, *, rows_per_mem, emit_v, tiles=2,
             slab_rows, slab_offset=0, into=None):
    n = x.shape[0]
    blk = tiles * TILE
    seqs = blk // rows_per_mem
    off = slab_offset // blk
    fill, shift = (0, off) if into is not None else (off, 0)
    row = lambda w: pl.BlockSpec((blk, w), lambda i: (jnp.maximum(i - fill, 0), 0))
    out_shape = [jax.ShapeDtypeStruct((slab_rows, D_MODEL), F32)]
    out_specs = [pl.BlockSpec((blk, D_MODEL), lambda i: (i + shift, 0))]
    if emit_v:
        out_shape.append(jax.ShapeDtypeStruct((n, GM_W), F32))
        out_specs.append(row(GM_W))
    return _call(
        functools.partial(_mixer_a_kernel, rows_per_mem=rows_per_mem, tiles=tiles,
                          shared_mem=mkt.shape[1] == 1, fill_steps=fill), "mixer_a", into=into,
        grid=(n // blk + fill,),
        in_specs=[
            row(D_MODEL),
            _const_spec((1, D_MODEL)), _const_spec((1, D_MODEL)),
            _const_spec((D_MODEL, 2 * GM_W + MEM_W)),
            _const_spec((1, GM_W)), _const_spec((1, GM_W)),
            _const_spec((GM_GROUPS, TILE, TILE)), _const_spec((TILE, GM_W)),
            _mem_spec(mkt, seqs, 0), _mem_spec(mvt, seqs, 0),
            _const_spec((GM_W + MEM_W, D_MODEL)),
        ],
        out_specs=out_specs,
        out_shape=out_shape,
    )(x, gpre, gpost, win, gln, bln, s_mat, bs, mkt, mvt, wout)


def _ffn_kernel(x_ref, gpre_ref, gpost_ref, w1_ref, w2_ref, o_ref, *o_tail, rows, tiles, ff_chunk, head_steps):
    def tile(t):
        r = slice(t * rows, (t + 1) * rows)
        x = x_ref[r, :]
        xn = _rms(x, gpre_ref[...]).astype(BF16)
        acc = jnp.zeros(x.shape, F32)
        yield
        for c in range(D_FF // ff_chunk):
            h = _dot(xn, w1_ref[:, c * ff_chunk:(c + 1) * ff_chunk])
            h = jnp.square(jnp.maximum(h, 0.0)).astype(BF16)
            acc = acc + _dot(h, w2_ref[c * ff_chunk:(c + 1) * ff_chunk, :])
            yield
        y = x + _rms(acc, gpost_ref[...])
        if not o_tail:
            o_ref[r, :] = y
        else:
            is_head = pl.program_id(0) < head_steps
            o_ref[r, :] = jnp.where(is_head, y, o_ref[r, :])
            o_tail[0][r, :] = y

    if o_tail:
        @pl.when(pl.program_id(0) < head_steps)
        def _():
            o_ref[...] = jnp.zeros(o_ref.shape, F32)

    _interleave([tile(t) for t in range(tiles)])


def _ffn(x, gpre, gpost, w1, w2, layer, *, rows=512, tiles=2, ff_chunk=1024, split=None):
    n = x.shape[0]
    blk = tiles * rows
    row = pl.BlockSpec((blk, D_MODEL), lambda i: (i, 0))
    if split is None:
        head_steps, out_specs, out_shape = None, row, jax.ShapeDtypeStruct((n, D_MODEL), F32)
    else:
        head_steps = split // blk
        out_specs = [pl.BlockSpec((blk, D_MODEL), lambda i: (jnp.minimum(i, head_steps - 1), 0)),
                     pl.BlockSpec((blk, D_MODEL), lambda i: (jnp.maximum(i - head_steps, 0), 0))]
        out_shape = [jax.ShapeDtypeStruct((split, D_MODEL), F32), jax.ShapeDtypeStruct((n - split, D_MODEL), F32)]
    return _call(
        functools.partial(_ffn_kernel, rows=rows, tiles=tiles, ff_chunk=ff_chunk, head_steps=head_steps),
        "ffn", semantics="arbitrary",
        grid=(n // blk,),
        in_specs=[row, _layer_spec((1, D_MODEL), layer), _layer_spec((1, D_MODEL), layer),
                  _layer_spec((D_MODEL, D_FF), layer), _layer_spec((D_FF, D_MODEL), layer)],
        out_specs=out_specs,
        out_shape=out_shape,
    )(x, gpre, gpost, w1, w2)


def _proj_b_kernel(x_ref, gkv_ref, gpre_ref, *refs, mode, seq_len):
    rows = x_ref.shape[0]

    def normed(r):
        x = x_ref[r, :]
        xh = x * lax.rsqrt(jnp.mean(x * x, axis=-1, keepdims=True) + EPS)
        return (xh * gkv_ref[...]).astype(BF16), (xh * gpre_ref[...]).astype(BF16)

    if mode == "tail":
        wkt_ref, wvt_ref, kt_ref, vt_ref = refs
        xkv, _ = normed(slice(None))
        kt_ref[...] = _dot_nt(wkt_ref[...], xkv)
        vt_ref[...] = _dot_nt(wvt_ref[...], xkv)
        return
    if mode == "prompt":
        wqt_ref, wqm_ref, wk_ref, wvt_ref, qt_ref, qm_ref, k_ref, vt_ref = refs

        def tile(t):
            r = slice(t * PROJ_ROWS, (t + 1) * PROJ_ROWS)
            xkv, xq = normed(r)
            yield
            qt = _dot_nt(wqt_ref[...], xq) * Q_SCALE
            for p in range(HEAD_PAIRS):
                qt_ref[p, :, r] = qt[p * LANES:(p + 1) * LANES, :].astype(BF16)
            yield
            vt = _dot_nt(wvt_ref[...], xkv)
            for p in range(HEAD_PAIRS):
                vt_ref[p, :, r] = vt[p * LANES:(p + 1) * LANES, :].astype(BF16)
            yield
            k = _dot(xkv, wk_ref[...])
            for p in range(HEAD_PAIRS):
                k_ref[p, r, :] = k[:, p * LANES:(p + 1) * LANES].astype(BF16)
            yield
            qm_ref[r, :] = (_dot(xq, wqm_ref[...]) * Q_SCALE).astype(BF16)

        _interleave([tile(t) for t in range(rows // PROJ_ROWS)])
        return
    win_ref, wk_ref, wv_ref, q_ref, qm_ref, k_ref, v_ref, k4_ref, v4_ref = refs
    xkv, xq = normed(slice(None))
    z = _dot(xq, win_ref[...]) * Q_SCALE
    qm_ref[...] = z[:, B_W:].astype(BF16)
    k = _dot(xkv, wk_ref[...])
    v = _dot(xkv, wv_ref[...])
    for p in range(HEAD_PAIRS):
        cols = slice(p * LANES, (p + 1) * LANES)
        q_ref[p] = z[:, cols].astype(BF16)
        k_ref[p] = k[:, cols].astype(BF16)
        v_ref[p] = v[:, cols].astype(BF16)
    for b in range(rows // seq_len):
        for h in range(B_HEADS):
            r, c = slice(b * seq_len, (b + 1) * seq_len), slice(h * HEAD_DIM, (h + 1) * HEAD_DIM)
            k4_ref[b, h] = k[r, c]
            v4_ref[b, h] = v[r, c]


def _proj_b(x, gkv, gpre, weights, *, mode, rows, row_offset=0, n_rows=None, seq_len=None):
    n = x.shape[0] if n_rows is None else n_rows
    off = row_offset // rows
    pair_rows = jax.ShapeDtypeStruct((HEAD_PAIRS, n, LANES), BF16)
    pair_rows_spec = pl.BlockSpec((HEAD_PAIRS, rows, LANES), lambda i: (0, i, 0))
    pair_cols = jax.ShapeDtypeStruct((HEAD_PAIRS, LANES, n), BF16)
    pair_cols_spec = pl.BlockSpec((HEAD_PAIRS, LANES, rows), lambda i: (0, 0, i))
    qm = jax.ShapeDtypeStruct((n, MEM_W), BF16)
    qm_spec = pl.BlockSpec((rows, MEM_W), lambda i: (i, 0))
    if mode == "tail":
        out_shape = [jax.ShapeDtypeStruct((B_W, n), F32)] * 2
        out_specs = [pl.BlockSpec((B_W, rows), lambda i: (0, i))] * 2
    elif mode == "prompt":
        out_shape = [pair_cols, qm, pair_rows, pair_cols]
        out_specs = [pair_cols_spec, qm_spec, pair_rows_spec, pair_cols_spec]
    else:
        seqs = rows // seq_len
        per_head = jax.ShapeDtypeStruct((n // seq_len, B_HEADS, seq_len, HEAD_DIM), F32)
        per_head_spec = pl.BlockSpec((seqs, B_HEADS, seq_len, HEAD_DIM), lambda i: (i, 0, 0, 0))
        out_shape = [pair_rows, qm, pair_rows, pair_rows, per_head, per_head]
        out_specs = [pair_rows_spec, qm_spec, pair_rows_spec, pair_rows_spec, per_head_spec, per_head_spec]
    return _call(
        functools.partial(_proj_b_kernel, mode=mode, seq_len=seq_len), "proj_b_" + mode,
        grid=(n // rows,),
        in_specs=[pl.BlockSpec((rows, D_MODEL), lambda i: (i + off, 0)),
                  _const_spec((1, D_MODEL)), _const_spec((1, D_MODEL))]
                 + [_const_spec(w.shape) for w in weights],
        out_specs=out_specs,
        out_shape=out_shape,
    )(x, gkv, gpre, *weights)


def _bias_kernel(g_ref, *o_refs, n_q, n_k, splits, band, keys_on_rows):
    n_rows, shift = (n_k, n_q) if keys_on_rows else (n_q, BIAS_PERIOD - (n_q - 1))
    for hh in range(2):
        x = jnp.broadcast_to(g_ref[hh:hh + 1, :], (n_rows, BIAS_PERIOD))
        t = pltpu.roll(x, shift, 1, stride=1, stride_axis=0)
        if band:
            a = lax.broadcasted_iota(jnp.int32, (n_rows, BIAS_PERIOD), 0)
            b = lax.broadcasted_iota(jnp.int32, (n_rows, BIAS_PERIOD), 1)
            r, w = (b, a) if keys_on_rows else (a, b)
            j = w - (r - (r & (CHUNK - 1)))
            t = jnp.where((j >= 0) & (j < BAND_PAST + CHUNK), t, NEG)
        for o_ref, (lo, hi) in zip(o_refs, splits):
            o_ref[hh * n_rows:(hh + 1) * n_rows, :] = t[:, lo:hi]


def _rel_bias_tables(rel_bias, n_q, n_k, splits, *, band, keys_on_rows):
    rel_bias = rel_bias - rel_bias[:, -1:]
    c0 = n_k - 1
    far = jnp.broadcast_to(rel_bias[:, -1:], (B_HEADS, c0 - REL_CLIP))
    near = jnp.broadcast_to(rel_bias[:, :1], (B_HEADS, BIAS_PERIOD - (c0 - REL_CLIP) - (2 * REL_CLIP + 1)))
    gen = [near, rel_bias, far] if keys_on_rows else [far, rel_bias[:, ::-1], near]
    gen = jnp.concatenate(gen, axis=1).reshape(HEAD_PAIRS, 2, BIAS_PERIOD)
    n_rows = n_k if keys_on_rows else n_q
    return _call(
        functools.partial(_bias_kernel, n_q=n_q, n_k=n_k, splits=splits, band=band,
                          keys_on_rows=keys_on_rows), "rel_bias",
        grid=(HEAD_PAIRS,),
        in_specs=[pl.BlockSpec((None, 2, BIAS_PERIOD), lambda p: (p, 0, 0))],
        out_specs=[pl.BlockSpec((None, 2 * n_rows, hi - lo), lambda p: (p, 0, 0)) for lo, hi in splits],
        out_shape=[jax.ShapeDtypeStruct((HEAD_PAIRS, 2 * n_rows, hi - lo), F32) for lo, hi in splits],
    )(gen)


def _band_block_kinds():
    n_k = KEY_BLOCKS * TILE
    r = np.arange(TILE)[None, :]
    w = np.arange(n_k)[:, None]
    j = w - CHUNK * (r // CHUNK)
    ok = (j >= 0) & (j < BAND_PAST + CHUNK)
    plain = ok & (BAND_PAST + r - w >= REL_CLIP)
    kinds = []
    for a in range(n_k // KEY_BLOCK):
        rows = slice(a * KEY_BLOCK, (a + 1) * KEY_BLOCK)
        blocks = [(rows, slice(b * LANES, (b + 1) * LANES)) for b in range(TILE // LANES)]
        kinds.append(["skip" if not ok[blk].any() else "plain" if plain[blk].all() else "biased"
                      for blk in blocks])
    return kinds


def _band_prompt_kernel(x_ref, qt_ref, qm_ref, kp_ref, kc_ref, vtp_ref, vtc_ref,
                        bias_ref, mk_ref, mv_ref, wout_ref, gpost_ref, o_ref, *, fill_steps):
    i = pl.program_id(0) - fill_steps
    n_k = KEY_BLOCKS * TILE
    kinds = _band_block_kinds()
    w = lax.broadcasted_iota(jnp.int32, (n_k, LANES), 0)
    c = lax.broadcasted_iota(jnp.int32, (n_k, LANES), 1)
    ones_rows = jnp.where(lax.broadcasted_iota(jnp.int32, (BF16_ROWS, n_k), 0) == 0, 1.0, 0.0).astype(BF16)
    row = lax.broadcasted_iota(jnp.int32, (LANES, TILE), 0)
    one_hot_row = jnp.where(row == 0, 1.0, 0.0).astype(BF16)
    lo = row < HEAD_DIM

    def key_step(a, s, vth, bias_rows, state):
        blocks = range(a * KEY_STEP // KEY_BLOCK, (a + 1) * KEY_STEP // KEY_BLOCK)
        es, alphas, active = [], [], []
        for b in range(TILE // LANES):
            cols = slice(b * LANES, (b + 1) * LANES)
            sjs = {}
            for blk in blocks:
                if kinds[blk][b] != "skip":
                    rows = slice(blk * KEY_BLOCK, (blk + 1) * KEY_BLOCK)
                    sjs[blk] = s[rows, cols] + bias_rows(rows, cols) if kinds[blk][b] == "biased" else s[rows, cols]
            active.append(bool(sjs))
            alphas.append(None)
            if not sjs:
                es.append(jnp.zeros((KEY_STEP, LANES), F32))
                continue
            mj = jnp.max(functools.reduce(jnp.maximum, sjs.values()), axis=0, keepdims=True)
            if state[b] is None:
                state[b] = [mj, None]
            else:
                m_new = jnp.maximum(state[b][0], mj)
                alphas[b] = jnp.exp(state[b][0] - m_new)
                state[b][0] = m_new
            es.append(jnp.concatenate(
                [jnp.exp(sjs[blk] - state[b][0]) if blk in sjs else jnp.zeros((KEY_BLOCK, LANES), F32)
                 for blk in blocks], axis=0))
        rows = slice(a * KEY_STEP, (a + 1) * KEY_STEP)
        pv = _dot(vth[:, rows], jnp.concatenate(es, axis=1).astype(BF16))[:OUT_ROWS]
        for b in range(TILE // LANES):
            if active[b]:
                pv_b = pv[:, b * LANES:(b + 1) * LANES]
                state[b][1] = pv_b if alphas[b] is None else state[b][1] * alphas[b] + pv_b

    def tile(t):
        q_rows = slice(t * TILE, (t + 1) * TILE)
        k_rows = slice(t * TILE, t * TILE + n_k)
        first_key_tile = i * BAND_TILES + t - (KEY_BLOCKS - 1)
        pen = jnp.where((w < -first_key_tile * TILE) & (c == 0), NEG, 0.0).astype(BF16)
        outs = []
        for p0 in range(0, HEAD_PAIRS, GROUP_PAIRS):
            heads = []
            for p in range(p0, p0 + GROUP_PAIRS):
                k_win = jnp.concatenate([kp_ref[p], kc_ref[p]], axis=0)[k_rows]
                k_ext = jnp.concatenate([k_win, pen], axis=1)
                vt = jnp.concatenate([vtp_ref[p], vtc_ref[p]], axis=1)[:, k_rows]
                qt = qt_ref[p, :, q_rows]
                zero = jnp.zeros_like(qt)
                for hh in range(2):
                    qh = jnp.where(lo, qt, zero) if hh == 0 else jnp.where(lo, zero, qt)
                    s = _dot(k_ext, jnp.concatenate([qh, one_hot_row], axis=0))
                    vth = jnp.concatenate([vt[hh * HEAD_DIM:(hh + 1) * HEAD_DIM, :], ones_rows], axis=0)
                    bias_rows = functools.partial(
                        lambda rows, cols, p, base: bias_ref[p, base + rows.start:base + rows.stop, cols],
                        p=p, base=hh * n_k)
                    heads.append((s, vth, bias_rows, [None] * (TILE // LANES)))
            for a in range(n_k // KEY_STEP):
                for head in heads:
                    key_step(a, *head)
            outs += [jnp.concatenate([o[:HEAD_DIM] * (1.0 / o[HEAD_DIM:HEAD_DIM + 1]) for _, o in state], axis=1)
                     for *_, state in heads]
            yield
        band = jnp.concatenate(outs, axis=0).T.astype(BF16)
        mo = _mem_attend(qm_ref[q_rows, :], mk_ref[0].astype(BF16), mv_ref[0].astype(BF16))
        cat = jnp.concatenate([band, mo.astype(BF16)], axis=-1)
        o_ref[q_rows, :] = x_ref[q_rows, :] + _rms(_dot(cat, wout_ref[...]), gpost_ref[...])

    _after_fill(o_ref, fill_steps, lambda: _interleave([tile(t) for t in range(BAND_TILES)]))


def _band_prompt(x, qt3, qm, k3, vt3, bias, mkt, mvt, wout, gpost):
    n = qm.shape[0]
    rows = BAND_TILES * TILE
    assert rows == BAND_PAST and n % rows == 0 and (x.shape[0] - n) % rows == 0
    fill = (x.shape[0] - n) // rows
    slab_spec = pl.BlockSpec((rows, D_MODEL), lambda i: (i, 0))
    own = lambda i: jnp.maximum(i - fill, 0)
    past = lambda i: jnp.maximum(i - fill - 1, 0)
    return _call(
        functools.partial(_band_prompt_kernel, fill_steps=fill), "band_prompt",
        grid=(n // rows + fill,),
        in_specs=[slab_spec,
                  pl.BlockSpec((HEAD_PAIRS, LANES, rows), lambda i: (0, 0, own(i))),
                  pl.BlockSpec((rows, MEM_W), lambda i: (own(i), 0)),
                  pl.BlockSpec((HEAD_PAIRS, rows, LANES), lambda i: (0, past(i), 0)),
                  pl.BlockSpec((HEAD_PAIRS, rows, LANES), lambda i: (0, own(i), 0)),
                  pl.BlockSpec((HEAD_PAIRS, LANES, rows), lambda i: (0, 0, past(i))),
                  pl.BlockSpec((HEAD_PAIRS, LANES, rows), lambda i: (0, 0, own(i))),
                  _const_spec((HEAD_PAIRS, 2 * KEY_BLOCKS * TILE, TILE)),
                  _mem_spec(mkt, 1, 1), _mem_spec(mvt, 1, 1),
                  _const_spec((B_W + MEM_W, D_MODEL)), _const_spec((1, D_MODEL))],
        out_specs=slab_spec,
        out_shape=jax.ShapeDtypeStruct(x.shape, F32),
    )(x, qt3, qm, k3, k3, vt3, vt3, bias, mkt, mvt, wout, gpost)


def _band_sample_kernel(x_ref, q_ref, qm_ref, kn_ref, vn_ref, ck_ref, cv_ref, bc_ref, bn_ref,
                        mk_ref, mv_ref, wout_ref, gpost_ref, o_ref, *, seqs, seq_len):
    lane = lax.broadcasted_iota(jnp.int32, (1, LANES), 1)
    lo = lane < HEAD_DIM

    def pair_attend(b, p, out):
        rows = slice(b * seq_len, (b + 1) * seq_len)
        hd = slice(p * LANES, (p + 1) * LANES)
        qp = q_ref[p, rows, :]
        zero = jnp.zeros_like(qp)
        qs = jnp.concatenate([jnp.where(lo, qp, zero), jnp.where(lo, zero, qp)], axis=0)
        sc = _dot(qs, ck_ref[b, hd, :].astype(BF16)) + bc_ref[p]
        sn = _dot_nt(qs, kn_ref[p, rows, :]) + bn_ref[p]
        yield
        m = jnp.maximum(_lmax(sc), _lmax(sn))
        ec = jnp.exp(sc - m)
        en = jnp.exp(sn - m)
        l = _lsum(ec) + _lsum(en)
        yield
        o = (_dot_nt(ec.astype(BF16), cv_ref[b, hd, :].astype(BF16))
             + _dot(en.astype(BF16), vn_ref[p, rows, :]))
        yield
        o = o * (1.0 / l)
        out[p] = jnp.where(lo, o[:seq_len], o[seq_len:]).astype(BF16)

    def seq_attend(b, out):
        band = [None] * HEAD_PAIRS
        yield from _round_robin([pair_attend(b, p, band) for p in range(HEAD_PAIRS)])
        rows = slice(b * seq_len, (b + 1) * seq_len)
        mo = _mem_attend(qm_ref[rows, :], mk_ref[b].astype(BF16), mv_ref[b].astype(BF16))
        out[b] = jnp.concatenate(band + [mo.astype(BF16)], axis=-1)

    rows_out = [None] * seqs
    _interleave([seq_attend(b, rows_out) for b in range(seqs)])
    cat = jnp.concatenate(rows_out, axis=0)
    o_ref[...] = x_ref[...] + _rms(_dot(cat, wout_ref[...]), gpost_ref[...])


def _band_sample(x, q3, qm, kn3, vn3, ckt, cvt, bias_c, bias_n, mkt, mvt, wout, gpost, *, into, seqs=4):
    n = qm.shape[0]
    n_seq, past = ckt.shape[0], ckt.shape[2]
    seq_len = n // n_seq
    rows = seqs * seq_len
    slab_spec = pl.BlockSpec((rows, D_MODEL), lambda i: (i, 0))
    pair_rows_spec = pl.BlockSpec((HEAD_PAIRS, rows, LANES), lambda i: (0, i, 0))
    cache_spec = pl.BlockSpec((seqs, B_W, past), lambda i: (i, 0, 0))
    return _call(
        functools.partial(_band_sample_kernel, seqs=seqs, seq_len=seq_len), "band_sample", into=into,
        grid=(n_seq // seqs,),
        in_specs=[slab_spec,
                  pair_rows_spec,
                  pl.BlockSpec((rows, MEM_W), lambda i: (i, 0)),
                  pair_rows_spec, pair_rows_spec,
                  cache_spec, cache_spec,
                  _const_spec((HEAD_PAIRS, 2 * seq_len, past)),
                  _const_spec((HEAD_PAIRS, 2 * seq_len, seq_len)),
                  _mem_spec(mkt, seqs, 1), _mem_spec(mvt, seqs, 1),
                  _const_spec((B_W + MEM_W, D_MODEL)), _const_spec((1, D_MODEL))],
        out_specs=slab_spec,
        out_shape=jax.ShapeDtypeStruct(x.shape, F32),
    )(x, q3, qm, kn3, vn3, ckt, cvt, bias_c, bias_n, mkt, mvt, wout, gpost)


def _spatial_tile(w_s, b_s, period):
    tril = jnp.tril(jnp.ones((GM_CHUNK, GM_CHUNK), dtype=bool))
    w = jnp.where(tril, w_s, jnp.zeros((), w_s.dtype))[:, :period, :period]
    eye = jnp.eye(TILE // period, dtype=w.dtype)
    s_mat = jnp.einsum("ab,gts->gatbs", eye, w).reshape(GM_GROUPS, TILE, TILE)
    rows = jnp.tile(b_s[:, :period], (1, TILE // period))
    bs = jnp.repeat(rows.T, GM_GW, axis=1)
    return s_mat.astype(BF16), bs


def _heads_last(t, lead):
    pos = t.shape[-1]
    t = t.reshape(lead + (-1, HEAD_DIM, pos))
    nd = len(lead)
    return jnp.transpose(t, tuple(range(nd)) + (nd + 2, nd, nd + 1))


def _positions_last(c):
    nd = c.ndim
    t = jnp.transpose(c, tuple(range(nd - 3)) + (nd - 2, nd - 1, nd - 3))
    return t.reshape(c.shape[:-3] + (c.shape[-2] * c.shape[-1], c.shape[-3]))


def kernel(x_prompt, x_sample, cache_mem_k, cache_mem_v, cache_band_k, cache_band_v, mem_prompt,
           g_mix_pre, g_mix_post, g_ffn_pre, g_ffn_post, g_mem, w_mem_kv,
           w_in_a, g_gm_ln, b_gm_ln, w_spatial, b_spatial, w_out_a,
           g_kv, w_kv, w_in_b, rel_bias, w_out_b, w_ff1, w_ff2):
    seq = x_prompt.shape[1]
    n_seq, seq_len = x_sample.shape[0], x_sample.shape[1]
    past = cache_band_k.shape[1]
    vec = lambda a: a.reshape(1, -1)
    stack = lambda a: a.reshape(DEPTH, 1, -1)

    win_a = w_in_a[0].astype(BF16)
    wout_a = w_out_a[0].astype(BF16)
    wk = w_kv[:, :B_W].astype(BF16)
    wv = w_kv[:, B_W:].astype(BF16)
    wkt, wvt = wk.T, wv.T
    win_b = w_in_b[0].astype(BF16)
    wqt, wqm = win_b[:, :B_W].T, win_b[:, B_W:]
    wout_b = w_out_b[0].astype(BF16)
    w1 = w_ff1.astype(BF16)
    w2 = w_ff2.astype(BF16)
    ln_g, ln_b = vec(g_gm_ln[0]), vec(b_gm_ln[0])
    gkv, gpre_b = vec(g_kv), vec(g_mix_pre[1])
    gf_pre, gf_post = stack(g_ffn_pre), stack(g_ffn_post)

    n_sample = n_seq * seq_len
    slab_rows = seq + n_sample
    pre_a, post_a = vec(g_mix_pre[0]), vec(g_mix_post[0])

    mem_kt, mem_vt = _memkv(mem_prompt[0], g_mem, jnp.swapaxes(w_mem_kv, 1, 2).astype(BF16))
    s_p, bs_p = _spatial_tile(w_spatial[0], b_spatial[0], GM_CHUNK)
    s_s, bs_s = _spatial_tile(w_spatial[0], b_spatial[0], seq_len)
    cmkt, cmvt = _positions_last(cache_mem_k), _positions_last(cache_mem_v)
    x, = _mixer_a(x_prompt[0], pre_a, post_a, win_a, ln_g, ln_b, s_p, bs_p, mem_kt, mem_vt, wout_a,
                  rows_per_mem=TILE, emit_v=False, tiles=2, slab_rows=slab_rows, slab_offset=n_sample)
    x, v_rows = _mixer_a(x_sample.reshape(n_sample, D_MODEL), pre_a, post_a, win_a, ln_g, ln_b, s_s, bs_s,
                         cmkt, cmvt, wout_a, rows_per_mem=seq_len, emit_v=True,
                         slab_rows=slab_rows, into=x)
    x = _ffn(x, gf_pre, gf_post, w1, w2, 0)

    qt3, qm, k3, vt3 = _proj_b(x, gkv, gpre_b, (wqt, wqm, wk, wvt), mode="prompt", rows=2 * PROJ_ROWS,
                               row_offset=n_sample, n_rows=seq)
    n_keep = min(BAND_PAST, seq)
    kt_tail, vt_tail = _proj_b(x, gkv, gpre_b, (wkt, wvt), mode="tail", rows=TILE,
                               row_offset=n_sample + seq - n_keep, n_rows=n_keep)
    q3s, qms, kn3, vn3, k_new, v_new = _proj_b(x, gkv, gpre_b, (win_b, wk, wv), mode="sample", rows=512,
                                               n_rows=n_sample, seq_len=seq_len)
    n_k = BAND_PAST + TILE
    bias_p, = _rel_bias_tables(rel_bias[0], TILE, n_k, ((0, TILE),), band=True, keys_on_rows=True)
    bias_c, bias_n = _rel_bias_tables(rel_bias[0], seq_len, past + seq_len,
                                      ((0, past), (past, past + seq_len)), band=False, keys_on_rows=False)
    post_b = vec(g_mix_post[1])
    y = _band_prompt(x, qt3, qm, k3, vt3, bias_p, mem_kt, mem_vt, wout_b, post_b)
    y = _band_sample(x, q3s, qms, kn3, vn3, _positions_last(cache_band_k), _positions_last(cache_band_v),
                     bias_c, bias_n, cmkt, cmvt, wout_b, post_b, into=y)
    y_sample, y_prompt = _ffn(y, gf_pre, gf_post, w1, w2, 1, split=n_sample)
    y_prompt = y_prompt[None]
    y_sample = y_sample.reshape(n_seq, seq_len, D_MODEL)

    return (y_prompt, y_sample,
            _heads_last(mem_kt, (DEPTH, 1)), _heads_last(mem_vt, (DEPTH, 1)),
            _heads_last(kt_tail, (1,)), _heads_last(vt_tail, (1,)),
            v_rows.reshape(1, n_seq, seq_len, GM_W),
            jnp.swapaxes(k_new, 1, 2), jnp.swapaxes(v_new, 1, 2))
```

```python
import functools

import jax
import jax.numpy as jnp
import numpy as np
from jax import lax
from jax.experimental import pallas as pl
from jax.experimental.pallas import tpu as pltpu

D_MODEL = 1024
DEPTH = 2
CHUNK = 64
HEAD_DIM = 64
GM_CHUNK = 128
GM_GROUPS = 4
GM_W = 768
GM_GW = GM_W // GM_GROUPS
MEM_LEN = 256
MEM_HEADS = 4
MEM_W = MEM_HEADS * HEAD_DIM
B_HEADS = 12
B_W = B_HEADS * HEAD_DIM
BAND_PAST = 512
REL_CLIP = 128
D_FF = 4 * D_MODEL
EPS = 1e-6

LANES = 128
HEAD_PAIRS = B_W // LANES
Q_SCALE = HEAD_DIM ** -0.5
NEG = -1e30
TILE = 256
KEY_BLOCKS = BAND_PAST // TILE + 1
KEY_STEP = 128
KEY_BLOCK = 128
PROJ_ROWS = 512
BF16_ROWS = 16
OUT_ROWS = HEAD_DIM + 8
BAND_TILES = BAND_PAST // TILE
GROUP_PAIRS = 2
BIAS_PERIOD = 1024
VMEM_LIMIT = 56 * 1024 * 1024

BF16 = jnp.bfloat16
F32 = jnp.float32


def _dot(a, b):
    return jnp.dot(a, b, preferred_element_type=F32)


def _dot_nt(a, b):
    return lax.dot_general(a, b, (((1,), (1,)), ((), ())), preferred_element_type=F32)


def _rms(x, g):
    ms = jnp.mean(x * x, axis=-1, keepdims=True)
    return x * lax.rsqrt(ms + EPS) * g


def _lsum(a):
    return jnp.sum(a, axis=-1, keepdims=True)


def _lmax(a):
    return jnp.max(a, axis=-1, keepdims=True)


def _const_spec(shape):
    nd = len(shape)
    return pl.BlockSpec(shape, lambda *_: (0,) * nd, pipeline_mode=pl.Buffered(1))


def _layer_spec(shape, layer):
    nd = len(shape)
    return pl.BlockSpec((None,) + shape, lambda *_: (layer,) + (0,) * nd, pipeline_mode=pl.Buffered(1))


def _without_ref(body, k):
    def wrapped(*refs):
        return body(*refs[:k], *refs[k + 1:])
    return wrapped


def _call(body, name, *, in_specs, into=None, semantics="parallel", **kw):
    params = pltpu.CompilerParams(dimension_semantics=(semantics,), vmem_limit_bytes=VMEM_LIMIT)
    if into is None:
        return pl.pallas_call(body, name=name, in_specs=in_specs, compiler_params=params, **kw)
    n_in = len(in_specs)
    call = pl.pallas_call(_without_ref(body, n_in), name=name,
                          in_specs=[*in_specs, pl.BlockSpec(memory_space=pl.ANY)],
                          input_output_aliases={n_in: 0}, compiler_params=params, **kw)
    return lambda *args: call(*args, into)


def _memkv_kernel(mem_ref, g_ref, wt_ref, kt_ref, vt_ref):
    ht = _dot_nt(wt_ref[...], _rms(mem_ref[...], g_ref[...]).astype(BF16))
    kt_ref[0] = ht[:MEM_W]
    vt_ref[0] = ht[MEM_W:]


def _memkv(mem, g_mem, w_mem_kv_t):
    out = jax.ShapeDtypeStruct((DEPTH, 1, MEM_W, MEM_LEN), F32)
    return _call(
        _memkv_kernel, "mem_kv",
        grid=(DEPTH,),
        in_specs=[
            pl.BlockSpec((MEM_LEN, D_MODEL), lambda l: (0, 0)),
            pl.BlockSpec((None, 1, D_MODEL), lambda l: (l, 0, 0)),
            pl.BlockSpec((None, 2 * MEM_W, D_MODEL), lambda l: (l, 0, 0)),
        ],
        out_specs=[pl.BlockSpec((None, 1, MEM_W, MEM_LEN), lambda l: (l, 0, 0, 0))] * 2,
        out_shape=[out, out],
    )(mem, g_mem.reshape(DEPTH, 1, D_MODEL), w_mem_kv_t)


def _mem_attend(qb, kt, vt):
    r = qb.shape[0]
    lane = lax.broadcasted_iota(jnp.int32, (1, MEM_W), 1)
    masks = [(lane >= h * HEAD_DIM) & (lane < (h + 1) * HEAD_DIM) for h in range(MEM_HEADS)]
    qs = jnp.concatenate([jnp.where(m, qb, jnp.zeros_like(qb)) for m in masks], axis=0)
    s = _dot(qs, kt)
    e = jnp.exp(s - _lmax(s))
    pv = _dot_nt(e.astype(BF16), vt) * (1.0 / _lsum(e))
    out = jnp.where(masks[0], pv[:r], 0.0)
    for h in range(1, MEM_HEADS):
        out = out + jnp.where(masks[h], pv[h * r:(h + 1) * r], 0.0)
    return out


def _gelu(x):
    c1 = float(np.sqrt(2.0 / np.pi))
    c2 = c1 * 0.044715
    half = 0.5 * x
    return half + half * jnp.tanh(x * (c1 + c2 * (x * x)))


def _cast_specs(cast, steps, fill):
    own = lambda i: jnp.maximum(i - fill, 0)
    in_specs, out_specs, out_shape = [], [], []
    for a, layer in cast:
        _, r, c = a.shape
        in_specs.append(pl.BlockSpec((None, r // steps, c), functools.partial(lambda i, l: (l, own(i), 0), l=layer)))
        out_specs.append(pl.BlockSpec((r // steps, c), lambda i: (own(i), 0)))
        out_shape.append(jax.ShapeDtypeStruct((r, c), BF16))
    return in_specs, out_specs, out_shape


def _split_cast_refs(refs, n_cast):
    return refs[:n_cast], refs[n_cast:len(refs) - n_cast], refs[len(refs) - n_cast:]


def _cast_blocks(srcs, dsts, stages=4):
    for src, dst in zip(srcs, dsts):
        step = src.shape[0] // stages
        for k in range(stages):
            dst[k * step:(k + 1) * step, :] = src[k * step:(k + 1) * step, :].astype(dst.dtype)
            yield


def _after_fill(o_ref, fill_steps, body):
    if not fill_steps:
        return body()

    @pl.when(pl.program_id(0) < fill_steps)
    def _():
        o_ref[...] = jnp.zeros(o_ref.shape, o_ref.dtype)

    @pl.when(pl.program_id(0) >= fill_steps)
    def _():
        body()


def _round_robin(stage_lists):
    live = list(stage_lists)
    while live:
        for g in list(live):
            if next(g, StopIteration) is StopIteration:
                live.remove(g)
        yield


def _interleave(stage_lists):
    for _ in _round_robin(stage_lists):
        pass


def _mixer_a_kernel(x_ref, gpre_ref, gpost_ref, win_ref, gln_ref, bln_ref, s_ref, bs_ref,
                    mk_ref, mv_ref, wout_ref, *rest, rows_per_mem, tiles, shared_mem, fill_steps, n_cast):
    cast_in, (o_ref, *v_out), cast_out = _split_cast_refs(rest, n_cast)
    nt = GM_W // LANES
    seqs = TILE // rows_per_mem
    lane = lax.broadcasted_iota(jnp.int32, (1, LANES), 1)
    lo = lane < (GM_GW - LANES)
    inv = 1.0 / GM_GW

    def group_stat(a):
        s0 = _lsum(a[0] + jnp.where(lo, a[1], 0.0)) * inv
        s1 = _lsum(jnp.where(lo, 0.0, a[1]) + a[2]) * inv
        s2 = _lsum(a[3] + jnp.where(lo, a[4], 0.0)) * inv
        s3 = _lsum(jnp.where(lo, 0.0, a[4]) + a[5]) * inv
        return [s0, jnp.where(lo, s0, s1), s1, s2, jnp.where(lo, s2, s3), s3]

    def tile(t):
        rows = slice(t * TILE, (t + 1) * TILE)
        z = _dot(_rms(x_ref[rows, :], gpre_ref[...]).astype(BF16), win_ref[...])
        yield
        u = [_gelu(z[:, j * LANES:(j + 1) * LANES]) for j in range(nt)]
        g = [_gelu(z[:, GM_W + j * LANES:GM_W + (j + 1) * LANES]) for j in range(nt)]
        mu = group_stat(g)
        c = [g[j] - mu[j] for j in range(nt)]
        var = group_stat([cj * cj for cj in c])
        gln = gln_ref[...]
        bln = bln_ref[...]
        vn = [c[j] * lax.rsqrt(var[j] + EPS) * gln[:, j * LANES:(j + 1) * LANES]
              + bln[:, j * LANES:(j + 1) * LANES] for j in range(nt)]
        if v_out:
            v_out[0][rows, :] = jnp.concatenate(vn, axis=-1)
        yield
        vb = [a.astype(BF16) for a in vn]
        win = [(0, 1), (1, 2), (3, 4), (4, 5)]
        m = [_dot(s_ref[k], jnp.concatenate([vb[a], vb[b]], axis=-1)) for k, (a, b) in enumerate(win)]
        mixed = [m[0][:, :LANES], jnp.where(lo, m[0][:, LANES:], m[1][:, :LANES]), m[1][:, LANES:],
                 m[2][:, :LANES], jnp.where(lo, m[2][:, LANES:], m[3][:, :LANES]), m[3][:, LANES:]]
        bs = bs_ref[...]
        gm = [u[j] * (mixed[j] + bs[:, j * LANES:(j + 1) * LANES]) for j in range(nt)]
        yield
        qm = (z[:, 2 * GM_W:] * Q_SCALE).astype(BF16)
        mo = []
        for b in range(seqs):
            r = slice(b * rows_per_mem, (b + 1) * rows_per_mem)
            mi = 0 if shared_mem else t * seqs + b
            mo.append(_mem_attend(qm[r], mk_ref[mi].astype(BF16), mv_ref[mi].astype(BF16)))
        mo = mo[0] if len(mo) == 1 else jnp.concatenate(mo, axis=0)
        yield
        cat = jnp.concatenate([a.astype(BF16) for a in gm] + [mo.astype(BF16)], axis=-1)
        o_ref[rows, :] = x_ref[rows, :] + _rms(_dot(cat, wout_ref[...]), gpost_ref[...])

    def body():
        _interleave([tile(t) for t in range(tiles)] + [_cast_blocks(cast_in, cast_out)])

    _after_fill(o_ref, fill_steps, body)


def _mem_spec(mem, seqs, layer):
    if mem.shape[1] == 1:
        return pl.BlockSpec((None, 1, MEM_W, MEM_LEN), lambda i: (layer, 0, 0, 0))
    return pl.BlockSpec((None, seqs, MEM_W, MEM_LEN), lambda i: (layer, i, 0, 0))


def _mixer_a(x, gpre, gpost, win, gln, bln, s_mat, bs, mkt, mvt, wout, *, rows_per_mem, emit_v, tiles=2,
             slab_rows, slab_offset=0, into=None, cast=()):
    n = x.shape[0]
    blk = tiles * TILE
    seqs = blk // rows_per_mem
    off = slab_offset // blk
    fill, shift = (0, off) if into is not None else (off, 0)
    row = lambda w: pl.BlockSpec((blk, w), lambda i: (jnp.maximum(i - fill, 0), 0))
    out_shape = [jax.ShapeDtypeStruct((slab_rows, D_MODEL), F32)]
    out_specs = [pl.BlockSpec((blk, D_MODEL), lambda i: (i + shift, 0))]
    if emit_v:
        out_shape.append(jax.ShapeDtypeStruct((n, GM_W), F32))
        out_specs.append(row(GM_W))
    cast_in_specs, cast_out_specs, cast_out_shape = _cast_specs(cast, n // blk, fill)
    return _call(
        functools.partial(_mixer_a_kernel, rows_per_mem=rows_per_mem, tiles=tiles,
                          shared_mem=mkt.shape[1] == 1, fill_steps=fill, n_cast=len(cast)), "mixer_a", into=into,
        grid=(n // blk + fill,),
        in_specs=[
            row(D_MODEL),
            _const_spec((1, D_MODEL)), _const_spec((1, D_MODEL)),
            _const_spec((D_MODEL, 2 * GM_W + MEM_W)),
            _const_spec((1, GM_W)), _const_spec((1, GM_W)),
            _const_spec((GM_GROUPS, TILE, TILE)), _const_spec((TILE, GM_W)),
            _mem_spec(mkt, seqs, 0), _mem_spec(mvt, seqs, 0),
            _const_spec((GM_W + MEM_W, D_MODEL)),
            *cast_in_specs,
        ],
        out_specs=out_specs + cast_out_specs,
        out_shape=out_shape + cast_out_shape,
    )(x, gpre, gpost, win, gln, bln, s_mat, bs, mkt, mvt, wout, *[a for a, _ in cast])


def _ffn_kernel(x_ref, gpre_ref, gpost_ref, w1_ref, w2_ref, o_ref, *o_tail, rows, tiles, ff_chunk, head_steps):
    def tile(t):
        r = slice(t * rows, (t + 1) * rows)
        x = x_ref[r, :]
        xn = _rms(x, gpre_ref[...]).astype(BF16)
        acc = jnp.zeros(x.shape, F32)
        yield
        for c in range(D_FF // ff_chunk):
            h = _dot(xn, w1_ref[:, c * ff_chunk:(c + 1) * ff_chunk])
            h = jnp.square(jnp.maximum(h, 0.0)).astype(BF16)
            acc = acc + _dot(h, w2_ref[c * ff_chunk:(c + 1) * ff_chunk, :])
            yield
        y = x + _rms(acc, gpost_ref[...])
        if not o_tail:
            o_ref[r, :] = y
        else:
            is_head = pl.program_id(0) < head_steps
            o_ref[r, :] = jnp.where(is_head, y, o_ref[r, :])
            o_tail[0][r, :] = y

    if o_tail:
        @pl.when(pl.program_id(0) < head_steps)
        def _():
            o_ref[...] = jnp.zeros(o_ref.shape, F32)

    _interleave([tile(t) for t in range(tiles)])


def _ffn(x, gpre, gpost, w1, w2, layer, *, rows=512, tiles=2, ff_chunk=1024, split=None):
    n = x.shape[0]
    blk = tiles * rows
    row = pl.BlockSpec((blk, D_MODEL), lambda i: (i, 0))
    if split is None:
        head_steps, out_specs, out_shape = None, row, jax.ShapeDtypeStruct((n, D_MODEL), F32)
    else:
        head_steps = split // blk
        out_specs = [pl.BlockSpec((blk, D_MODEL), lambda i: (jnp.minimum(i, head_steps - 1), 0)),
                     pl.BlockSpec((blk, D_MODEL), lambda i: (jnp.maximum(i - head_steps, 0), 0))]
        out_shape = [jax.ShapeDtypeStruct((split, D_MODEL), F32), jax.ShapeDtypeStruct((n - split, D_MODEL), F32)]
    return _call(
        functools.partial(_ffn_kernel, rows=rows, tiles=tiles, ff_chunk=ff_chunk, head_steps=head_steps),
        "ffn", semantics="arbitrary",
        grid=(n // blk,),
        in_specs=[row, _layer_spec((1, D_MODEL), layer), _layer_spec((1, D_MODEL), layer),
                  _const_spec((D_MODEL, D_FF)), _const_spec((D_FF, D_MODEL))],
        out_specs=out_specs,
        out_shape=out_shape,
    )(x, gpre, gpost, w1, w2)


def _proj_b_kernel(x_ref, gkv_ref, gpre_ref, *refs, mode, seq_len, n_weights, n_cast):
    rows = x_ref.shape[0]
    cast_in, outs, cast_out = _split_cast_refs(refs[n_weights:], n_cast)
    refs = (*refs[:n_weights], *outs)

    def normed(r):
        x = x_ref[r, :]
        xh = x * lax.rsqrt(jnp.mean(x * x, axis=-1, keepdims=True) + EPS)
        return (xh * gkv_ref[...]).astype(BF16), (xh * gpre_ref[...]).astype(BF16)

    if mode == "tail":
        wkt_ref, wvt_ref, kt_ref, vt_ref = refs
        xkv, _ = normed(slice(None))
        kt_ref[...] = _dot_nt(wkt_ref[...], xkv)
        vt_ref[...] = _dot_nt(wvt_ref[...], xkv)
        return
    if mode == "prompt":
        wqt_ref, wqm_ref, wk_ref, wvt_ref, qt_ref, qm_ref, k_ref, vt_ref = refs

        def tile(t):
            r = slice(t * PROJ_ROWS, (t + 1) * PROJ_ROWS)
            xkv, xq = normed(r)
            yield
            qt = _dot_nt(wqt_ref[...], xq) * Q_SCALE
            for p in range(HEAD_PAIRS):
                qt_ref[p, :, r] = qt[p * LANES:(p + 1) * LANES, :].astype(BF16)
            yield
            vt = _dot_nt(wvt_ref[...], xkv)
            for p in range(HEAD_PAIRS):
                vt_ref[p, :, r] = vt[p * LANES:(p + 1) * LANES, :].astype(BF16)
            yield
            k = _dot(xkv, wk_ref[...])
            for p in range(HEAD_PAIRS):
                k_ref[p, r, :] = k[:, p * LANES:(p + 1) * LANES].astype(BF16)
            yield
            qm_ref[r, :] = (_dot(xq, wqm_ref[...]) * Q_SCALE).astype(BF16)

        _interleave([tile(t) for t in range(rows // PROJ_ROWS)] + [_cast_blocks(cast_in, cast_out)])
        return
    win_ref, wk_ref, wv_ref, q_ref, qm_ref, k_ref, v_ref, k4_ref, v4_ref = refs
    xkv, xq = normed(slice(None))
    z = _dot(xq, win_ref[...]) * Q_SCALE
    qm_ref[...] = z[:, B_W:].astype(BF16)
    k = _dot(xkv, wk_ref[...])
    v = _dot(xkv, wv_ref[...])
    for p in range(HEAD_PAIRS):
        cols = slice(p * LANES, (p + 1) * LANES)
        q_ref[p] = z[:, cols].astype(BF16)
        k_ref[p] = k[:, cols].astype(BF16)
        v_ref[p] = v[:, cols].astype(BF16)
    for b in range(rows // seq_len):
        for h in range(B_HEADS):
            r, c = slice(b * seq_len, (b + 1) * seq_len), slice(h * HEAD_DIM, (h + 1) * HEAD_DIM)
            k4_ref[b, h] = k[r, c]
            v4_ref[b, h] = v[r, c]


def _proj_b(x, gkv, gpre, weights, *, mode, rows, row_offset=0, n_rows=None, seq_len=None, cast=()):
    n = x.shape[0] if n_rows is None else n_rows
    off = row_offset // rows
    pair_rows = jax.ShapeDtypeStruct((HEAD_PAIRS, n, LANES), BF16)
    pair_rows_spec = pl.BlockSpec((HEAD_PAIRS, rows, LANES), lambda i: (0, i, 0))
    pair_cols = jax.ShapeDtypeStruct((HEAD_PAIRS, LANES, n), BF16)
    pair_cols_spec = pl.BlockSpec((HEAD_PAIRS, LANES, rows), lambda i: (0, 0, i))
    qm = jax.ShapeDtypeStruct((n, MEM_W), BF16)
    qm_spec = pl.BlockSpec((rows, MEM_W), lambda i: (i, 0))
    if mode == "tail":
        out_shape = [jax.ShapeDtypeStruct((B_W, n), F32)] * 2
        out_specs = [pl.BlockSpec((B_W, rows), lambda i: (0, i))] * 2
    elif mode == "prompt":
        out_shape = [pair_cols, qm, pair_rows, pair_cols]
        out_specs = [pair_cols_spec, qm_spec, pair_rows_spec, pair_cols_spec]
    else:
        seqs = rows // seq_len
        per_head = jax.ShapeDtypeStruct((n // seq_len, B_HEADS, seq_len, HEAD_DIM), F32)
        per_head_spec = pl.BlockSpec((seqs, B_HEADS, seq_len, HEAD_DIM), lambda i: (i, 0, 0, 0))
        out_shape = [pair_rows, qm, pair_rows, pair_rows, per_head, per_head]
        out_specs = [pair_rows_spec, qm_spec, pair_rows_spec, pair_rows_spec, per_head_spec, per_head_spec]
    cast_in_specs, cast_out_specs, cast_out_shape = _cast_specs(cast, n // rows, 0)
    return _call(
        functools.partial(_proj_b_kernel, mode=mode, seq_len=seq_len, n_weights=len(weights),
                          n_cast=len(cast)), "proj_b_" + mode,
        grid=(n // rows,),
        in_specs=[pl.BlockSpec((rows, D_MODEL), lambda i: (i + off, 0)),
                  _const_spec((1, D_MODEL)), _const_spec((1, D_MODEL))]
                 + [_const_spec(w.shape) for w in weights] + cast_in_specs,
        out_specs=out_specs + cast_out_specs,
        out_shape=out_shape + cast_out_shape,
    )(x, gkv, gpre, *weights, *[a for a, _ in cast])


def _bias_kernel(g_ref, *o_refs, n_q, n_k, splits, band, keys_on_rows):
    n_rows, shift = (n_k, n_q) if keys_on_rows else (n_q, BIAS_PERIOD - (n_q - 1))
    for hh in range(2):
        x = jnp.broadcast_to(g_ref[hh:hh + 1, :], (n_rows, BIAS_PERIOD))
        t = pltpu.roll(x, shift, 1, stride=1, stride_axis=0)
        if band:
            a = lax.broadcasted_iota(jnp.int32, (n_rows, BIAS_PERIOD), 0)
            b = lax.broadcasted_iota(jnp.int32, (n_rows, BIAS_PERIOD), 1)
            r, w = (b, a) if keys_on_rows else (a, b)
            j = w - (r - (r & (CHUNK - 1)))
            t = jnp.where((j >= 0) & (j < BAND_PAST + CHUNK), t, NEG)
        for o_ref, (lo, hi) in zip(o_refs, splits):
            o_ref[hh * n_rows:(hh + 1) * n_rows, :] = t[:, lo:hi]


def _rel_bias_tables(rel_bias, n_q, n_k, splits, *, band, keys_on_rows):
    rel_bias = rel_bias - rel_bias[:, -1:]
    c0 = n_k - 1
    far = jnp.broadcast_to(rel_bias[:, -1:], (B_HEADS, c0 - REL_CLIP))
    near = jnp.broadcast_to(rel_bias[:, :1], (B_HEADS, BIAS_PERIOD - (c0 - REL_CLIP) - (2 * REL_CLIP + 1)))
    gen = [near, rel_bias, far] if keys_on_rows else [far, rel_bias[:, ::-1], near]
    gen = jnp.concatenate(gen, axis=1).reshape(HEAD_PAIRS, 2, BIAS_PERIOD)
    n_rows = n_k if keys_on_rows else n_q
    return _call(
        functools.partial(_bias_kernel, n_q=n_q, n_k=n_k, splits=splits, band=band,
                          keys_on_rows=keys_on_rows), "rel_bias",
        grid=(HEAD_PAIRS,),
        in_specs=[pl.BlockSpec((None, 2, BIAS_PERIOD), lambda p: (p, 0, 0))],
        out_specs=[pl.BlockSpec((None, 2 * n_rows, hi - lo), lambda p: (p, 0, 0)) for lo, hi in splits],
        out_shape=[jax.ShapeDtypeStruct((HEAD_PAIRS, 2 * n_rows, hi - lo), F32) for lo, hi in splits],
    )(gen)


def _band_block_kinds():
    n_k = KEY_BLOCKS * TILE
    r = np.arange(TILE)[None, :]
    w = np.arange(n_k)[:, None]
    j = w - CHUNK * (r // CHUNK)
    ok = (j >= 0) & (j < BAND_PAST + CHUNK)
    plain = ok & (BAND_PAST + r - w >= REL_CLIP)
    kinds = []
    for a in range(n_k // KEY_BLOCK):
        rows = slice(a * KEY_BLOCK, (a + 1) * KEY_BLOCK)
        blocks = [(rows, slice(b * LANES, (b + 1) * LANES)) for b in range(TILE // LANES)]
        kinds.append(["skip" if not ok[blk].any() else "plain" if plain[blk].all() else "biased"
                      for blk in blocks])
    return kinds


def _band_prompt_kernel(x_ref, qt_ref, qm_ref, kp_ref, kc_ref, vtp_ref, vtc_ref,
                        bias_ref, mk_ref, mv_ref, wout_ref, gpost_ref, o_ref, *, fill_steps):
    i = pl.program_id(0) - fill_steps
    n_k = KEY_BLOCKS * TILE
    kinds = _band_block_kinds()
    w = lax.broadcasted_iota(jnp.int32, (n_k, LANES), 0)
    c = lax.broadcasted_iota(jnp.int32, (n_k, LANES), 1)
    ones_rows = jnp.where(lax.broadcasted_iota(jnp.int32, (BF16_ROWS, n_k), 0) == 0, 1.0, 0.0).astype(BF16)
    row = lax.broadcasted_iota(jnp.int32, (LANES, TILE), 0)
    one_hot_row = jnp.where(row == 0, 1.0, 0.0).astype(BF16)
    lo = row < HEAD_DIM

    def key_step(a, s, vth, bias_rows, state):
        blocks = range(a * KEY_STEP // KEY_BLOCK, (a + 1) * KEY_STEP // KEY_BLOCK)
        es, alphas, active = [], [], []
        for b in range(TILE // LANES):
            cols = slice(b * LANES, (b + 1) * LANES)
            sjs = {}
            for blk in blocks:
                if kinds[blk][b] != "skip":
                    rows = slice(blk * KEY_BLOCK, (blk + 1) * KEY_BLOCK)
                    sjs[blk] = s[rows, cols] + bias_rows(rows, cols) if kinds[blk][b] == "biased" else s[rows, cols]
            active.append(bool(sjs))
            alphas.append(None)
            if not sjs:
                es.append(jnp.zeros((KEY_STEP, LANES), F32))
                continue
            mj = jnp.max(functools.reduce(jnp.maximum, sjs.values()), axis=0, keepdims=True)
            if state[b] is None:
                state[b] = [mj, None]
            else:
                m_new = jnp.maximum(state[b][0], mj)
                alphas[b] = jnp.exp(state[b][0] - m_new)
                state[b][0] = m_new
            es.append(jnp.concatenate(
                [jnp.exp(sjs[blk] - state[b][0]) if blk in sjs else jnp.zeros((KEY_BLOCK, LANES), F32)
                 for blk in blocks], axis=0))
        rows = slice(a * KEY_STEP, (a + 1) * KEY_STEP)
        pv = _dot(vth[:, rows], jnp.concatenate(es, axis=1).astype(BF16))[:OUT_ROWS]
        for b in range(TILE // LANES):
            if active[b]:
                pv_b = pv[:, b * LANES:(b + 1) * LANES]
                state[b][1] = pv_b if alphas[b] is None else state[b][1] * alphas[b] + pv_b

    def tile(t):
        q_rows = slice(t * TILE, (t + 1) * TILE)
        k_rows = slice(t * TILE, t * TILE + n_k)
        first_key_tile = i * BAND_TILES + t - (KEY_BLOCKS - 1)
        pen = jnp.where((w < -first_key_tile * TILE) & (c == 0), NEG, 0.0).astype(BF16)
        outs = []
        for p0 in range(0, HEAD_PAIRS, GROUP_PAIRS):
            heads = []
            for p in range(p0, p0 + GROUP_PAIRS):
                k_win = jnp.concatenate([kp_ref[p], kc_ref[p]], axis=0)[k_rows]
                k_ext = jnp.concatenate([k_win, pen], axis=1)
                vt = jnp.concatenate([vtp_ref[p], vtc_ref[p]], axis=1)[:, k_rows]
                qt = qt_ref[p, :, q_rows]
                zero = jnp.zeros_like(qt)
                for hh in range(2):
                    qh = jnp.where(lo, qt, zero) if hh == 0 else jnp.where(lo, zero, qt)
                    s = _dot(k_ext, jnp.concatenate([qh, one_hot_row], axis=0))
                    vth = jnp.concatenate([vt[hh * HEAD_DIM:(hh + 1) * HEAD_DIM, :], ones_rows], axis=0)
                    bias_rows = functools.partial(
                        lambda rows, cols, p, base: bias_ref[p, base + rows.start:base + rows.stop, cols],
                        p=p, base=hh * n_k)
                    heads.append((s, vth, bias_rows, [None] * (TILE // LANES)))
            for a in range(n_k // KEY_STEP):
                for head in heads:
                    key_step(a, *head)
            outs += [jnp.concatenate([o[:HEAD_DIM] * (1.0 / o[HEAD_DIM:HEAD_DIM + 1]) for _, o in state], axis=1)
                     for *_, state in heads]
            yield
        band = jnp.concatenate(outs, axis=0).T.astype(BF16)
        mo = _mem_attend(qm_ref[q_rows, :], mk_ref[0].astype(BF16), mv_ref[0].astype(BF16))
        cat = jnp.concatenate([band, mo.astype(BF16)], axis=-1)
        o_ref[q_rows, :] = x_ref[q_rows, :] + _rms(_dot(cat, wout_ref[...]), gpost_ref[...])

    _after_fill(o_ref, fill_steps, lambda: _interleave([tile(t) for t in range(BAND_TILES)]))


def _band_prompt(x, qt3, qm, k3, vt3, bias, mkt, mvt, wout, gpost):
    n = qm.shape[0]
    rows = BAND_TILES * TILE
    assert rows == BAND_PAST and n % rows == 0 and (x.shape[0] - n) % rows == 0
    fill = (x.shape[0] - n) // rows
    slab_spec = pl.BlockSpec((rows, D_MODEL), lambda i: (i, 0))
    own = lambda i: jnp.maximum(i - fill, 0)
    past = lambda i: jnp.maximum(i - fill - 1, 0)
    return _call(
        functools.partial(_band_prompt_kernel, fill_steps=fill), "band_prompt",
        grid=(n // rows + fill,),
        in_specs=[slab_spec,
                  pl.BlockSpec((HEAD_PAIRS, LANES, rows), lambda i: (0, 0, own(i))),
                  pl.BlockSpec((rows, MEM_W), lambda i: (own(i), 0)),
                  pl.BlockSpec((HEAD_PAIRS, rows, LANES), lambda i: (0, past(i), 0)),
                  pl.BlockSpec((HEAD_PAIRS, rows, LANES), lambda i: (0, own(i), 0)),
                  pl.BlockSpec((HEAD_PAIRS, LANES, rows), lambda i: (0, 0, past(i))),
                  pl.BlockSpec((HEAD_PAIRS, LANES, rows), lambda i: (0, 0, own(i))),
                  _const_spec((HEAD_PAIRS, 2 * KEY_BLOCKS * TILE, TILE)),
                  _mem_spec(mkt, 1, 1), _mem_spec(mvt, 1, 1),
                  _const_spec((B_W + MEM_W, D_MODEL)), _const_spec((1, D_MODEL))],
        out_specs=slab_spec,
        out_shape=jax.ShapeDtypeStruct(x.shape, F32),
    )(x, qt3, qm, k3, k3, vt3, vt3, bias, mkt, mvt, wout, gpost)


def _band_sample_kernel(x_ref, q_ref, qm_ref, kn_ref, vn_ref, ck_ref, cv_ref, bc_ref, bn_ref,
                        mk_ref, mv_ref, wout_ref, gpost_ref, o_ref, *, seqs, seq_len):
    lane = lax.broadcasted_iota(jnp.int32, (1, LANES), 1)
    lo = lane < HEAD_DIM

    def pair_attend(b, p, out):
        rows = slice(b * seq_len, (b + 1) * seq_len)
        hd = slice(p * LANES, (p + 1) * LANES)
        qp = q_ref[p, rows, :]
        zero = jnp.zeros_like(qp)
        qs = jnp.concatenate([jnp.where(lo, qp, zero), jnp.where(lo, zero, qp)], axis=0)
        sc = _dot(qs, ck_ref[b, hd, :].astype(BF16)) + bc_ref[p]
        sn = _dot_nt(qs, kn_ref[p, rows, :]) + bn_ref[p]
        yield
        m = jnp.maximum(_lmax(sc), _lmax(sn))
        ec = jnp.exp(sc - m)
        en = jnp.exp(sn - m)
        l = _lsum(ec) + _lsum(en)
        yield
        o = (_dot_nt(ec.astype(BF16), cv_ref[b, hd, :].astype(BF16))
             + _dot(en.astype(BF16), vn_ref[p, rows, :]))
        yield
        o = o * (1.0 / l)
        out[p] = jnp.where(lo, o[:seq_len], o[seq_len:]).astype(BF16)

    def seq_attend(b, out):
        band = [None] * HEAD_PAIRS
        yield from _round_robin([pair_attend(b, p, band) for p in range(HEAD_PAIRS)])
        rows = slice(b * seq_len, (b + 1) * seq_len)
        mo = _mem_attend(qm_ref[rows, :], mk_ref[b].astype(BF16), mv_ref[b].astype(BF16))
        out[b] = jnp.concatenate(band + [mo.astype(BF16)], axis=-1)

    rows_out = [None] * seqs
    _interleave([seq_attend(b, rows_out) for b in range(seqs)])
    cat = jnp.concatenate(rows_out, axis=0)
    o_ref[...] = x_ref[...] + _rms(_dot(cat, wout_ref[...]), gpost_ref[...])


def _band_sample(x, q3, qm, kn3, vn3, ckt, cvt, bias_c, bias_n, mkt, mvt, wout, gpost, *, into, seqs=4):
    n = qm.shape[0]
    n_seq, past = ckt.shape[0], ckt.shape[2]
    seq_len = n // n_seq
    rows = seqs * seq_len
    slab_spec = pl.BlockSpec((rows, D_MODEL), lambda i: (i, 0))
    pair_rows_spec = pl.BlockSpec((HEAD_PAIRS, rows, LANES), lambda i: (0, i, 0))
    cache_spec = pl.BlockSpec((seqs, B_W, past), lambda i: (i, 0, 0))
    return _call(
        functools.partial(_band_sample_kernel, seqs=seqs, seq_len=seq_len), "band_sample", into=into,
        grid=(n_seq // seqs,),
        in_specs=[slab_spec,
                  pair_rows_spec,
                  pl.BlockSpec((rows, MEM_W), lambda i: (i, 0)),
                  pair_rows_spec, pair_rows_spec,
                  cache_spec, cache_spec,
                  _const_spec((HEAD_PAIRS, 2 * seq_len, past)),
                  _const_spec((HEAD_PAIRS, 2 * seq_len, seq_len)),
                  _mem_spec(mkt, seqs, 1), _mem_spec(mvt, seqs, 1),
                  _const_spec((B_W + MEM_W, D_MODEL)), _const_spec((1, D_MODEL))],
        out_specs=slab_spec,
        out_shape=jax.ShapeDtypeStruct(x.shape, F32),
    )(x, q3, qm, kn3, vn3, ckt, cvt, bias_c, bias_n, mkt, mvt, wout, gpost)


def _spatial_tile(w_s, b_s, period):
    tril = jnp.tril(jnp.ones((GM_CHUNK, GM_CHUNK), dtype=bool))
    w = jnp.where(tril, w_s, jnp.zeros((), w_s.dtype))[:, :period, :period]
    eye = jnp.eye(TILE // period, dtype=w.dtype)
    s_mat = jnp.einsum("ab,gts->gatbs", eye, w).reshape(GM_GROUPS, TILE, TILE)
    rows = jnp.tile(b_s[:, :period], (1, TILE // period))
    bs = jnp.repeat(rows.T, GM_GW, axis=1)
    return s_mat.astype(BF16), bs


def _heads_last(t, lead):
    pos = t.shape[-1]
    t = t.reshape(lead + (-1, HEAD_DIM, pos))
    nd = len(lead)
    return jnp.transpose(t, tuple(range(nd)) + (nd + 2, nd, nd + 1))


def _positions_last(c):
    nd = c.ndim
    t = jnp.transpose(c, tuple(range(nd - 3)) + (nd - 2, nd - 1, nd - 3))
    return t.reshape(c.shape[:-3] + (c.shape[-2] * c.shape[-1], c.shape[-3]))


def kernel(x_prompt, x_sample, cache_mem_k, cache_mem_v, cache_band_k, cache_band_v, mem_prompt,
           g_mix_pre, g_mix_post, g_ffn_pre, g_ffn_post, g_mem, w_mem_kv,
           w_in_a, g_gm_ln, b_gm_ln, w_spatial, b_spatial, w_out_a,
           g_kv, w_kv, w_in_b, rel_bias, w_out_b, w_ff1, w_ff2):
    seq = x_prompt.shape[1]
    n_seq, seq_len = x_sample.shape[0], x_sample.shape[1]
    past = cache_band_k.shape[1]
    vec = lambda a: a.reshape(1, -1)
    stack = lambda a: a.reshape(DEPTH, 1, -1)

    win_a = w_in_a[0].astype(BF16)
    wout_a = w_out_a[0].astype(BF16)
    wk = w_kv[:, :B_W].astype(BF16)
    wv = w_kv[:, B_W:].astype(BF16)
    wkt, wvt = wk.T, wv.T
    win_b = w_in_b[0].astype(BF16)
    wqt, wqm = win_b[:, :B_W].T, win_b[:, B_W:]
    wout_b = w_out_b[0].astype(BF16)
    ln_g, ln_b = vec(g_gm_ln[0]), vec(b_gm_ln[0])
    gkv, gpre_b = vec(g_kv), vec(g_mix_pre[1])
    gf_pre, gf_post = stack(g_ffn_pre), stack(g_ffn_post)

    n_sample = n_seq * seq_len
    slab_rows = seq + n_sample
    pre_a, post_a = vec(g_mix_pre[0]), vec(g_mix_post[0])

    mem_kt, mem_vt = _memkv(mem_prompt[0], g_mem, jnp.swapaxes(w_mem_kv, 1, 2).astype(BF16))
    s_p, bs_p = _spatial_tile(w_spatial[0], b_spatial[0], GM_CHUNK)
    s_s, bs_s = _spatial_tile(w_spatial[0], b_spatial[0], seq_len)
    cmkt, cmvt = _positions_last(cache_mem_k), _positions_last(cache_mem_v)
    x, w1, w2 = _mixer_a(x_prompt[0], pre_a, post_a, win_a, ln_g, ln_b, s_p, bs_p, mem_kt, mem_vt, wout_a,
                         rows_per_mem=TILE, emit_v=False, tiles=4, slab_rows=slab_rows, slab_offset=n_sample,
                         cast=((w_ff1, 0), (w_ff2, 0)))
    x, v_rows = _mixer_a(x_sample.reshape(n_sample, D_MODEL), pre_a, post_a, win_a, ln_g, ln_b, s_s, bs_s,
                         cmkt, cmvt, wout_a, rows_per_mem=seq_len, emit_v=True,
                         slab_rows=slab_rows, into=x)
    x = _ffn(x, gf_pre, gf_post, w1, w2, 0)

    qt3, qm, k3, vt3, w1, w2 = _proj_b(x, gkv, gpre_b, (wqt, wqm, wk, wvt), mode="prompt", rows=2 * PROJ_ROWS,
                                       row_offset=n_sample, n_rows=seq, cast=((w_ff1, 1), (w_ff2, 1)))
    n_keep = min(BAND_PAST, seq)
    kt_tail, vt_tail = _proj_b(x, gkv, gpre_b, (wkt, wvt), mode="tail", rows=TILE,
                               row_offset=n_sample + seq - n_keep, n_rows=n_keep)
    q3s, qms, kn3, vn3, k_new, v_new = _proj_b(x, gkv, gpre_b, (win_b, wk, wv), mode="sample", rows=512,
                                               n_rows=n_sample, seq_len=seq_len)
    n_k = BAND_PAST + TILE
    bias_p, = _rel_bias_tables(rel_bias[0], TILE, n_k, ((0, TILE),), band=True, keys_on_rows=True)
    bias_c, bias_n = _rel_bias_tables(rel_bias[0], seq_len, past + seq_len,
                                      ((0, past), (past, past + seq_len)), band=False, keys_on_rows=False)
    post_b = vec(g_mix_post[1])
    y = _band_prompt(x, qt3, qm, k3, vt3, bias_p, mem_kt, mem_vt, wout_b, post_b)
    y = _band_sample(x, q3s, qms, kn3, vn3, _positions_last(cache_band_k), _positions_last(cache_band_v),
                     bias_c, bias_n, cmkt, cmvt, wout_b, post_b, into=y)
    y_sample, y_prompt = _ffn(y, gf_pre, gf_post, w1, w2, 1, split=n_sample)
    y_prompt = y_prompt[None]
    y_sample = y_sample.reshape(n_seq, seq_len, D_MODEL)

    return (y_prompt, y_sample,
            _heads_last(mem_kt, (DEPTH, 1)), _heads_last(mem_vt, (DEPTH, 1)),
            _heads_last(kt_tail, (1,)), _heads_last(vt_tail, (1,)),
            v_rows.reshape(1, n_seq, seq_len, GM_W),
            jnp.swapaxes(k_new, 1, 2), jnp.swapaxes(v_new, 1, 2))
```

```python
import functools

import jax
import jax.numpy as jnp
import numpy as np
from jax import lax
from jax.experimental import pallas as pl
from jax.experimental.pallas import tpu as pltpu

D_MODEL = 1024
DEPTH = 2
CHUNK = 64
HEAD_DIM = 64
GM_CHUNK = 128
GM_GROUPS = 4
GM_W = 768
GM_GW = GM_W // GM_GROUPS
MEM_LEN = 256
MEM_HEADS = 4
MEM_W = MEM_HEADS * HEAD_DIM
B_HEADS = 12
B_W = B_HEADS * HEAD_DIM
BAND_PAST = 512
REL_CLIP = 128
D_FF = 4 * D_MODEL
EPS = 1e-6

LANES = 128
SUBLANES = 8
HEAD_PAIRS = B_W // LANES
Q_SCALE = HEAD_DIM ** -0.5
NEG = -1e30
TILE = 256
KEY_BLOCKS = BAND_PAST // TILE + 1
KEY_STEP = 128
KEY_BLOCK = 128
PROJ_ROWS = 512
BF16_ROWS = 2 * SUBLANES
OUT_ROWS = HEAD_DIM + SUBLANES
BAND_TILES = BAND_PAST // TILE
GROUP_PAIRS = 2
BIAS_PERIOD = 1024
V7X_VMEM_BYTES = 64 * 1024 * 1024
VMEM_LIMIT = V7X_VMEM_BYTES * 7 // 8

BF16 = jnp.bfloat16
F32 = jnp.float32


def _dot(a, b):
    return jnp.dot(a, b, preferred_element_type=F32)


def _dot_nt(a, b):
    return lax.dot_general(a, b, (((1,), (1,)), ((), ())), preferred_element_type=F32)


def _rms(x, g):
    ms = jnp.mean(x * x, axis=-1, keepdims=True)
    return x * lax.rsqrt(ms + EPS) * g


def _lsum(a):
    return jnp.sum(a, axis=-1, keepdims=True)


def _lmax(a):
    return jnp.max(a, axis=-1, keepdims=True)


def _const_spec(shape):
    nd = len(shape)
    return pl.BlockSpec(shape, lambda *_: (0,) * nd, pipeline_mode=pl.Buffered(1))


def _layer_spec(shape, layer):
    nd = len(shape)
    return pl.BlockSpec((None,) + shape, lambda *_: (layer,) + (0,) * nd, pipeline_mode=pl.Buffered(1))


def _without_ref(body, k):
    def wrapped(*refs):
        return body(*refs[:k], *refs[k + 1:])
    return wrapped


def _call(body, name, *, in_specs, into=None, semantics="parallel", **kw):
    params = pltpu.CompilerParams(dimension_semantics=(semantics,), vmem_limit_bytes=VMEM_LIMIT)
    if into is None:
        return pl.pallas_call(body, name=name, in_specs=in_specs, compiler_params=params, **kw)
    n_in = len(in_specs)
    call = pl.pallas_call(_without_ref(body, n_in), name=name,
                          in_specs=[*in_specs, pl.BlockSpec(memory_space=pl.ANY)],
                          input_output_aliases={n_in: 0}, compiler_params=params, **kw)
    return lambda *args: call(*args, into)


def _memkv_kernel(mem_ref, g_ref, wt_ref, kt_ref, vt_ref):
    ht = _dot_nt(wt_ref[...], _rms(mem_ref[...], g_ref[...]).astype(BF16))
    kt_ref[0] = ht[:MEM_W]
    vt_ref[0] = ht[MEM_W:]


def _memkv(mem, g_mem, w_mem_kv_t):
    out = jax.ShapeDtypeStruct((DEPTH, 1, MEM_W, MEM_LEN), F32)
    return _call(
        _memkv_kernel, "mem_kv",
        grid=(DEPTH,),
        in_specs=[
            pl.BlockSpec((MEM_LEN, D_MODEL), lambda l: (0, 0)),
            pl.BlockSpec((None, 1, D_MODEL), lambda l: (l, 0, 0)),
            pl.BlockSpec((None, 2 * MEM_W, D_MODEL), lambda l: (l, 0, 0)),
        ],
        out_specs=[pl.BlockSpec((None, 1, MEM_W, MEM_LEN), lambda l: (l, 0, 0, 0))] * 2,
        out_shape=[out, out],
    )(mem, g_mem.reshape(DEPTH, 1, D_MODEL), w_mem_kv_t)


def _mem_attend(qb, kt, vt):
    r = qb.shape[0]
    lane = lax.broadcasted_iota(jnp.int32, (1, MEM_W), 1)
    masks = [(lane >= h * HEAD_DIM) & (lane < (h + 1) * HEAD_DIM) for h in range(MEM_HEADS)]
    qs = jnp.concatenate([jnp.where(m, qb, jnp.zeros_like(qb)) for m in masks], axis=0)
    s = _dot(qs, kt)
    e = jnp.exp(s - _lmax(s))
    pv = _dot_nt(e.astype(BF16), vt) * (1.0 / _lsum(e))
    out = jnp.where(masks[0], pv[:r], 0.0)
    for h in range(1, MEM_HEADS):
        out = out + jnp.where(masks[h], pv[h * r:(h + 1) * r], 0.0)
    return out


def _gelu(x):
    c1 = float(np.sqrt(2.0 / np.pi))
    c2 = c1 * 0.044715
    half = 0.5 * x
    return half + half * jnp.tanh(x * (c1 + c2 * (x * x)))


def _cast_specs(cast, steps, fill):
    own = lambda i: jnp.maximum(i - fill, 0)
    in_specs, out_specs, out_shape = [], [], []
    for a, layer in cast:
        _, r, c = a.shape
        in_specs.append(pl.BlockSpec((None, r // steps, c), functools.partial(lambda i, l: (l, own(i), 0), l=layer)))
        out_specs.append(pl.BlockSpec((r // steps, c), lambda i: (own(i), 0)))
        out_shape.append(jax.ShapeDtypeStruct((r, c), BF16))
    return in_specs, out_specs, out_shape


def _split_cast_refs(refs, n_cast):
    return refs[:n_cast], refs[n_cast:len(refs) - n_cast], refs[len(refs) - n_cast:]


def _cast_blocks(srcs, dsts, stages=4):
    for src, dst in zip(srcs, dsts):
        step = src.shape[0] // stages
        for k in range(stages):
            dst[k * step:(k + 1) * step, :] = src[k * step:(k + 1) * step, :].astype(dst.dtype)
            yield


def _after_fill(o_ref, fill_steps, body):
    if not fill_steps:
        return body()

    @pl.when(pl.program_id(0) < fill_steps)
    def _():
        o_ref[...] = jnp.zeros(o_ref.shape, o_ref.dtype)

    @pl.when(pl.program_id(0) >= fill_steps)
    def _():
        body()


def _round_robin(stage_lists):
    live = list(stage_lists)
    while live:
        for g in list(live):
            if next(g, StopIteration) is StopIteration:
                live.remove(g)
        yield


def _interleave(stage_lists):
    for _ in _round_robin(stage_lists):
        pass


def _mixer_a_kernel(x_ref, gpre_ref, gpost_ref, win_ref, gln_ref, bln_ref, s_ref, bs_ref,
                    mk_ref, mv_ref, wout_ref, *rest, rows_per_mem, tiles, shared_mem, fill_steps, n_cast):
    cast_in, (o_ref, *v_out), cast_out = _split_cast_refs(rest, n_cast)
    nt = GM_W // LANES
    seqs = TILE // rows_per_mem
    lane = lax.broadcasted_iota(jnp.int32, (1, LANES), 1)
    lo = lane < (GM_GW - LANES)
    inv = 1.0 / GM_GW

    def group_stat(a):
        s0 = _lsum(a[0] + jnp.where(lo, a[1], 0.0)) * inv
        s1 = _lsum(jnp.where(lo, 0.0, a[1]) + a[2]) * inv
        s2 = _lsum(a[3] + jnp.where(lo, a[4], 0.0)) * inv
        s3 = _lsum(jnp.where(lo, 0.0, a[4]) + a[5]) * inv
        return [s0, jnp.where(lo, s0, s1), s1, s2, jnp.where(lo, s2, s3), s3]

    def tile(t):
        rows = slice(t * TILE, (t + 1) * TILE)
        z = _dot(_rms(x_ref[rows, :], gpre_ref[...]).astype(BF16), win_ref[...])
        yield
        u = [_gelu(z[:, j * LANES:(j + 1) * LANES]) for j in range(nt)]
        g = [_gelu(z[:, GM_W + j * LANES:GM_W + (j + 1) * LANES]) for j in range(nt)]
        mu = group_stat(g)
        c = [g[j] - mu[j] for j in range(nt)]
        var = group_stat([cj * cj for cj in c])
        gln = gln_ref[...]
        bln = bln_ref[...]
        vn = [c[j] * lax.rsqrt(var[j] + EPS) * gln[:, j * LANES:(j + 1) * LANES]
              + bln[:, j * LANES:(j + 1) * LANES] for j in range(nt)]
        if v_out:
            v_out[0][rows, :] = jnp.concatenate(vn, axis=-1)
        yield
        vb = [a.astype(BF16) for a in vn]
        win = [(0, 1), (1, 2), (3, 4), (4, 5)]
        m = [_dot(s_ref[k], jnp.concatenate([vb[a], vb[b]], axis=-1)) for k, (a, b) in enumerate(win)]
        mixed = [m[0][:, :LANES], jnp.where(lo, m[0][:, LANES:], m[1][:, :LANES]), m[1][:, LANES:],
                 m[2][:, :LANES], jnp.where(lo, m[2][:, LANES:], m[3][:, :LANES]), m[3][:, LANES:]]
        bs = bs_ref[...]
        gm = [u[j] * (mixed[j] + bs[:, j * LANES:(j + 1) * LANES]) for j in range(nt)]
        yield
        qm = (z[:, 2 * GM_W:] * Q_SCALE).astype(BF16)
        mo = []
        for b in range(seqs):
            r = slice(b * rows_per_mem, (b + 1) * rows_per_mem)
            mi = 0 if shared_mem else t * seqs + b
            mo.append(_mem_attend(qm[r], mk_ref[mi].astype(BF16), mv_ref[mi].astype(BF16)))
        mo = mo[0] if len(mo) == 1 else jnp.concatenate(mo, axis=0)
        yield
        cat = jnp.concatenate([a.astype(BF16) for a in gm] + [mo.astype(BF16)], axis=-1)
        o_ref[rows, :] = x_ref[rows, :] + _rms(_dot(cat, wout_ref[...]), gpost_ref[...])

    def body():
        _interleave([tile(t) for t in range(tiles)] + [_cast_blocks(cast_in, cast_out)])

    _after_fill(o_ref, fill_steps, body)


def _mem_spec(mem, seqs, layer):
    if mem.shape[1] == 1:
        return pl.BlockSpec((None, 1, MEM_W, MEM_LEN), lambda i: (layer, 0, 0, 0))
    return pl.BlockSpec((None, seqs, MEM_W, MEM_LEN), lambda i: (layer, i, 0, 0))


def _mixer_a(x, gpre, gpost, win, gln, bln, s_mat, bs, mkt, mvt, wout, *, rows_per_mem, emit_v, tiles=2,
             slab_rows, slab_offset=0, into=None, cast=()):
    n = x.shape[0]
    blk = tiles * TILE
    seqs = blk // rows_per_mem
    off = slab_offset // blk
    fill, shift = (0, off) if into is not None else (off, 0)
    row = lambda w: pl.BlockSpec((blk, w), lambda i: (jnp.maximum(i - fill, 0), 0))
    out_shape = [jax.ShapeDtypeStruct((slab_rows, D_MODEL), F32)]
    out_specs = [pl.BlockSpec((blk, D_MODEL), lambda i: (i + shift, 0))]
    if emit_v:
        out_shape.append(jax.ShapeDtypeStruct((n, GM_W), F32))
        out_specs.append(row(GM_W))
    cast_in_specs, cast_out_specs, cast_out_shape = _cast_specs(cast, n // blk, fill)
    return _call(
        functools.partial(_mixer_a_kernel, rows_per_mem=rows_per_mem, tiles=tiles,
                          shared_mem=mkt.shape[1] == 1, fill_steps=fill, n_cast=len(cast)), "mixer_a", into=into,
        semantics="arbitrary" if fill and cast else "parallel",
        grid=(n // blk + fill,),
        in_specs=[
            row(D_MODEL),
            _const_spec((1, D_MODEL)), _const_spec((1, D_MODEL)),
            _const_spec((D_MODEL, 2 * GM_W + MEM_W)),
            _const_spec((1, GM_W)), _const_spec((1, GM_W)),
            _const_spec((GM_GROUPS, TILE, TILE)), _const_spec((TILE, GM_W)),
            _mem_spec(mkt, seqs, 0), _mem_spec(mvt, seqs, 0),
            _const_spec((GM_W + MEM_W, D_MODEL)),
            *cast_in_specs,
        ],
        out_specs=out_specs + cast_out_specs,
        out_shape=out_shape + cast_out_shape,
    )(x, gpre, gpost, win, gln, bln, s_mat, bs, mkt, mvt, wout, *[a for a, _ in cast])


def _ffn_kernel(x_ref, gpre_ref, gpost_ref, w1_ref, w2_ref, o_ref, *o_tail, rows, tiles, ff_chunk, head_steps):
    def tile(t):
        r = slice(t * rows, (t + 1) * rows)
        x = x_ref[r, :]
        xn = _rms(x, gpre_ref[...]).astype(BF16)
        acc = jnp.zeros(x.shape, F32)
        yield
        for c in range(D_FF // ff_chunk):
            h = _dot(xn, w1_ref[:, c * ff_chunk:(c + 1) * ff_chunk])
            h = jnp.square(jnp.maximum(h, 0.0)).astype(BF16)
            acc = acc + _dot(h, w2_ref[c * ff_chunk:(c + 1) * ff_chunk, :])
            yield
        y = x + _rms(acc, gpost_ref[...])
        if not o_tail:
            o_ref[r, :] = y
        else:
            is_head = pl.program_id(0) < head_steps
            o_ref[r, :] = jnp.where(is_head, y, o_ref[r, :])
            o_tail[0][r, :] = y

    if o_tail:
        @pl.when(pl.program_id(0) < head_steps)
        def _():
            o_ref[...] = jnp.zeros(o_ref.shape, F32)

    _interleave([tile(t) for t in range(tiles)])


def _ffn(x, gpre, gpost, w1, w2, layer, *, rows=512, tiles=2, ff_chunk=1024, split=None):
    n = x.shape[0]
    blk = tiles * rows
    row = pl.BlockSpec((blk, D_MODEL), lambda i: (i, 0))
    if split is None:
        head_steps, out_specs, out_shape = None, row, jax.ShapeDtypeStruct((n, D_MODEL), F32)
    else:
        head_steps = split // blk
        out_specs = [pl.BlockSpec((blk, D_MODEL), lambda i: (jnp.minimum(i, head_steps - 1), 0)),
                     pl.BlockSpec((blk, D_MODEL), lambda i: (jnp.maximum(i - head_steps, 0), 0))]
        out_shape = [jax.ShapeDtypeStruct((split, D_MODEL), F32), jax.ShapeDtypeStruct((n - split, D_MODEL), F32)]
    return _call(
        functools.partial(_ffn_kernel, rows=rows, tiles=tiles, ff_chunk=ff_chunk, head_steps=head_steps),
        "ffn", semantics="arbitrary",
        grid=(n // blk,),
        in_specs=[row, _layer_spec((1, D_MODEL), layer), _layer_spec((1, D_MODEL), layer),
                  _const_spec((D_MODEL, D_FF)), _const_spec((D_FF, D_MODEL))],
        out_specs=out_specs,
        out_shape=out_shape,
    )(x, gpre, gpost, w1, w2)


def _proj_b_kernel(x_ref, gkv_ref, gpre_ref, *refs, mode, seq_len, n_weights, n_cast):
    rows = x_ref.shape[0]
    cast_in, outs, cast_out = _split_cast_refs(refs[n_weights:], n_cast)
    refs = (*refs[:n_weights], *outs)

    def normed(r):
        x = x_ref[r, :]
        xh = x * lax.rsqrt(jnp.mean(x * x, axis=-1, keepdims=True) + EPS)
        return (xh * gkv_ref[...]).astype(BF16), (xh * gpre_ref[...]).astype(BF16)

    if mode == "tail":
        wkt_ref, wvt_ref, kt_ref, vt_ref = refs
        xkv, _ = normed(slice(None))
        kt_ref[...] = _dot_nt(wkt_ref[...], xkv)
        vt_ref[...] = _dot_nt(wvt_ref[...], xkv)
        return
    if mode == "prompt":
        wqt_ref, wqm_ref, wk_ref, wvt_ref, qt_ref, qm_ref, k_ref, vt_ref = refs

        def tile(t):
            r = slice(t * PROJ_ROWS, (t + 1) * PROJ_ROWS)
            xkv, xq = normed(r)
            yield
            qt = _dot_nt(wqt_ref[...], xq) * Q_SCALE
            for p in range(HEAD_PAIRS):
                qt_ref[p, :, r] = qt[p * LANES:(p + 1) * LANES, :].astype(BF16)
            yield
            vt = _dot_nt(wvt_ref[...], xkv)
            for p in range(HEAD_PAIRS):
                vt_ref[p, :, r] = vt[p * LANES:(p + 1) * LANES, :].astype(BF16)
            yield
            k = _dot(xkv, wk_ref[...])
            for p in range(HEAD_PAIRS):
                k_ref[p, r, :] = k[:, p * LANES:(p + 1) * LANES].astype(BF16)
            yield
            qm_ref[r, :] = (_dot(xq, wqm_ref[...]) * Q_SCALE).astype(BF16)

        _interleave([tile(t) for t in range(rows // PROJ_ROWS)] + [_cast_blocks(cast_in, cast_out)])
        return
    win_ref, wk_ref, wv_ref, q_ref, qm_ref, k_ref, v_ref, k4_ref, v4_ref = refs
    xkv, xq = normed(slice(None))
    z = _dot(xq, win_ref[...]) * Q_SCALE
    qm_ref[...] = z[:, B_W:].astype(BF16)
    k = _dot(xkv, wk_ref[...])
    v = _dot(xkv, wv_ref[...])
    for p in range(HEAD_PAIRS):
        cols = slice(p * LANES, (p + 1) * LANES)
        q_ref[p] = z[:, cols].astype(BF16)
        k_ref[p] = k[:, cols].astype(BF16)
        v_ref[p] = v[:, cols].astype(BF16)
    for b in range(rows // seq_len):
        for h in range(B_HEADS):
            r, c = slice(b * seq_len, (b + 1) * seq_len), slice(h * HEAD_DIM, (h + 1) * HEAD_DIM)
            k4_ref[b, h] = k[r, c]
            v4_ref[b, h] = v[r, c]


def _proj_b(x, gkv, gpre, weights, *, mode, rows, row_offset=0, n_rows=None, seq_len=None, cast=()):
    n = x.shape[0] if n_rows is None else n_rows
    off = row_offset // rows
    pair_rows = jax.ShapeDtypeStruct((HEAD_PAIRS, n, LANES), BF16)
    pair_rows_spec = pl.BlockSpec((HEAD_PAIRS, rows, LANES), lambda i: (0, i, 0))
    pair_cols = jax.ShapeDtypeStruct((HEAD_PAIRS, LANES, n), BF16)
    pair_cols_spec = pl.BlockSpec((HEAD_PAIRS, LANES, rows), lambda i: (0, 0, i))
    qm = jax.ShapeDtypeStruct((n, MEM_W), BF16)
    qm_spec = pl.BlockSpec((rows, MEM_W), lambda i: (i, 0))
    if mode == "tail":
        out_shape = [jax.ShapeDtypeStruct((B_W, n), F32)] * 2
        out_specs = [pl.BlockSpec((B_W, rows), lambda i: (0, i))] * 2
    elif mode == "prompt":
        out_shape = [pair_cols, qm, pair_rows, pair_cols]
        out_specs = [pair_cols_spec, qm_spec, pair_rows_spec, pair_cols_spec]
    else:
        seqs = rows // seq_len
        per_head = jax.ShapeDtypeStruct((n // seq_len, B_HEADS, seq_len, HEAD_DIM), F32)
        per_head_spec = pl.BlockSpec((seqs, B_HEADS, seq_len, HEAD_DIM), lambda i: (i, 0, 0, 0))
        out_shape = [pair_rows, qm, pair_rows, pair_rows, per_head, per_head]
        out_specs = [pair_rows_spec, qm_spec, pair_rows_spec, pair_rows_spec, per_head_spec, per_head_spec]
    cast_in_specs, cast_out_specs, cast_out_shape = _cast_specs(cast, n // rows, 0)
    return _call(
        functools.partial(_proj_b_kernel, mode=mode, seq_len=seq_len, n_weights=len(weights),
                          n_cast=len(cast)), "proj_b_" + mode,
        grid=(n // rows,),
        in_specs=[pl.BlockSpec((rows, D_MODEL), lambda i: (i + off, 0)),
                  _const_spec((1, D_MODEL)), _const_spec((1, D_MODEL))]
                 + [_const_spec(w.shape) for w in weights] + cast_in_specs,
        out_specs=out_specs + cast_out_specs,
        out_shape=out_shape + cast_out_shape,
    )(x, gkv, gpre, *weights, *[a for a, _ in cast])


def _bias_kernel(g_ref, *o_refs, n_q, n_k, splits, band, keys_on_rows):
    n_rows, shift = (n_k, n_q) if keys_on_rows else (n_q, BIAS_PERIOD - (n_q - 1))
    for hh in range(2):
        x = jnp.broadcast_to(g_ref[hh:hh + 1, :], (n_rows, BIAS_PERIOD))
        t = pltpu.roll(x, shift, 1, stride=1, stride_axis=0)
        if band:
            a = lax.broadcasted_iota(jnp.int32, (n_rows, BIAS_PERIOD), 0)
            b = lax.broadcasted_iota(jnp.int32, (n_rows, BIAS_PERIOD), 1)
            r, w = (b, a) if keys_on_rows else (a, b)
            j = w - (r - (r & (CHUNK - 1)))
            t = jnp.where((j >= 0) & (j < BAND_PAST + CHUNK), t, NEG)
        for o_ref, (lo, hi) in zip(o_refs, splits):
            o_ref[hh * n_rows:(hh + 1) * n_rows, :] = t[:, lo:hi]


def _rel_bias_tables(rel_bias, n_q, n_k, splits, *, band, keys_on_rows):
    rel_bias = rel_bias - rel_bias[:, -1:]
    c0 = n_k - 1
    far = jnp.broadcast_to(rel_bias[:, -1:], (B_HEADS, c0 - REL_CLIP))
    near = jnp.broadcast_to(rel_bias[:, :1], (B_HEADS, BIAS_PERIOD - (c0 - REL_CLIP) - (2 * REL_CLIP + 1)))
    gen = [near, rel_bias, far] if keys_on_rows else [far, rel_bias[:, ::-1], near]
    gen = jnp.concatenate(gen, axis=1).reshape(HEAD_PAIRS, 2, BIAS_PERIOD)
    n_rows = n_k if keys_on_rows else n_q
    return _call(
        functools.partial(_bias_kernel, n_q=n_q, n_k=n_k, splits=splits, band=band,
                          keys_on_rows=keys_on_rows), "rel_bias",
        grid=(HEAD_PAIRS,),
        in_specs=[pl.BlockSpec((None, 2, BIAS_PERIOD), lambda p: (p, 0, 0))],
        out_specs=[pl.BlockSpec((None, 2 * n_rows, hi - lo), lambda p: (p, 0, 0)) for lo, hi in splits],
        out_shape=[jax.ShapeDtypeStruct((HEAD_PAIRS, 2 * n_rows, hi - lo), F32) for lo, hi in splits],
    )(gen)


def _band_block_kinds():
    n_k = KEY_BLOCKS * TILE
    r = np.arange(TILE)[None, :]
    w = np.arange(n_k)[:, None]
    j = w - CHUNK * (r // CHUNK)
    ok = (j >= 0) & (j < BAND_PAST + CHUNK)
    plain = ok & (BAND_PAST + r - w >= REL_CLIP)
    kinds = []
    for a in range(n_k // KEY_BLOCK):
        rows = slice(a * KEY_BLOCK, (a + 1) * KEY_BLOCK)
        blocks = [(rows, slice(b * LANES, (b + 1) * LANES)) for b in range(TILE // LANES)]
        kinds.append(["skip" if not ok[blk].any() else "plain" if plain[blk].all() else "biased"
                      for blk in blocks])
    return kinds


def _band_prompt_kernel(x_ref, qt_ref, qm_ref, kp_ref, kc_ref, vtp_ref, vtc_ref,
                        bias_ref, mk_ref, mv_ref, wout_ref, gpost_ref, o_ref, *, fill_steps):
    i = pl.program_id(0) - fill_steps
    n_k = KEY_BLOCKS * TILE
    kinds = _band_block_kinds()
    w = lax.broadcasted_iota(jnp.int32, (n_k, LANES), 0)
    c = lax.broadcasted_iota(jnp.int32, (n_k, LANES), 1)
    ones_rows = jnp.where(lax.broadcasted_iota(jnp.int32, (BF16_ROWS, n_k), 0) == 0, 1.0, 0.0).astype(BF16)
    row = lax.broadcasted_iota(jnp.int32, (LANES, TILE), 0)
    one_hot_row = jnp.where(row == 0, 1.0, 0.0).astype(BF16)
    lo = row < HEAD_DIM

    def key_step(a, s, vth, bias_rows, state):
        blocks = range(a * KEY_STEP // KEY_BLOCK, (a + 1) * KEY_STEP // KEY_BLOCK)
        es, alphas, active = [], [], []
        for b in range(TILE // LANES):
            cols = slice(b * LANES, (b + 1) * LANES)
            sjs = {}
            for blk in blocks:
                if kinds[blk][b] != "skip":
                    rows = slice(blk * KEY_BLOCK, (blk + 1) * KEY_BLOCK)
                    sjs[blk] = s[rows, cols] + bias_rows(rows, cols) if kinds[blk][b] == "biased" else s[rows, cols]
            active.append(bool(sjs))
            alphas.append(None)
            if not sjs:
                es.append(jnp.zeros((KEY_STEP, LANES), F32))
                continue
            mj = jnp.max(functools.reduce(jnp.maximum, sjs.values()), axis=0, keepdims=True)
            if state[b] is None:
                state[b] = [mj, None]
            else:
                m_new = jnp.maximum(state[b][0], mj)
                alphas[b] = jnp.exp(state[b][0] - m_new)
                state[b][0] = m_new
            es.append(jnp.concatenate(
                [jnp.exp(sjs[blk] - state[b][0]) if blk in sjs else jnp.zeros((KEY_BLOCK, LANES), F32)
                 for blk in blocks], axis=0))
        rows = slice(a * KEY_STEP, (a + 1) * KEY_STEP)
        pv = _dot(vth[:, rows], jnp.concatenate(es, axis=1).astype(BF16))[:OUT_ROWS]
        for b in range(TILE // LANES):
            if active[b]:
                pv_b = pv[:, b * LANES:(b + 1) * LANES]
                state[b][1] = pv_b if alphas[b] is None else state[b][1] * alphas[b] + pv_b

    def tile(t):
        q_rows = slice(t * TILE, (t + 1) * TILE)
        k_rows = slice(t * TILE, t * TILE + n_k)
        first_key_tile = i * BAND_TILES + t - (KEY_BLOCKS - 1)
        pen = jnp.where((w < -first_key_tile * TILE) & (c == 0), NEG, 0.0).astype(BF16)
        outs = []
        for p0 in range(0, HEAD_PAIRS, GROUP_PAIRS):
            heads = []
            for p in range(p0, p0 + GROUP_PAIRS):
                k_win = jnp.concatenate([kp_ref[p], kc_ref[p]], axis=0)[k_rows]
                k_ext = jnp.concatenate([k_win, pen], axis=1)
                vt = jnp.concatenate([vtp_ref[p], vtc_ref[p]], axis=1)[:, k_rows]
                qt = qt_ref[p, :, q_rows]
                zero = jnp.zeros_like(qt)
                for hh in range(2):
                    qh = jnp.where(lo, qt, zero) if hh == 0 else jnp.where(lo, zero, qt)
                    s = _dot(k_ext, jnp.concatenate([qh, one_hot_row], axis=0))
                    vth = jnp.concatenate([vt[hh * HEAD_DIM:(hh + 1) * HEAD_DIM, :], ones_rows], axis=0)
                    bias_rows = functools.partial(
                        lambda rows, cols, p, base: bias_ref[p, base + rows.start:base + rows.stop, cols],
                        p=p, base=hh * n_k)
                    heads.append((s, vth, bias_rows, [None] * (TILE // LANES)))
            for a in range(n_k // KEY_STEP):
                for head in heads:
                    key_step(a, *head)
            outs += [jnp.concatenate([o[:HEAD_DIM] * (1.0 / o[HEAD_DIM:HEAD_DIM + 1]) for _, o in state], axis=1)
                     for *_, state in heads]
            yield
        band = jnp.concatenate(outs, axis=0).T.astype(BF16)
        mo = _mem_attend(qm_ref[q_rows, :], mk_ref[0].astype(BF16), mv_ref[0].astype(BF16))
        cat = jnp.concatenate([band, mo.astype(BF16)], axis=-1)
        o_ref[q_rows, :] = x_ref[q_rows, :] + _rms(_dot(cat, wout_ref[...]), gpost_ref[...])

    _after_fill(o_ref, fill_steps, lambda: _interleave([tile(t) for t in range(BAND_TILES)]))


def _band_prompt(x, qt3, qm, k3, vt3, bias, mkt, mvt, wout, gpost):
    n = qm.shape[0]
    rows = BAND_TILES * TILE
    assert rows == BAND_PAST and n % rows == 0 and (x.shape[0] - n) % rows == 0
    fill = (x.shape[0] - n) // rows
    slab_spec = pl.BlockSpec((rows, D_MODEL), lambda i: (i, 0))
    own = lambda i: jnp.maximum(i - fill, 0)
    past = lambda i: jnp.maximum(i - fill - 1, 0)
    return _call(
        functools.partial(_band_prompt_kernel, fill_steps=fill), "band_prompt",
        grid=(n // rows + fill,),
        in_specs=[slab_spec,
                  pl.BlockSpec((HEAD_PAIRS, LANES, rows), lambda i: (0, 0, own(i))),
                  pl.BlockSpec((rows, MEM_W), lambda i: (own(i), 0)),
                  pl.BlockSpec((HEAD_PAIRS, rows, LANES), lambda i: (0, past(i), 0)),
                  pl.BlockSpec((HEAD_PAIRS, rows, LANES), lambda i: (0, own(i), 0)),
                  pl.BlockSpec((HEAD_PAIRS, LANES, rows), lambda i: (0, 0, past(i))),
                  pl.BlockSpec((HEAD_PAIRS, LANES, rows), lambda i: (0, 0, own(i))),
                  _const_spec((HEAD_PAIRS, 2 * KEY_BLOCKS * TILE, TILE)),
                  _mem_spec(mkt, 1, 1), _mem_spec(mvt, 1, 1),
                  _const_spec((B_W + MEM_W, D_MODEL)), _const_spec((1, D_MODEL))],
        out_specs=slab_spec,
        out_shape=jax.ShapeDtypeStruct(x.shape, F32),
    )(x, qt3, qm, k3, k3, vt3, vt3, bias, mkt, mvt, wout, gpost)


def _band_sample_kernel(x_ref, q_ref, qm_ref, kn_ref, vn_ref, ck_ref, cv_ref, bc_ref, bn_ref,
                        mk_ref, mv_ref, wout_ref, gpost_ref, o_ref, *, seqs, seq_len):
    lane = lax.broadcasted_iota(jnp.int32, (1, LANES), 1)
    lo = lane < HEAD_DIM

    def pair_attend(b, p, out):
        rows = slice(b * seq_len, (b + 1) * seq_len)
        hd = slice(p * LANES, (p + 1) * LANES)
        qp = q_ref[p, rows, :]
        zero = jnp.zeros_like(qp)
        qs = jnp.concatenate([jnp.where(lo, qp, zero), jnp.where(lo, zero, qp)], axis=0)
        sc = _dot(qs, ck_ref[b, hd, :].astype(BF16)) + bc_ref[p]
        sn = _dot_nt(qs, kn_ref[p, rows, :]) + bn_ref[p]
        yield
        m = jnp.maximum(_lmax(sc), _lmax(sn))
        ec = jnp.exp(sc - m)
        en = jnp.exp(sn - m)
        l = _lsum(ec) + _lsum(en)
        yield
        o = (_dot_nt(ec.astype(BF16), cv_ref[b, hd, :].astype(BF16))
             + _dot(en.astype(BF16), vn_ref[p, rows, :]))
        yield
        o = o * (1.0 / l)
        out[p] = jnp.where(lo, o[:seq_len], o[seq_len:]).astype(BF16)

    def seq_attend(b, out):
        band = [None] * HEAD_PAIRS
        yield from _round_robin([pair_attend(b, p, band) for p in range(HEAD_PAIRS)])
        rows = slice(b * seq_len, (b + 1) * seq_len)
        mo = _mem_attend(qm_ref[rows, :], mk_ref[b].astype(BF16), mv_ref[b].astype(BF16))
        out[b] = jnp.concatenate(band + [mo.astype(BF16)], axis=-1)

    rows_out = [None] * seqs
    _interleave([seq_attend(b, rows_out) for b in range(seqs)])
    cat = jnp.concatenate(rows_out, axis=0)
    o_ref[...] = x_ref[...] + _rms(_dot(cat, wout_ref[...]), gpost_ref[...])


def _band_sample(x, q3, qm, kn3, vn3, ckt, cvt, bias_c, bias_n, mkt, mvt, wout, gpost, *, into, seqs=4):
    n = qm.shape[0]
    n_seq, past = ckt.shape[0], ckt.shape[2]
    seq_len = n // n_seq
    rows = seqs * seq_len
    slab_spec = pl.BlockSpec((rows, D_MODEL), lambda i: (i, 0))
    pair_rows_spec = pl.BlockSpec((HEAD_PAIRS, rows, LANES), lambda i: (0, i, 0))
    cache_spec = pl.BlockSpec((seqs, B_W, past), lambda i: (i, 0, 0))
    return _call(
        functools.partial(_band_sample_kernel, seqs=seqs, seq_len=seq_len), "band_sample", into=into,
        grid=(n_seq // seqs,),
        in_specs=[slab_spec,
                  pair_rows_spec,
                  pl.BlockSpec((rows, MEM_W), lambda i: (i, 0)),
                  pair_rows_spec, pair_rows_spec,
                  cache_spec, cache_spec,
                  _const_spec((HEAD_PAIRS, 2 * seq_len, past)),
                  _const_spec((HEAD_PAIRS, 2 * seq_len, seq_len)),
                  _mem_spec(mkt, seqs, 1), _mem_spec(mvt, seqs, 1),
                  _const_spec((B_W + MEM_W, D_MODEL)), _const_spec((1, D_MODEL))],
        out_specs=slab_spec,
        out_shape=jax.ShapeDtypeStruct(x.shape, F32),
    )(x, q3, qm, kn3, vn3, ckt, cvt, bias_c, bias_n, mkt, mvt, wout, gpost)


def _spatial_tile(w_s, b_s, period):
    tril = jnp.tril(jnp.ones((GM_CHUNK, GM_CHUNK), dtype=bool))
    w = jnp.where(tril, w_s, jnp.zeros((), w_s.dtype))[:, :period, :period]
    eye = jnp.eye(TILE // period, dtype=w.dtype)
    s_mat = jnp.einsum("ab,gts->gatbs", eye, w).reshape(GM_GROUPS, TILE, TILE)
    rows = jnp.tile(b_s[:, :period], (1, TILE // period))
    bs = jnp.repeat(rows.T, GM_GW, axis=1)
    return s_mat.astype(BF16), bs


def _heads_last(t, lead):
    pos = t.shape[-1]
    t = t.reshape(lead + (-1, HEAD_DIM, pos))
    nd = len(lead)
    return jnp.transpose(t, tuple(range(nd)) + (nd + 2, nd, nd + 1))


def _positions_last(c):
    nd = c.ndim
    t = jnp.transpose(c, tuple(range(nd - 3)) + (nd - 2, nd - 1, nd - 3))
    return t.reshape(c.shape[:-3] + (c.shape[-2] * c.shape[-1], c.shape[-3]))


def kernel(x_prompt, x_sample, cache_mem_k, cache_mem_v, cache_band_k, cache_band_v, mem_prompt,
           g_mix_pre, g_mix_post, g_ffn_pre, g_ffn_post, g_mem, w_mem_kv,
           w_in_a, g_gm_ln, b_gm_ln, w_spatial, b_spatial, w_out_a,
           g_kv, w_kv, w_in_b, rel_bias, w_out_b, w_ff1, w_ff2):
    seq = x_prompt.shape[1]
    n_seq, seq_len = x_sample.shape[0], x_sample.shape[1]
    past = cache_band_k.shape[1]
    vec = lambda a: a.reshape(1, -1)
    stack = lambda a: a.reshape(DEPTH, 1, -1)

    win_a = w_in_a[0].astype(BF16)
    wout_a = w_out_a[0].astype(BF16)
    wk = w_kv[:, :B_W].astype(BF16)
    wv = w_kv[:, B_W:].astype(BF16)
    wkt, wvt = wk.T, wv.T
    win_b = w_in_b[0].astype(BF16)
    wqt, wqm = win_b[:, :B_W].T, win_b[:, B_W:]
    wout_b = w_out_b[0].astype(BF16)
    ln_g, ln_b = vec(g_gm_ln[0]), vec(b_gm_ln[0])
    gkv, gpre_b = vec(g_kv), vec(g_mix_pre[1])
    gf_pre, gf_post = stack(g_ffn_pre), stack(g_ffn_post)

    n_sample = n_seq * seq_len
    slab_rows = seq + n_sample
    pre_a, post_a = vec(g_mix_pre[0]), vec(g_mix_post[0])

    mem_kt, mem_vt = _memkv(mem_prompt[0], g_mem, jnp.swapaxes(w_mem_kv, 1, 2).astype(BF16))
    s_p, bs_p = _spatial_tile(w_spatial[0], b_spatial[0], GM_CHUNK)
    s_s, bs_s = _spatial_tile(w_spatial[0], b_spatial[0], seq_len)
    cmkt, cmvt = _positions_last(cache_mem_k), _positions_last(cache_mem_v)
    x, w1, w2 = _mixer_a(x_prompt[0], pre_a, post_a, win_a, ln_g, ln_b, s_p, bs_p, mem_kt, mem_vt, wout_a,
                         rows_per_mem=TILE, emit_v=False, tiles=4, slab_rows=slab_rows, slab_offset=n_sample,
                         cast=((w_ff1, 0), (w_ff2, 0)))
    x, v_rows = _mixer_a(x_sample.reshape(n_sample, D_MODEL), pre_a, post_a, win_a, ln_g, ln_b, s_s, bs_s,
                         cmkt, cmvt, wout_a, rows_per_mem=seq_len, emit_v=True,
                         slab_rows=slab_rows, into=x)
    x = _ffn(x, gf_pre, gf_post, w1, w2, 0)

    qt3, qm, k3, vt3, w1, w2 = _proj_b(x, gkv, gpre_b, (wqt, wqm, wk, wvt), mode="prompt", rows=2 * PROJ_ROWS,
                                       row_offset=n_sample, n_rows=seq, cast=((w_ff1, 1), (w_ff2, 1)))
    n_keep = min(BAND_PAST, seq)
    kt_tail, vt_tail = _proj_b(x, gkv, gpre_b, (wkt, wvt), mode="tail", rows=TILE,
                               row_offset=n_sample + seq - n_keep, n_rows=n_keep)
    q3s, qms, kn3, vn3, k_new, v_new = _proj_b(x, gkv, gpre_b, (win_b, wk, wv), mode="sample", rows=512,
                                               n_rows=n_sample, seq_len=seq_len)
    n_k = BAND_PAST + TILE
    bias_p, = _rel_bias_tables(rel_bias[0], TILE, n_k, ((0, TILE),), band=True, keys_on_rows=True)
    bias_c, bias_n = _rel_bias_tables(rel_bias[0], seq_len, past + seq_len,
                                      ((0, past), (past, past + seq_len)), band=False, keys_on_rows=False)
    post_b = vec(g_mix_post[1])
    y = _band_prompt(x, qt3, qm, k3, vt3, bias_p, mem_kt, mem_vt, wout_b, post_b)
    y = _band_sample(x, q3s, qms, kn3, vn3, _positions_last(cache_band_k), _positions_last(cache_band_v),
                     bias_c, bias_n, cmkt, cmvt, wout_b, post_b, into=y)
    y_sample, y_prompt = _ffn(y, gf_pre, gf_post, w1, w2, 1, split=n_sample)
    y_prompt = y_prompt[None]
    y_sample = y_sample.reshape(n_seq, seq_len, D_MODEL)

    return (y_prompt, y_sample,
            _heads_last(mem_kt, (DEPTH, 1)), _heads_last(mem_vt, (DEPTH, 1)),
            _heads_last(kt_tail, (1,)), _heads_last(vt_tail, (1,)),
            v_rows.reshape(1, n_seq, seq_len, GM_W),
            jnp.swapaxes(k_new, 1, 2), jnp.swapaxes(v_new, 1, 2))
```

```python
import functools

import jax
import jax.numpy as jnp
import numpy as np
from jax import lax
from jax.experimental import pallas as pl
from jax.experimental.pallas import tpu as pltpu

D_MODEL = 1024
DEPTH = 2
CHUNK = 64
HEAD_DIM = 64
GM_CHUNK = 128
GM_GROUPS = 4
GM_W = 768
GM_GW = GM_W // GM_GROUPS
MEM_LEN = 256
MEM_HEADS = 4
MEM_W = MEM_HEADS * HEAD_DIM
B_HEADS = 12
B_W = B_HEADS * HEAD_DIM
BAND_PAST = 512
REL_CLIP = 128
D_FF = 4 * D_MODEL
EPS = 1e-6

LANES = 128
SUBLANES = 8
HEAD_PAIRS = B_W // LANES
Q_SCALE = HEAD_DIM ** -0.5
NEG = -1e30
TILE = 256
KEY_BLOCKS = BAND_PAST // TILE + 1
KEY_STEP = 128
KEY_BLOCK = 128
PROJ_ROWS = 512
BF16_ROWS = 2 * SUBLANES
OUT_ROWS = HEAD_DIM + SUBLANES
BAND_TILES = BAND_PAST // TILE
GROUP_PAIRS = 2
BIAS_PERIOD = 1024
V7X_VMEM_BYTES = 64 * 1024 * 1024
VMEM_LIMIT = V7X_VMEM_BYTES * 7 // 8

BF16 = jnp.bfloat16
F32 = jnp.float32


def _dot(a, b):
    return jnp.dot(a, b, preferred_element_type=F32)


def _dot_nt(a, b):
    return lax.dot_general(a, b, (((1,), (1,)), ((), ())), preferred_element_type=F32)


def _rms(x, g):
    ms = jnp.mean(x * x, axis=-1, keepdims=True)
    return x * lax.rsqrt(ms + EPS) * g


def _lsum(a):
    return jnp.sum(a, axis=-1, keepdims=True)


def _lmax(a):
    return jnp.max(a, axis=-1, keepdims=True)


def _const_spec(shape):
    nd = len(shape)
    return pl.BlockSpec(shape, lambda *_: (0,) * nd, pipeline_mode=pl.Buffered(1))


def _layer_spec(shape, layer):
    nd = len(shape)
    return pl.BlockSpec((None,) + shape, lambda *_: (layer,) + (0,) * nd, pipeline_mode=pl.Buffered(1))


def _call(body, name, *, semantics="parallel", **kw):
    params = pltpu.CompilerParams(dimension_semantics=(semantics,), vmem_limit_bytes=VMEM_LIMIT)
    return pl.pallas_call(body, name=name, compiler_params=params, **kw)


def _memkv_kernel(mem_ref, g_ref, wt_ref, kt_ref, vt_ref):
    ht = _dot_nt(wt_ref[...], _rms(mem_ref[...], g_ref[...]).astype(BF16))
    kt_ref[0] = ht[:MEM_W]
    vt_ref[0] = ht[MEM_W:]


def _memkv(mem, g_mem, w_mem_kv_t):
    out = jax.ShapeDtypeStruct((DEPTH, 1, MEM_W, MEM_LEN), F32)
    return _call(
        _memkv_kernel, "mem_kv",
        grid=(DEPTH,),
        in_specs=[
            pl.BlockSpec((MEM_LEN, D_MODEL), lambda l: (0, 0)),
            pl.BlockSpec((None, 1, D_MODEL), lambda l: (l, 0, 0)),
            pl.BlockSpec((None, 2 * MEM_W, D_MODEL), lambda l: (l, 0, 0)),
        ],
        out_specs=[pl.BlockSpec((None, 1, MEM_W, MEM_LEN), lambda l: (l, 0, 0, 0))] * 2,
        out_shape=[out, out],
    )(mem, g_mem.reshape(DEPTH, 1, D_MODEL), w_mem_kv_t)


def _mem_attend(qb, kt, vt):
    r = qb.shape[0]
    lane = lax.broadcasted_iota(jnp.int32, (1, MEM_W), 1)
    masks = [(lane >= h * HEAD_DIM) & (lane < (h + 1) * HEAD_DIM) for h in range(MEM_HEADS)]
    qs = jnp.concatenate([jnp.where(m, qb, jnp.zeros_like(qb)) for m in masks], axis=0)
    s = _dot(qs, kt)
    e = jnp.exp(s - _lmax(s))
    pv = _dot_nt(e.astype(BF16), vt) * (1.0 / _lsum(e))
    out = jnp.where(masks[0], pv[:r], 0.0)
    for h in range(1, MEM_HEADS):
        out = out + jnp.where(masks[h], pv[h * r:(h + 1) * r], 0.0)
    return out


def _gelu(x):
    c1 = float(np.sqrt(2.0 / np.pi))
    c2 = c1 * 0.044715
    half = 0.5 * x
    return half + half * jnp.tanh(x * (c1 + c2 * (x * x)))


def _cast_specs(cast, steps):
    in_specs, out_specs, out_shape = [], [], []
    for a, layer in cast:
        _, r, c = a.shape
        in_specs.append(pl.BlockSpec((None, r // steps, c), functools.partial(lambda i, l: (l, i, 0), l=layer)))
        out_specs.append(pl.BlockSpec((r // steps, c), lambda i: (i, 0)))
        out_shape.append(jax.ShapeDtypeStruct((r, c), BF16))
    return in_specs, out_specs, out_shape


def _split_cast_refs(refs, n_cast):
    return refs[:n_cast], refs[n_cast:len(refs) - n_cast], refs[len(refs) - n_cast:]


def _cast_blocks(srcs, dsts, stages=4):
    for src, dst in zip(srcs, dsts):
        step = src.shape[0] // stages
        for k in range(stages):
            dst[k * step:(k + 1) * step, :] = src[k * step:(k + 1) * step, :].astype(dst.dtype)
            yield


def _round_robin(stage_lists):
    live = list(stage_lists)
    while live:
        for g in list(live):
            if next(g, StopIteration) is StopIteration:
                live.remove(g)
        yield


def _interleave(stage_lists):
    for _ in _round_robin(stage_lists):
        pass


def _mixer_a_kernel(x_ref, gpre_ref, gpost_ref, win_ref, gln_ref, bln_ref, s_ref, bs_ref,
                    mk_ref, mv_ref, wout_ref, *rest, rows_per_mem, tiles, shared_mem, n_cast):
    cast_in, (o_ref, *v_out), cast_out = _split_cast_refs(rest, n_cast)
    nt = GM_W // LANES
    seqs = TILE // rows_per_mem
    lane = lax.broadcasted_iota(jnp.int32, (1, LANES), 1)
    lo = lane < (GM_GW - LANES)
    inv = 1.0 / GM_GW

    def group_stat(a):
        s0 = _lsum(a[0] + jnp.where(lo, a[1], 0.0)) * inv
        s1 = _lsum(jnp.where(lo, 0.0, a[1]) + a[2]) * inv
        s2 = _lsum(a[3] + jnp.where(lo, a[4], 0.0)) * inv
        s3 = _lsum(jnp.where(lo, 0.0, a[4]) + a[5]) * inv
        return [s0, jnp.where(lo, s0, s1), s1, s2, jnp.where(lo, s2, s3), s3]

    def tile(t):
        rows = slice(t * TILE, (t + 1) * TILE)
        z = _dot(_rms(x_ref[rows, :], gpre_ref[...]).astype(BF16), win_ref[...])
        yield
        u = [_gelu(z[:, j * LANES:(j + 1) * LANES]) for j in range(nt)]
        g = [_gelu(z[:, GM_W + j * LANES:GM_W + (j + 1) * LANES]) for j in range(nt)]
        mu = group_stat(g)
        c = [g[j] - mu[j] for j in range(nt)]
        var = group_stat([cj * cj for cj in c])
        gln = gln_ref[...]
        bln = bln_ref[...]
        vn = [c[j] * lax.rsqrt(var[j] + EPS) * gln[:, j * LANES:(j + 1) * LANES]
              + bln[:, j * LANES:(j + 1) * LANES] for j in range(nt)]
        if v_out:
            v_out[0][rows, :] = jnp.concatenate(vn, axis=-1)
        yield
        vb = [a.astype(BF16) for a in vn]
        win = [(0, 1), (1, 2), (3, 4), (4, 5)]
        m = [_dot(s_ref[k], jnp.concatenate([vb[a], vb[b]], axis=-1)) for k, (a, b) in enumerate(win)]
        mixed = [m[0][:, :LANES], jnp.where(lo, m[0][:, LANES:], m[1][:, :LANES]), m[1][:, LANES:],
                 m[2][:, :LANES], jnp.where(lo, m[2][:, LANES:], m[3][:, :LANES]), m[3][:, LANES:]]
        bs = bs_ref[...]
        gm = [u[j] * (mixed[j] + bs[:, j * LANES:(j + 1) * LANES]) for j in range(nt)]
        yield
        qm = (z[:, 2 * GM_W:] * Q_SCALE).astype(BF16)
        mo = []
        for b in range(seqs):
            r = slice(b * rows_per_mem, (b + 1) * rows_per_mem)
            mi = 0 if shared_mem else t * seqs + b
            mo.append(_mem_attend(qm[r], mk_ref[mi].astype(BF16), mv_ref[mi].astype(BF16)))
        mo = mo[0] if len(mo) == 1 else jnp.concatenate(mo, axis=0)
        yield
        cat = jnp.concatenate([a.astype(BF16) for a in gm] + [mo.astype(BF16)], axis=-1)
        o_ref[rows, :] = x_ref[rows, :] + _rms(_dot(cat, wout_ref[...]), gpost_ref[...])

    _interleave([tile(t) for t in range(tiles)] + [_cast_blocks(cast_in, cast_out)])


def _mem_spec(mem, seqs, layer):
    if mem.shape[1] == 1:
        return pl.BlockSpec((None, 1, MEM_W, MEM_LEN), lambda i: (layer, 0, 0, 0))
    return pl.BlockSpec((None, seqs, MEM_W, MEM_LEN), lambda i: (layer, i, 0, 0))


def _mixer_a(x, gpre, gpost, win, gln, bln, s_mat, bs, mkt, mvt, wout, *, rows_per_mem, emit_v, tiles=2,
             cast=()):
    n = x.shape[0]
    blk = tiles * TILE
    seqs = blk // rows_per_mem
    row = lambda w: pl.BlockSpec((blk, w), lambda i: (i, 0))
    out_shape = [jax.ShapeDtypeStruct((n, D_MODEL), F32)]
    out_specs = [row(D_MODEL)]
    if emit_v:
        out_shape.append(jax.ShapeDtypeStruct((n, GM_W), F32))
        out_specs.append(row(GM_W))
    cast_in_specs, cast_out_specs, cast_out_shape = _cast_specs(cast, n // blk)
    return _call(
        functools.partial(_mixer_a_kernel, rows_per_mem=rows_per_mem, tiles=tiles,
                          shared_mem=mkt.shape[1] == 1, n_cast=len(cast)), "mixer_a",
        grid=(n // blk,),
        in_specs=[
            row(D_MODEL),
            _const_spec((1, D_MODEL)), _const_spec((1, D_MODEL)),
            _const_spec((D_MODEL, 2 * GM_W + MEM_W)),
            _const_spec((1, GM_W)), _const_spec((1, GM_W)),
            _const_spec((GM_GROUPS, TILE, TILE)), _const_spec((TILE, GM_W)),
            _mem_spec(mkt, seqs, 0), _mem_spec(mvt, seqs, 0),
            _const_spec((GM_W + MEM_W, D_MODEL)),
            *cast_in_specs,
        ],
        out_specs=out_specs + cast_out_specs,
        out_shape=out_shape + cast_out_shape,
    )(x, gpre, gpost, win, gln, bln, s_mat, bs, mkt, mvt, wout, *[a for a, _ in cast])


def _ffn_kernel(xh_ref, xt_ref, gpre_ref, gpost_ref, w1_ref, w2_ref, o_ref, *o_tail,
                rows, tiles, ff_chunk, head_steps):
    is_head = pl.program_id(0) < head_steps

    def tile(t):
        r = slice(t * rows, (t + 1) * rows)
        x = jnp.where(is_head, xh_ref[r, :], xt_ref[r, :])
        xn = _rms(x, gpre_ref[...]).astype(BF16)
        acc = jnp.zeros(x.shape, F32)
        yield
        for c in range(D_FF // ff_chunk):
            h = _dot(xn, w1_ref[:, c * ff_chunk:(c + 1) * ff_chunk])
            h = jnp.square(jnp.maximum(h, 0.0)).astype(BF16)
            acc = acc + _dot(h, w2_ref[c * ff_chunk:(c + 1) * ff_chunk, :])
            yield
        y = x + _rms(acc, gpost_ref[...])
        if not o_tail:
            o_ref[r, :] = y
        else:
            o_ref[r, :] = jnp.where(is_head, y, o_ref[r, :])
            o_tail[0][r, :] = y

    if o_tail:
        @pl.when(pl.program_id(0) < head_steps)
        def _():
            o_ref[...] = jnp.zeros(o_ref.shape, F32)

    _interleave([tile(t) for t in range(tiles)])


def _ffn(x_head, x_tail, gpre, gpost, w1, w2, layer, *, rows=512, tiles=2, ff_chunk=1024, split_out):
    n_head, n = x_head.shape[0], x_head.shape[0] + x_tail.shape[0]
    blk = tiles * rows
    head_steps = n_head // blk
    head = pl.BlockSpec((blk, D_MODEL), lambda i: (jnp.minimum(i, head_steps - 1), 0), pipeline_mode=pl.Buffered(1))
    tail = pl.BlockSpec((blk, D_MODEL), lambda i: (jnp.maximum(i - head_steps, 0), 0))
    if split_out:
        out_specs = [head, tail]
        out_shape = [jax.ShapeDtypeStruct((n_head, D_MODEL), F32), jax.ShapeDtypeStruct((n - n_head, D_MODEL), F32)]
    else:
        out_specs, out_shape = pl.BlockSpec((blk, D_MODEL), lambda i: (i, 0)), jax.ShapeDtypeStruct((n, D_MODEL), F32)
    return _call(
        functools.partial(_ffn_kernel, rows=rows, tiles=tiles, ff_chunk=ff_chunk, head_steps=head_steps),
        "ffn", semantics="arbitrary",
        grid=(n // blk,),
        in_specs=[head, tail, _layer_spec((1, D_MODEL), layer), _layer_spec((1, D_MODEL), layer),
                  _const_spec((D_MODEL, D_FF)), _const_spec((D_FF, D_MODEL))],
        out_specs=out_specs,
        out_shape=out_shape,
    )(x_head, x_tail, gpre, gpost, w1, w2)


def _proj_b_kernel(x_ref, gkv_ref, gpre_ref, *refs, mode, seq_len, n_weights, n_cast):
    rows = x_ref.shape[0]
    cast_in, outs, cast_out = _split_cast_refs(refs[n_weights:], n_cast)
    refs = (*refs[:n_weights], *outs)

    def normed(r):
        x = x_ref[r, :]
        xh = x * lax.rsqrt(jnp.mean(x * x, axis=-1, keepdims=True) + EPS)
        return (xh * gkv_ref[...]).astype(BF16), (xh * gpre_ref[...]).astype(BF16)

    if mode == "tail":
        wkt_ref, wvt_ref, kt_ref, vt_ref = refs
        xkv, _ = normed(slice(None))
        kt_ref[...] = _dot_nt(wkt_ref[...], xkv)
        vt_ref[...] = _dot_nt(wvt_ref[...], xkv)
        return
    if mode == "prompt":
        wqt_ref, wqm_ref, wk_ref, wvt_ref, qt_ref, qm_ref, k_ref, vt_ref = refs

        def tile(t):
            r = slice(t * PROJ_ROWS, (t + 1) * PROJ_ROWS)
            xkv, xq = normed(r)
            yield
            qt = _dot_nt(wqt_ref[...], xq) * Q_SCALE
            for p in range(HEAD_PAIRS):
                qt_ref[p, :, r] = qt[p * LANES:(p + 1) * LANES, :].astype(BF16)
            yield
            vt = _dot_nt(wvt_ref[...], xkv)
            for p in range(HEAD_PAIRS):
                vt_ref[p, :, r] = vt[p * LANES:(p + 1) * LANES, :].astype(BF16)
            yield
            k = _dot(xkv, wk_ref[...])
            for p in range(HEAD_PAIRS):
                k_ref[p, r, :] = k[:, p * LANES:(p + 1) * LANES].astype(BF16)
            yield
            qm_ref[r, :] = (_dot(xq, wqm_ref[...]) * Q_SCALE).astype(BF16)

        _interleave([tile(t) for t in range(rows // PROJ_ROWS)] + [_cast_blocks(cast_in, cast_out)])
        return
    win_ref, wk_ref, wv_ref, q_ref, qm_ref, k_ref, v_ref, k4_ref, v4_ref = refs
    xkv, xq = normed(slice(None))
    z = _dot(xq, win_ref[...]) * Q_SCALE
    qm_ref[...] = z[:, B_W:].astype(BF16)
    k = _dot(xkv, wk_ref[...])
    v = _dot(xkv, wv_ref[...])
    for p in range(HEAD_PAIRS):
        cols = slice(p * LANES, (p + 1) * LANES)
        q_ref[p] = z[:, cols].astype(BF16)
        k_ref[p] = k[:, cols].astype(BF16)
        v_ref[p] = v[:, cols].astype(BF16)
    for b in range(rows // seq_len):
        for h in range(B_HEADS):
            r, c = slice(b * seq_len, (b + 1) * seq_len), slice(h * HEAD_DIM, (h + 1) * HEAD_DIM)
            k4_ref[b, h] = k[r, c]
            v4_ref[b, h] = v[r, c]


def _proj_b(x, gkv, gpre, weights, *, mode, rows, row_offset=0, n_rows=None, seq_len=None, cast=()):
    n = x.shape[0] if n_rows is None else n_rows
    off = row_offset // rows
    pair_rows = jax.ShapeDtypeStruct((HEAD_PAIRS, n, LANES), BF16)
    pair_rows_spec = pl.BlockSpec((HEAD_PAIRS, rows, LANES), lambda i: (0, i, 0))
    pair_cols = jax.ShapeDtypeStruct((HEAD_PAIRS, LANES, n), BF16)
    pair_cols_spec = pl.BlockSpec((HEAD_PAIRS, LANES, rows), lambda i: (0, 0, i))
    qm = jax.ShapeDtypeStruct((n, MEM_W), BF16)
    qm_spec = pl.BlockSpec((rows, MEM_W), lambda i: (i, 0))
    if mode == "tail":
        out_shape = [jax.ShapeDtypeStruct((B_W, n), F32)] * 2
        out_specs = [pl.BlockSpec((B_W, rows), lambda i: (0, i))] * 2
    elif mode == "prompt":
        out_shape = [pair_cols, qm, pair_rows, pair_cols]
        out_specs = [pair_cols_spec, qm_spec, pair_rows_spec, pair_cols_spec]
    else:
        seqs = rows // seq_len
        per_head = jax.ShapeDtypeStruct((n // seq_len, B_HEADS, seq_len, HEAD_DIM), F32)
        per_head_spec = pl.BlockSpec((seqs, B_HEADS, seq_len, HEAD_DIM), lambda i: (i, 0, 0, 0))
        out_shape = [pair_rows, qm, pair_rows, pair_rows, per_head, per_head]
        out_specs = [pair_rows_spec, qm_spec, pair_rows_spec, pair_rows_spec, per_head_spec, per_head_spec]
    cast_in_specs, cast_out_specs, cast_out_shape = _cast_specs(cast, n // rows)
    return _call(
        functools.partial(_proj_b_kernel, mode=mode, seq_len=seq_len, n_weights=len(weights),
                          n_cast=len(cast)), "proj_b_" + mode,
        grid=(n // rows,),
        in_specs=[pl.BlockSpec((rows, D_MODEL), lambda i: (i + off, 0)),
                  _const_spec((1, D_MODEL)), _const_spec((1, D_MODEL))]
                 + [_const_spec(w.shape) for w in weights] + cast_in_specs,
        out_specs=out_specs + cast_out_specs,
        out_shape=out_shape + cast_out_shape,
    )(x, gkv, gpre, *weights, *[a for a, _ in cast])


def _bias_kernel(g_ref, *o_refs, n_q, n_k, splits, band, keys_on_rows):
    n_rows, shift = (n_k, n_q) if keys_on_rows else (n_q, BIAS_PERIOD - (n_q - 1))
    for hh in range(2):
        x = jnp.broadcast_to(g_ref[hh:hh + 1, :], (n_rows, BIAS_PERIOD))
        t = pltpu.roll(x, shift, 1, stride=1, stride_axis=0)
        if band:
            a = lax.broadcasted_iota(jnp.int32, (n_rows, BIAS_PERIOD), 0)
            b = lax.broadcasted_iota(jnp.int32, (n_rows, BIAS_PERIOD), 1)
            r, w = (b, a) if keys_on_rows else (a, b)
            j = w - (r - (r & (CHUNK - 1)))
            t = jnp.where((j >= 0) & (j < BAND_PAST + CHUNK), t, NEG)
        for o_ref, (lo, hi) in zip(o_refs, splits):
            o_ref[hh * n_rows:(hh + 1) * n_rows, :] = t[:, lo:hi]


def _rel_bias_tables(rel_bias, n_q, n_k, splits, *, band, keys_on_rows):
    rel_bias = rel_bias - rel_bias[:, -1:]
    c0 = n_k - 1
    far = jnp.broadcast_to(rel_bias[:, -1:], (B_HEADS, c0 - REL_CLIP))
    near = jnp.broadcast_to(rel_bias[:, :1], (B_HEADS, BIAS_PERIOD - (c0 - REL_CLIP) - (2 * REL_CLIP + 1)))
    gen = [near, rel_bias, far] if keys_on_rows else [far, rel_bias[:, ::-1], near]
    gen = jnp.concatenate(gen, axis=1).reshape(HEAD_PAIRS, 2, BIAS_PERIOD)
    n_rows = n_k if keys_on_rows else n_q
    return _call(
        functools.partial(_bias_kernel, n_q=n_q, n_k=n_k, splits=splits, band=band,
                          keys_on_rows=keys_on_rows), "rel_bias",
        grid=(HEAD_PAIRS,),
        in_specs=[pl.BlockSpec((None, 2, BIAS_PERIOD), lambda p: (p, 0, 0))],
        out_specs=[pl.BlockSpec((None, 2 * n_rows, hi - lo), lambda p: (p, 0, 0)) for lo, hi in splits],
        out_shape=[jax.ShapeDtypeStruct((HEAD_PAIRS, 2 * n_rows, hi - lo), F32) for lo, hi in splits],
    )(gen)


def _band_block_kinds():
    n_k = KEY_BLOCKS * TILE
    r = np.arange(TILE)[None, :]
    w = np.arange(n_k)[:, None]
    j = w - CHUNK * (r // CHUNK)
    ok = (j >= 0) & (j < BAND_PAST + CHUNK)
    plain = ok & (BAND_PAST + r - w >= REL_CLIP)
    kinds = []
    for a in range(n_k // KEY_BLOCK):
        rows = slice(a * KEY_BLOCK, (a + 1) * KEY_BLOCK)
        blocks = [(rows, slice(b * LANES, (b + 1) * LANES)) for b in range(TILE // LANES)]
        kinds.append(["skip" if not ok[blk].any() else "plain" if plain[blk].all() else "biased"
                      for blk in blocks])
    return kinds


def _band_prompt_kernel(x_ref, qt_ref, qm_ref, kp_ref, kc_ref, vtp_ref, vtc_ref,
                        bias_ref, mk_ref, mv_ref, wout_ref, gpost_ref, o_ref):
    i = pl.program_id(0)
    n_k = KEY_BLOCKS * TILE
    kinds = _band_block_kinds()
    w = lax.broadcasted_iota(jnp.int32, (n_k, LANES), 0)
    c = lax.broadcasted_iota(jnp.int32, (n_k, LANES), 1)
    ones_rows = jnp.where(lax.broadcasted_iota(jnp.int32, (BF16_ROWS, n_k), 0) == 0, 1.0, 0.0).astype(BF16)
    row = lax.broadcasted_iota(jnp.int32, (LANES, TILE), 0)
    one_hot_row = jnp.where(row == 0, 1.0, 0.0).astype(BF16)
    lo = row < HEAD_DIM

    def key_step(a, s, vth, bias_rows, state):
        blocks = range(a * KEY_STEP // KEY_BLOCK, (a + 1) * KEY_STEP // KEY_BLOCK)
        es, alphas, active = [], [], []
        for b in range(TILE // LANES):
            cols = slice(b * LANES, (b + 1) * LANES)
            sjs = {}
            for blk in blocks:
                if kinds[blk][b] != "skip":
                    rows = slice(blk * KEY_BLOCK, (blk + 1) * KEY_BLOCK)
                    sjs[blk] = s[rows, cols] + bias_rows(rows, cols) if kinds[blk][b] == "biased" else s[rows, cols]
            active.append(bool(sjs))
            alphas.append(None)
            if not sjs:
                es.append(jnp.zeros((KEY_STEP, LANES), F32))
                continue
            mj = jnp.max(functools.reduce(jnp.maximum, sjs.values()), axis=0, keepdims=True)
            if state[b] is None:
                state[b] = [mj, None]
            else:
                m_new = jnp.maximum(state[b][0], mj)
                alphas[b] = jnp.exp(state[b][0] - m_new)
                state[b][0] = m_new
            es.append(jnp.concatenate(
                [jnp.exp(sjs[blk] - state[b][0]) if blk in sjs else jnp.zeros((KEY_BLOCK, LANES), F32)
                 for blk in blocks], axis=0))
        rows = slice(a * KEY_STEP, (a + 1) * KEY_STEP)
        pv = _dot(vth[:, rows], jnp.concatenate(es, axis=1).astype(BF16))[:OUT_ROWS]
        for b in range(TILE // LANES):
            if active[b]:
                pv_b = pv[:, b * LANES:(b + 1) * LANES]
                state[b][1] = pv_b if alphas[b] is None else state[b][1] * alphas[b] + pv_b

    def tile(t):
        q_rows = slice(t * TILE, (t + 1) * TILE)
        k_rows = slice(t * TILE, t * TILE + n_k)
        first_key_tile = i * BAND_TILES + t - (KEY_BLOCKS - 1)
        pen = jnp.where((w < -first_key_tile * TILE) & (c == 0), NEG, 0.0).astype(BF16)
        outs = []
        for p0 in range(0, HEAD_PAIRS, GROUP_PAIRS):
            heads = []
            for p in range(p0, p0 + GROUP_PAIRS):
                k_win = jnp.concatenate([kp_ref[p], kc_ref[p]], axis=0)[k_rows]
                k_ext = jnp.concatenate([k_win, pen], axis=1)
                vt = jnp.concatenate([vtp_ref[p], vtc_ref[p]], axis=1)[:, k_rows]
                qt = qt_ref[p, :, q_rows]
                zero = jnp.zeros_like(qt)
                for hh in range(2):
                    qh = jnp.where(lo, qt, zero) if hh == 0 else jnp.where(lo, zero, qt)
                    s = _dot(k_ext, jnp.concatenate([qh, one_hot_row], axis=0))
                    vth = jnp.concatenate([vt[hh * HEAD_DIM:(hh + 1) * HEAD_DIM, :], ones_rows], axis=0)
                    bias_rows = functools.partial(
                        lambda rows, cols, p, base: bias_ref[p, base + rows.start:base + rows.stop, cols],
                        p=p, base=hh * n_k)
                    heads.append((s, vth, bias_rows, [None] * (TILE // LANES)))
            for a in range(n_k // KEY_STEP):
                for head in heads:
                    key_step(a, *head)
            outs += [jnp.concatenate([o[:HEAD_DIM] * (1.0 / o[HEAD_DIM:HEAD_DIM + 1]) for _, o in state], axis=1)
                     for *_, state in heads]
            yield
        band = jnp.concatenate(outs, axis=0).T.astype(BF16)
        mo = _mem_attend(qm_ref[q_rows, :], mk_ref[0].astype(BF16), mv_ref[0].astype(BF16))
        cat = jnp.concatenate([band, mo.astype(BF16)], axis=-1)
        o_ref[q_rows, :] = x_ref[q_rows, :] + _rms(_dot(cat, wout_ref[...]), gpost_ref[...])

    _interleave([tile(t) for t in range(BAND_TILES)])


def _band_prompt(x, qt3, qm, k3, vt3, bias, mkt, mvt, wout, gpost):
    n = qm.shape[0]
    rows = BAND_TILES * TILE
    assert rows == BAND_PAST and n % rows == 0 and (x.shape[0] - n) % rows == 0
    off = (x.shape[0] - n) // rows
    past = lambda i: jnp.maximum(i - 1, 0)
    return _call(
        _band_prompt_kernel, "band_prompt",
        grid=(n // rows,),
        in_specs=[pl.BlockSpec((rows, D_MODEL), lambda i: (i + off, 0)),
                  pl.BlockSpec((HEAD_PAIRS, LANES, rows), lambda i: (0, 0, i)),
                  pl.BlockSpec((rows, MEM_W), lambda i: (i, 0)),
                  pl.BlockSpec((HEAD_PAIRS, rows, LANES), lambda i: (0, past(i), 0)),
                  pl.BlockSpec((HEAD_PAIRS, rows, LANES), lambda i: (0, i, 0)),
                  pl.BlockSpec((HEAD_PAIRS, LANES, rows), lambda i: (0, 0, past(i))),
                  pl.BlockSpec((HEAD_PAIRS, LANES, rows), lambda i: (0, 0, i)),
                  _const_spec((HEAD_PAIRS, 2 * KEY_BLOCKS * TILE, TILE)),
                  _mem_spec(mkt, 1, 1), _mem_spec(mvt, 1, 1),
                  _const_spec((B_W + MEM_W, D_MODEL)), _const_spec((1, D_MODEL))],
        out_specs=pl.BlockSpec((rows, D_MODEL), lambda i: (i, 0)),
        out_shape=jax.ShapeDtypeStruct((n, D_MODEL), F32),
    )(x, qt3, qm, k3, k3, vt3, vt3, bias, mkt, mvt, wout, gpost)


def _band_sample_kernel(x_ref, q_ref, qm_ref, kn_ref, vn_ref, ck_ref, cv_ref, bc_ref, bn_ref,
                        mk_ref, mv_ref, wout_ref, gpost_ref, o_ref, *, seqs, seq_len):
    lane = lax.broadcasted_iota(jnp.int32, (1, LANES), 1)
    lo = lane < HEAD_DIM

    def pair_attend(b, p, out):
        rows = slice(b * seq_len, (b + 1) * seq_len)
        hd = slice(p * LANES, (p + 1) * LANES)
        qp = q_ref[p, rows, :]
        zero = jnp.zeros_like(qp)
        qs = jnp.concatenate([jnp.where(lo, qp, zero), jnp.where(lo, zero, qp)], axis=0)
        sc = _dot(qs, ck_ref[b, hd, :].astype(BF16)) + bc_ref[p]
        sn = _dot_nt(qs, kn_ref[p, rows, :]) + bn_ref[p]
        yield
        m = jnp.maximum(_lmax(sc), _lmax(sn))
        ec = jnp.exp(sc - m)
        en = jnp.exp(sn - m)
        l = _lsum(ec) + _lsum(en)
        yield
        o = (_dot_nt(ec.astype(BF16), cv_ref[b, hd, :].astype(BF16))
             + _dot(en.astype(BF16), vn_ref[p, rows, :]))
        yield
        o = o * (1.0 / l)
        out[p] = jnp.where(lo, o[:seq_len], o[seq_len:]).astype(BF16)

    def seq_attend(b, out):
        band = [None] * HEAD_PAIRS
        yield from _round_robin([pair_attend(b, p, band) for p in range(HEAD_PAIRS)])
        rows = slice(b * seq_len, (b + 1) * seq_len)
        mo = _mem_attend(qm_ref[rows, :], mk_ref[b].astype(BF16), mv_ref[b].astype(BF16))
        out[b] = jnp.concatenate(band + [mo.astype(BF16)], axis=-1)

    rows_out = [None] * seqs
    _interleave([seq_attend(b, rows_out) for b in range(seqs)])
    cat = jnp.concatenate(rows_out, axis=0)
    o_ref[...] = x_ref[...] + _rms(_dot(cat, wout_ref[...]), gpost_ref[...])


def _band_sample(x, q3, qm, kn3, vn3, ckt, cvt, bias_c, bias_n, mkt, mvt, wout, gpost, *, seqs=4):
    n = qm.shape[0]
    n_seq, past = ckt.shape[0], ckt.shape[2]
    seq_len = n // n_seq
    rows = seqs * seq_len
    slab_spec = pl.BlockSpec((rows, D_MODEL), lambda i: (i, 0))
    pair_rows_spec = pl.BlockSpec((HEAD_PAIRS, rows, LANES), lambda i: (0, i, 0))
    cache_spec = pl.BlockSpec((seqs, B_W, past), lambda i: (i, 0, 0))
    return _call(
        functools.partial(_band_sample_kernel, seqs=seqs, seq_len=seq_len), "band_sample",
        grid=(n_seq // seqs,),
        in_specs=[slab_spec,
                  pair_rows_spec,
                  pl.BlockSpec((rows, MEM_W), lambda i: (i, 0)),
                  pair_rows_spec, pair_rows_spec,
                  cache_spec, cache_spec,
                  _const_spec((HEAD_PAIRS, 2 * seq_len, past)),
                  _const_spec((HEAD_PAIRS, 2 * seq_len, seq_len)),
                  _mem_spec(mkt, seqs, 1), _mem_spec(mvt, seqs, 1),
                  _const_spec((B_W + MEM_W, D_MODEL)), _const_spec((1, D_MODEL))],
        out_specs=slab_spec,
        out_shape=jax.ShapeDtypeStruct((n, D_MODEL), F32),
    )(x, q3, qm, kn3, vn3, ckt, cvt, bias_c, bias_n, mkt, mvt, wout, gpost)


def _spatial_tile(w_s, b_s, period):
    tril = jnp.tril(jnp.ones((GM_CHUNK, GM_CHUNK), dtype=bool))
    w = jnp.where(tril, w_s, jnp.zeros((), w_s.dtype))[:, :period, :period]
    eye = jnp.eye(TILE // period, dtype=w.dtype)
    s_mat = jnp.einsum("ab,gts->gatbs", eye, w).reshape(GM_GROUPS, TILE, TILE)
    rows = jnp.tile(b_s[:, :period], (1, TILE // period))
    bs = jnp.repeat(rows.T, GM_GW, axis=1)
    return s_mat.astype(BF16), bs


def _heads_last(t, lead):
    pos = t.shape[-1]
    t = t.reshape(lead + (-1, HEAD_DIM, pos))
    nd = len(lead)
    return jnp.transpose(t, tuple(range(nd)) + (nd + 2, nd, nd + 1))


def _positions_last(c):
    nd = c.ndim
    t = jnp.transpose(c, tuple(range(nd - 3)) + (nd - 2, nd - 1, nd - 3))
    return t.reshape(c.shape[:-3] + (c.shape[-2] * c.shape[-1], c.shape[-3]))


def kernel(x_prompt, x_sample, cache_mem_k, cache_mem_v, cache_band_k, cache_band_v, mem_prompt,
           g_mix_pre, g_mix_post, g_ffn_pre, g_ffn_post, g_mem, w_mem_kv,
           w_in_a, g_gm_ln, b_gm_ln, w_spatial, b_spatial, w_out_a,
           g_kv, w_kv, w_in_b, rel_bias, w_out_b, w_ff1, w_ff2):
    seq = x_prompt.shape[1]
    n_seq, seq_len = x_sample.shape[0], x_sample.shape[1]
    past = cache_band_k.shape[1]
    vec = lambda a: a.reshape(1, -1)
    stack = lambda a: a.reshape(DEPTH, 1, -1)

    win_a = w_in_a[0].astype(BF16)
    wout_a = w_out_a[0].astype(BF16)
    wk = w_kv[:, :B_W].astype(BF16)
    wv = w_kv[:, B_W:].astype(BF16)
    wkt, wvt = wk.T, wv.T
    win_b = w_in_b[0].astype(BF16)
    wqt, wqm = win_b[:, :B_W].T, win_b[:, B_W:]
    wout_b = w_out_b[0].astype(BF16)
    ln_g, ln_b = vec(g_gm_ln[0]), vec(b_gm_ln[0])
    gkv, gpre_b = vec(g_kv), vec(g_mix_pre[1])
    gf_pre, gf_post = stack(g_ffn_pre), stack(g_ffn_post)

    n_sample = n_seq * seq_len
    pre_a, post_a = vec(g_mix_pre[0]), vec(g_mix_post[0])

    mem_kt, mem_vt = _memkv(mem_prompt[0], g_mem, jnp.swapaxes(w_mem_kv, 1, 2).astype(BF16))
    s_p, bs_p = _spatial_tile(w_spatial[0], b_spatial[0], GM_CHUNK)
    s_s, bs_s = _spatial_tile(w_spatial[0], b_spatial[0], seq_len)
    cmkt, cmvt = _positions_last(cache_mem_k), _positions_last(cache_mem_v)
    xp, w1, w2 = _mixer_a(x_prompt[0], pre_a, post_a, win_a, ln_g, ln_b, s_p, bs_p, mem_kt, mem_vt, wout_a,
                          rows_per_mem=TILE, emit_v=False, tiles=4, cast=((w_ff1, 0), (w_ff2, 0)))
    xs, v_rows = _mixer_a(x_sample.reshape(n_sample, D_MODEL), pre_a, post_a, win_a, ln_g, ln_b, s_s, bs_s,
                          cmkt, cmvt, wout_a, rows_per_mem=seq_len, emit_v=True)
    x = _ffn(xs, xp, gf_pre, gf_post, w1, w2, 0, split_out=False)

    qt3, qm, k3, vt3, w1, w2 = _proj_b(x, gkv, gpre_b, (wqt, wqm, wk, wvt), mode="prompt", rows=2 * PROJ_ROWS,
                                       row_offset=n_sample, n_rows=seq, cast=((w_ff1, 1), (w_ff2, 1)))
    n_keep = min(BAND_PAST, seq)
    kt_tail, vt_tail = _proj_b(x, gkv, gpre_b, (wkt, wvt), mode="tail", rows=TILE,
                               row_offset=n_sample + seq - n_keep, n_rows=n_keep)
    q3s, qms, kn3, vn3, k_new, v_new = _proj_b(x, gkv, gpre_b, (win_b, wk, wv), mode="sample", rows=512,
                                               n_rows=n_sample, seq_len=seq_len)
    n_k = BAND_PAST + TILE
    bias_p, = _rel_bias_tables(rel_bias[0], TILE, n_k, ((0, TILE),), band=True, keys_on_rows=True)
    bias_c, bias_n = _rel_bias_tables(rel_bias[0], seq_len, past + seq_len,
                                      ((0, past), (past, past + seq_len)), band=False, keys_on_rows=False)
    post_b = vec(g_mix_post[1])
    yp = _band_prompt(x, qt3, qm, k3, vt3, bias_p, mem_kt, mem_vt, wout_b, post_b)
    ys = _band_sample(x, q3s, qms, kn3, vn3, _positions_last(cache_band_k), _positions_last(cache_band_v),
                      bias_c, bias_n, cmkt, cmvt, wout_b, post_b)
    y_sample, y_prompt = _ffn(ys, yp, gf_pre, gf_post, w1, w2, 1, split_out=True)
    y_prompt = y_prompt[None]
    y_sample = y_sample.reshape(n_seq, seq_len, D_MODEL)

    return (y_prompt, y_sample,
            _heads_last(mem_kt, (DEPTH, 1)), _heads_last(mem_vt, (DEPTH, 1)),
            _heads_last(kt_tail, (1,)), _heads_last(vt_tail, (1,)),
            v_rows.reshape(1, n_seq, seq_len, GM_W),
            jnp.swapaxes(k_new, 1, 2), jnp.swapaxes(v_new, 1, 2))
```

```python
import functools

import jax
import jax.numpy as jnp
import numpy as np
from jax import lax
from jax.experimental import pallas as pl
from jax.experimental.pallas import tpu as pltpu

D_MODEL = 1024
DEPTH = 2
CHUNK = 64
HEAD_DIM = 64
GM_CHUNK = 128
GM_GROUPS = 4
GM_W = 768
GM_GW = GM_W // GM_GROUPS
MEM_LEN = 256
MEM_HEADS = 4
MEM_W = MEM_HEADS * HEAD_DIM
B_HEADS = 12
B_W = B_HEADS * HEAD_DIM
BAND_PAST = 512
REL_CLIP = 128
D_FF = 4 * D_MODEL
EPS = 1e-6

LANES = 128
SUBLANES = 8
HEAD_PAIRS = B_W // LANES
Q_SCALE = HEAD_DIM ** -0.5
NEG = -1e30
TILE = 256
KEY_BLOCKS = BAND_PAST // TILE + 1
KEY_STEP = 128
KEY_BLOCK = 128
PROJ_ROWS = 512
BF16_ROWS = 2 * SUBLANES
OUT_ROWS = HEAD_DIM + SUBLANES
BAND_TILES = BAND_PAST // TILE
GROUP_PAIRS = 2
BIAS_PERIOD = 1024
V7X_VMEM_BYTES = 64 * 1024 * 1024
VMEM_LIMIT = V7X_VMEM_BYTES * 7 // 8

BF16 = jnp.bfloat16
F32 = jnp.float32


def _dot(a, b):
    return jnp.dot(a, b, preferred_element_type=F32)


def _dot_nt(a, b):
    return lax.dot_general(a, b, (((1,), (1,)), ((), ())), preferred_element_type=F32)


def _rms(x, g):
    ms = jnp.mean(x * x, axis=-1, keepdims=True)
    return x * lax.rsqrt(ms + EPS) * g


def _lsum(a):
    return jnp.sum(a, axis=-1, keepdims=True)


def _lmax(a):
    return jnp.max(a, axis=-1, keepdims=True)


def _const_spec(shape):
    nd = len(shape)
    return pl.BlockSpec(shape, lambda *_: (0,) * nd, pipeline_mode=pl.Buffered(1))


def _layer_spec(shape, layer):
    nd = len(shape)
    return pl.BlockSpec((None,) + shape, lambda *_: (layer,) + (0,) * nd, pipeline_mode=pl.Buffered(1))


def _without_ref(body, k):
    def wrapped(*refs):
        return body(*refs[:k], *refs[k + 1:])
    return wrapped


def _call(body, name, *, in_specs, into=None, semantics="parallel", **kw):
    params = pltpu.CompilerParams(dimension_semantics=(semantics,), vmem_limit_bytes=VMEM_LIMIT)
    if into is None:
        return pl.pallas_call(body, name=name, in_specs=in_specs, compiler_params=params, **kw)
    n_in = len(in_specs)
    call = pl.pallas_call(_without_ref(body, n_in), name=name,
                          in_specs=[*in_specs, pl.BlockSpec(memory_space=pl.ANY)],
                          input_output_aliases={n_in: 0}, compiler_params=params, **kw)
    return lambda *args: call(*args, into)


def _memkv_kernel(mem_ref, g_ref, wt_ref, kt_ref, vt_ref):
    ht = _dot_nt(wt_ref[...], _rms(mem_ref[...], g_ref[...]).astype(BF16))
    kt_ref[0] = ht[:MEM_W]
    vt_ref[0] = ht[MEM_W:]


def _memkv(mem, g_mem, w_mem_kv_t):
    out = jax.ShapeDtypeStruct((DEPTH, 1, MEM_W, MEM_LEN), F32)
    return _call(
        _memkv_kernel, "mem_kv",
        grid=(DEPTH,),
        in_specs=[
            pl.BlockSpec((MEM_LEN, D_MODEL), lambda l: (0, 0)),
            pl.BlockSpec((None, 1, D_MODEL), lambda l: (l, 0, 0)),
            pl.BlockSpec((None, 2 * MEM_W, D_MODEL), lambda l: (l, 0, 0)),
        ],
        out_specs=[pl.BlockSpec((None, 1, MEM_W, MEM_LEN), lambda l: (l, 0, 0, 0))] * 2,
        out_shape=[out, out],
    )(mem, g_mem.reshape(DEPTH, 1, D_MODEL), w_mem_kv_t)


def _mem_attend(qb, kt, vt):
    r = qb.shape[0]
    lane = lax.broadcasted_iota(jnp.int32, (1, MEM_W), 1)
    masks = [(lane >= h * HEAD_DIM) & (lane < (h + 1) * HEAD_DIM) for h in range(MEM_HEADS)]
    qs = jnp.concatenate([jnp.where(m, qb, jnp.zeros_like(qb)) for m in masks], axis=0)
    s = _dot(qs, kt)
    e = jnp.exp(s - _lmax(s))
    pv = _dot_nt(e.astype(BF16), vt) * (1.0 / _lsum(e))
    out = jnp.where(masks[0], pv[:r], 0.0)
    for h in range(1, MEM_HEADS):
        out = out + jnp.where(masks[h], pv[h * r:(h + 1) * r], 0.0)
    return out


def _gelu(x):
    c1 = float(np.sqrt(2.0 / np.pi))
    c2 = c1 * 0.044715
    half = 0.5 * x
    return half + half * jnp.tanh(x * (c1 + c2 * (x * x)))


def _cast_specs(cast, steps, fill):
    own = lambda i: jnp.maximum(i - fill, 0)
    in_specs, out_specs, out_shape = [], [], []
    for a, layer in cast:
        _, r, c = a.shape
        in_specs.append(pl.BlockSpec((None, r // steps, c), functools.partial(lambda i, l: (l, own(i), 0), l=layer)))
        out_specs.append(pl.BlockSpec((r // steps, c), lambda i: (own(i), 0)))
        out_shape.append(jax.ShapeDtypeStruct((r, c), BF16))
    return in_specs, out_specs, out_shape


def _split_cast_refs(refs, n_cast):
    return refs[:n_cast], refs[n_cast:len(refs) - n_cast], refs[len(refs) - n_cast:]


def _cast_blocks(srcs, dsts, stages=4):
    for src, dst in zip(srcs, dsts):
        step = src.shape[0] // stages
        for k in range(stages):
            dst[k * step:(k + 1) * step, :] = src[k * step:(k + 1) * step, :].astype(dst.dtype)
            yield


def _after_fill(o_ref, fill_steps, body):
    if not fill_steps:
        return body()

    @pl.when(pl.program_id(0) < fill_steps)
    def _():
        o_ref[...] = jnp.zeros(o_ref.shape, o_ref.dtype)

    @pl.when(pl.program_id(0) >= fill_steps)
    def _():
        body()


def _round_robin(stage_lists):
    live = list(stage_lists)
    while live:
        for g in list(live):
            if next(g, StopIteration) is StopIteration:
                live.remove(g)
        yield


def _interleave(stage_lists):
    for _ in _round_robin(stage_lists):
        pass


def _mixer_a_kernel(x_ref, gpre_ref, gpost_ref, win_ref, gln_ref, bln_ref, s_ref, bs_ref,
                    mk_ref, mv_ref, wout_ref, *rest, rows_per_mem, tiles, shared_mem, fill_steps, n_cast):
    cast_in, (o_ref, *v_out), cast_out = _split_cast_refs(rest, n_cast)
    nt = GM_W // LANES
    seqs = TILE // rows_per_mem
    lane = lax.broadcasted_iota(jnp.int32, (1, LANES), 1)
    lo = lane < (GM_GW - LANES)
    inv = 1.0 / GM_GW

    def group_stat(a):
        s0 = _lsum(a[0] + jnp.where(lo, a[1], 0.0)) * inv
        s1 = _lsum(jnp.where(lo, 0.0, a[1]) + a[2]) * inv
        s2 = _lsum(a[3] + jnp.where(lo, a[4], 0.0)) * inv
        s3 = _lsum(jnp.where(lo, 0.0, a[4]) + a[5]) * inv
        return [s0, jnp.where(lo, s0, s1), s1, s2, jnp.where(lo, s2, s3), s3]

    def tile(t):
        rows = slice(t * TILE, (t + 1) * TILE)
        z = _dot(_rms(x_ref[rows, :], gpre_ref[...]).astype(BF16), win_ref[...])
        yield
        u = [_gelu(z[:, j * LANES:(j + 1) * LANES]) for j in range(nt)]
        g = [_gelu(z[:, GM_W + j * LANES:GM_W + (j + 1) * LANES]) for j in range(nt)]
        mu = group_stat(g)
        c = [g[j] - mu[j] for j in range(nt)]
        var = group_stat([cj * cj for cj in c])
        gln = gln_ref[...]
        bln = bln_ref[...]
        vn = [c[j] * lax.rsqrt(var[j] + EPS) * gln[:, j * LANES:(j + 1) * LANES]
              + bln[:, j * LANES:(j + 1) * LANES] for j in range(nt)]
        if v_out:
            v_out[0][rows, :] = jnp.concatenate(vn, axis=-1)
        yield
        vb = [a.astype(BF16) for a in vn]
        win = [(0, 1), (1, 2), (3, 4), (4, 5)]
        m = [_dot(s_ref[k], jnp.concatenate([vb[a], vb[b]], axis=-1)) for k, (a, b) in enumerate(win)]
        mixed = [m[0][:, :LANES], jnp.where(lo, m[0][:, LANES:], m[1][:, :LANES]), m[1][:, LANES:],
                 m[2][:, :LANES], jnp.where(lo, m[2][:, LANES:], m[3][:, :LANES]), m[3][:, LANES:]]
        bs = bs_ref[...]
        gm = [u[j] * (mixed[j] + bs[:, j * LANES:(j + 1) * LANES]) for j in range(nt)]
        yield
        qm = (z[:, 2 * GM_W:] * Q_SCALE).astype(BF16)
        mo = []
        for b in range(seqs):
            r = slice(b * rows_per_mem, (b + 1) * rows_per_mem)
            mi = 0 if shared_mem else t * seqs + b
            mo.append(_mem_attend(qm[r], mk_ref[mi].astype(BF16), mv_ref[mi].astype(BF16)))
        mo = mo[0] if len(mo) == 1 else jnp.concatenate(mo, axis=0)
        yield
        cat = jnp.concatenate([a.astype(BF16) for a in gm] + [mo.astype(BF16)], axis=-1)
        o_ref[rows, :] = x_ref[rows, :] + _rms(_dot(cat, wout_ref[...]), gpost_ref[...])

    def body():
        _interleave([tile(t) for t in range(tiles)] + [_cast_blocks(cast_in, cast_out)])

    _after_fill(o_ref, fill_steps, body)


def _mem_spec(mem, seqs, layer):
    if mem.shape[1] == 1:
        return pl.BlockSpec((None, 1, MEM_W, MEM_LEN), lambda i: (layer, 0, 0, 0))
    return pl.BlockSpec((None, seqs, MEM_W, MEM_LEN), lambda i: (layer, i, 0, 0))


def _mixer_a(x, gpre, gpost, win, gln, bln, s_mat, bs, mkt, mvt, wout, *, rows_per_mem, emit_v, tiles=2,
             slab_rows, slab_offset=0, into=None, cast=()):
    n = x.shape[0]
    blk = tiles * TILE
    seqs = blk // rows_per_mem
    off = slab_offset // blk
    fill, shift = (0, off) if into is not None else (off, 0)
    row = lambda w: pl.BlockSpec((blk, w), lambda i: (jnp.maximum(i - fill, 0), 0))
    out_shape = [jax.ShapeDtypeStruct((slab_rows, D_MODEL), F32)]
    out_specs = [pl.BlockSpec((blk, D_MODEL), lambda i: (i + shift, 0))]
    if emit_v:
        out_shape.append(jax.ShapeDtypeStruct((n, GM_W), F32))
        out_specs.append(row(GM_W))
    cast_in_specs, cast_out_specs, cast_out_shape = _cast_specs(cast, n // blk, fill)
    return _call(
        functools.partial(_mixer_a_kernel, rows_per_mem=rows_per_mem, tiles=tiles,
                          shared_mem=mkt.shape[1] == 1, fill_steps=fill, n_cast=len(cast)), "mixer_a", into=into,
        semantics="arbitrary" if fill and cast else "parallel",
        grid=(n // blk + fill,),
        in_specs=[
            row(D_MODEL),
            _const_spec((1, D_MODEL)), _const_spec((1, D_MODEL)),
            _const_spec((D_MODEL, 2 * GM_W + MEM_W)),
            _const_spec((1, GM_W)), _const_spec((1, GM_W)),
            _const_spec((GM_GROUPS, TILE, TILE)), _const_spec((TILE, GM_W)),
            _mem_spec(mkt, seqs, 0), _mem_spec(mvt, seqs, 0),
            _const_spec((GM_W + MEM_W, D_MODEL)),
            *cast_in_specs,
        ],
        out_specs=out_specs + cast_out_specs,
        out_shape=out_shape + cast_out_shape,
    )(x, gpre, gpost, win, gln, bln, s_mat, bs, mkt, mvt, wout, *[a for a, _ in cast])


def _ffn_kernel(x_ref, gpre_ref, gpost_ref, w1_ref, w2_ref, o_ref, *o_tail, tile_rows, ff_chunk, head_steps):
    def tile(t):
        r = slice(sum(tile_rows[:t]), sum(tile_rows[:t + 1]))
        x = x_ref[r, :]
        xn = _rms(x, gpre_ref[...]).astype(BF16)
        acc = jnp.zeros(x.shape, F32)
        yield
        for c in range(D_FF // ff_chunk):
            h = _dot(xn, w1_ref[:, c * ff_chunk:(c + 1) * ff_chunk])
            h = jnp.square(jnp.maximum(h, 0.0)).astype(BF16)
            acc = acc + _dot(h, w2_ref[c * ff_chunk:(c + 1) * ff_chunk, :])
            yield
        y = x + _rms(acc, gpost_ref[...])
        if not o_tail:
            o_ref[r, :] = y
        else:
            is_head = pl.program_id(0) < head_steps
            o_ref[r, :] = jnp.where(is_head, y, o_ref[r, :])
            o_tail[0][r, :] = y

    if o_tail:
        @pl.when(pl.program_id(0) < head_steps)
        def _():
            o_ref[...] = jnp.zeros(o_ref.shape, F32)

    _interleave([tile(t) for t in range(len(tile_rows))])


def _ffn(x, gpre, gpost, w1, w2, layer, *, tile_rows=(256, 512, 256), ff_chunk=1024, split=None):
    n = x.shape[0]
    blk = sum(tile_rows)
    row = pl.BlockSpec((blk, D_MODEL), lambda i: (i, 0))
    if split is None:
        head_steps, out_specs, out_shape = None, row, jax.ShapeDtypeStruct((n, D_MODEL), F32)
    else:
        head_steps = split // blk
        out_specs = [pl.BlockSpec((blk, D_MODEL), lambda i: (jnp.minimum(i, head_steps - 1), 0)),
                     pl.BlockSpec((blk, D_MODEL), lambda i: (jnp.maximum(i - head_steps, 0), 0))]
        out_shape = [jax.ShapeDtypeStruct((split, D_MODEL), F32), jax.ShapeDtypeStruct((n - split, D_MODEL), F32)]
    return _call(
        functools.partial(_ffn_kernel, tile_rows=tile_rows, ff_chunk=ff_chunk, head_steps=head_steps),
        "ffn", semantics="arbitrary",
        grid=(n // blk,),
        in_specs=[row, _layer_spec((1, D_MODEL), layer), _layer_spec((1, D_MODEL), layer),
                  _const_spec((D_MODEL, D_FF)), _const_spec((D_FF, D_MODEL))],
        out_specs=out_specs,
        out_shape=out_shape,
    )(x, gpre, gpost, w1, w2)


def _proj_b_kernel(x_ref, gkv_ref, gpre_ref, *refs, mode, seq_len, n_weights, n_cast):
    rows = x_ref.shape[0]
    cast_in, outs, cast_out = _split_cast_refs(refs[n_weights:], n_cast)
    refs = (*refs[:n_weights], *outs)

    def normed(r):
        x = x_ref[r, :]
        xh = x * lax.rsqrt(jnp.mean(x * x, axis=-1, keepdims=True) + EPS)
        return (xh * gkv_ref[...]).astype(BF16), (xh * gpre_ref[...]).astype(BF16)

    if mode == "tail":
        wkt_ref, wvt_ref, kt_ref, vt_ref = refs
        xkv, _ = normed(slice(None))
        kt_ref[...] = _dot_nt(wkt_ref[...], xkv)
        vt_ref[...] = _dot_nt(wvt_ref[...], xkv)
        return
    if mode == "prompt":
        wqt_ref, wqm_ref, wk_ref, wvt_ref, qt_ref, qm_ref, k_ref, vt_ref = refs

        def tile(t):
            r = slice(t * PROJ_ROWS, (t + 1) * PROJ_ROWS)
            xkv, xq = normed(r)
            yield
            qt = _dot_nt(wqt_ref[...], xq) * Q_SCALE
            for p in range(HEAD_PAIRS):
                qt_ref[p, :, r] = qt[p * LANES:(p + 1) * LANES, :].astype(BF16)
            yield
            vt = _dot_nt(wvt_ref[...], xkv)
            for p in range(HEAD_PAIRS):
                vt_ref[p, :, r] = vt[p * LANES:(p + 1) * LANES, :].astype(BF16)
            yield
            k = _dot(xkv, wk_ref[...])
            for p in range(HEAD_PAIRS):
                k_ref[p, r, :] = k[:, p * LANES:(p + 1) * LANES].astype(BF16)
            yield
            qm_ref[r, :] = (_dot(xq, wqm_ref[...]) * Q_SCALE).astype(BF16)

        _interleave([tile(t) for t in range(rows // PROJ_ROWS)] + [_cast_blocks(cast_in, cast_out)])
        return
    win_ref, wk_ref, wv_ref, q_ref, qm_ref, k_ref, v_ref, k4_ref, v4_ref = refs
    xkv, xq = normed(slice(None))
    z = _dot(xq, win_ref[...]) * Q_SCALE
    qm_ref[...] = z[:, B_W:].astype(BF16)
    k = _dot(xkv, wk_ref[...])
    v = _dot(xkv, wv_ref[...])
    for p in range(HEAD_PAIRS):
        cols = slice(p * LANES, (p + 1) * LANES)
        q_ref[p] = z[:, cols].astype(BF16)
        k_ref[p] = k[:, cols].astype(BF16)
        v_ref[p] = v[:, cols].astype(BF16)
    for b in range(rows // seq_len):
        for h in range(B_HEADS):
            r, c = slice(b * seq_len, (b + 1) * seq_len), slice(h * HEAD_DIM, (h + 1) * HEAD_DIM)
            k4_ref[b, h] = k[r, c]
            v4_ref[b, h] = v[r, c]


def _proj_b(x, gkv, gpre, weights, *, mode, rows, row_offset=0, n_rows=None, seq_len=None, cast=()):
    n = x.shape[0] if n_rows is None else n_rows
    off = row_offset // rows
    pair_rows = jax.ShapeDtypeStruct((HEAD_PAIRS, n, LANES), BF16)
    pair_rows_spec = pl.BlockSpec((HEAD_PAIRS, rows, LANES), lambda i: (0, i, 0))
    pair_cols = jax.ShapeDtypeStruct((HEAD_PAIRS, LANES, n), BF16)
    pair_cols_spec = pl.BlockSpec((HEAD_PAIRS, LANES, rows), lambda i: (0, 0, i))
    qm = jax.ShapeDtypeStruct((n, MEM_W), BF16)
    qm_spec = pl.BlockSpec((rows, MEM_W), lambda i: (i, 0))
    if mode == "tail":
        out_shape = [jax.ShapeDtypeStruct((B_W, n), F32)] * 2
        out_specs = [pl.BlockSpec((B_W, rows), lambda i: (0, i))] * 2
    elif mode == "prompt":
        out_shape = [pair_cols, qm, pair_rows, pair_cols]
        out_specs = [pair_cols_spec, qm_spec, pair_rows_spec, pair_cols_spec]
    else:
        seqs = rows // seq_len
        per_head = jax.ShapeDtypeStruct((n // seq_len, B_HEADS, seq_len, HEAD_DIM), F32)
        per_head_spec = pl.BlockSpec((seqs, B_HEADS, seq_len, HEAD_DIM), lambda i: (i, 0, 0, 0))
        out_shape = [pair_rows, qm, pair_rows, pair_rows, per_head, per_head]
        out_specs = [pair_rows_spec, qm_spec, pair_rows_spec, pair_rows_spec, per_head_spec, per_head_spec]
    cast_in_specs, cast_out_specs, cast_out_shape = _cast_specs(cast, n // rows, 0)
    return _call(
        functools.partial(_proj_b_kernel, mode=mode, seq_len=seq_len, n_weights=len(weights),
                          n_cast=len(cast)), "proj_b_" + mode,
        grid=(n // rows,),
        in_specs=[pl.BlockSpec((rows, D_MODEL), lambda i: (i + off, 0)),
                  _const_spec((1, D_MODEL)), _const_spec((1, D_MODEL))]
                 + [_const_spec(w.shape) for w in weights] + cast_in_specs,
        out_specs=out_specs + cast_out_specs,
        out_shape=out_shape + cast_out_shape,
    )(x, gkv, gpre, *weights, *[a for a, _ in cast])


def _bias_kernel(g_ref, *o_refs, n_q, n_k, splits, band, keys_on_rows):
    n_rows, shift = (n_k, n_q) if keys_on_rows else (n_q, BIAS_PERIOD - (n_q - 1))
    for hh in range(2):
        x = jnp.broadcast_to(g_ref[hh:hh + 1, :], (n_rows, BIAS_PERIOD))
        t = pltpu.roll(x, shift, 1, stride=1, stride_axis=0)
        if band:
            a = lax.broadcasted_iota(jnp.int32, (n_rows, BIAS_PERIOD), 0)
            b = lax.broadcasted_iota(jnp.int32, (n_rows, BIAS_PERIOD), 1)
            r, w = (b, a) if keys_on_rows else (a, b)
            j = w - (r - (r & (CHUNK - 1)))
            t = jnp.where((j >= 0) & (j < BAND_PAST + CHUNK), t, NEG)
        for o_ref, (lo, hi) in zip(o_refs, splits):
            o_ref[hh * n_rows:(hh + 1) * n_rows, :] = t[:, lo:hi]


def _rel_bias_tables(rel_bias, n_q, n_k, splits, *, band, keys_on_rows):
    rel_bias = rel_bias - rel_bias[:, -1:]
    c0 = n_k - 1
    far = jnp.broadcast_to(rel_bias[:, -1:], (B_HEADS, c0 - REL_CLIP))
    near = jnp.broadcast_to(rel_bias[:, :1], (B_HEADS, BIAS_PERIOD - (c0 - REL_CLIP) - (2 * REL_CLIP + 1)))
    gen = [near, rel_bias, far] if keys_on_rows else [far, rel_bias[:, ::-1], near]
    gen = jnp.concatenate(gen, axis=1).reshape(HEAD_PAIRS, 2, BIAS_PERIOD)
    n_rows = n_k if keys_on_rows else n_q
    return _call(
        functools.partial(_bias_kernel, n_q=n_q, n_k=n_k, splits=splits, band=band,
                          keys_on_rows=keys_on_rows), "rel_bias",
        grid=(HEAD_PAIRS,),
        in_specs=[pl.BlockSpec((None, 2, BIAS_PERIOD), lambda p: (p, 0, 0))],
        out_specs=[pl.BlockSpec((None, 2 * n_rows, hi - lo), lambda p: (p, 0, 0)) for lo, hi in splits],
        out_shape=[jax.ShapeDtypeStruct((HEAD_PAIRS, 2 * n_rows, hi - lo), F32) for lo, hi in splits],
    )(gen)


def _band_block_kinds():
    n_k = KEY_BLOCKS * TILE
    r = np.arange(TILE)[None, :]
    w = np.arange(n_k)[:, None]
    j = w - CHUNK * (r // CHUNK)
    ok = (j >= 0) & (j < BAND_PAST + CHUNK)
    plain = ok & (BAND_PAST + r - w >= REL_CLIP)
    kinds = []
    for a in range(n_k // KEY_BLOCK):
        rows = slice(a * KEY_BLOCK, (a + 1) * KEY_BLOCK)
        blocks = [(rows, slice(b * LANES, (b + 1) * LANES)) for b in range(TILE // LANES)]
        kinds.append(["skip" if not ok[blk].any() else "plain" if plain[blk].all() else "biased"
                      for blk in blocks])
    return kinds


def _band_prompt_kernel(x_ref, qt_ref, qm_ref, kp_ref, kc_ref, vtp_ref, vtc_ref,
                        bias_ref, mk_ref, mv_ref, wout_ref, gpost_ref, o_ref, *, fill_steps):
    i = pl.program_id(0) - fill_steps
    n_k = KEY_BLOCKS * TILE
    kinds = _band_block_kinds()
    w = lax.broadcasted_iota(jnp.int32, (n_k, LANES), 0)
    c = lax.broadcasted_iota(jnp.int32, (n_k, LANES), 1)
    ones_rows = jnp.where(lax.broadcasted_iota(jnp.int32, (BF16_ROWS, n_k), 0) == 0, 1.0, 0.0).astype(BF16)
    row = lax.broadcasted_iota(jnp.int32, (LANES, TILE), 0)
    one_hot_row = jnp.where(row == 0, 1.0, 0.0).astype(BF16)
    lo = row < HEAD_DIM

    def key_step(a, s, vth, bias_rows, state):
        blocks = range(a * KEY_STEP // KEY_BLOCK, (a + 1) * KEY_STEP // KEY_BLOCK)
        es, alphas, active = [], [], []
        for b in range(TILE // LANES):
            cols = slice(b * LANES, (b + 1) * LANES)
            sjs = {}
            for blk in blocks:
                if kinds[blk][b] != "skip":
                    rows = slice(blk * KEY_BLOCK, (blk + 1) * KEY_BLOCK)
                    sjs[blk] = s[rows, cols] + bias_rows(rows, cols) if kinds[blk][b] == "biased" else s[rows, cols]
            active.append(bool(sjs))
            alphas.append(None)
            if not sjs:
                es.append(jnp.zeros((KEY_STEP, LANES), F32))
                continue
            mj = jnp.max(functools.reduce(jnp.maximum, sjs.values()), axis=0, keepdims=True)
            if state[b] is None:
                state[b] = [mj, None]
            else:
                m_new = jnp.maximum(state[b][0], mj)
                alphas[b] = jnp.exp(state[b][0] - m_new)
                state[b][0] = m_new
            es.append(jnp.concatenate(
                [jnp.exp(sjs[blk] - state[b][0]) if blk in sjs else jnp.zeros((KEY_BLOCK, LANES), F32)
                 for blk in blocks], axis=0))
        rows = slice(a * KEY_STEP, (a + 1) * KEY_STEP)
        pv = _dot(vth[:, rows], jnp.concatenate(es, axis=1).astype(BF16))[:OUT_ROWS]
        for b in range(TILE // LANES):
            if active[b]:
                pv_b = pv[:, b * LANES:(b + 1) * LANES]
                state[b][1] = pv_b if alphas[b] is None else state[b][1] * alphas[b] + pv_b

    def tile(t):
        q_rows = slice(t * TILE, (t + 1) * TILE)
        k_rows = slice(t * TILE, t * TILE + n_k)
        first_key_tile = i * BAND_TILES + t - (KEY_BLOCKS - 1)
        pen = jnp.where((w < -first_key_tile * TILE) & (c == 0), NEG, 0.0).astype(BF16)
        outs = []
        for p0 in range(0, HEAD_PAIRS, GROUP_PAIRS):
            heads = []
            for p in range(p0, p0 + GROUP_PAIRS):
                k_win = jnp.concatenate([kp_ref[p], kc_ref[p]], axis=0)[k_rows]
                k_ext = jnp.concatenate([k_win, pen], axis=1)
                vt = jnp.concatenate([vtp_ref[p], vtc_ref[p]], axis=1)[:, k_rows]
                qt = qt_ref[p, :, q_rows]
                zero = jnp.zeros_like(qt)
                for hh in range(2):
                    qh = jnp.where(lo, qt, zero) if hh == 0 else jnp.where(lo, zero, qt)
                    s = _dot(k_ext, jnp.concatenate([qh, one_hot_row], axis=0))
                    vth = jnp.concatenate([vt[hh * HEAD_DIM:(hh + 1) * HEAD_DIM, :], ones_rows], axis=0)
                    bias_rows = functools.partial(
                        lambda rows, cols, p, base: bias_ref[p, base + rows.start:base + rows.stop, cols],
                        p=p, base=hh * n_k)
                    heads.append((s, vth, bias_rows, [None] * (TILE // LANES)))
            for a in range(n_k // KEY_STEP):
                for head in heads:
                    key_step(a, *head)
            outs += [jnp.concatenate([o[:HEAD_DIM] * (1.0 / o[HEAD_DIM:HEAD_DIM + 1]) for _, o in state], axis=1)
                     for *_, state in heads]
            yield
        band = jnp.concatenate(outs, axis=0).T.astype(BF16)
        mo = _mem_attend(qm_ref[q_rows, :], mk_ref[0].astype(BF16), mv_ref[0].astype(BF16))
        cat = jnp.concatenate([band, mo.astype(BF16)], axis=-1)
        o_ref[q_rows, :] = x_ref[q_rows, :] + _rms(_dot(cat, wout_ref[...]), gpost_ref[...])

    _after_fill(o_ref, fill_steps, lambda: _interleave([tile(t) for t in range(BAND_TILES)]))


def _band_prompt(x, qt3, qm, k3, vt3, bias, mkt, mvt, wout, gpost):
    n = qm.shape[0]
    rows = BAND_TILES * TILE
    assert rows == BAND_PAST and n % rows == 0 and (x.shape[0] - n) % rows == 0
    fill = (x.shape[0] - n) // rows
    slab_spec = pl.BlockSpec((rows, D_MODEL), lambda i: (i, 0))
    own = lambda i: jnp.maximum(i - fill, 0)
    past = lambda i: jnp.maximum(i - fill - 1, 0)
    return _call(
        functools.partial(_band_prompt_kernel, fill_steps=fill), "band_prompt",
        grid=(n // rows + fill,),
        in_specs=[slab_spec,
                  pl.BlockSpec((HEAD_PAIRS, LANES, rows), lambda i: (0, 0, own(i))),
                  pl.BlockSpec((rows, MEM_W), lambda i: (own(i), 0)),
                  pl.BlockSpec((HEAD_PAIRS, rows, LANES), lambda i: (0, past(i), 0)),
                  pl.BlockSpec((HEAD_PAIRS, rows, LANES), lambda i: (0, own(i), 0)),
                  pl.BlockSpec((HEAD_PAIRS, LANES, rows), lambda i: (0, 0, past(i))),
                  pl.BlockSpec((HEAD_PAIRS, LANES, rows), lambda i: (0, 0, own(i))),
                  _const_spec((HEAD_PAIRS, 2 * KEY_BLOCKS * TILE, TILE)),
                  _mem_spec(mkt, 1, 1), _mem_spec(mvt, 1, 1),
                  _const_spec((B_W + MEM_W, D_MODEL)), _const_spec((1, D_MODEL))],
        out_specs=slab_spec,
        out_shape=jax.ShapeDtypeStruct(x.shape, F32),
    )(x, qt3, qm, k3, k3, vt3, vt3, bias, mkt, mvt, wout, gpost)


def _band_sample_kernel(x_ref, q_ref, qm_ref, kn_ref, vn_ref, ck_ref, cv_ref, bc_ref, bn_ref,
                        mk_ref, mv_ref, wout_ref, gpost_ref, o_ref, *, seqs, seq_len):
    lane = lax.broadcasted_iota(jnp.int32, (1, LANES), 1)
    lo = lane < HEAD_DIM

    def pair_attend(b, p, out):
        rows = slice(b * seq_len, (b + 1) * seq_len)
        hd = slice(p * LANES, (p + 1) * LANES)
        qp = q_ref[p, rows, :]
        zero = jnp.zeros_like(qp)
        qs = jnp.concatenate([jnp.where(lo, qp, zero), jnp.where(lo, zero, qp)], axis=0)
        sc = _dot(qs, ck_ref[b, hd, :].astype(BF16)) + bc_ref[p]
        sn = _dot_nt(qs, kn_ref[p, rows, :]) + bn_ref[p]
        yield
        m = jnp.maximum(_lmax(sc), _lmax(sn))
        ec = jnp.exp(sc - m)
        en = jnp.exp(sn - m)
        l = _lsum(ec) + _lsum(en)
        yield
        o = (_dot_nt(ec.astype(BF16), cv_ref[b, hd, :].astype(BF16))
             + _dot(en.astype(BF16), vn_ref[p, rows, :]))
        yield
        o = o * (1.0 / l)
        out[p] = jnp.where(lo, o[:seq_len], o[seq_len:]).astype(BF16)

    def seq_attend(b, out):
        band = [None] * HEAD_PAIRS
        yield from _round_robin([pair_attend(b, p, band) for p in range(HEAD_PAIRS)])
        rows = slice(b * seq_len, (b + 1) * seq_len)
        mo = _mem_attend(qm_ref[rows, :], mk_ref[b].astype(BF16), mv_ref[b].astype(BF16))
        out[b] = jnp.concatenate(band + [mo.astype(BF16)], axis=-1)

    rows_out = [None] * seqs
    _interleave([seq_attend(b, rows_out) for b in range(seqs)])
    cat = jnp.concatenate(rows_out, axis=0)
    o_ref[...] = x_ref[...] + _rms(_dot(cat, wout_ref[...]), gpost_ref[...])


def _band_sample(x, q3, qm, kn3, vn3, ckt, cvt, bias_c, bias_n, mkt, mvt, wout, gpost, *, into, seqs=4):
    n = qm.shape[0]
    n_seq, past = ckt.shape[0], ckt.shape[2]
    seq_len = n // n_seq
    rows = seqs * seq_len
    slab_spec = pl.BlockSpec((rows, D_MODEL), lambda i: (i, 0))
    pair_rows_spec = pl.BlockSpec((HEAD_PAIRS, rows, LANES), lambda i: (0, i, 0))
    cache_spec = pl.BlockSpec((seqs, B_W, past), lambda i: (i, 0, 0))
    return _call(
        functools.partial(_band_sample_kernel, seqs=seqs, seq_len=seq_len), "band_sample", into=into,
        grid=(n_seq // seqs,),
        in_specs=[slab_spec,
                  pair_rows_spec,
                  pl.BlockSpec((rows, MEM_W), lambda i: (i, 0)),
                  pair_rows_spec, pair_rows_spec,
                  cache_spec, cache_spec,
                  _const_spec((HEAD_PAIRS, 2 * seq_len, past)),
                  _const_spec((HEAD_PAIRS, 2 * seq_len, seq_len)),
                  _mem_spec(mkt, seqs, 1), _mem_spec(mvt, seqs, 1),
                  _const_spec((B_W + MEM_W, D_MODEL)), _const_spec((1, D_MODEL))],
        out_specs=slab_spec,
        out_shape=jax.ShapeDtypeStruct(x.shape, F32),
    )(x, q3, qm, kn3, vn3, ckt, cvt, bias_c, bias_n, mkt, mvt, wout, gpost)


def _spatial_tile(w_s, b_s, period):
    tril = jnp.tril(jnp.ones((GM_CHUNK, GM_CHUNK), dtype=bool))
    w = jnp.where(tril, w_s, jnp.zeros((), w_s.dtype))[:, :period, :period]
    eye = jnp.eye(TILE // period, dtype=w.dtype)
    s_mat = jnp.einsum("ab,gts->gatbs", eye, w).reshape(GM_GROUPS, TILE, TILE)
    rows = jnp.tile(b_s[:, :period], (1, TILE // period))
    bs = jnp.repeat(rows.T, GM_GW, axis=1)
    return s_mat.astype(BF16), bs


def _heads_last(t, lead):
    pos = t.shape[-1]
    t = t.reshape(lead + (-1, HEAD_DIM, pos))
    nd = len(lead)
    return jnp.transpose(t, tuple(range(nd)) + (nd + 2, nd, nd + 1))


def _positions_last(c):
    nd = c.ndim
    t = jnp.transpose(c, tuple(range(nd - 3)) + (nd - 2, nd - 1, nd - 3))
    return t.reshape(c.shape[:-3] + (c.shape[-2] * c.shape[-1], c.shape[-3]))


def kernel(x_prompt, x_sample, cache_mem_k, cache_mem_v, cache_band_k, cache_band_v, mem_prompt,
           g_mix_pre, g_mix_post, g_ffn_pre, g_ffn_post, g_mem, w_mem_kv,
           w_in_a, g_gm_ln, b_gm_ln, w_spatial, b_spatial, w_out_a,
           g_kv, w_kv, w_in_b, rel_bias, w_out_b, w_ff1, w_ff2):
    seq = x_prompt.shape[1]
    n_seq, seq_len = x_sample.shape[0], x_sample.shape[1]
    past = cache_band_k.shape[1]
    vec = lambda a: a.reshape(1, -1)
    stack = lambda a: a.reshape(DEPTH, 1, -1)

    win_a = w_in_a[0].astype(BF16)
    wout_a = w_out_a[0].astype(BF16)
    wk = w_kv[:, :B_W].astype(BF16)
    wv = w_kv[:, B_W:].astype(BF16)
    wkt, wvt = wk.T, wv.T
    win_b = w_in_b[0].astype(BF16)
    wqt, wqm = win_b[:, :B_W].T, win_b[:, B_W:]
    wout_b = w_out_b[0].astype(BF16)
    ln_g, ln_b = vec(g_gm_ln[0]), vec(b_gm_ln[0])
    gkv, gpre_b = vec(g_kv), vec(g_mix_pre[1])
    gf_pre, gf_post = stack(g_ffn_pre), stack(g_ffn_post)

    n_sample = n_seq * seq_len
    slab_rows = seq + n_sample
    pre_a, post_a = vec(g_mix_pre[0]), vec(g_mix_post[0])

    mem_kt, mem_vt = _memkv(mem_prompt[0], g_mem, jnp.swapaxes(w_mem_kv, 1, 2).astype(BF16))
    s_p, bs_p = _spatial_tile(w_spatial[0], b_spatial[0], GM_CHUNK)
    s_s, bs_s = _spatial_tile(w_spatial[0], b_spatial[0], seq_len)
    cmkt, cmvt = _positions_last(cache_mem_k), _positions_last(cache_mem_v)
    x, w1, w2 = _mixer_a(x_prompt[0], pre_a, post_a, win_a, ln_g, ln_b, s_p, bs_p, mem_kt, mem_vt, wout_a,
                         rows_per_mem=TILE, emit_v=False, tiles=4, slab_rows=slab_rows, slab_offset=n_sample,
                         cast=((w_ff1, 0), (w_ff2, 0)))
    x, v_rows = _mixer_a(x_sample.reshape(n_sample, D_MODEL), pre_a, post_a, win_a, ln_g, ln_b, s_s, bs_s,
                         cmkt, cmvt, wout_a, rows_per_mem=seq_len, emit_v=True,
                         slab_rows=slab_rows, into=x)
    x = _ffn(x, gf_pre, gf_post, w1, w2, 0)

    qt3, qm, k3, vt3, w1, w2 = _proj_b(x, gkv, gpre_b, (wqt, wqm, wk, wvt), mode="prompt", rows=2 * PROJ_ROWS,
                                       row_offset=n_sample, n_rows=seq, cast=((w_ff1, 1), (w_ff2, 1)))
    n_keep = min(BAND_PAST, seq)
    kt_tail, vt_tail = _proj_b(x, gkv, gpre_b, (wkt, wvt), mode="tail", rows=TILE,
                               row_offset=n_sample + seq - n_keep, n_rows=n_keep)
    q3s, qms, kn3, vn3, k_new, v_new = _proj_b(x, gkv, gpre_b, (win_b, wk, wv), mode="sample", rows=512,
                                               n_rows=n_sample, seq_len=seq_len)
    n_k = BAND_PAST + TILE
    bias_p, = _rel_bias_tables(rel_bias[0], TILE, n_k, ((0, TILE),), band=True, keys_on_rows=True)
    bias_c, bias_n = _rel_bias_tables(rel_bias[0], seq_len, past + seq_len,
                                      ((0, past), (past, past + seq_len)), band=False, keys_on_rows=False)
    post_b = vec(g_mix_post[1])
    y = _band_prompt(x, qt3, qm, k3, vt3, bias_p, mem_kt, mem_vt, wout_b, post_b)
    y = _band_sample(x, q3s, qms, kn3, vn3, _positions_last(cache_band_k), _positions_last(cache_band_v),
                     bias_c, bias_n, cmkt, cmvt, wout_b, post_b, into=y)
    y_sample, y_prompt = _ffn(y, gf_pre, gf_post, w1, w2, 1, split=n_sample)
    y_prompt = y_prompt[None]
    y_sample = y_sample.reshape(n_seq, seq_len, D_MODEL)

    return (y_prompt, y_sample,
            _heads_last(mem_kt, (DEPTH, 1)), _heads_last(mem_vt, (DEPTH, 1)),
            _heads_last(kt_tail, (1,)), _heads_last(vt_tail, (1,)),
            v_rows.reshape(1, n_seq, seq_len, GM_W),
            jnp.swapaxes(k_new, 1, 2), jnp.swapaxes(v_new, 1, 2))
```

```python
import functools

import jax
import jax.numpy as jnp
import numpy as np
from jax import lax
from jax.experimental import pallas as pl
from jax.experimental.pallas import tpu as pltpu

D_MODEL = 1024
DEPTH = 2
CHUNK = 64
HEAD_DIM = 64
GM_CHUNK = 128
GM_GROUPS = 4
GM_W = 768
GM_GW = GM_W // GM_GROUPS
MEM_LEN = 256
MEM_HEADS = 4
MEM_W = MEM_HEADS * HEAD_DIM
B_HEADS = 12
B_W = B_HEADS * HEAD_DIM
BAND_PAST = 512
REL_CLIP = 128
D_FF = 4 * D_MODEL
EPS = 1e-6

LANES = 128
SUBLANES = 8
HEAD_PAIRS = B_W // LANES
Q_SCALE = HEAD_DIM ** -0.5
NEG = -1e30
TILE = 256
KEY_BLOCKS = BAND_PAST // TILE + 1
KEY_STEP = 128
KEY_BLOCK = 128
PROJ_ROWS = 512
BF16_ROWS = 2 * SUBLANES
OUT_ROWS = HEAD_DIM + SUBLANES
BAND_TILES = BAND_PAST // TILE
GROUP_PAIRS = 2
BIAS_PERIOD = 1024
V7X_VMEM_BYTES = 64 * 1024 * 1024
VMEM_LIMIT = V7X_VMEM_BYTES * 7 // 8

BF16 = jnp.bfloat16
F32 = jnp.float32


def _dot(a, b):
    return jnp.dot(a, b, preferred_element_type=F32)


def _dot_nt(a, b):
    return lax.dot_general(a, b, (((1,), (1,)), ((), ())), preferred_element_type=F32)


def _rms(x, g):
    ms = jnp.mean(x * x, axis=-1, keepdims=True)
    return x * lax.rsqrt(ms + EPS) * g


def _lsum(a):
    return jnp.sum(a, axis=-1, keepdims=True)


def _lmax(a):
    return jnp.max(a, axis=-1, keepdims=True)


def _const_spec(shape):
    nd = len(shape)
    return pl.BlockSpec(shape, lambda *_: (0,) * nd, pipeline_mode=pl.Buffered(1))


def _layer_spec(shape, layer):
    nd = len(shape)
    return pl.BlockSpec((None,) + shape, lambda *_: (layer,) + (0,) * nd, pipeline_mode=pl.Buffered(1))


def _without_ref(body, k):
    def wrapped(*refs):
        return body(*refs[:k], *refs[k + 1:])
    return wrapped


def _call(body, name, *, in_specs, into=None, semantics="parallel", **kw):
    params = pltpu.CompilerParams(dimension_semantics=(semantics,), vmem_limit_bytes=VMEM_LIMIT)
    if into is None:
        return pl.pallas_call(body, name=name, in_specs=in_specs, compiler_params=params, **kw)
    n_in = len(in_specs)
    call = pl.pallas_call(_without_ref(body, n_in), name=name,
                          in_specs=[*in_specs, pl.BlockSpec(memory_space=pl.ANY)],
                          input_output_aliases={n_in: 0}, compiler_params=params, **kw)
    return lambda *args: call(*args, into)


def _memkv_kernel(mem_ref, g_ref, w_ref, kt_ref, vt_ref):
    h = _dot(_rms(mem_ref[...], g_ref[...]).astype(BF16), w_ref[...].astype(BF16))
    kt_ref[0] = h[:, :MEM_W].T
    vt_ref[0] = h[:, MEM_W:].T


def _memkv(mem, g_mem, w_mem_kv):
    out = jax.ShapeDtypeStruct((DEPTH, 1, MEM_W, MEM_LEN), F32)
    return _call(
        _memkv_kernel, "mem_kv",
        grid=(DEPTH,),
        in_specs=[
            pl.BlockSpec((MEM_LEN, D_MODEL), lambda l: (0, 0)),
            pl.BlockSpec((None, 1, D_MODEL), lambda l: (l, 0, 0)),
            pl.BlockSpec((None, D_MODEL, 2 * MEM_W), lambda l: (l, 0, 0)),
        ],
        out_specs=[pl.BlockSpec((None, 1, MEM_W, MEM_LEN), lambda l: (l, 0, 0, 0))] * 2,
        out_shape=[out, out],
    )(mem, g_mem.reshape(DEPTH, 1, D_MODEL), w_mem_kv)


def _mem_attend(qb, kt, vt):
    r = qb.shape[0]
    lane = lax.broadcasted_iota(jnp.int32, (1, MEM_W), 1)
    masks = [(lane >= h * HEAD_DIM) & (lane < (h + 1) * HEAD_DIM) for h in range(MEM_HEADS)]
    qs = jnp.concatenate([jnp.where(m, qb, jnp.zeros_like(qb)) for m in masks], axis=0)
    s = _dot(qs, kt)
    e = jnp.exp(s - _lmax(s))
    pv = _dot_nt(e.astype(BF16), vt) * (1.0 / _lsum(e))
    out = jnp.where(masks[0], pv[:r], 0.0)
    for h in range(1, MEM_HEADS):
        out = out + jnp.where(masks[h], pv[h * r:(h + 1) * r], 0.0)
    return out


def _gelu(x):
    c1 = float(np.sqrt(2.0 / np.pi))
    c2 = c1 * 0.044715
    half = 0.5 * x
    return half + half * jnp.tanh(x * (c1 + c2 * (x * x)))


def _cast_specs(cast, steps, fill):
    own = lambda i: jnp.maximum(i - fill, 0)
    in_specs, out_specs, out_shape = [], [], []
    for a, layer in cast:
        _, r, c = a.shape
        in_specs.append(pl.BlockSpec((None, r // steps, c), functools.partial(lambda i, l: (l, own(i), 0), l=layer)))
        out_specs.append(pl.BlockSpec((r // steps, c), lambda i: (own(i), 0)))
        out_shape.append(jax.ShapeDtypeStruct((r, c), BF16))
    return in_specs, out_specs, out_shape


def _split_cast_refs(refs, n_cast):
    return refs[:n_cast], refs[n_cast:len(refs) - n_cast], refs[len(refs) - n_cast:]


def _cast_blocks(srcs, dsts, stages=4):
    for src, dst in zip(srcs, dsts):
        step = src.shape[0] // stages
        for k in range(stages):
            dst[k * step:(k + 1) * step, :] = src[k * step:(k + 1) * step, :].astype(dst.dtype)
            yield


def _after_fill(o_ref, fill_steps, body):
    if not fill_steps:
        return body()

    @pl.when(pl.program_id(0) < fill_steps)
    def _():
        o_ref[...] = jnp.zeros(o_ref.shape, o_ref.dtype)

    @pl.when(pl.program_id(0) >= fill_steps)
    def _():
        body()


def _round_robin(stage_lists):
    live = list(stage_lists)
    while live:
        for g in list(live):
            if next(g, StopIteration) is StopIteration:
                live.remove(g)
        yield


def _interleave(stage_lists):
    for _ in _round_robin(stage_lists):
        pass


def _mixer_a_kernel(x_ref, gpre_ref, gpost_ref, win_ref, gln_ref, bln_ref, s_ref, bs_ref,
                    mk_ref, mv_ref, wout_ref, *rest, rows_per_mem, tiles, shared_mem, fill_steps, n_cast):
    cast_in, (o_ref, *v_out), cast_out = _split_cast_refs(rest, n_cast)
    nt = GM_W // LANES
    seqs = TILE // rows_per_mem
    lane = lax.broadcasted_iota(jnp.int32, (1, LANES), 1)
    lo = lane < (GM_GW - LANES)
    inv = 1.0 / GM_GW

    def group_stat(a):
        s0 = _lsum(a[0] + jnp.where(lo, a[1], 0.0)) * inv
        s1 = _lsum(jnp.where(lo, 0.0, a[1]) + a[2]) * inv
        s2 = _lsum(a[3] + jnp.where(lo, a[4], 0.0)) * inv
        s3 = _lsum(jnp.where(lo, 0.0, a[4]) + a[5]) * inv
        return [s0, jnp.where(lo, s0, s1), s1, s2, jnp.where(lo, s2, s3), s3]

    def tile(t):
        rows = slice(t * TILE, (t + 1) * TILE)
        z = _dot(_rms(x_ref[rows, :], gpre_ref[...]).astype(BF16), win_ref[...])
        yield
        u = [_gelu(z[:, j * LANES:(j + 1) * LANES]) for j in range(nt)]
        g = [_gelu(z[:, GM_W + j * LANES:GM_W + (j + 1) * LANES]) for j in range(nt)]
        mu = group_stat(g)
        c = [g[j] - mu[j] for j in range(nt)]
        var = group_stat([cj * cj for cj in c])
        gln = gln_ref[...]
        bln = bln_ref[...]
        vn = [c[j] * lax.rsqrt(var[j] + EPS) * gln[:, j * LANES:(j + 1) * LANES]
              + bln[:, j * LANES:(j + 1) * LANES] for j in range(nt)]
        if v_out:
            v_out[0][rows, :] = jnp.concatenate(vn, axis=-1)
        yield
        vb = [a.astype(BF16) for a in vn]
        win = [(0, 1), (1, 2), (3, 4), (4, 5)]
        m = [_dot(s_ref[k], jnp.concatenate([vb[a], vb[b]], axis=-1)) for k, (a, b) in enumerate(win)]
        mixed = [m[0][:, :LANES], jnp.where(lo, m[0][:, LANES:], m[1][:, :LANES]), m[1][:, LANES:],
                 m[2][:, :LANES], jnp.where(lo, m[2][:, LANES:], m[3][:, :LANES]), m[3][:, LANES:]]
        bs = bs_ref[...]
        gm = [u[j] * (mixed[j] + bs[:, j * LANES:(j + 1) * LANES]) for j in range(nt)]
        yield
        qm = (z[:, 2 * GM_W:] * Q_SCALE).astype(BF16)
        mo = []
        for b in range(seqs):
            r = slice(b * rows_per_mem, (b + 1) * rows_per_mem)
            mi = 0 if shared_mem else t * seqs + b
            mo.append(_mem_attend(qm[r], mk_ref[mi].astype(BF16), mv_ref[mi].astype(BF16)))
        mo = mo[0] if len(mo) == 1 else jnp.concatenate(mo, axis=0)
        yield
        cat = jnp.concatenate([a.astype(BF16) for a in gm] + [mo.astype(BF16)], axis=-1)
        o_ref[rows, :] = x_ref[rows, :] + _rms(_dot(cat, wout_ref[...]), gpost_ref[...])

    def body():
        _interleave([tile(t) for t in range(tiles)] + [_cast_blocks(cast_in, cast_out)])

    _after_fill(o_ref, fill_steps, body)


def _mem_spec(mem, seqs, layer):
    if mem.shape[1] == 1:
        return pl.BlockSpec((None, 1, MEM_W, MEM_LEN), lambda i: (layer, 0, 0, 0))
    return pl.BlockSpec((None, seqs, MEM_W, MEM_LEN), lambda i: (layer, i, 0, 0))


def _mixer_a(x, gpre, gpost, win, gln, bln, s_mat, bs, mkt, mvt, wout, *, rows_per_mem, emit_v, tiles=2,
             slab_rows, slab_offset=0, into=None, cast=()):
    n = x.shape[0]
    blk = tiles * TILE
    seqs = blk // rows_per_mem
    off = slab_offset // blk
    fill, shift = (0, off) if into is not None else (off, 0)
    row = lambda w: pl.BlockSpec((blk, w), lambda i: (jnp.maximum(i - fill, 0), 0))
    out_shape = [jax.ShapeDtypeStruct((slab_rows, D_MODEL), F32)]
    out_specs = [pl.BlockSpec((blk, D_MODEL), lambda i: (i + shift, 0))]
    if emit_v:
        out_shape.append(jax.ShapeDtypeStruct((n, GM_W), F32))
        out_specs.append(row(GM_W))
    cast_in_specs, cast_out_specs, cast_out_shape = _cast_specs(cast, n // blk, fill)
    return _call(
        functools.partial(_mixer_a_kernel, rows_per_mem=rows_per_mem, tiles=tiles,
                          shared_mem=mkt.shape[1] == 1, fill_steps=fill, n_cast=len(cast)), "mixer_a", into=into,
        semantics="arbitrary" if fill and cast else "parallel",
        grid=(n // blk + fill,),
        in_specs=[
            row(D_MODEL),
            _const_spec((1, D_MODEL)), _const_spec((1, D_MODEL)),
            _const_spec((D_MODEL, 2 * GM_W + MEM_W)),
            _const_spec((1, GM_W)), _const_spec((1, GM_W)),
            _const_spec((GM_GROUPS, TILE, TILE)), _const_spec((TILE, GM_W)),
            _mem_spec(mkt, seqs, 0), _mem_spec(mvt, seqs, 0),
            _const_spec((GM_W + MEM_W, D_MODEL)),
            *cast_in_specs,
        ],
        out_specs=out_specs + cast_out_specs,
        out_shape=out_shape + cast_out_shape,
    )(x, gpre, gpost, win, gln, bln, s_mat, bs, mkt, mvt, wout, *[a for a, _ in cast])


def _ffn_kernel(x_ref, gpre_ref, gpost_ref, w1_ref, w2_ref, o_ref, *o_tail, tile_rows, ff_chunk, head_steps):
    def tile(t):
        r = slice(sum(tile_rows[:t]), sum(tile_rows[:t + 1]))
        x = x_ref[r, :]
        xn = _rms(x, gpre_ref[...]).astype(BF16)
        acc = jnp.zeros(x.shape, F32)
        yield
        for c in range(D_FF // ff_chunk):
            h = _dot(xn, w1_ref[:, c * ff_chunk:(c + 1) * ff_chunk])
            h = jnp.square(jnp.maximum(h, 0.0)).astype(BF16)
            acc = acc + _dot(h, w2_ref[c * ff_chunk:(c + 1) * ff_chunk, :])
            yield
        y = x + _rms(acc, gpost_ref[...])
        if not o_tail:
            o_ref[r, :] = y
        else:
            is_head = pl.program_id(0) < head_steps
            o_ref[r, :] = jnp.where(is_head, y, o_ref[r, :])
            o_tail[0][r, :] = y

    if o_tail:
        @pl.when(pl.program_id(0) < head_steps)
        def _():
            o_ref[...] = jnp.zeros(o_ref.shape, F32)

    _interleave([tile(t) for t in range(len(tile_rows))])


def _ffn(x, gpre, gpost, w1, w2, layer, *, tile_rows=(256, 512, 256), ff_chunk=1024, split=None):
    n = x.shape[0]
    blk = sum(tile_rows)
    row = pl.BlockSpec((blk, D_MODEL), lambda i: (i, 0))
    if split is None:
        head_steps, out_specs, out_shape = None, row, jax.ShapeDtypeStruct((n, D_MODEL), F32)
    else:
        head_steps = split // blk
        out_specs = [pl.BlockSpec((blk, D_MODEL), lambda i: (jnp.minimum(i, head_steps - 1), 0)),
                     pl.BlockSpec((blk, D_MODEL), lambda i: (jnp.maximum(i - head_steps, 0), 0))]
        out_shape = [jax.ShapeDtypeStruct((split, D_MODEL), F32), jax.ShapeDtypeStruct((n - split, D_MODEL), F32)]
    return _call(
        functools.partial(_ffn_kernel, tile_rows=tile_rows, ff_chunk=ff_chunk, head_steps=head_steps),
        "ffn", semantics="arbitrary",
        grid=(n // blk,),
        in_specs=[row, _layer_spec((1, D_MODEL), layer), _layer_spec((1, D_MODEL), layer),
                  _const_spec((D_MODEL, D_FF)), _const_spec((D_FF, D_MODEL))],
        out_specs=out_specs,
        out_shape=out_shape,
    )(x, gpre, gpost, w1, w2)


def _proj_b_kernel(x_ref, gkv_ref, gpre_ref, *refs, mode, seq_len, n_weights, n_cast):
    rows = x_ref.shape[0]
    cast_in, outs, cast_out = _split_cast_refs(refs[n_weights:], n_cast)
    refs = (*refs[:n_weights], *outs)

    def normed(r):
        x = x_ref[r, :]
        xh = x * lax.rsqrt(jnp.mean(x * x, axis=-1, keepdims=True) + EPS)
        return (xh * gkv_ref[...]).astype(BF16), (xh * gpre_ref[...]).astype(BF16)

    if mode == "tail":
        wk_ref, wv_ref, kt_ref, vt_ref = refs
        xkv, _ = normed(slice(None))
        kt_ref[...] = _dot(xkv, wk_ref[...]).T
        vt_ref[...] = _dot(xkv, wv_ref[...]).T
        return
    if mode == "prompt":
        wqt_ref, wqm_ref, wk_ref, wvt_ref, qt_ref, qm_ref, k_ref, vt_ref = refs

        def tile(t):
            r = slice(t * PROJ_ROWS, (t + 1) * PROJ_ROWS)
            xkv, xq = normed(r)
            yield
            qt = _dot_nt(wqt_ref[...], xq) * Q_SCALE
            for p in range(HEAD_PAIRS):
                qt_ref[p, :, r] = qt[p * LANES:(p + 1) * LANES, :].astype(BF16)
            yield
            vt = _dot_nt(wvt_ref[...], xkv)
            for p in range(HEAD_PAIRS):
                vt_ref[p, :, r] = vt[p * LANES:(p + 1) * LANES, :].astype(BF16)
            yield
            k = _dot(xkv, wk_ref[...])
            for p in range(HEAD_PAIRS):
                k_ref[p, r, :] = k[:, p * LANES:(p + 1) * LANES].astype(BF16)
            yield
            qm_ref[r, :] = (_dot(xq, wqm_ref[...]) * Q_SCALE).astype(BF16)

        _interleave([tile(t) for t in range(rows // PROJ_ROWS)] + [_cast_blocks(cast_in, cast_out)])
        return
    win_ref, wk_ref, wv_ref, q_ref, qm_ref, k_ref, v_ref, k4_ref, v4_ref = refs
    xkv, xq = normed(slice(None))
    z = _dot(xq, win_ref[...]) * Q_SCALE
    qm_ref[...] = z[:, B_W:].astype(BF16)
    k = _dot(xkv, wk_ref[...])
    v = _dot(xkv, wv_ref[...])
    for p in range(HEAD_PAIRS):
        cols = slice(p * LANES, (p + 1) * LANES)
        q_ref[p] = z[:, cols].astype(BF16)
        k_ref[p] = k[:, cols].astype(BF16)
        v_ref[p] = v[:, cols].astype(BF16)
    for b in range(rows // seq_len):
        for h in range(B_HEADS):
            r, c = slice(b * seq_len, (b + 1) * seq_len), slice(h * HEAD_DIM, (h + 1) * HEAD_DIM)
            k4_ref[b, h] = k[r, c]
            v4_ref[b, h] = v[r, c]


def _proj_b(x, gkv, gpre, weights, *, mode, rows, row_offset=0, n_rows=None, seq_len=None, cast=()):
    n = x.shape[0] if n_rows is None else n_rows
    off = row_offset // rows
    pair_rows = jax.ShapeDtypeStruct((HEAD_PAIRS, n, LANES), BF16)
    pair_rows_spec = pl.BlockSpec((HEAD_PAIRS, rows, LANES), lambda i: (0, i, 0))
    pair_cols = jax.ShapeDtypeStruct((HEAD_PAIRS, LANES, n), BF16)
    pair_cols_spec = pl.BlockSpec((HEAD_PAIRS, LANES, rows), lambda i: (0, 0, i))
    qm = jax.ShapeDtypeStruct((n, MEM_W), BF16)
    qm_spec = pl.BlockSpec((rows, MEM_W), lambda i: (i, 0))
    if mode == "tail":
        out_shape = [jax.ShapeDtypeStruct((B_W, n), F32)] * 2
        out_specs = [pl.BlockSpec((B_W, rows), lambda i: (0, i))] * 2
    elif mode == "prompt":
        out_shape = [pair_cols, qm, pair_rows, pair_cols]
        out_specs = [pair_cols_spec, qm_spec, pair_rows_spec, pair_cols_spec]
    else:
        seqs = rows // seq_len
        per_head = jax.ShapeDtypeStruct((n // seq_len, B_HEADS, seq_len, HEAD_DIM), F32)
        per_head_spec = pl.BlockSpec((seqs, B_HEADS, seq_len, HEAD_DIM), lambda i: (i, 0, 0, 0))
        out_shape = [pair_rows, qm, pair_rows, pair_rows, per_head, per_head]
        out_specs = [pair_rows_spec, qm_spec, pair_rows_spec, pair_rows_spec, per_head_spec, per_head_spec]
    cast_in_specs, cast_out_specs, cast_out_shape = _cast_specs(cast, n // rows, 0)
    return _call(
        functools.partial(_proj_b_kernel, mode=mode, seq_len=seq_len, n_weights=len(weights),
                          n_cast=len(cast)), "proj_b_" + mode,
        grid=(n // rows,),
        in_specs=[pl.BlockSpec((rows, D_MODEL), lambda i: (i + off, 0)),
                  _const_spec((1, D_MODEL)), _const_spec((1, D_MODEL))]
                 + [_const_spec(w.shape) for w in weights] + cast_in_specs,
        out_specs=out_specs + cast_out_specs,
        out_shape=out_shape + cast_out_shape,
    )(x, gkv, gpre, *weights, *[a for a, _ in cast])


def _bias_kernel(g_ref, *o_refs, n_q, n_k, splits, band, keys_on_rows):
    n_rows, shift = (n_k, n_q) if keys_on_rows else (n_q, BIAS_PERIOD - (n_q - 1))
    for hh in range(2):
        x = jnp.broadcast_to(g_ref[hh:hh + 1, :], (n_rows, BIAS_PERIOD))
        t = pltpu.roll(x, shift, 1, stride=1, stride_axis=0)
        if band:
            a = lax.broadcasted_iota(jnp.int32, (n_rows, BIAS_PERIOD), 0)
            b = lax.broadcasted_iota(jnp.int32, (n_rows, BIAS_PERIOD), 1)
            r, w = (b, a) if keys_on_rows else (a, b)
            j = w - (r - (r & (CHUNK - 1)))
            t = jnp.where((j >= 0) & (j < BAND_PAST + CHUNK), t, NEG)
        for o_ref, (lo, hi) in zip(o_refs, splits):
            o_ref[hh * n_rows:(hh + 1) * n_rows, :] = t[:, lo:hi]


def _rel_bias_tables(rel_bias, n_q, n_k, splits, *, band, keys_on_rows):
    rel_bias = rel_bias - rel_bias[:, -1:]
    c0 = n_k - 1
    far = jnp.broadcast_to(rel_bias[:, -1:], (B_HEADS, c0 - REL_CLIP))
    near = jnp.broadcast_to(rel_bias[:, :1], (B_HEADS, BIAS_PERIOD - (c0 - REL_CLIP) - (2 * REL_CLIP + 1)))
    gen = [near, rel_bias, far] if keys_on_rows else [far, rel_bias[:, ::-1], near]
    gen = jnp.concatenate(gen, axis=1).reshape(HEAD_PAIRS, 2, BIAS_PERIOD)
    n_rows = n_k if keys_on_rows else n_q
    return _call(
        functools.partial(_bias_kernel, n_q=n_q, n_k=n_k, splits=splits, band=band,
                          keys_on_rows=keys_on_rows), "rel_bias",
        grid=(HEAD_PAIRS,),
        in_specs=[pl.BlockSpec((None, 2, BIAS_PERIOD), lambda p: (p, 0, 0))],
        out_specs=[pl.BlockSpec((None, 2 * n_rows, hi - lo), lambda p: (p, 0, 0)) for lo, hi in splits],
        out_shape=[jax.ShapeDtypeStruct((HEAD_PAIRS, 2 * n_rows, hi - lo), F32) for lo, hi in splits],
    )(gen)


def _band_block_kinds():
    n_k = KEY_BLOCKS * TILE
    r = np.arange(TILE)[None, :]
    w = np.arange(n_k)[:, None]
    j = w - CHUNK * (r // CHUNK)
    ok = (j >= 0) & (j < BAND_PAST + CHUNK)
    plain = ok & (BAND_PAST + r - w >= REL_CLIP)
    kinds = []
    for a in range(n_k // KEY_BLOCK):
        rows = slice(a * KEY_BLOCK, (a + 1) * KEY_BLOCK)
        blocks = [(rows, slice(b * LANES, (b + 1) * LANES)) for b in range(TILE // LANES)]
        kinds.append(["skip" if not ok[blk].any() else "plain" if plain[blk].all() else "biased"
                      for blk in blocks])
    return kinds


def _band_prompt_kernel(x_ref, qt_ref, qm_ref, kp_ref, kc_ref, vtp_ref, vtc_ref,
                        bias_ref, mk_ref, mv_ref, wout_ref, gpost_ref, o_ref, *, fill_steps):
    i = pl.program_id(0) - fill_steps
    n_k = KEY_BLOCKS * TILE
    kinds = _band_block_kinds()
    w = lax.broadcasted_iota(jnp.int32, (n_k, LANES), 0)
    c = lax.broadcasted_iota(jnp.int32, (n_k, LANES), 1)
    ones_rows = jnp.where(lax.broadcasted_iota(jnp.int32, (BF16_ROWS, n_k), 0) == 0, 1.0, 0.0).astype(BF16)
    row = lax.broadcasted_iota(jnp.int32, (LANES, TILE), 0)
    one_hot_row = jnp.where(row == 0, 1.0, 0.0).astype(BF16)
    lo = row < HEAD_DIM

    def key_step(a, s, vth, bias_rows, state):
        blocks = range(a * KEY_STEP // KEY_BLOCK, (a + 1) * KEY_STEP // KEY_BLOCK)
        es, alphas, active = [], [], []
        for b in range(TILE // LANES):
            cols = slice(b * LANES, (b + 1) * LANES)
            sjs = {}
            for blk in blocks:
                if kinds[blk][b] != "skip":
                    rows = slice(blk * KEY_BLOCK, (blk + 1) * KEY_BLOCK)
                    sjs[blk] = s[rows, cols] + bias_rows(rows, cols) if kinds[blk][b] == "biased" else s[rows, cols]
            active.append(bool(sjs))
            alphas.append(None)
            if not sjs:
                es.append(jnp.zeros((KEY_STEP, LANES), F32))
                continue
            mj = jnp.max(functools.reduce(jnp.maximum, sjs.values()), axis=0, keepdims=True)
            if state[b] is None:
                state[b] = [mj, None]
            else:
                m_new = jnp.maximum(state[b][0], mj)
                alphas[b] = jnp.exp(state[b][0] - m_new)
                state[b][0] = m_new
            es.append(jnp.concatenate(
                [jnp.exp(sjs[blk] - state[b][0]) if blk in sjs else jnp.zeros((KEY_BLOCK, LANES), F32)
                 for blk in blocks], axis=0))
        rows = slice(a * KEY_STEP, (a + 1) * KEY_STEP)
        pv = _dot(vth[:, rows], jnp.concatenate(es, axis=1).astype(BF16))[:OUT_ROWS]
        for b in range(TILE // LANES):
            if active[b]:
                pv_b = pv[:, b * LANES:(b + 1) * LANES]
                state[b][1] = pv_b if alphas[b] is None else state[b][1] * alphas[b] + pv_b

    def tile(t):
        q_rows = slice(t * TILE, (t + 1) * TILE)
        k_rows = slice(t * TILE, t * TILE + n_k)
        first_key_tile = i * BAND_TILES + t - (KEY_BLOCKS - 1)
        pen = jnp.where((w < -first_key_tile * TILE) & (c == 0), NEG, 0.0).astype(BF16)
        outs = []
        for p0 in range(0, HEAD_PAIRS, GROUP_PAIRS):
            heads = []
            for p in range(p0, p0 + GROUP_PAIRS):
                k_win = jnp.concatenate([kp_ref[p], kc_ref[p]], axis=0)[k_rows]
                k_ext = jnp.concatenate([k_win, pen], axis=1)
                vt = jnp.concatenate([vtp_ref[p], vtc_ref[p]], axis=1)[:, k_rows]
                qt = qt_ref[p, :, q_rows]
                zero = jnp.zeros_like(qt)
                for hh in range(2):
                    qh = jnp.where(lo, qt, zero) if hh == 0 else jnp.where(lo, zero, qt)
                    s = _dot(k_ext, jnp.concatenate([qh, one_hot_row], axis=0))
                    vth = jnp.concatenate([vt[hh * HEAD_DIM:(hh + 1) * HEAD_DIM, :], ones_rows], axis=0)
                    bias_rows = functools.partial(
                        lambda rows, cols, p, base: bias_ref[p, base + rows.start:base + rows.stop, cols],
                        p=p, base=hh * n_k)
                    heads.append((s, vth, bias_rows, [None] * (TILE // LANES)))
            for a in range(n_k // KEY_STEP):
                for head in heads:
                    key_step(a, *head)
            outs += [jnp.concatenate([o[:HEAD_DIM] * (1.0 / o[HEAD_DIM:HEAD_DIM + 1]) for _, o in state], axis=1)
                     for *_, state in heads]
            yield
        band = jnp.concatenate(outs, axis=0).T.astype(BF16)
        mo = _mem_attend(qm_ref[q_rows, :], mk_ref[0].astype(BF16), mv_ref[0].astype(BF16))
        cat = jnp.concatenate([band, mo.astype(BF16)], axis=-1)
        o_ref[q_rows, :] = x_ref[q_rows, :] + _rms(_dot(cat, wout_ref[...]), gpost_ref[...])

    _after_fill(o_ref, fill_steps, lambda: _interleave([tile(t) for t in range(BAND_TILES)]))


def _band_prompt(x, qt3, qm, k3, vt3, bias, mkt, mvt, wout, gpost):
    n = qm.shape[0]
    rows = BAND_TILES * TILE
    assert rows == BAND_PAST and n % rows == 0 and (x.shape[0] - n) % rows == 0
    fill = (x.shape[0] - n) // rows
    slab_spec = pl.BlockSpec((rows, D_MODEL), lambda i: (i, 0))
    own = lambda i: jnp.maximum(i - fill, 0)
    past = lambda i: jnp.maximum(i - fill - 1, 0)
    return _call(
        functools.partial(_band_prompt_kernel, fill_steps=fill), "band_prompt",
        grid=(n // rows + fill,),
        in_specs=[slab_spec,
                  pl.BlockSpec((HEAD_PAIRS, LANES, rows), lambda i: (0, 0, own(i))),
                  pl.BlockSpec((rows, MEM_W), lambda i: (own(i), 0)),
                  pl.BlockSpec((HEAD_PAIRS, rows, LANES), lambda i: (0, past(i), 0)),
                  pl.BlockSpec((HEAD_PAIRS, rows, LANES), lambda i: (0, own(i), 0)),
                  pl.BlockSpec((HEAD_PAIRS, LANES, rows), lambda i: (0, 0, past(i))),
                  pl.BlockSpec((HEAD_PAIRS, LANES, rows), lambda i: (0, 0, own(i))),
                  _const_spec((HEAD_PAIRS, 2 * KEY_BLOCKS * TILE, TILE)),
                  _mem_spec(mkt, 1, 1), _mem_spec(mvt, 1, 1),
                  _const_spec((B_W + MEM_W, D_MODEL)), _const_spec((1, D_MODEL))],
        out_specs=slab_spec,
        out_shape=jax.ShapeDtypeStruct(x.shape, F32),
    )(x, qt3, qm, k3, k3, vt3, vt3, bias, mkt, mvt, wout, gpost)


def _band_sample_kernel(x_ref, q_ref, qm_ref, kn_ref, vn_ref, ck_ref, cv_ref, bc_ref, bn_ref,
                        mk_ref, mv_ref, wout_ref, gpost_ref, o_ref, *, seqs, seq_len):
    lane = lax.broadcasted_iota(jnp.int32, (1, LANES), 1)
    lo = lane < HEAD_DIM

    def pair_attend(b, p, out):
        rows = slice(b * seq_len, (b + 1) * seq_len)
        hd = slice(p * LANES, (p + 1) * LANES)
        qp = q_ref[p, rows, :]
        zero = jnp.zeros_like(qp)
        qs = jnp.concatenate([jnp.where(lo, qp, zero), jnp.where(lo, zero, qp)], axis=0)
        sc = _dot(qs, ck_ref[b, hd, :].astype(BF16)) + bc_ref[p]
        sn = _dot_nt(qs, kn_ref[p, rows, :]) + bn_ref[p]
        yield
        m = jnp.maximum(_lmax(sc), _lmax(sn))
        ec = jnp.exp(sc - m)
        en = jnp.exp(sn - m)
        l = _lsum(ec) + _lsum(en)
        yield
        o = (_dot_nt(ec.astype(BF16), cv_ref[b, hd, :].astype(BF16))
             + _dot(en.astype(BF16), vn_ref[p, rows, :]))
        yield
        o = o * (1.0 / l)
        out[p] = jnp.where(lo, o[:seq_len], o[seq_len:]).astype(BF16)

    def seq_attend(b, out):
        band = [None] * HEAD_PAIRS
        yield from _round_robin([pair_attend(b, p, band) for p in range(HEAD_PAIRS)])
        rows = slice(b * seq_len, (b + 1) * seq_len)
        mo = _mem_attend(qm_ref[rows, :], mk_ref[b].astype(BF16), mv_ref[b].astype(BF16))
        out[b] = jnp.concatenate(band + [mo.astype(BF16)], axis=-1)

    rows_out = [None] * seqs
    _interleave([seq_attend(b, rows_out) for b in range(seqs)])
    cat = jnp.concatenate(rows_out, axis=0)
    o_ref[...] = x_ref[...] + _rms(_dot(cat, wout_ref[...]), gpost_ref[...])


def _band_sample(x, q3, qm, kn3, vn3, ckt, cvt, bias_c, bias_n, mkt, mvt, wout, gpost, *, into, seqs=4):
    n = qm.shape[0]
    n_seq, past = ckt.shape[0], ckt.shape[2]
    seq_len = n // n_seq
    rows = seqs * seq_len
    slab_spec = pl.BlockSpec((rows, D_MODEL), lambda i: (i, 0))
    pair_rows_spec = pl.BlockSpec((HEAD_PAIRS, rows, LANES), lambda i: (0, i, 0))
    cache_spec = pl.BlockSpec((seqs, B_W, past), lambda i: (i, 0, 0))
    return _call(
        functools.partial(_band_sample_kernel, seqs=seqs, seq_len=seq_len), "band_sample", into=into,
        grid=(n_seq // seqs,),
        in_specs=[slab_spec,
                  pair_rows_spec,
                  pl.BlockSpec((rows, MEM_W), lambda i: (i, 0)),
                  pair_rows_spec, pair_rows_spec,
                  cache_spec, cache_spec,
                  _const_spec((HEAD_PAIRS, 2 * seq_len, past)),
                  _const_spec((HEAD_PAIRS, 2 * seq_len, seq_len)),
                  _mem_spec(mkt, seqs, 1), _mem_spec(mvt, seqs, 1),
                  _const_spec((B_W + MEM_W, D_MODEL)), _const_spec((1, D_MODEL))],
        out_specs=slab_spec,
        out_shape=jax.ShapeDtypeStruct(x.shape, F32),
    )(x, q3, qm, kn3, vn3, ckt, cvt, bias_c, bias_n, mkt, mvt, wout, gpost)


def _spatial_tile(w_s, b_s, period):
    tril = jnp.tril(jnp.ones((GM_CHUNK, GM_CHUNK), dtype=bool))
    w = jnp.where(tril, w_s, jnp.zeros((), w_s.dtype))[:, :period, :period]
    eye = jnp.eye(TILE // period, dtype=w.dtype)
    s_mat = jnp.einsum("ab,gts->gatbs", eye, w).reshape(GM_GROUPS, TILE, TILE)
    rows = jnp.tile(b_s[:, :period], (1, TILE // period))
    bs = jnp.repeat(rows.T, GM_GW, axis=1)
    return s_mat.astype(BF16), bs


def _heads_last(t, lead):
    pos = t.shape[-1]
    t = t.reshape(lead + (-1, HEAD_DIM, pos))
    nd = len(lead)
    return jnp.transpose(t, tuple(range(nd)) + (nd + 2, nd, nd + 1))


def _positions_last(c):
    nd = c.ndim
    t = jnp.transpose(c, tuple(range(nd - 3)) + (nd - 2, nd - 1, nd - 3))
    return t.reshape(c.shape[:-3] + (c.shape[-2] * c.shape[-1], c.shape[-3]))


def kernel(x_prompt, x_sample, cache_mem_k, cache_mem_v, cache_band_k, cache_band_v, mem_prompt,
           g_mix_pre, g_mix_post, g_ffn_pre, g_ffn_post, g_mem, w_mem_kv,
           w_in_a, g_gm_ln, b_gm_ln, w_spatial, b_spatial, w_out_a,
           g_kv, w_kv, w_in_b, rel_bias, w_out_b, w_ff1, w_ff2):
    seq = x_prompt.shape[1]
    n_seq, seq_len = x_sample.shape[0], x_sample.shape[1]
    past = cache_band_k.shape[1]
    vec = lambda a: a.reshape(1, -1)
    stack = lambda a: a.reshape(DEPTH, 1, -1)

    win_a = w_in_a[0].astype(BF16)
    wout_a = w_out_a[0].astype(BF16)
    wk = w_kv[:, :B_W].astype(BF16)
    wv = w_kv[:, B_W:].astype(BF16)
    wvt = wv.T
    win_b = w_in_b[0].astype(BF16)
    wqt, wqm = win_b[:, :B_W].T, win_b[:, B_W:]
    wout_b = w_out_b[0].astype(BF16)
    ln_g, ln_b = vec(g_gm_ln[0]), vec(b_gm_ln[0])
    gkv, gpre_b = vec(g_kv), vec(g_mix_pre[1])
    gf_pre, gf_post = stack(g_ffn_pre), stack(g_ffn_post)

    n_sample = n_seq * seq_len
    slab_rows = seq + n_sample
    pre_a, post_a = vec(g_mix_pre[0]), vec(g_mix_post[0])

    mem_kt, mem_vt = _memkv(mem_prompt[0], g_mem, w_mem_kv)
    s_p, bs_p = _spatial_tile(w_spatial[0], b_spatial[0], GM_CHUNK)
    s_s, bs_s = _spatial_tile(w_spatial[0], b_spatial[0], seq_len)
    cmkt, cmvt = _positions_last(cache_mem_k), _positions_last(cache_mem_v)
    x, w1, w2 = _mixer_a(x_prompt[0], pre_a, post_a, win_a, ln_g, ln_b, s_p, bs_p, mem_kt, mem_vt, wout_a,
                         rows_per_mem=TILE, emit_v=False, tiles=4, slab_rows=slab_rows, slab_offset=n_sample,
                         cast=((w_ff1, 0), (w_ff2, 0)))
    x, v_rows = _mixer_a(x_sample.reshape(n_sample, D_MODEL), pre_a, post_a, win_a, ln_g, ln_b, s_s, bs_s,
                         cmkt, cmvt, wout_a, rows_per_mem=seq_len, emit_v=True,
                         slab_rows=slab_rows, into=x)
    x = _ffn(x, gf_pre, gf_post, w1, w2, 0)

    qt3, qm, k3, vt3, w1, w2 = _proj_b(x, gkv, gpre_b, (wqt, wqm, wk, wvt), mode="prompt", rows=2 * PROJ_ROWS,
                                       row_offset=n_sample, n_rows=seq, cast=((w_ff1, 1), (w_ff2, 1)))
    n_keep = min(BAND_PAST, seq)
    kt_tail, vt_tail = _proj_b(x, gkv, gpre_b, (wk, wv), mode="tail", rows=TILE,
                               row_offset=n_sample + seq - n_keep, n_rows=n_keep)
    q3s, qms, kn3, vn3, k_new, v_new = _proj_b(x, gkv, gpre_b, (win_b, wk, wv), mode="sample", rows=512,
                                               n_rows=n_sample, seq_len=seq_len)
    n_k = BAND_PAST + TILE
    bias_p, = _rel_bias_tables(rel_bias[0], TILE, n_k, ((0, TILE),), band=True, keys_on_rows=True)
    bias_c, bias_n = _rel_bias_tables(rel_bias[0], seq_len, past + seq_len,
                                      ((0, past), (past, past + seq_len)), band=False, keys_on_rows=False)
    post_b = vec(g_mix_post[1])
    y = _band_prompt(x, qt3, qm, k3, vt3, bias_p, mem_kt, mem_vt, wout_b, post_b)
    y = _band_sample(x, q3s, qms, kn3, vn3, _positions_last(cache_band_k), _positions_last(cache_band_v),
                     bias_c, bias_n, cmkt, cmvt, wout_b, post_b, into=y)
    y_sample, y_prompt = _ffn(y, gf_pre, gf_post, w1, w2, 1, split=n_sample)
    y_prompt = y_prompt[None]
    y_sample = y_sample.reshape(n_seq, seq_len, D_MODEL)

    return (y_prompt, y_sample,
            _heads_last(mem_kt, (DEPTH, 1)), _heads_last(mem_vt, (DEPTH, 1)),
            _heads_last(kt_tail, (1,)), _heads_last(vt_tail, (1,)),
            v_rows.reshape(1, n_seq, seq_len, GM_W),
            jnp.swapaxes(k_new, 1, 2), jnp.swapaxes(v_new, 1, 2))
```

```python
import functools

import jax
import jax.numpy as jnp
import numpy as np
from jax import lax
from jax.experimental import pallas as pl
from jax.experimental.pallas import tpu as pltpu

D_MODEL = 1024
DEPTH = 2
CHUNK = 64
HEAD_DIM = 64
GM_CHUNK = 128
GM_GROUPS = 4
GM_W = 768
GM_GW = GM_W // GM_GROUPS
MEM_LEN = 256
MEM_HEADS = 4
MEM_W = MEM_HEADS * HEAD_DIM
B_HEADS = 12
B_W = B_HEADS * HEAD_DIM
BAND_PAST = 512
REL_CLIP = 128
D_FF = 4 * D_MODEL
EPS = 1e-6

LANES = 128
SUBLANES = 8
HEAD_PAIRS = B_W // LANES
Q_SCALE = HEAD_DIM ** -0.5
NEG = -1e30
TILE = 256
KEY_BLOCKS = BAND_PAST // TILE + 1
KEY_STEP = 128
KEY_BLOCK = 128
PROJ_ROWS = 512
BF16_ROWS = 2 * SUBLANES
OUT_ROWS = HEAD_DIM + SUBLANES
BAND_TILES = BAND_PAST // TILE
GROUP_PAIRS = 2
BIAS_PERIOD = 1024
V7X_VMEM_BYTES = 64 * 1024 * 1024
VMEM_LIMIT = V7X_VMEM_BYTES * 7 // 8

BF16 = jnp.bfloat16
F32 = jnp.float32


def _dot(a, b):
    return jnp.dot(a, b, preferred_element_type=F32)


def _dot_nt(a, b):
    return lax.dot_general(a, b, (((1,), (1,)), ((), ())), preferred_element_type=F32)


def _rms(x, g):
    ms = jnp.mean(x * x, axis=-1, keepdims=True)
    return x * lax.rsqrt(ms + EPS) * g


def _lsum(a):
    return jnp.sum(a, axis=-1, keepdims=True)


def _lmax(a):
    return jnp.max(a, axis=-1, keepdims=True)


def _const_spec(shape):
    nd = len(shape)
    return pl.BlockSpec(shape, lambda *_: (0,) * nd, pipeline_mode=pl.Buffered(1))


def _layer_spec(shape, layer):
    nd = len(shape)
    return pl.BlockSpec((None,) + shape, lambda *_: (layer,) + (0,) * nd, pipeline_mode=pl.Buffered(1))


def _without_ref(body, k):
    def wrapped(*refs):
        return body(*refs[:k], *refs[k + 1:])
    return wrapped


def _call(body, name, *, in_specs, into=None, semantics="parallel", **kw):
    params = pltpu.CompilerParams(dimension_semantics=(semantics,), vmem_limit_bytes=VMEM_LIMIT)
    if into is None:
        return pl.pallas_call(body, name=name, in_specs=in_specs, compiler_params=params, **kw)
    n_in = len(in_specs)
    call = pl.pallas_call(_without_ref(body, n_in), name=name,
                          in_specs=[*in_specs, pl.BlockSpec(memory_space=pl.ANY)],
                          input_output_aliases={n_in: 0}, compiler_params=params, **kw)
    return lambda *args: call(*args, into)


def _memkv_kernel(mem_ref, g_ref, w_ref, kt_ref, vt_ref):
    h = _dot(_rms(mem_ref[...], g_ref[...]).astype(BF16), w_ref[...].astype(BF16))
    kt_ref[0] = h[:, :MEM_W].T
    vt_ref[0] = h[:, MEM_W:].T


def _memkv(mem, g_mem, w_mem_kv):
    out = jax.ShapeDtypeStruct((DEPTH, 1, MEM_W, MEM_LEN), F32)
    return _call(
        _memkv_kernel, "mem_kv",
        grid=(DEPTH,),
        in_specs=[
            pl.BlockSpec((MEM_LEN, D_MODEL), lambda l: (0, 0)),
            pl.BlockSpec((None, 1, D_MODEL), lambda l: (l, 0, 0)),
            pl.BlockSpec((None, D_MODEL, 2 * MEM_W), lambda l: (l, 0, 0)),
        ],
        out_specs=[pl.BlockSpec((None, 1, MEM_W, MEM_LEN), lambda l: (l, 0, 0, 0))] * 2,
        out_shape=[out, out],
    )(mem, g_mem.reshape(DEPTH, 1, D_MODEL), w_mem_kv)


def _mem_attend(qb, kt, vt):
    r = qb.shape[0]
    lane = lax.broadcasted_iota(jnp.int32, (1, MEM_W), 1)
    masks = [(lane >= h * HEAD_DIM) & (lane < (h + 1) * HEAD_DIM) for h in range(MEM_HEADS)]
    qs = jnp.concatenate([jnp.where(m, qb, jnp.zeros_like(qb)) for m in masks], axis=0)
    s = _dot(qs, kt)
    e = jnp.exp(s - _lmax(s))
    pv = _dot_nt(e.astype(BF16), vt) * (1.0 / _lsum(e))
    out = jnp.where(masks[0], pv[:r], 0.0)
    for h in range(1, MEM_HEADS):
        out = out + jnp.where(masks[h], pv[h * r:(h + 1) * r], 0.0)
    return out


def _gelu(x):
    c1 = float(np.sqrt(2.0 / np.pi))
    c2 = c1 * 0.044715
    half = 0.5 * x
    return half + half * jnp.tanh(x * (c1 + c2 * (x * x)))


def _cast_specs(cast, steps, fill):
    own = lambda i: jnp.maximum(i - fill, 0)
    in_specs, out_specs, out_shape = [], [], []
    for a, layer in cast:
        _, r, c = a.shape
        in_specs.append(pl.BlockSpec((None, r // steps, c), functools.partial(lambda i, l: (l, own(i), 0), l=layer)))
        out_specs.append(pl.BlockSpec((r // steps, c), lambda i: (own(i), 0)))
        out_shape.append(jax.ShapeDtypeStruct((r, c), BF16))
    return in_specs, out_specs, out_shape


def _split_cast_refs(refs, n_cast):
    return refs[:n_cast], refs[n_cast:len(refs) - n_cast], refs[len(refs) - n_cast:]


def _cast_blocks(srcs, dsts, stages=4):
    for src, dst in zip(srcs, dsts):
        step = src.shape[0] // stages
        for k in range(stages):
            dst[k * step:(k + 1) * step, :] = src[k * step:(k + 1) * step, :].astype(dst.dtype)
            yield


def _after_fill(o_ref, fill_steps, body):
    if not fill_steps:
        return body()

    @pl.when(pl.program_id(0) < fill_steps)
    def _():
        o_ref[...] = jnp.zeros(o_ref.shape, o_ref.dtype)

    @pl.when(pl.program_id(0) >= fill_steps)
    def _():
        body()


def _round_robin(stage_lists):
    live = list(stage_lists)
    while live:
        for g in list(live):
            if next(g, StopIteration) is StopIteration:
                live.remove(g)
        yield


def _interleave(stage_lists):
    for _ in _round_robin(stage_lists):
        pass


def _mixer_a_kernel(x_ref, gpre_ref, gpost_ref, win_ref, gln_ref, bln_ref, s_ref, bs_ref,
                    mk_ref, mv_ref, wout_ref, *rest, rows_per_mem, tiles, shared_mem, fill_steps, n_cast):
    cast_in, (o_ref, *v_out), cast_out = _split_cast_refs(rest, n_cast)
    nt = GM_W // LANES
    seqs = TILE // rows_per_mem
    lane = lax.broadcasted_iota(jnp.int32, (1, LANES), 1)
    lo = lane < (GM_GW - LANES)
    inv = 1.0 / GM_GW

    def group_stat(a):
        s0 = _lsum(a[0] + jnp.where(lo, a[1], 0.0)) * inv
        s1 = _lsum(jnp.where(lo, 0.0, a[1]) + a[2]) * inv
        s2 = _lsum(a[3] + jnp.where(lo, a[4], 0.0)) * inv
        s3 = _lsum(jnp.where(lo, 0.0, a[4]) + a[5]) * inv
        return [s0, jnp.where(lo, s0, s1), s1, s2, jnp.where(lo, s2, s3), s3]

    def tile(t):
        rows = slice(t * TILE, (t + 1) * TILE)
        z = _dot(_rms(x_ref[rows, :], gpre_ref[...]).astype(BF16), win_ref[...])
        yield
        u = [_gelu(z[:, j * LANES:(j + 1) * LANES]) for j in range(nt)]
        g = [_gelu(z[:, GM_W + j * LANES:GM_W + (j + 1) * LANES]) for j in range(nt)]
        mu = group_stat(g)
        c = [g[j] - mu[j] for j in range(nt)]
        var = group_stat([cj * cj for cj in c])
        gln = gln_ref[...]
        bln = bln_ref[...]
        vn = [c[j] * lax.rsqrt(var[j] + EPS) * gln[:, j * LANES:(j + 1) * LANES]
              + bln[:, j * LANES:(j + 1) * LANES] for j in range(nt)]
        if v_out:
            v_out[0][rows, :] = jnp.concatenate(vn, axis=-1)
        yield
        vb = [a.astype(BF16) for a in vn]
        win = [(0, 1), (1, 2), (3, 4), (4, 5)]
        m = [_dot(s_ref[k], jnp.concatenate([vb[a], vb[b]], axis=-1)) for k, (a, b) in enumerate(win)]
        mixed = [m[0][:, :LANES], jnp.where(lo, m[0][:, LANES:], m[1][:, :LANES]), m[1][:, LANES:],
                 m[2][:, :LANES], jnp.where(lo, m[2][:, LANES:], m[3][:, :LANES]), m[3][:, LANES:]]
        bs = bs_ref[...]
        gm = [u[j] * (mixed[j] + bs[:, j * LANES:(j + 1) * LANES]) for j in range(nt)]
        yield
        qm = (z[:, 2 * GM_W:] * Q_SCALE).astype(BF16)
        mo = []
        for b in range(seqs):
            r = slice(b * rows_per_mem, (b + 1) * rows_per_mem)
            mi = 0 if shared_mem else t * seqs + b
            mo.append(_mem_attend(qm[r], mk_ref[mi].astype(BF16), mv_ref[mi].astype(BF16)))
        mo = mo[0] if len(mo) == 1 else jnp.concatenate(mo, axis=0)
        yield
        cat = jnp.concatenate([a.astype(BF16) for a in gm] + [mo.astype(BF16)], axis=-1)
        o_ref[rows, :] = x_ref[rows, :] + _rms(_dot(cat, wout_ref[...]), gpost_ref[...])

    def body():
        _interleave([tile(t) for t in range(tiles)] + [_cast_blocks(cast_in, cast_out)])

    _after_fill(o_ref, fill_steps, body)


def _mem_spec(mem, seqs, layer):
    if mem.shape[1] == 1:
        return pl.BlockSpec((None, 1, MEM_W, MEM_LEN), lambda i: (layer, 0, 0, 0))
    return pl.BlockSpec((None, seqs, MEM_W, MEM_LEN), lambda i: (layer, i, 0, 0))


def _mixer_a(x, gpre, gpost, win, gln, bln, s_mat, bs, mkt, mvt, wout, *, rows_per_mem, emit_v, tiles=2,
             slab_rows, slab_offset=0, into=None, cast=()):
    n = x.shape[0]
    blk = tiles * TILE
    seqs = blk // rows_per_mem
    off = slab_offset // blk
    fill, shift = (0, off) if into is not None else (off, 0)
    row = lambda w: pl.BlockSpec((blk, w), lambda i: (jnp.maximum(i - fill, 0), 0))
    out_shape = [jax.ShapeDtypeStruct((slab_rows, D_MODEL), F32)]
    out_specs = [pl.BlockSpec((blk, D_MODEL), lambda i: (i + shift, 0))]
    if emit_v:
        out_shape.append(jax.ShapeDtypeStruct((n, GM_W), F32))
        out_specs.append(row(GM_W))
    cast_in_specs, cast_out_specs, cast_out_shape = _cast_specs(cast, n // blk, fill)
    return _call(
        functools.partial(_mixer_a_kernel, rows_per_mem=rows_per_mem, tiles=tiles,
                          shared_mem=mkt.shape[1] == 1, fill_steps=fill, n_cast=len(cast)), "mixer_a", into=into,
        semantics="arbitrary" if fill and cast else "parallel",
        grid=(n // blk + fill,),
        in_specs=[
            row(D_MODEL),
            _const_spec((1, D_MODEL)), _const_spec((1, D_MODEL)),
            _const_spec((D_MODEL, 2 * GM_W + MEM_W)),
            _const_spec((1, GM_W)), _const_spec((1, GM_W)),
            _const_spec((GM_GROUPS, TILE, TILE)), _const_spec((TILE, GM_W)),
            _mem_spec(mkt, seqs, 0), _mem_spec(mvt, seqs, 0),
            _const_spec((GM_W + MEM_W, D_MODEL)),
            *cast_in_specs,
        ],
        out_specs=out_specs + cast_out_specs,
        out_shape=out_shape + cast_out_shape,
    )(x, gpre, gpost, win, gln, bln, s_mat, bs, mkt, mvt, wout, *[a for a, _ in cast])


def _ffn_kernel(x_ref, gpre_ref, gpost_ref, w1_ref, w2_ref, o_ref, *o_tail, tile_rows, ff_chunk, head_steps):
    def tile(t):
        r = slice(sum(tile_rows[:t]), sum(tile_rows[:t + 1]))
        x = x_ref[r, :]
        xn = _rms(x, gpre_ref[...]).astype(BF16)
        acc = jnp.zeros(x.shape, F32)
        yield
        for c in range(D_FF // ff_chunk):
            h = _dot(xn, w1_ref[:, c * ff_chunk:(c + 1) * ff_chunk])
            h = jnp.square(jnp.maximum(h, 0.0)).astype(BF16)
            acc = acc + _dot(h, w2_ref[c * ff_chunk:(c + 1) * ff_chunk, :])
            yield
        y = x + _rms(acc, gpost_ref[...])
        if not o_tail:
            o_ref[r, :] = y
        else:
            is_head = pl.program_id(0) < head_steps
            o_ref[r, :] = jnp.where(is_head, y, o_ref[r, :])
            o_tail[0][r, :] = y

    if o_tail:
        @pl.when(pl.program_id(0) < head_steps)
        def _():
            o_ref[...] = jnp.zeros(o_ref.shape, F32)

    _interleave([tile(t) for t in range(len(tile_rows))])


def _ffn(x, gpre, gpost, w1, w2, layer, *, tile_rows=(256, 512, 256), ff_chunk=1024, split=None):
    n = x.shape[0]
    blk = sum(tile_rows)
    row = pl.BlockSpec((blk, D_MODEL), lambda i: (i, 0))
    if split is None:
        head_steps, out_specs, out_shape = None, row, jax.ShapeDtypeStruct((n, D_MODEL), F32)
    else:
        head_steps = split // blk
        out_specs = [pl.BlockSpec((blk, D_MODEL), lambda i: (jnp.minimum(i, head_steps - 1), 0)),
                     pl.BlockSpec((blk, D_MODEL), lambda i: (jnp.maximum(i - head_steps, 0), 0))]
        out_shape = [jax.ShapeDtypeStruct((split, D_MODEL), F32), jax.ShapeDtypeStruct((n - split, D_MODEL), F32)]
    return _call(
        functools.partial(_ffn_kernel, tile_rows=tile_rows, ff_chunk=ff_chunk, head_steps=head_steps),
        "ffn", semantics="arbitrary",
        grid=(n // blk,),
        in_specs=[row, _layer_spec((1, D_MODEL), layer), _layer_spec((1, D_MODEL), layer),
                  _const_spec((D_MODEL, D_FF)), _const_spec((D_FF, D_MODEL))],
        out_specs=out_specs,
        out_shape=out_shape,
    )(x, gpre, gpost, w1, w2)


def _proj_b_kernel(x_ref, gkv_ref, gpre_ref, *refs, mode, seq_len, n_weights, n_cast):
    rows = x_ref.shape[0]
    cast_in, outs, cast_out = _split_cast_refs(refs[n_weights:], n_cast)
    refs = (*refs[:n_weights], *outs)

    def normed(r):
        x = x_ref[r, :]
        xh = x * lax.rsqrt(jnp.mean(x * x, axis=-1, keepdims=True) + EPS)
        return (xh * gkv_ref[...]).astype(BF16), (xh * gpre_ref[...]).astype(BF16)

    if mode == "tail":
        wk_ref, wv_ref, kt_ref, vt_ref = refs
        xkv, _ = normed(slice(None))
        kt_ref[...] = _dot(xkv, wk_ref[...]).T
        vt_ref[...] = _dot(xkv, wv_ref[...]).T
        return
    if mode == "prompt":
        win_ref, wk_ref, wv_ref, qt_ref, qm_ref, k_ref, vt_ref = refs

        def tile(t):
            r = slice(t * PROJ_ROWS, (t + 1) * PROJ_ROWS)
            xkv, xq = normed(r)
            yield
            z = _dot(xq, win_ref[...]) * Q_SCALE
            qm_ref[r, :] = z[:, B_W:].astype(BF16)
            for p in range(HEAD_PAIRS):
                qt_ref[p, :, r] = z[:, p * LANES:(p + 1) * LANES].T.astype(BF16)
            yield
            v = _dot(xkv, wv_ref[...])
            for p in range(HEAD_PAIRS):
                vt_ref[p, :, r] = v[:, p * LANES:(p + 1) * LANES].T.astype(BF16)
            yield
            k = _dot(xkv, wk_ref[...])
            for p in range(HEAD_PAIRS):
                k_ref[p, r, :] = k[:, p * LANES:(p + 1) * LANES].astype(BF16)

        _interleave([tile(t) for t in range(rows // PROJ_ROWS)] + [_cast_blocks(cast_in, cast_out)])
        return
    win_ref, wk_ref, wv_ref, q_ref, qm_ref, k_ref, v_ref, k4_ref, v4_ref = refs
    xkv, xq = normed(slice(None))
    z = _dot(xq, win_ref[...]) * Q_SCALE
    qm_ref[...] = z[:, B_W:].astype(BF16)
    k = _dot(xkv, wk_ref[...])
    v = _dot(xkv, wv_ref[...])
    for p in range(HEAD_PAIRS):
        cols = slice(p * LANES, (p + 1) * LANES)
        q_ref[p] = z[:, cols].astype(BF16)
        k_ref[p] = k[:, cols].astype(BF16)
        v_ref[p] = v[:, cols].astype(BF16)
    for b in range(rows // seq_len):
        for h in range(B_HEADS):
            r, c = slice(b * seq_len, (b + 1) * seq_len), slice(h * HEAD_DIM, (h + 1) * HEAD_DIM)
            k4_ref[b, h] = k[r, c]
            v4_ref[b, h] = v[r, c]


def _proj_b(x, gkv, gpre, weights, *, mode, rows, row_offset=0, n_rows=None, seq_len=None, cast=()):
    n = x.shape[0] if n_rows is None else n_rows
    off = row_offset // rows
    pair_rows = jax.ShapeDtypeStruct((HEAD_PAIRS, n, LANES), BF16)
    pair_rows_spec = pl.BlockSpec((HEAD_PAIRS, rows, LANES), lambda i: (0, i, 0))
    pair_cols = jax.ShapeDtypeStruct((HEAD_PAIRS, LANES, n), BF16)
    pair_cols_spec = pl.BlockSpec((HEAD_PAIRS, LANES, rows), lambda i: (0, 0, i))
    qm = jax.ShapeDtypeStruct((n, MEM_W), BF16)
    qm_spec = pl.BlockSpec((rows, MEM_W), lambda i: (i, 0))
    if mode == "tail":
        out_shape = [jax.ShapeDtypeStruct((B_W, n), F32)] * 2
        out_specs = [pl.BlockSpec((B_W, rows), lambda i: (0, i))] * 2
    elif mode == "prompt":
        out_shape = [pair_cols, qm, pair_rows, pair_cols]
        out_specs = [pair_cols_spec, qm_spec, pair_rows_spec, pair_cols_spec]
    else:
        seqs = rows // seq_len
        per_head = jax.ShapeDtypeStruct((n // seq_len, B_HEADS, seq_len, HEAD_DIM), F32)
        per_head_spec = pl.BlockSpec((seqs, B_HEADS, seq_len, HEAD_DIM), lambda i: (i, 0, 0, 0))
        out_shape = [pair_rows, qm, pair_rows, pair_rows, per_head, per_head]
        out_specs = [pair_rows_spec, qm_spec, pair_rows_spec, pair_rows_spec, per_head_spec, per_head_spec]
    cast_in_specs, cast_out_specs, cast_out_shape = _cast_specs(cast, n // rows, 0)
    return _call(
        functools.partial(_proj_b_kernel, mode=mode, seq_len=seq_len, n_weights=len(weights),
                          n_cast=len(cast)), "proj_b_" + mode,
        grid=(n // rows,),
        in_specs=[pl.BlockSpec((rows, D_MODEL), lambda i: (i + off, 0)),
                  _const_spec((1, D_MODEL)), _const_spec((1, D_MODEL))]
                 + [_const_spec(w.shape) for w in weights] + cast_in_specs,
        out_specs=out_specs + cast_out_specs,
        out_shape=out_shape + cast_out_shape,
    )(x, gkv, gpre, *weights, *[a for a, _ in cast])


def _bias_kernel(g_ref, *o_refs, n_q, n_k, splits, band, keys_on_rows):
    n_rows, shift = (n_k, n_q) if keys_on_rows else (n_q, BIAS_PERIOD - (n_q - 1))
    for hh in range(2):
        x = jnp.broadcast_to(g_ref[hh:hh + 1, :], (n_rows, BIAS_PERIOD))
        t = pltpu.roll(x, shift, 1, stride=1, stride_axis=0)
        if band:
            a = lax.broadcasted_iota(jnp.int32, (n_rows, BIAS_PERIOD), 0)
            b = lax.broadcasted_iota(jnp.int32, (n_rows, BIAS_PERIOD), 1)
            r, w = (b, a) if keys_on_rows else (a, b)
            j = w - (r - (r & (CHUNK - 1)))
            t = jnp.where((j >= 0) & (j < BAND_PAST + CHUNK), t, NEG)
        for o_ref, (lo, hi) in zip(o_refs, splits):
            o_ref[hh * n_rows:(hh + 1) * n_rows, :] = t[:, lo:hi]


def _rel_bias_tables(rel_bias, n_q, n_k, splits, *, band, keys_on_rows):
    rel_bias = rel_bias - rel_bias[:, -1:]
    c0 = n_k - 1
    far = jnp.broadcast_to(rel_bias[:, -1:], (B_HEADS, c0 - REL_CLIP))
    near = jnp.broadcast_to(rel_bias[:, :1], (B_HEADS, BIAS_PERIOD - (c0 - REL_CLIP) - (2 * REL_CLIP + 1)))
    gen = [near, rel_bias, far] if keys_on_rows else [far, rel_bias[:, ::-1], near]
    gen = jnp.concatenate(gen, axis=1).reshape(HEAD_PAIRS, 2, BIAS_PERIOD)
    n_rows = n_k if keys_on_rows else n_q
    return _call(
        functools.partial(_bias_kernel, n_q=n_q, n_k=n_k, splits=splits, band=band,
                          keys_on_rows=keys_on_rows), "rel_bias",
        grid=(HEAD_PAIRS,),
        in_specs=[pl.BlockSpec((None, 2, BIAS_PERIOD), lambda p: (p, 0, 0))],
        out_specs=[pl.BlockSpec((None, 2 * n_rows, hi - lo), lambda p: (p, 0, 0)) for lo, hi in splits],
        out_shape=[jax.ShapeDtypeStruct((HEAD_PAIRS, 2 * n_rows, hi - lo), F32) for lo, hi in splits],
    )(gen)


def _band_block_kinds():
    n_k = KEY_BLOCKS * TILE
    r = np.arange(TILE)[None, :]
    w = np.arange(n_k)[:, None]
    j = w - CHUNK * (r // CHUNK)
    ok = (j >= 0) & (j < BAND_PAST + CHUNK)
    plain = ok & (BAND_PAST + r - w >= REL_CLIP)
    kinds = []
    for a in range(n_k // KEY_BLOCK):
        rows = slice(a * KEY_BLOCK, (a + 1) * KEY_BLOCK)
        blocks = [(rows, slice(b * LANES, (b + 1) * LANES)) for b in range(TILE // LANES)]
        kinds.append(["skip" if not ok[blk].any() else "plain" if plain[blk].all() else "biased"
                      for blk in blocks])
    return kinds


def _band_prompt_kernel(x_ref, qt_ref, qm_ref, kp_ref, kc_ref, vtp_ref, vtc_ref,
                        bias_ref, mk_ref, mv_ref, wout_ref, gpost_ref, o_ref, *, fill_steps):
    i = pl.program_id(0) - fill_steps
    n_k = KEY_BLOCKS * TILE
    kinds = _band_block_kinds()
    w = lax.broadcasted_iota(jnp.int32, (n_k, LANES), 0)
    c = lax.broadcasted_iota(jnp.int32, (n_k, LANES), 1)
    ones_rows = jnp.where(lax.broadcasted_iota(jnp.int32, (BF16_ROWS, n_k), 0) == 0, 1.0, 0.0).astype(BF16)
    row = lax.broadcasted_iota(jnp.int32, (LANES, TILE), 0)
    one_hot_row = jnp.where(row == 0, 1.0, 0.0).astype(BF16)
    lo = row < HEAD_DIM

    def key_step(a, s, vth, bias_rows, state):
        blocks = range(a * KEY_STEP // KEY_BLOCK, (a + 1) * KEY_STEP // KEY_BLOCK)
        es, alphas, active = [], [], []
        for b in range(TILE // LANES):
            cols = slice(b * LANES, (b + 1) * LANES)
            sjs = {}
            for blk in blocks:
                if kinds[blk][b] != "skip":
                    rows = slice(blk * KEY_BLOCK, (blk + 1) * KEY_BLOCK)
                    sjs[blk] = s[rows, cols] + bias_rows(rows, cols) if kinds[blk][b] == "biased" else s[rows, cols]
            active.append(bool(sjs))
            alphas.append(None)
            if not sjs:
                es.append(jnp.zeros((KEY_STEP, LANES), F32))
                continue
            mj = jnp.max(functools.reduce(jnp.maximum, sjs.values()), axis=0, keepdims=True)
            if state[b] is None:
                state[b] = [mj, None]
            else:
                m_new = jnp.maximum(state[b][0], mj)
                alphas[b] = jnp.exp(state[b][0] - m_new)
                state[b][0] = m_new
            es.append(jnp.concatenate(
                [jnp.exp(sjs[blk] - state[b][0]) if blk in sjs else jnp.zeros((KEY_BLOCK, LANES), F32)
                 for blk in blocks], axis=0))
        rows = slice(a * KEY_STEP, (a + 1) * KEY_STEP)
        pv = _dot(vth[:, rows], jnp.concatenate(es, axis=1).astype(BF16))[:OUT_ROWS]
        for b in range(TILE // LANES):
            if active[b]:
                pv_b = pv[:, b * LANES:(b + 1) * LANES]
                state[b][1] = pv_b if alphas[b] is None else state[b][1] * alphas[b] + pv_b

    def tile(t):
        q_rows = slice(t * TILE, (t + 1) * TILE)
        k_rows = slice(t * TILE, t * TILE + n_k)
        first_key_tile = i * BAND_TILES + t - (KEY_BLOCKS - 1)
        pen = jnp.where((w < -first_key_tile * TILE) & (c == 0), NEG, 0.0).astype(BF16)
        outs = []
        for p0 in range(0, HEAD_PAIRS, GROUP_PAIRS):
            heads = []
            for p in range(p0, p0 + GROUP_PAIRS):
                k_win = jnp.concatenate([kp_ref[p], kc_ref[p]], axis=0)[k_rows]
                k_ext = jnp.concatenate([k_win, pen], axis=1)
                vt = jnp.concatenate([vtp_ref[p], vtc_ref[p]], axis=1)[:, k_rows]
                qt = qt_ref[p, :, q_rows]
                zero = jnp.zeros_like(qt)
                for hh in range(2):
                    qh = jnp.where(lo, qt, zero) if hh == 0 else jnp.where(lo, zero, qt)
                    s = _dot(k_ext, jnp.concatenate([qh, one_hot_row], axis=0))
                    vth = jnp.concatenate([vt[hh * HEAD_DIM:(hh + 1) * HEAD_DIM, :], ones_rows], axis=0)
                    bias_rows = functools.partial(
                        lambda rows, cols, p, base: bias_ref[p, base + rows.start:base + rows.stop, cols],
                        p=p, base=hh * n_k)
                    heads.append((s, vth, bias_rows, [None] * (TILE // LANES)))
            for a in range(n_k // KEY_STEP):
                for head in heads:
                    key_step(a, *head)
            outs += [jnp.concatenate([o[:HEAD_DIM] * (1.0 / o[HEAD_DIM:HEAD_DIM + 1]) for _, o in state], axis=1)
                     for *_, state in heads]
            yield
        band = jnp.concatenate(outs, axis=0).T.astype(BF16)
        mo = _mem_attend(qm_ref[q_rows, :], mk_ref[0].astype(BF16), mv_ref[0].astype(BF16))
        cat = jnp.concatenate([band, mo.astype(BF16)], axis=-1)
        o_ref[q_rows, :] = x_ref[q_rows, :] + _rms(_dot(cat, wout_ref[...]), gpost_ref[...])

    _after_fill(o_ref, fill_steps, lambda: _interleave([tile(t) for t in range(BAND_TILES)]))


def _band_prompt(x, qt3, qm, k3, vt3, bias, mkt, mvt, wout, gpost):
    n = qm.shape[0]
    rows = BAND_TILES * TILE
    assert rows == BAND_PAST and n % rows == 0 and (x.shape[0] - n) % rows == 0
    fill = (x.shape[0] - n) // rows
    slab_spec = pl.BlockSpec((rows, D_MODEL), lambda i: (i, 0))
    own = lambda i: jnp.maximum(i - fill, 0)
    past = lambda i: jnp.maximum(i - fill - 1, 0)
    return _call(
        functools.partial(_band_prompt_kernel, fill_steps=fill), "band_prompt",
        grid=(n // rows + fill,),
        in_specs=[slab_spec,
                  pl.BlockSpec((HEAD_PAIRS, LANES, rows), lambda i: (0, 0, own(i))),
                  pl.BlockSpec((rows, MEM_W), lambda i: (own(i), 0)),
                  pl.BlockSpec((HEAD_PAIRS, rows, LANES), lambda i: (0, past(i), 0)),
                  pl.BlockSpec((HEAD_PAIRS, rows, LANES), lambda i: (0, own(i), 0)),
                  pl.BlockSpec((HEAD_PAIRS, LANES, rows), lambda i: (0, 0, past(i))),
                  pl.BlockSpec((HEAD_PAIRS, LANES, rows), lambda i: (0, 0, own(i))),
                  _const_spec((HEAD_PAIRS, 2 * KEY_BLOCKS * TILE, TILE)),
                  _mem_spec(mkt, 1, 1), _mem_spec(mvt, 1, 1),
                  _const_spec((B_W + MEM_W, D_MODEL)), _const_spec((1, D_MODEL))],
        out_specs=slab_spec,
        out_shape=jax.ShapeDtypeStruct(x.shape, F32),
    )(x, qt3, qm, k3, k3, vt3, vt3, bias, mkt, mvt, wout, gpost)


def _band_sample_kernel(x_ref, q_ref, qm_ref, kn_ref, vn_ref, ck_ref, cv_ref, bc_ref, bn_ref,
                        mk_ref, mv_ref, wout_ref, gpost_ref, o_ref, *, seqs, seq_len):
    lane = lax.broadcasted_iota(jnp.int32, (1, LANES), 1)
    lo = lane < HEAD_DIM

    def pair_attend(b, p, out):
        rows = slice(b * seq_len, (b + 1) * seq_len)
        hd = slice(p * LANES, (p + 1) * LANES)
        qp = q_ref[p, rows, :]
        zero = jnp.zeros_like(qp)
        qs = jnp.concatenate([jnp.where(lo, qp, zero), jnp.where(lo, zero, qp)], axis=0)
        sc = _dot(qs, ck_ref[b, hd, :].astype(BF16)) + bc_ref[p]
        sn = _dot_nt(qs, kn_ref[p, rows, :]) + bn_ref[p]
        yield
        m = jnp.maximum(_lmax(sc), _lmax(sn))
        ec = jnp.exp(sc - m)
        en = jnp.exp(sn - m)
        l = _lsum(ec) + _lsum(en)
        yield
        o = (_dot_nt(ec.astype(BF16), cv_ref[b, hd, :].astype(BF16))
             + _dot(en.astype(BF16), vn_ref[p, rows, :]))
        yield
        o = o * (1.0 / l)
        out[p] = jnp.where(lo, o[:seq_len], o[seq_len:]).astype(BF16)

    def seq_attend(b, out):
        band = [None] * HEAD_PAIRS
        yield from _round_robin([pair_attend(b, p, band) for p in range(HEAD_PAIRS)])
        rows = slice(b * seq_len, (b + 1) * seq_len)
        mo = _mem_attend(qm_ref[rows, :], mk_ref[b].astype(BF16), mv_ref[b].astype(BF16))
        out[b] = jnp.concatenate(band + [mo.astype(BF16)], axis=-1)

    rows_out = [None] * seqs
    _interleave([seq_attend(b, rows_out) for b in range(seqs)])
    cat = jnp.concatenate(rows_out, axis=0)
    o_ref[...] = x_ref[...] + _rms(_dot(cat, wout_ref[...]), gpost_ref[...])


def _band_sample(x, q3, qm, kn3, vn3, ckt, cvt, bias_c, bias_n, mkt, mvt, wout, gpost, *, into, seqs=4):
    n = qm.shape[0]
    n_seq, past = ckt.shape[0], ckt.shape[2]
    seq_len = n // n_seq
    rows = seqs * seq_len
    slab_spec = pl.BlockSpec((rows, D_MODEL), lambda i: (i, 0))
    pair_rows_spec = pl.BlockSpec((HEAD_PAIRS, rows, LANES), lambda i: (0, i, 0))
    cache_spec = pl.BlockSpec((seqs, B_W, past), lambda i: (i, 0, 0))
    return _call(
        functools.partial(_band_sample_kernel, seqs=seqs, seq_len=seq_len), "band_sample", into=into,
        grid=(n_seq // seqs,),
        in_specs=[slab_spec,
                  pair_rows_spec,
                  pl.BlockSpec((rows, MEM_W), lambda i: (i, 0)),
                  pair_rows_spec, pair_rows_spec,
                  cache_spec, cache_spec,
                  _const_spec((HEAD_PAIRS, 2 * seq_len, past)),
                  _const_spec((HEAD_PAIRS, 2 * seq_len, seq_len)),
                  _mem_spec(mkt, seqs, 1), _mem_spec(mvt, seqs, 1),
                  _const_spec((B_W + MEM_W, D_MODEL)), _const_spec((1, D_MODEL))],
        out_specs=slab_spec,
        out_shape=jax.ShapeDtypeStruct(x.shape, F32),
    )(x, q3, qm, kn3, vn3, ckt, cvt, bias_c, bias_n, mkt, mvt, wout, gpost)


def _spatial_tile(w_s, b_s, period):
    tril = jnp.tril(jnp.ones((GM_CHUNK, GM_CHUNK), dtype=bool))
    w = jnp.where(tril, w_s, jnp.zeros((), w_s.dtype))[:, :period, :period]
    eye = jnp.eye(TILE // period, dtype=w.dtype)
    s_mat = jnp.einsum("ab,gts->gatbs", eye, w).reshape(GM_GROUPS, TILE, TILE)
    rows = jnp.tile(b_s[:, :period], (1, TILE // period))
    bs = jnp.repeat(rows.T, GM_GW, axis=1)
    return s_mat.astype(BF16), bs


def _heads_last(t, lead):
    pos = t.shape[-1]
    t = t.reshape(lead + (-1, HEAD_DIM, pos))
    nd = len(lead)
    return jnp.transpose(t, tuple(range(nd)) + (nd + 2, nd, nd + 1))


def _positions_last(c):
    nd = c.ndim
    t = jnp.transpose(c, tuple(range(nd - 3)) + (nd - 2, nd - 1, nd - 3))
    return t.reshape(c.shape[:-3] + (c.shape[-2] * c.shape[-1], c.shape[-3]))


def kernel(x_prompt, x_sample, cache_mem_k, cache_mem_v, cache_band_k, cache_band_v, mem_prompt,
           g_mix_pre, g_mix_post, g_ffn_pre, g_ffn_post, g_mem, w_mem_kv,
           w_in_a, g_gm_ln, b_gm_ln, w_spatial, b_spatial, w_out_a,
           g_kv, w_kv, w_in_b, rel_bias, w_out_b, w_ff1, w_ff2):
    seq = x_prompt.shape[1]
    n_seq, seq_len = x_sample.shape[0], x_sample.shape[1]
    past = cache_band_k.shape[1]
    vec = lambda a: a.reshape(1, -1)
    stack = lambda a: a.reshape(DEPTH, 1, -1)

    win_a = w_in_a[0].astype(BF16)
    wout_a = w_out_a[0].astype(BF16)
    wk = w_kv[:, :B_W].astype(BF16)
    wv = w_kv[:, B_W:].astype(BF16)
    win_b = w_in_b[0].astype(BF16)
    wout_b = w_out_b[0].astype(BF16)
    ln_g, ln_b = vec(g_gm_ln[0]), vec(b_gm_ln[0])
    gkv, gpre_b = vec(g_kv), vec(g_mix_pre[1])
    gf_pre, gf_post = stack(g_ffn_pre), stack(g_ffn_post)

    n_sample = n_seq * seq_len
    slab_rows = seq + n_sample
    pre_a, post_a = vec(g_mix_pre[0]), vec(g_mix_post[0])

    mem_kt, mem_vt = _memkv(mem_prompt[0], g_mem, w_mem_kv)
    s_p, bs_p = _spatial_tile(w_spatial[0], b_spatial[0], GM_CHUNK)
    s_s, bs_s = _spatial_tile(w_spatial[0], b_spatial[0], seq_len)
    cmkt, cmvt = _positions_last(cache_mem_k), _positions_last(cache_mem_v)
    x, w1, w2 = _mixer_a(x_prompt[0], pre_a, post_a, win_a, ln_g, ln_b, s_p, bs_p, mem_kt, mem_vt, wout_a,
                         rows_per_mem=TILE, emit_v=False, tiles=4, slab_rows=slab_rows, slab_offset=n_sample,
                         cast=((w_ff1, 0), (w_ff2, 0)))
    x, v_rows = _mixer_a(x_sample.reshape(n_sample, D_MODEL), pre_a, post_a, win_a, ln_g, ln_b, s_s, bs_s,
                         cmkt, cmvt, wout_a, rows_per_mem=seq_len, emit_v=True,
                         slab_rows=slab_rows, into=x)
    x = _ffn(x, gf_pre, gf_post, w1, w2, 0)

    qt3, qm, k3, vt3, w1, w2 = _proj_b(x, gkv, gpre_b, (win_b, wk, wv), mode="prompt", rows=2 * PROJ_ROWS,
                                       row_offset=n_sample, n_rows=seq, cast=((w_ff1, 1), (w_ff2, 1)))
    n_keep = min(BAND_PAST, seq)
    kt_tail, vt_tail = _proj_b(x, gkv, gpre_b, (wk, wv), mode="tail", rows=TILE,
                               row_offset=n_sample + seq - n_keep, n_rows=n_keep)
    q3s, qms, kn3, vn3, k_new, v_new = _proj_b(x, gkv, gpre_b, (win_b, wk, wv), mode="sample", rows=512,
                                               n_rows=n_sample, seq_len=seq_len)
    n_k = BAND_PAST + TILE
    bias_p, = _rel_bias_tables(rel_bias[0], TILE, n_k, ((0, TILE),), band=True, keys_on_rows=True)
    bias_c, bias_n = _rel_bias_tables(rel_bias[0], seq_len, past + seq_len,
                                      ((0, past), (past, past + seq_len)), band=False, keys_on_rows=False)
    post_b = vec(g_mix_post[1])
    y = _band_prompt(x, qt3, qm, k3, vt3, bias_p, mem_kt, mem_vt, wout_b, post_b)
    y = _band_sample(x, q3s, qms, kn3, vn3, _positions_last(cache_band_k), _positions_last(cache_band_v),
                     bias_c, bias_n, cmkt, cmvt, wout_b, post_b, into=y)
    y_sample, y_prompt = _ffn(y, gf_pre, gf_post, w1, w2, 1, split=n_sample)
    y_prompt = y_prompt[None]
    y_sample = y_sample.reshape(n_seq, seq_len, D_MODEL)

    return (y_prompt, y_sample,
            _heads_last(mem_kt, (DEPTH, 1)), _heads_last(mem_vt, (DEPTH, 1)),
            _heads_last(kt_tail, (1,)), _heads_last(vt_tail, (1,)),
            v_rows.reshape(1, n_seq, seq_len, GM_W),
            jnp.swapaxes(k_new, 1, 2), jnp.swapaxes(v_new, 1, 2))
```

```python
import functools

import jax
import jax.numpy as jnp
import numpy as np
from jax import lax
from jax.experimental import pallas as pl
from jax.experimental.pallas import tpu as pltpu

D_MODEL = 1024
DEPTH = 2
CHUNK = 64
HEAD_DIM = 64
GM_CHUNK = 128
GM_GROUPS = 4
GM_W = 768
GM_GW = GM_W // GM_GROUPS
MEM_LEN = 256
MEM_HEADS = 4
MEM_W = MEM_HEADS * HEAD_DIM
B_HEADS = 12
B_W = B_HEADS * HEAD_DIM
BAND_PAST = 512
REL_CLIP = 128
D_FF = 4 * D_MODEL
EPS = 1e-6

LANES = 128
SUBLANES = 8
HEAD_PAIRS = B_W // LANES
Q_SCALE = HEAD_DIM ** -0.5
NEG = -1e30
TILE = 256
KEY_BLOCKS = BAND_PAST // TILE + 1
KEY_STEP = 128
KEY_BLOCK = 128
PROJ_ROWS = 512
BF16_ROWS = 2 * SUBLANES
OUT_ROWS = HEAD_DIM + SUBLANES
BAND_TILES = BAND_PAST // TILE
GROUP_PAIRS = 2
BIAS_PERIOD = 1024
V7X_VMEM_BYTES = 64 * 1024 * 1024
VMEM_LIMIT = V7X_VMEM_BYTES * 7 // 8

BF16 = jnp.bfloat16
F32 = jnp.float32


def _dot(a, b):
    return jnp.dot(a, b, preferred_element_type=F32)


def _dot_nt(a, b):
    return lax.dot_general(a, b, (((1,), (1,)), ((), ())), preferred_element_type=F32)


def _rms(x, g):
    ms = jnp.mean(x * x, axis=-1, keepdims=True)
    return x * lax.rsqrt(ms + EPS) * g


def _lsum(a):
    return jnp.sum(a, axis=-1, keepdims=True)


def _lmax(a):
    return jnp.max(a, axis=-1, keepdims=True)


def _const_spec(shape):
    nd = len(shape)
    return pl.BlockSpec(shape, lambda *_: (0,) * nd, pipeline_mode=pl.Buffered(1))


def _layer_spec(shape, layer):
    nd = len(shape)
    return pl.BlockSpec((None,) + shape, lambda *_: (layer,) + (0,) * nd, pipeline_mode=pl.Buffered(1))


def _without_ref(body, k):
    def wrapped(*refs):
        return body(*refs[:k], *refs[k + 1:])
    return wrapped


def _call(body, name, *, in_specs, into=None, semantics="parallel", **kw):
    params = pltpu.CompilerParams(dimension_semantics=(semantics,), vmem_limit_bytes=VMEM_LIMIT)
    if into is None:
        return pl.pallas_call(body, name=name, in_specs=in_specs, compiler_params=params, **kw)
    n_in = len(in_specs)
    call = pl.pallas_call(_without_ref(body, n_in), name=name,
                          in_specs=[*in_specs, pl.BlockSpec(memory_space=pl.ANY)],
                          input_output_aliases={n_in: 0}, compiler_params=params, **kw)
    return lambda *args: call(*args, into)


def _memkv_kernel(mem_ref, g_ref, w_ref, kt_ref, vt_ref):
    h = _dot(_rms(mem_ref[...], g_ref[...]).astype(BF16), w_ref[...].astype(BF16))
    kt_ref[0] = h[:, :MEM_W].T
    vt_ref[0] = h[:, MEM_W:].T


def _memkv(mem, g_mem, w_mem_kv):
    out = jax.ShapeDtypeStruct((DEPTH, 1, MEM_W, MEM_LEN), F32)
    return _call(
        _memkv_kernel, "mem_kv",
        grid=(DEPTH,),
        in_specs=[
            pl.BlockSpec((MEM_LEN, D_MODEL), lambda l: (0, 0)),
            pl.BlockSpec((None, 1, D_MODEL), lambda l: (l, 0, 0)),
            pl.BlockSpec((None, D_MODEL, 2 * MEM_W), lambda l: (l, 0, 0)),
        ],
        out_specs=[pl.BlockSpec((None, 1, MEM_W, MEM_LEN), lambda l: (l, 0, 0, 0))] * 2,
        out_shape=[out, out],
    )(mem, g_mem.reshape(DEPTH, 1, D_MODEL), w_mem_kv)


def _mem_attend(qb, kt, vt):
    r = qb.shape[0]
    lane = lax.broadcasted_iota(jnp.int32, (1, MEM_W), 1)
    masks = [(lane >= h * HEAD_DIM) & (lane < (h + 1) * HEAD_DIM) for h in range(MEM_HEADS)]
    qs = jnp.concatenate([jnp.where(m, qb, jnp.zeros_like(qb)) for m in masks], axis=0)
    s = _dot(qs, kt)
    e = jnp.exp(s - _lmax(s))
    pv = _dot_nt(e.astype(BF16), vt) * (1.0 / _lsum(e))
    out = jnp.where(masks[0], pv[:r], 0.0)
    for h in range(1, MEM_HEADS):
        out = out + jnp.where(masks[h], pv[h * r:(h + 1) * r], 0.0)
    return out


def _gelu(x):
    c1 = float(np.sqrt(2.0 / np.pi))
    c2 = c1 * 0.044715
    half = 0.5 * x
    return half + half * jnp.tanh(x * (c1 + c2 * (x * x)))


def _cast_specs(cast, steps, fill):
    own = lambda i: jnp.maximum(i - fill, 0)
    in_specs, out_specs, out_shape = [], [], []
    for a, layer in cast:
        _, r, c = a.shape
        in_specs.append(pl.BlockSpec((None, r // steps, c), functools.partial(lambda i, l: (l, own(i), 0), l=layer)))
        out_specs.append(pl.BlockSpec((r // steps, c), lambda i: (own(i), 0)))
        out_shape.append(jax.ShapeDtypeStruct((r, c), BF16))
    return in_specs, out_specs, out_shape


def _split_cast_refs(refs, n_cast):
    return refs[:n_cast], refs[n_cast:len(refs) - n_cast], refs[len(refs) - n_cast:]


def _cast_blocks(srcs, dsts, stages=4):
    for src, dst in zip(srcs, dsts):
        step = src.shape[0] // stages
        for k in range(stages):
            dst[k * step:(k + 1) * step, :] = src[k * step:(k + 1) * step, :].astype(dst.dtype)
            yield


def _after_fill(o_ref, fill_steps, body):
    if not fill_steps:
        return body()

    @pl.when(pl.program_id(0) < fill_steps)
    def _():
        o_ref[...] = jnp.zeros(o_ref.shape, o_ref.dtype)

    @pl.when(pl.program_id(0) >= fill_steps)
    def _():
        body()


def _round_robin(stage_lists):
    live = list(stage_lists)
    while live:
        for g in list(live):
            if next(g, StopIteration) is StopIteration:
                live.remove(g)
        yield


def _interleave(stage_lists):
    for _ in _round_robin(stage_lists):
        pass


def _mixer_a_kernel(x_ref, gpre_ref, gpost_ref, win_ref, gln_ref, bln_ref, s_ref, bs_ref,
                    mk_ref, mv_ref, wout_ref, *rest, rows_per_mem, tiles, shared_mem, fill_steps, n_cast):
    cast_in, (o_ref, *v_out), cast_out = _split_cast_refs(rest, n_cast)
    nt = GM_W // LANES
    seqs = TILE // rows_per_mem
    lane = lax.broadcasted_iota(jnp.int32, (1, LANES), 1)
    lo = lane < (GM_GW - LANES)
    inv = 1.0 / GM_GW

    def group_stat(a):
        s0 = _lsum(a[0] + jnp.where(lo, a[1], 0.0)) * inv
        s1 = _lsum(jnp.where(lo, 0.0, a[1]) + a[2]) * inv
        s2 = _lsum(a[3] + jnp.where(lo, a[4], 0.0)) * inv
        s3 = _lsum(jnp.where(lo, 0.0, a[4]) + a[5]) * inv
        return [s0, jnp.where(lo, s0, s1), s1, s2, jnp.where(lo, s2, s3), s3]

    def tile(t):
        rows = slice(t * TILE, (t + 1) * TILE)
        z = _dot(_rms(x_ref[rows, :], gpre_ref[...]).astype(BF16), win_ref[...])
        yield
        u = [_gelu(z[:, j * LANES:(j + 1) * LANES]) for j in range(nt)]
        g = [_gelu(z[:, GM_W + j * LANES:GM_W + (j + 1) * LANES]) for j in range(nt)]
        mu = group_stat(g)
        c = [g[j] - mu[j] for j in range(nt)]
        var = group_stat([cj * cj for cj in c])
        gln = gln_ref[...]
        bln = bln_ref[...]
        vn = [c[j] * lax.rsqrt(var[j] + EPS) * gln[:, j * LANES:(j + 1) * LANES]
              + bln[:, j * LANES:(j + 1) * LANES] for j in range(nt)]
        if v_out:
            v_out[0][rows, :] = jnp.concatenate(vn, axis=-1)
        yield
        vb = [a.astype(BF16) for a in vn]
        win = [(0, 1), (1, 2), (3, 4), (4, 5)]
        m = [_dot(s_ref[k], jnp.concatenate([vb[a], vb[b]], axis=-1)) for k, (a, b) in enumerate(win)]
        mixed = [m[0][:, :LANES], jnp.where(lo, m[0][:, LANES:], m[1][:, :LANES]), m[1][:, LANES:],
                 m[2][:, :LANES], jnp.where(lo, m[2][:, LANES:], m[3][:, :LANES]), m[3][:, LANES:]]
        bs = bs_ref[...]
        gm = [u[j] * (mixed[j] + bs[:, j * LANES:(j + 1) * LANES]) for j in range(nt)]
        yield
        qm = (z[:, 2 * GM_W:] * Q_SCALE).astype(BF16)
        mo = []
        for b in range(seqs):
            r = slice(b * rows_per_mem, (b + 1) * rows_per_mem)
            mi = 0 if shared_mem else t * seqs + b
            mo.append(_mem_attend(qm[r], mk_ref[mi].astype(BF16), mv_ref[mi].astype(BF16)))
        mo = mo[0] if len(mo) == 1 else jnp.concatenate(mo, axis=0)
        yield
        cat = jnp.concatenate([a.astype(BF16) for a in gm] + [mo.astype(BF16)], axis=-1)
        o_ref[rows, :] = x_ref[rows, :] + _rms(_dot(cat, wout_ref[...]), gpost_ref[...])

    def body():
        _interleave([tile(t) for t in range(tiles)] + [_cast_blocks(cast_in, cast_out)])

    _after_fill(o_ref, fill_steps, body)


def _mem_spec(mem, seqs, layer):
    if mem.shape[1] == 1:
        return pl.BlockSpec((None, 1, MEM_W, MEM_LEN), lambda i: (layer, 0, 0, 0))
    return pl.BlockSpec((None, seqs, MEM_W, MEM_LEN), lambda i: (layer, i, 0, 0))


def _mixer_a(x, gpre, gpost, win, gln, bln, s_mat, bs, mkt, mvt, wout, *, rows_per_mem, emit_v, tiles=2,
             slab_rows, slab_offset=0, into=None, cast=()):
    n = x.shape[0]
    blk = tiles * TILE
    seqs = blk // rows_per_mem
    off = slab_offset // blk
    fill, shift = (0, off) if into is not None else (off, 0)
    row = lambda w: pl.BlockSpec((blk, w), lambda i: (jnp.maximum(i - fill, 0), 0))
    out_shape = [jax.ShapeDtypeStruct((slab_rows, D_MODEL), F32)]
    out_specs = [pl.BlockSpec((blk, D_MODEL), lambda i: (i + shift, 0))]
    if emit_v:
        out_shape.append(jax.ShapeDtypeStruct((n, GM_W), F32))
        out_specs.append(row(GM_W))
    cast_in_specs, cast_out_specs, cast_out_shape = _cast_specs(cast, n // blk, fill)
    return _call(
        functools.partial(_mixer_a_kernel, rows_per_mem=rows_per_mem, tiles=tiles,
                          shared_mem=mkt.shape[1] == 1, fill_steps=fill, n_cast=len(cast)), "mixer_a", into=into,
        semantics="arbitrary" if fill and cast else "parallel",
        grid=(n // blk + fill,),
        in_specs=[
            row(D_MODEL),
            _const_spec((1, D_MODEL)), _const_spec((1, D_MODEL)),
            _const_spec((D_MODEL, 2 * GM_W + MEM_W)),
            _const_spec((1, GM_W)), _const_spec((1, GM_W)),
            _const_spec((GM_GROUPS, TILE, TILE)), _const_spec((TILE, GM_W)),
            _mem_spec(mkt, seqs, 0), _mem_spec(mvt, seqs, 0),
            _const_spec((GM_W + MEM_W, D_MODEL)),
            *cast_in_specs,
        ],
        out_specs=out_specs + cast_out_specs,
        out_shape=out_shape + cast_out_shape,
    )(x, gpre, gpost, win, gln, bln, s_mat, bs, mkt, mvt, wout, *[a for a, _ in cast])


def _ffn_kernel(x_ref, gpre_ref, gpost_ref, w1_ref, w2_ref, o_ref, *o_tail, tile_rows, ff_chunk, head_steps):
    def tile(t):
        r = slice(sum(tile_rows[:t]), sum(tile_rows[:t + 1]))
        x = x_ref[r, :]
        xn = _rms(x, gpre_ref[...]).astype(BF16)
        acc = jnp.zeros(x.shape, F32)
        yield
        for c in range(D_FF // ff_chunk):
            h = _dot(xn, w1_ref[:, c * ff_chunk:(c + 1) * ff_chunk])
            h = jnp.square(jnp.maximum(h, 0.0)).astype(BF16)
            acc = acc + _dot(h, w2_ref[c * ff_chunk:(c + 1) * ff_chunk, :])
            yield
        y = x + _rms(acc, gpost_ref[...])
        if not o_tail:
            o_ref[r, :] = y
        else:
            is_head = pl.program_id(0) < head_steps
            o_ref[r, :] = jnp.where(is_head, y, o_ref[r, :])
            o_tail[0][r, :] = y

    if o_tail:
        @pl.when(pl.program_id(0) < head_steps)
        def _():
            o_ref[...] = jnp.zeros(o_ref.shape, F32)

    _interleave([tile(t) for t in range(len(tile_rows))])


def _ffn(x, gpre, gpost, w1, w2, layer, *, tile_rows=(256, 512, 256), ff_chunk=1024, split=None):
    n = x.shape[0]
    blk = sum(tile_rows)
    row = pl.BlockSpec((blk, D_MODEL), lambda i: (i, 0))
    if split is None:
        head_steps, out_specs, out_shape = None, row, jax.ShapeDtypeStruct((n, D_MODEL), F32)
    else:
        head_steps = split // blk
        out_specs = [pl.BlockSpec((blk, D_MODEL), lambda i: (jnp.minimum(i, head_steps - 1), 0)),
                     pl.BlockSpec((blk, D_MODEL), lambda i: (jnp.maximum(i - head_steps, 0), 0))]
        out_shape = [jax.ShapeDtypeStruct((split, D_MODEL), F32), jax.ShapeDtypeStruct((n - split, D_MODEL), F32)]
    return _call(
        functools.partial(_ffn_kernel, tile_rows=tile_rows, ff_chunk=ff_chunk, head_steps=head_steps),
        "ffn", semantics="arbitrary",
        grid=(n // blk,),
        in_specs=[row, _layer_spec((1, D_MODEL), layer), _layer_spec((1, D_MODEL), layer),
                  _const_spec((D_MODEL, D_FF)), _const_spec((D_FF, D_MODEL))],
        out_specs=out_specs,
        out_shape=out_shape,
    )(x, gpre, gpost, w1, w2)


def _proj_b_kernel(x_ref, gkv_ref, gpre_ref, *refs, mode, seq_len, n_weights, n_cast):
    rows = x_ref.shape[0]
    cast_in, outs, cast_out = _split_cast_refs(refs[n_weights:], n_cast)
    refs = (*refs[:n_weights], *outs)

    def normed(r):
        x = x_ref[r, :]
        xh = x * lax.rsqrt(jnp.mean(x * x, axis=-1, keepdims=True) + EPS)
        return (xh * gkv_ref[...]).astype(BF16), (xh * gpre_ref[...]).astype(BF16)

    if mode == "tail":
        wk_ref, wv_ref, kt_ref, vt_ref = refs
        xkv, _ = normed(slice(None))
        kt_ref[...] = _dot(xkv, wk_ref[...]).T
        vt_ref[...] = _dot(xkv, wv_ref[...]).T
        return
    if mode == "prompt":
        win_ref, wk_ref, wv_ref, qt_ref, qm_ref, k_ref, vt_ref = refs

        def tile(t):
            r = slice(t * PROJ_ROWS, (t + 1) * PROJ_ROWS)
            xkv, xq = normed(r)
            yield
            z = _dot(xq, win_ref[...]) * Q_SCALE
            qm_ref[r, :] = z[:, B_W:].astype(BF16)
            for p in range(HEAD_PAIRS):
                qt_ref[p, :, r] = z[:, p * LANES:(p + 1) * LANES].T.astype(BF16)
            yield
            v = _dot(xkv, wv_ref[...])
            for p in range(HEAD_PAIRS):
                vt_ref[p, :, r] = v[:, p * LANES:(p + 1) * LANES].T.astype(BF16)
            yield
            k = _dot(xkv, wk_ref[...])
            for p in range(HEAD_PAIRS):
                k_ref[p, r, :] = k[:, p * LANES:(p + 1) * LANES].astype(BF16)

        _interleave([tile(t) for t in range(rows // PROJ_ROWS)] + [_cast_blocks(cast_in, cast_out)])
        return
    win_ref, wk_ref, wv_ref, q_ref, qm_ref, k_ref, v_ref, k4_ref, v4_ref = refs
    xkv, xq = normed(slice(None))
    z = _dot(xq, win_ref[...]) * Q_SCALE
    qm_ref[...] = z[:, B_W:].astype(BF16)
    k = _dot(xkv, wk_ref[...])
    v = _dot(xkv, wv_ref[...])
    for p in range(HEAD_PAIRS):
        cols = slice(p * LANES, (p + 1) * LANES)
        q_ref[p] = z[:, cols].astype(BF16)
        k_ref[p] = k[:, cols].astype(BF16)
        v_ref[p] = v[:, cols].astype(BF16)
    for b in range(rows // seq_len):
        for h in range(B_HEADS):
            r, c = slice(b * seq_len, (b + 1) * seq_len), slice(h * HEAD_DIM, (h + 1) * HEAD_DIM)
            k4_ref[b, h] = k[r, c]
            v4_ref[b, h] = v[r, c]


def _proj_b(x, gkv, gpre, weights, *, mode, rows, row_offset=0, n_rows=None, seq_len=None, cast=()):
    n = x.shape[0] if n_rows is None else n_rows
    off = row_offset // rows
    pair_rows = jax.ShapeDtypeStruct((HEAD_PAIRS, n, LANES), BF16)
    pair_rows_spec = pl.BlockSpec((HEAD_PAIRS, rows, LANES), lambda i: (0, i, 0))
    pair_cols = jax.ShapeDtypeStruct((HEAD_PAIRS, LANES, n), BF16)
    pair_cols_spec = pl.BlockSpec((HEAD_PAIRS, LANES, rows), lambda i: (0, 0, i))
    qm = jax.ShapeDtypeStruct((n, MEM_W), BF16)
    qm_spec = pl.BlockSpec((rows, MEM_W), lambda i: (i, 0))
    if mode == "tail":
        out_shape = [jax.ShapeDtypeStruct((B_W, n), F32)] * 2
        out_specs = [pl.BlockSpec((B_W, rows), lambda i: (0, i))] * 2
    elif mode == "prompt":
        out_shape = [pair_cols, qm, pair_rows, pair_cols]
        out_specs = [pair_cols_spec, qm_spec, pair_rows_spec, pair_cols_spec]
    else:
        seqs = rows // seq_len
        per_head = jax.ShapeDtypeStruct((n // seq_len, B_HEADS, seq_len, HEAD_DIM), F32)
        per_head_spec = pl.BlockSpec((seqs, B_HEADS, seq_len, HEAD_DIM), lambda i: (i, 0, 0, 0))
        out_shape = [pair_rows, qm, pair_rows, pair_rows, per_head, per_head]
        out_specs = [pair_rows_spec, qm_spec, pair_rows_spec, pair_rows_spec, per_head_spec, per_head_spec]
    cast_in_specs, cast_out_specs, cast_out_shape = _cast_specs(cast, n // rows, 0)
    return _call(
        functools.partial(_proj_b_kernel, mode=mode, seq_len=seq_len, n_weights=len(weights),
                          n_cast=len(cast)), "proj_b_" + mode,
        grid=(n // rows,),
        in_specs=[pl.BlockSpec((rows, D_MODEL), lambda i: (i + off, 0)),
                  _const_spec((1, D_MODEL)), _const_spec((1, D_MODEL))]
                 + [_const_spec(w.shape) for w in weights] + cast_in_specs,
        out_specs=out_specs + cast_out_specs,
        out_shape=out_shape + cast_out_shape,
    )(x, gkv, gpre, *weights, *[a for a, _ in cast])


def _bias_kernel(*refs, tables):
    g_refs, o_refs = refs[:len(tables)], list(refs[len(tables):])
    for g_ref, (n_q, n_k, splits, band, keys_on_rows) in zip(g_refs, tables):
        outs = [o_refs.pop(0) for _ in splits]
        n_rows, shift = (n_k, n_q) if keys_on_rows else (n_q, BIAS_PERIOD - (n_q - 1))
        for hh in range(2):
            x = jnp.broadcast_to(g_ref[hh:hh + 1, :], (n_rows, BIAS_PERIOD))
            t = pltpu.roll(x, shift, 1, stride=1, stride_axis=0)
            if band:
                a = lax.broadcasted_iota(jnp.int32, (n_rows, BIAS_PERIOD), 0)
                b = lax.broadcasted_iota(jnp.int32, (n_rows, BIAS_PERIOD), 1)
                r, w = (b, a) if keys_on_rows else (a, b)
                j = w - (r - (r & (CHUNK - 1)))
                t = jnp.where((j >= 0) & (j < BAND_PAST + CHUNK), t, NEG)
            for o_ref, (lo, hi) in zip(outs, splits):
                o_ref[hh * n_rows:(hh + 1) * n_rows, :] = t[:, lo:hi]


def _rel_bias_tables(rel_bias, tables):
    rel_bias = rel_bias - rel_bias[:, -1:]
    gens, out_specs, out_shape = [], [], []
    for n_q, n_k, splits, _, keys_on_rows in tables:
        c0 = n_k - 1
        far = jnp.broadcast_to(rel_bias[:, -1:], (B_HEADS, c0 - REL_CLIP))
        near = jnp.broadcast_to(rel_bias[:, :1], (B_HEADS, BIAS_PERIOD - (c0 - REL_CLIP) - (2 * REL_CLIP + 1)))
        gen = [near, rel_bias, far] if keys_on_rows else [far, rel_bias[:, ::-1], near]
        gens.append(jnp.concatenate(gen, axis=1).reshape(HEAD_PAIRS, 2, BIAS_PERIOD))
        n_rows = n_k if keys_on_rows else n_q
        out_specs += [pl.BlockSpec((None, 2 * n_rows, hi - lo), lambda p: (p, 0, 0)) for lo, hi in splits]
        out_shape += [jax.ShapeDtypeStruct((HEAD_PAIRS, 2 * n_rows, hi - lo), F32) for lo, hi in splits]
    return _call(
        functools.partial(_bias_kernel, tables=tables), "rel_bias",
        grid=(HEAD_PAIRS,),
        in_specs=[pl.BlockSpec((None, 2, BIAS_PERIOD), lambda p: (p, 0, 0))] * len(tables),
        out_specs=out_specs,
        out_shape=out_shape,
    )(*gens)


def _band_block_kinds():
    n_k = KEY_BLOCKS * TILE
    r = np.arange(TILE)[None, :]
    w = np.arange(n_k)[:, None]
    j = w - CHUNK * (r // CHUNK)
    ok = (j >= 0) & (j < BAND_PAST + CHUNK)
    plain = ok & (BAND_PAST + r - w >= REL_CLIP)
    kinds = []
    for a in range(n_k // KEY_BLOCK):
        rows = slice(a * KEY_BLOCK, (a + 1) * KEY_BLOCK)
        blocks = [(rows, slice(b * LANES, (b + 1) * LANES)) for b in range(TILE // LANES)]
        kinds.append(["skip" if not ok[blk].any() else "plain" if plain[blk].all() else "biased"
                      for blk in blocks])
    return kinds


def _band_prompt_kernel(x_ref, qt_ref, qm_ref, kp_ref, kc_ref, vtp_ref, vtc_ref,
                        bias_ref, mk_ref, mv_ref, wout_ref, gpost_ref, o_ref, *, fill_steps):
    i = pl.program_id(0) - fill_steps
    n_k = KEY_BLOCKS * TILE
    kinds = _band_block_kinds()
    w = lax.broadcasted_iota(jnp.int32, (n_k, LANES), 0)
    c = lax.broadcasted_iota(jnp.int32, (n_k, LANES), 1)
    ones_rows = jnp.where(lax.broadcasted_iota(jnp.int32, (BF16_ROWS, n_k), 0) == 0, 1.0, 0.0).astype(BF16)
    row = lax.broadcasted_iota(jnp.int32, (LANES, TILE), 0)
    one_hot_row = jnp.where(row == 0, 1.0, 0.0).astype(BF16)
    lo = row < HEAD_DIM

    def key_step(a, s, vth, bias_rows, state):
        blocks = range(a * KEY_STEP // KEY_BLOCK, (a + 1) * KEY_STEP // KEY_BLOCK)
        es, alphas, active = [], [], []
        for b in range(TILE // LANES):
            cols = slice(b * LANES, (b + 1) * LANES)
            sjs = {}
            for blk in blocks:
                if kinds[blk][b] != "skip":
                    rows = slice(blk * KEY_BLOCK, (blk + 1) * KEY_BLOCK)
                    sjs[blk] = s[rows, cols] + bias_rows(rows, cols) if kinds[blk][b] == "biased" else s[rows, cols]
            active.append(bool(sjs))
            alphas.append(None)
            if not sjs:
                es.append(jnp.zeros((KEY_STEP, LANES), F32))
                continue
            mj = jnp.max(functools.reduce(jnp.maximum, sjs.values()), axis=0, keepdims=True)
            if state[b] is None:
                state[b] = [mj, None]
            else:
                m_new = jnp.maximum(state[b][0], mj)
                alphas[b] = jnp.exp(state[b][0] - m_new)
                state[b][0] = m_new
            es.append(jnp.concatenate(
                [jnp.exp(sjs[blk] - state[b][0]) if blk in sjs else jnp.zeros((KEY_BLOCK, LANES), F32)
                 for blk in blocks], axis=0))
        rows = slice(a * KEY_STEP, (a + 1) * KEY_STEP)
        pv = _dot(vth[:, rows], jnp.concatenate(es, axis=1).astype(BF16))[:OUT_ROWS]
        for b in range(TILE // LANES):
            if active[b]:
                pv_b = pv[:, b * LANES:(b + 1) * LANES]
                state[b][1] = pv_b if alphas[b] is None else state[b][1] * alphas[b] + pv_b

    def tile(t):
        q_rows = slice(t * TILE, (t + 1) * TILE)
        k_rows = slice(t * TILE, t * TILE + n_k)
        first_key_tile = i * BAND_TILES + t - (KEY_BLOCKS - 1)
        pen = jnp.where((w < -first_key_tile * TILE) & (c == 0), NEG, 0.0).astype(BF16)
        outs = []
        for p0 in range(0, HEAD_PAIRS, GROUP_PAIRS):
            heads = []
            for p in range(p0, p0 + GROUP_PAIRS):
                k_win = jnp.concatenate([kp_ref[p], kc_ref[p]], axis=0)[k_rows]
                k_ext = jnp.concatenate([k_win, pen], axis=1)
                vt = jnp.concatenate([vtp_ref[p], vtc_ref[p]], axis=1)[:, k_rows]
                qt = qt_ref[p, :, q_rows]
                zero = jnp.zeros_like(qt)
                for hh in range(2):
                    qh = jnp.where(lo, qt, zero) if hh == 0 else jnp.where(lo, zero, qt)
                    s = _dot(k_ext, jnp.concatenate([qh, one_hot_row], axis=0))
                    vth = jnp.concatenate([vt[hh * HEAD_DIM:(hh + 1) * HEAD_DIM, :], ones_rows], axis=0)
                    bias_rows = functools.partial(
                        lambda rows, cols, p, base: bias_ref[p, base + rows.start:base + rows.stop, cols],
                        p=p, base=hh * n_k)
                    heads.append((s, vth, bias_rows, [None] * (TILE // LANES)))
            for a in range(n_k // KEY_STEP):
                for head in heads:
                    key_step(a, *head)
            outs += [jnp.concatenate([o[:HEAD_DIM] * (1.0 / o[HEAD_DIM:HEAD_DIM + 1]) for _, o in state], axis=1)
                     for *_, state in heads]
            yield
        band = jnp.concatenate(outs, axis=0).T.astype(BF16)
        mo = _mem_attend(qm_ref[q_rows, :], mk_ref[0].astype(BF16), mv_ref[0].astype(BF16))
        cat = jnp.concatenate([band, mo.astype(BF16)], axis=-1)
        o_ref[q_rows, :] = x_ref[q_rows, :] + _rms(_dot(cat, wout_ref[...]), gpost_ref[...])

    _after_fill(o_ref, fill_steps, lambda: _interleave([tile(t) for t in range(BAND_TILES)]))


def _band_prompt(x, qt3, qm, k3, vt3, bias, mkt, mvt, wout, gpost):
    n = qm.shape[0]
    rows = BAND_TILES * TILE
    assert rows == BAND_PAST and n % rows == 0 and (x.shape[0] - n) % rows == 0
    fill = (x.shape[0] - n) // rows
    slab_spec = pl.BlockSpec((rows, D_MODEL), lambda i: (i, 0))
    own = lambda i: jnp.maximum(i - fill, 0)
    past = lambda i: jnp.maximum(i - fill - 1, 0)
    return _call(
        functools.partial(_band_prompt_kernel, fill_steps=fill), "band_prompt",
        grid=(n // rows + fill,),
        in_specs=[slab_spec,
                  pl.BlockSpec((HEAD_PAIRS, LANES, rows), lambda i: (0, 0, own(i))),
                  pl.BlockSpec((rows, MEM_W), lambda i: (own(i), 0)),
                  pl.BlockSpec((HEAD_PAIRS, rows, LANES), lambda i: (0, past(i), 0)),
                  pl.BlockSpec((HEAD_PAIRS, rows, LANES), lambda i: (0, own(i), 0)),
                  pl.BlockSpec((HEAD_PAIRS, LANES, rows), lambda i: (0, 0, past(i))),
                  pl.BlockSpec((HEAD_PAIRS, LANES, rows), lambda i: (0, 0, own(i))),
                  _const_spec((HEAD_PAIRS, 2 * KEY_BLOCKS * TILE, TILE)),
                  _mem_spec(mkt, 1, 1), _mem_spec(mvt, 1, 1),
                  _const_spec((B_W + MEM_W, D_MODEL)), _const_spec((1, D_MODEL))],
        out_specs=slab_spec,
        out_shape=jax.ShapeDtypeStruct(x.shape, F32),
    )(x, qt3, qm, k3, k3, vt3, vt3, bias, mkt, mvt, wout, gpost)


def _band_sample_kernel(x_ref, q_ref, qm_ref, kn_ref, vn_ref, ck_ref, cv_ref, bc_ref, bn_ref,
                        mk_ref, mv_ref, wout_ref, gpost_ref, o_ref, *, seqs, seq_len):
    lane = lax.broadcasted_iota(jnp.int32, (1, LANES), 1)
    lo = lane < HEAD_DIM

    def pair_attend(b, p, out):
        rows = slice(b * seq_len, (b + 1) * seq_len)
        hd = slice(p * LANES, (p + 1) * LANES)
        qp = q_ref[p, rows, :]
        zero = jnp.zeros_like(qp)
        qs = jnp.concatenate([jnp.where(lo, qp, zero), jnp.where(lo, zero, qp)], axis=0)
        sc = _dot(qs, ck_ref[b, hd, :].astype(BF16)) + bc_ref[p]
        sn = _dot_nt(qs, kn_ref[p, rows, :]) + bn_ref[p]
        yield
        m = jnp.maximum(_lmax(sc), _lmax(sn))
        ec = jnp.exp(sc - m)
        en = jnp.exp(sn - m)
        l = _lsum(ec) + _lsum(en)
        yield
        o = (_dot_nt(ec.astype(BF16), cv_ref[b, hd, :].astype(BF16))
             + _dot(en.astype(BF16), vn_ref[p, rows, :]))
        yield
        o = o * (1.0 / l)
        out[p] = jnp.where(lo, o[:seq_len], o[seq_len:]).astype(BF16)

    def seq_attend(b, out):
        band = [None] * HEAD_PAIRS
        yield from _round_robin([pair_attend(b, p, band) for p in range(HEAD_PAIRS)])
        rows = slice(b * seq_len, (b + 1) * seq_len)
        mo = _mem_attend(qm_ref[rows, :], mk_ref[b].astype(BF16), mv_ref[b].astype(BF16))
        out[b] = jnp.concatenate(band + [mo.astype(BF16)], axis=-1)

    rows_out = [None] * seqs
    _interleave([seq_attend(b, rows_out) for b in range(seqs)])
    cat = jnp.concatenate(rows_out, axis=0)
    o_ref[...] = x_ref[...] + _rms(_dot(cat, wout_ref[...]), gpost_ref[...])


def _band_sample(x, q3, qm, kn3, vn3, ckt, cvt, bias_c, bias_n, mkt, mvt, wout, gpost, *, into, seqs=4):
    n = qm.shape[0]
    n_seq, past = ckt.shape[0], ckt.shape[2]
    seq_len = n // n_seq
    rows = seqs * seq_len
    slab_spec = pl.BlockSpec((rows, D_MODEL), lambda i: (i, 0))
    pair_rows_spec = pl.BlockSpec((HEAD_PAIRS, rows, LANES), lambda i: (0, i, 0))
    cache_spec = pl.BlockSpec((seqs, B_W, past), lambda i: (i, 0, 0))
    return _call(
        functools.partial(_band_sample_kernel, seqs=seqs, seq_len=seq_len), "band_sample", into=into,
        grid=(n_seq // seqs,),
        in_specs=[slab_spec,
                  pair_rows_spec,
                  pl.BlockSpec((rows, MEM_W), lambda i: (i, 0)),
                  pair_rows_spec, pair_rows_spec,
                  cache_spec, cache_spec,
                  _const_spec((HEAD_PAIRS, 2 * seq_len, past)),
                  _const_spec((HEAD_PAIRS, 2 * seq_len, seq_len)),
                  _mem_spec(mkt, seqs, 1), _mem_spec(mvt, seqs, 1),
                  _const_spec((B_W + MEM_W, D_MODEL)), _const_spec((1, D_MODEL))],
        out_specs=slab_spec,
        out_shape=jax.ShapeDtypeStruct(x.shape, F32),
    )(x, q3, qm, kn3, vn3, ckt, cvt, bias_c, bias_n, mkt, mvt, wout, gpost)


def _spatial_tile(w_s, b_s, period):
    tril = jnp.tril(jnp.ones((GM_CHUNK, GM_CHUNK), dtype=bool))
    w = jnp.where(tril, w_s, jnp.zeros((), w_s.dtype))[:, :period, :period]
    eye = jnp.eye(TILE // period, dtype=w.dtype)
    s_mat = jnp.einsum("ab,gts->gatbs", eye, w).reshape(GM_GROUPS, TILE, TILE)
    rows = jnp.tile(b_s[:, :period], (1, TILE // period))
    bs = jnp.repeat(rows.T, GM_GW, axis=1)
    return s_mat.astype(BF16), bs


def _heads_last(t, lead):
    pos = t.shape[-1]
    t = t.reshape(lead + (-1, HEAD_DIM, pos))
    nd = len(lead)
    return jnp.transpose(t, tuple(range(nd)) + (nd + 2, nd, nd + 1))


def _positions_last(c):
    nd = c.ndim
    t = jnp.transpose(c, tuple(range(nd - 3)) + (nd - 2, nd - 1, nd - 3))
    return t.reshape(c.shape[:-3] + (c.shape[-2] * c.shape[-1], c.shape[-3]))


def kernel(x_prompt, x_sample, cache_mem_k, cache_mem_v, cache_band_k, cache_band_v, mem_prompt,
           g_mix_pre, g_mix_post, g_ffn_pre, g_ffn_post, g_mem, w_mem_kv,
           w_in_a, g_gm_ln, b_gm_ln, w_spatial, b_spatial, w_out_a,
           g_kv, w_kv, w_in_b, rel_bias, w_out_b, w_ff1, w_ff2):
    seq = x_prompt.shape[1]
    n_seq, seq_len = x_sample.shape[0], x_sample.shape[1]
    past = cache_band_k.shape[1]
    vec = lambda a: a.reshape(1, -1)
    stack = lambda a: a.reshape(DEPTH, 1, -1)

    win_a = w_in_a[0].astype(BF16)
    wout_a = w_out_a[0].astype(BF16)
    wk = w_kv[:, :B_W].astype(BF16)
    wv = w_kv[:, B_W:].astype(BF16)
    win_b = w_in_b[0].astype(BF16)
    wout_b = w_out_b[0].astype(BF16)
    ln_g, ln_b = vec(g_gm_ln[0]), vec(b_gm_ln[0])
    gkv, gpre_b = vec(g_kv), vec(g_mix_pre[1])
    gf_pre, gf_post = stack(g_ffn_pre), stack(g_ffn_post)

    n_sample = n_seq * seq_len
    slab_rows = seq + n_sample
    pre_a, post_a = vec(g_mix_pre[0]), vec(g_mix_post[0])

    mem_kt, mem_vt = _memkv(mem_prompt[0], g_mem, w_mem_kv)
    s_p, bs_p = _spatial_tile(w_spatial[0], b_spatial[0], GM_CHUNK)
    s_s, bs_s = _spatial_tile(w_spatial[0], b_spatial[0], seq_len)
    cmkt, cmvt = _positions_last(cache_mem_k), _positions_last(cache_mem_v)
    x, w1, w2 = _mixer_a(x_prompt[0], pre_a, post_a, win_a, ln_g, ln_b, s_p, bs_p, mem_kt, mem_vt, wout_a,
                         rows_per_mem=TILE, emit_v=False, tiles=4, slab_rows=slab_rows, slab_offset=n_sample,
                         cast=((w_ff1, 0), (w_ff2, 0)))
    x, v_rows = _mixer_a(x_sample.reshape(n_sample, D_MODEL), pre_a, post_a, win_a, ln_g, ln_b, s_s, bs_s,
                         cmkt, cmvt, wout_a, rows_per_mem=seq_len, emit_v=True,
                         slab_rows=slab_rows, into=x)
    x = _ffn(x, gf_pre, gf_post, w1, w2, 0)

    qt3, qm, k3, vt3, w1, w2 = _proj_b(x, gkv, gpre_b, (win_b, wk, wv), mode="prompt", rows=2 * PROJ_ROWS,
                                       row_offset=n_sample, n_rows=seq, cast=((w_ff1, 1), (w_ff2, 1)))
    n_keep = min(BAND_PAST, seq)
    kt_tail, vt_tail = _proj_b(x, gkv, gpre_b, (wk, wv), mode="tail", rows=TILE,
                               row_offset=n_sample + seq - n_keep, n_rows=n_keep)
    q3s, qms, kn3, vn3, k_new, v_new = _proj_b(x, gkv, gpre_b, (win_b, wk, wv), mode="sample", rows=512,
                                               n_rows=n_sample, seq_len=seq_len)
    bias_p, bias_c, bias_n = _rel_bias_tables(rel_bias[0], (
        (TILE, BAND_PAST + TILE, ((0, TILE),), True, True),
        (seq_len, past + seq_len, ((0, past), (past, past + seq_len)), False, False)))
    post_b = vec(g_mix_post[1])
    y = _band_prompt(x, qt3, qm, k3, vt3, bias_p, mem_kt, mem_vt, wout_b, post_b)
    y = _band_sample(x, q3s, qms, kn3, vn3, _positions_last(cache_band_k), _positions_last(cache_band_v),
                     bias_c, bias_n, cmkt, cmvt, wout_b, post_b, into=y)
    y_sample, y_prompt = _ffn(y, gf_pre, gf_post, w1, w2, 1, split=n_sample)
    y_prompt = y_prompt[None]
    y_sample = y_sample.reshape(n_seq, seq_len, D_MODEL)

    return (y_prompt, y_sample,
            _heads_last(mem_kt, (DEPTH, 1)), _heads_last(mem_vt, (DEPTH, 1)),
            _heads_last(kt_tail, (1,)), _heads_last(vt_tail, (1,)),
            v_rows.reshape(1, n_seq, seq_len, GM_W),
            jnp.swapaxes(k_new, 1, 2), jnp.swapaxes(v_new, 1, 2))
```

```python
import functools

import jax
import jax.numpy as jnp
import numpy as np
from jax import lax
from jax.experimental import pallas as pl
from jax.experimental.pallas import tpu as pltpu

D_MODEL = 1024
DEPTH = 2
CHUNK = 64
HEAD_DIM = 64
GM_CHUNK = 128
GM_GROUPS = 4
GM_W = 768
GM_GW = GM_W // GM_GROUPS
MEM_LEN = 256
MEM_HEADS = 4
MEM_W = MEM_HEADS * HEAD_DIM
B_HEADS = 12
B_W = B_HEADS * HEAD_DIM
BAND_PAST = 512
REL_CLIP = 128
D_FF = 4 * D_MODEL
EPS = 1e-6

LANES = 128
SUBLANES = 8
HEAD_PAIRS = B_W // LANES
Q_SCALE = HEAD_DIM ** -0.5
NEG = -1e30
TILE = 256
KEY_BLOCKS = BAND_PAST // TILE + 1
KEY_STEP = 128
KEY_BLOCK = 128
PROJ_ROWS = 512
BF16_ROWS = 2 * SUBLANES
OUT_ROWS = HEAD_DIM + SUBLANES
BAND_TILES = BAND_PAST // TILE
GROUP_PAIRS = 2
BIAS_PERIOD = 1024
V7X_VMEM_BYTES = 64 * 1024 * 1024
VMEM_LIMIT = V7X_VMEM_BYTES * 7 // 8

BF16 = jnp.bfloat16
F32 = jnp.float32


def _dot(a, b):
    return jnp.dot(a, b, preferred_element_type=F32)


def _dot_nt(a, b):
    return lax.dot_general(a, b, (((1,), (1,)), ((), ())), preferred_element_type=F32)


def _rms(x, g):
    ms = jnp.mean(x * x, axis=-1, keepdims=True)
    return x * lax.rsqrt(ms + EPS) * g


def _lsum(a):
    return jnp.sum(a, axis=-1, keepdims=True)


def _lmax(a):
    return jnp.max(a, axis=-1, keepdims=True)


def _const_spec(shape):
    nd = len(shape)
    return pl.BlockSpec(shape, lambda *_: (0,) * nd, pipeline_mode=pl.Buffered(1))


def _layer_spec(shape, layer):
    nd = len(shape)
    return pl.BlockSpec((None,) + shape, lambda *_: (layer,) + (0,) * nd, pipeline_mode=pl.Buffered(1))


def _without_ref(body, k):
    def wrapped(*refs):
        return body(*refs[:k], *refs[k + 1:])
    return wrapped


def _call(body, name, *, in_specs, into=None, semantics="parallel", **kw):
    params = pltpu.CompilerParams(dimension_semantics=(semantics,), vmem_limit_bytes=VMEM_LIMIT)
    if into is None:
        return pl.pallas_call(body, name=name, in_specs=in_specs, compiler_params=params, **kw)
    n_in = len(in_specs)
    call = pl.pallas_call(_without_ref(body, n_in), name=name,
                          in_specs=[*in_specs, pl.BlockSpec(memory_space=pl.ANY)],
                          input_output_aliases={n_in: 0}, compiler_params=params, **kw)
    return lambda *args: call(*args, into)


def _memkv_kernel(mem_ref, g_ref, w_ref, kt_ref, vt_ref):
    h = _dot(_rms(mem_ref[...], g_ref[...]).astype(BF16), w_ref[...].astype(BF16))
    kt_ref[0] = h[:, :MEM_W].T
    vt_ref[0] = h[:, MEM_W:].T


def _memkv(mem, g_mem, w_mem_kv):
    out = jax.ShapeDtypeStruct((DEPTH, 1, MEM_W, MEM_LEN), F32)
    return _call(
        _memkv_kernel, "mem_kv",
        grid=(DEPTH,),
        in_specs=[
            pl.BlockSpec((MEM_LEN, D_MODEL), lambda l: (0, 0)),
            pl.BlockSpec((None, 1, D_MODEL), lambda l: (l, 0, 0)),
            pl.BlockSpec((None, D_MODEL, 2 * MEM_W), lambda l: (l, 0, 0)),
        ],
        out_specs=[pl.BlockSpec((None, 1, MEM_W, MEM_LEN), lambda l: (l, 0, 0, 0))] * 2,
        out_shape=[out, out],
    )(mem, g_mem.reshape(DEPTH, 1, D_MODEL), w_mem_kv)


def _mem_attend(qb, kt, vt):
    r = qb.shape[0]
    lane = lax.broadcasted_iota(jnp.int32, (1, MEM_W), 1)
    masks = [(lane >= h * HEAD_DIM) & (lane < (h + 1) * HEAD_DIM) for h in range(MEM_HEADS)]
    qs = jnp.concatenate([jnp.where(m, qb, jnp.zeros_like(qb)) for m in masks], axis=0)
    s = _dot(qs, kt)
    e = jnp.exp(s - _lmax(s))
    pv = _dot_nt(e.astype(BF16), vt) * (1.0 / _lsum(e))
    out = jnp.where(masks[0], pv[:r], 0.0)
    for h in range(1, MEM_HEADS):
        out = out + jnp.where(masks[h], pv[h * r:(h + 1) * r], 0.0)
    return out


def _gelu(x):
    c1 = float(np.sqrt(2.0 / np.pi))
    c2 = c1 * 0.044715
    half = 0.5 * x
    return half + half * jnp.tanh(x * (c1 + c2 * (x * x)))


def _cast_specs(cast, steps, fill):
    own = lambda i: jnp.maximum(i - fill, 0)
    in_specs, out_specs, out_shape = [], [], []
    for a, layer in cast:
        _, r, c = a.shape
        in_specs.append(pl.BlockSpec((None, r // steps, c), functools.partial(lambda i, l: (l, own(i), 0), l=layer)))
        out_specs.append(pl.BlockSpec((r // steps, c), lambda i: (own(i), 0)))
        out_shape.append(jax.ShapeDtypeStruct((r, c), BF16))
    return in_specs, out_specs, out_shape


def _split_cast_refs(refs, n_cast):
    return refs[:n_cast], refs[n_cast:len(refs) - n_cast], refs[len(refs) - n_cast:]


def _cast_blocks(srcs, dsts, stages=4):
    for src, dst in zip(srcs, dsts):
        step = src.shape[0] // stages
        for k in range(stages):
            dst[k * step:(k + 1) * step, :] = src[k * step:(k + 1) * step, :].astype(dst.dtype)
            yield


def _after_fill(o_ref, fill_steps, body):
    if not fill_steps:
        return body()

    @pl.when(pl.program_id(0) < fill_steps)
    def _():
        o_ref[...] = jnp.zeros(o_ref.shape, o_ref.dtype)

    @pl.when(pl.program_id(0) >= fill_steps)
    def _():
        body()


def _round_robin(stage_lists):
    live = list(stage_lists)
    while live:
        for g in list(live):
            if next(g, StopIteration) is StopIteration:
                live.remove(g)
        yield


def _interleave(stage_lists):
    for _ in _round_robin(stage_lists):
        pass


def _mixer_a_kernel(x_ref, gpre_ref, gpost_ref, win_ref, gln_ref, bln_ref, s_ref, bs_ref,
                    mk_ref, mv_ref, wout_ref, *rest, rows_per_mem, tiles, shared_mem, fill_steps, n_cast):
    cast_in, (o_ref, *v_out), cast_out = _split_cast_refs(rest, n_cast)
    nt = GM_W // LANES
    seqs = TILE // rows_per_mem
    lane = lax.broadcasted_iota(jnp.int32, (1, LANES), 1)
    lo = lane < (GM_GW - LANES)
    inv = 1.0 / GM_GW

    def group_stat(a):
        s0 = _lsum(a[0] + jnp.where(lo, a[1], 0.0)) * inv
        s1 = _lsum(jnp.where(lo, 0.0, a[1]) + a[2]) * inv
        s2 = _lsum(a[3] + jnp.where(lo, a[4], 0.0)) * inv
        s3 = _lsum(jnp.where(lo, 0.0, a[4]) + a[5]) * inv
        return [s0, jnp.where(lo, s0, s1), s1, s2, jnp.where(lo, s2, s3), s3]

    def tile(t):
        rows = slice(t * TILE, (t + 1) * TILE)
        z = _dot(_rms(x_ref[rows, :], gpre_ref[...]).astype(BF16), win_ref[...])
        yield
        u = [_gelu(z[:, j * LANES:(j + 1) * LANES]) for j in range(nt)]
        g = [_gelu(z[:, GM_W + j * LANES:GM_W + (j + 1) * LANES]) for j in range(nt)]
        mu = group_stat(g)
        c = [g[j] - mu[j] for j in range(nt)]
        var = group_stat([cj * cj for cj in c])
        gln = gln_ref[...]
        bln = bln_ref[...]
        vn = [c[j] * lax.rsqrt(var[j] + EPS) * gln[:, j * LANES:(j + 1) * LANES]
              + bln[:, j * LANES:(j + 1) * LANES] for j in range(nt)]
        if v_out:
            v_out[0][rows, :] = jnp.concatenate(vn, axis=-1)
        yield
        vb = [a.astype(BF16) for a in vn]
        win = [(0, 1), (1, 2), (3, 4), (4, 5)]
        m = [_dot(s_ref[k], jnp.concatenate([vb[a], vb[b]], axis=-1)) for k, (a, b) in enumerate(win)]
        mixed = [m[0][:, :LANES], jnp.where(lo, m[0][:, LANES:], m[1][:, :LANES]), m[1][:, LANES:],
                 m[2][:, :LANES], jnp.where(lo, m[2][:, LANES:], m[3][:, :LANES]), m[3][:, LANES:]]
        bs = bs_ref[...]
        gm = [u[j] * (mixed[j] + bs[:, j * LANES:(j + 1) * LANES]) for j in range(nt)]
        yield
        qm = (z[:, 2 * GM_W:] * Q_SCALE).astype(BF16)
        mo = []
        for b in range(seqs):
            r = slice(b * rows_per_mem, (b + 1) * rows_per_mem)
            mi = 0 if shared_mem else t * seqs + b
            mo.append(_mem_attend(qm[r], mk_ref[mi].astype(BF16), mv_ref[mi].astype(BF16)))
        mo = mo[0] if len(mo) == 1 else jnp.concatenate(mo, axis=0)
        yield
        cat = jnp.concatenate([a.astype(BF16) for a in gm] + [mo.astype(BF16)], axis=-1)
        o_ref[rows, :] = x_ref[rows, :] + _rms(_dot(cat, wout_ref[...]), gpost_ref[...])

    def body():
        _interleave([tile(t) for t in range(tiles)] + [_cast_blocks(cast_in, cast_out)])

    _after_fill(o_ref, fill_steps, body)


def _mem_spec(mem, seqs, layer):
    if mem.shape[1] == 1:
        return pl.BlockSpec((None, 1, MEM_W, MEM_LEN), lambda i: (layer, 0, 0, 0))
    return pl.BlockSpec((None, seqs, MEM_W, MEM_LEN), lambda i: (layer, i, 0, 0))


def _mixer_a(x, gpre, gpost, win, gln, bln, s_mat, bs, mkt, mvt, wout, *, rows_per_mem, emit_v, tiles=2,
             slab_rows, slab_offset=0, into=None, cast=()):
    n = x.shape[0]
    blk = tiles * TILE
    seqs = blk // rows_per_mem
    off = slab_offset // blk
    fill, shift = (0, off) if into is not None else (off, 0)
    row = lambda w: pl.BlockSpec((blk, w), lambda i: (jnp.maximum(i - fill, 0), 0))
    out_shape = [jax.ShapeDtypeStruct((slab_rows, D_MODEL), F32)]
    out_specs = [pl.BlockSpec((blk, D_MODEL), lambda i: (i + shift, 0))]
    if emit_v:
        out_shape.append(jax.ShapeDtypeStruct((n, GM_W), F32))
        out_specs.append(row(GM_W))
    cast_in_specs, cast_out_specs, cast_out_shape = _cast_specs(cast, n // blk, fill)
    return _call(
        functools.partial(_mixer_a_kernel, rows_per_mem=rows_per_mem, tiles=tiles,
                          shared_mem=mkt.shape[1] == 1, fill_steps=fill, n_cast=len(cast)), "mixer_a", into=into,
        semantics="arbitrary" if fill and cast else "parallel",
        grid=(n // blk + fill,),
        in_specs=[
            row(D_MODEL),
            _const_spec((1, D_MODEL)), _const_spec((1, D_MODEL)),
            _const_spec((D_MODEL, 2 * GM_W + MEM_W)),
            _const_spec((1, GM_W)), _const_spec((1, GM_W)),
            _const_spec((GM_GROUPS, TILE, TILE)), _const_spec((TILE, GM_W)),
            _mem_spec(mkt, seqs, 0), _mem_spec(mvt, seqs, 0),
            _const_spec((GM_W + MEM_W, D_MODEL)),
            *cast_in_specs,
        ],
        out_specs=out_specs + cast_out_specs,
        out_shape=out_shape + cast_out_shape,
    )(x, gpre, gpost, win, gln, bln, s_mat, bs, mkt, mvt, wout, *[a for a, _ in cast])


def _ffn_kernel(x_ref, gpre_ref, gpost_ref, w1_ref, w2_ref, o_ref, *o_tail, tile_rows, ff_chunk, head_steps):
    def tile(t):
        r = slice(sum(tile_rows[:t]), sum(tile_rows[:t + 1]))
        x = x_ref[r, :]
        xn = _rms(x, gpre_ref[...]).astype(BF16)
        acc = jnp.zeros(x.shape, F32)
        yield
        for c in range(D_FF // ff_chunk):
            h = _dot(xn, w1_ref[:, c * ff_chunk:(c + 1) * ff_chunk])
            h = jnp.square(jnp.maximum(h, 0.0)).astype(BF16)
            acc = acc + _dot(h, w2_ref[c * ff_chunk:(c + 1) * ff_chunk, :])
            yield
        y = x + _rms(acc, gpost_ref[...])
        if not o_tail:
            o_ref[r, :] = y
        else:
            is_head = pl.program_id(0) < head_steps
            o_ref[r, :] = jnp.where(is_head, y, o_ref[r, :])
            o_tail[0][r, :] = y

    if o_tail:
        @pl.when(pl.program_id(0) < head_steps)
        def _():
            o_ref[...] = jnp.zeros(o_ref.shape, F32)

    _interleave([tile(t) for t in range(len(tile_rows))])


def _ffn(x, gpre, gpost, w1, w2, layer, *, tile_rows=(256, 512, 256), ff_chunk=1024, split=None):
    n = x.shape[0]
    blk = sum(tile_rows)
    row = pl.BlockSpec((blk, D_MODEL), lambda i: (i, 0))
    if split is None:
        head_steps, out_specs, out_shape = None, row, jax.ShapeDtypeStruct((n, D_MODEL), F32)
    else:
        head_steps = split // blk
        out_specs = [pl.BlockSpec((blk, D_MODEL), lambda i: (jnp.minimum(i, head_steps - 1), 0)),
                     pl.BlockSpec((blk, D_MODEL), lambda i: (jnp.maximum(i - head_steps, 0), 0))]
        out_shape = [jax.ShapeDtypeStruct((split, D_MODEL), F32), jax.ShapeDtypeStruct((n - split, D_MODEL), F32)]
    return _call(
        functools.partial(_ffn_kernel, tile_rows=tile_rows, ff_chunk=ff_chunk, head_steps=head_steps),
        "ffn", semantics="arbitrary",
        grid=(n // blk,),
        in_specs=[row, _layer_spec((1, D_MODEL), layer), _layer_spec((1, D_MODEL), layer),
                  _const_spec((D_MODEL, D_FF)), _const_spec((D_FF, D_MODEL))],
        out_specs=out_specs,
        out_shape=out_shape,
    )(x, gpre, gpost, w1, w2)


def _proj_b_kernel(x_ref, gkv_ref, gpre_ref, *refs, mode, seq_len, n_weights, n_cast):
    rows = x_ref.shape[0]
    cast_in, outs, cast_out = _split_cast_refs(refs[n_weights:], n_cast)
    refs = (*refs[:n_weights], *outs)

    def normed(r, src=x_ref):
        x = src[r, :]
        xh = x * lax.rsqrt(jnp.mean(x * x, axis=-1, keepdims=True) + EPS)
        return (xh * gkv_ref[...]).astype(BF16), (xh * gpre_ref[...]).astype(BF16)

    if mode == "prompt":
        win_ref, wk_ref, wv_ref, qt_ref, qm_ref, k_ref, vt_ref = refs

        def tile(t):
            r = slice(t * PROJ_ROWS, (t + 1) * PROJ_ROWS)
            xkv, xq = normed(r)
            yield
            z = _dot(xq, win_ref[...]) * Q_SCALE
            qm_ref[r, :] = z[:, B_W:].astype(BF16)
            for p in range(HEAD_PAIRS):
                qt_ref[p, :, r] = z[:, p * LANES:(p + 1) * LANES].T.astype(BF16)
            yield
            v = _dot(xkv, wv_ref[...])
            for p in range(HEAD_PAIRS):
                vt_ref[p, :, r] = v[:, p * LANES:(p + 1) * LANES].T.astype(BF16)
            yield
            k = _dot(xkv, wk_ref[...])
            for p in range(HEAD_PAIRS):
                k_ref[p, r, :] = k[:, p * LANES:(p + 1) * LANES].astype(BF16)

        _interleave([tile(t) for t in range(rows // PROJ_ROWS)] + [_cast_blocks(cast_in, cast_out)])
        return
    win_ref, wk_ref, wv_ref, xt_ref, q_ref, qm_ref, k_ref, v_ref, k4_ref, v4_ref, kt_ref, vt_ref = refs
    xkv_tail, _ = normed(slice(None), xt_ref)
    kt_ref[...] = _dot(xkv_tail, wk_ref[...]).T
    vt_ref[...] = _dot(xkv_tail, wv_ref[...]).T
    xkv, xq = normed(slice(None))
    z = _dot(xq, win_ref[...]) * Q_SCALE
    qm_ref[...] = z[:, B_W:].astype(BF16)
    k = _dot(xkv, wk_ref[...])
    v = _dot(xkv, wv_ref[...])
    for p in range(HEAD_PAIRS):
        cols = slice(p * LANES, (p + 1) * LANES)
        q_ref[p] = z[:, cols].astype(BF16)
        k_ref[p] = k[:, cols].astype(BF16)
        v_ref[p] = v[:, cols].astype(BF16)
    for b in range(rows // seq_len):
        for h in range(B_HEADS):
            r, c = slice(b * seq_len, (b + 1) * seq_len), slice(h * HEAD_DIM, (h + 1) * HEAD_DIM)
            k4_ref[b, h] = k[r, c]
            v4_ref[b, h] = v[r, c]


def _proj_b(x, gkv, gpre, weights, *, mode, rows, row_offset=0, n_rows=None, seq_len=None, cast=(),
            tail_offset=None, tail_rows=None):
    n = x.shape[0] if n_rows is None else n_rows
    off = row_offset // rows
    steps = n // rows
    pair_rows = jax.ShapeDtypeStruct((HEAD_PAIRS, n, LANES), BF16)
    pair_rows_spec = pl.BlockSpec((HEAD_PAIRS, rows, LANES), lambda i: (0, i, 0))
    pair_cols = jax.ShapeDtypeStruct((HEAD_PAIRS, LANES, n), BF16)
    pair_cols_spec = pl.BlockSpec((HEAD_PAIRS, LANES, rows), lambda i: (0, 0, i))
    qm = jax.ShapeDtypeStruct((n, MEM_W), BF16)
    qm_spec = pl.BlockSpec((rows, MEM_W), lambda i: (i, 0))
    extra_in, extra_specs = [], []
    if mode == "prompt":
        out_shape = [pair_cols, qm, pair_rows, pair_cols]
        out_specs = [pair_cols_spec, qm_spec, pair_rows_spec, pair_cols_spec]
    else:
        seqs = rows // seq_len
        per_head = jax.ShapeDtypeStruct((n // seq_len, B_HEADS, seq_len, HEAD_DIM), F32)
        per_head_spec = pl.BlockSpec((seqs, B_HEADS, seq_len, HEAD_DIM), lambda i: (i, 0, 0, 0))
        tail_blk = tail_rows // steps
        tail_off = tail_offset // tail_blk
        extra_in, extra_specs = [x], [pl.BlockSpec((tail_blk, D_MODEL), lambda i: (i + tail_off, 0))]
        out_shape = ([pair_rows, qm, pair_rows, pair_rows, per_head, per_head]
                     + [jax.ShapeDtypeStruct((B_W, tail_rows), F32)] * 2)
        out_specs = ([pair_rows_spec, qm_spec, pair_rows_spec, pair_rows_spec, per_head_spec, per_head_spec]
                     + [pl.BlockSpec((B_W, tail_blk), lambda i: (0, i))] * 2)
    cast_in_specs, cast_out_specs, cast_out_shape = _cast_specs(cast, steps, 0)
    return _call(
        functools.partial(_proj_b_kernel, mode=mode, seq_len=seq_len, n_weights=len(weights) + len(extra_in),
                          n_cast=len(cast)), "proj_b_" + mode,
        grid=(steps,),
        in_specs=[pl.BlockSpec((rows, D_MODEL), lambda i: (i + off, 0)),
                  _const_spec((1, D_MODEL)), _const_spec((1, D_MODEL))]
                 + [_const_spec(w.shape) for w in weights] + extra_specs + cast_in_specs,
        out_specs=out_specs + cast_out_specs,
        out_shape=out_shape + cast_out_shape,
    )(x, gkv, gpre, *weights, *extra_in, *[a for a, _ in cast])


def _bias_kernel(*refs, tables):
    g_refs, o_refs = refs[:len(tables)], list(refs[len(tables):])
    for g_ref, (n_q, n_k, splits, band, keys_on_rows) in zip(g_refs, tables):
        outs = [o_refs.pop(0) for _ in splits]
        n_rows, shift = (n_k, n_q) if keys_on_rows else (n_q, BIAS_PERIOD - (n_q - 1))
        for hh in range(2):
            x = jnp.broadcast_to(g_ref[hh:hh + 1, :], (n_rows, BIAS_PERIOD))
            t = pltpu.roll(x, shift, 1, stride=1, stride_axis=0)
            if band:
                a = lax.broadcasted_iota(jnp.int32, (n_rows, BIAS_PERIOD), 0)
                b = lax.broadcasted_iota(jnp.int32, (n_rows, BIAS_PERIOD), 1)
                r, w = (b, a) if keys_on_rows else (a, b)
                j = w - (r - (r & (CHUNK - 1)))
                t = jnp.where((j >= 0) & (j < BAND_PAST + CHUNK), t, NEG)
            for o_ref, (lo, hi) in zip(outs, splits):
                o_ref[hh * n_rows:(hh + 1) * n_rows, :] = t[:, lo:hi]


def _rel_bias_tables(rel_bias, tables):
    rel_bias = rel_bias - rel_bias[:, -1:]
    gens, out_specs, out_shape = [], [], []
    for n_q, n_k, splits, _, keys_on_rows in tables:
        c0 = n_k - 1
        far = jnp.broadcast_to(rel_bias[:, -1:], (B_HEADS, c0 - REL_CLIP))
        near = jnp.broadcast_to(rel_bias[:, :1], (B_HEADS, BIAS_PERIOD - (c0 - REL_CLIP) - (2 * REL_CLIP + 1)))
        gen = [near, rel_bias, far] if keys_on_rows else [far, rel_bias[:, ::-1], near]
        gens.append(jnp.concatenate(gen, axis=1).reshape(HEAD_PAIRS, 2, BIAS_PERIOD))
        n_rows = n_k if keys_on_rows else n_q
        out_specs += [pl.BlockSpec((None, 2 * n_rows, hi - lo), lambda p: (p, 0, 0)) for lo, hi in splits]
        out_shape += [jax.ShapeDtypeStruct((HEAD_PAIRS, 2 * n_rows, hi - lo), F32) for lo, hi in splits]
    return _call(
        functools.partial(_bias_kernel, tables=tables), "rel_bias",
        grid=(HEAD_PAIRS,),
        in_specs=[pl.BlockSpec((None, 2, BIAS_PERIOD), lambda p: (p, 0, 0))] * len(tables),
        out_specs=out_specs,
        out_shape=out_shape,
    )(*gens)


def _band_block_kinds():
    n_k = KEY_BLOCKS * TILE
    r = np.arange(TILE)[None, :]
    w = np.arange(n_k)[:, None]
    j = w - CHUNK * (r // CHUNK)
    ok = (j >= 0) & (j < BAND_PAST + CHUNK)
    plain = ok & (BAND_PAST + r - w >= REL_CLIP)
    kinds = []
    for a in range(n_k // KEY_BLOCK):
        rows = slice(a * KEY_BLOCK, (a + 1) * KEY_BLOCK)
        blocks = [(rows, slice(b * LANES, (b + 1) * LANES)) for b in range(TILE // LANES)]
        kinds.append(["skip" if not ok[blk].any() else "plain" if plain[blk].all() else "biased"
                      for blk in blocks])
    return kinds


def _band_prompt_kernel(x_ref, qt_ref, qm_ref, kp_ref, kc_ref, vtp_ref, vtc_ref,
                        bias_ref, mk_ref, mv_ref, wout_ref, gpost_ref, o_ref, *, fill_steps):
    i = pl.program_id(0) - fill_steps
    n_k = KEY_BLOCKS * TILE
    kinds = _band_block_kinds()
    w = lax.broadcasted_iota(jnp.int32, (n_k, LANES), 0)
    c = lax.broadcasted_iota(jnp.int32, (n_k, LANES), 1)
    ones_rows = jnp.where(lax.broadcasted_iota(jnp.int32, (BF16_ROWS, n_k), 0) == 0, 1.0, 0.0).astype(BF16)
    row = lax.broadcasted_iota(jnp.int32, (LANES, TILE), 0)
    one_hot_row = jnp.where(row == 0, 1.0, 0.0).astype(BF16)
    lo = row < HEAD_DIM

    def key_step(a, s, vth, bias_rows, state):
        blocks = range(a * KEY_STEP // KEY_BLOCK, (a + 1) * KEY_STEP // KEY_BLOCK)
        es, alphas, active = [], [], []
        for b in range(TILE // LANES):
            cols = slice(b * LANES, (b + 1) * LANES)
            sjs = {}
            for blk in blocks:
                if kinds[blk][b] != "skip":
                    rows = slice(blk * KEY_BLOCK, (blk + 1) * KEY_BLOCK)
                    sjs[blk] = s[rows, cols] + bias_rows(rows, cols) if kinds[blk][b] == "biased" else s[rows, cols]
            active.append(bool(sjs))
            alphas.append(None)
            if not sjs:
                es.append(jnp.zeros((KEY_STEP, LANES), F32))
                continue
            mj = jnp.max(functools.reduce(jnp.maximum, sjs.values()), axis=0, keepdims=True)
            if state[b] is None:
                state[b] = [mj, None]
            else:
                m_new = jnp.maximum(state[b][0], mj)
                alphas[b] = jnp.exp(state[b][0] - m_new)
                state[b][0] = m_new
            es.append(jnp.concatenate(
                [jnp.exp(sjs[blk] - state[b][0]) if blk in sjs else jnp.zeros((KEY_BLOCK, LANES), F32)
                 for blk in blocks], axis=0))
        rows = slice(a * KEY_STEP, (a + 1) * KEY_STEP)
        pv = _dot(vth[:, rows], jnp.concatenate(es, axis=1).astype(BF16))[:OUT_ROWS]
        for b in range(TILE // LANES):
            if active[b]:
                pv_b = pv[:, b * LANES:(b + 1) * LANES]
                state[b][1] = pv_b if alphas[b] is None else state[b][1] * alphas[b] + pv_b

    def tile(t):
        q_rows = slice(t * TILE, (t + 1) * TILE)
        k_rows = slice(t * TILE, t * TILE + n_k)
        first_key_tile = i * BAND_TILES + t - (KEY_BLOCKS - 1)
        pen = jnp.where((w < -first_key_tile * TILE) & (c == 0), NEG, 0.0).astype(BF16)
        outs = []
        for p0 in range(0, HEAD_PAIRS, GROUP_PAIRS):
            heads = []
            for p in range(p0, p0 + GROUP_PAIRS):
                k_win = jnp.concatenate([kp_ref[p], kc_ref[p]], axis=0)[k_rows]
                k_ext = jnp.concatenate([k_win, pen], axis=1)
                vt = jnp.concatenate([vtp_ref[p], vtc_ref[p]], axis=1)[:, k_rows]
                qt = qt_ref[p, :, q_rows]
                zero = jnp.zeros_like(qt)
                for hh in range(2):
                    qh = jnp.where(lo, qt, zero) if hh == 0 else jnp.where(lo, zero, qt)
                    s = _dot(k_ext, jnp.concatenate([qh, one_hot_row], axis=0))
                    vth = jnp.concatenate([vt[hh * HEAD_DIM:(hh + 1) * HEAD_DIM, :], ones_rows], axis=0)
                    bias_rows = functools.partial(
                        lambda rows, cols, p, base: bias_ref[p, base + rows.start:base + rows.stop, cols],
                        p=p, base=hh * n_k)
                    heads.append((s, vth, bias_rows, [None] * (TILE // LANES)))
            for a in range(n_k // KEY_STEP):
                for head in heads:
                    key_step(a, *head)
            outs += [jnp.concatenate([o[:HEAD_DIM] * (1.0 / o[HEAD_DIM:HEAD_DIM + 1]) for _, o in state], axis=1)
                     for *_, state in heads]
            yield
        band = jnp.concatenate(outs, axis=0).T.astype(BF16)
        mo = _mem_attend(qm_ref[q_rows, :], mk_ref[0].astype(BF16), mv_ref[0].astype(BF16))
        cat = jnp.concatenate([band, mo.astype(BF16)], axis=-1)
        o_ref[q_rows, :] = x_ref[q_rows, :] + _rms(_dot(cat, wout_ref[...]), gpost_ref[...])

    _after_fill(o_ref, fill_steps, lambda: _interleave([tile(t) for t in range(BAND_TILES)]))


def _band_prompt(x, qt3, qm, k3, vt3, bias, mkt, mvt, wout, gpost):
    n = qm.shape[0]
    rows = BAND_TILES * TILE
    assert rows == BAND_PAST and n % rows == 0 and (x.shape[0] - n) % rows == 0
    fill = (x.shape[0] - n) // rows
    slab_spec = pl.BlockSpec((rows, D_MODEL), lambda i: (i, 0))
    own = lambda i: jnp.maximum(i - fill, 0)
    past = lambda i: jnp.maximum(i - fill - 1, 0)
    return _call(
        functools.partial(_band_prompt_kernel, fill_steps=fill), "band_prompt",
        grid=(n // rows + fill,),
        in_specs=[slab_spec,
                  pl.BlockSpec((HEAD_PAIRS, LANES, rows), lambda i: (0, 0, own(i))),
                  pl.BlockSpec((rows, MEM_W), lambda i: (own(i), 0)),
                  pl.BlockSpec((HEAD_PAIRS, rows, LANES), lambda i: (0, past(i), 0)),
                  pl.BlockSpec((HEAD_PAIRS, rows, LANES), lambda i: (0, own(i), 0)),
                  pl.BlockSpec((HEAD_PAIRS, LANES, rows), lambda i: (0, 0, past(i))),
                  pl.BlockSpec((HEAD_PAIRS, LANES, rows), lambda i: (0, 0, own(i))),
                  _const_spec((HEAD_PAIRS, 2 * KEY_BLOCKS * TILE, TILE)),
                  _mem_spec(mkt, 1, 1), _mem_spec(mvt, 1, 1),
                  _const_spec((B_W + MEM_W, D_MODEL)), _const_spec((1, D_MODEL))],
        out_specs=slab_spec,
        out_shape=jax.ShapeDtypeStruct(x.shape, F32),
    )(x, qt3, qm, k3, k3, vt3, vt3, bias, mkt, mvt, wout, gpost)


def _band_sample_kernel(x_ref, q_ref, qm_ref, kn_ref, vn_ref, ck_ref, cv_ref, bc_ref, bn_ref,
                        mk_ref, mv_ref, wout_ref, gpost_ref, o_ref, *, seqs, seq_len):
    lane = lax.broadcasted_iota(jnp.int32, (1, LANES), 1)
    lo = lane < HEAD_DIM

    def pair_attend(b, p, out):
        rows = slice(b * seq_len, (b + 1) * seq_len)
        hd = slice(p * LANES, (p + 1) * LANES)
        qp = q_ref[p, rows, :]
        zero = jnp.zeros_like(qp)
        qs = jnp.concatenate([jnp.where(lo, qp, zero), jnp.where(lo, zero, qp)], axis=0)
        sc = _dot(qs, ck_ref[b, hd, :].astype(BF16)) + bc_ref[p]
        sn = _dot_nt(qs, kn_ref[p, rows, :]) + bn_ref[p]
        yield
        m = jnp.maximum(_lmax(sc), _lmax(sn))
        ec = jnp.exp(sc - m)
        en = jnp.exp(sn - m)
        l = _lsum(ec) + _lsum(en)
        yield
        o = (_dot_nt(ec.astype(BF16), cv_ref[b, hd, :].astype(BF16))
             + _dot(en.astype(BF16), vn_ref[p, rows, :]))
        yield
        o = o * (1.0 / l)
        out[p] = jnp.where(lo, o[:seq_len], o[seq_len:]).astype(BF16)

    def seq_attend(b, out):
        band = [None] * HEAD_PAIRS
        yield from _round_robin([pair_attend(b, p, band) for p in range(HEAD_PAIRS)])
        rows = slice(b * seq_len, (b + 1) * seq_len)
        mo = _mem_attend(qm_ref[rows, :], mk_ref[b].astype(BF16), mv_ref[b].astype(BF16))
        out[b] = jnp.concatenate(band + [mo.astype(BF16)], axis=-1)

    rows_out = [None] * seqs
    _interleave([seq_attend(b, rows_out) for b in range(seqs)])
    cat = jnp.concatenate(rows_out, axis=0)
    o_ref[...] = x_ref[...] + _rms(_dot(cat, wout_ref[...]), gpost_ref[...])


def _band_sample(x, q3, qm, kn3, vn3, ckt, cvt, bias_c, bias_n, mkt, mvt, wout, gpost, *, into, seqs=4):
    n = qm.shape[0]
    n_seq, past = ckt.shape[0], ckt.shape[2]
    seq_len = n // n_seq
    rows = seqs * seq_len
    slab_spec = pl.BlockSpec((rows, D_MODEL), lambda i: (i, 0))
    pair_rows_spec = pl.BlockSpec((HEAD_PAIRS, rows, LANES), lambda i: (0, i, 0))
    cache_spec = pl.BlockSpec((seqs, B_W, past), lambda i: (i, 0, 0))
    return _call(
        functools.partial(_band_sample_kernel, seqs=seqs, seq_len=seq_len), "band_sample", into=into,
        grid=(n_seq // seqs,),
        in_specs=[slab_spec,
                  pair_rows_spec,
                  pl.BlockSpec((rows, MEM_W), lambda i: (i, 0)),
                  pair_rows_spec, pair_rows_spec,
                  cache_spec, cache_spec,
                  _const_spec((HEAD_PAIRS, 2 * seq_len, past)),
                  _const_spec((HEAD_PAIRS, 2 * seq_len, seq_len)),
                  _mem_spec(mkt, seqs, 1), _mem_spec(mvt, seqs, 1),
                  _const_spec((B_W + MEM_W, D_MODEL)), _const_spec((1, D_MODEL))],
        out_specs=slab_spec,
        out_shape=jax.ShapeDtypeStruct(x.shape, F32),
    )(x, q3, qm, kn3, vn3, ckt, cvt, bias_c, bias_n, mkt, mvt, wout, gpost)


def _spatial_tile(w_s, b_s, period):
    tril = jnp.tril(jnp.ones((GM_CHUNK, GM_CHUNK), dtype=bool))
    w = jnp.where(tril, w_s, jnp.zeros((), w_s.dtype))[:, :period, :period]
    eye = jnp.eye(TILE // period, dtype=w.dtype)
    s_mat = jnp.einsum("ab,gts->gatbs", eye, w).reshape(GM_GROUPS, TILE, TILE)
    rows = jnp.tile(b_s[:, :period], (1, TILE // period))
    bs = jnp.repeat(rows.T, GM_GW, axis=1)
    return s_mat.astype(BF16), bs


def _heads_last(t, lead):
    pos = t.shape[-1]
    t = t.reshape(lead + (-1, HEAD_DIM, pos))
    nd = len(lead)
    return jnp.transpose(t, tuple(range(nd)) + (nd + 2, nd, nd + 1))


def _positions_last(c):
    nd = c.ndim
    t = jnp.transpose(c, tuple(range(nd - 3)) + (nd - 2, nd - 1, nd - 3))
    return t.reshape(c.shape[:-3] + (c.shape[-2] * c.shape[-1], c.shape[-3]))


def kernel(x_prompt, x_sample, cache_mem_k, cache_mem_v, cache_band_k, cache_band_v, mem_prompt,
           g_mix_pre, g_mix_post, g_ffn_pre, g_ffn_post, g_mem, w_mem_kv,
           w_in_a, g_gm_ln, b_gm_ln, w_spatial, b_spatial, w_out_a,
           g_kv, w_kv, w_in_b, rel_bias, w_out_b, w_ff1, w_ff2):
    seq = x_prompt.shape[1]
    n_seq, seq_len = x_sample.shape[0], x_sample.shape[1]
    past = cache_band_k.shape[1]
    vec = lambda a: a.reshape(1, -1)
    stack = lambda a: a.reshape(DEPTH, 1, -1)

    win_a = w_in_a[0].astype(BF16)
    wout_a = w_out_a[0].astype(BF16)
    wk = w_kv[:, :B_W].astype(BF16)
    wv = w_kv[:, B_W:].astype(BF16)
    win_b = w_in_b[0].astype(BF16)
    wout_b = w_out_b[0].astype(BF16)
    ln_g, ln_b = vec(g_gm_ln[0]), vec(b_gm_ln[0])
    gkv, gpre_b = vec(g_kv), vec(g_mix_pre[1])
    gf_pre, gf_post = stack(g_ffn_pre), stack(g_ffn_post)

    n_sample = n_seq * seq_len
    slab_rows = seq + n_sample
    pre_a, post_a = vec(g_mix_pre[0]), vec(g_mix_post[0])

    mem_kt, mem_vt = _memkv(mem_prompt[0], g_mem, w_mem_kv)
    s_p, bs_p = _spatial_tile(w_spatial[0], b_spatial[0], GM_CHUNK)
    s_s, bs_s = _spatial_tile(w_spatial[0], b_spatial[0], seq_len)
    cmkt, cmvt = _positions_last(cache_mem_k), _positions_last(cache_mem_v)
    x, w1, w2 = _mixer_a(x_prompt[0], pre_a, post_a, win_a, ln_g, ln_b, s_p, bs_p, mem_kt, mem_vt, wout_a,
                         rows_per_mem=TILE, emit_v=False, tiles=4, slab_rows=slab_rows, slab_offset=n_sample,
                         cast=((w_ff1, 0), (w_ff2, 0)))
    x, v_rows = _mixer_a(x_sample.reshape(n_sample, D_MODEL), pre_a, post_a, win_a, ln_g, ln_b, s_s, bs_s,
                         cmkt, cmvt, wout_a, rows_per_mem=seq_len, emit_v=True,
                         slab_rows=slab_rows, into=x)
    x = _ffn(x, gf_pre, gf_post, w1, w2, 0)

    qt3, qm, k3, vt3, w1, w2 = _proj_b(x, gkv, gpre_b, (win_b, wk, wv), mode="prompt", rows=2 * PROJ_ROWS,
                                       row_offset=n_sample, n_rows=seq, cast=((w_ff1, 1), (w_ff2, 1)))
    n_keep = min(BAND_PAST, seq)
    q3s, qms, kn3, vn3, k_new, v_new, kt_tail, vt_tail = _proj_b(
        x, gkv, gpre_b, (win_b, wk, wv), mode="sample", rows=512, n_rows=n_sample, seq_len=seq_len,
        tail_offset=n_sample + seq - n_keep, tail_rows=n_keep)
    bias_p, bias_c, bias_n = _rel_bias_tables(rel_bias[0], (
        (TILE, BAND_PAST + TILE, ((0, TILE),), True, True),
        (seq_len, past + seq_len, ((0, past), (past, past + seq_len)), False, False)))
    post_b = vec(g_mix_post[1])
    y = _band_prompt(x, qt3, qm, k3, vt3, bias_p, mem_kt, mem_vt, wout_b, post_b)
    y = _band_sample(x, q3s, qms, kn3, vn3, _positions_last(cache_band_k), _positions_last(cache_band_v),
                     bias_c, bias_n, cmkt, cmvt, wout_b, post_b, into=y)
    y_sample, y_prompt = _ffn(y, gf_pre, gf_post, w1, w2, 1, split=n_sample)
    y_prompt = y_prompt[None]
    y_sample = y_sample.reshape(n_seq, seq_len, D_MODEL)

    return (y_prompt, y_sample,
            _heads_last(mem_kt, (DEPTH, 1)), _heads_last(mem_vt, (DEPTH, 1)),
            _heads_last(kt_tail, (1,)), _heads_last(vt_tail, (1,)),
            v_rows.reshape(1, n_seq, seq_len, GM_W),
            jnp.swapaxes(k_new, 1, 2), jnp.swapaxes(v_new, 1, 2))
```

```python
import functools

import jax
import jax.numpy as jnp
import numpy as np
from jax import lax
from jax.experimental import pallas as pl
from jax.experimental.pallas import tpu as pltpu

D_MODEL = 1024
DEPTH = 2
CHUNK = 64
HEAD_DIM = 64
GM_CHUNK = 128
GM_GROUPS = 4
GM_W = 768
GM_GW = GM_W // GM_GROUPS
MEM_LEN = 256
MEM_HEADS = 4
MEM_W = MEM_HEADS * HEAD_DIM
B_HEADS = 12
B_W = B_HEADS * HEAD_DIM
BAND_PAST = 512
REL_CLIP = 128
D_FF = 4 * D_MODEL
EPS = 1e-6

LANES = 128
SUBLANES = 8
HEAD_PAIRS = B_W // LANES
Q_SCALE = HEAD_DIM ** -0.5
NEG = -1e30
TILE = 256
KEY_BLOCKS = BAND_PAST // TILE + 1
KEY_STEP = 128
KEY_BLOCK = 128
PROJ_ROWS = 512
BF16_ROWS = 2 * SUBLANES
OUT_ROWS = HEAD_DIM + SUBLANES
BAND_TILES = BAND_PAST // TILE
GROUP_PAIRS = 2
BIAS_PERIOD = 1024
V7X_VMEM_BYTES = 64 * 1024 * 1024
VMEM_LIMIT = V7X_VMEM_BYTES * 7 // 8

BF16 = jnp.bfloat16
F32 = jnp.float32


def _dot(a, b):
    return jnp.dot(a, b, preferred_element_type=F32)


def _dot_nt(a, b):
    return lax.dot_general(a, b, (((1,), (1,)), ((), ())), preferred_element_type=F32)


def _rms(x, g):
    ms = jnp.mean(x * x, axis=-1, keepdims=True)
    return x * lax.rsqrt(ms + EPS) * g


def _lsum(a):
    return jnp.sum(a, axis=-1, keepdims=True)


def _lmax(a):
    return jnp.max(a, axis=-1, keepdims=True)


def _const_spec(shape):
    nd = len(shape)
    return pl.BlockSpec(shape, lambda *_: (0,) * nd, pipeline_mode=pl.Buffered(1))


def _layer_spec(shape, layer):
    nd = len(shape)
    return pl.BlockSpec((None,) + shape, lambda *_: (layer,) + (0,) * nd, pipeline_mode=pl.Buffered(1))


def _without_ref(body, k):
    def wrapped(*refs):
        return body(*refs[:k], *refs[k + 1:])
    return wrapped


def _call(body, name, *, in_specs, into=None, semantics="parallel", **kw):
    params = pltpu.CompilerParams(dimension_semantics=(semantics,), vmem_limit_bytes=VMEM_LIMIT)
    if into is None:
        return pl.pallas_call(body, name=name, in_specs=in_specs, compiler_params=params, **kw)
    n_in = len(in_specs)
    call = pl.pallas_call(_without_ref(body, n_in), name=name,
                          in_specs=[*in_specs, pl.BlockSpec(memory_space=pl.ANY)],
                          input_output_aliases={n_in: 0}, compiler_params=params, **kw)
    return lambda *args: call(*args, into)


def _memkv_kernel(mem_ref, g_ref, w_ref, kt_ref, vt_ref):
    h = _dot(_rms(mem_ref[...], g_ref[...]).astype(BF16), w_ref[...].astype(BF16))
    kt_ref[0] = h[:, :MEM_W].T
    vt_ref[0] = h[:, MEM_W:].T


def _memkv(mem, g_mem, w_mem_kv):
    out = jax.ShapeDtypeStruct((DEPTH, 1, MEM_W, MEM_LEN), F32)
    return _call(
        _memkv_kernel, "mem_kv",
        grid=(DEPTH,),
        in_specs=[
            pl.BlockSpec((MEM_LEN, D_MODEL), lambda l: (0, 0)),
            pl.BlockSpec((None, 1, D_MODEL), lambda l: (l, 0, 0)),
            pl.BlockSpec((None, D_MODEL, 2 * MEM_W), lambda l: (l, 0, 0)),
        ],
        out_specs=[pl.BlockSpec((None, 1, MEM_W, MEM_LEN), lambda l: (l, 0, 0, 0))] * 2,
        out_shape=[out, out],
    )(mem, g_mem.reshape(DEPTH, 1, D_MODEL), w_mem_kv)


def _mem_attend(qb, kt, vt):
    r = qb.shape[0]
    lane = lax.broadcasted_iota(jnp.int32, (1, MEM_W), 1)
    masks = [(lane >= h * HEAD_DIM) & (lane < (h + 1) * HEAD_DIM) for h in range(MEM_HEADS)]
    qs = jnp.concatenate([jnp.where(m, qb, jnp.zeros_like(qb)) for m in masks], axis=0)
    s = _dot(qs, kt)
    e = jnp.exp(s - _lmax(s))
    pv = _dot_nt(e.astype(BF16), vt) * (1.0 / _lsum(e))
    out = jnp.where(masks[0], pv[:r], 0.0)
    for h in range(1, MEM_HEADS):
        out = out + jnp.where(masks[h], pv[h * r:(h + 1) * r], 0.0)
    return out


def _gelu(x):
    c1 = float(np.sqrt(2.0 / np.pi))
    c2 = c1 * 0.044715
    half = 0.5 * x
    return half + half * jnp.tanh(x * (c1 + c2 * (x * x)))


def _cast_specs(cast, steps, fill):
    own = lambda i: jnp.maximum(i - fill, 0)
    in_specs, out_specs, out_shape = [], [], []
    for a, layer in cast:
        _, r, c = a.shape
        in_specs.append(pl.BlockSpec((None, r // steps, c), functools.partial(lambda i, l: (l, own(i), 0), l=layer)))
        out_specs.append(pl.BlockSpec((r // steps, c), lambda i: (own(i), 0)))
        out_shape.append(jax.ShapeDtypeStruct((r, c), BF16))
    return in_specs, out_specs, out_shape


def _split_cast_refs(refs, n_cast):
    return refs[:n_cast], refs[n_cast:len(refs) - n_cast], refs[len(refs) - n_cast:]


def _cast_blocks(srcs, dsts, stages=4):
    for src, dst in zip(srcs, dsts):
        step = src.shape[0] // stages
        for k in range(stages):
            dst[k * step:(k + 1) * step, :] = src[k * step:(k + 1) * step, :].astype(dst.dtype)
            yield


def _after_fill(o_ref, fill_steps, body):
    if not fill_steps:
        return body()

    @pl.when(pl.program_id(0) < fill_steps)
    def _():
        o_ref[...] = jnp.zeros(o_ref.shape, o_ref.dtype)

    @pl.when(pl.program_id(0) >= fill_steps)
    def _():
        body()


def _round_robin(stage_lists):
    live = list(stage_lists)
    while live:
        for g in list(live):
            if next(g, StopIteration) is StopIteration:
                live.remove(g)
        yield


def _interleave(stage_lists):
    for _ in _round_robin(stage_lists):
        pass


def _mixer_a_kernel(x_ref, gpre_ref, gpost_ref, win_ref, gln_ref, bln_ref, s_ref, bs_ref,
                    mk_ref, mv_ref, wout_ref, *rest, rows_per_mem, tiles, shared_mem, fill_steps, n_cast):
    cast_in, (o_ref, *v_out), cast_out = _split_cast_refs(rest, n_cast)
    nt = GM_W // LANES
    seqs = TILE // rows_per_mem
    lane = lax.broadcasted_iota(jnp.int32, (1, LANES), 1)
    lo = lane < (GM_GW - LANES)
    inv = 1.0 / GM_GW

    def group_stat(a):
        s0 = _lsum(a[0] + jnp.where(lo, a[1], 0.0)) * inv
        s1 = _lsum(jnp.where(lo, 0.0, a[1]) + a[2]) * inv
        s2 = _lsum(a[3] + jnp.where(lo, a[4], 0.0)) * inv
        s3 = _lsum(jnp.where(lo, 0.0, a[4]) + a[5]) * inv
        return [s0, jnp.where(lo, s0, s1), s1, s2, jnp.where(lo, s2, s3), s3]

    def tile(t):
        rows = slice(t * TILE, (t + 1) * TILE)
        z = _dot(_rms(x_ref[rows, :], gpre_ref[...]).astype(BF16), win_ref[...])
        yield
        u = [_gelu(z[:, j * LANES:(j + 1) * LANES]) for j in range(nt)]
        g = [_gelu(z[:, GM_W + j * LANES:GM_W + (j + 1) * LANES]) for j in range(nt)]
        mu = group_stat(g)
        c = [g[j] - mu[j] for j in range(nt)]
        var = group_stat([cj * cj for cj in c])
        gln = gln_ref[...]
        bln = bln_ref[...]
        vn = [c[j] * lax.rsqrt(var[j] + EPS) * gln[:, j * LANES:(j + 1) * LANES]
              + bln[:, j * LANES:(j + 1) * LANES] for j in range(nt)]
        if v_out:
            v_out[0][rows, :] = jnp.concatenate(vn, axis=-1)
        yield
        vb = [a.astype(BF16) for a in vn]
        win = [(0, 1), (1, 2), (3, 4), (4, 5)]
        m = [_dot(s_ref[k], jnp.concatenate([vb[a], vb[b]], axis=-1)) for k, (a, b) in enumerate(win)]
        mixed = [m[0][:, :LANES], jnp.where(lo, m[0][:, LANES:], m[1][:, :LANES]), m[1][:, LANES:],
                 m[2][:, :LANES], jnp.where(lo, m[2][:, LANES:], m[3][:, :LANES]), m[3][:, LANES:]]
        bs = bs_ref[...]
        gm = [u[j] * (mixed[j] + bs[:, j * LANES:(j + 1) * LANES]) for j in range(nt)]
        yield
        qm = (z[:, 2 * GM_W:] * Q_SCALE).astype(BF16)
        mo = []
        for b in range(seqs):
            r = slice(b * rows_per_mem, (b + 1) * rows_per_mem)
            mi = 0 if shared_mem else t * seqs + b
            mo.append(_mem_attend(qm[r], mk_ref[mi].astype(BF16), mv_ref[mi].astype(BF16)))
        mo = mo[0] if len(mo) == 1 else jnp.concatenate(mo, axis=0)
        yield
        cat = jnp.concatenate([a.astype(BF16) for a in gm] + [mo.astype(BF16)], axis=-1)
        o_ref[rows, :] = x_ref[rows, :] + _rms(_dot(cat, wout_ref[...]), gpost_ref[...])

    def body():
        _interleave([tile(t) for t in range(tiles)] + [_cast_blocks(cast_in, cast_out)])

    _after_fill(o_ref, fill_steps, body)


def _mem_spec(mem, seqs, layer):
    if mem.shape[1] == 1:
        return pl.BlockSpec((None, 1, MEM_W, MEM_LEN), lambda i: (layer, 0, 0, 0))
    return pl.BlockSpec((None, seqs, MEM_W, MEM_LEN), lambda i: (layer, i, 0, 0))


def _mixer_a(x, gpre, gpost, win, gln, bln, s_mat, bs, mkt, mvt, wout, *, rows_per_mem, emit_v, tiles=2,
             slab_rows, slab_offset=0, into=None, cast=()):
    n = x.shape[0]
    blk = tiles * TILE
    seqs = blk // rows_per_mem
    off = slab_offset // blk
    fill, shift = (0, off) if into is not None else (off, 0)
    row = lambda w: pl.BlockSpec((blk, w), lambda i: (jnp.maximum(i - fill, 0), 0))
    out_shape = [jax.ShapeDtypeStruct((slab_rows, D_MODEL), F32)]
    out_specs = [pl.BlockSpec((blk, D_MODEL), lambda i: (i + shift, 0))]
    if emit_v:
        out_shape.append(jax.ShapeDtypeStruct((n, GM_W), F32))
        out_specs.append(row(GM_W))
    cast_in_specs, cast_out_specs, cast_out_shape = _cast_specs(cast, n // blk, fill)
    return _call(
        functools.partial(_mixer_a_kernel, rows_per_mem=rows_per_mem, tiles=tiles,
                          shared_mem=mkt.shape[1] == 1, fill_steps=fill, n_cast=len(cast)), "mixer_a", into=into,
        semantics="arbitrary" if fill and cast else "parallel",
        grid=(n // blk + fill,),
        in_specs=[
            row(D_MODEL),
            _const_spec((1, D_MODEL)), _const_spec((1, D_MODEL)),
            _const_spec((D_MODEL, 2 * GM_W + MEM_W)),
            _const_spec((1, GM_W)), _const_spec((1, GM_W)),
            _const_spec((GM_GROUPS, TILE, TILE)), _const_spec((TILE, GM_W)),
            _mem_spec(mkt, seqs, 0), _mem_spec(mvt, seqs, 0),
            _const_spec((GM_W + MEM_W, D_MODEL)),
            *cast_in_specs,
        ],
        out_specs=out_specs + cast_out_specs,
        out_shape=out_shape + cast_out_shape,
    )(x, gpre, gpost, win, gln, bln, s_mat, bs, mkt, mvt, wout, *[a for a, _ in cast])


def _ffn_kernel(x_ref, gpre_ref, gpost_ref, w1_ref, w2_ref, o_ref, *o_tail, tile_rows, ff_chunk, head_steps):
    def tile(t):
        r = slice(sum(tile_rows[:t]), sum(tile_rows[:t + 1]))
        x = x_ref[r, :]
        xn = _rms(x, gpre_ref[...]).astype(BF16)
        acc = jnp.zeros(x.shape, F32)
        yield
        for c in range(D_FF // ff_chunk):
            h = _dot(xn, w1_ref[:, c * ff_chunk:(c + 1) * ff_chunk])
            h = jnp.square(jnp.maximum(h, 0.0)).astype(BF16)
            acc = acc + _dot(h, w2_ref[c * ff_chunk:(c + 1) * ff_chunk, :])
            yield
        y = x + _rms(acc, gpost_ref[...])
        if not o_tail:
            o_ref[r, :] = y
        else:
            is_head = pl.program_id(0) < head_steps
            o_ref[r, :] = jnp.where(is_head, y, o_ref[r, :])
            o_tail[0][r, :] = y

    if o_tail:
        @pl.when(pl.program_id(0) < head_steps)
        def _():
            o_ref[...] = jnp.zeros(o_ref.shape, F32)

    _interleave([tile(t) for t in range(len(tile_rows))])


def _ffn(x, gpre, gpost, w1, w2, layer, *, tile_rows=(256, 512, 256), ff_chunk=1024, split=None):
    n = x.shape[0]
    blk = sum(tile_rows)
    row = pl.BlockSpec((blk, D_MODEL), lambda i: (i, 0))
    if split is None:
        head_steps, out_specs, out_shape = None, row, jax.ShapeDtypeStruct((n, D_MODEL), F32)
    else:
        head_steps = split // blk
        out_specs = [pl.BlockSpec((blk, D_MODEL), lambda i: (jnp.minimum(i, head_steps - 1), 0)),
                     pl.BlockSpec((blk, D_MODEL), lambda i: (jnp.maximum(i - head_steps, 0), 0))]
        out_shape = [jax.ShapeDtypeStruct((split, D_MODEL), F32), jax.ShapeDtypeStruct((n - split, D_MODEL), F32)]
    return _call(
        functools.partial(_ffn_kernel, tile_rows=tile_rows, ff_chunk=ff_chunk, head_steps=head_steps),
        "ffn", semantics="arbitrary",
        grid=(n // blk,),
        in_specs=[row, _layer_spec((1, D_MODEL), layer), _layer_spec((1, D_MODEL), layer),
                  _const_spec((D_MODEL, D_FF)), _const_spec((D_FF, D_MODEL))],
        out_specs=out_specs,
        out_shape=out_shape,
    )(x, gpre, gpost, w1, w2)


def _proj_b_kernel(x_ref, gkv_ref, gpre_ref, *refs, mode, seq_len, n_weights, n_cast):
    rows = x_ref.shape[0]
    cast_in, outs, cast_out = _split_cast_refs(refs[n_weights:], n_cast)
    refs = (*refs[:n_weights], *outs)

    def normed(r, src=x_ref):
        x = src[r, :]
        xh = x * lax.rsqrt(jnp.mean(x * x, axis=-1, keepdims=True) + EPS)
        return (xh * gkv_ref[...]).astype(BF16), (xh * gpre_ref[...]).astype(BF16)

    if mode == "prompt":
        win_ref, wk_ref, wv_ref, qt_ref, qm_ref, k_ref, vt_ref = refs

        def tile(t):
            r = slice(t * PROJ_ROWS, (t + 1) * PROJ_ROWS)
            xkv, xq = normed(r)
            yield
            z = _dot(xq, win_ref[...]) * Q_SCALE
            qm_ref[r, :] = z[:, B_W:].astype(BF16)
            for p in range(HEAD_PAIRS):
                qt_ref[p, :, r] = z[:, p * LANES:(p + 1) * LANES].T.astype(BF16)
            yield
            v = _dot(xkv, wv_ref[...])
            for p in range(HEAD_PAIRS):
                vt_ref[p, :, r] = v[:, p * LANES:(p + 1) * LANES].T.astype(BF16)
            yield
            k = _dot(xkv, wk_ref[...])
            for p in range(HEAD_PAIRS):
                k_ref[p, r, :] = k[:, p * LANES:(p + 1) * LANES].astype(BF16)

        _interleave([tile(t) for t in range(rows // PROJ_ROWS)] + [_cast_blocks(cast_in, cast_out)])
        return
    win_ref, wk_ref, wv_ref, xt_ref, q_ref, qm_ref, k_ref, v_ref, k4_ref, v4_ref, kt_ref, vt_ref = refs
    xkv_tail, _ = normed(slice(None), xt_ref)
    kt_ref[...] = _dot(xkv_tail, wk_ref[...]).T
    vt_ref[...] = _dot(xkv_tail, wv_ref[...]).T
    xkv, xq = normed(slice(None))
    z = _dot(xq, win_ref[...]) * Q_SCALE
    qm_ref[...] = z[:, B_W:].astype(BF16)
    k = _dot(xkv, wk_ref[...])
    v = _dot(xkv, wv_ref[...])
    for p in range(HEAD_PAIRS):
        cols = slice(p * LANES, (p + 1) * LANES)
        q_ref[p] = z[:, cols].astype(BF16)
        k_ref[p] = k[:, cols].astype(BF16)
        v_ref[p] = v[:, cols].astype(BF16)
    for b in range(rows // seq_len):
        for h in range(B_HEADS):
            r, c = slice(b * seq_len, (b + 1) * seq_len), slice(h * HEAD_DIM, (h + 1) * HEAD_DIM)
            k4_ref[b, h] = k[r, c]
            v4_ref[b, h] = v[r, c]


def _proj_b(x, gkv, gpre, weights, *, mode, rows, row_offset=0, n_rows=None, seq_len=None, cast=(),
            tail_offset=None, tail_rows=None):
    n = x.shape[0] if n_rows is None else n_rows
    off = row_offset // rows
    steps = n // rows
    pair_rows = jax.ShapeDtypeStruct((HEAD_PAIRS, n, LANES), BF16)
    pair_rows_spec = pl.BlockSpec((HEAD_PAIRS, rows, LANES), lambda i: (0, i, 0))
    pair_cols = jax.ShapeDtypeStruct((HEAD_PAIRS, LANES, n), BF16)
    pair_cols_spec = pl.BlockSpec((HEAD_PAIRS, LANES, rows), lambda i: (0, 0, i))
    qm = jax.ShapeDtypeStruct((n, MEM_W), BF16)
    qm_spec = pl.BlockSpec((rows, MEM_W), lambda i: (i, 0))
    extra_in, extra_specs = [], []
    if mode == "prompt":
        out_shape = [pair_cols, qm, pair_rows, pair_cols]
        out_specs = [pair_cols_spec, qm_spec, pair_rows_spec, pair_cols_spec]
    else:
        seqs = rows // seq_len
        per_head = jax.ShapeDtypeStruct((n // seq_len, B_HEADS, seq_len, HEAD_DIM), F32)
        per_head_spec = pl.BlockSpec((seqs, B_HEADS, seq_len, HEAD_DIM), lambda i: (i, 0, 0, 0))
        tail_blk = tail_rows // steps
        tail_off = tail_offset // tail_blk
        extra_in, extra_specs = [x], [pl.BlockSpec((tail_blk, D_MODEL), lambda i: (i + tail_off, 0))]
        out_shape = ([pair_rows, qm, pair_rows, pair_rows, per_head, per_head]
                     + [jax.ShapeDtypeStruct((B_W, tail_rows), F32)] * 2)
        out_specs = ([pair_rows_spec, qm_spec, pair_rows_spec, pair_rows_spec, per_head_spec, per_head_spec]
                     + [pl.BlockSpec((B_W, tail_blk), lambda i: (0, i))] * 2)
    cast_in_specs, cast_out_specs, cast_out_shape = _cast_specs(cast, steps, 0)
    return _call(
        functools.partial(_proj_b_kernel, mode=mode, seq_len=seq_len, n_weights=len(weights) + len(extra_in),
                          n_cast=len(cast)), "proj_b_" + mode,
        grid=(steps,),
        in_specs=[pl.BlockSpec((rows, D_MODEL), lambda i: (i + off, 0)),
                  _const_spec((1, D_MODEL)), _const_spec((1, D_MODEL))]
                 + [_const_spec(w.shape) for w in weights] + extra_specs + cast_in_specs,
        out_specs=out_specs + cast_out_specs,
        out_shape=out_shape + cast_out_shape,
    )(x, gkv, gpre, *weights, *extra_in, *[a for a, _ in cast])


def _bias_kernel(*refs, tables):
    g_refs, o_refs = refs[:len(tables)], list(refs[len(tables):])
    for g_ref, (n_q, n_k, splits, band, keys_on_rows) in zip(g_refs, tables):
        outs = [o_refs.pop(0) for _ in splits]
        n_rows, shift = (n_k, n_q) if keys_on_rows else (n_q, BIAS_PERIOD - (n_q - 1))
        for hh in range(2):
            x = jnp.broadcast_to(g_ref[hh:hh + 1, :], (n_rows, BIAS_PERIOD))
            t = pltpu.roll(x, shift, 1, stride=1, stride_axis=0)
            if band:
                a = lax.broadcasted_iota(jnp.int32, (n_rows, BIAS_PERIOD), 0)
                b = lax.broadcasted_iota(jnp.int32, (n_rows, BIAS_PERIOD), 1)
                r, w = (b, a) if keys_on_rows else (a, b)
                j = w - (r - (r & (CHUNK - 1)))
                t = jnp.where((j >= 0) & (j < BAND_PAST + CHUNK), t, NEG)
            for o_ref, (lo, hi) in zip(outs, splits):
                o_ref[hh * n_rows:(hh + 1) * n_rows, :] = t[:, lo:hi]


def _rel_bias_tables(rel_bias, tables):
    rel_bias = rel_bias - rel_bias[:, -1:]
    gens, out_specs, out_shape = [], [], []
    for n_q, n_k, splits, _, keys_on_rows in tables:
        c0 = n_k - 1
        far = jnp.broadcast_to(rel_bias[:, -1:], (B_HEADS, c0 - REL_CLIP))
        near = jnp.broadcast_to(rel_bias[:, :1], (B_HEADS, BIAS_PERIOD - (c0 - REL_CLIP) - (2 * REL_CLIP + 1)))
        gen = [near, rel_bias, far] if keys_on_rows else [far, rel_bias[:, ::-1], near]
        gens.append(jnp.concatenate(gen, axis=1).reshape(HEAD_PAIRS, 2, BIAS_PERIOD))
        n_rows = n_k if keys_on_rows else n_q
        out_specs += [pl.BlockSpec((None, 2 * n_rows, hi - lo), lambda p: (p, 0, 0)) for lo, hi in splits]
        out_shape += [jax.ShapeDtypeStruct((HEAD_PAIRS, 2 * n_rows, hi - lo), F32) for lo, hi in splits]
    return _call(
        functools.partial(_bias_kernel, tables=tables), "rel_bias",
        grid=(HEAD_PAIRS,),
        in_specs=[pl.BlockSpec((None, 2, BIAS_PERIOD), lambda p: (p, 0, 0))] * len(tables),
        out_specs=out_specs,
        out_shape=out_shape,
    )(*gens)


def _band_block_kinds():
    n_k = KEY_BLOCKS * TILE
    r = np.arange(TILE)[None, :]
    w = np.arange(n_k)[:, None]
    j = w - CHUNK * (r // CHUNK)
    ok = (j >= 0) & (j < BAND_PAST + CHUNK)
    plain = ok & (BAND_PAST + r - w >= REL_CLIP)
    kinds = []
    for a in range(n_k // KEY_BLOCK):
        rows = slice(a * KEY_BLOCK, (a + 1) * KEY_BLOCK)
        blocks = [(rows, slice(b * LANES, (b + 1) * LANES)) for b in range(TILE // LANES)]
        kinds.append(["skip" if not ok[blk].any() else "plain" if plain[blk].all() else "biased"
                      for blk in blocks])
    return kinds


def _band_prompt_kernel(x_ref, qt_ref, qm_ref, kp_ref, kc_ref, vtp_ref, vtc_ref,
                        bias_ref, mk_ref, mv_ref, wout_ref, gpost_ref, o_ref, *, fill_steps):
    i = pl.program_id(0) - fill_steps
    n_k = KEY_BLOCKS * TILE
    kinds = _band_block_kinds()
    w = lax.broadcasted_iota(jnp.int32, (n_k, LANES), 0)
    c = lax.broadcasted_iota(jnp.int32, (n_k, LANES), 1)
    ones_rows = jnp.where(lax.broadcasted_iota(jnp.int32, (BF16_ROWS, n_k), 0) == 0, 1.0, 0.0).astype(BF16)
    row = lax.broadcasted_iota(jnp.int32, (LANES, TILE), 0)
    one_hot_row = jnp.where(row == 0, 1.0, 0.0).astype(BF16)
    lo = row < HEAD_DIM

    def key_step(a, s, vth, bias_rows, state):
        blocks = range(a * KEY_STEP // KEY_BLOCK, (a + 1) * KEY_STEP // KEY_BLOCK)
        es, alphas, active = [], [], []
        for b in range(TILE // LANES):
            cols = slice(b * LANES, (b + 1) * LANES)
            sjs = {}
            for blk in blocks:
                if kinds[blk][b] != "skip":
                    rows = slice(blk * KEY_BLOCK, (blk + 1) * KEY_BLOCK)
                    sjs[blk] = s[rows, cols] + bias_rows(rows, cols) if kinds[blk][b] == "biased" else s[rows, cols]
            active.append(bool(sjs))
            alphas.append(None)
            if not sjs:
                es.append(jnp.zeros((KEY_STEP, LANES), F32))
                continue
            mj = jnp.max(functools.reduce(jnp.maximum, sjs.values()), axis=0, keepdims=True)
            if state[b] is None:
                state[b] = [mj, None]
            else:
                m_new = jnp.maximum(state[b][0], mj)
                alphas[b] = jnp.exp(state[b][0] - m_new)
                state[b][0] = m_new
            es.append(jnp.concatenate(
                [jnp.exp(sjs[blk] - state[b][0]) if blk in sjs else jnp.zeros((KEY_BLOCK, LANES), F32)
                 for blk in blocks], axis=0))
        rows = slice(a * KEY_STEP, (a + 1) * KEY_STEP)
        pv = _dot(vth[:, rows], jnp.concatenate(es, axis=1).astype(BF16))[:OUT_ROWS]
        for b in range(TILE // LANES):
            if active[b]:
                pv_b = pv[:, b * LANES:(b + 1) * LANES]
                state[b][1] = pv_b if alphas[b] is None else state[b][1] * alphas[b] + pv_b

    def tile(t):
        q_rows = slice(t * TILE, (t + 1) * TILE)
        k_rows = slice(t * TILE, t * TILE + n_k)
        first_key_tile = i * BAND_TILES + t - (KEY_BLOCKS - 1)
        pen = jnp.where((w < -first_key_tile * TILE) & (c == 0), NEG, 0.0).astype(BF16)
        outs = []
        for p0 in range(0, HEAD_PAIRS, GROUP_PAIRS):
            heads = []
            for p in range(p0, p0 + GROUP_PAIRS):
                k_win = jnp.concatenate([kp_ref[p], kc_ref[p]], axis=0)[k_rows]
                k_ext = jnp.concatenate([k_win, pen], axis=1)
                vt = jnp.concatenate([vtp_ref[p], vtc_ref[p]], axis=1)[:, k_rows]
                qt = qt_ref[p, :, q_rows]
                zero = jnp.zeros_like(qt)
                for hh in range(2):
                    qh = jnp.where(lo, qt, zero) if hh == 0 else jnp.where(lo, zero, qt)
                    s = _dot(k_ext, jnp.concatenate([qh, one_hot_row], axis=0))
                    vth = jnp.concatenate([vt[hh * HEAD_DIM:(hh + 1) * HEAD_DIM, :], ones_rows], axis=0)
                    bias_rows = functools.partial(
                        lambda rows, cols, p, base: bias_ref[p, base + rows.start:base + rows.stop, cols],
                        p=p, base=hh * n_k)
                    heads.append((s, vth, bias_rows, [None] * (TILE // LANES)))
            for a in range(n_k // KEY_STEP):
                for head in heads:
                    key_step(a, *head)
            outs += [jnp.concatenate([o[:HEAD_DIM] * (1.0 / o[HEAD_DIM:HEAD_DIM + 1]) for _, o in state], axis=1)
                     for *_, state in heads]
            yield
        band = jnp.concatenate(outs, axis=0).T.astype(BF16)
        mo = _mem_attend(qm_ref[q_rows, :], mk_ref[0].astype(BF16), mv_ref[0].astype(BF16))
        cat = jnp.concatenate([band, mo.astype(BF16)], axis=-1)
        o_ref[q_rows, :] = x_ref[q_rows, :] + _rms(_dot(cat, wout_ref[...]), gpost_ref[...])

    _after_fill(o_ref, fill_steps, lambda: _interleave([tile(t) for t in range(BAND_TILES)]))


def _band_prompt(x, qt3, qm, k3, vt3, bias, mkt, mvt, wout, gpost):
    n = qm.shape[0]
    rows = BAND_TILES * TILE
    assert rows == BAND_PAST and n % rows == 0 and (x.shape[0] - n) % rows == 0
    fill = (x.shape[0] - n) // rows
    slab_spec = pl.BlockSpec((rows, D_MODEL), lambda i: (i, 0))
    own = lambda i: jnp.maximum(i - fill, 0)
    past = lambda i: jnp.maximum(i - fill - 1, 0)
    return _call(
        functools.partial(_band_prompt_kernel, fill_steps=fill), "band_prompt",
        grid=(n // rows + fill,),
        in_specs=[slab_spec,
                  pl.BlockSpec((HEAD_PAIRS, LANES, rows), lambda i: (0, 0, own(i))),
                  pl.BlockSpec((rows, MEM_W), lambda i: (own(i), 0)),
                  pl.BlockSpec((HEAD_PAIRS, rows, LANES), lambda i: (0, past(i), 0)),
                  pl.BlockSpec((HEAD_PAIRS, rows, LANES), lambda i: (0, own(i), 0)),
                  pl.BlockSpec((HEAD_PAIRS, LANES, rows), lambda i: (0, 0, past(i))),
                  pl.BlockSpec((HEAD_PAIRS, LANES, rows), lambda i: (0, 0, own(i))),
                  _const_spec((HEAD_PAIRS, 2 * KEY_BLOCKS * TILE, TILE)),
                  _mem_spec(mkt, 1, 1), _mem_spec(mvt, 1, 1),
                  _const_spec((B_W + MEM_W, D_MODEL)), _const_spec((1, D_MODEL))],
        out_specs=slab_spec,
        out_shape=jax.ShapeDtypeStruct(x.shape, F32),
    )(x, qt3, qm, k3, k3, vt3, vt3, bias, mkt, mvt, wout, gpost)


def _band_sample_kernel(x_ref, q_ref, qm_ref, kn_ref, vn_ref, ck_ref, cv_ref, bc_ref, bn_ref,
                        mk_ref, mv_ref, wout_ref, gpost_ref, o_ref, *, seqs, seq_len):
    lane = lax.broadcasted_iota(jnp.int32, (1, LANES), 1)
    lo = lane < HEAD_DIM

    def pair_attend(b, p, out):
        rows = slice(b * seq_len, (b + 1) * seq_len)
        hd = slice(p * LANES, (p + 1) * LANES)
        qp = q_ref[p, rows, :]
        zero = jnp.zeros_like(qp)
        qs = jnp.concatenate([jnp.where(lo, qp, zero), jnp.where(lo, zero, qp)], axis=0)
        sc = _dot(qs, ck_ref[b, hd, :].astype(BF16)) + bc_ref[p]
        sn = _dot_nt(qs, kn_ref[p, rows, :]) + bn_ref[p]
        yield
        m = jnp.maximum(_lmax(sc), _lmax(sn))
        ec = jnp.exp(sc - m)
        en = jnp.exp(sn - m)
        l = _lsum(ec) + _lsum(en)
        yield
        o = (_dot_nt(ec.astype(BF16), cv_ref[b, hd, :].astype(BF16))
             + _dot(en.astype(BF16), vn_ref[p, rows, :]))
        yield
        o = o * (1.0 / l)
        out[p] = jnp.where(lo, o[:seq_len], o[seq_len:]).astype(BF16)

    def seq_attend(b, out):
        band = [None] * HEAD_PAIRS
        yield from _round_robin([pair_attend(b, p, band) for p in range(HEAD_PAIRS)])
        rows = slice(b * seq_len, (b + 1) * seq_len)
        mo = _mem_attend(qm_ref[rows, :], mk_ref[b].astype(BF16), mv_ref[b].astype(BF16))
        out[b] = jnp.concatenate(band + [mo.astype(BF16)], axis=-1)

    rows_out = [None] * seqs
    _interleave([seq_attend(b, rows_out) for b in range(seqs)])
    cat = jnp.concatenate(rows_out, axis=0)
    o_ref[...] = x_ref[...] + _rms(_dot(cat, wout_ref[...]), gpost_ref[...])


def _band_sample(x, q3, qm, kn3, vn3, ckt, cvt, bias_c, bias_n, mkt, mvt, wout, gpost, *, into, seqs=4):
    n = qm.shape[0]
    n_seq, past = ckt.shape[0], ckt.shape[2]
    seq_len = n // n_seq
    rows = seqs * seq_len
    slab_spec = pl.BlockSpec((rows, D_MODEL), lambda i: (i, 0))
    pair_rows_spec = pl.BlockSpec((HEAD_PAIRS, rows, LANES), lambda i: (0, i, 0))
    cache_spec = pl.BlockSpec((seqs, B_W, past), lambda i: (i, 0, 0))
    return _call(
        functools.partial(_band_sample_kernel, seqs=seqs, seq_len=seq_len), "band_sample", into=into,
        grid=(n_seq // seqs,),
        in_specs=[slab_spec,
                  pair_rows_spec,
                  pl.BlockSpec((rows, MEM_W), lambda i: (i, 0)),
                  pair_rows_spec, pair_rows_spec,
                  cache_spec, cache_spec,
                  _const_spec((HEAD_PAIRS, 2 * seq_len, past)),
                  _const_spec((HEAD_PAIRS, 2 * seq_len, seq_len)),
                  _mem_spec(mkt, seqs, 1), _mem_spec(mvt, seqs, 1),
                  _const_spec((B_W + MEM_W, D_MODEL)), _const_spec((1, D_MODEL))],
        out_specs=slab_spec,
        out_shape=jax.ShapeDtypeStruct(x.shape, F32),
    )(x, q3, qm, kn3, vn3, ckt, cvt, bias_c, bias_n, mkt, mvt, wout, gpost)


def _spatial_tile(w_s, b_s, period):
    reps = TILE // period
    keep = np.kron(np.eye(reps, dtype=bool), np.tril(np.ones((period, period), dtype=bool)))
    s_mat = jnp.where(keep, jnp.tile(w_s[:, :period, :period], (1, reps, reps)), 0.0)
    rows = jnp.tile(b_s[:, :period], (1, reps))
    bs = jnp.repeat(rows.T, GM_GW, axis=1)
    return s_mat.astype(BF16), bs


def _heads_last(t, lead):
    pos = t.shape[-1]
    t = t.reshape(lead + (-1, HEAD_DIM, pos))
    nd = len(lead)
    return jnp.transpose(t, tuple(range(nd)) + (nd + 2, nd, nd + 1))


def _positions_last(c):
    nd = c.ndim
    t = jnp.transpose(c, tuple(range(nd - 3)) + (nd - 2, nd - 1, nd - 3))
    return t.reshape(c.shape[:-3] + (c.shape[-2] * c.shape[-1], c.shape[-3]))


def kernel(x_prompt, x_sample, cache_mem_k, cache_mem_v, cache_band_k, cache_band_v, mem_prompt,
           g_mix_pre, g_mix_post, g_ffn_pre, g_ffn_post, g_mem, w_mem_kv,
           w_in_a, g_gm_ln, b_gm_ln, w_spatial, b_spatial, w_out_a,
           g_kv, w_kv, w_in_b, rel_bias, w_out_b, w_ff1, w_ff2):
    seq = x_prompt.shape[1]
    n_seq, seq_len = x_sample.shape[0], x_sample.shape[1]
    past = cache_band_k.shape[1]
    vec = lambda a: a.reshape(1, -1)
    stack = lambda a: a.reshape(DEPTH, 1, -1)

    win_a = w_in_a[0].astype(BF16)
    wout_a = w_out_a[0].astype(BF16)
    wk = w_kv[:, :B_W].astype(BF16)
    wv = w_kv[:, B_W:].astype(BF16)
    win_b = w_in_b[0].astype(BF16)
    wout_b = w_out_b[0].astype(BF16)
    ln_g, ln_b = vec(g_gm_ln[0]), vec(b_gm_ln[0])
    gkv, gpre_b = vec(g_kv), vec(g_mix_pre[1])
    gf_pre, gf_post = stack(g_ffn_pre), stack(g_ffn_post)

    n_sample = n_seq * seq_len
    slab_rows = seq + n_sample
    pre_a, post_a = vec(g_mix_pre[0]), vec(g_mix_post[0])

    mem_kt, mem_vt = _memkv(mem_prompt[0], g_mem, w_mem_kv)
    s_p, bs_p = _spatial_tile(w_spatial[0], b_spatial[0], GM_CHUNK)
    s_s, bs_s = _spatial_tile(w_spatial[0], b_spatial[0], seq_len)
    cmkt, cmvt = _positions_last(cache_mem_k), _positions_last(cache_mem_v)
    x, w1, w2 = _mixer_a(x_prompt[0], pre_a, post_a, win_a, ln_g, ln_b, s_p, bs_p, mem_kt, mem_vt, wout_a,
                         rows_per_mem=TILE, emit_v=False, tiles=4, slab_rows=slab_rows, slab_offset=n_sample,
                         cast=((w_ff1, 0), (w_ff2, 0)))
    x, v_rows = _mixer_a(x_sample.reshape(n_sample, D_MODEL), pre_a, post_a, win_a, ln_g, ln_b, s_s, bs_s,
                         cmkt, cmvt, wout_a, rows_per_mem=seq_len, emit_v=True,
                         slab_rows=slab_rows, into=x)
    x = _ffn(x, gf_pre, gf_post, w1, w2, 0)

    qt3, qm, k3, vt3, w1, w2 = _proj_b(x, gkv, gpre_b, (win_b, wk, wv), mode="prompt", rows=2 * PROJ_ROWS,
                                       row_offset=n_sample, n_rows=seq, cast=((w_ff1, 1), (w_ff2, 1)))
    n_keep = min(BAND_PAST, seq)
    q3s, qms, kn3, vn3, k_new, v_new, kt_tail, vt_tail = _proj_b(
        x, gkv, gpre_b, (win_b, wk, wv), mode="sample", rows=512, n_rows=n_sample, seq_len=seq_len,
        tail_offset=n_sample + seq - n_keep, tail_rows=n_keep)
    bias_p, bias_c, bias_n = _rel_bias_tables(rel_bias[0], (
        (TILE, BAND_PAST + TILE, ((0, TILE),), True, True),
        (seq_len, past + seq_len, ((0, past), (past, past + seq_len)), False, False)))
    post_b = vec(g_mix_post[1])
    y = _band_prompt(x, qt3, qm, k3, vt3, bias_p, mem_kt, mem_vt, wout_b, post_b)
    y = _band_sample(x, q3s, qms, kn3, vn3, _positions_last(cache_band_k), _positions_last(cache_band_v),
                     bias_c, bias_n, cmkt, cmvt, wout_b, post_b, into=y)
    y_sample, y_prompt = _ffn(y, gf_pre, gf_post, w1, w2, 1, split=n_sample)
    y_prompt = y_prompt[None]
    y_sample = y_sample.reshape(n_seq, seq_len, D_MODEL)

    return (y_prompt, y_sample,
            _heads_last(mem_kt, (DEPTH, 1)), _heads_last(mem_vt, (DEPTH, 1)),
            _heads_last(kt_tail, (1,)), _heads_last(vt_tail, (1,)),
            v_rows.reshape(1, n_seq, seq_len, GM_W),
            jnp.swapaxes(k_new, 1, 2), jnp.swapaxes(v_new, 1, 2))
```

```python
import functools

import jax
import jax.numpy as jnp
import numpy as np
from jax import lax
from jax.experimental import pallas as pl
from jax.experimental.pallas import tpu as pltpu

D_MODEL = 1024
DEPTH = 2
CHUNK = 64
HEAD_DIM = 64
GM_CHUNK = 128
GM_GROUPS = 4
GM_W = 768
GM_GW = GM_W // GM_GROUPS
MEM_LEN = 256
MEM_HEADS = 4
MEM_W = MEM_HEADS * HEAD_DIM
B_HEADS = 12
B_W = B_HEADS * HEAD_DIM
BAND_PAST = 512
REL_CLIP = 128
D_FF = 4 * D_MODEL
EPS = 1e-6

LANES = 128
SUBLANES = 8
HEAD_PAIRS = B_W // LANES
Q_SCALE = HEAD_DIM ** -0.5
NEG = -1e30
TILE = 256
KEY_BLOCKS = BAND_PAST // TILE + 1
KEY_STEP = 128
KEY_BLOCK = 128
PROJ_ROWS = 512
BF16_ROWS = 2 * SUBLANES
OUT_ROWS = HEAD_DIM + SUBLANES
BAND_TILES = BAND_PAST // TILE
GROUP_PAIRS = 2
BIAS_PERIOD = 1024
V7X_VMEM_BYTES = 64 * 1024 * 1024
VMEM_LIMIT = V7X_VMEM_BYTES * 7 // 8

BF16 = jnp.bfloat16
F32 = jnp.float32


def _dot(a, b):
    return jnp.dot(a, b, preferred_element_type=F32)


def _dot_nt(a, b):
    return lax.dot_general(a, b, (((1,), (1,)), ((), ())), preferred_element_type=F32)


def _rms(x, g):
    ms = jnp.mean(x * x, axis=-1, keepdims=True)
    return x * lax.rsqrt(ms + EPS) * g


def _lsum(a):
    return jnp.sum(a, axis=-1, keepdims=True)


def _lmax(a):
    return jnp.max(a, axis=-1, keepdims=True)


def _const_spec(shape):
    nd = len(shape)
    return pl.BlockSpec(shape, lambda *_: (0,) * nd, pipeline_mode=pl.Buffered(1))


def _layer_spec(shape, layer):
    nd = len(shape)
    return pl.BlockSpec((None,) + shape, lambda *_: (layer,) + (0,) * nd, pipeline_mode=pl.Buffered(1))


def _without_ref(body, k):
    def wrapped(*refs):
        return body(*refs[:k], *refs[k + 1:])
    return wrapped


def _call(body, name, *, in_specs, into=None, semantics="parallel", **kw):
    params = pltpu.CompilerParams(dimension_semantics=(semantics,), vmem_limit_bytes=VMEM_LIMIT)
    if into is None:
        return pl.pallas_call(body, name=name, in_specs=in_specs, compiler_params=params, **kw)
    n_in = len(in_specs)
    call = pl.pallas_call(_without_ref(body, n_in), name=name,
                          in_specs=[*in_specs, pl.BlockSpec(memory_space=pl.ANY)],
                          input_output_aliases={n_in: 0}, compiler_params=params, **kw)
    return lambda *args: call(*args, into)


def _memkv_kernel(mem_ref, g_ref, w_ref, kt_ref, vt_ref):
    h = _dot(_rms(mem_ref[...], g_ref[...]).astype(BF16), w_ref[...].astype(BF16))
    kt_ref[0] = h[:, :MEM_W].T
    vt_ref[0] = h[:, MEM_W:].T


def _memkv(mem, g_mem, w_mem_kv):
    out = jax.ShapeDtypeStruct((DEPTH, 1, MEM_W, MEM_LEN), F32)
    return _call(
        _memkv_kernel, "mem_kv",
        grid=(DEPTH,),
        in_specs=[
            pl.BlockSpec((MEM_LEN, D_MODEL), lambda l: (0, 0)),
            pl.BlockSpec((None, 1, D_MODEL), lambda l: (l, 0, 0)),
            pl.BlockSpec((None, D_MODEL, 2 * MEM_W), lambda l: (l, 0, 0)),
        ],
        out_specs=[pl.BlockSpec((None, 1, MEM_W, MEM_LEN), lambda l: (l, 0, 0, 0))] * 2,
        out_shape=[out, out],
    )(mem, g_mem.reshape(DEPTH, 1, D_MODEL), w_mem_kv)


def _mem_attend(qb, kt, vt):
    r = qb.shape[0]
    lane = lax.broadcasted_iota(jnp.int32, (1, MEM_W), 1)
    masks = [(lane >= h * HEAD_DIM) & (lane < (h + 1) * HEAD_DIM) for h in range(MEM_HEADS)]
    qs = jnp.concatenate([jnp.where(m, qb, jnp.zeros_like(qb)) for m in masks], axis=0)
    s = _dot(qs, kt)
    e = jnp.exp(s - _lmax(s))
    pv = _dot_nt(e.astype(BF16), vt) * (1.0 / _lsum(e))
    out = jnp.where(masks[0], pv[:r], 0.0)
    for h in range(1, MEM_HEADS):
        out = out + jnp.where(masks[h], pv[h * r:(h + 1) * r], 0.0)
    return out


def _gelu(x):
    c1 = float(np.sqrt(2.0 / np.pi))
    c2 = c1 * 0.044715
    half = 0.5 * x
    return half + half * jnp.tanh(x * (c1 + c2 * (x * x)))


def _cast_specs(cast, steps, fill):
    own = lambda i: jnp.maximum(i - fill, 0)
    in_specs, out_specs, out_shape = [], [], []
    for a, layer in cast:
        _, r, c = a.shape
        in_specs.append(pl.BlockSpec((None, r // steps, c), functools.partial(lambda i, l: (l, own(i), 0), l=layer)))
        out_specs.append(pl.BlockSpec((r // steps, c), lambda i: (own(i), 0)))
        out_shape.append(jax.ShapeDtypeStruct((r, c), BF16))
    return in_specs, out_specs, out_shape


def _split_cast_refs(refs, n_cast):
    return refs[:n_cast], refs[n_cast:len(refs) - n_cast], refs[len(refs) - n_cast:]


def _cast_blocks(srcs, dsts, stages=4):
    for src, dst in zip(srcs, dsts):
        step = src.shape[0] // stages
        for k in range(stages):
            dst[k * step:(k + 1) * step, :] = src[k * step:(k + 1) * step, :].astype(dst.dtype)
            yield


def _after_fill(o_ref, fill_steps, body):
    if not fill_steps:
        return body()

    @pl.when(pl.program_id(0) < fill_steps)
    def _():
        o_ref[...] = jnp.zeros(o_ref.shape, o_ref.dtype)

    @pl.when(pl.program_id(0) >= fill_steps)
    def _():
        body()


def _round_robin(stage_lists):
    live = list(stage_lists)
    while live:
        for g in list(live):
            if next(g, StopIteration) is StopIteration:
                live.remove(g)
        yield


def _interleave(stage_lists):
    for _ in _round_robin(stage_lists):
        pass


def _mixer_a_kernel(x_ref, gpre_ref, gpost_ref, win_ref, gln_ref, bln_ref, s_ref, bs_ref,
                    mk_ref, mv_ref, wout_ref, *rest, rows_per_mem, tiles, shared_mem, fill_steps, n_cast):
    cast_in, (o_ref, *v_out), cast_out = _split_cast_refs(rest, n_cast)
    nt = GM_W // LANES
    seqs = TILE // rows_per_mem
    lane = lax.broadcasted_iota(jnp.int32, (1, LANES), 1)
    lo = lane < (GM_GW - LANES)
    inv = 1.0 / GM_GW

    def group_stat(a):
        s0 = _lsum(a[0] + jnp.where(lo, a[1], 0.0)) * inv
        s1 = _lsum(jnp.where(lo, 0.0, a[1]) + a[2]) * inv
        s2 = _lsum(a[3] + jnp.where(lo, a[4], 0.0)) * inv
        s3 = _lsum(jnp.where(lo, 0.0, a[4]) + a[5]) * inv
        return [s0, jnp.where(lo, s0, s1), s1, s2, jnp.where(lo, s2, s3), s3]

    def tile(t):
        rows = slice(t * TILE, (t + 1) * TILE)
        z = _dot(_rms(x_ref[rows, :], gpre_ref[...]).astype(BF16), win_ref[...])
        yield
        u = [_gelu(z[:, j * LANES:(j + 1) * LANES]) for j in range(nt)]
        g = [_gelu(z[:, GM_W + j * LANES:GM_W + (j + 1) * LANES]) for j in range(nt)]
        mu = group_stat(g)
        c = [g[j] - mu[j] for j in range(nt)]
        var = group_stat([cj * cj for cj in c])
        gln = gln_ref[...]
        bln = bln_ref[...]
        vn = [c[j] * lax.rsqrt(var[j] + EPS) * gln[:, j * LANES:(j + 1) * LANES]
              + bln[:, j * LANES:(j + 1) * LANES] for j in range(nt)]
        if v_out:
            v_out[0][rows, :] = jnp.concatenate(vn, axis=-1)
        yield
        vb = [a.astype(BF16) for a in vn]
        win = [(0, 1), (1, 2), (3, 4), (4, 5)]
        m = [_dot(s_ref[k], jnp.concatenate([vb[a], vb[b]], axis=-1)) for k, (a, b) in enumerate(win)]
        mixed = [m[0][:, :LANES], jnp.where(lo, m[0][:, LANES:], m[1][:, :LANES]), m[1][:, LANES:],
                 m[2][:, :LANES], jnp.where(lo, m[2][:, LANES:], m[3][:, :LANES]), m[3][:, LANES:]]
        bs = bs_ref[...]
        gm = [u[j] * (mixed[j] + bs[:, j * LANES:(j + 1) * LANES]) for j in range(nt)]
        yield
        qm = (z[:, 2 * GM_W:] * Q_SCALE).astype(BF16)
        mo = []
        for b in range(seqs):
            r = slice(b * rows_per_mem, (b + 1) * rows_per_mem)
            mi = 0 if shared_mem else t * seqs + b
            mo.append(_mem_attend(qm[r], mk_ref[mi].astype(BF16), mv_ref[mi].astype(BF16)))
        mo = mo[0] if len(mo) == 1 else jnp.concatenate(mo, axis=0)
        yield
        cat = jnp.concatenate([a.astype(BF16) for a in gm] + [mo.astype(BF16)], axis=-1)
        o_ref[rows, :] = x_ref[rows, :] + _rms(_dot(cat, wout_ref[...]), gpost_ref[...])

    def body():
        _interleave([tile(t) for t in range(tiles)] + [_cast_blocks(cast_in, cast_out)])

    _after_fill(o_ref, fill_steps, body)


def _mem_spec(mem, seqs, layer):
    if mem.shape[1] == 1:
        return pl.BlockSpec((None, 1, MEM_W, MEM_LEN), lambda i: (layer, 0, 0, 0))
    return pl.BlockSpec((None, seqs, MEM_W, MEM_LEN), lambda i: (layer, i, 0, 0))


def _mixer_a(x, gpre, gpost, win, gln, bln, s_mat, bs, mkt, mvt, wout, *, rows_per_mem, emit_v, tiles=2,
             slab_rows, slab_offset=0, into=None, cast=()):
    n = x.shape[0]
    blk = tiles * TILE
    seqs = blk // rows_per_mem
    off = slab_offset // blk
    fill, shift = (0, off) if into is not None else (off, 0)
    row = lambda w: pl.BlockSpec((blk, w), lambda i: (jnp.maximum(i - fill, 0), 0))
    out_shape = [jax.ShapeDtypeStruct((slab_rows, D_MODEL), F32)]
    out_specs = [pl.BlockSpec((blk, D_MODEL), lambda i: (i + shift, 0))]
    if emit_v:
        out_shape.append(jax.ShapeDtypeStruct((n, GM_W), F32))
        out_specs.append(row(GM_W))
    cast_in_specs, cast_out_specs, cast_out_shape = _cast_specs(cast, n // blk, fill)
    return _call(
        functools.partial(_mixer_a_kernel, rows_per_mem=rows_per_mem, tiles=tiles,
                          shared_mem=mkt.shape[1] == 1, fill_steps=fill, n_cast=len(cast)), "mixer_a", into=into,
        semantics="arbitrary" if fill and cast else "parallel",
        grid=(n // blk + fill,),
        in_specs=[
            row(D_MODEL),
            _const_spec((1, D_MODEL)), _const_spec((1, D_MODEL)),
            _const_spec((D_MODEL, 2 * GM_W + MEM_W)),
            _const_spec((1, GM_W)), _const_spec((1, GM_W)),
            _const_spec((GM_GROUPS, TILE, TILE)), _const_spec((TILE, GM_W)),
            _mem_spec(mkt, seqs, 0), _mem_spec(mvt, seqs, 0),
            _const_spec((GM_W + MEM_W, D_MODEL)),
            *cast_in_specs,
        ],
        out_specs=out_specs + cast_out_specs,
        out_shape=out_shape + cast_out_shape,
    )(x, gpre, gpost, win, gln, bln, s_mat, bs, mkt, mvt, wout, *[a for a, _ in cast])


def _ffn_kernel(x_ref, gpre_ref, gpost_ref, w1_ref, w2_ref, o_ref, *o_tail, tile_rows, ff_chunk, head_steps):
    def tile(t):
        r = slice(sum(tile_rows[:t]), sum(tile_rows[:t + 1]))
        x = x_ref[r, :]
        xn = _rms(x, gpre_ref[...]).astype(BF16)
        acc = jnp.zeros(x.shape, F32)
        yield
        for c in range(D_FF // ff_chunk):
            h = _dot(xn, w1_ref[:, c * ff_chunk:(c + 1) * ff_chunk])
            h = jnp.square(jnp.maximum(h, 0.0)).astype(BF16)
            acc = acc + _dot(h, w2_ref[c * ff_chunk:(c + 1) * ff_chunk, :])
            yield
        y = x + _rms(acc, gpost_ref[...])
        if not o_tail:
            o_ref[r, :] = y
        else:
            is_head = pl.program_id(0) < head_steps
            o_ref[r, :] = jnp.where(is_head, y, o_ref[r, :])
            o_tail[0][r, :] = y

    if o_tail:
        @pl.when(pl.program_id(0) < head_steps)
        def _():
            o_ref[...] = jnp.zeros(o_ref.shape, F32)

    _interleave([tile(t) for t in range(len(tile_rows))])


def _ffn(x, gpre, gpost, w1, w2, layer, *, tile_rows=(256, 512, 256), ff_chunk=1024, split=None):
    n = x.shape[0]
    blk = sum(tile_rows)
    row = pl.BlockSpec((blk, D_MODEL), lambda i: (i, 0))
    if split is None:
        head_steps, out_specs, out_shape = None, row, jax.ShapeDtypeStruct((n, D_MODEL), F32)
    else:
        head_steps = split // blk
        out_specs = [pl.BlockSpec((blk, D_MODEL), lambda i: (jnp.minimum(i, head_steps - 1), 0)),
                     pl.BlockSpec((blk, D_MODEL), lambda i: (jnp.maximum(i - head_steps, 0), 0))]
        out_shape = [jax.ShapeDtypeStruct((split, D_MODEL), F32), jax.ShapeDtypeStruct((n - split, D_MODEL), F32)]
    return _call(
        functools.partial(_ffn_kernel, tile_rows=tile_rows, ff_chunk=ff_chunk, head_steps=head_steps),
        "ffn", semantics="arbitrary",
        grid=(n // blk,),
        in_specs=[row, _layer_spec((1, D_MODEL), layer), _layer_spec((1, D_MODEL), layer),
                  _const_spec((D_MODEL, D_FF)), _const_spec((D_FF, D_MODEL))],
        out_specs=out_specs,
        out_shape=out_shape,
    )(x, gpre, gpost, w1, w2)


def _proj_b_kernel(x_ref, gkv_ref, gpre_ref, *refs, mode, seq_len, n_weights, n_cast):
    rows = x_ref.shape[0]
    cast_in, outs, cast_out = _split_cast_refs(refs[n_weights:], n_cast)
    refs = (*refs[:n_weights], *outs)

    def normed(r, src=x_ref):
        x = src[r, :]
        xh = x * lax.rsqrt(jnp.mean(x * x, axis=-1, keepdims=True) + EPS)
        return (xh * gkv_ref[...]).astype(BF16), (xh * gpre_ref[...]).astype(BF16)

    if mode == "prompt":
        win_ref, wk_ref, wv_ref, qt_ref, qm_ref, k_ref, vt_ref = refs

        def tile(t):
            r = slice(t * PROJ_ROWS, (t + 1) * PROJ_ROWS)
            xkv, xq = normed(r)
            yield
            z = _dot(xq, win_ref[...]) * Q_SCALE
            qm_ref[r, :] = z[:, B_W:].astype(BF16)
            for p in range(HEAD_PAIRS):
                qt_ref[p, :, r] = z[:, p * LANES:(p + 1) * LANES].T.astype(BF16)
            yield
            v = _dot(xkv, wv_ref[...])
            for p in range(HEAD_PAIRS):
                vt_ref[p, :, r] = v[:, p * LANES:(p + 1) * LANES].T.astype(BF16)
            yield
            k = _dot(xkv, wk_ref[...])
            for p in range(HEAD_PAIRS):
                k_ref[p, r, :] = k[:, p * LANES:(p + 1) * LANES].astype(BF16)

        _interleave([tile(t) for t in range(rows // PROJ_ROWS)] + [_cast_blocks(cast_in, cast_out)])
        return
    win_ref, wk_ref, wv_ref, xt_ref, q_ref, qm_ref, k_ref, v_ref, k4_ref, v4_ref, kt_ref, vt_ref = refs
    xkv_tail, _ = normed(slice(None), xt_ref)
    kt_ref[...] = _dot(xkv_tail, wk_ref[...]).T
    vt_ref[...] = _dot(xkv_tail, wv_ref[...]).T
    xkv, xq = normed(slice(None))
    z = _dot(xq, win_ref[...]) * Q_SCALE
    qm_ref[...] = z[:, B_W:].astype(BF16)
    k = _dot(xkv, wk_ref[...])
    v = _dot(xkv, wv_ref[...])
    for p in range(HEAD_PAIRS):
        cols = slice(p * LANES, (p + 1) * LANES)
        q_ref[p] = z[:, cols].astype(BF16)
        k_ref[p] = k[:, cols].astype(BF16)
        v_ref[p] = v[:, cols].astype(BF16)
    for b in range(rows // seq_len):
        for h in range(B_HEADS):
            r, c = slice(b * seq_len, (b + 1) * seq_len), slice(h * HEAD_DIM, (h + 1) * HEAD_DIM)
            k4_ref[b, h] = k[r, c]
            v4_ref[b, h] = v[r, c]


def _proj_b(x, gkv, gpre, weights, *, mode, rows, row_offset=0, n_rows=None, seq_len=None, cast=(),
            tail_offset=None, tail_rows=None):
    n = x.shape[0] if n_rows is None else n_rows
    off = row_offset // rows
    steps = n // rows
    pair_rows = jax.ShapeDtypeStruct((HEAD_PAIRS, n, LANES), BF16)
    pair_rows_spec = pl.BlockSpec((HEAD_PAIRS, rows, LANES), lambda i: (0, i, 0))
    pair_cols = jax.ShapeDtypeStruct((HEAD_PAIRS, LANES, n), BF16)
    pair_cols_spec = pl.BlockSpec((HEAD_PAIRS, LANES, rows), lambda i: (0, 0, i))
    qm = jax.ShapeDtypeStruct((n, MEM_W), BF16)
    qm_spec = pl.BlockSpec((rows, MEM_W), lambda i: (i, 0))
    extra_in, extra_specs = [], []
    if mode == "prompt":
        out_shape = [pair_cols, qm, pair_rows, pair_cols]
        out_specs = [pair_cols_spec, qm_spec, pair_rows_spec, pair_cols_spec]
    else:
        seqs = rows // seq_len
        per_head = jax.ShapeDtypeStruct((n // seq_len, B_HEADS, seq_len, HEAD_DIM), F32)
        per_head_spec = pl.BlockSpec((seqs, B_HEADS, seq_len, HEAD_DIM), lambda i: (i, 0, 0, 0))
        tail_blk = tail_rows // steps
        tail_off = tail_offset // tail_blk
        extra_in, extra_specs = [x], [pl.BlockSpec((tail_blk, D_MODEL), lambda i: (i + tail_off, 0))]
        out_shape = ([pair_rows, qm, pair_rows, pair_rows, per_head, per_head]
                     + [jax.ShapeDtypeStruct((B_W, tail_rows), F32)] * 2)
        out_specs = ([pair_rows_spec, qm_spec, pair_rows_spec, pair_rows_spec, per_head_spec, per_head_spec]
                     + [pl.BlockSpec((B_W, tail_blk), lambda i: (0, i))] * 2)
    cast_in_specs, cast_out_specs, cast_out_shape = _cast_specs(cast, steps, 0)
    return _call(
        functools.partial(_proj_b_kernel, mode=mode, seq_len=seq_len, n_weights=len(weights) + len(extra_in),
                          n_cast=len(cast)), "proj_b_" + mode,
        grid=(steps,),
        in_specs=[pl.BlockSpec((rows, D_MODEL), lambda i: (i + off, 0)),
                  _const_spec((1, D_MODEL)), _const_spec((1, D_MODEL))]
                 + [_const_spec(w.shape) for w in weights] + extra_specs + cast_in_specs,
        out_specs=out_specs + cast_out_specs,
        out_shape=out_shape + cast_out_shape,
    )(x, gkv, gpre, *weights, *extra_in, *[a for a, _ in cast])


def _bias_kernel(*refs, tables):
    g_refs, o_refs = refs[:len(tables)], list(refs[len(tables):])
    for g_ref, (n_q, n_k, splits, band, keys_on_rows) in zip(g_refs, tables):
        outs = [o_refs.pop(0) for _ in splits] if not band else [o_refs.pop(0)]
        n_rows, shift = (n_k, n_q) if keys_on_rows else (n_q, BIAS_PERIOD - (n_q - 1))
        for hh in range(2):
            x = jnp.broadcast_to(g_ref[hh:hh + 1, :], (n_rows, BIAS_PERIOD))
            t = pltpu.roll(x, shift, 1, stride=1, stride_axis=0)
            if band:
                a = lax.broadcasted_iota(jnp.int32, (n_rows, BIAS_PERIOD), 0)
                b = lax.broadcasted_iota(jnp.int32, (n_rows, BIAS_PERIOD), 1)
                r, w = (b, a) if keys_on_rows else (a, b)
                j = w - (r - (r & (CHUNK - 1)))
                t = jnp.where((j >= 0) & (j < BAND_PAST + CHUNK), t, NEG)
            if band:
                slots = _band_biased_blocks()
                for k, (blk, b) in enumerate(slots):
                    row0 = (hh * len(slots) + k) * KEY_BLOCK
                    outs[0][row0:row0 + KEY_BLOCK, :] = t[blk * KEY_BLOCK:(blk + 1) * KEY_BLOCK,
                                                          b * LANES:(b + 1) * LANES]
                continue
            for o_ref, (lo, hi) in zip(outs, splits):
                o_ref[hh * n_rows:(hh + 1) * n_rows, :] = t[:, lo:hi]


def _rel_bias_tables(rel_bias, tables):
    rel_bias = rel_bias - rel_bias[:, -1:]
    gens, out_specs, out_shape = [], [], []
    for n_q, n_k, splits, band, keys_on_rows in tables:
        c0 = n_k - 1
        far = jnp.broadcast_to(rel_bias[:, -1:], (B_HEADS, c0 - REL_CLIP))
        near = jnp.broadcast_to(rel_bias[:, :1], (B_HEADS, BIAS_PERIOD - (c0 - REL_CLIP) - (2 * REL_CLIP + 1)))
        gen = [near, rel_bias, far] if keys_on_rows else [far, rel_bias[:, ::-1], near]
        gens.append(jnp.concatenate(gen, axis=1).reshape(HEAD_PAIRS, 2, BIAS_PERIOD))
        n_rows = n_k if keys_on_rows else n_q
        if band:
            splits, n_rows = ((0, LANES),), len(_band_biased_blocks()) * KEY_BLOCK
        out_specs += [pl.BlockSpec((None, 2 * n_rows, hi - lo), lambda p: (p, 0, 0)) for lo, hi in splits]
        out_shape += [jax.ShapeDtypeStruct((HEAD_PAIRS, 2 * n_rows, hi - lo), F32) for lo, hi in splits]
    return _call(
        functools.partial(_bias_kernel, tables=tables), "rel_bias",
        grid=(HEAD_PAIRS,),
        in_specs=[pl.BlockSpec((None, 2, BIAS_PERIOD), lambda p: (p, 0, 0))] * len(tables),
        out_specs=out_specs,
        out_shape=out_shape,
    )(*gens)


def _band_block_kinds():
    n_k = KEY_BLOCKS * TILE
    r = np.arange(TILE)[None, :]
    w = np.arange(n_k)[:, None]
    j = w - CHUNK * (r // CHUNK)
    ok = (j >= 0) & (j < BAND_PAST + CHUNK)
    plain = ok & (BAND_PAST + r - w >= REL_CLIP)
    kinds = []
    for a in range(n_k // KEY_BLOCK):
        rows = slice(a * KEY_BLOCK, (a + 1) * KEY_BLOCK)
        blocks = [(rows, slice(b * LANES, (b + 1) * LANES)) for b in range(TILE // LANES)]
        kinds.append(["skip" if not ok[blk].any() else "plain" if plain[blk].all() else "biased"
                      for blk in blocks])
    return kinds


def _band_biased_blocks():
    kinds = _band_block_kinds()
    return [(blk, b) for blk, row in enumerate(kinds) for b, kind in enumerate(row) if kind == "biased"]


def _band_prompt_kernel(x_ref, qt_ref, qm_ref, kp_ref, kc_ref, vtp_ref, vtc_ref,
                        bias_ref, mk_ref, mv_ref, wout_ref, gpost_ref, o_ref, *, fill_steps):
    i = pl.program_id(0) - fill_steps
    n_k = KEY_BLOCKS * TILE
    kinds = _band_block_kinds()
    slots = _band_biased_blocks()
    w = lax.broadcasted_iota(jnp.int32, (n_k, LANES), 0)
    c = lax.broadcasted_iota(jnp.int32, (n_k, LANES), 1)
    ones_rows = jnp.where(lax.broadcasted_iota(jnp.int32, (BF16_ROWS, n_k), 0) == 0, 1.0, 0.0).astype(BF16)
    row = lax.broadcasted_iota(jnp.int32, (LANES, TILE), 0)
    one_hot_row = jnp.where(row == 0, 1.0, 0.0).astype(BF16)
    lo = row < HEAD_DIM

    def key_step(a, s, vth, bias_block, state):
        blocks = range(a * KEY_STEP // KEY_BLOCK, (a + 1) * KEY_STEP // KEY_BLOCK)
        es, alphas, active = [], [], []
        for b in range(TILE // LANES):
            cols = slice(b * LANES, (b + 1) * LANES)
            sjs = {}
            for blk in blocks:
                if kinds[blk][b] != "skip":
                    rows = slice(blk * KEY_BLOCK, (blk + 1) * KEY_BLOCK)
                    sjs[blk] = s[rows, cols] + bias_block(blk, b) if kinds[blk][b] == "biased" else s[rows, cols]
            active.append(bool(sjs))
            alphas.append(None)
            if not sjs:
                es.append(jnp.zeros((KEY_STEP, LANES), F32))
                continue
            mj = jnp.max(functools.reduce(jnp.maximum, sjs.values()), axis=0, keepdims=True)
            if state[b] is None:
                state[b] = [mj, None]
            else:
                m_new = jnp.maximum(state[b][0], mj)
                alphas[b] = jnp.exp(state[b][0] - m_new)
                state[b][0] = m_new
            es.append(jnp.concatenate(
                [jnp.exp(sjs[blk] - state[b][0]) if blk in sjs else jnp.zeros((KEY_BLOCK, LANES), F32)
                 for blk in blocks], axis=0))
        rows = slice(a * KEY_STEP, (a + 1) * KEY_STEP)
        pv = _dot(vth[:, rows], jnp.concatenate(es, axis=1).astype(BF16))[:OUT_ROWS]
        for b in range(TILE // LANES):
            if active[b]:
                pv_b = pv[:, b * LANES:(b + 1) * LANES]
                state[b][1] = pv_b if alphas[b] is None else state[b][1] * alphas[b] + pv_b

    def tile(t):
        q_rows = slice(t * TILE, (t + 1) * TILE)
        k_rows = slice(t * TILE, t * TILE + n_k)
        first_key_tile = i * BAND_TILES + t - (KEY_BLOCKS - 1)
        pen = jnp.where((w < -first_key_tile * TILE) & (c == 0), NEG, 0.0).astype(BF16)
        outs = []
        for p0 in range(0, HEAD_PAIRS, GROUP_PAIRS):
            heads = []
            for p in range(p0, p0 + GROUP_PAIRS):
                k_win = jnp.concatenate([kp_ref[p], kc_ref[p]], axis=0)[k_rows]
                k_ext = jnp.concatenate([k_win, pen], axis=1)
                vt = jnp.concatenate([vtp_ref[p], vtc_ref[p]], axis=1)[:, k_rows]
                qt = qt_ref[p, :, q_rows]
                zero = jnp.zeros_like(qt)
                for hh in range(2):
                    qh = jnp.where(lo, qt, zero) if hh == 0 else jnp.where(lo, zero, qt)
                    s = _dot(k_ext, jnp.concatenate([qh, one_hot_row], axis=0))
                    vth = jnp.concatenate([vt[hh * HEAD_DIM:(hh + 1) * HEAD_DIM, :], ones_rows], axis=0)
                    bias_rows = functools.partial(
                        lambda blk, b, p, hh: bias_ref[p, pl.ds((hh * len(slots) + slots.index((blk, b))) * KEY_BLOCK,
                                                                KEY_BLOCK), :],
                        p=p, hh=hh)
                    heads.append((s, vth, bias_rows, [None] * (TILE // LANES)))
            for a in range(n_k // KEY_STEP):
                for head in heads:
                    key_step(a, *head)
            outs += [jnp.concatenate([o[:HEAD_DIM] * (1.0 / o[HEAD_DIM:HEAD_DIM + 1]) for _, o in state], axis=1)
                     for *_, state in heads]
            yield
        band = jnp.concatenate(outs, axis=0).T.astype(BF16)
        mo = _mem_attend(qm_ref[q_rows, :], mk_ref[0].astype(BF16), mv_ref[0].astype(BF16))
        cat = jnp.concatenate([band, mo.astype(BF16)], axis=-1)
        o_ref[q_rows, :] = x_ref[q_rows, :] + _rms(_dot(cat, wout_ref[...]), gpost_ref[...])

    _after_fill(o_ref, fill_steps, lambda: _interleave([tile(t) for t in range(BAND_TILES)]))


def _band_prompt(x, qt3, qm, k3, vt3, bias, mkt, mvt, wout, gpost):
    n = qm.shape[0]
    rows = BAND_TILES * TILE
    assert rows == BAND_PAST and n % rows == 0 and (x.shape[0] - n) % rows == 0
    fill = (x.shape[0] - n) // rows
    slab_spec = pl.BlockSpec((rows, D_MODEL), lambda i: (i, 0))
    own = lambda i: jnp.maximum(i - fill, 0)
    past = lambda i: jnp.maximum(i - fill - 1, 0)
    return _call(
        functools.partial(_band_prompt_kernel, fill_steps=fill), "band_prompt",
        grid=(n // rows + fill,),
        in_specs=[slab_spec,
                  pl.BlockSpec((HEAD_PAIRS, LANES, rows), lambda i: (0, 0, own(i))),
                  pl.BlockSpec((rows, MEM_W), lambda i: (own(i), 0)),
                  pl.BlockSpec((HEAD_PAIRS, rows, LANES), lambda i: (0, past(i), 0)),
                  pl.BlockSpec((HEAD_PAIRS, rows, LANES), lambda i: (0, own(i), 0)),
                  pl.BlockSpec((HEAD_PAIRS, LANES, rows), lambda i: (0, 0, past(i))),
                  pl.BlockSpec((HEAD_PAIRS, LANES, rows), lambda i: (0, 0, own(i))),
                  _const_spec(bias.shape),
                  _mem_spec(mkt, 1, 1), _mem_spec(mvt, 1, 1),
                  _const_spec((B_W + MEM_W, D_MODEL)), _const_spec((1, D_MODEL))],
        out_specs=slab_spec,
        out_shape=jax.ShapeDtypeStruct(x.shape, F32),
    )(x, qt3, qm, k3, k3, vt3, vt3, bias, mkt, mvt, wout, gpost)


def _band_sample_kernel(x_ref, q_ref, qm_ref, kn_ref, vn_ref, ck_ref, cv_ref, bc_ref, bn_ref,
                        mk_ref, mv_ref, wout_ref, gpost_ref, o_ref, *, seqs, seq_len):
    lane = lax.broadcasted_iota(jnp.int32, (1, LANES), 1)
    lo = lane < HEAD_DIM

    def pair_attend(b, p, out):
        rows = slice(b * seq_len, (b + 1) * seq_len)
        hd = slice(p * LANES, (p + 1) * LANES)
        qp = q_ref[p, rows, :]
        zero = jnp.zeros_like(qp)
        qs = jnp.concatenate([jnp.where(lo, qp, zero), jnp.where(lo, zero, qp)], axis=0)
        sc = _dot(qs, ck_ref[b, hd, :].astype(BF16)) + bc_ref[p]
        sn = _dot_nt(qs, kn_ref[p, rows, :]) + bn_ref[p]
        yield
        m = jnp.maximum(_lmax(sc), _lmax(sn))
        ec = jnp.exp(sc - m)
        en = jnp.exp(sn - m)
        l = _lsum(ec) + _lsum(en)
        yield
        o = (_dot_nt(ec.astype(BF16), cv_ref[b, hd, :].astype(BF16))
             + _dot(en.astype(BF16), vn_ref[p, rows, :]))
        yield
        o = o * (1.0 / l)
        out[p] = jnp.where(lo, o[:seq_len], o[seq_len:]).astype(BF16)

    def seq_attend(b, out):
        band = [None] * HEAD_PAIRS
        yield from _round_robin([pair_attend(b, p, band) for p in range(HEAD_PAIRS)])
        rows = slice(b * seq_len, (b + 1) * seq_len)
        mo = _mem_attend(qm_ref[rows, :], mk_ref[b].astype(BF16), mv_ref[b].astype(BF16))
        out[b] = jnp.concatenate(band + [mo.astype(BF16)], axis=-1)

    rows_out = [None] * seqs
    _interleave([seq_attend(b, rows_out) for b in range(seqs)])
    cat = jnp.concatenate(rows_out, axis=0)
    o_ref[...] = x_ref[...] + _rms(_dot(cat, wout_ref[...]), gpost_ref[...])


def _band_sample(x, q3, qm, kn3, vn3, ckt, cvt, bias_c, bias_n, mkt, mvt, wout, gpost, *, into, seqs=4):
    n = qm.shape[0]
    n_seq, past = ckt.shape[0], ckt.shape[2]
    seq_len = n // n_seq
    rows = seqs * seq_len
    slab_spec = pl.BlockSpec((rows, D_MODEL), lambda i: (i, 0))
    pair_rows_spec = pl.BlockSpec((HEAD_PAIRS, rows, LANES), lambda i: (0, i, 0))
    cache_spec = pl.BlockSpec((seqs, B_W, past), lambda i: (i, 0, 0))
    return _call(
        functools.partial(_band_sample_kernel, seqs=seqs, seq_len=seq_len), "band_sample", into=into,
        grid=(n_seq // seqs,),
        in_specs=[slab_spec,
                  pair_rows_spec,
                  pl.BlockSpec((rows, MEM_W), lambda i: (i, 0)),
                  pair_rows_spec, pair_rows_spec,
                  cache_spec, cache_spec,
                  _const_spec((HEAD_PAIRS, 2 * seq_len, past)),
                  _const_spec((HEAD_PAIRS, 2 * seq_len, seq_len)),
                  _mem_spec(mkt, seqs, 1), _mem_spec(mvt, seqs, 1),
                  _const_spec((B_W + MEM_W, D_MODEL)), _const_spec((1, D_MODEL))],
        out_specs=slab_spec,
        out_shape=jax.ShapeDtypeStruct(x.shape, F32),
    )(x, q3, qm, kn3, vn3, ckt, cvt, bias_c, bias_n, mkt, mvt, wout, gpost)


def _spatial_tile(w_s, b_s, period):
    tril = jnp.tril(jnp.ones((GM_CHUNK, GM_CHUNK), dtype=bool))
    w = jnp.where(tril, w_s, jnp.zeros((), w_s.dtype))[:, :period, :period]
    eye = jnp.eye(TILE // period, dtype=w.dtype)
    s_mat = jnp.einsum("ab,gts->gatbs", eye, w).reshape(GM_GROUPS, TILE, TILE)
    rows = jnp.tile(b_s[:, :period], (1, TILE // period))
    bs = jnp.repeat(rows.T, GM_GW, axis=1)
    return s_mat.astype(BF16), bs


def _heads_last(t, lead):
    pos = t.shape[-1]
    t = t.reshape(lead + (-1, HEAD_DIM, pos))
    nd = len(lead)
    return jnp.transpose(t, tuple(range(nd)) + (nd + 2, nd, nd + 1))


def _positions_last(c):
    nd = c.ndim
    t = jnp.transpose(c, tuple(range(nd - 3)) + (nd - 2, nd - 1, nd - 3))
    return t.reshape(c.shape[:-3] + (c.shape[-2] * c.shape[-1], c.shape[-3]))


def kernel(x_prompt, x_sample, cache_mem_k, cache_mem_v, cache_band_k, cache_band_v, mem_prompt,
           g_mix_pre, g_mix_post, g_ffn_pre, g_ffn_post, g_mem, w_mem_kv,
           w_in_a, g_gm_ln, b_gm_ln, w_spatial, b_spatial, w_out_a,
           g_kv, w_kv, w_in_b, rel_bias, w_out_b, w_ff1, w_ff2):
    seq = x_prompt.shape[1]
    n_seq, seq_len = x_sample.shape[0], x_sample.shape[1]
    past = cache_band_k.shape[1]
    vec = lambda a: a.reshape(1, -1)
    stack = lambda a: a.reshape(DEPTH, 1, -1)

    win_a = w_in_a[0].astype(BF16)
    wout_a = w_out_a[0].astype(BF16)
    wk = w_kv[:, :B_W].astype(BF16)
    wv = w_kv[:, B_W:].astype(BF16)
    win_b = w_in_b[0].astype(BF16)
    wout_b = w_out_b[0].astype(BF16)
    ln_g, ln_b = vec(g_gm_ln[0]), vec(b_gm_ln[0])
    gkv, gpre_b = vec(g_kv), vec(g_mix_pre[1])
    gf_pre, gf_post = stack(g_ffn_pre), stack(g_ffn_post)

    n_sample = n_seq * seq_len
    slab_rows = seq + n_sample
    pre_a, post_a = vec(g_mix_pre[0]), vec(g_mix_post[0])

    mem_kt, mem_vt = _memkv(mem_prompt[0], g_mem, w_mem_kv)
    s_p, bs_p = _spatial_tile(w_spatial[0], b_spatial[0], GM_CHUNK)
    s_s, bs_s = _spatial_tile(w_spatial[0], b_spatial[0], seq_len)
    cmkt, cmvt = _positions_last(cache_mem_k), _positions_last(cache_mem_v)
    x, w1, w2 = _mixer_a(x_prompt[0], pre_a, post_a, win_a, ln_g, ln_b, s_p, bs_p, mem_kt, mem_vt, wout_a,
                         rows_per_mem=TILE, emit_v=False, tiles=4, slab_rows=slab_rows, slab_offset=n_sample,
                         cast=((w_ff1, 0), (w_ff2, 0)))
    x, v_rows = _mixer_a(x_sample.reshape(n_sample, D_MODEL), pre_a, post_a, win_a, ln_g, ln_b, s_s, bs_s,
                         cmkt, cmvt, wout_a, rows_per_mem=seq_len, emit_v=True,
                         slab_rows=slab_rows, into=x)
    x = _ffn(x, gf_pre, gf_post, w1, w2, 0)

    qt3, qm, k3, vt3, w1, w2 = _proj_b(x, gkv, gpre_b, (win_b, wk, wv), mode="prompt", rows=2 * PROJ_ROWS,
                                       row_offset=n_sample, n_rows=seq, cast=((w_ff1, 1), (w_ff2, 1)))
    n_keep = min(BAND_PAST, seq)
    q3s, qms, kn3, vn3, k_new, v_new, kt_tail, vt_tail = _proj_b(
        x, gkv, gpre_b, (win_b, wk, wv), mode="sample", rows=512, n_rows=n_sample, seq_len=seq_len,
        tail_offset=n_sample + seq - n_keep, tail_rows=n_keep)
    bias_p, bias_c, bias_n = _rel_bias_tables(rel_bias[0], (
        (TILE, BAND_PAST + TILE, (), True, True),
        (seq_len, past + seq_len, ((0, past), (past, past + seq_len)), False, False)))
    post_b = vec(g_mix_post[1])
    y = _band_prompt(x, qt3, qm, k3, vt3, bias_p, mem_kt, mem_vt, wout_b, post_b)
    y = _band_sample(x, q3s, qms, kn3, vn3, _positions_last(cache_band_k), _positions_last(cache_band_v),
                     bias_c, bias_n, cmkt, cmvt, wout_b, post_b, into=y)
    y_sample, y_prompt = _ffn(y, gf_pre, gf_post, w1, w2, 1, split=n_sample)
    y_prompt = y_prompt[None]
    y_sample = y_sample.reshape(n_seq, seq_len, D_MODEL)

    return (y_prompt, y_sample,
            _heads_last(mem_kt, (DEPTH, 1)), _heads_last(mem_vt, (DEPTH, 1)),
            _heads_last(kt_tail, (1,)), _heads_last(vt_tail, (1,)),
            v_rows.reshape(1, n_seq, seq_len, GM_W),
            jnp.swapaxes(k_new, 1, 2), jnp.swapaxes(v_new, 1, 2))
```

```python
import functools

import jax
import jax.numpy as jnp
import numpy as np
from jax import lax
from jax.experimental import pallas as pl
from jax.experimental.pallas import tpu as pltpu

D_MODEL = 1024
DEPTH = 2
CHUNK = 64
HEAD_DIM = 64
GM_CHUNK = 128
GM_GROUPS = 4
GM_W = 768
GM_GW = GM_W // GM_GROUPS
MEM_LEN = 256
MEM_HEADS = 4
MEM_W = MEM_HEADS * HEAD_DIM
B_HEADS = 12
B_W = B_HEADS * HEAD_DIM
BAND_PAST = 512
REL_CLIP = 128
D_FF = 4 * D_MODEL
EPS = 1e-6

LANES = 128
SUBLANES = 8
HEAD_PAIRS = B_W // LANES
Q_SCALE = HEAD_DIM ** -0.5
NEG = -1e30
TILE = 256
KEY_BLOCKS = BAND_PAST // TILE + 1
KEY_STEP = 128
KEY_BLOCK = 128
PROJ_ROWS = 512
BF16_ROWS = 2 * SUBLANES
OUT_ROWS = HEAD_DIM + SUBLANES
BAND_TILES = BAND_PAST // TILE
GROUP_PAIRS = 2
BIAS_PERIOD = 1024
V7X_VMEM_BYTES = 64 * 1024 * 1024
VMEM_LIMIT = V7X_VMEM_BYTES * 7 // 8

BF16 = jnp.bfloat16
F32 = jnp.float32


def _dot(a, b):
    return jnp.dot(a, b, preferred_element_type=F32)


def _dot_nt(a, b):
    return lax.dot_general(a, b, (((1,), (1,)), ((), ())), preferred_element_type=F32)


def _rms(x, g):
    ms = jnp.mean(x * x, axis=-1, keepdims=True)
    return x * lax.rsqrt(ms + EPS) * g


def _lsum(a):
    return jnp.sum(a, axis=-1, keepdims=True)


def _lmax(a):
    return jnp.max(a, axis=-1, keepdims=True)


def _const_spec(shape):
    nd = len(shape)
    return pl.BlockSpec(shape, lambda *_: (0,) * nd, pipeline_mode=pl.Buffered(1))


def _layer_spec(shape, layer):
    nd = len(shape)
    return pl.BlockSpec((None,) + shape, lambda *_: (layer,) + (0,) * nd, pipeline_mode=pl.Buffered(1))


def _without_ref(body, k):
    def wrapped(*refs):
        return body(*refs[:k], *refs[k + 1:])
    return wrapped


def _call(body, name, *, in_specs, into=None, semantics="parallel", **kw):
    params = pltpu.CompilerParams(dimension_semantics=(semantics,), vmem_limit_bytes=VMEM_LIMIT)
    if into is None:
        return pl.pallas_call(body, name=name, in_specs=in_specs, compiler_params=params, **kw)
    n_in = len(in_specs)
    call = pl.pallas_call(_without_ref(body, n_in), name=name,
                          in_specs=[*in_specs, pl.BlockSpec(memory_space=pl.ANY)],
                          input_output_aliases={n_in: 0}, compiler_params=params, **kw)
    return lambda *args: call(*args, into)


def _memkv_kernel(mem_ref, g_ref, w_ref, kt_ref, vt_ref):
    h = _dot(_rms(mem_ref[...], g_ref[...]).astype(BF16), w_ref[...].astype(BF16))
    kt_ref[0] = h[:, :MEM_W].T
    vt_ref[0] = h[:, MEM_W:].T


def _memkv(mem, g_mem, w_mem_kv):
    out = jax.ShapeDtypeStruct((DEPTH, 1, MEM_W, MEM_LEN), F32)
    return _call(
        _memkv_kernel, "mem_kv",
        grid=(DEPTH,),
        in_specs=[
            pl.BlockSpec((MEM_LEN, D_MODEL), lambda l: (0, 0)),
            pl.BlockSpec((None, 1, D_MODEL), lambda l: (l, 0, 0)),
            pl.BlockSpec((None, D_MODEL, 2 * MEM_W), lambda l: (l, 0, 0)),
        ],
        out_specs=[pl.BlockSpec((None, 1, MEM_W, MEM_LEN), lambda l: (l, 0, 0, 0))] * 2,
        out_shape=[out, out],
    )(mem, g_mem.reshape(DEPTH, 1, D_MODEL), w_mem_kv)


def _mem_attend(qb, kt, vt):
    r = qb.shape[0]
    lane = lax.broadcasted_iota(jnp.int32, (1, MEM_W), 1)
    masks = [(lane >= h * HEAD_DIM) & (lane < (h + 1) * HEAD_DIM) for h in range(MEM_HEADS)]
    qs = jnp.concatenate([jnp.where(m, qb, jnp.zeros_like(qb)) for m in masks], axis=0)
    s = _dot(qs, kt)
    e = jnp.exp(s - _lmax(s))
    pv = _dot_nt(e.astype(BF16), vt) * (1.0 / _lsum(e))
    out = jnp.where(masks[0], pv[:r], 0.0)
    for h in range(1, MEM_HEADS):
        out = out + jnp.where(masks[h], pv[h * r:(h + 1) * r], 0.0)
    return out


def _gelu(x):
    c1 = float(np.sqrt(2.0 / np.pi))
    c2 = c1 * 0.044715
    half = 0.5 * x
    return half + half * jnp.tanh(x * (c1 + c2 * (x * x)))


def _cast_specs(cast, steps, fill):
    own = lambda i: jnp.maximum(i - fill, 0)
    in_specs, out_specs, out_shape = [], [], []
    for a, layer in cast:
        _, r, c = a.shape
        in_specs.append(pl.BlockSpec((None, r // steps, c), functools.partial(lambda i, l: (l, own(i), 0), l=layer)))
        out_specs.append(pl.BlockSpec((r // steps, c), lambda i: (own(i), 0)))
        out_shape.append(jax.ShapeDtypeStruct((r, c), BF16))
    return in_specs, out_specs, out_shape


def _split_cast_refs(refs, n_cast):
    return refs[:n_cast], refs[n_cast:len(refs) - n_cast], refs[len(refs) - n_cast:]


def _cast_blocks(srcs, dsts, stages=4):
    for src, dst in zip(srcs, dsts):
        step = src.shape[0] // stages
        for k in range(stages):
            dst[k * step:(k + 1) * step, :] = src[k * step:(k + 1) * step, :].astype(dst.dtype)
            yield


def _after_fill(o_ref, fill_steps, body):
    if not fill_steps:
        return body()

    @pl.when(pl.program_id(0) < fill_steps)
    def _():
        o_ref[...] = jnp.zeros(o_ref.shape, o_ref.dtype)

    @pl.when(pl.program_id(0) >= fill_steps)
    def _():
        body()


def _round_robin(stage_lists):
    live = list(stage_lists)
    while live:
        for g in list(live):
            if next(g, StopIteration) is StopIteration:
                live.remove(g)
        yield


def _interleave(stage_lists):
    for _ in _round_robin(stage_lists):
        pass


def _mixer_a_kernel(x_ref, gpre_ref, gpost_ref, win_ref, gln_ref, bln_ref, s_ref, bs_ref,
                    mk_ref, mv_ref, wout_ref, *rest, rows_per_mem, tiles, shared_mem, fill_steps, n_cast):
    cast_in, (o_ref, *v_out), cast_out = _split_cast_refs(rest, n_cast)
    nt = GM_W // LANES
    seqs = TILE // rows_per_mem
    lane = lax.broadcasted_iota(jnp.int32, (1, LANES), 1)
    lo = lane < (GM_GW - LANES)
    inv = 1.0 / GM_GW

    def group_stat(a):
        s0 = _lsum(a[0] + jnp.where(lo, a[1], 0.0)) * inv
        s1 = _lsum(jnp.where(lo, 0.0, a[1]) + a[2]) * inv
        s2 = _lsum(a[3] + jnp.where(lo, a[4], 0.0)) * inv
        s3 = _lsum(jnp.where(lo, 0.0, a[4]) + a[5]) * inv
        return [s0, jnp.where(lo, s0, s1), s1, s2, jnp.where(lo, s2, s3), s3]

    def tile(t):
        rows = slice(t * TILE, (t + 1) * TILE)
        z = _dot(_rms(x_ref[rows, :], gpre_ref[...]).astype(BF16), win_ref[...])
        yield
        u = [_gelu(z[:, j * LANES:(j + 1) * LANES]) for j in range(nt)]
        g = [_gelu(z[:, GM_W + j * LANES:GM_W + (j + 1) * LANES]) for j in range(nt)]
        mu = group_stat(g)
        c = [g[j] - mu[j] for j in range(nt)]
        var = group_stat([cj * cj for cj in c])
        gln = gln_ref[...]
        bln = bln_ref[...]
        vn = [c[j] * lax.rsqrt(var[j] + EPS) * gln[:, j * LANES:(j + 1) * LANES]
              + bln[:, j * LANES:(j + 1) * LANES] for j in range(nt)]
        if v_out:
            v_out[0][rows, :] = jnp.concatenate(vn, axis=-1)
        yield
        vb = [a.astype(BF16) for a in vn]
        win = [(0, 1), (1, 2), (3, 4), (4, 5)]
        m = [_dot(s_ref[k], jnp.concatenate([vb[a], vb[b]], axis=-1)) for k, (a, b) in enumerate(win)]
        mixed = [m[0][:, :LANES], jnp.where(lo, m[0][:, LANES:], m[1][:, :LANES]), m[1][:, LANES:],
                 m[2][:, :LANES], jnp.where(lo, m[2][:, LANES:], m[3][:, :LANES]), m[3][:, LANES:]]
        bs = bs_ref[...]
        gm = [u[j] * (mixed[j] + bs[:, j * LANES:(j + 1) * LANES]) for j in range(nt)]
        yield
        qm = (z[:, 2 * GM_W:] * Q_SCALE).astype(BF16)
        mo = []
        for b in range(seqs):
            r = slice(b * rows_per_mem, (b + 1) * rows_per_mem)
            mi = 0 if shared_mem else t * seqs + b
            mo.append(_mem_attend(qm[r], mk_ref[mi].astype(BF16), mv_ref[mi].astype(BF16)))
        mo = mo[0] if len(mo) == 1 else jnp.concatenate(mo, axis=0)
        yield
        cat = jnp.concatenate([a.astype(BF16) for a in gm] + [mo.astype(BF16)], axis=-1)
        o_ref[rows, :] = x_ref[rows, :] + _rms(_dot(cat, wout_ref[...]), gpost_ref[...])

    def body():
        _interleave([tile(t) for t in range(tiles)] + [_cast_blocks(cast_in, cast_out)])

    _after_fill(o_ref, fill_steps, body)


def _mem_spec(mem, seqs, layer):
    if mem.shape[1] == 1:
        return pl.BlockSpec((None, 1, MEM_W, MEM_LEN), lambda i: (layer, 0, 0, 0))
    return pl.BlockSpec((None, seqs, MEM_W, MEM_LEN), lambda i: (layer, i, 0, 0))


def _mixer_a(x, gpre, gpost, win, gln, bln, s_mat, bs, mkt, mvt, wout, *, rows_per_mem, emit_v, tiles=2,
             slab_rows, slab_offset=0, into=None, cast=()):
    n = x.shape[0]
    blk = tiles * TILE
    seqs = blk // rows_per_mem
    off = slab_offset // blk
    fill, shift = (0, off) if into is not None else (off, 0)
    row = lambda w: pl.BlockSpec((blk, w), lambda i: (jnp.maximum(i - fill, 0), 0))
    out_shape = [jax.ShapeDtypeStruct((slab_rows, D_MODEL), F32)]
    out_specs = [pl.BlockSpec((blk, D_MODEL), lambda i: (i + shift, 0))]
    if emit_v:
        out_shape.append(jax.ShapeDtypeStruct((n, GM_W), F32))
        out_specs.append(row(GM_W))
    cast_in_specs, cast_out_specs, cast_out_shape = _cast_specs(cast, n // blk, fill)
    return _call(
        functools.partial(_mixer_a_kernel, rows_per_mem=rows_per_mem, tiles=tiles,
                          shared_mem=mkt.shape[1] == 1, fill_steps=fill, n_cast=len(cast)), "mixer_a", into=into,
        semantics="arbitrary" if fill and cast else "parallel",
        grid=(n // blk + fill,),
        in_specs=[
            row(D_MODEL),
            _const_spec((1, D_MODEL)), _const_spec((1, D_MODEL)),
            _const_spec((D_MODEL, 2 * GM_W + MEM_W)),
            _const_spec((1, GM_W)), _const_spec((1, GM_W)),
            _const_spec((GM_GROUPS, TILE, TILE)), _const_spec((TILE, GM_W)),
            _mem_spec(mkt, seqs, 0), _mem_spec(mvt, seqs, 0),
            _const_spec((GM_W + MEM_W, D_MODEL)),
            *cast_in_specs,
        ],
        out_specs=out_specs + cast_out_specs,
        out_shape=out_shape + cast_out_shape,
    )(x, gpre, gpost, win, gln, bln, s_mat, bs, mkt, mvt, wout, *[a for a, _ in cast])


def _ffn_kernel(x_ref, gpre_ref, gpost_ref, w1_ref, w2_ref, o_ref, *o_tail, tile_rows, ff_chunk, head_steps):
    def tile(t):
        r = slice(sum(tile_rows[:t]), sum(tile_rows[:t + 1]))
        x = x_ref[r, :]
        xn = _rms(x, gpre_ref[...]).astype(BF16)
        acc = jnp.zeros(x.shape, F32)
        yield
        for c in range(D_FF // ff_chunk):
            h = _dot(xn, w1_ref[:, c * ff_chunk:(c + 1) * ff_chunk])
            h = jnp.square(jnp.maximum(h, 0.0)).astype(BF16)
            acc = acc + _dot(h, w2_ref[c * ff_chunk:(c + 1) * ff_chunk, :])
            yield
        y = x + _rms(acc, gpost_ref[...])
        if not o_tail:
            o_ref[r, :] = y
        else:
            is_head = pl.program_id(0) < head_steps
            o_ref[r, :] = jnp.where(is_head, y, o_ref[r, :])
            o_tail[0][r, :] = y

    if o_tail:
        @pl.when(pl.program_id(0) < head_steps)
        def _():
            o_ref[...] = jnp.zeros(o_ref.shape, F32)

    _interleave([tile(t) for t in range(len(tile_rows))])


def _ffn(x, gpre, gpost, w1, w2, layer, *, tile_rows=(256, 512, 256), ff_chunk=1024, split=None):
    n = x.shape[0]
    blk = sum(tile_rows)
    row = pl.BlockSpec((blk, D_MODEL), lambda i: (i, 0))
    if split is None:
        head_steps, out_specs, out_shape = None, row, jax.ShapeDtypeStruct((n, D_MODEL), F32)
    else:
        head_steps = split // blk
        out_specs = [pl.BlockSpec((blk, D_MODEL), lambda i: (jnp.minimum(i, head_steps - 1), 0)),
                     pl.BlockSpec((blk, D_MODEL), lambda i: (jnp.maximum(i - head_steps, 0), 0))]
        out_shape = [jax.ShapeDtypeStruct((split, D_MODEL), F32), jax.ShapeDtypeStruct((n - split, D_MODEL), F32)]
    return _call(
        functools.partial(_ffn_kernel, tile_rows=tile_rows, ff_chunk=ff_chunk, head_steps=head_steps),
        "ffn", semantics="arbitrary",
        grid=(n // blk,),
        in_specs=[row, _layer_spec((1, D_MODEL), layer), _layer_spec((1, D_MODEL), layer),
                  _const_spec((D_MODEL, D_FF)), _const_spec((D_FF, D_MODEL))],
        out_specs=out_specs,
        out_shape=out_shape,
    )(x, gpre, gpost, w1, w2)


def _proj_b_kernel(x_ref, gkv_ref, gpre_ref, *refs, mode, seq_len, n_weights, n_cast):
    rows = x_ref.shape[0]
    cast_in, outs, cast_out = _split_cast_refs(refs[n_weights:], n_cast)
    refs = (*refs[:n_weights], *outs)

    def normed(r, src=x_ref):
        x = src[r, :]
        xh = x * lax.rsqrt(jnp.mean(x * x, axis=-1, keepdims=True) + EPS)
        return (xh * gkv_ref[...]).astype(BF16), (xh * gpre_ref[...]).astype(BF16)

    if mode == "prompt":
        win_ref, wk_ref, wv_ref, qt_ref, qm_ref, k_ref, vt_ref = refs

        def tile(t):
            r = slice(t * PROJ_ROWS, (t + 1) * PROJ_ROWS)
            xkv, xq = normed(r)
            yield
            z = _dot(xq, win_ref[...]) * Q_SCALE
            qm_ref[r, :] = z[:, B_W:].astype(BF16)
            for p in range(HEAD_PAIRS):
                qt_ref[p, :, r] = z[:, p * LANES:(p + 1) * LANES].T.astype(BF16)
            yield
            v = _dot(xkv, wv_ref[...])
            for p in range(HEAD_PAIRS):
                vt_ref[p, :, r] = v[:, p * LANES:(p + 1) * LANES].T.astype(BF16)
            yield
            k = _dot(xkv, wk_ref[...])
            for p in range(HEAD_PAIRS):
                k_ref[p, r, :] = k[:, p * LANES:(p + 1) * LANES].astype(BF16)

        _interleave([tile(t) for t in range(rows // PROJ_ROWS)] + [_cast_blocks(cast_in, cast_out)])
        return
    win_ref, wk_ref, wv_ref, xt_ref, q_ref, qm_ref, k_ref, v_ref, k4_ref, v4_ref, kt_ref, vt_ref = refs
    xkv_tail, _ = normed(slice(None), xt_ref)
    kt_ref[...] = _dot(xkv_tail, wk_ref[...]).T
    vt_ref[...] = _dot(xkv_tail, wv_ref[...]).T
    xkv, xq = normed(slice(None))
    z = _dot(xq, win_ref[...]) * Q_SCALE
    qm_ref[...] = z[:, B_W:].astype(BF16)
    k = _dot(xkv, wk_ref[...])
    v = _dot(xkv, wv_ref[...])
    for p in range(HEAD_PAIRS):
        cols = slice(p * LANES, (p + 1) * LANES)
        q_ref[p] = z[:, cols].astype(BF16)
        k_ref[p] = k[:, cols].astype(BF16)
        v_ref[p] = v[:, cols].astype(BF16)
    for b in range(rows // seq_len):
        for h in range(B_HEADS):
            r, c = slice(b * seq_len, (b + 1) * seq_len), slice(h * HEAD_DIM, (h + 1) * HEAD_DIM)
            k4_ref[b, h] = k[r, c]
            v4_ref[b, h] = v[r, c]


def _proj_b(x, gkv, gpre, weights, *, mode, rows, row_offset=0, n_rows=None, seq_len=None, cast=(),
            tail_offset=None, tail_rows=None):
    n = x.shape[0] if n_rows is None else n_rows
    off = row_offset // rows
    steps = n // rows
    pair_rows = jax.ShapeDtypeStruct((HEAD_PAIRS, n, LANES), BF16)
    pair_rows_spec = pl.BlockSpec((HEAD_PAIRS, rows, LANES), lambda i: (0, i, 0))
    pair_cols = jax.ShapeDtypeStruct((HEAD_PAIRS, LANES, n), BF16)
    pair_cols_spec = pl.BlockSpec((HEAD_PAIRS, LANES, rows), lambda i: (0, 0, i))
    qm = jax.ShapeDtypeStruct((n, MEM_W), BF16)
    qm_spec = pl.BlockSpec((rows, MEM_W), lambda i: (i, 0))
    extra_in, extra_specs = [], []
    if mode == "prompt":
        out_shape = [pair_cols, qm, pair_rows, pair_cols]
        out_specs = [pair_cols_spec, qm_spec, pair_rows_spec, pair_cols_spec]
    else:
        seqs = rows // seq_len
        per_head = jax.ShapeDtypeStruct((n // seq_len, B_HEADS, seq_len, HEAD_DIM), F32)
        per_head_spec = pl.BlockSpec((seqs, B_HEADS, seq_len, HEAD_DIM), lambda i: (i, 0, 0, 0))
        tail_blk = tail_rows // steps
        tail_off = tail_offset // tail_blk
        extra_in, extra_specs = [x], [pl.BlockSpec((tail_blk, D_MODEL), lambda i: (i + tail_off, 0))]
        out_shape = ([pair_rows, qm, pair_rows, pair_rows, per_head, per_head]
                     + [jax.ShapeDtypeStruct((B_W, tail_rows), F32)] * 2)
        out_specs = ([pair_rows_spec, qm_spec, pair_rows_spec, pair_rows_spec, per_head_spec, per_head_spec]
                     + [pl.BlockSpec((B_W, tail_blk), lambda i: (0, i))] * 2)
    cast_in_specs, cast_out_specs, cast_out_shape = _cast_specs(cast, steps, 0)
    return _call(
        functools.partial(_proj_b_kernel, mode=mode, seq_len=seq_len, n_weights=len(weights) + len(extra_in),
                          n_cast=len(cast)), "proj_b_" + mode,
        grid=(steps,),
        in_specs=[pl.BlockSpec((rows, D_MODEL), lambda i: (i + off, 0)),
                  _const_spec((1, D_MODEL)), _const_spec((1, D_MODEL))]
                 + [_const_spec(w.shape) for w in weights] + extra_specs + cast_in_specs,
        out_specs=out_specs + cast_out_specs,
        out_shape=out_shape + cast_out_shape,
    )(x, gkv, gpre, *weights, *extra_in, *[a for a, _ in cast])


def _bias_kernel(*refs, tables):
    g_refs, o_refs = refs[:len(tables)], list(refs[len(tables):])
    for g_ref, (n_q, n_k, splits, band, keys_on_rows) in zip(g_refs, tables):
        outs = [o_refs.pop(0) for _ in splits] if not band else [o_refs.pop(0)]
        n_rows, shift = (n_k, n_q) if keys_on_rows else (n_q, BIAS_PERIOD - (n_q - 1))
        for hh in range(2):
            x = jnp.broadcast_to(g_ref[hh:hh + 1, :], (n_rows, BIAS_PERIOD))
            t = pltpu.roll(x, shift, 1, stride=1, stride_axis=0)
            if band:
                a = lax.broadcasted_iota(jnp.int32, (n_rows, BIAS_PERIOD), 0)
                b = lax.broadcasted_iota(jnp.int32, (n_rows, BIAS_PERIOD), 1)
                r, w = (b, a) if keys_on_rows else (a, b)
                j = w - (r - (r & (CHUNK - 1)))
                t = jnp.where((j >= 0) & (j < BAND_PAST + CHUNK), t, NEG)
            if band:
                slots = _band_biased_blocks()
                for k, (blk, b) in enumerate(slots):
                    row0 = (hh * len(slots) + k) * KEY_BLOCK
                    outs[0][row0:row0 + KEY_BLOCK, :] = t[blk * KEY_BLOCK:(blk + 1) * KEY_BLOCK,
                                                          b * LANES:(b + 1) * LANES]
                continue
            for o_ref, (lo, hi) in zip(outs, splits):
                o_ref[hh * n_rows:(hh + 1) * n_rows, :] = t[:, lo:hi]


def _rel_bias_tables(rel_bias, tables):
    rel_bias = rel_bias - rel_bias[:, -1:]
    gens, out_specs, out_shape = [], [], []
    for n_q, n_k, splits, band, keys_on_rows in tables:
        c0 = n_k - 1
        far = jnp.broadcast_to(rel_bias[:, -1:], (B_HEADS, c0 - REL_CLIP))
        near = jnp.broadcast_to(rel_bias[:, :1], (B_HEADS, BIAS_PERIOD - (c0 - REL_CLIP) - (2 * REL_CLIP + 1)))
        gen = [near, rel_bias, far] if keys_on_rows else [far, rel_bias[:, ::-1], near]
        gens.append(jnp.concatenate(gen, axis=1).reshape(HEAD_PAIRS, 2, BIAS_PERIOD))
        n_rows = n_k if keys_on_rows else n_q
        if band:
            splits, n_rows = ((0, LANES),), len(_band_biased_blocks()) * KEY_BLOCK
        out_specs += [pl.BlockSpec((None, 2 * n_rows, hi - lo), lambda p: (p, 0, 0)) for lo, hi in splits]
        out_shape += [jax.ShapeDtypeStruct((HEAD_PAIRS, 2 * n_rows, hi - lo), F32) for lo, hi in splits]
    return _call(
        functools.partial(_bias_kernel, tables=tables), "rel_bias",
        grid=(HEAD_PAIRS,),
        in_specs=[pl.BlockSpec((None, 2, BIAS_PERIOD), lambda p: (p, 0, 0))] * len(tables),
        out_specs=out_specs,
        out_shape=out_shape,
    )(*gens)


def _band_block_kinds():
    n_k = KEY_BLOCKS * TILE
    r = np.arange(TILE)[None, :]
    w = np.arange(n_k)[:, None]
    j = w - CHUNK * (r // CHUNK)
    ok = (j >= 0) & (j < BAND_PAST + CHUNK)
    plain = ok & (BAND_PAST + r - w >= REL_CLIP)
    kinds = []
    for a in range(n_k // KEY_BLOCK):
        rows = slice(a * KEY_BLOCK, (a + 1) * KEY_BLOCK)
        blocks = [(rows, slice(b * LANES, (b + 1) * LANES)) for b in range(TILE // LANES)]
        kinds.append(["skip" if not ok[blk].any() else "plain" if plain[blk].all() else "biased"
                      for blk in blocks])
    return kinds


def _band_biased_blocks():
    kinds = _band_block_kinds()
    return [(blk, b) for blk, row in enumerate(kinds) for b, kind in enumerate(row) if kind == "biased"]


def _band_prompt_kernel(x_ref, qt_ref, qm_ref, kp_ref, kc_ref, vtp_ref, vtc_ref,
                        bias_ref, mk_ref, mv_ref, wout_ref, gpost_ref, o_ref, *, fill_steps):
    i = pl.program_id(0) - fill_steps
    n_k = KEY_BLOCKS * TILE
    kinds = _band_block_kinds()
    slots = _band_biased_blocks()
    w = lax.broadcasted_iota(jnp.int32, (n_k, LANES), 0)
    c = lax.broadcasted_iota(jnp.int32, (n_k, LANES), 1)
    ones_rows = jnp.where(lax.broadcasted_iota(jnp.int32, (BF16_ROWS, n_k), 0) == 0, 1.0, 0.0).astype(BF16)
    row = lax.broadcasted_iota(jnp.int32, (LANES, TILE), 0)
    one_hot_row = jnp.where(row == 0, 1.0, 0.0).astype(BF16)
    lo = row < HEAD_DIM

    def key_step(a, s, vth, bias_block, state):
        blocks = range(a * KEY_STEP // KEY_BLOCK, (a + 1) * KEY_STEP // KEY_BLOCK)
        es, alphas, active = [], [], []
        for b in range(TILE // LANES):
            cols = slice(b * LANES, (b + 1) * LANES)
            sjs = {}
            for blk in blocks:
                if kinds[blk][b] != "skip":
                    rows = slice(blk * KEY_BLOCK, (blk + 1) * KEY_BLOCK)
                    sjs[blk] = s[rows, cols] + bias_block(blk, b) if kinds[blk][b] == "biased" else s[rows, cols]
            active.append(bool(sjs))
            alphas.append(None)
            if not sjs:
                es.append(jnp.zeros((KEY_STEP, LANES), F32))
                continue
            mj = jnp.max(functools.reduce(jnp.maximum, sjs.values()), axis=0, keepdims=True)
            if state[b] is None:
                state[b] = [mj, None]
            else:
                m_new = jnp.maximum(state[b][0], mj)
                alphas[b] = jnp.exp(state[b][0] - m_new)
                state[b][0] = m_new
            es.append(jnp.concatenate(
                [jnp.exp(sjs[blk] - state[b][0]) if blk in sjs else jnp.zeros((KEY_BLOCK, LANES), F32)
                 for blk in blocks], axis=0))
        rows = slice(a * KEY_STEP, (a + 1) * KEY_STEP)
        pv = _dot(vth[:, rows], jnp.concatenate(es, axis=1).astype(BF16))[:OUT_ROWS]
        for b in range(TILE // LANES):
            if active[b]:
                pv_b = pv[:, b * LANES:(b + 1) * LANES]
                state[b][1] = pv_b if alphas[b] is None else state[b][1] * alphas[b] + pv_b

    def tile(t):
        q_rows = slice(t * TILE, (t + 1) * TILE)
        k_rows = slice(t * TILE, t * TILE + n_k)
        first_key_tile = i * BAND_TILES + t - (KEY_BLOCKS - 1)
        pen = jnp.where((w < -first_key_tile * TILE) & (c == 0), NEG, 0.0).astype(BF16)
        outs = []
        for p0 in range(0, HEAD_PAIRS, GROUP_PAIRS):
            heads = []
            for p in range(p0, p0 + GROUP_PAIRS):
                k_win = jnp.concatenate([kp_ref[p], kc_ref[p]], axis=0)[k_rows]
                k_ext = jnp.concatenate([k_win, pen], axis=1)
                vt = jnp.concatenate([vtp_ref[p], vtc_ref[p]], axis=1)[:, k_rows]
                qt = qt_ref[p, :, q_rows]
                zero = jnp.zeros_like(qt)
                for hh in range(2):
                    qh = jnp.where(lo, qt, zero) if hh == 0 else jnp.where(lo, zero, qt)
                    s = _dot(k_ext, jnp.concatenate([qh, one_hot_row], axis=0))
                    vth = jnp.concatenate([vt[hh * HEAD_DIM:(hh + 1) * HEAD_DIM, :], ones_rows], axis=0)
                    bias_rows = functools.partial(
                        lambda blk, b, p, hh: bias_ref[p, pl.ds((hh * len(slots) + slots.index((blk, b))) * KEY_BLOCK,
                                                                KEY_BLOCK), :],
                        p=p, hh=hh)
                    heads.append((s, vth, bias_rows, [None] * (TILE // LANES)))
            for a in range(n_k // KEY_STEP):
                for head in heads:
                    key_step(a, *head)
            outs += [jnp.concatenate([o[:HEAD_DIM] * (1.0 / o[HEAD_DIM:HEAD_DIM + 1]) for _, o in state], axis=1)
                     for *_, state in heads]
            yield
        band = jnp.concatenate(outs, axis=0).T.astype(BF16)
        mo = _mem_attend(qm_ref[q_rows, :], mk_ref[0].astype(BF16), mv_ref[0].astype(BF16))
        cat = jnp.concatenate([band, mo.astype(BF16)], axis=-1)
        o_ref[q_rows, :] = x_ref[q_rows, :] + _rms(_dot(cat, wout_ref[...]), gpost_ref[...])

    _after_fill(o_ref, fill_steps, lambda: _interleave([tile(t) for t in range(BAND_TILES)]))


def _band_prompt(x, qt3, qm, k3, vt3, bias, mkt, mvt, wout, gpost):
    n = qm.shape[0]
    rows = BAND_TILES * TILE
    assert rows == BAND_PAST and n % rows == 0 and (x.shape[0] - n) % rows == 0
    fill = (x.shape[0] - n) // rows
    slab_spec = pl.BlockSpec((rows, D_MODEL), lambda i: (i, 0))
    own = lambda i: jnp.maximum(i - fill, 0)
    past = lambda i: jnp.maximum(i - fill - 1, 0)
    return _call(
        functools.partial(_band_prompt_kernel, fill_steps=fill), "band_prompt",
        grid=(n // rows + fill,),
        in_specs=[slab_spec,
                  pl.BlockSpec((HEAD_PAIRS, LANES, rows), lambda i: (0, 0, own(i))),
                  pl.BlockSpec((rows, MEM_W), lambda i: (own(i), 0)),
                  pl.BlockSpec((HEAD_PAIRS, rows, LANES), lambda i: (0, past(i), 0)),
                  pl.BlockSpec((HEAD_PAIRS, rows, LANES), lambda i: (0, own(i), 0)),
                  pl.BlockSpec((HEAD_PAIRS, LANES, rows), lambda i: (0, 0, past(i))),
                  pl.BlockSpec((HEAD_PAIRS, LANES, rows), lambda i: (0, 0, own(i))),
                  _const_spec(bias.shape),
                  _mem_spec(mkt, 1, 1), _mem_spec(mvt, 1, 1),
                  _const_spec((B_W + MEM_W, D_MODEL)), _const_spec((1, D_MODEL))],
        out_specs=slab_spec,
        out_shape=jax.ShapeDtypeStruct(x.shape, F32),
    )(x, qt3, qm, k3, k3, vt3, vt3, bias, mkt, mvt, wout, gpost)


def _band_sample_kernel(x_ref, q_ref, qm_ref, kn_ref, vn_ref, ck_ref, cv_ref, bc_ref, bn_ref,
                        mk_ref, mv_ref, wout_ref, gpost_ref, o_ref, *, seqs, seq_len):
    lane = lax.broadcasted_iota(jnp.int32, (1, LANES), 1)
    lo = lane < HEAD_DIM

    def pair_attend(b, p, out):
        rows = slice(b * seq_len, (b + 1) * seq_len)
        hd = slice(p * LANES, (p + 1) * LANES)
        qp = q_ref[p, rows, :]
        zero = jnp.zeros_like(qp)
        qs = jnp.concatenate([jnp.where(lo, qp, zero), jnp.where(lo, zero, qp)], axis=0)
        sc = _dot(qs, ck_ref[b, hd, :].astype(BF16)) + bc_ref[p]
        sn = _dot_nt(qs, kn_ref[p, rows, :]) + bn_ref[p]
        yield
        m = jnp.maximum(_lmax(sc), _lmax(sn))
        ec = jnp.exp(sc - m)
        en = jnp.exp(sn - m)
        l = _lsum(ec) + _lsum(en)
        yield
        o = (_dot_nt(ec.astype(BF16), cv_ref[b, hd, :].astype(BF16))
             + _dot(en.astype(BF16), vn_ref[p, rows, :]))
        yield
        o = o * (1.0 / l)
        out[p] = jnp.where(lo, o[:seq_len], o[seq_len:]).astype(BF16)

    def seq_attend(b, out):
        band = [None] * HEAD_PAIRS
        yield from _round_robin([pair_attend(b, p, band) for p in range(HEAD_PAIRS)])
        rows = slice(b * seq_len, (b + 1) * seq_len)
        mo = _mem_attend(qm_ref[rows, :], mk_ref[b].astype(BF16), mv_ref[b].astype(BF16))
        out[b] = jnp.concatenate(band + [mo.astype(BF16)], axis=-1)

    rows_out = [None] * seqs
    _interleave([seq_attend(b, rows_out) for b in range(seqs)])
    cat = jnp.concatenate(rows_out, axis=0)
    o_ref[...] = x_ref[...] + _rms(_dot(cat, wout_ref[...]), gpost_ref[...])


def _band_sample(x, q3, qm, kn3, vn3, ckt, cvt, bias_c, bias_n, mkt, mvt, wout, gpost, *, into, seqs=4):
    n = qm.shape[0]
    n_seq, past = ckt.shape[0], ckt.shape[2]
    seq_len = n // n_seq
    rows = seqs * seq_len
    slab_spec = pl.BlockSpec((rows, D_MODEL), lambda i: (i, 0))
    pair_rows_spec = pl.BlockSpec((HEAD_PAIRS, rows, LANES), lambda i: (0, i, 0))
    cache_spec = pl.BlockSpec((seqs, B_W, past), lambda i: (i, 0, 0))
    return _call(
        functools.partial(_band_sample_kernel, seqs=seqs, seq_len=seq_len), "band_sample", into=into,
        grid=(n_seq // seqs,),
        in_specs=[slab_spec,
                  pair_rows_spec,
                  pl.BlockSpec((rows, MEM_W), lambda i: (i, 0)),
                  pair_rows_spec, pair_rows_spec,
                  cache_spec, cache_spec,
                  _const_spec((HEAD_PAIRS, 2 * seq_len, past)),
                  _const_spec((HEAD_PAIRS, 2 * seq_len, seq_len)),
                  _mem_spec(mkt, seqs, 1), _mem_spec(mvt, seqs, 1),
                  _const_spec((B_W + MEM_W, D_MODEL)), _const_spec((1, D_MODEL))],
        out_specs=slab_spec,
        out_shape=jax.ShapeDtypeStruct(x.shape, F32),
    )(x, q3, qm, kn3, vn3, ckt, cvt, bias_c, bias_n, mkt, mvt, wout, gpost)


def _spatial_tile(w_s, b_s, period):
    tril = jnp.tril(jnp.ones((GM_CHUNK, GM_CHUNK), dtype=bool))
    w = jnp.where(tril, w_s, jnp.zeros((), w_s.dtype))[:, :period, :period]
    eye = jnp.eye(TILE // period, dtype=w.dtype)
    s_mat = jnp.einsum("ab,gts->gatbs", eye, w).reshape(GM_GROUPS, TILE, TILE)
    rows = jnp.tile(b_s[:, :period], (1, TILE // period))
    bs = jnp.repeat(rows.T, GM_GW, axis=1)
    return s_mat.astype(BF16), bs


def _heads_last(t, lead):
    pos = t.shape[-1]
    t = t.reshape(lead + (-1, HEAD_DIM, pos))
    nd = len(lead)
    return jnp.transpose(t, tuple(range(nd)) + (nd + 2, nd, nd + 1))


def _positions_last(c):
    nd = c.ndim
    t = jnp.transpose(c, tuple(range(nd - 3)) + (nd - 2, nd - 1, nd - 3))
    return t.reshape(c.shape[:-3] + (c.shape[-2] * c.shape[-1], c.shape[-3]))


def kernel(x_prompt, x_sample, cache_mem_k, cache_mem_v, cache_band_k, cache_band_v, mem_prompt,
           g_mix_pre, g_mix_post, g_ffn_pre, g_ffn_post, g_mem, w_mem_kv,
           w_in_a, g_gm_ln, b_gm_ln, w_spatial, b_spatial, w_out_a,
           g_kv, w_kv, w_in_b, rel_bias, w_out_b, w_ff1, w_ff2):
    seq = x_prompt.shape[1]
    n_seq, seq_len = x_sample.shape[0], x_sample.shape[1]
    past = cache_band_k.shape[1]
    vec = lambda a: a.reshape(1, -1)
    stack = lambda a: a.reshape(DEPTH, 1, -1)

    win_a = w_in_a[0].astype(BF16)
    wout_a = w_out_a[0].astype(BF16)
    wk = w_kv[:, :B_W].astype(BF16)
    wv = w_kv[:, B_W:].astype(BF16)
    ln_g, ln_b = vec(g_gm_ln[0]), vec(b_gm_ln[0])
    gkv, gpre_b = vec(g_kv), vec(g_mix_pre[1])
    gf_pre, gf_post = stack(g_ffn_pre), stack(g_ffn_post)

    n_sample = n_seq * seq_len
    slab_rows = seq + n_sample
    pre_a, post_a = vec(g_mix_pre[0]), vec(g_mix_post[0])

    mem_kt, mem_vt = _memkv(mem_prompt[0], g_mem, w_mem_kv)
    s_p, bs_p = _spatial_tile(w_spatial[0], b_spatial[0], GM_CHUNK)
    s_s, bs_s = _spatial_tile(w_spatial[0], b_spatial[0], seq_len)
    cmkt, cmvt = _positions_last(cache_mem_k), _positions_last(cache_mem_v)
    x, w1, w2, win_b, wout_b = _mixer_a(
        x_prompt[0], pre_a, post_a, win_a, ln_g, ln_b, s_p, bs_p, mem_kt, mem_vt, wout_a,
        rows_per_mem=TILE, emit_v=False, tiles=4, slab_rows=slab_rows, slab_offset=n_sample,
        cast=((w_ff1, 0), (w_ff2, 0), (w_in_b, 0), (w_out_b, 0)))
    x, v_rows = _mixer_a(x_sample.reshape(n_sample, D_MODEL), pre_a, post_a, win_a, ln_g, ln_b, s_s, bs_s,
                         cmkt, cmvt, wout_a, rows_per_mem=seq_len, emit_v=True,
                         slab_rows=slab_rows, into=x)
    x = _ffn(x, gf_pre, gf_post, w1, w2, 0)

    qt3, qm, k3, vt3, w1, w2 = _proj_b(x, gkv, gpre_b, (win_b, wk, wv), mode="prompt", rows=2 * PROJ_ROWS,
                                       row_offset=n_sample, n_rows=seq, cast=((w_ff1, 1), (w_ff2, 1)))
    n_keep = min(BAND_PAST, seq)
    q3s, qms, kn3, vn3, k_new, v_new, kt_tail, vt_tail = _proj_b(
        x, gkv, gpre_b, (win_b, wk, wv), mode="sample", rows=512, n_rows=n_sample, seq_len=seq_len,
        tail_offset=n_sample + seq - n_keep, tail_rows=n_keep)
    bias_p, bias_c, bias_n = _rel_bias_tables(rel_bias[0], (
        (TILE, BAND_PAST + TILE, (), True, True),
        (seq_len, past + seq_len, ((0, past), (past, past + seq_len)), False, False)))
    post_b = vec(g_mix_post[1])
    y = _band_prompt(x, qt3, qm, k3, vt3, bias_p, mem_kt, mem_vt, wout_b, post_b)
    y = _band_sample(x, q3s, qms, kn3, vn3, _positions_last(cache_band_k), _positions_last(cache_band_v),
                     bias_c, bias_n, cmkt, cmvt, wout_b, post_b, into=y)
    y_sample, y_prompt = _ffn(y, gf_pre, gf_post, w1, w2, 1, split=n_sample)
    y_prompt = y_prompt[None]
    y_sample = y_sample.reshape(n_seq, seq_len, D_MODEL)

    return (y_prompt, y_sample,
            _heads_last(mem_kt, (DEPTH, 1)), _heads_last(mem_vt, (DEPTH, 1)),
            _heads_last(kt_tail, (1,)), _heads_last(vt_tail, (1,)),
            v_rows.reshape(1, n_seq, seq_len, GM_W),
            jnp.swapaxes(k_new, 1, 2), jnp.swapaxes(v_new, 1, 2))
```

```python
import functools

import jax
import jax.numpy as jnp
import numpy as np
from jax import lax
from jax.experimental import pallas as pl
from jax.experimental.pallas import tpu as pltpu

D_MODEL = 1024
DEPTH = 2
CHUNK = 64
HEAD_DIM = 64
GM_CHUNK = 128
GM_GROUPS = 4
GM_W = 768
GM_GW = GM_W // GM_GROUPS
MEM_LEN = 256
MEM_HEADS = 4
MEM_W = MEM_HEADS * HEAD_DIM
B_HEADS = 12
B_W = B_HEADS * HEAD_DIM
BAND_PAST = 512
REL_CLIP = 128
D_FF = 4 * D_MODEL
EPS = 1e-6

LANES = 128
SUBLANES = 8
HEAD_PAIRS = B_W // LANES
Q_SCALE = HEAD_DIM ** -0.5
NEG = -1e30
TILE = 256
KEY_BLOCKS = BAND_PAST // TILE + 1
KEY_STEP = 128
KEY_BLOCK = 128
PROJ_ROWS = 512
BF16_ROWS = 2 * SUBLANES
OUT_ROWS = HEAD_DIM + SUBLANES
BAND_TILES = BAND_PAST // TILE
GROUP_PAIRS = 2
BIAS_PERIOD = 1024
V7X_VMEM_BYTES = 64 * 1024 * 1024
VMEM_LIMIT = V7X_VMEM_BYTES * 7 // 8

BF16 = jnp.bfloat16
F32 = jnp.float32


def _dot(a, b):
    return jnp.dot(a, b, preferred_element_type=F32)


def _dot_nt(a, b):
    return lax.dot_general(a, b, (((1,), (1,)), ((), ())), preferred_element_type=F32)


def _rms(x, g):
    ms = jnp.mean(x * x, axis=-1, keepdims=True)
    return x * lax.rsqrt(ms + EPS) * g


def _lsum(a):
    return jnp.sum(a, axis=-1, keepdims=True)


def _lmax(a):
    return jnp.max(a, axis=-1, keepdims=True)


def _const_spec(shape):
    nd = len(shape)
    return pl.BlockSpec(shape, lambda *_: (0,) * nd, pipeline_mode=pl.Buffered(1))


def _layer_spec(shape, layer):
    nd = len(shape)
    return pl.BlockSpec((None,) + shape, lambda *_: (layer,) + (0,) * nd, pipeline_mode=pl.Buffered(1))


def _without_ref(body, k):
    def wrapped(*refs):
        return body(*refs[:k], *refs[k + 1:])
    return wrapped


def _call(body, name, *, in_specs, into=None, semantics="parallel", **kw):
    params = pltpu.CompilerParams(dimension_semantics=(semantics,), vmem_limit_bytes=VMEM_LIMIT)
    if into is None:
        return pl.pallas_call(body, name=name, in_specs=in_specs, compiler_params=params, **kw)
    n_in = len(in_specs)
    call = pl.pallas_call(_without_ref(body, n_in), name=name,
                          in_specs=[*in_specs, pl.BlockSpec(memory_space=pl.ANY)],
                          input_output_aliases={n_in: 0}, compiler_params=params, **kw)
    return lambda *args: call(*args, into)


def _memkv_kernel(mem_ref, g_ref, w_ref, kt_ref, vt_ref):
    h = _dot(_rms(mem_ref[...], g_ref[...]).astype(BF16), w_ref[...].astype(BF16))
    kt_ref[0] = h[:, :MEM_W].T
    vt_ref[0] = h[:, MEM_W:].T


def _memkv(mem, g_mem, w_mem_kv):
    out = jax.ShapeDtypeStruct((DEPTH, 1, MEM_W, MEM_LEN), F32)
    return _call(
        _memkv_kernel, "mem_kv",
        grid=(DEPTH,),
        in_specs=[
            pl.BlockSpec((MEM_LEN, D_MODEL), lambda l: (0, 0)),
            pl.BlockSpec((None, 1, D_MODEL), lambda l: (l, 0, 0)),
            pl.BlockSpec((None, D_MODEL, 2 * MEM_W), lambda l: (l, 0, 0)),
        ],
        out_specs=[pl.BlockSpec((None, 1, MEM_W, MEM_LEN), lambda l: (l, 0, 0, 0))] * 2,
        out_shape=[out, out],
    )(mem, g_mem.reshape(DEPTH, 1, D_MODEL), w_mem_kv)


def _mem_attend(qb, kt, vt):
    r = qb.shape[0]
    lane = lax.broadcasted_iota(jnp.int32, (1, MEM_W), 1)
    masks = [(lane >= h * HEAD_DIM) & (lane < (h + 1) * HEAD_DIM) for h in range(MEM_HEADS)]
    qs = jnp.concatenate([jnp.where(m, qb, jnp.zeros_like(qb)) for m in masks], axis=0)
    s = _dot(qs, kt)
    e = jnp.exp(s - _lmax(s))
    pv = _dot_nt(e.astype(BF16), vt) * (1.0 / _lsum(e))
    out = jnp.where(masks[0], pv[:r], 0.0)
    for h in range(1, MEM_HEADS):
        out = out + jnp.where(masks[h], pv[h * r:(h + 1) * r], 0.0)
    return out


def _gelu(x):
    c1 = float(np.sqrt(2.0 / np.pi))
    c2 = c1 * 0.044715
    half = 0.5 * x
    return half + half * jnp.tanh(x * (c1 + c2 * (x * x)))


def _cast_specs(cast, steps, fill):
    own = lambda i: jnp.maximum(i - fill, 0)
    in_specs, out_specs, out_shape = [], [], []
    for a, layer in cast:
        _, r, c = a.shape
        in_specs.append(pl.BlockSpec((None, r // steps, c), functools.partial(lambda i, l: (l, own(i), 0), l=layer)))
        out_specs.append(pl.BlockSpec((r // steps, c), lambda i: (own(i), 0)))
        out_shape.append(jax.ShapeDtypeStruct((r, c), BF16))
    return in_specs, out_specs, out_shape


def _split_cast_refs(refs, n_cast):
    return refs[:n_cast], refs[n_cast:len(refs) - n_cast], refs[len(refs) - n_cast:]


def _cast_blocks(srcs, dsts, stages=4):
    for src, dst in zip(srcs, dsts):
        step = src.shape[0] // stages
        for k in range(stages):
            dst[k * step:(k + 1) * step, :] = src[k * step:(k + 1) * step, :].astype(dst.dtype)
            yield


def _after_fill(o_ref, fill_steps, body):
    if not fill_steps:
        return body()

    @pl.when(pl.program_id(0) < fill_steps)
    def _():
        o_ref[...] = jnp.zeros(o_ref.shape, o_ref.dtype)

    @pl.when(pl.program_id(0) >= fill_steps)
    def _():
        body()


def _round_robin(stage_lists):
    live = list(stage_lists)
    while live:
        for g in list(live):
            if next(g, StopIteration) is StopIteration:
                live.remove(g)
        yield


def _interleave(stage_lists):
    for _ in _round_robin(stage_lists):
        pass


def _mixer_a_kernel(x_ref, gpre_ref, gpost_ref, win_ref, gln_ref, bln_ref, s_ref, bs_ref,
                    mk_ref, mv_ref, wout_ref, *rest, rows_per_mem, tiles, shared_mem, fill_steps, n_cast):
    cast_in, (o_ref, *v_out), cast_out = _split_cast_refs(rest, n_cast)
    nt = GM_W // LANES
    seqs = TILE // rows_per_mem
    lane = lax.broadcasted_iota(jnp.int32, (1, LANES), 1)
    lo = lane < (GM_GW - LANES)
    inv = 1.0 / GM_GW

    def group_stat(a):
        s0 = _lsum(a[0] + jnp.where(lo, a[1], 0.0)) * inv
        s1 = _lsum(jnp.where(lo, 0.0, a[1]) + a[2]) * inv
        s2 = _lsum(a[3] + jnp.where(lo, a[4], 0.0)) * inv
        s3 = _lsum(jnp.where(lo, 0.0, a[4]) + a[5]) * inv
        return [s0, jnp.where(lo, s0, s1), s1, s2, jnp.where(lo, s2, s3), s3]

    def tile(t):
        rows = slice(t * TILE, (t + 1) * TILE)
        z = _dot(_rms(x_ref[rows, :], gpre_ref[...]).astype(BF16), win_ref[...])
        yield
        u = [_gelu(z[:, j * LANES:(j + 1) * LANES]) for j in range(nt)]
        g = [_gelu(z[:, GM_W + j * LANES:GM_W + (j + 1) * LANES]) for j in range(nt)]
        mu = group_stat(g)
        c = [g[j] - mu[j] for j in range(nt)]
        var = group_stat([cj * cj for cj in c])
        gln = gln_ref[...]
        bln = bln_ref[...]
        vn = [c[j] * lax.rsqrt(var[j] + EPS) * gln[:, j * LANES:(j + 1) * LANES]
              + bln[:, j * LANES:(j + 1) * LANES] for j in range(nt)]
        if v_out:
            v_out[0][rows, :] = jnp.concatenate(vn, axis=-1)
        yield
        vb = [a.astype(BF16) for a in vn]
        win = [(0, 1), (1, 2), (3, 4), (4, 5)]
        m = [_dot(s_ref[k], jnp.concatenate([vb[a], vb[b]], axis=-1)) for k, (a, b) in enumerate(win)]
        mixed = [m[0][:, :LANES], jnp.where(lo, m[0][:, LANES:], m[1][:, :LANES]), m[1][:, LANES:],
                 m[2][:, :LANES], jnp.where(lo, m[2][:, LANES:], m[3][:, :LANES]), m[3][:, LANES:]]
        bs = bs_ref[...]
        gm = [u[j] * (mixed[j] + bs[:, j * LANES:(j + 1) * LANES]) for j in range(nt)]
        yield
        qm = (z[:, 2 * GM_W:] * Q_SCALE).astype(BF16)
        mo = []
        for b in range(seqs):
            r = slice(b * rows_per_mem, (b + 1) * rows_per_mem)
            mi = 0 if shared_mem else t * seqs + b
            mo.append(_mem_attend(qm[r], mk_ref[mi].astype(BF16), mv_ref[mi].astype(BF16)))
        mo = mo[0] if len(mo) == 1 else jnp.concatenate(mo, axis=0)
        yield
        cat = jnp.concatenate([a.astype(BF16) for a in gm] + [mo.astype(BF16)], axis=-1)
        o_ref[rows, :] = x_ref[rows, :] + _rms(_dot(cat, wout_ref[...]), gpost_ref[...])

    def body():
        _interleave([tile(t) for t in range(tiles)] + [_cast_blocks(cast_in, cast_out)])

    _after_fill(o_ref, fill_steps, body)


def _mem_spec(mem, seqs, layer):
    if mem.shape[1] == 1:
        return pl.BlockSpec((None, 1, MEM_W, MEM_LEN), lambda i: (layer, 0, 0, 0))
    return pl.BlockSpec((None, seqs, MEM_W, MEM_LEN), lambda i: (layer, i, 0, 0))


def _mixer_a(x, gpre, gpost, win, gln, bln, s_mat, bs, mkt, mvt, wout, *, rows_per_mem, emit_v, tiles=2,
             slab_rows, slab_offset=0, into=None, cast=()):
    n = x.shape[0]
    blk = tiles * TILE
    seqs = blk // rows_per_mem
    off = slab_offset // blk
    fill, shift = (0, off) if into is not None else (off, 0)
    row = lambda w: pl.BlockSpec((blk, w), lambda i: (jnp.maximum(i - fill, 0), 0))
    out_shape = [jax.ShapeDtypeStruct((slab_rows, D_MODEL), F32)]
    out_specs = [pl.BlockSpec((blk, D_MODEL), lambda i: (i + shift, 0))]
    if emit_v:
        out_shape.append(jax.ShapeDtypeStruct((n, GM_W), F32))
        out_specs.append(row(GM_W))
    cast_in_specs, cast_out_specs, cast_out_shape = _cast_specs(cast, n // blk, fill)
    return _call(
        functools.partial(_mixer_a_kernel, rows_per_mem=rows_per_mem, tiles=tiles,
                          shared_mem=mkt.shape[1] == 1, fill_steps=fill, n_cast=len(cast)), "mixer_a", into=into,
        semantics="arbitrary" if fill and cast else "parallel",
        grid=(n // blk + fill,),
        in_specs=[
            row(D_MODEL),
            _const_spec((1, D_MODEL)), _const_spec((1, D_MODEL)),
            _const_spec((D_MODEL, 2 * GM_W + MEM_W)),
            _const_spec((1, GM_W)), _const_spec((1, GM_W)),
            _const_spec((GM_GROUPS, TILE, TILE)), _const_spec((TILE, GM_W)),
            _mem_spec(mkt, seqs, 0), _mem_spec(mvt, seqs, 0),
            _const_spec((GM_W + MEM_W, D_MODEL)),
            *cast_in_specs,
        ],
        out_specs=out_specs + cast_out_specs,
        out_shape=out_shape + cast_out_shape,
    )(x, gpre, gpost, win, gln, bln, s_mat, bs, mkt, mvt, wout, *[a for a, _ in cast])


def _ffn_kernel(x_ref, gpre_ref, gpost_ref, w1_ref, w2_ref, o_ref, *o_tail, tile_rows, ff_chunk, head_steps):
    def tile(t):
        r = slice(sum(tile_rows[:t]), sum(tile_rows[:t + 1]))
        x = x_ref[r, :]
        xn = _rms(x, gpre_ref[...]).astype(BF16)
        acc = jnp.zeros(x.shape, F32)
        yield
        for c in range(D_FF // ff_chunk):
            h = _dot(xn, w1_ref[:, c * ff_chunk:(c + 1) * ff_chunk])
            h = jnp.square(jnp.maximum(h, 0.0)).astype(BF16)
            acc = acc + _dot(h, w2_ref[c * ff_chunk:(c + 1) * ff_chunk, :])
            yield
        y = x + _rms(acc, gpost_ref[...])
        if not o_tail:
            o_ref[r, :] = y
        else:
            is_head = pl.program_id(0) < head_steps
            o_ref[r, :] = jnp.where(is_head, y, o_ref[r, :])
            o_tail[0][r, :] = y

    if o_tail:
        @pl.when(pl.program_id(0) < head_steps)
        def _():
            o_ref[...] = jnp.zeros(o_ref.shape, F32)

    _interleave([tile(t) for t in range(len(tile_rows))])


def _ffn(x, gpre, gpost, w1, w2, layer, *, tile_rows=(256, 512, 256), ff_chunk=1024, split=None):
    n = x.shape[0]
    blk = sum(tile_rows)
    row = pl.BlockSpec((blk, D_MODEL), lambda i: (i, 0))
    if split is None:
        head_steps, out_specs, out_shape = None, row, jax.ShapeDtypeStruct((n, D_MODEL), F32)
    else:
        head_steps = split // blk
        out_specs = [pl.BlockSpec((blk, D_MODEL), lambda i: (jnp.minimum(i, head_steps - 1), 0)),
                     pl.BlockSpec((blk, D_MODEL), lambda i: (jnp.maximum(i - head_steps, 0), 0))]
        out_shape = [jax.ShapeDtypeStruct((split, D_MODEL), F32), jax.ShapeDtypeStruct((n - split, D_MODEL), F32)]
    return _call(
        functools.partial(_ffn_kernel, tile_rows=tile_rows, ff_chunk=ff_chunk, head_steps=head_steps),
        "ffn", semantics="arbitrary",
        grid=(n // blk,),
        in_specs=[row, _layer_spec((1, D_MODEL), layer), _layer_spec((1, D_MODEL), layer),
                  _const_spec((D_MODEL, D_FF)), _const_spec((D_FF, D_MODEL))],
        out_specs=out_specs,
        out_shape=out_shape,
    )(x, gpre, gpost, w1, w2)


def _proj_b_kernel(x_ref, gkv_ref, gpre_ref, *refs, mode, seq_len, n_weights, n_cast):
    rows = x_ref.shape[0]
    cast_in, outs, cast_out = _split_cast_refs(refs[n_weights:], n_cast)
    refs = (*refs[:n_weights], *outs)

    def normed(r, src=x_ref):
        x = src[r, :]
        xh = x * lax.rsqrt(jnp.mean(x * x, axis=-1, keepdims=True) + EPS)
        return (xh * gkv_ref[...]).astype(BF16), (xh * gpre_ref[...]).astype(BF16)

    if mode == "prompt":
        win_ref, wkv_ref, qt_ref, qm_ref, k_ref, vt_ref = refs

        def tile(t):
            r = slice(t * PROJ_ROWS, (t + 1) * PROJ_ROWS)
            xkv, xq = normed(r)
            yield
            z = _dot(xq, win_ref[...]) * Q_SCALE
            qm_ref[r, :] = z[:, B_W:].astype(BF16)
            for p in range(HEAD_PAIRS):
                qt_ref[p, :, r] = z[:, p * LANES:(p + 1) * LANES].T.astype(BF16)
            yield
            v = _dot(xkv, wkv_ref[:, B_W:])
            for p in range(HEAD_PAIRS):
                vt_ref[p, :, r] = v[:, p * LANES:(p + 1) * LANES].T.astype(BF16)
            yield
            k = _dot(xkv, wkv_ref[:, :B_W])
            for p in range(HEAD_PAIRS):
                k_ref[p, r, :] = k[:, p * LANES:(p + 1) * LANES].astype(BF16)

        _interleave([tile(t) for t in range(rows // PROJ_ROWS)] + [_cast_blocks(cast_in, cast_out)])
        return
    win_ref, wkv_ref, xt_ref, q_ref, qm_ref, k_ref, v_ref, k4_ref, v4_ref, kt_ref, vt_ref = refs
    xkv_tail, _ = normed(slice(None), xt_ref)
    kt_ref[...] = _dot(xkv_tail, wkv_ref[:, :B_W]).T
    vt_ref[...] = _dot(xkv_tail, wkv_ref[:, B_W:]).T
    xkv, xq = normed(slice(None))
    z = _dot(xq, win_ref[...]) * Q_SCALE
    qm_ref[...] = z[:, B_W:].astype(BF16)
    k = _dot(xkv, wkv_ref[:, :B_W])
    v = _dot(xkv, wkv_ref[:, B_W:])
    for p in range(HEAD_PAIRS):
        cols = slice(p * LANES, (p + 1) * LANES)
        q_ref[p] = z[:, cols].astype(BF16)
        k_ref[p] = k[:, cols].astype(BF16)
        v_ref[p] = v[:, cols].astype(BF16)
    for b in range(rows // seq_len):
        for h in range(B_HEADS):
            r, c = slice(b * seq_len, (b + 1) * seq_len), slice(h * HEAD_DIM, (h + 1) * HEAD_DIM)
            k4_ref[b, h] = k[r, c]
            v4_ref[b, h] = v[r, c]


def _proj_b(x, gkv, gpre, weights, *, mode, rows, row_offset=0, n_rows=None, seq_len=None, cast=(),
            tail_offset=None, tail_rows=None):
    n = x.shape[0] if n_rows is None else n_rows
    off = row_offset // rows
    steps = n // rows
    pair_rows = jax.ShapeDtypeStruct((HEAD_PAIRS, n, LANES), BF16)
    pair_rows_spec = pl.BlockSpec((HEAD_PAIRS, rows, LANES), lambda i: (0, i, 0))
    pair_cols = jax.ShapeDtypeStruct((HEAD_PAIRS, LANES, n), BF16)
    pair_cols_spec = pl.BlockSpec((HEAD_PAIRS, LANES, rows), lambda i: (0, 0, i))
    qm = jax.ShapeDtypeStruct((n, MEM_W), BF16)
    qm_spec = pl.BlockSpec((rows, MEM_W), lambda i: (i, 0))
    extra_in, extra_specs = [], []
    if mode == "prompt":
        out_shape = [pair_cols, qm, pair_rows, pair_cols]
        out_specs = [pair_cols_spec, qm_spec, pair_rows_spec, pair_cols_spec]
    else:
        seqs = rows // seq_len
        per_head = jax.ShapeDtypeStruct((n // seq_len, B_HEADS, seq_len, HEAD_DIM), F32)
        per_head_spec = pl.BlockSpec((seqs, B_HEADS, seq_len, HEAD_DIM), lambda i: (i, 0, 0, 0))
        tail_blk = tail_rows // steps
        tail_off = tail_offset // tail_blk
        extra_in, extra_specs = [x], [pl.BlockSpec((tail_blk, D_MODEL), lambda i: (i + tail_off, 0))]
        out_shape = ([pair_rows, qm, pair_rows, pair_rows, per_head, per_head]
                     + [jax.ShapeDtypeStruct((B_W, tail_rows), F32)] * 2)
        out_specs = ([pair_rows_spec, qm_spec, pair_rows_spec, pair_rows_spec, per_head_spec, per_head_spec]
                     + [pl.BlockSpec((B_W, tail_blk), lambda i: (0, i))] * 2)
    cast_in_specs, cast_out_specs, cast_out_shape = _cast_specs(cast, steps, 0)
    return _call(
        functools.partial(_proj_b_kernel, mode=mode, seq_len=seq_len, n_weights=len(weights) + len(extra_in),
                          n_cast=len(cast)), "proj_b_" + mode,
        grid=(steps,),
        in_specs=[pl.BlockSpec((rows, D_MODEL), lambda i: (i + off, 0)),
                  _const_spec((1, D_MODEL)), _const_spec((1, D_MODEL))]
                 + [_const_spec(w.shape) for w in weights] + extra_specs + cast_in_specs,
        out_specs=out_specs + cast_out_specs,
        out_shape=out_shape + cast_out_shape,
    )(x, gkv, gpre, *weights, *extra_in, *[a for a, _ in cast])


def _bias_kernel(*refs, tables):
    g_refs, o_refs = refs[:len(tables)], list(refs[len(tables):])
    for g_ref, (n_q, n_k, splits, band, keys_on_rows) in zip(g_refs, tables):
        outs = [o_refs.pop(0) for _ in splits] if not band else [o_refs.pop(0)]
        n_rows, shift = (n_k, n_q) if keys_on_rows else (n_q, BIAS_PERIOD - (n_q - 1))
        for hh in range(2):
            x = jnp.broadcast_to(g_ref[hh:hh + 1, :], (n_rows, BIAS_PERIOD))
            t = pltpu.roll(x, shift, 1, stride=1, stride_axis=0)
            if band:
                a = lax.broadcasted_iota(jnp.int32, (n_rows, BIAS_PERIOD), 0)
                b = lax.broadcasted_iota(jnp.int32, (n_rows, BIAS_PERIOD), 1)
                r, w = (b, a) if keys_on_rows else (a, b)
                j = w - (r - (r & (CHUNK - 1)))
                t = jnp.where((j >= 0) & (j < BAND_PAST + CHUNK), t, NEG)
            if band:
                slots = _band_biased_blocks()
                for k, (blk, b) in enumerate(slots):
                    row0 = (hh * len(slots) + k) * KEY_BLOCK
                    outs[0][row0:row0 + KEY_BLOCK, :] = t[blk * KEY_BLOCK:(blk + 1) * KEY_BLOCK,
                                                          b * LANES:(b + 1) * LANES]
                continue
            for o_ref, (lo, hi) in zip(outs, splits):
                o_ref[hh * n_rows:(hh + 1) * n_rows, :] = t[:, lo:hi]


def _rel_bias_tables(rel_bias, tables):
    rel_bias = rel_bias - rel_bias[:, -1:]
    gens, out_specs, out_shape = [], [], []
    for n_q, n_k, splits, band, keys_on_rows in tables:
        c0 = n_k - 1
        far = jnp.broadcast_to(rel_bias[:, -1:], (B_HEADS, c0 - REL_CLIP))
        near = jnp.broadcast_to(rel_bias[:, :1], (B_HEADS, BIAS_PERIOD - (c0 - REL_CLIP) - (2 * REL_CLIP + 1)))
        gen = [near, rel_bias, far] if keys_on_rows else [far, rel_bias[:, ::-1], near]
        gens.append(jnp.concatenate(gen, axis=1).reshape(HEAD_PAIRS, 2, BIAS_PERIOD))
        n_rows = n_k if keys_on_rows else n_q
        if band:
            splits, n_rows = ((0, LANES),), len(_band_biased_blocks()) * KEY_BLOCK
        out_specs += [pl.BlockSpec((None, 2 * n_rows, hi - lo), lambda p: (p, 0, 0)) for lo, hi in splits]
        out_shape += [jax.ShapeDtypeStruct((HEAD_PAIRS, 2 * n_rows, hi - lo), F32) for lo, hi in splits]
    return _call(
        functools.partial(_bias_kernel, tables=tables), "rel_bias",
        grid=(HEAD_PAIRS,),
        in_specs=[pl.BlockSpec((None, 2, BIAS_PERIOD), lambda p: (p, 0, 0))] * len(tables),
        out_specs=out_specs,
        out_shape=out_shape,
    )(*gens)


def _band_block_kinds():
    n_k = KEY_BLOCKS * TILE
    r = np.arange(TILE)[None, :]
    w = np.arange(n_k)[:, None]
    j = w - CHUNK * (r // CHUNK)
    ok = (j >= 0) & (j < BAND_PAST + CHUNK)
    plain = ok & (BAND_PAST + r - w >= REL_CLIP)
    kinds = []
    for a in range(n_k // KEY_BLOCK):
        rows = slice(a * KEY_BLOCK, (a + 1) * KEY_BLOCK)
        blocks = [(rows, slice(b * LANES, (b + 1) * LANES)) for b in range(TILE // LANES)]
        kinds.append(["skip" if not ok[blk].any() else "plain" if plain[blk].all() else "biased"
                      for blk in blocks])
    return kinds


def _band_biased_blocks():
    kinds = _band_block_kinds()
    return [(blk, b) for blk, row in enumerate(kinds) for b, kind in enumerate(row) if kind == "biased"]


def _band_prompt_kernel(x_ref, qt_ref, qm_ref, kp_ref, kc_ref, vtp_ref, vtc_ref,
                        bias_ref, mk_ref, mv_ref, wout_ref, gpost_ref, o_ref, *, fill_steps):
    i = pl.program_id(0) - fill_steps
    n_k = KEY_BLOCKS * TILE
    kinds = _band_block_kinds()
    slots = _band_biased_blocks()
    w = lax.broadcasted_iota(jnp.int32, (n_k, LANES), 0)
    c = lax.broadcasted_iota(jnp.int32, (n_k, LANES), 1)
    ones_rows = jnp.where(lax.broadcasted_iota(jnp.int32, (BF16_ROWS, n_k), 0) == 0, 1.0, 0.0).astype(BF16)
    row = lax.broadcasted_iota(jnp.int32, (LANES, TILE), 0)
    one_hot_row = jnp.where(row == 0, 1.0, 0.0).astype(BF16)
    lo = row < HEAD_DIM

    def key_step(a, s, vth, bias_block, state):
        blocks = range(a * KEY_STEP // KEY_BLOCK, (a + 1) * KEY_STEP // KEY_BLOCK)
        es, alphas, active = [], [], []
        for b in range(TILE // LANES):
            cols = slice(b * LANES, (b + 1) * LANES)
            sjs = {}
            for blk in blocks:
                if kinds[blk][b] != "skip":
                    rows = slice(blk * KEY_BLOCK, (blk + 1) * KEY_BLOCK)
                    sjs[blk] = s[rows, cols] + bias_block(blk, b) if kinds[blk][b] == "biased" else s[rows, cols]
            active.append(bool(sjs))
            alphas.append(None)
            if not sjs:
                es.append(jnp.zeros((KEY_STEP, LANES), F32))
                continue
            mj = jnp.max(functools.reduce(jnp.maximum, sjs.values()), axis=0, keepdims=True)
            if state[b] is None:
                state[b] = [mj, None]
            else:
                m_new = jnp.maximum(state[b][0], mj)
                alphas[b] = jnp.exp(state[b][0] - m_new)
                state[b][0] = m_new
            es.append(jnp.concatenate(
                [jnp.exp(sjs[blk] - state[b][0]) if blk in sjs else jnp.zeros((KEY_BLOCK, LANES), F32)
                 for blk in blocks], axis=0))
        rows = slice(a * KEY_STEP, (a + 1) * KEY_STEP)
        pv = _dot(vth[:, rows], jnp.concatenate(es, axis=1).astype(BF16))[:OUT_ROWS]
        for b in range(TILE // LANES):
            if active[b]:
                pv_b = pv[:, b * LANES:(b + 1) * LANES]
                state[b][1] = pv_b if alphas[b] is None else state[b][1] * alphas[b] + pv_b

    def tile(t):
        q_rows = slice(t * TILE, (t + 1) * TILE)
        k_rows = slice(t * TILE, t * TILE + n_k)
        first_key_tile = i * BAND_TILES + t - (KEY_BLOCKS - 1)
        pen = jnp.where((w < -first_key_tile * TILE) & (c == 0), NEG, 0.0).astype(BF16)
        outs = []
        for p0 in range(0, HEAD_PAIRS, GROUP_PAIRS):
            heads = []
            for p in range(p0, p0 + GROUP_PAIRS):
                k_win = jnp.concatenate([kp_ref[p], kc_ref[p]], axis=0)[k_rows]
                k_ext = jnp.concatenate([k_win, pen], axis=1)
                vt = jnp.concatenate([vtp_ref[p], vtc_ref[p]], axis=1)[:, k_rows]
                qt = qt_ref[p, :, q_rows]
                zero = jnp.zeros_like(qt)
                for hh in range(2):
                    qh = jnp.where(lo, qt, zero) if hh == 0 else jnp.where(lo, zero, qt)
                    s = _dot(k_ext, jnp.concatenate([qh, one_hot_row], axis=0))
                    vth = jnp.concatenate([vt[hh * HEAD_DIM:(hh + 1) * HEAD_DIM, :], ones_rows], axis=0)
                    bias_rows = functools.partial(
                        lambda blk, b, p, hh: bias_ref[p, pl.ds((hh * len(slots) + slots.index((blk, b))) * KEY_BLOCK,
                                                                KEY_BLOCK), :],
                        p=p, hh=hh)
                    heads.append((s, vth, bias_rows, [None] * (TILE // LANES)))
            for a in range(n_k // KEY_STEP):
                for head in heads:
                    key_step(a, *head)
            outs += [jnp.concatenate([o[:HEAD_DIM] * (1.0 / o[HEAD_DIM:HEAD_DIM + 1]) for _, o in state], axis=1)
                     for *_, state in heads]
            yield
        band = jnp.concatenate(outs, axis=0).T.astype(BF16)
        mo = _mem_attend(qm_ref[q_rows, :], mk_ref[0].astype(BF16), mv_ref[0].astype(BF16))
        cat = jnp.concatenate([band, mo.astype(BF16)], axis=-1)
        o_ref[q_rows, :] = x_ref[q_rows, :] + _rms(_dot(cat, wout_ref[...]), gpost_ref[...])

    _after_fill(o_ref, fill_steps, lambda: _interleave([tile(t) for t in range(BAND_TILES)]))


def _band_prompt(x, qt3, qm, k3, vt3, bias, mkt, mvt, wout, gpost):
    n = qm.shape[0]
    rows = BAND_TILES * TILE
    assert rows == BAND_PAST and n % rows == 0 and (x.shape[0] - n) % rows == 0
    fill = (x.shape[0] - n) // rows
    slab_spec = pl.BlockSpec((rows, D_MODEL), lambda i: (i, 0))
    own = lambda i: jnp.maximum(i - fill, 0)
    past = lambda i: jnp.maximum(i - fill - 1, 0)
    return _call(
        functools.partial(_band_prompt_kernel, fill_steps=fill), "band_prompt",
        grid=(n // rows + fill,),
        in_specs=[slab_spec,
                  pl.BlockSpec((HEAD_PAIRS, LANES, rows), lambda i: (0, 0, own(i))),
                  pl.BlockSpec((rows, MEM_W), lambda i: (own(i), 0)),
                  pl.BlockSpec((HEAD_PAIRS, rows, LANES), lambda i: (0, past(i), 0)),
                  pl.BlockSpec((HEAD_PAIRS, rows, LANES), lambda i: (0, own(i), 0)),
                  pl.BlockSpec((HEAD_PAIRS, LANES, rows), lambda i: (0, 0, past(i))),
                  pl.BlockSpec((HEAD_PAIRS, LANES, rows), lambda i: (0, 0, own(i))),
                  _const_spec(bias.shape),
                  _mem_spec(mkt, 1, 1), _mem_spec(mvt, 1, 1),
                  _const_spec((B_W + MEM_W, D_MODEL)), _const_spec((1, D_MODEL))],
        out_specs=slab_spec,
        out_shape=jax.ShapeDtypeStruct(x.shape, F32),
    )(x, qt3, qm, k3, k3, vt3, vt3, bias, mkt, mvt, wout, gpost)


def _band_sample_kernel(x_ref, q_ref, qm_ref, kn_ref, vn_ref, ck_ref, cv_ref, bc_ref, bn_ref,
                        mk_ref, mv_ref, wout_ref, gpost_ref, o_ref, *, seqs, seq_len):
    lane = lax.broadcasted_iota(jnp.int32, (1, LANES), 1)
    lo = lane < HEAD_DIM

    def pair_attend(b, p, out):
        rows = slice(b * seq_len, (b + 1) * seq_len)
        hd = slice(p * LANES, (p + 1) * LANES)
        qp = q_ref[p, rows, :]
        zero = jnp.zeros_like(qp)
        qs = jnp.concatenate([jnp.where(lo, qp, zero), jnp.where(lo, zero, qp)], axis=0)
        sc = _dot(qs, ck_ref[b, hd, :].astype(BF16)) + bc_ref[p]
        sn = _dot_nt(qs, kn_ref[p, rows, :]) + bn_ref[p]
        yield
        m = jnp.maximum(_lmax(sc), _lmax(sn))
        ec = jnp.exp(sc - m)
        en = jnp.exp(sn - m)
        l = _lsum(ec) + _lsum(en)
        yield
        o = (_dot_nt(ec.astype(BF16), cv_ref[b, hd, :].astype(BF16))
             + _dot(en.astype(BF16), vn_ref[p, rows, :]))
        yield
        o = o * (1.0 / l)
        out[p] = jnp.where(lo, o[:seq_len], o[seq_len:]).astype(BF16)

    def seq_attend(b, out):
        band = [None] * HEAD_PAIRS
        yield from _round_robin([pair_attend(b, p, band) for p in range(HEAD_PAIRS)])
        rows = slice(b * seq_len, (b + 1) * seq_len)
        mo = _mem_attend(qm_ref[rows, :], mk_ref[b].astype(BF16), mv_ref[b].astype(BF16))
        out[b] = jnp.concatenate(band + [mo.astype(BF16)], axis=-1)

    rows_out = [None] * seqs
    _interleave([seq_attend(b, rows_out) for b in range(seqs)])
    cat = jnp.concatenate(rows_out, axis=0)
    o_ref[...] = x_ref[...] + _rms(_dot(cat, wout_ref[...]), gpost_ref[...])


def _band_sample(x, q3, qm, kn3, vn3, ckt, cvt, bias_c, bias_n, mkt, mvt, wout, gpost, *, into, seqs=4):
    n = qm.shape[0]
    n_seq, past = ckt.shape[0], ckt.shape[2]
    seq_len = n // n_seq
    rows = seqs * seq_len
    slab_spec = pl.BlockSpec((rows, D_MODEL), lambda i: (i, 0))
    pair_rows_spec = pl.BlockSpec((HEAD_PAIRS, rows, LANES), lambda i: (0, i, 0))
    cache_spec = pl.BlockSpec((seqs, B_W, past), lambda i: (i, 0, 0))
    return _call(
        functools.partial(_band_sample_kernel, seqs=seqs, seq_len=seq_len), "band_sample", into=into,
        grid=(n_seq // seqs,),
        in_specs=[slab_spec,
                  pair_rows_spec,
                  pl.BlockSpec((rows, MEM_W), lambda i: (i, 0)),
                  pair_rows_spec, pair_rows_spec,
                  cache_spec, cache_spec,
                  _const_spec((HEAD_PAIRS, 2 * seq_len, past)),
                  _const_spec((HEAD_PAIRS, 2 * seq_len, seq_len)),
                  _mem_spec(mkt, seqs, 1), _mem_spec(mvt, seqs, 1),
                  _const_spec((B_W + MEM_W, D_MODEL)), _const_spec((1, D_MODEL))],
        out_specs=slab_spec,
        out_shape=jax.ShapeDtypeStruct(x.shape, F32),
    )(x, q3, qm, kn3, vn3, ckt, cvt, bias_c, bias_n, mkt, mvt, wout, gpost)


def _spatial_tile(w_s, b_s, period):
    tril = jnp.tril(jnp.ones((GM_CHUNK, GM_CHUNK), dtype=bool))
    w = jnp.where(tril, w_s, jnp.zeros((), w_s.dtype))[:, :period, :period]
    eye = jnp.eye(TILE // period, dtype=w.dtype)
    s_mat = jnp.einsum("ab,gts->gatbs", eye, w).reshape(GM_GROUPS, TILE, TILE)
    rows = jnp.tile(b_s[:, :period], (1, TILE // period))
    bs = jnp.repeat(rows.T, GM_GW, axis=1)
    return s_mat.astype(BF16), bs


def _heads_last(t, lead):
    pos = t.shape[-1]
    t = t.reshape(lead + (-1, HEAD_DIM, pos))
    nd = len(lead)
    return jnp.transpose(t, tuple(range(nd)) + (nd + 2, nd, nd + 1))


def _positions_last(c):
    nd = c.ndim
    t = jnp.transpose(c, tuple(range(nd - 3)) + (nd - 2, nd - 1, nd - 3))
    return t.reshape(c.shape[:-3] + (c.shape[-2] * c.shape[-1], c.shape[-3]))


def kernel(x_prompt, x_sample, cache_mem_k, cache_mem_v, cache_band_k, cache_band_v, mem_prompt,
           g_mix_pre, g_mix_post, g_ffn_pre, g_ffn_post, g_mem, w_mem_kv,
           w_in_a, g_gm_ln, b_gm_ln, w_spatial, b_spatial, w_out_a,
           g_kv, w_kv, w_in_b, rel_bias, w_out_b, w_ff1, w_ff2):
    seq = x_prompt.shape[1]
    n_seq, seq_len = x_sample.shape[0], x_sample.shape[1]
    past = cache_band_k.shape[1]
    vec = lambda a: a.reshape(1, -1)
    stack = lambda a: a.reshape(DEPTH, 1, -1)

    win_a = w_in_a[0].astype(BF16)
    wout_a = w_out_a[0].astype(BF16)
    ln_g, ln_b = vec(g_gm_ln[0]), vec(b_gm_ln[0])
    gkv, gpre_b = vec(g_kv), vec(g_mix_pre[1])
    gf_pre, gf_post = stack(g_ffn_pre), stack(g_ffn_post)

    n_sample = n_seq * seq_len
    slab_rows = seq + n_sample
    pre_a, post_a = vec(g_mix_pre[0]), vec(g_mix_post[0])

    mem_kt, mem_vt = _memkv(mem_prompt[0], g_mem, w_mem_kv)
    s_p, bs_p = _spatial_tile(w_spatial[0], b_spatial[0], GM_CHUNK)
    s_s, bs_s = _spatial_tile(w_spatial[0], b_spatial[0], seq_len)
    cmkt, cmvt = _positions_last(cache_mem_k), _positions_last(cache_mem_v)
    x, w1, w2, win_b, wout_b, wkv = _mixer_a(
        x_prompt[0], pre_a, post_a, win_a, ln_g, ln_b, s_p, bs_p, mem_kt, mem_vt, wout_a,
        rows_per_mem=TILE, emit_v=False, tiles=4, slab_rows=slab_rows, slab_offset=n_sample,
        cast=((w_ff1, 0), (w_ff2, 0), (w_in_b, 0), (w_out_b, 0), (w_kv[None], 0)))
    x, v_rows = _mixer_a(x_sample.reshape(n_sample, D_MODEL), pre_a, post_a, win_a, ln_g, ln_b, s_s, bs_s,
                         cmkt, cmvt, wout_a, rows_per_mem=seq_len, emit_v=True,
                         slab_rows=slab_rows, into=x)
    x = _ffn(x, gf_pre, gf_post, w1, w2, 0)

    qt3, qm, k3, vt3, w1, w2 = _proj_b(x, gkv, gpre_b, (win_b, wkv), mode="prompt", rows=2 * PROJ_ROWS,
                                       row_offset=n_sample, n_rows=seq, cast=((w_ff1, 1), (w_ff2, 1)))
    n_keep = min(BAND_PAST, seq)
    q3s, qms, kn3, vn3, k_new, v_new, kt_tail, vt_tail = _proj_b(
        x, gkv, gpre_b, (win_b, wkv), mode="sample", rows=512, n_rows=n_sample, seq_len=seq_len,
        tail_offset=n_sample + seq - n_keep, tail_rows=n_keep)
    bias_p, bias_c, bias_n = _rel_bias_tables(rel_bias[0], (
        (TILE, BAND_PAST + TILE, (), True, True),
        (seq_len, past + seq_len, ((0, past), (past, past + seq_len)), False, False)))
    post_b = vec(g_mix_post[1])
    y = _band_prompt(x, qt3, qm, k3, vt3, bias_p, mem_kt, mem_vt, wout_b, post_b)
    y = _band_sample(x, q3s, qms, kn3, vn3, _positions_last(cache_band_k), _positions_last(cache_band_v),
                     bias_c, bias_n, cmkt, cmvt, wout_b, post_b, into=y)
    y_sample, y_prompt = _ffn(y, gf_pre, gf_post, w1, w2, 1, split=n_sample)
    y_prompt = y_prompt[None]
    y_sample = y_sample.reshape(n_seq, seq_len, D_MODEL)

    return (y_prompt, y_sample,
            _heads_last(mem_kt, (DEPTH, 1)), _heads_last(mem_vt, (DEPTH, 1)),
            _heads_last(kt_tail, (1,)), _heads_last(vt_tail, (1,)),
            v_rows.reshape(1, n_seq, seq_len, GM_W),
            jnp.swapaxes(k_new, 1, 2), jnp.swapaxes(v_new, 1, 2))
```
